```python
import math
import jax
import jax.numpy as jnp
from jax import lax
import numpy as np

D_MODEL = 4096
BATCH = 4
SEQ = 2048
DEPTH = 2
DEC_BATCH = 8
DEC_SEQ = 4
PAST_LEN = 16384
PAGE_SIZE = 128

N_EVEN = (DEPTH + 1) // 2
N_ODD = DEPTH // 2
N_MEM = 256
EPS = 1e-6
CHUNK = 64

HG_DK = 128
HG_DV = 128
HG_HEADS = D_MODEL // 2 // HG_DV
HG_W = HG_HEADS * HG_DV

NSA_HD = 128
NSA_HEADS = D_MODEL // 2 // NSA_HD
NSA_KVH = 4
NSA_G = NSA_HEADS // NSA_KVH
NSA_W = NSA_HEADS * NSA_HD
NSA_KV_W = NSA_KVH * NSA_HD
CMP_BLOCK = 32
CMP_STRIDE = 16
SEL_BLOCK = 64
N_SEL = 16
WINDOW = 512
Q_BLOCK = 128
SEL_Q_BLOCK = 32

ML_HEADS = D_MODEL // 512
ML_DK = D_MODEL // 2 // ML_HEADS
ML_DV = D_MODEL // ML_HEADS
ML_QK_W = ML_HEADS * ML_DK
ML_V_W = ML_HEADS * ML_DV

MEM_HEADS = 4
MEM_HD = 128
MEM_W = MEM_HEADS * MEM_HD

REL_BUCKETS = 32
REL_MAX_DIST = 128

EVEN_SPLITS = (HG_W, HG_W, HG_W, HG_W, NSA_W, 6 * NSA_KV_W, 3 * NSA_HEADS, NSA_W, MEM_W)
ODD_SPLITS = (ML_QK_W, ML_QK_W, ML_V_W, ML_V_W, ML_HEADS, ML_HEADS, ML_V_W, MEM_W)
F32 = jnp.float32

kernel_name = 'hybrid_hgrn2_nsa_mlstm_step'


def _rmsnorm(x, w):
    xf = x.astype(F32)
    y = xf * lax.rsqrt(jnp.mean(xf * xf, axis=-1, keepdims=True) + EPS)
    return (y * w.astype(F32)).astype(x.dtype)


def _split(a, sizes):
    offs = np.cumsum(sizes)[:-1].tolist()
    return jnp.split(a, offs, axis=-1)


def _masked_softmax(s, mask):
    s = jnp.where(mask, s, -jnp.inf)
    m = jnp.max(s, axis=-1, keepdims=True)
    m = jnp.where(jnp.isfinite(m), m, 0.0)
    p = jnp.exp(s - m)
    return p / jnp.maximum(p.sum(axis=-1, keepdims=True), jnp.finfo(F32).tiny)


def _rel_bucket(dist):
    n = jnp.maximum(dist, 0)
    exact = REL_BUCKETS // 2
    nf = jnp.maximum(n, 1).astype(F32)
    large = exact + (jnp.log(nf / exact) / math.log(REL_MAX_DIST / exact) * (REL_BUCKETS - exact)).astype(jnp.int32)
    return jnp.where(n < exact, n, jnp.minimum(large, REL_BUCKETS - 1))


def _rel_bias(rel_bias, dist):
    b = rel_bias[_rel_bucket(dist)]
    b = b.reshape(b.shape[:-1] + (NSA_KVH, NSA_G))
    return jnp.moveaxis(b, (-2, -1), (-4, -3)).astype(F32)


def _chunks(a, L):
    B, T = a.shape[:2]
    return a.reshape((B, T // L, L) + a.shape[2:]).swapaxes(0, 1)


def _unchunk(a):
    n, B, L = a.shape[:3]
    return a.swapaxes(0, 1).reshape((B, n * L) + a.shape[3:])


def _chunk_len(T):
    return CHUNK if T % CHUNK == 0 else T


def _hgrn_inputs(qa, fa, ia, lb):
    B, T = qa.shape[:2]
    shp = (B, T, HG_HEADS, HG_DK)
    q = jax.nn.silu(qa.astype(F32)).reshape(shp)
    lb = lb.reshape(HG_HEADS, HG_DK)
    logf = jnp.logaddexp(jnp.log(lb), jnp.log1p(-lb) + jax.nn.log_sigmoid(fa.astype(F32).reshape(shp)))
    k = -jnp.expm1(logf)
    v = ia.astype(F32).reshape(B, T, HG_HEADS, HG_DV)
    return q, k, v, logf


def _hgrn2_scan(q, k, v, logf, S0):
    L = _chunk_len(q.shape[1])
    causal = jnp.tril(jnp.ones((L, L), bool))

    def step(S, inp):
        qc, kc, vc, gc = inp
        Bc = jnp.cumsum(gc, axis=1)
        inter = jnp.einsum('blhk,bhkv->blhv', qc * jnp.exp(Bc), S)
        diff = Bc[:, :, None] - Bc[:, None]
        decay = jnp.exp(jnp.where(causal[None, :, :, None, None], diff, -jnp.inf))
        att = jnp.einsum('bthk,bshk,btshk->bhts', qc, kc, decay)
        intra = jnp.einsum('bhts,bshv->bthv', att, vc)
        Bl = Bc[:, -1]
        S = jnp.exp(Bl)[..., None] * S + jnp.einsum('bshk,bshv->bhkv', kc * jnp.exp(Bl[:, None] - Bc), vc)
        return S, inter + intra

    S, o = lax.scan(step, S0.astype(F32), (_chunks(q, L), _chunks(k, L), _chunks(v, L), _chunks(logf, L)))
    return _unchunk(o), S


def _mlstm_scan(q, k, v, log_i, log_f, C0, n0, m0):
    L = _chunk_len(q.shape[1])
    causal = jnp.tril(jnp.ones((L, L), bool))

    def step(carry, inp):
        C, n, m = carry
        qc, kc, vc, ic, fc = inp
        b = jnp.cumsum(fc, axis=1)
        dmat = jnp.where(causal[None, :, :, None], b[:, :, None] - b[:, None] + ic[:, None], -jnp.inf)
        inter = b + m[:, None]
        mt = jnp.maximum(inter, dmat.max(axis=2))
        w_in = jnp.exp(dmat - mt[:, :, None])
        w_x = jnp.exp(inter - mt)
        sw = jnp.einsum('bthk,bshk->btsh', qc, kc) * w_in
        num = w_x[..., None] * jnp.einsum('bthk,bhvk->bthv', qc, C) + jnp.einsum('btsh,bshv->bthv', sw, vc)
        den = w_x * jnp.einsum('bthk,bhk->bth', qc, n) + sw.sum(axis=2)
        h = num / jnp.maximum(jnp.abs(den), jnp.exp(-mt))[..., None]
        mL = mt[:, -1]
        w_end = jnp.exp(b[:, -1:] - b + ic - mL[:, None])
        dC = jnp.exp(b[:, -1] + m - mL)
        C = dC[..., None, None] * C + jnp.einsum('bsh,bshv,bshk->bhvk', w_end, vc, kc)
        n = dC[..., None] * n + jnp.einsum('bsh,bshk->bhk', w_end, kc)
        return (C, n, mL), h

    init = (C0.astype(F32), n0.astype(F32), m0.astype(F32))
    xs = (_chunks(q, L), _chunks(k, L), _chunks(v, L), _chunks(log_i, L), _chunks(log_f, L))
    (C, n, m), h = lax.scan(step, init, xs)
    return _unchunk(h), C, n, m


def _compress(rows, w1, b1, w2, pe):
    B, T = rows.shape[:2]
    r = CMP_BLOCK // CMP_STRIDE
    nch = T // CMP_STRIDE
    n_cmp = nch - r + 1
    rc = rows[:, :nch * CMP_STRIDE].reshape(B, nch, CMP_STRIDE, NSA_KVH, NSA_HD)
    pe_c = pe.reshape(r, CMP_STRIDE, NSA_HD)
    w1_c = w1.reshape(r, CMP_STRIDE, NSA_HD, NSA_HD)
    h = b1
    for j in range(r):
        h = h + jnp.einsum('bnskd,sde->bnke', rc[:, j:j + n_cmp] + pe_c[j][:, None, :], w1_c[j])
    return jnp.einsum('bnke,ed->bnkd', jax.nn.gelu(h), w2)


def _cmp_attn(q, qpos, kc, vc, rel_bias):
    kend = jnp.arange(kc.shape[1]) * CMP_STRIDE + CMP_BLOCK - 1
    dist = qpos[:, None] - kend[None, :]
    s = jnp.einsum('bqkgd,bnkd->bkgqn', q, kc).astype(F32) * NSA_HD ** -0.5 + _rel_bias(rel_bias, dist)
    p = _masked_softmax(s, dist >= 0)
    return jnp.einsum('bkgqn,bnkd->bqkgd', p.astype(vc.dtype), vc), p


def _cmp_to_slc(p, n_slc):
    r = SEL_BLOCK // CMP_STRIDE
    c = CMP_BLOCK // CMP_STRIDE
    front = c - 1
    back = max(r * n_slc + r - p.shape[-1], 0)
    pp = jnp.pad(p, [(0, 0)] * (p.ndim - 1) + [(front, back)])
    terms = [lax.slice_in_dim(pp, front + m - n, front + m - n + r * (n_slc - 1) + 1, stride=r, axis=p.ndim - 1)
             for m in range(r) for n in range(c)]
    return sum(terms[1:], terms[0])


def _select(p, qpos, n_slc):
    ps = _cmp_to_slc(p.sum(axis=2), n_slc)
    blk = jnp.arange(n_slc)
    cur = (qpos // SEL_BLOCK)[:, None]
    forced = (blk == 0) | (blk == cur) | (blk == cur - 1)
    score = jnp.where(forced, jnp.inf, ps)
    score = jnp.where(blk > cur, -jnp.inf, score)
    _, idx = lax.top_k(score, min(N_SEL, n_slc))
    idx = jnp.moveaxis(idx, 1, 2)
    valid = idx <= (qpos // SEL_BLOCK)[None, :, None, None]
    return idx, valid


def _sel_attn(q, qpos, idx, valid, ks, vs, rel_bias):
    B, Tq = q.shape[:2]
    nk = idx.shape[-1]
    kpos = idx[..., None] * SEL_BLOCK + jnp.arange(SEL_BLOCK)
    dist = qpos[None, :, None, None, None] - kpos
    mask = (valid[..., None] & (dist >= 0)).reshape(B, Tq, NSA_KVH, 1, nk * SEL_BLOCK)
    tbl = rel_bias.reshape(REL_BUCKETS, NSA_KVH, NSA_G)
    bias = tbl[_rel_bucket(dist), jnp.arange(NSA_KVH)[:, None, None]]
    s = jnp.einsum('bqkgd,bqknsd->bqkgns', q, ks).astype(F32) * NSA_HD ** -0.5 + jnp.moveaxis(bias, -1, 3).astype(F32)
    p = _masked_softmax(s.reshape(B, Tq, NSA_KVH, NSA_G, nk * SEL_BLOCK), mask)
    return jnp.einsum('bqkgm,bqkmd->bqkgd', p.astype(vs.dtype), vs.reshape(B, Tq, NSA_KVH, nk * SEL_BLOCK, NSA_HD))


def _sel_prompt(q, qpos, idx, valid, k, v, rel_bias):
    B, T = q.shape[:2]
    nb = T // SEL_Q_BLOCK

    def blocks(a):
        return a.reshape(B, T // SEL_BLOCK, SEL_BLOCK, NSA_KVH, NSA_HD).transpose(0, 3, 1, 2, 4)

    kt, vt = blocks(k), blocks(v)
    bi = jnp.arange(B)[:, None, None, None]
    hi = jnp.arange(NSA_KVH)[None, None, :, None]

    def qsplit(a):
        return a.reshape((B, nb, SEL_Q_BLOCK) + a.shape[2:]).swapaxes(0, 1)

    def step(args):
        qb, pb, ib, vb = args
        return _sel_attn(qb, pb, ib, vb, kt[bi, hi, ib], vt[bi, hi, ib], rel_bias)

    out = lax.map(step, (qsplit(q), qpos.reshape(nb, SEL_Q_BLOCK), qsplit(idx), qsplit(valid)))
    return out.swapaxes(0, 1).reshape(q.shape)


def _gather_pages(pool, page_table):
    g = pool[page_table]
    return g.reshape((g.shape[0], -1) + g.shape[3:])


def _gather_sel_sample(pool, new, page_table, idx):
    B = idx.shape[0]
    n_pages = PAST_LEN // PAGE_SIZE
    pos = idx[..., None] * SEL_BLOCK + jnp.arange(SEL_BLOCK)
    bi = jnp.arange(B)[:, None, None, None, None]
    hi = jnp.arange(NSA_KVH)[None, None, :, None, None]
    page = page_table[bi, jnp.clip(pos // PAGE_SIZE, 0, n_pages - 1)]
    past_rows = pool[page, pos % PAGE_SIZE, hi]
    new_rows = new[bi, jnp.clip(pos - PAST_LEN, 0, new.shape[1] - 1), hi]
    return jnp.where((pos < PAST_LEN)[..., None], past_rows, new_rows)


def _local_attn(q, k, v, qpos, kpos, rel_bias):
    dist = qpos[..., :, None] - kpos[..., None, :]
    mask = (dist >= 0) & (dist < WINDOW) & (kpos[..., None, :] >= 0)
    s = jnp.einsum('b...qkgd,b...skd->b...kgqs', q, k).astype(F32) * NSA_HD ** -0.5 + _rel_bias(rel_bias, dist)
    p = _masked_softmax(s, mask[..., None, None, :, :])
    return jnp.einsum('b...kgqs,b...skd->b...qkgd', p.astype(v.dtype), v)


def _window_prompt(q, k, v, rel_bias):
    B, T = q.shape[:2]
    nq = T // Q_BLOCK
    nk = WINDOW // Q_BLOCK + 1
    pad = [(0, 0), (WINDOW, 0), (0, 0), (0, 0)]

    def band(a):
        ab = jnp.pad(a, pad).reshape(B, nq + nk - 1, Q_BLOCK, NSA_KVH, NSA_HD)
        return jnp.stack([ab[:, j:j + nq] for j in range(nk)], axis=2).reshape(B, nq, nk * Q_BLOCK, NSA_KVH, NSA_HD)

    qpos = jnp.arange(T).reshape(nq, Q_BLOCK)
    kpos = (jnp.arange(nq) * Q_BLOCK - WINDOW)[:, None] + jnp.arange(nk * Q_BLOCK)[None, :]
    qb = q.reshape(B, nq, Q_BLOCK, NSA_KVH, NSA_G, NSA_HD)
    return _local_attn(qb, band(k), band(v), qpos, kpos, rel_bias).reshape(q.shape)


def _nsa_heads(qb, kvb):
    B, T = qb.shape[:2]
    q = qb.reshape(B, T, NSA_KVH, NSA_G, NSA_HD)
    kv = kvb.reshape(B, T, 6, NSA_KVH, NSA_HD)
    return q, [kv[:, :, j] for j in range(6)]


def _mem_attn(qm, mk, mv):
    B, T = qm.shape[:2]
    q = qm.reshape(B, T, MEM_HEADS, MEM_HD)
    s = jnp.einsum('bqhd,bmhd->bhqm', q, mk).astype(F32) * MEM_HD ** -0.5
    p = jax.nn.softmax(s, axis=-1)
    return jnp.einsum('bhqm,bmhd->bqhd', p.astype(mv.dtype), mv).reshape(B, T, MEM_W)


def _even_out(hn, oa, za, o3, gb, b_gate, zb, qm, mk, mv, g_norm, w_out):
    B, T = hn.shape[:2]
    oa = _rmsnorm(oa, g_norm).reshape(B, T, HG_W) * jax.nn.silu(za)
    g = jax.nn.sigmoid((gb + b_gate).astype(F32)).reshape(B, T, 3, NSA_KVH, NSA_G, 1)
    ob = g[:, :, 0] * o3[0] + g[:, :, 1] * o3[1] + g[:, :, 2] * o3[2]
    ob = ob.reshape(B, T, NSA_W) * jax.nn.silu(zb)
    om = _mem_attn(qm, mk, mv)
    y = jnp.einsum('bte,ed->btd', jnp.concatenate([oa, ob, om], axis=-1), w_out)
    return y.astype(hn.dtype)


def _even_prompt(hn, mk, mv, w_in, b_gate, w1, b1, w2, pe, lb, g_norm, w_out, rel_bias):
    B, T = hn.shape[:2]
    qa, fa, ia, za, qb, kvb, gb, zb, qm = _split(jnp.einsum('btd,de->bte', hn, w_in), EVEN_SPLITS)
    hq, hk, hv, hf = _hgrn_inputs(qa, fa, ia, lb)
    oa, S = _hgrn2_scan(hq, hk, hv, hf, jnp.zeros((B, HG_HEADS, HG_DK, HG_DV), F32))
    q, (kc, vc, ks, vs, kw, vw) = _nsa_heads(qb, kvb)
    qpos = jnp.arange(T)
    kcmp = _compress(kc, w1[0], b1[0], w2[0], pe[0])
    vcmp = _compress(vc, w1[1], b1[1], w2[1], pe[1])
    o_cmp, p = _cmp_attn(q, qpos, kcmp, vcmp, rel_bias)
    idx, valid = _select(p, qpos, T // SEL_BLOCK)
    o_sel = _sel_prompt(q, qpos, idx, valid, ks, vs, rel_bias)
    o_win = _window_prompt(q, kw, vw, rel_bias)
    y = _even_out(hn, oa, za, (o_cmp, o_sel, o_win), gb, b_gate, zb, qm, mk, mv, g_norm, w_out)
    wb = min(WINDOW, T)
    return y, (kc, vc, ks, vs, kw[:, -wb:], vw[:, -wb:], S)


def _even_sample(hn, mk, mv, page_table, pk_cmp, pv_cmp, pk_sel, pv_sel, wk, wv, S0,
                 w_in, b_gate, w1, b1, w2, pe, lb, g_norm, w_out, rel_bias):
    B, T = hn.shape[:2]
    qa, fa, ia, za, qb, kvb, gb, zb, qm = _split(jnp.einsum('btd,de->bte', hn, w_in), EVEN_SPLITS)
    hq, hk, hv, hf = _hgrn_inputs(qa, fa, ia, lb)
    oa, S = _hgrn2_scan(hq, hk, hv, hf, S0)
    q, (kc, vc, ks, vs, kw, vw) = _nsa_heads(qb, kvb)
    qpos = PAST_LEN + jnp.arange(T)
    kcmp = _compress(jnp.concatenate([_gather_pages(pk_cmp, page_table), kc], axis=1), w1[0], b1[0], w2[0], pe[0])
    vcmp = _compress(jnp.concatenate([_gather_pages(pv_cmp, page_table), vc], axis=1), w1[1], b1[1], w2[1], pe[1])
    o_cmp, p = _cmp_attn(q, qpos, kcmp, vcmp, rel_bias)
    idx, valid = _select(p, qpos, -(-(PAST_LEN + T) // SEL_BLOCK))
    o_sel = _sel_attn(q, qpos, idx, valid, _gather_sel_sample(pk_sel, ks, page_table, idx),
                      _gather_sel_sample(pv_sel, vs, page_table, idx), rel_bias)
    wb = wk.shape[1]
    kk = jnp.concatenate([wk, kw], axis=1)
    vv = jnp.concatenate([wv, vw], axis=1)
    o_win = _local_attn(q, kk, vv, qpos, PAST_LEN - wb + jnp.arange(wb + T), rel_bias)
    y = _even_out(hn, oa, za, (o_cmp, o_sel, o_win), gb, b_gate, zb, qm, mk, mv, g_norm, w_out)
    return y, (kc, vc, ks, vs, kk[:, -wb:], vv[:, -wb:], S)


def _odd_mix(hn, mk, mv, C0, n0, m0, w_in, b_if, g_norm, w_out):
    B, T = hn.shape[:2]
    q, k, v, og, ig, fg, z, qm = _split(jnp.einsum('btd,de->bte', hn, w_in), ODD_SPLITS)
    q = q.astype(F32).reshape(B, T, ML_HEADS, ML_DK)
    k = k.astype(F32).reshape(B, T, ML_HEADS, ML_DK) * ML_DK ** -0.5
    v = v.astype(F32).reshape(B, T, ML_HEADS, ML_DV)
    log_i = ig.astype(F32) + b_if[0]
    log_f = jax.nn.log_sigmoid(fg.astype(F32) + b_if[1])
    h, C, n, m = _mlstm_scan(q, k, v, log_i, log_f, C0, n0, m0)
    h = _rmsnorm(h, g_norm) * jax.nn.sigmoid(og.astype(F32)).reshape(B, T, ML_HEADS, ML_DV)
    h = h.reshape(B, T, ML_V_W) * jax.nn.silu(z)
    om = _mem_attn(qm, mk, mv)
    y = jnp.einsum('bte,ed->btd', jnp.concatenate([h, om], axis=-1), w_out)
    return y.astype(hn.dtype), (C, n, m)


def _stack(lst, i):
    return jnp.stack([t[i] for t in lst])


def setup_inputs(seed: int = 0) -> dict:
    key = jax.random.key(seed)
    ks = iter(jax.random.split(key, 48))

    def nrm(shape, scale=1.0):
        return scale * jax.random.normal(next(ks), shape, F32)

    n_pages = PAST_LEN // PAGE_SIZE
    n_used = DEC_BATCH * n_pages
    n_pool = n_used + (n_used + 3) // 4
    win_buf = min(WINDOW, PAST_LEN)
    even_in = sum(EVEN_SPLITS)
    odd_in = sum(ODD_SPLITS)
    even_out = HG_W + NSA_W + MEM_W
    odd_out = ML_V_W + MEM_W
    page_table = jax.random.permutation(next(ks), n_pool)[:n_used].reshape(DEC_BATCH, n_pages).astype(jnp.int32)
    forget_bias = jnp.linspace(3.0, 6.0, ML_HEADS, dtype=F32)
    return {
        'x_prompt': nrm((BATCH, SEQ, D_MODEL)),
        'x_sample': nrm((DEC_BATCH, DEC_SEQ, D_MODEL)),
        'cache_mem_k': nrm((DEPTH, DEC_BATCH, N_MEM, MEM_HEADS, MEM_HD)),
        'cache_mem_v': nrm((DEPTH, DEC_BATCH, N_MEM, MEM_HEADS, MEM_HD)),
        'cache_cmp_k': nrm((N_EVEN, n_pool, PAGE_SIZE, NSA_KVH, NSA_HD)),
        'cache_cmp_v': nrm((N_EVEN, n_pool, PAGE_SIZE, NSA_KVH, NSA_HD)),
        'cache_sel_k': nrm((N_EVEN, n_pool, PAGE_SIZE, NSA_KVH, NSA_HD)),
        'cache_sel_v': nrm((N_EVEN, n_pool, PAGE_SIZE, NSA_KVH, NSA_HD)),
        'cache_win_k': nrm((N_EVEN, DEC_BATCH, win_buf, NSA_KVH, NSA_HD)),
        'cache_win_v': nrm((N_EVEN, DEC_BATCH, win_buf, NSA_KVH, NSA_HD)),
        'state_hgrn': nrm((N_EVEN, DEC_BATCH, HG_HEADS, HG_DK, HG_DV), 0.5),
        'state_mlstm_c': nrm((N_ODD, DEC_BATCH, ML_HEADS, ML_DV, ML_DK), 0.3),
        'state_mlstm_n': nrm((N_ODD, DEC_BATCH, ML_HEADS, ML_DK), 0.3),
        'state_mlstm_m': nrm((N_ODD, DEC_BATCH, ML_HEADS)),
        'page_table': page_table,
        'mem_prompt': nrm((BATCH, N_MEM, D_MODEL)),
        'norm_w': 1.0 + nrm((DEPTH, D_MODEL), 0.02),
        'mem_norm_w': 1.0 + nrm((DEPTH, D_MODEL), 0.02),
        'final_norm_w': 1.0 + nrm((D_MODEL,), 0.02),
        'rel_bias': nrm((REL_BUCKETS, NSA_HEADS), 0.5),
        'w_mem_kv': nrm((DEPTH, D_MODEL, 2 * MEM_W), D_MODEL ** -0.5),
        'w_in_even': nrm((N_EVEN, D_MODEL, even_in), D_MODEL ** -0.5),
        'b_nsa_gate': nrm((N_EVEN, 3 * NSA_HEADS), 0.1),
        'w_cmp1': nrm((N_EVEN, 2, CMP_BLOCK, NSA_HD, NSA_HD), (CMP_BLOCK * NSA_HD) ** -0.5),
        'b_cmp1': nrm((N_EVEN, 2, NSA_HD), 0.02),
        'w_cmp2': nrm((N_EVEN, 2, NSA_HD, NSA_HD), NSA_HD ** -0.5),
        'pe_cmp': nrm((N_EVEN, 2, CMP_BLOCK, NSA_HD), 0.1),
        'hgrn_lb_logits': nrm((DEPTH + 1, HG_W), 0.5),
        'hgrn_norm_w': 1.0 + nrm((N_EVEN, HG_DV), 0.02),
        'w_out_even': nrm((N_EVEN, even_out, D_MODEL), even_out ** -0.5),
        'w_in_odd': nrm((N_ODD, D_MODEL, odd_in), D_MODEL ** -0.5),
        'b_mlstm_if': jnp.stack([nrm((N_ODD, ML_HEADS), 0.1), forget_bias + nrm((N_ODD, ML_HEADS), 0.1)], axis=1),
        'mlstm_norm_w': 1.0 + nrm((N_ODD, ML_HEADS, ML_DV), 0.02),
        'w_out_odd': nrm((N_ODD, odd_out, D_MODEL), odd_out ** -0.5),
    }


def reference(x_prompt, x_sample, cache_mem_k, cache_mem_v, cache_cmp_k, cache_cmp_v, cache_sel_k, cache_sel_v,
              cache_win_k, cache_win_v, state_hgrn, state_mlstm_c, state_mlstm_n, state_mlstm_m, page_table,
              mem_prompt, norm_w, mem_norm_w, final_norm_w, rel_bias, w_mem_kv, w_in_even, b_nsa_gate,
              w_cmp1, b_cmp1, w_cmp2, pe_cmp, hgrn_lb_logits, hgrn_norm_w, w_out_even, w_in_odd, b_mlstm_if,
              mlstm_norm_w, w_out_odd):
    lbs = jnp.cumsum(jax.nn.softmax(hgrn_lb_logits.astype(F32), axis=0), axis=0)
    hp, hs = x_prompt, x_sample
    mem_new, even_p, even_s, odd_p, odd_s = [], [], [], [], []
    for l in range(DEPTH):
        npre = _rmsnorm(hp, norm_w[l])
        nsam = _rmsnorm(hs, norm_w[l])
        mkv = jnp.einsum('bmd,de->bme', _rmsnorm(mem_prompt, mem_norm_w[l]), w_mem_kv[l])
        mk_p, mv_p = [a.reshape(a.shape[0], a.shape[1], MEM_HEADS, MEM_HD) for a in _split(mkv, (MEM_W, MEM_W))]
        mem_new.append((mk_p, mv_p))
        mk_s, mv_s = cache_mem_k[l], cache_mem_v[l]
        if l % 2 == 0:
            e = l // 2
            wts = (w_in_even[e], b_nsa_gate[e], w_cmp1[e], b_cmp1[e], w_cmp2[e], pe_cmp[e], lbs[l],
                   hgrn_norm_w[e], w_out_even[e], rel_bias)
            yp, st_p = _even_prompt(npre, mk_p, mv_p, *wts)
            ys, st_s = _even_sample(nsam, mk_s, mv_s, page_table, cache_cmp_k[e], cache_cmp_v[e], cache_sel_k[e],
                                    cache_sel_v[e], cache_win_k[e], cache_win_v[e], state_hgrn[e], *wts)
            even_p.append(st_p)
            even_s.append(st_s)
        else:
            o = l // 2
            wts = (w_in_odd[o], b_mlstm_if[o], mlstm_norm_w[o], w_out_odd[o])
            bp = hp.shape[0]
            yp, st_p = _odd_mix(npre, mk_p, mv_p, jnp.zeros((bp, ML_HEADS, ML_DV, ML_DK), F32),
                                jnp.zeros((bp, ML_HEADS, ML_DK), F32), jnp.zeros((bp, ML_HEADS), F32), *wts)
            ys, st_s = _odd_mix(nsam, mk_s, mv_s, state_mlstm_c[o], state_mlstm_n[o], state_mlstm_m[o], *wts)
            odd_p.append(st_p)
            odd_s.append(st_s)
        hp = hp + yp
        hs = hs + ys
    y_prompt = _rmsnorm(hp, final_norm_w)
    y_sample = _rmsnorm(hs, final_norm_w)
    return (y_prompt, y_sample,
            _stack(mem_new, 0), _stack(mem_new, 1),
            _stack(even_p, 0), _stack(even_p, 1), _stack(even_p, 2), _stack(even_p, 3),
            _stack(even_p, 4), _stack(even_p, 5), _stack(even_p, 6),
            _stack(odd_p, 0), _stack(odd_p, 1), _stack(odd_p, 2),
            _stack(even_s, 0), _stack(even_s, 1), _stack(even_s, 2), _stack(even_s, 3),
            _stack(even_s, 4), _stack(even_s, 5), _stack(even_s, 6),
            _stack(odd_s, 0), _stack(odd_s, 1), _stack(odd_s, 2))
```

```python
import math

import jax
import jax.numpy as jnp
import numpy as np
from jax import lax
from jax.experimental import pallas as pl
from jax.experimental.pallas import tpu as pltpu

D_MODEL = 4096
DEPTH = 2
PAST_LEN = 16384
PAGE_SIZE = 128
N_MEM = 256
EPS = 1e-6
CHUNK = 64

HG_DK = 128
HG_DV = 128
HG_HEADS = D_MODEL // 2 // HG_DV
HG_W = HG_HEADS * HG_DV

NSA_HD = 128
NSA_HEADS = D_MODEL // 2 // NSA_HD
NSA_KVH = 4
NSA_G = NSA_HEADS // NSA_KVH
NSA_W = NSA_HEADS * NSA_HD
NSA_KV_W = NSA_KVH * NSA_HD
CMP_BLOCK = 32
CMP_STRIDE = 16
SEL_BLOCK = 64
N_SEL = 16
WINDOW = 512
Q_BLOCK = 128
SEL_Q_BLOCK = 32

ML_HEADS = D_MODEL // 512
ML_DK = D_MODEL // 2 // ML_HEADS
ML_DV = D_MODEL // ML_HEADS
ML_QK_W = ML_HEADS * ML_DK
ML_V_W = ML_HEADS * ML_DV

MEM_HEADS = 4
MEM_HD = 128
MEM_W = MEM_HEADS * MEM_HD

REL_BUCKETS = 32
REL_MAX_DIST = 128

F32 = jnp.float32
BF16 = jnp.bfloat16

EVEN_SEGS = (("qa", HG_W), ("fa", HG_W), ("ia", HG_W), ("za", HG_W), ("qb", NSA_W), ("kvb", 6 * NSA_KV_W),
             ("gb", 3 * NSA_HEADS), ("zb", NSA_W), ("qm", MEM_W))
EVEN_ORDER = ("qa", "fa", "ia", "za", "qb", "kvb", "zb", "qm", "gb")
ODD_SEGS = (("q", ML_QK_W), ("k", ML_QK_W), ("v", ML_V_W), ("og", ML_V_W), ("ig", ML_HEADS), ("fg", ML_HEADS),
            ("z", ML_V_W), ("qm", MEM_W))
ODD_ORDER = ("q", "k", "v", "og", "z", "qm", "ig", "fg")
PROJ_TILE_N = 1024


def _layout(segs, order):
    src, off = {}, 0
    for name, w in segs:
        src[name] = (off, w)
        off += w
    dst, off = {}, 0
    for name in order:
        dst[name] = (off, src[name][1])
        off += src[name][1]
    total = -(-off // PROJ_TILE_N) * PROJ_TILE_N
    return src, dst, total


def _relayout_w_in(w, segs, order):
    src, dst, total = _layout(segs, order)
    cols = [lax.slice_in_dim(w, src[n][0], src[n][0] + src[n][1], axis=1) for n in order]
    used = sum(src[n][1] for n in order)
    cols.append(jnp.zeros((w.shape[0], total - used), w.dtype))
    return jnp.concatenate(cols, axis=1).astype(BF16), dst


def _rmsnorm_body(x_ref, w_ref, o_ref):
    x = x_ref[...].astype(F32)
    y = x * lax.rsqrt(jnp.mean(x * x, axis=-1, keepdims=True) + EPS)
    o_ref[...] = (y * w_ref[...].astype(F32)).astype(o_ref.dtype)


def _rmsnorm_rows(x2d, w, out_dtype, tm=256):
    m, d = x2d.shape
    tm = min(tm, m)
    return pl.pallas_call(
        _rmsnorm_body,
        grid=(m // tm,),
        in_specs=[pl.BlockSpec((tm, d), lambda i: (i, 0)), pl.BlockSpec((1, d), lambda i: (0, 0))],
        out_specs=pl.BlockSpec((tm, d), lambda i: (i, 0)),
        out_shape=jax.ShapeDtypeStruct((m, d), out_dtype),
        name="rmsnorm",
    )(x2d, w.reshape(1, d))


def _matmul_body(a_ref, b_ref, o_ref, acc_ref):
    @pl.when(pl.program_id(2) == 0)
    def _():
        acc_ref[...] = jnp.zeros_like(acc_ref)

    acc_ref[...] += jnp.dot(a_ref[...], b_ref[...], preferred_element_type=F32)

    @pl.when(pl.program_id(2) == pl.num_programs(2) - 1)
    def _():
        o_ref[...] = acc_ref[...]


def _matmul(a, b, tm=1024, tn=1024, tk=512):
    m, k = a.shape
    _, n = b.shape
    tm, tn, tk = min(tm, m), min(tn, n), min(tk, k)
    assert m % tm == 0 and n % tn == 0 and k % tk == 0, (a.shape, b.shape)
    return pl.pallas_call(
        _matmul_body,
        grid=(m // tm, n // tn, k // tk),
        in_specs=[pl.BlockSpec((tm, tk), lambda i, j, l: (i, l)), pl.BlockSpec((tk, tn), lambda i, j, l: (l, j))],
        out_specs=pl.BlockSpec((tm, tn), lambda i, j, l: (i, j)),
        out_shape=jax.ShapeDtypeStruct((m, n), F32),
        scratch_shapes=[pltpu.VMEM((tm, tn), F32)],
        compiler_params=pltpu.CompilerParams(dimension_semantics=("parallel", "parallel", "arbitrary"),
                                             vmem_limit_bytes=48 * 1024 * 1024),
        name="matmul",
    )(a, b)


def _rmsnorm(x, w):
    xf = x.astype(F32)
    y = xf * lax.rsqrt(jnp.mean(xf * xf, axis=-1, keepdims=True) + EPS)
    return (y * w.astype(F32)).astype(x.dtype)


def _masked_softmax(s, mask):
    s = jnp.where(mask, s, -jnp.inf)
    m = jnp.max(s, axis=-1, keepdims=True)
    m = jnp.where(jnp.isfinite(m), m, 0.0)
    p = jnp.exp(s - m)
    return p / jnp.maximum(p.sum(axis=-1, keepdims=True), jnp.finfo(F32).tiny)


def _rel_bucket(dist):
    n = jnp.maximum(dist, 0)
    exact = REL_BUCKETS // 2
    nf = jnp.maximum(n, 1).astype(F32)
    large = exact + (jnp.log(nf / exact) / math.log(REL_MAX_DIST / exact) * (REL_BUCKETS - exact)).astype(jnp.int32)
    return jnp.where(n < exact, n, jnp.minimum(large, REL_BUCKETS - 1))


def _rel_bias(rel_bias, dist):
    b = rel_bias[_rel_bucket(dist)]
    b = b.reshape(b.shape[:-1] + (NSA_KVH, NSA_G))
    return jnp.moveaxis(b, (-2, -1), (-4, -3)).astype(F32)


def _chunks(a, L):
    B, T = a.shape[:2]
    return a.reshape((B, T // L, L) + a.shape[2:]).swapaxes(0, 1)


def _unchunk(a):
    n, B, L = a.shape[:3]
    return a.swapaxes(0, 1).reshape((B, n * L) + a.shape[3:])


def _chunk_len(T):
    return CHUNK if T % CHUNK == 0 else T


def _hgrn_inputs(qa, fa, ia, lb):
    B, T = qa.shape[:2]
    shp = (B, T, HG_HEADS, HG_DK)
    q = jax.nn.silu(qa.astype(F32)).reshape(shp)
    lb = lb.reshape(HG_HEADS, HG_DK)
    logf = jnp.logaddexp(jnp.log(lb), jnp.log1p(-lb) + jax.nn.log_sigmoid(fa.astype(F32).reshape(shp)))
    k = -jnp.expm1(logf)
    v = ia.astype(F32).reshape(B, T, HG_HEADS, HG_DV)
    return q, k, v, logf


def _hgrn2_scan(q, k, v, logf, S0):
    L = _chunk_len(q.shape[1])
    causal = jnp.tril(jnp.ones((L, L), bool))

    def step(S, inp):
        qc, kc, vc, gc = inp
        Bc = jnp.cumsum(gc, axis=1)
        inter = jnp.einsum('blhk,bhkv->blhv', qc * jnp.exp(Bc), S)
        diff = Bc[:, :, None] - Bc[:, None]
        decay = jnp.exp(jnp.where(causal[None, :, :, None, None], diff, -jnp.inf))
        att = jnp.einsum('bthk,bshk,btshk->bhts', qc, kc, decay)
        intra = jnp.einsum('bhts,bshv->bthv', att, vc)
        Bl = Bc[:, -1]
        S = jnp.exp(Bl)[..., None] * S + jnp.einsum('bshk,bshv->bhkv', kc * jnp.exp(Bl[:, None] - Bc), vc)
        return S, inter + intra

    S, o = lax.scan(step, S0.astype(F32), (_chunks(q, L), _chunks(k, L), _chunks(v, L), _chunks(logf, L)))
    return _unchunk(o), S


def _mlstm_scan(q, k, v, log_i, log_f, C0, n0, m0):
    L = _chunk_len(q.shape[1])
    causal = jnp.tril(jnp.ones((L, L), bool))

    def step(carry, inp):
        C, n, m = carry
        qc, kc, vc, ic, fc = inp
        b = jnp.cumsum(fc, axis=1)
        dmat = jnp.where(causal[None, :, :, None], b[:, :, None] - b[:, None] + ic[:, None], -jnp.inf)
        inter = b + m[:, None]
        mt = jnp.maximum(inter, dmat.max(axis=2))
        w_in = jnp.exp(dmat - mt[:, :, None])
        w_x = jnp.exp(inter - mt)
        sw = jnp.einsum('bthk,bshk->btsh', qc, kc) * w_in
        num = w_x[..., None] * jnp.einsum('bthk,bhvk->bthv', qc, C) + jnp.einsum('btsh,bshv->bthv', sw, vc)
        den = w_x * jnp.einsum('bthk,bhk->bth', qc, n) + sw.sum(axis=2)
        h = num / jnp.maximum(jnp.abs(den), jnp.exp(-mt))[..., None]
        mL = mt[:, -1]
        w_end = jnp.exp(b[:, -1:] - b + ic - mL[:, None])
        dC = jnp.exp(b[:, -1] + m - mL)
        C = dC[..., None, None] * C + jnp.einsum('bsh,bshv,bshk->bhvk', w_end, vc, kc)
        n = dC[..., None] * n + jnp.einsum('bsh,bshk->bhk', w_end, kc)
        return (C, n, mL), h

    init = (C0.astype(F32), n0.astype(F32), m0.astype(F32))
    xs = (_chunks(q, L), _chunks(k, L), _chunks(v, L), _chunks(log_i, L), _chunks(log_f, L))
    (C, n, m), h = lax.scan(step, init, xs)
    return _unchunk(h), C, n, m


def _compress(rows, w1, b1, w2, pe):
    B, T = rows.shape[:2]
    r = CMP_BLOCK // CMP_STRIDE
    nch = T // CMP_STRIDE
    n_cmp = nch - r + 1
    rc = rows[:, :nch * CMP_STRIDE].reshape(B, nch, CMP_STRIDE, NSA_KVH, NSA_HD)
    pe_c = pe.reshape(r, CMP_STRIDE, NSA_HD)
    w1_c = w1.reshape(r, CMP_STRIDE, NSA_HD, NSA_HD)
    h = b1
    for j in range(r):
        h = h + jnp.einsum('bnskd,sde->bnke', rc[:, j:j + n_cmp] + pe_c[j][:, None, :], w1_c[j])
    return jnp.einsum('bnke,ed->bnkd', jax.nn.gelu(h), w2)


def _cmp_attn(q, qpos, kc, vc, rel_bias):
    kend = jnp.arange(kc.shape[1]) * CMP_STRIDE + CMP_BLOCK - 1
    dist = qpos[:, None] - kend[None, :]
    s = jnp.einsum('bqkgd,bnkd->bkgqn', q, kc).astype(F32) * NSA_HD ** -0.5 + _rel_bias(rel_bias, dist)
    p = _masked_softmax(s, dist >= 0)
    return jnp.einsum('bkgqn,bnkd->bqkgd', p.astype(vc.dtype), vc), p


def _cmp_to_slc(p, n_slc):
    r = SEL_BLOCK // CMP_STRIDE
    c = CMP_BLOCK // CMP_STRIDE
    front = c - 1
    back = max(r * n_slc + r - p.shape[-1], 0)
    pp = jnp.pad(p, [(0, 0)] * (p.ndim - 1) + [(front, back)])
    terms = [lax.slice_in_dim(pp, front + m - n, front + m - n + r * (n_slc - 1) + 1, stride=r, axis=p.ndim - 1)
             for m in range(r) for n in range(c)]
    return sum(terms[1:], terms[0])


def _select(p, qpos, n_slc):
    ps = _cmp_to_slc(p.sum(axis=2), n_slc)
    blk = jnp.arange(n_slc)
    cur = (qpos // SEL_BLOCK)[:, None]
    forced = (blk == 0) | (blk == cur) | (blk == cur - 1)
    score = jnp.where(forced, jnp.inf, ps)
    score = jnp.where(blk > cur, -jnp.inf, score)
    _, idx = lax.top_k(score, min(N_SEL, n_slc))
    idx = jnp.moveaxis(idx, 1, 2)
    valid = idx <= (qpos // SEL_BLOCK)[None, :, None, None]
    return idx, valid


def _sel_attn(q, qpos, idx, valid, ks, vs, rel_bias):
    B, Tq = q.shape[:2]
    nk = idx.shape[-1]
    kpos = idx[..., None] * SEL_BLOCK + jnp.arange(SEL_BLOCK)
    dist = qpos[None, :, None, None, None] - kpos
    mask = (valid[..., None] & (dist >= 0)).reshape(B, Tq, NSA_KVH, 1, nk * SEL_BLOCK)
    tbl = rel_bias.reshape(REL_BUCKETS, NSA_KVH, NSA_G)
    bias = tbl[_rel_bucket(dist), jnp.arange(NSA_KVH)[:, None, None]]
    s = jnp.einsum('bqkgd,bqknsd->bqkgns', q, ks).astype(F32) * NSA_HD ** -0.5 + jnp.moveaxis(bias, -1, 3).astype(F32)
    p = _masked_softmax(s.reshape(B, Tq, NSA_KVH, NSA_G, nk * SEL_BLOCK), mask)
    return jnp.einsum('bqkgm,bqkmd->bqkgd', p.astype(vs.dtype), vs.reshape(B, Tq, NSA_KVH, nk * SEL_BLOCK, NSA_HD))


def _sel_prompt(q, qpos, idx, valid, k, v, rel_bias):
    B, T = q.shape[:2]
    nb = T // SEL_Q_BLOCK

    def blocks(a):
        return a.reshape(B, T // SEL_BLOCK, SEL_BLOCK, NSA_KVH, NSA_HD).transpose(0, 3, 1, 2, 4)

    kt, vt = blocks(k), blocks(v)
    bi = jnp.arange(B)[:, None, None, None]
    hi = jnp.arange(NSA_KVH)[None, None, :, None]

    def qsplit(a):
        return a.reshape((B, nb, SEL_Q_BLOCK) + a.shape[2:]).swapaxes(0, 1)

    def step(args):
        qb, pb, ib, vb = args
        return _sel_attn(qb, pb, ib, vb, kt[bi, hi, ib], vt[bi, hi, ib], rel_bias)

    out = lax.map(step, (qsplit(q), qpos.reshape(nb, SEL_Q_BLOCK), qsplit(idx), qsplit(valid)))
    return out.swapaxes(0, 1).reshape(q.shape)


def _gather_pages(pool, page_table):
    g = pool[page_table]
    return g.reshape((g.shape[0], -1) + g.shape[3:])


def _gather_sel_sample(pool, new, page_table, idx):
    B = idx.shape[0]
    n_pages = PAST_LEN // PAGE_SIZE
    pos = idx[..., None] * SEL_BLOCK + jnp.arange(SEL_BLOCK)
    bi = jnp.arange(B)[:, None, None, None, None]
    hi = jnp.arange(NSA_KVH)[None, None, :, None, None]
    page = page_table[bi, jnp.clip(pos // PAGE_SIZE, 0, n_pages - 1)]
    past_rows = pool[page, pos % PAGE_SIZE, hi]
    new_rows = new[bi, jnp.clip(pos - PAST_LEN, 0, new.shape[1] - 1), hi]
    return jnp.where((pos < PAST_LEN)[..., None], past_rows, new_rows)


def _local_attn(q, k, v, qpos, kpos, rel_bias):
    dist = qpos[..., :, None] - kpos[..., None, :]
    mask = (dist >= 0) & (dist < WINDOW) & (kpos[..., None, :] >= 0)
    s = jnp.einsum('b...qkgd,b...skd->b...kgqs', q, k).astype(F32) * NSA_HD ** -0.5 + _rel_bias(rel_bias, dist)
    p = _masked_softmax(s, mask[..., None, None, :, :])
    return jnp.einsum('b...kgqs,b...skd->b...qkgd', p.astype(v.dtype), v)


def _window_prompt(q, k, v, rel_bias):
    B, T = q.shape[:2]
    nq = T // Q_BLOCK
    nk = WINDOW // Q_BLOCK + 1
    pad = [(0, 0), (WINDOW, 0), (0, 0), (0, 0)]

    def band(a):
        ab = jnp.pad(a, pad).reshape(B, nq + nk - 1, Q_BLOCK, NSA_KVH, NSA_HD)
        return jnp.stack([ab[:, j:j + nq] for j in range(nk)], axis=2).reshape(B, nq, nk * Q_BLOCK, NSA_KVH, NSA_HD)

    qpos = jnp.arange(T).reshape(nq, Q_BLOCK)
    kpos = (jnp.arange(nq) * Q_BLOCK - WINDOW)[:, None] + jnp.arange(nk * Q_BLOCK)[None, :]
    qb = q.reshape(B, nq, Q_BLOCK, NSA_KVH, NSA_G, NSA_HD)
    return _local_attn(qb, band(k), band(v), qpos, kpos, rel_bias).reshape(q.shape)


def _nsa_heads(qb, kvb):
    B, T = qb.shape[:2]
    q = qb.reshape(B, T, NSA_KVH, NSA_G, NSA_HD)
    kv = kvb.reshape(B, T, 6, NSA_KVH, NSA_HD)
    return q, [kv[:, :, j] for j in range(6)]


def _mem_attn(qm, mk, mv):
    B, T = qm.shape[:2]
    q = qm.reshape(B, T, MEM_HEADS, MEM_HD)
    s = jnp.einsum('bqhd,bmhd->bhqm', q, mk).astype(F32) * MEM_HD ** -0.5
    p = jax.nn.softmax(s, axis=-1)
    return jnp.einsum('bhqm,bmhd->bqhd', p.astype(mv.dtype), mv).reshape(B, T, MEM_W)


def _project(hn, w_bf16, dst):
    B, T, D = hn.shape
    y = _matmul(hn.reshape(B * T, D), w_bf16)
    return {n: y[:, o:o + w].reshape(B, T, w) for n, (o, w) in dst.items()}


def _out_project(parts, w_out_bf16):
    B, T = parts[0].shape[:2]
    x = jnp.concatenate([p.astype(BF16) for p in parts], axis=-1)
    return _matmul(x.reshape(B * T, x.shape[-1]), w_out_bf16).reshape(B, T, -1)


def _even_mix_inputs(hn, w_in, dst):
    s = _project(hn, w_in, dst)
    return [s[n] for n in ("qa", "fa", "ia", "za", "qb", "kvb", "gb", "zb", "qm")]


def _even_out(hn, oa, za, o3, gb, b_gate, zb, qm, mk, mv, g_norm, w_out):
    B, T = hn.shape[:2]
    oa = _rmsnorm(oa, g_norm).reshape(B, T, HG_W) * jax.nn.silu(za)
    g = jax.nn.sigmoid((gb + b_gate).astype(F32)).reshape(B, T, 3, NSA_KVH, NSA_G, 1)
    ob = g[:, :, 0] * o3[0] + g[:, :, 1] * o3[1] + g[:, :, 2] * o3[2]
    ob = ob.reshape(B, T, NSA_W) * jax.nn.silu(zb)
    om = _mem_attn(qm, mk, mv)
    return _out_project([oa, ob, om], w_out)


def _even_prompt(hn, mk, mv, w_in, dst, b_gate, w1, b1, w2, pe, lb, g_norm, w_out, rel_bias):
    B, T = hn.shape[:2]
    qa, fa, ia, za, qb, kvb, gb, zb, qm = _even_mix_inputs(hn, w_in, dst)
    hq, hk, hv, hf = _hgrn_inputs(qa, fa, ia, lb)
    oa, S = _hgrn2_scan(hq, hk, hv, hf, jnp.zeros((B, HG_HEADS, HG_DK, HG_DV), F32))
    q, (kc, vc, ks, vs, kw, vw) = _nsa_heads(qb, kvb)
    qpos = jnp.arange(T)
    kcmp = _compress(kc, w1[0], b1[0], w2[0], pe[0])
    vcmp = _compress(vc, w1[1], b1[1], w2[1], pe[1])
    o_cmp, p = _cmp_attn(q, qpos, kcmp, vcmp, rel_bias)
    idx, valid = _select(p, qpos, T // SEL_BLOCK)
    o_sel = _sel_prompt(q, qpos, idx, valid, ks, vs, rel_bias)
    o_win = _window_prompt(q, kw, vw, rel_bias)
    y = _even_out(hn, oa, za, (o_cmp, o_sel, o_win), gb, b_gate, zb, qm, mk, mv, g_norm, w_out)
    wb = min(WINDOW, T)
    return y, (kc, vc, ks, vs, kw[:, -wb:], vw[:, -wb:], S)


def _even_sample(hn, mk, mv, page_table, pk_cmp, pv_cmp, pk_sel, pv_sel, wk, wv, S0,
                 w_in, dst, b_gate, w1, b1, w2, pe, lb, g_norm, w_out, rel_bias):
    B, T = hn.shape[:2]
    qa, fa, ia, za, qb, kvb, gb, zb, qm = _even_mix_inputs(hn, w_in, dst)
    hq, hk, hv, hf = _hgrn_inputs(qa, fa, ia, lb)
    oa, S = _hgrn2_scan(hq, hk, hv, hf, S0)
    q, (kc, vc, ks, vs, kw, vw) = _nsa_heads(qb, kvb)
    qpos = PAST_LEN + jnp.arange(T)
    kcmp = _compress(jnp.concatenate([_gather_pages(pk_cmp, page_table), kc], axis=1), w1[0], b1[0], w2[0], pe[0])
    vcmp = _compress(jnp.concatenate([_gather_pages(pv_cmp, page_table), vc], axis=1), w1[1], b1[1], w2[1], pe[1])
    o_cmp, p = _cmp_attn(q, qpos, kcmp, vcmp, rel_bias)
    idx, valid = _select(p, qpos, -(-(PAST_LEN + T) // SEL_BLOCK))
    o_sel = _sel_attn(q, qpos, idx, valid, _gather_sel_sample(pk_sel, ks, page_table, idx),
                      _gather_sel_sample(pv_sel, vs, page_table, idx), rel_bias)
    wb = wk.shape[1]
    kk = jnp.concatenate([wk, kw], axis=1)
    vv = jnp.concatenate([wv, vw], axis=1)
    o_win = _local_attn(q, kk, vv, qpos, PAST_LEN - wb + jnp.arange(wb + T), rel_bias)
    y = _even_out(hn, oa, za, (o_cmp, o_sel, o_win), gb, b_gate, zb, qm, mk, mv, g_norm, w_out)
    return y, (kc, vc, ks, vs, kk[:, -wb:], vv[:, -wb:], S)


def _odd_mix(hn, mk, mv, C0, n0, m0, w_in, dst, b_if, g_norm, w_out):
    B, T = hn.shape[:2]
    s = _project(hn, w_in, dst)
    q, k, v, og, ig, fg, z, qm = [s[n] for n in ("q", "k", "v", "og", "ig", "fg", "z", "qm")]
    q = q.astype(F32).reshape(B, T, ML_HEADS, ML_DK)
    k = k.astype(F32).reshape(B, T, ML_HEADS, ML_DK) * ML_DK ** -0.5
    v = v.astype(F32).reshape(B, T, ML_HEADS, ML_DV)
    log_i = ig.astype(F32) + b_if[0]
    log_f = jax.nn.log_sigmoid(fg.astype(F32) + b_if[1])
    h, C, n, m = _mlstm_scan(q, k, v, log_i, log_f, C0, n0, m0)
    h = _rmsnorm(h, g_norm) * jax.nn.sigmoid(og.astype(F32)).reshape(B, T, ML_HEADS, ML_DV)
    h = h.reshape(B, T, ML_V_W) * jax.nn.silu(z)
    om = _mem_attn(qm, mk, mv)
    return _out_project([h, om], w_out), (C, n, m)


def _stack(lst, i):
    return jnp.stack([t[i] for t in lst])


def _norm3(x, w, out_dtype):
    B, T, D = x.shape
    return _rmsnorm_rows(x.reshape(B * T, D), w, out_dtype).reshape(B, T, D)


def kernel(x_prompt, x_sample, cache_mem_k, cache_mem_v, cache_cmp_k, cache_cmp_v, cache_sel_k, cache_sel_v,
           cache_win_k, cache_win_v, state_hgrn, state_mlstm_c, state_mlstm_n, state_mlstm_m, page_table,
           mem_prompt, norm_w, mem_norm_w, final_norm_w, rel_bias, w_mem_kv, w_in_even, b_nsa_gate,
           w_cmp1, b_cmp1, w_cmp2, pe_cmp, hgrn_lb_logits, hgrn_norm_w, w_out_even, w_in_odd, b_mlstm_if,
           mlstm_norm_w, w_out_odd):
    lbs = jnp.cumsum(jax.nn.softmax(hgrn_lb_logits.astype(F32), axis=0), axis=0)
    hp, hs = x_prompt, x_sample
    mem_new, even_p, even_s, odd_p, odd_s = [], [], [], [], []
    for l in range(DEPTH):
        npre = _norm3(hp, norm_w[l], BF16)
        nsam = _norm3(hs, norm_w[l], BF16)
        mem_n = _norm3(mem_prompt, mem_norm_w[l], BF16)
        bm, nm = mem_n.shape[:2]
        mkv = _matmul(mem_n.reshape(bm * nm, D_MODEL), w_mem_kv[l].astype(BF16)).reshape(bm, nm, 2 * MEM_W)
        mk_p = mkv[..., :MEM_W].reshape(bm, nm, MEM_HEADS, MEM_HD)
        mv_p = mkv[..., MEM_W:].reshape(bm, nm, MEM_HEADS, MEM_HD)
        mem_new.append((mk_p, mv_p))
        mk_s, mv_s = cache_mem_k[l], cache_mem_v[l]
        if l % 2 == 0:
            e = l // 2
            w_in, dst = _relayout_w_in(w_in_even[e], EVEN_SEGS, EVEN_ORDER)
            wts = (w_in, dst, b_nsa_gate[e], w_cmp1[e], b_cmp1[e], w_cmp2[e], pe_cmp[e], lbs[l],
                   hgrn_norm_w[e], w_out_even[e].astype(BF16), rel_bias)
            yp, st_p = _even_prompt(npre, mk_p, mv_p, *wts)
            ys, st_s = _even_sample(nsam, mk_s, mv_s, page_table, cache_cmp_k[e], cache_cmp_v[e], cache_sel_k[e],
                                    cache_sel_v[e], cache_win_k[e], cache_win_v[e], state_hgrn[e], *wts)
            even_p.append(st_p)
            even_s.append(st_s)
        else:
            o = l // 2
            w_in, dst = _relayout_w_in(w_in_odd[o], ODD_SEGS, ODD_ORDER)
            wts = (w_in, dst, b_mlstm_if[o], mlstm_norm_w[o], w_out_odd[o].astype(BF16))
            bp = hp.shape[0]
            yp, st_p = _odd_mix(npre, mk_p, mv_p, jnp.zeros((bp, ML_HEADS, ML_DV, ML_DK), F32),
                                jnp.zeros((bp, ML_HEADS, ML_DK), F32), jnp.zeros((bp, ML_HEADS), F32), *wts)
            ys, st_s = _odd_mix(nsam, mk_s, mv_s, state_mlstm_c[o], state_mlstm_n[o], state_mlstm_m[o], *wts)
            odd_p.append(st_p)
            odd_s.append(st_s)
        hp = hp + yp
        hs = hs + ys
    y_prompt = _norm3(hp, final_norm_w, F32)
    y_sample = _norm3(hs, final_norm_w, F32)
    return (y_prompt, y_sample,
            _stack(mem_new, 0), _stack(mem_new, 1),
            _stack(even_p, 0), _stack(even_p, 1), _stack(even_p, 2), _stack(even_p, 3),
            _stack(even_p, 4), _stack(even_p, 5), _stack(even_p, 6),
            _stack(odd_p, 0), _stack(odd_p, 1), _stack(odd_p, 2),
            _stack(even_s, 0), _stack(even_s, 1), _stack(even_s, 2), _stack(even_s, 3),
            _stack(even_s, 4), _stack(even_s, 5), _stack(even_s, 6),
            _stack(odd_s, 0), _stack(odd_s, 1), _stack(odd_s, 2))
```

```python
import functools
import math

import jax
import jax.numpy as jnp
import numpy as np
from jax import lax
from jax.experimental import pallas as pl
from jax.experimental.pallas import tpu as pltpu

D_MODEL = 4096
DEPTH = 2
PAST_LEN = 16384
PAGE_SIZE = 128
N_MEM = 256
EPS = 1e-6
CHUNK = 64

HG_DK = 128
HG_DV = 128
HG_HEADS = D_MODEL // 2 // HG_DV
HG_W = HG_HEADS * HG_DV

NSA_HD = 128
NSA_HEADS = D_MODEL // 2 // NSA_HD
NSA_KVH = 4
NSA_G = NSA_HEADS // NSA_KVH
NSA_W = NSA_HEADS * NSA_HD
NSA_KV_W = NSA_KVH * NSA_HD
CMP_BLOCK = 32
CMP_STRIDE = 16
SEL_BLOCK = 64
SEL_SHIFT = SEL_BLOCK.bit_length() - 1
N_SEL = 16
WINDOW = 512
Q_BLOCK = 128

ML_HEADS = D_MODEL // 512
ML_DK = D_MODEL // 2 // ML_HEADS
ML_DV = D_MODEL // ML_HEADS
ML_QK_W = ML_HEADS * ML_DK
ML_V_W = ML_HEADS * ML_DV

MEM_HEADS = 4
MEM_HD = 128
MEM_W = MEM_HEADS * MEM_HD

REL_BUCKETS = 32
REL_MAX_DIST = 128

F32 = jnp.float32
BF16 = jnp.bfloat16
LANES = 128
NEG_INF = float("-inf")
TINY = float(np.finfo(np.float32).tiny)
EXP_CLAMP = 80.0
VMEM_LIMIT = 56 * 1024 * 1024

HG_HB = 4
ML_HB = 2
HG_SUB = 16
SAMPLE_PAD_T = 16

EVEN_OFF = {"qa": 0, "fa": HG_W, "ia": 2 * HG_W, "za": 3 * HG_W, "qb": 4 * HG_W, "kvb": 4 * HG_W + NSA_W}
EVEN_OFF["zb"] = EVEN_OFF["kvb"] + 6 * NSA_KV_W
EVEN_OFF["qm"] = EVEN_OFF["zb"] + NSA_W
EVEN_OFF["gb"] = EVEN_OFF["qm"] + MEM_W
EVEN_N = EVEN_OFF["gb"] + NSA_KVH * LANES
ODD_OFF = {"q": 0, "k": ML_QK_W, "v": 2 * ML_QK_W, "og": 2 * ML_QK_W + ML_V_W, "z": 2 * ML_QK_W + 2 * ML_V_W}
ODD_OFF["qm"] = ODD_OFF["z"] + ML_V_W
ODD_OFF["gates"] = ODD_OFF["qm"] + MEM_W
ODD_N = ODD_OFF["gates"] + (ML_HEADS // ML_HB) * LANES


def _dot(a, b):
    return jnp.dot(a, b, preferred_element_type=F32)


def _dot_nt(a, b):
    return lax.dot_general(a, b, (((1,), (1,)), ((), ())), preferred_element_type=F32)


def _dot_tn(a, b):
    return lax.dot_general(a, b, (((0,), (0,)), ((), ())), preferred_element_type=F32)


def _iota2(shape, dim):
    return lax.broadcasted_iota(jnp.int32, shape, dim)


def _cumsum_rows(x, tri_b):
    hi = x.astype(BF16)
    r1 = x - hi.astype(F32)
    mid = r1.astype(BF16)
    lo = (r1 - mid.astype(F32)).astype(BF16)
    return _dot(tri_b, hi) + _dot(tri_b, mid) + _dot(tri_b, lo)


def _row_to_col(row, n):
    eye = _iota2((n, n), 0) == _iota2((n, n), 1)
    return jnp.sum(jnp.where(eye, row, 0.0), axis=1, keepdims=True)


def _col_to_row(col, n):
    eye = _iota2((n, n), 0) == _iota2((n, n), 1)
    return jnp.sum(jnp.where(eye, col, 0.0), axis=0, keepdims=True)


def _silu(x):
    return x * jax.nn.sigmoid(x)


def _params(sem):
    return pltpu.CompilerParams(dimension_semantics=sem, vmem_limit_bytes=VMEM_LIMIT)


def _rmsnorm_body(x_ref, w_ref, o_ref):
    x = x_ref[...].astype(F32)
    y = x * lax.rsqrt(jnp.mean(x * x, axis=-1, keepdims=True) + EPS)
    o_ref[...] = (y * w_ref[...].astype(F32)).astype(o_ref.dtype)


def _rmsnorm_rows(x2d, w, out_dtype, tm=256):
    m, d = x2d.shape
    tm = min(tm, m)
    return pl.pallas_call(
        _rmsnorm_body,
        grid=(m // tm,),
        in_specs=[pl.BlockSpec((tm, d), lambda i: (i, 0)), pl.BlockSpec((1, d), lambda i: (0, 0))],
        out_specs=pl.BlockSpec((tm, d), lambda i: (i, 0)),
        out_shape=jax.ShapeDtypeStruct((m, d), out_dtype),
        compiler_params=_params(("parallel",)),
        name="rmsnorm",
    )(x2d, w.reshape(1, d))


def _matmul_body(a_ref, b_ref, o_ref, acc_ref):
    @pl.when(pl.program_id(2) == 0)
    def _():
        acc_ref[...] = jnp.zeros_like(acc_ref)

    acc_ref[...] += _dot(a_ref[...], b_ref[...])

    @pl.when(pl.program_id(2) == pl.num_programs(2) - 1)
    def _():
        o_ref[...] = acc_ref[...]


def _matmul(a, b, tm=1024, tn=1024, tk=512):
    m, k = a.shape
    _, n = b.shape
    tm, tn, tk = min(tm, m), min(tn, n), min(tk, k)
    assert m % tm == 0 and n % tn == 0 and k % tk == 0, (a.shape, b.shape)
    return pl.pallas_call(
        _matmul_body,
        grid=(m // tm, n // tn, k // tk),
        in_specs=[pl.BlockSpec((tm, tk), lambda i, j, l: (i, l)), pl.BlockSpec((tk, tn), lambda i, j, l: (l, j))],
        out_specs=pl.BlockSpec((tm, tn), lambda i, j, l: (i, j)),
        out_shape=jax.ShapeDtypeStruct((m, n), F32),
        scratch_shapes=[pltpu.VMEM((tm, tn), F32)],
        compiler_params=_params(("parallel", "parallel", "arbitrary")),
        name="matmul",
    )(a, b)


def _outproj_body(*refs, widths):
    xs = refs[:len(widths)]
    w_ref, r_ref, o_ref = refs[len(widths):]
    acc = r_ref[...]
    off = 0
    for x_ref, w in zip(xs, widths):
        acc = acc + _dot(x_ref[...], w_ref[off:off + w, :])
        off += w
    o_ref[...] = acc


def _outproj(xs, w_bf16, resid, tm=1024, tn=512):
    m = resid.shape[0]
    n = w_bf16.shape[1]
    widths = tuple(x.shape[1] for x in xs)
    assert sum(widths) == w_bf16.shape[0]
    tm = min(tm, m)
    in_specs = [pl.BlockSpec((tm, w), lambda i, j: (i, 0)) for w in widths]
    in_specs += [pl.BlockSpec((w_bf16.shape[0], tn), lambda i, j: (0, j)), pl.BlockSpec((tm, tn), lambda i, j: (i, j))]
    return pl.pallas_call(
        functools.partial(_outproj_body, widths=widths),
        grid=(m // tm, n // tn),
        in_specs=in_specs,
        out_specs=pl.BlockSpec((tm, tn), lambda i, j: (i, j)),
        out_shape=jax.ShapeDtypeStruct((m, n), F32),
        compiler_params=_params(("parallel", "parallel")),
        name="outproj",
    )(*xs, w_bf16, resid)


def _hgrn_body(qa_ref, fa_ref, ia_ref, za_ref, lb_ref, gn_ref, s0_ref, o_ref, s_out, s_scr, *, L, valid):
    c = pl.program_id(2)

    @pl.when(c == 0)
    def _():
        s_scr[...] = s0_ref[...]

    lb = lb_ref[...]
    sig = jax.nn.sigmoid(fa_ref[...])
    logf = jnp.log(lb + (1.0 - lb) * sig)
    kk = (1.0 - lb) * (1.0 - sig)
    if valid < L:
        live = _iota2((L, 1), 0) < valid
        logf = jnp.where(live, logf, 0.0)
        kk = jnp.where(live, kk, 0.0)
    tri_b = (_iota2((L, L), 0) >= _iota2((L, L), 1)).astype(BF16)
    bc = _cumsum_rows(logf, tri_b)
    q = _silu(qa_ref[...])
    gate = _silu(za_ref[...])
    v = ia_ref[...]
    gn = gn_ref[...]
    for j in range(HG_HB):
        sl = slice(j * HG_DK, (j + 1) * HG_DK)
        bj, qj, kj = bc[:, sl], q[:, sl], kk[:, sl]
        vb = v[:, sl].astype(BF16)
        s_prev = s_scr[j]
        inter = _dot((qj * jnp.exp(bj)).astype(BF16), s_prev.astype(BF16))
        for i in range(L // HG_SUB):
            r0, r1 = i * HG_SUB, (i + 1) * HG_SUB
            mid = bj[r0 + HG_SUB // 2:r0 + HG_SUB // 2 + 1, :]
            qi = qj[r0:r1] * jnp.exp(jnp.minimum(bj[r0:r1] - mid, EXP_CLAMP))
            ki = kj * jnp.exp(jnp.minimum(mid - bj, EXP_CLAMP))
            att = _dot_nt(qi.astype(BF16), ki.astype(BF16))
            keep = _iota2((HG_SUB, L), 1) <= _iota2((HG_SUB, L), 0) + r0
            att = jnp.where(keep, att, 0.0)
            o_i = inter[r0:r1] + _dot(att.astype(BF16), vb)
            o_n = o_i * lax.rsqrt(jnp.mean(o_i * o_i, axis=-1, keepdims=True) + EPS) * gn
            o_ref[r0:r1, sl] = (o_n * gate[r0:r1, sl]).astype(o_ref.dtype)
        bl = bj[L - 1:L, :]
        kd = kj * jnp.exp(bl - bj)
        s_scr[j] = _row_to_col(jnp.exp(bl), HG_DK) * s_prev + _dot_tn(kd.astype(BF16), vb)

    @pl.when(c == pl.num_programs(2) - 1)
    def _():
        s_out[...] = s_scr[...]


def _hgrn_call(y, s0, lb, gn, *, B, T, L, valid):
    nc = T // L
    w = HG_HB * HG_DK

    def col(name):
        blk = EVEN_OFF[name] // w
        return pl.BlockSpec((L, w), lambda b, hg, c: (b * nc + c, blk + hg))

    state_spec = pl.BlockSpec((None, HG_HB, HG_DK, HG_DV), lambda b, hg, c: (b, hg, 0, 0))
    return pl.pallas_call(
        functools.partial(_hgrn_body, L=L, valid=valid),
        grid=(B, HG_HEADS // HG_HB, nc),
        in_specs=[col("qa"), col("fa"), col("ia"), col("za"),
                  pl.BlockSpec((1, w), lambda b, hg, c: (0, hg)),
                  pl.BlockSpec((1, HG_DV), lambda b, hg, c: (0, 0)),
                  state_spec],
        out_specs=[pl.BlockSpec((L, w), lambda b, hg, c: (b * nc + c, hg)), state_spec],
        out_shape=[jax.ShapeDtypeStruct((B * T, HG_W), BF16),
                   jax.ShapeDtypeStruct((B, HG_HEADS, HG_DK, HG_DV), F32)],
        scratch_shapes=[pltpu.VMEM((HG_HB, HG_DK, HG_DV), F32)],
        compiler_params=_params(("arbitrary", "arbitrary", "arbitrary")),
        name="hgrn2",
    )(y, y, y, y, lb.reshape(1, HG_W), gn.reshape(1, HG_DV), s0)


def _mlstm_body(q_ref, k_ref, v_ref, og_ref, z_ref, g_ref, bif_ref, gn_ref, c0_ref, n0_ref, m0_ref,
                h_ref, c_out, n_out, m_out, c_scr, n_scr, m_scr, *, L, valid):
    c = pl.program_id(2)

    @pl.when(c == 0)
    def _():
        c_scr[...] = c0_ref[...]
        n_scr[...] = n0_ref[...]
        m_scr[...] = m0_ref[...]

    gates = g_ref[...] + bif_ref[...]
    log_i = gates
    log_f = jnp.minimum(gates, 0.0) - jnp.log(1.0 + jnp.exp(-jnp.abs(gates)))
    if valid < L:
        live = _iota2((L, 1), 0) < valid
        log_i = jnp.where(live, log_i, -1e30)
        log_f = jnp.where(live, log_f, 0.0)
    tri = _iota2((L, L), 0) >= _iota2((L, L), 1)
    bcs = _cumsum_rows(log_f, tri.astype(BF16))
    for j in range(ML_HB):
        b_col = bcs[:, ML_HB + j:ML_HB + j + 1]
        i_col = log_i[:, j:j + 1]
        b_row = _col_to_row(b_col, L)
        i_row = _col_to_row(i_col, L)
        m_prev = m_scr[:, j:j + 1]
        dmat = jnp.where(tri, b_col - b_row + i_row, NEG_INF)
        inter = b_col + m_prev
        mt = jnp.maximum(inter, jnp.max(dmat, axis=1, keepdims=True))
        w_in = jnp.exp(dmat - mt)
        w_x = jnp.exp(inter - mt)
        qj = q_ref[:, j * ML_DK:(j + 1) * ML_DK]
        kj = k_ref[:, j * ML_DK:(j + 1) * ML_DK] * (ML_DK ** -0.5)
        vj = v_ref[:, j * ML_DV:(j + 1) * ML_DV]
        qb, kb = qj.astype(BF16), kj.astype(BF16)
        sw = _dot_nt(qb, kb) * w_in
        c_prev = c_scr[j]
        n_prev = n_scr[:, j * ML_DK:(j + 1) * ML_DK]
        num = w_x * _dot_nt(qb, c_prev.astype(BF16)) + _dot(sw.astype(BF16), vj.astype(BF16))
        den = w_x * jnp.sum(qj * n_prev, axis=1, keepdims=True) + jnp.sum(sw, axis=1, keepdims=True)
        h = num / jnp.maximum(jnp.abs(den), jnp.exp(-mt))
        m_last = mt[L - 1:L, :]
        b_last = b_col[L - 1:L, :]
        w_end = jnp.exp(b_last - b_col + i_col - m_last)
        d_c = jnp.exp(b_last + m_prev - m_last)
        c_scr[j] = d_c * c_prev + _dot_tn((w_end * vj).astype(BF16), kb)
        n_scr[:, j * ML_DK:(j + 1) * ML_DK] = d_c * n_prev + jnp.sum(w_end * kj, axis=0, keepdims=True)
        m_scr[:, j:j + 1] = m_last
        sv = slice(j * ML_DV, (j + 1) * ML_DV)
        h_n = h * lax.rsqrt(jnp.mean(h * h, axis=-1, keepdims=True) + EPS) * gn_ref[:, sv]
        h_ref[:, sv] = (h_n * jax.nn.sigmoid(og_ref[:, sv]) * _silu(z_ref[:, sv])).astype(h_ref.dtype)

    @pl.when(c == pl.num_programs(2) - 1)
    def _():
        c_out[...] = c_scr[...]
        n_out[...] = n_scr[...]
        m_out[...] = m_scr[...]


def _mlstm_call(y, c0, n0, m0, bif_r, gn, *, B, T, L, valid):
    nc = T // L
    ng = ML_HEADS // ML_HB
    wk, wv = ML_HB * ML_DK, ML_HB * ML_DV

    def col(name, w):
        blk = ODD_OFF[name] // w
        return pl.BlockSpec((L, w), lambda b, hg, c: (b * nc + c, blk + hg))

    c_spec = pl.BlockSpec((None, ML_HB, ML_DV, ML_DK), lambda b, hg, c: (b, hg, 0, 0))
    n_spec = pl.BlockSpec((None, 1, wk), lambda b, hg, c: (b, 0, hg))
    m_spec = pl.BlockSpec((None, None, 1, LANES), lambda b, hg, c: (b, hg, 0, 0))
    m0_r = jnp.pad(m0.reshape(B, ng, 1, ML_HB), ((0, 0), (0, 0), (0, 0), (0, LANES - ML_HB)))
    h, c_new, n_new, m_new = pl.pallas_call(
        functools.partial(_mlstm_body, L=L, valid=valid),
        grid=(B, ng, nc),
        in_specs=[col("q", wk), col("k", wk), col("v", wv), col("og", wv), col("z", wv), col("gates", LANES),
                  pl.BlockSpec((None, 1, LANES), lambda b, hg, c: (hg, 0, 0)),
                  pl.BlockSpec((1, wv), lambda b, hg, c: (0, hg)),
                  c_spec, n_spec, m_spec],
        out_specs=[pl.BlockSpec((L, wv), lambda b, hg, c: (b * nc + c, hg)), c_spec, n_spec, m_spec],
        out_shape=[jax.ShapeDtypeStruct((B * T, ML_V_W), BF16),
                   jax.ShapeDtypeStruct((B, ML_HEADS, ML_DV, ML_DK), F32),
                   jax.ShapeDtypeStruct((B, 1, ML_QK_W), F32),
                   jax.ShapeDtypeStruct((B, ng, 1, LANES), F32)],
        scratch_shapes=[pltpu.VMEM((ML_HB, ML_DV, ML_DK), F32), pltpu.VMEM((1, wk), F32), pltpu.VMEM((1, LANES), F32)],
        compiler_params=_params(("arbitrary", "arbitrary", "arbitrary")),
        name="mlstm",
    )(y, y, y, y, y, y, bif_r, gn.reshape(1, ML_V_W), c0, n0.reshape(B, 1, ML_QK_W), m0_r)
    return h, c_new, n_new.reshape(B, ML_HEADS, ML_DK), m_new[:, :, 0, :ML_HB].reshape(B, ML_HEADS)


def _mem_body(q_ref, k_ref, v_ref, o_ref):
    q = q_ref[...] * (MEM_HD ** -0.5)
    for h in range(MEM_HEADS):
        sl = slice(h * MEM_HD, (h + 1) * MEM_HD)
        s = _dot_nt(q[:, sl].astype(BF16), k_ref[:, sl].astype(BF16))
        p = jnp.exp(s - jnp.max(s, axis=-1, keepdims=True))
        o = _dot(p.astype(BF16), v_ref[:, sl].astype(BF16)) / jnp.sum(p, axis=-1, keepdims=True)
        o_ref[:, sl] = o.astype(o_ref.dtype)


def _mem_call(y, q_off, k2d, k_blk, v2d, v_blk, *, B, T, tq=256):
    tq = min(tq, T)
    nq = T // tq
    qb = q_off // MEM_W
    return pl.pallas_call(
        _mem_body,
        grid=(B, nq),
        in_specs=[pl.BlockSpec((tq, MEM_W), lambda b, i: (b * nq + i, qb)),
                  pl.BlockSpec((N_MEM, MEM_W), lambda b, i: (b, k_blk)),
                  pl.BlockSpec((N_MEM, MEM_W), lambda b, i: (b, v_blk))],
        out_specs=pl.BlockSpec((tq, MEM_W), lambda b, i: (b * nq + i, 0)),
        out_shape=jax.ShapeDtypeStruct((B * T, MEM_W), BF16),
        compiler_params=_params(("parallel", "parallel")),
        name="mem_attn",
    )(y, k2d, v2d)


def _gelu_tanh(x):
    return 0.5 * x * (1.0 + jnp.tanh(math.sqrt(2.0 / math.pi) * (x + 0.044715 * (x * x * x))))


def _compress_body(x_ref, w1_ref, b1_ref, w2_ref, pe_ref, o_ref, *, nch):
    a = jnp.zeros((nch, NSA_HD), F32)
    b = jnp.zeros((nch, NSA_HD), F32)
    for s in range(CMP_STRIDE):
        r = x_ref[pl.ds(s, nch, stride=CMP_STRIDE), :]
        a = a + _dot((r + pe_ref[s:s + 1, :]).astype(BF16), w1_ref[s])
        b = b + _dot((r + pe_ref[CMP_STRIDE + s:CMP_STRIDE + s + 1, :]).astype(BF16), w1_ref[CMP_STRIDE + s])
    h = a + pltpu.roll(b, nch - 1, 0) + b1_ref[...]
    o_ref[...] = _dot(_gelu_tanh(h).astype(BF16), w2_ref[...])


def _compress_call(y, w1, b1, w2, pe, *, B, T):
    nch = T // CMP_STRIDE
    kv_blk = EVEN_OFF["kvb"] // NSA_HD
    return pl.pallas_call(
        functools.partial(_compress_body, nch=nch),
        grid=(2, B, NSA_KVH),
        in_specs=[pl.BlockSpec((T, NSA_HD), lambda t, b, h: (b, kv_blk + t * NSA_KVH + h)),
                  pl.BlockSpec((None, CMP_BLOCK, NSA_HD, NSA_HD), lambda t, b, h: (t, 0, 0, 0)),
                  pl.BlockSpec((None, 1, NSA_HD), lambda t, b, h: (t, 0, 0)),
                  pl.BlockSpec((None, NSA_HD, NSA_HD), lambda t, b, h: (t, 0, 0)),
                  pl.BlockSpec((None, CMP_BLOCK, NSA_HD), lambda t, b, h: (t, 0, 0))],
        out_specs=pl.BlockSpec((None, None, None, nch, NSA_HD), lambda t, b, h: (t, b, h, 0, 0)),
        out_shape=jax.ShapeDtypeStruct((2, B, NSA_KVH, nch, NSA_HD), F32),
        compiler_params=_params(("parallel", "parallel", "parallel")),
        name="nsa_compress",
    )(y, w1.astype(BF16), b1.reshape(2, 1, NSA_HD), w2.astype(BF16), pe)


def _softmax_rows(s):
    m = jnp.max(s, axis=-1, keepdims=True)
    m = jnp.where(m == NEG_INF, 0.0, m)
    p = jnp.exp(s - m)
    return p, jnp.sum(p, axis=-1, keepdims=True)


def _nsa_prompt_body(q_ref, zb_ref, gb_ref, bg_ref, ks_ref, vs_ref, kw_ref, vw_ref, kc_ref, vc_ref,
                     bc_ref, bs_ref, bw_ref, o_ref, ksp, vsp, kwp, vwp, *, T):
    qi = pl.program_id(2)
    tq = Q_BLOCK
    front = T - tq
    wlen = WINDOW + tq
    n_slc = T // SEL_BLOCK

    @pl.when(qi == 0)
    def _():
        ksp[0:front, :] = jnp.zeros((front, NSA_HD), BF16)
        vsp[0:front, :] = jnp.zeros((front, NSA_HD), BF16)
        ksp[front:front + T, :] = ks_ref[...].astype(BF16)
        vsp[front:front + T, :] = vs_ref[...].astype(BF16)
        kwp[0:WINDOW, :] = jnp.zeros((WINDOW, NSA_HD), BF16)
        vwp[0:WINDOW, :] = jnp.zeros((WINDOW, NSA_HD), BF16)
        kwp[WINDOW:WINDOW + T, :] = kw_ref[...].astype(BF16)
        vwp[WINDOW:WINDOW + T, :] = vw_ref[...].astype(BF16)

    t0 = pl.multiple_of(qi * tq, tq)
    tpos = _iota2((tq, 1), 0) + t0
    q_all = q_ref[...] * (NSA_HD ** -0.5)
    qs = [q_all[:, g * NSA_HD:(g + 1) * NSA_HD].astype(BF16) for g in range(NSA_G)]

    ncmp = T // CMP_STRIDE
    vis = tpos >= _iota2((1, ncmp), 1) * CMP_STRIDE + (CMP_BLOCK - 1)
    kcb = kc_ref[...].astype(BF16)
    vcb = vc_ref[...].astype(BF16)
    psum = jnp.zeros((tq, ncmp), F32)
    o_cmp = []
    for g in range(NSA_G):
        s = jnp.where(vis, _dot_nt(qs[g], kcb) + bc_ref[g], NEG_INF)
        p, l = _softmax_rows(s)
        p = p / jnp.maximum(l, TINY)
        psum = psum + p
        o_cmp.append(_dot(p.astype(BF16), vcb))

    d = _iota2((ncmp, LANES), 0) - 4 * _iota2((ncmp, LANES), 1)
    wgt = jnp.where((d == -1) | (d == 3), 1.0, jnp.where((d >= 0) & (d <= 2), 2.0, 0.0))
    wgt = jnp.where(_iota2((ncmp, LANES), 1) < n_slc, wgt, 0.0).astype(BF16)
    p_hi = psum.astype(BF16)
    p_lo = (psum - p_hi.astype(F32)).astype(BF16)
    slc = _dot(p_hi, wgt) + _dot(p_lo, wgt)

    blk = _iota2((tq, LANES), 1)
    cur = jnp.right_shift(tpos, SEL_SHIFT)
    forced = (blk == 0) | (blk == cur) | (blk == cur - 1)
    score = jnp.where(forced, jnp.inf, slc)
    score = jnp.where(blk > cur, NEG_INF, score)
    blk_f = blk.astype(F32)
    sel = jnp.zeros((tq, LANES), F32)
    for _ in range(min(N_SEL, n_slc)):
        mx = jnp.max(score, axis=-1, keepdims=True)
        first = jnp.min(jnp.where(score == mx, blk_f, float(LANES)), axis=-1, keepdims=True)
        pick = blk_f == first
        sel = jnp.where(pick, 1.0, sel)
        score = jnp.where(pick, NEG_INF, score)
    member = jnp.where(blk <= cur, sel, 0.0).astype(BF16)

    col_blk = jnp.right_shift(_iota2((LANES, T), 1), SEL_SHIFT) + (qi * (tq // SEL_BLOCK) + (tq - T) // SEL_BLOCK)
    expand = (col_blk == _iota2((LANES, T), 0)).astype(BF16)
    kpos = _iota2((1, T), 1) + (t0 + tq - T)
    allowed = (_dot(member, expand) > 0.5) & (kpos <= tpos)
    mask_s = jnp.where(allowed, 0.0, NEG_INF)
    k_s = ksp[pl.ds(t0, T), :]
    v_s = vsp[pl.ds(t0, T), :]
    o_sel = []
    for g in range(NSA_G):
        p, l = _softmax_rows(_dot_nt(qs[g], k_s) + bs_ref[g] + mask_s)
        o_sel.append(_dot(p.astype(BF16), v_s) / jnp.maximum(l, TINY))

    dist = WINDOW + _iota2((tq, wlen), 0) - _iota2((tq, wlen), 1)
    in_win = (dist >= 0) & (dist < WINDOW) & (_iota2((1, wlen), 1) + (t0 - WINDOW) >= 0)
    mask_w = jnp.where(in_win, 0.0, NEG_INF)
    k_w = kwp[pl.ds(t0, wlen), :]
    v_w = vwp[pl.ds(t0, wlen), :]
    gate = jax.nn.sigmoid(gb_ref[...] + bg_ref[...])
    zb = _silu(zb_ref[...])
    for g in range(NSA_G):
        p, l = _softmax_rows(_dot_nt(qs[g], k_w) + bw_ref[g] + mask_w)
        o_win = _dot(p.astype(BF16), v_w) / jnp.maximum(l, TINY)
        mix = (gate[:, g:g + 1] * o_cmp[g] + gate[:, NSA_G + g:NSA_G + g + 1] * o_sel[g]
               + gate[:, 2 * NSA_G + g:2 * NSA_G + g + 1] * o_win)
        sl = slice(g * NSA_HD, (g + 1) * NSA_HD)
        o_ref[:, sl] = (mix * zb[:, sl]).astype(o_ref.dtype)


def _nsa_prompt_call(y, kvcmp, bg_r, bias_c, bias_s, bias_w, *, B, T):
    nq = T // Q_BLOCK
    gw = NSA_G * NSA_HD
    kv_blk = EVEN_OFF["kvb"] // NSA_HD
    wlen = WINDOW + Q_BLOCK

    def kv_spec(j):
        return pl.BlockSpec((T, NSA_HD), lambda b, h, i: (b, kv_blk + j * NSA_KVH + h))

    def cmp_spec(t):
        return pl.BlockSpec((None, None, None, T // CMP_STRIDE, NSA_HD), lambda b, h, i: (t, b, h, 0, 0))

    return pl.pallas_call(
        functools.partial(_nsa_prompt_body, T=T),
        grid=(B, NSA_KVH, nq),
        in_specs=[pl.BlockSpec((Q_BLOCK, gw), lambda b, h, i: (b * nq + i, EVEN_OFF["qb"] // gw + h)),
                  pl.BlockSpec((Q_BLOCK, gw), lambda b, h, i: (b * nq + i, EVEN_OFF["zb"] // gw + h)),
                  pl.BlockSpec((Q_BLOCK, LANES), lambda b, h, i: (b * nq + i, EVEN_OFF["gb"] // LANES + h)),
                  pl.BlockSpec((None, 1, LANES), lambda b, h, i: (h, 0, 0)),
                  kv_spec(2), kv_spec(3), kv_spec(4), kv_spec(5), cmp_spec(0), cmp_spec(1),
                  pl.BlockSpec((None, NSA_G, Q_BLOCK, T // CMP_STRIDE), lambda b, h, i: (h, 0, i, 0)),
                  pl.BlockSpec((None, NSA_G, Q_BLOCK, T), lambda b, h, i: (h, 0, 0, 0)),
                  pl.BlockSpec((None, NSA_G, Q_BLOCK, wlen), lambda b, h, i: (h, 0, 0, 0))],
        out_specs=pl.BlockSpec((Q_BLOCK, gw), lambda b, h, i: (b * nq + i, h)),
        out_shape=jax.ShapeDtypeStruct((B * T, NSA_W), BF16),
        scratch_shapes=[pltpu.VMEM((2 * T - Q_BLOCK, NSA_HD), BF16), pltpu.VMEM((2 * T - Q_BLOCK, NSA_HD), BF16),
                        pltpu.VMEM((WINDOW + T, NSA_HD), BF16), pltpu.VMEM((WINDOW + T, NSA_HD), BF16)],
        compiler_params=_params(("arbitrary", "arbitrary", "arbitrary")),
        name="nsa_prompt",
    )(y, y, y, bg_r, y, y, y, y, kvcmp, kvcmp, bias_c, bias_s, bias_w)


def _relayout_even(w):
    src = {}
    off = 0
    for name, width in (("qa", HG_W), ("fa", HG_W), ("ia", HG_W), ("za", HG_W), ("qb", NSA_W), ("kvb", 6 * NSA_KV_W),
                        ("gb", 3 * NSA_HEADS), ("zb", NSA_W), ("qm", MEM_W)):
        src[name] = (off, width)
        off += width
    cols = [w[:, src[n][0]:src[n][0] + src[n][1]] for n in ("qa", "fa", "ia", "za", "qb", "kvb", "zb", "qm")]
    g0 = src["gb"][0]
    for h in range(NSA_KVH):
        for j in range(3):
            cols.append(w[:, g0 + j * NSA_HEADS + h * NSA_G:g0 + j * NSA_HEADS + (h + 1) * NSA_G])
        cols.append(jnp.zeros((w.shape[0], LANES - 3 * NSA_G), w.dtype))
    return jnp.concatenate(cols, axis=1).astype(BF16)


def _relayout_odd(w):
    src = {}
    off = 0
    for name, width in (("q", ML_QK_W), ("k", ML_QK_W), ("v", ML_V_W), ("og", ML_V_W), ("ig", ML_HEADS),
                        ("fg", ML_HEADS), ("z", ML_V_W), ("qm", MEM_W)):
        src[name] = (off, width)
        off += width
    cols = [w[:, src[n][0]:src[n][0] + src[n][1]] for n in ("q", "k", "v", "og", "z", "qm")]
    for hg in range(ML_HEADS // ML_HB):
        for n in ("ig", "fg"):
            cols.append(w[:, src[n][0] + hg * ML_HB:src[n][0] + (hg + 1) * ML_HB])
        cols.append(jnp.zeros((w.shape[0], LANES - 2 * ML_HB), w.dtype))
    return jnp.concatenate(cols, axis=1).astype(BF16)


def _gate_bias_even(b_gate):
    g = b_gate.reshape(3, NSA_KVH, NSA_G).transpose(1, 0, 2).reshape(NSA_KVH, 1, 3 * NSA_G)
    return jnp.pad(g, ((0, 0), (0, 0), (0, LANES - 3 * NSA_G)))


def _gate_bias_odd(b_if):
    g = b_if.reshape(2, ML_HEADS // ML_HB, ML_HB).transpose(1, 0, 2).reshape(ML_HEADS // ML_HB, 1, 2 * ML_HB)
    return jnp.pad(g, ((0, 0), (0, 0), (0, LANES - 2 * ML_HB)))


def _rel_bucket(dist):
    n = jnp.maximum(dist, 0)
    exact = REL_BUCKETS // 2
    nf = jnp.maximum(n, 1).astype(F32)
    large = exact + (jnp.log(nf / exact) / math.log(REL_MAX_DIST / exact) * (REL_BUCKETS - exact)).astype(jnp.int32)
    return jnp.where(n < exact, n, jnp.minimum(large, REL_BUCKETS - 1))


def _bias_table(rel_bias, dist):
    b = rel_bias[_rel_bucket(jnp.asarray(dist, jnp.int32))].astype(F32)
    return jnp.moveaxis(b, -1, 0).reshape((NSA_KVH, NSA_G) + dist.shape)


def _prompt_bias_tables(rel_bias, T):
    r = np.arange(Q_BLOCK)[:, None]
    d_c = np.arange(T)[:, None] - (np.arange(T // CMP_STRIDE)[None, :] * CMP_STRIDE + CMP_BLOCK - 1)
    d_s = r - np.arange(T)[None, :] + (T - Q_BLOCK)
    d_w = r - np.arange(WINDOW + Q_BLOCK)[None, :] + WINDOW
    return _bias_table(rel_bias, d_c), _bias_table(rel_bias, d_s), _bias_table(rel_bias, d_w)


def _masked_softmax(s, mask):
    s = jnp.where(mask, s, -jnp.inf)
    m = jnp.max(s, axis=-1, keepdims=True)
    m = jnp.where(jnp.isfinite(m), m, 0.0)
    p = jnp.exp(s - m)
    return p / jnp.maximum(p.sum(axis=-1, keepdims=True), jnp.finfo(F32).tiny)


def _rel_bias(rel_bias, dist):
    b = rel_bias[_rel_bucket(dist)]
    b = b.reshape(b.shape[:-1] + (NSA_KVH, NSA_G))
    return jnp.moveaxis(b, (-2, -1), (-4, -3)).astype(F32)


def _compress(rows, w1, b1, w2, pe):
    B, T = rows.shape[:2]
    r = CMP_BLOCK // CMP_STRIDE
    nch = T // CMP_STRIDE
    n_cmp = nch - r + 1
    rc = rows[:, :nch * CMP_STRIDE].reshape(B, nch, CMP_STRIDE, NSA_KVH, NSA_HD)
    pe_c = pe.reshape(r, CMP_STRIDE, NSA_HD)
    w1_c = w1.reshape(r, CMP_STRIDE, NSA_HD, NSA_HD)
    h = b1
    for j in range(r):
        h = h + jnp.einsum('bnskd,sde->bnke', rc[:, j:j + n_cmp] + pe_c[j][:, None, :], w1_c[j])
    return jnp.einsum('bnke,ed->bnkd', jax.nn.gelu(h), w2)


def _cmp_attn(q, qpos, kc, vc, rel_bias):
    kend = jnp.arange(kc.shape[1]) * CMP_STRIDE + CMP_BLOCK - 1
    dist = qpos[:, None] - kend[None, :]
    s = jnp.einsum('bqkgd,bnkd->bkgqn', q, kc).astype(F32) * NSA_HD ** -0.5 + _rel_bias(rel_bias, dist)
    p = _masked_softmax(s, dist >= 0)
    return jnp.einsum('bkgqn,bnkd->bqkgd', p.astype(vc.dtype), vc), p


def _cmp_to_slc(p, n_slc):
    r = SEL_BLOCK // CMP_STRIDE
    c = CMP_BLOCK // CMP_STRIDE
    front = c - 1
    back = max(r * n_slc + r - p.shape[-1], 0)
    pp = jnp.pad(p, [(0, 0)] * (p.ndim - 1) + [(front, back)])
    terms = [lax.slice_in_dim(pp, front + m - n, front + m - n + r * (n_slc - 1) + 1, stride=r, axis=p.ndim - 1)
             for m in range(r) for n in range(c)]
    return sum(terms[1:], terms[0])


def _select(p, qpos, n_slc):
    ps = _cmp_to_slc(p.sum(axis=2), n_slc)
    blk = jnp.arange(n_slc)
    cur = (qpos // SEL_BLOCK)[:, None]
    forced = (blk == 0) | (blk == cur) | (blk == cur - 1)
    score = jnp.where(forced, jnp.inf, ps)
    score = jnp.where(blk > cur, -jnp.inf, score)
    _, idx = lax.top_k(score, min(N_SEL, n_slc))
    idx = jnp.moveaxis(idx, 1, 2)
    valid = idx <= (qpos // SEL_BLOCK)[None, :, None, None]
    return idx, valid


def _sel_attn(q, qpos, idx, valid, ks, vs, rel_bias):
    B, Tq = q.shape[:2]
    nk = idx.shape[-1]
    kpos = idx[..., None] * SEL_BLOCK + jnp.arange(SEL_BLOCK)
    dist = qpos[None, :, None, None, None] - kpos
    mask = (valid[..., None] & (dist >= 0)).reshape(B, Tq, NSA_KVH, 1, nk * SEL_BLOCK)
    tbl = rel_bias.reshape(REL_BUCKETS, NSA_KVH, NSA_G)
    bias = tbl[_rel_bucket(dist), jnp.arange(NSA_KVH)[:, None, None]]
    s = jnp.einsum('bqkgd,bqknsd->bqkgns', q, ks).astype(F32) * NSA_HD ** -0.5 + jnp.moveaxis(bias, -1, 3).astype(F32)
    p = _masked_softmax(s.reshape(B, Tq, NSA_KVH, NSA_G, nk * SEL_BLOCK), mask)
    return jnp.einsum('bqkgm,bqkmd->bqkgd', p.astype(vs.dtype), vs.reshape(B, Tq, NSA_KVH, nk * SEL_BLOCK, NSA_HD))


def _gather_pages(pool, page_table):
    g = pool[page_table]
    return g.reshape((g.shape[0], -1) + g.shape[3:])


def _gather_sel_sample(pool, new, page_table, idx):
    B = idx.shape[0]
    n_pages = PAST_LEN // PAGE_SIZE
    pos = idx[..., None] * SEL_BLOCK + jnp.arange(SEL_BLOCK)
    bi = jnp.arange(B)[:, None, None, None, None]
    hi = jnp.arange(NSA_KVH)[None, None, :, None, None]
    page = page_table[bi, jnp.clip(pos // PAGE_SIZE, 0, n_pages - 1)]
    past_rows = pool[page, pos % PAGE_SIZE, hi]
    new_rows = new[bi, jnp.clip(pos - PAST_LEN, 0, new.shape[1] - 1), hi]
    return jnp.where((pos < PAST_LEN)[..., None], past_rows, new_rows)


def _local_attn(q, k, v, qpos, kpos, rel_bias):
    dist = qpos[..., :, None] - kpos[..., None, :]
    mask = (dist >= 0) & (dist < WINDOW) & (kpos[..., None, :] >= 0)
    s = jnp.einsum('b...qkgd,b...skd->b...kgqs', q, k).astype(F32) * NSA_HD ** -0.5 + _rel_bias(rel_bias, dist)
    p = _masked_softmax(s, mask[..., None, None, :, :])
    return jnp.einsum('b...kgqs,b...skd->b...qkgd', p.astype(v.dtype), v)


def _nsa_sample(ys, page_table, pk_cmp, pv_cmp, pk_sel, pv_sel, wk, wv, b_gate, w1, b1, w2, pe, rel_bias, *, B, T):
    y3 = ys.reshape(B, SAMPLE_PAD_T, EVEN_N)[:, :T]
    o = EVEN_OFF
    q = y3[..., o["qb"]:o["qb"] + NSA_W].reshape(B, T, NSA_KVH, NSA_G, NSA_HD)
    kv = y3[..., o["kvb"]:o["kvb"] + 6 * NSA_KV_W].reshape(B, T, 6, NSA_KVH, NSA_HD)
    kc, vc, ks, vs, kw, vw = [kv[:, :, j] for j in range(6)]
    zb = y3[..., o["zb"]:o["zb"] + NSA_W]
    gb = y3[..., o["gb"]:o["gb"] + NSA_KVH * LANES].reshape(B, T, NSA_KVH, LANES)[..., :3 * NSA_G]
    gb = gb.reshape(B, T, NSA_KVH, 3, NSA_G).transpose(0, 1, 3, 2, 4).reshape(B, T, 3 * NSA_HEADS)
    qpos = PAST_LEN + jnp.arange(T)
    kcmp = _compress(jnp.concatenate([_gather_pages(pk_cmp, page_table), kc], axis=1), w1[0], b1[0], w2[0], pe[0])
    vcmp = _compress(jnp.concatenate([_gather_pages(pv_cmp, page_table), vc], axis=1), w1[1], b1[1], w2[1], pe[1])
    o_cmp, p = _cmp_attn(q, qpos, kcmp, vcmp, rel_bias)
    idx, valid = _select(p, qpos, -(-(PAST_LEN + T) // SEL_BLOCK))
    o_sel = _sel_attn(q, qpos, idx, valid, _gather_sel_sample(pk_sel, ks, page_table, idx),
                      _gather_sel_sample(pv_sel, vs, page_table, idx), rel_bias)
    wb = wk.shape[1]
    kk = jnp.concatenate([wk, kw], axis=1)
    vv = jnp.concatenate([wv, vw], axis=1)
    o_win = _local_attn(q, kk, vv, qpos, PAST_LEN - wb + jnp.arange(wb + T), rel_bias)
    g = jax.nn.sigmoid((gb + b_gate).astype(F32)).reshape(B, T, 3, NSA_KVH, NSA_G, 1)
    ob = g[:, :, 0] * o_cmp + g[:, :, 1] * o_sel + g[:, :, 2] * o_win
    ob = ob.reshape(B, T, NSA_W) * jax.nn.silu(zb)
    ob = jnp.pad(ob, ((0, 0), (0, SAMPLE_PAD_T - T), (0, 0))).astype(BF16).reshape(B * SAMPLE_PAD_T, NSA_W)
    return ob, (kc, vc, ks, vs, kk[:, -wb:], vv[:, -wb:])


def _kv_rows(y, B, T, j):
    off = EVEN_OFF["kvb"] + j * NSA_KV_W
    return y[:, off:off + NSA_KV_W].reshape(B, T, NSA_KVH, NSA_HD)


def _even_prompt(hp2d, npre, mkv_p, w_in, bg_r, w1, b1, w2, pe, lb, g_norm, w_out, rel_bias, *, B, T):
    y = _matmul(npre, w_in)
    oa, s_new = _hgrn_call(y, jnp.zeros((B, HG_HEADS, HG_DK, HG_DV), F32), lb, g_norm, B=B, T=T, L=CHUNK, valid=CHUNK)
    kvcmp = _compress_call(y, w1, b1, w2, pe, B=B, T=T)
    ob = _nsa_prompt_call(y, kvcmp, bg_r, *_prompt_bias_tables(rel_bias, T), B=B, T=T)
    om = _mem_call(y, EVEN_OFF["qm"], mkv_p, 0, mkv_p, 1, B=B, T=T)
    h_new = _outproj([oa, ob, om], w_out, hp2d)
    wb = min(WINDOW, T)
    kw, vw = _kv_rows(y, B, T, 4), _kv_rows(y, B, T, 5)
    return h_new, (_kv_rows(y, B, T, 0), _kv_rows(y, B, T, 1), _kv_rows(y, B, T, 2), _kv_rows(y, B, T, 3),
                   kw[:, -wb:], vw[:, -wb:], s_new)


def _even_sample(hs2d, nsam, mk_s, mv_s, page_table, pk_cmp, pv_cmp, pk_sel, pv_sel, wk, wv, s0,
                 w_in, bg_r, b_gate, w1, b1, w2, pe, lb, g_norm, w_out, rel_bias, *, B, T):
    tp = SAMPLE_PAD_T
    y = _matmul(nsam, w_in)
    oa, s_new = _hgrn_call(y, s0, lb, g_norm, B=B, T=tp, L=tp, valid=T)
    ob, caches = _nsa_sample(y, page_table, pk_cmp, pv_cmp, pk_sel, pv_sel, wk, wv, b_gate, w1, b1, w2, pe, rel_bias,
                             B=B, T=T)
    om = _mem_call(y, EVEN_OFF["qm"], mk_s.reshape(B * N_MEM, MEM_W), 0, mv_s.reshape(B * N_MEM, MEM_W), 0, B=B, T=tp)
    return _outproj([oa, ob, om], w_out, hs2d), caches + (s_new,)


def _odd_mix(h2d, hn, k2d, k_blk, v2d, v_blk, c0, n0, m0, w_in, bif_r, g_norm, w_out, *, B, T, L, valid):
    y = _matmul(hn, w_in)
    h, c_new, n_new, m_new = _mlstm_call(y, c0, n0, m0, bif_r, g_norm, B=B, T=T, L=L, valid=valid)
    om = _mem_call(y, ODD_OFF["qm"], k2d, k_blk, v2d, v_blk, B=B, T=T)
    return _outproj([h, om], w_out, h2d), (c_new, n_new, m_new)


def _stack(lst, i):
    return jnp.stack([t[i] for t in lst])


def kernel(x_prompt, x_sample, cache_mem_k, cache_mem_v, cache_cmp_k, cache_cmp_v, cache_sel_k, cache_sel_v,
           cache_win_k, cache_win_v, state_hgrn, state_mlstm_c, state_mlstm_n, state_mlstm_m, page_table,
           mem_prompt, norm_w, mem_norm_w, final_norm_w, rel_bias, w_mem_kv, w_in_even, b_nsa_gate,
           w_cmp1, b_cmp1, w_cmp2, pe_cmp, hgrn_lb_logits, hgrn_norm_w, w_out_even, w_in_odd, b_mlstm_if,
           mlstm_norm_w, w_out_odd):
    bp, tp = x_prompt.shape[:2]
    bs, ts = x_sample.shape[:2]
    tsp = SAMPLE_PAD_T
    lbs = jnp.cumsum(jax.nn.softmax(hgrn_lb_logits.astype(F32), axis=0), axis=0)
    hp = x_prompt.reshape(bp * tp, D_MODEL)
    hs = jnp.pad(x_sample, ((0, 0), (0, tsp - ts), (0, 0))).reshape(bs * tsp, D_MODEL)
    mem2d = mem_prompt.reshape(bp * N_MEM, D_MODEL)
    mem_new, even_p, even_s, odd_p, odd_s = [], [], [], [], []
    for l in range(DEPTH):
        npre = _rmsnorm_rows(hp, norm_w[l], BF16)
        nsam = _rmsnorm_rows(hs, norm_w[l], BF16)
        mkv = _matmul(_rmsnorm_rows(mem2d, mem_norm_w[l], BF16), w_mem_kv[l].astype(BF16))
        mem_new.append((mkv[:, :MEM_W].reshape(bp, N_MEM, MEM_HEADS, MEM_HD),
                        mkv[:, MEM_W:].reshape(bp, N_MEM, MEM_HEADS, MEM_HD)))
        mk_s, mv_s = cache_mem_k[l], cache_mem_v[l]
        if l % 2 == 0:
            e = l // 2
            w_in = _relayout_even(w_in_even[e])
            w_out = w_out_even[e].astype(BF16)
            bg_r = _gate_bias_even(b_nsa_gate[e])
            cmpw = (w_cmp1[e].reshape(2, CMP_BLOCK, NSA_HD, NSA_HD), b_cmp1[e], w_cmp2[e], pe_cmp[e])
            hp, st_p = _even_prompt(hp, npre, mkv, w_in, bg_r, *cmpw, lbs[l], hgrn_norm_w[e], w_out, rel_bias,
                                    B=bp, T=tp)
            hs, st_s = _even_sample(hs, nsam, mk_s, mv_s, page_table, cache_cmp_k[e], cache_cmp_v[e], cache_sel_k[e],
                                    cache_sel_v[e], cache_win_k[e], cache_win_v[e], state_hgrn[e], w_in, bg_r,
                                    b_nsa_gate[e], w_cmp1[e], b_cmp1[e], w_cmp2[e], pe_cmp[e], lbs[l], hgrn_norm_w[e],
                                    w_out, rel_bias, B=bs, T=ts)
            even_p.append(st_p)
            even_s.append(st_s)
        else:
            o = l // 2
            w_in = _relayout_odd(w_in_odd[o])
            w_out = w_out_odd[o].astype(BF16)
            bif_r = _gate_bias_odd(b_mlstm_if[o])
            hp, st_p = _odd_mix(hp, npre, mkv, 0, mkv, 1, jnp.zeros((bp, ML_HEADS, ML_DV, ML_DK), F32),
                                jnp.zeros((bp, ML_HEADS, ML_DK), F32), jnp.zeros((bp, ML_HEADS), F32),
                                w_in, bif_r, mlstm_norm_w[o], w_out, B=bp, T=tp, L=CHUNK, valid=CHUNK)
            hs, st_s = _odd_mix(hs, nsam, mk_s.reshape(bs * N_MEM, MEM_W), 0, mv_s.reshape(bs * N_MEM, MEM_W), 0,
                                state_mlstm_c[o], state_mlstm_n[o], state_mlstm_m[o],
                                w_in, bif_r, mlstm_norm_w[o], w_out, B=bs, T=tsp, L=tsp, valid=ts)
            odd_p.append(st_p)
            odd_s.append(st_s)
    y_prompt = _rmsnorm_rows(hp, final_norm_w, F32).reshape(bp, tp, D_MODEL)
    y_sample = _rmsnorm_rows(hs, final_norm_w, F32).reshape(bs, tsp, D_MODEL)[:, :ts]
    return (y_prompt, y_sample,
            _stack(mem_new, 0), _stack(mem_new, 1),
            _stack(even_p, 0), _stack(even_p, 1), _stack(even_p, 2), _stack(even_p, 3),
            _stack(even_p, 4), _stack(even_p, 5), _stack(even_p, 6),
            _stack(odd_p, 0), _stack(odd_p, 1), _stack(odd_p, 2),
            _stack(even_s, 0), _stack(even_s, 1), _stack(even_s, 2), _stack(even_s, 3),
            _stack(even_s, 4), _stack(even_s, 5), _stack(even_s, 6),
            _stack(odd_s, 0), _stack(odd_s, 1), _stack(odd_s, 2))
```

```python
import functools
import math

import jax
import jax.numpy as jnp
import numpy as np
from jax import lax
from jax.experimental import pallas as pl
from jax.experimental.pallas import tpu as pltpu

D_MODEL = 4096
DEPTH = 2
PAST_LEN = 16384
PAGE_SIZE = 128
N_MEM = 256
EPS = 1e-6
CHUNK = 64

HG_DK = 128
HG_DV = 128
HG_HEADS = D_MODEL // 2 // HG_DV
HG_W = HG_HEADS * HG_DV

NSA_HD = 128
NSA_HEADS = D_MODEL // 2 // NSA_HD
NSA_KVH = 4
NSA_G = NSA_HEADS // NSA_KVH
NSA_W = NSA_HEADS * NSA_HD
NSA_KV_W = NSA_KVH * NSA_HD
CMP_BLOCK = 32
CMP_STRIDE = 16
SEL_BLOCK = 64
SEL_SHIFT = SEL_BLOCK.bit_length() - 1
N_SEL = 16
WINDOW = 512
Q_BLOCK = 128

ML_HEADS = D_MODEL // 512
ML_DK = D_MODEL // 2 // ML_HEADS
ML_DV = D_MODEL // ML_HEADS
ML_QK_W = ML_HEADS * ML_DK
ML_V_W = ML_HEADS * ML_DV

MEM_HEADS = 4
MEM_HD = 128
MEM_W = MEM_HEADS * MEM_HD

REL_BUCKETS = 32
REL_MAX_DIST = 128

F32 = jnp.float32
BF16 = jnp.bfloat16
LANES = 128
NEG_INF = float("-inf")
TINY = float(np.finfo(np.float32).tiny)
EXP_CLAMP = 80.0
VMEM_LIMIT = 56 * 1024 * 1024

HG_HB = 4
ML_HB = 2
HG_SUB = 16
SAMPLE_PAD_T = 16

EVEN_OFF = {"qa": 0, "fa": HG_W, "ia": 2 * HG_W, "za": 3 * HG_W, "qb": 4 * HG_W, "kvb": 4 * HG_W + NSA_W}
EVEN_OFF["zb"] = EVEN_OFF["kvb"] + 6 * NSA_KV_W
EVEN_OFF["qm"] = EVEN_OFF["zb"] + NSA_W
EVEN_OFF["gb"] = EVEN_OFF["qm"] + MEM_W
EVEN_N = EVEN_OFF["gb"] + NSA_KVH * LANES
ODD_OFF = {"q": 0, "k": ML_QK_W, "v": 2 * ML_QK_W, "og": 2 * ML_QK_W + ML_V_W, "z": 2 * ML_QK_W + 2 * ML_V_W}
ODD_OFF["qm"] = ODD_OFF["z"] + ML_V_W
ODD_OFF["gates"] = ODD_OFF["qm"] + MEM_W
ODD_N = ODD_OFF["gates"] + (ML_HEADS // ML_HB) * LANES


def _dot(a, b):
    return jnp.dot(a, b, preferred_element_type=F32)


def _dot_nt(a, b):
    return lax.dot_general(a, b, (((1,), (1,)), ((), ())), preferred_element_type=F32)


def _dot_tn(a, b):
    return lax.dot_general(a, b, (((0,), (0,)), ((), ())), preferred_element_type=F32)


def _iota2(shape, dim):
    return lax.broadcasted_iota(jnp.int32, shape, dim)


def _cumsum_rows(x, tri_b):
    hi = x.astype(BF16)
    r1 = x - hi.astype(F32)
    mid = r1.astype(BF16)
    lo = (r1 - mid.astype(F32)).astype(BF16)
    return _dot(tri_b, hi) + _dot(tri_b, mid) + _dot(tri_b, lo)


def _row_to_col(row, n):
    eye = _iota2((n, n), 0) == _iota2((n, n), 1)
    return jnp.sum(jnp.where(eye, row, 0.0), axis=1, keepdims=True)


def _col_to_row(col, n):
    eye = _iota2((n, n), 0) == _iota2((n, n), 1)
    return jnp.sum(jnp.where(eye, col, 0.0), axis=0, keepdims=True)


def _silu(x):
    return x * jax.nn.sigmoid(x)


def _params(sem):
    return pltpu.CompilerParams(dimension_semantics=sem, vmem_limit_bytes=VMEM_LIMIT)


def _rmsnorm_body(x_ref, w_ref, o_ref):
    x = x_ref[...].astype(F32)
    y = x * lax.rsqrt(jnp.mean(x * x, axis=-1, keepdims=True) + EPS)
    o_ref[...] = (y * w_ref[...].astype(F32)).astype(o_ref.dtype)


def _rmsnorm_rows(x2d, w, out_dtype, tm=256):
    m, d = x2d.shape
    tm = min(tm, m)
    return pl.pallas_call(
        _rmsnorm_body,
        grid=(m // tm,),
        in_specs=[pl.BlockSpec((tm, d), lambda i: (i, 0)), pl.BlockSpec((1, d), lambda i: (0, 0))],
        out_specs=pl.BlockSpec((tm, d), lambda i: (i, 0)),
        out_shape=jax.ShapeDtypeStruct((m, d), out_dtype),
        compiler_params=_params(("parallel",)),
        name="rmsnorm",
    )(x2d, w.reshape(1, d))


def _matmul_body(a_ref, b_ref, o_ref, acc_ref):
    @pl.when(pl.program_id(2) == 0)
    def _():
        acc_ref[...] = jnp.zeros_like(acc_ref)

    acc_ref[...] += _dot(a_ref[...], b_ref[...])

    @pl.when(pl.program_id(2) == pl.num_programs(2) - 1)
    def _():
        o_ref[...] = acc_ref[...]


def _matmul(a, b, tm=1024, tn=1024, tk=512):
    m, k = a.shape
    _, n = b.shape
    tm, tn, tk = min(tm, m), min(tn, n), min(tk, k)
    assert m % tm == 0 and n % tn == 0 and k % tk == 0, (a.shape, b.shape)
    return pl.pallas_call(
        _matmul_body,
        grid=(m // tm, n // tn, k // tk),
        in_specs=[pl.BlockSpec((tm, tk), lambda i, j, l: (i, l)), pl.BlockSpec((tk, tn), lambda i, j, l: (l, j))],
        out_specs=pl.BlockSpec((tm, tn), lambda i, j, l: (i, j)),
        out_shape=jax.ShapeDtypeStruct((m, n), F32),
        scratch_shapes=[pltpu.VMEM((tm, tn), F32)],
        compiler_params=_params(("parallel", "parallel", "arbitrary")),
        name="matmul",
    )(a, b)


def _outproj_body(*refs, widths):
    xs = refs[:len(widths)]
    w_ref, r_ref, o_ref = refs[len(widths):]
    acc = r_ref[...]
    off = 0
    for x_ref, w in zip(xs, widths):
        acc = acc + _dot(x_ref[...], w_ref[off:off + w, :])
        off += w
    o_ref[...] = acc


def _outproj(xs, w_bf16, resid, tm=1024, tn=512):
    m = resid.shape[0]
    n = w_bf16.shape[1]
    widths = tuple(x.shape[1] for x in xs)
    assert sum(widths) == w_bf16.shape[0]
    tm = min(tm, m)
    in_specs = [pl.BlockSpec((tm, w), lambda i, j: (i, 0)) for w in widths]
    in_specs += [pl.BlockSpec((w_bf16.shape[0], tn), lambda i, j: (0, j)), pl.BlockSpec((tm, tn), lambda i, j: (i, j))]
    return pl.pallas_call(
        functools.partial(_outproj_body, widths=widths),
        grid=(m // tm, n // tn),
        in_specs=in_specs,
        out_specs=pl.BlockSpec((tm, tn), lambda i, j: (i, j)),
        out_shape=jax.ShapeDtypeStruct((m, n), F32),
        compiler_params=_params(("parallel", "parallel")),
        name="outproj",
    )(*xs, w_bf16, resid)


def _hgrn_body(qa_ref, fa_ref, ia_ref, za_ref, lb_ref, gn_ref, s0_ref, o_ref, s_out, s_scr, *, L, valid):
    c = pl.program_id(2)

    @pl.when(c == 0)
    def _():
        s_scr[...] = s0_ref[...]

    lb = lb_ref[...]
    sig = jax.nn.sigmoid(fa_ref[...])
    logf = jnp.log(lb + (1.0 - lb) * sig)
    kk = (1.0 - lb) * (1.0 - sig)
    if valid < L:
        live = _iota2((L, 1), 0) < valid
        logf = jnp.where(live, logf, 0.0)
        kk = jnp.where(live, kk, 0.0)
    tri_b = (_iota2((L, L), 0) >= _iota2((L, L), 1)).astype(BF16)
    bc = _cumsum_rows(logf, tri_b)
    q = _silu(qa_ref[...])
    gate = _silu(za_ref[...])
    v = ia_ref[...]
    gn = gn_ref[...]
    for j in range(HG_HB):
        sl = slice(j * HG_DK, (j + 1) * HG_DK)
        bj, qj, kj = bc[:, sl], q[:, sl], kk[:, sl]
        vb = v[:, sl].astype(BF16)
        s_prev = s_scr[j]
        inter = _dot((qj * jnp.exp(bj)).astype(BF16), s_prev.astype(BF16))
        for i in range(L // HG_SUB):
            r0, r1 = i * HG_SUB, (i + 1) * HG_SUB
            mid = bj[r0 + HG_SUB // 2:r0 + HG_SUB // 2 + 1, :]
            qi = qj[r0:r1] * jnp.exp(jnp.minimum(bj[r0:r1] - mid, EXP_CLAMP))
            ki = kj * jnp.exp(jnp.minimum(mid - bj, EXP_CLAMP))
            att = _dot_nt(qi.astype(BF16), ki.astype(BF16))
            keep = _iota2((HG_SUB, L), 1) <= _iota2((HG_SUB, L), 0) + r0
            att = jnp.where(keep, att, 0.0)
            o_i = inter[r0:r1] + _dot(att.astype(BF16), vb)
            o_n = o_i * lax.rsqrt(jnp.mean(o_i * o_i, axis=-1, keepdims=True) + EPS) * gn
            o_ref[r0:r1, sl] = (o_n * gate[r0:r1, sl]).astype(o_ref.dtype)
        bl = bj[L - 1:L, :]
        kd = kj * jnp.exp(bl - bj)
        s_scr[j] = _row_to_col(jnp.exp(bl), HG_DK) * s_prev + _dot_tn(kd.astype(BF16), vb)

    @pl.when(c == pl.num_programs(2) - 1)
    def _():
        s_out[...] = s_scr[...]


def _hgrn_call(y, s0, lb, gn, *, B, T, L, valid):
    nc = T // L
    w = HG_HB * HG_DK

    def col(name):
        blk = EVEN_OFF[name] // w
        return pl.BlockSpec((L, w), lambda b, hg, c: (b * nc + c, blk + hg))

    state_spec = pl.BlockSpec((None, HG_HB, HG_DK, HG_DV), lambda b, hg, c: (b, hg, 0, 0))
    return pl.pallas_call(
        functools.partial(_hgrn_body, L=L, valid=valid),
        grid=(B, HG_HEADS // HG_HB, nc),
        in_specs=[col("qa"), col("fa"), col("ia"), col("za"),
                  pl.BlockSpec((1, w), lambda b, hg, c: (0, hg)),
                  pl.BlockSpec((1, HG_DV), lambda b, hg, c: (0, 0)),
                  state_spec],
        out_specs=[pl.BlockSpec((L, w), lambda b, hg, c: (b * nc + c, hg)), state_spec],
        out_shape=[jax.ShapeDtypeStruct((B * T, HG_W), BF16),
                   jax.ShapeDtypeStruct((B, HG_HEADS, HG_DK, HG_DV), F32)],
        scratch_shapes=[pltpu.VMEM((HG_HB, HG_DK, HG_DV), F32)],
        compiler_params=_params(("arbitrary", "arbitrary", "arbitrary")),
        name="hgrn2",
    )(y, y, y, y, lb.reshape(1, HG_W), gn.reshape(1, HG_DV), s0)


def _mlstm_body(q_ref, k_ref, v_ref, og_ref, z_ref, g_ref, bif_ref, gn_ref, c0_ref, n0_ref, m0_ref,
                h_ref, c_out, n_out, m_out, c_scr, n_scr, m_scr, *, L, valid):
    c = pl.program_id(2)

    @pl.when(c == 0)
    def _():
        c_scr[...] = c0_ref[...]
        n_scr[...] = n0_ref[...]
        m_scr[...] = m0_ref[...]

    gates = g_ref[...] + bif_ref[...]
    log_i = gates
    log_f = jnp.minimum(gates, 0.0) - jnp.log(1.0 + jnp.exp(-jnp.abs(gates)))
    if valid < L:
        live = _iota2((L, 1), 0) < valid
        log_i = jnp.where(live, log_i, -1e30)
        log_f = jnp.where(live, log_f, 0.0)
    tri = _iota2((L, L), 0) >= _iota2((L, L), 1)
    bcs = _cumsum_rows(log_f, tri.astype(BF16))
    for j in range(ML_HB):
        b_col = bcs[:, ML_HB + j:ML_HB + j + 1]
        i_col = log_i[:, j:j + 1]
        b_row = _col_to_row(b_col, L)
        i_row = _col_to_row(i_col, L)
        m_prev = m_scr[:, j:j + 1]
        dmat = jnp.where(tri, b_col - b_row + i_row, NEG_INF)
        inter = b_col + m_prev
        mt = jnp.maximum(inter, jnp.max(dmat, axis=1, keepdims=True))
        w_in = jnp.exp(dmat - mt)
        w_x = jnp.exp(inter - mt)
        qj = q_ref[:, j * ML_DK:(j + 1) * ML_DK]
        kj = k_ref[:, j * ML_DK:(j + 1) * ML_DK] * (ML_DK ** -0.5)
        vj = v_ref[:, j * ML_DV:(j + 1) * ML_DV]
        qb, kb = qj.astype(BF16), kj.astype(BF16)
        sw = _dot_nt(qb, kb) * w_in
        c_prev = c_scr[j]
        n_prev = n_scr[:, j * ML_DK:(j + 1) * ML_DK]
        num = w_x * _dot_nt(qb, c_prev.astype(BF16)) + _dot(sw.astype(BF16), vj.astype(BF16))
        den = w_x * jnp.sum(qj * n_prev, axis=1, keepdims=True) + jnp.sum(sw, axis=1, keepdims=True)
        h = num / jnp.maximum(jnp.abs(den), jnp.exp(-mt))
        m_last = mt[L - 1:L, :]
        b_last = b_col[L - 1:L, :]
        w_end = jnp.exp(b_last - b_col + i_col - m_last)
        d_c = jnp.exp(b_last + m_prev - m_last)
        c_scr[j] = d_c * c_prev + _dot_tn((w_end * vj).astype(BF16), kb)
        n_scr[:, j * ML_DK:(j + 1) * ML_DK] = d_c * n_prev + jnp.sum(w_end * kj, axis=0, keepdims=True)
        m_scr[:, j:j + 1] = m_last
        sv = slice(j * ML_DV, (j + 1) * ML_DV)
        h_n = h * lax.rsqrt(jnp.mean(h * h, axis=-1, keepdims=True) + EPS) * gn_ref[:, sv]
        h_ref[:, sv] = (h_n * jax.nn.sigmoid(og_ref[:, sv]) * _silu(z_ref[:, sv])).astype(h_ref.dtype)

    @pl.when(c == pl.num_programs(2) - 1)
    def _():
        c_out[...] = c_scr[...]
        n_out[...] = n_scr[...]
        m_out[...] = m_scr[...]


def _mlstm_call(y, c0, n0, m0, bif_r, gn, *, B, T, L, valid):
    nc = T // L
    ng = ML_HEADS // ML_HB
    wk, wv = ML_HB * ML_DK, ML_HB * ML_DV

    def col(name, w):
        blk = ODD_OFF[name] // w
        return pl.BlockSpec((L, w), lambda b, hg, c: (b * nc + c, blk + hg))

    c_spec = pl.BlockSpec((None, ML_HB, ML_DV, ML_DK), lambda b, hg, c: (b, hg, 0, 0))
    n_spec = pl.BlockSpec((None, 1, wk), lambda b, hg, c: (b, 0, hg))
    m_spec = pl.BlockSpec((None, None, 1, LANES), lambda b, hg, c: (b, hg, 0, 0))
    m0_r = jnp.pad(m0.reshape(B, ng, 1, ML_HB), ((0, 0), (0, 0), (0, 0), (0, LANES - ML_HB)))
    h, c_new, n_new, m_new = pl.pallas_call(
        functools.partial(_mlstm_body, L=L, valid=valid),
        grid=(B, ng, nc),
        in_specs=[col("q", wk), col("k", wk), col("v", wv), col("og", wv), col("z", wv), col("gates", LANES),
                  pl.BlockSpec((None, 1, LANES), lambda b, hg, c: (hg, 0, 0)),
                  pl.BlockSpec((1, wv), lambda b, hg, c: (0, hg)),
                  c_spec, n_spec, m_spec],
        out_specs=[pl.BlockSpec((L, wv), lambda b, hg, c: (b * nc + c, hg)), c_spec, n_spec, m_spec],
        out_shape=[jax.ShapeDtypeStruct((B * T, ML_V_W), BF16),
                   jax.ShapeDtypeStruct((B, ML_HEADS, ML_DV, ML_DK), F32),
                   jax.ShapeDtypeStruct((B, 1, ML_QK_W), F32),
                   jax.ShapeDtypeStruct((B, ng, 1, LANES), F32)],
        scratch_shapes=[pltpu.VMEM((ML_HB, ML_DV, ML_DK), F32), pltpu.VMEM((1, wk), F32), pltpu.VMEM((1, LANES), F32)],
        compiler_params=_params(("arbitrary", "arbitrary", "arbitrary")),
        name="mlstm",
    )(y, y, y, y, y, y, bif_r, gn.reshape(1, ML_V_W), c0, n0.reshape(B, 1, ML_QK_W), m0_r)
    return h, c_new, n_new.reshape(B, ML_HEADS, ML_DK), m_new[:, :, 0, :ML_HB].reshape(B, ML_HEADS)


def _mem_body(q_ref, k_ref, v_ref, o_ref):
    q = q_ref[...] * (MEM_HD ** -0.5)
    for h in range(MEM_HEADS):
        sl = slice(h * MEM_HD, (h + 1) * MEM_HD)
        s = _dot_nt(q[:, sl].astype(BF16), k_ref[:, sl].astype(BF16))
        p = jnp.exp(s - jnp.max(s, axis=-1, keepdims=True))
        o = _dot(p.astype(BF16), v_ref[:, sl].astype(BF16)) / jnp.sum(p, axis=-1, keepdims=True)
        o_ref[:, sl] = o.astype(o_ref.dtype)


def _mem_call(y, q_off, k2d, k_blk, v2d, v_blk, *, B, T, tq=256):
    tq = min(tq, T)
    nq = T // tq
    qb = q_off // MEM_W
    return pl.pallas_call(
        _mem_body,
        grid=(B, nq),
        in_specs=[pl.BlockSpec((tq, MEM_W), lambda b, i: (b * nq + i, qb)),
                  pl.BlockSpec((N_MEM, MEM_W), lambda b, i: (b, k_blk)),
                  pl.BlockSpec((N_MEM, MEM_W), lambda b, i: (b, v_blk))],
        out_specs=pl.BlockSpec((tq, MEM_W), lambda b, i: (b * nq + i, 0)),
        out_shape=jax.ShapeDtypeStruct((B * T, MEM_W), BF16),
        compiler_params=_params(("parallel", "parallel")),
        name="mem_attn",
    )(y, k2d, v2d)


def _gelu_tanh(x):
    return 0.5 * x * (1.0 + jnp.tanh(math.sqrt(2.0 / math.pi) * (x + 0.044715 * (x * x * x))))


def _compress_body(x_ref, w1_ref, b1_ref, w2_ref, pe_ref, o_ref, *, nch):
    a = jnp.zeros((nch, NSA_HD), F32)
    b = jnp.zeros((nch, NSA_HD), F32)
    for s in range(CMP_STRIDE):
        r = x_ref[pl.ds(s, nch, stride=CMP_STRIDE), :]
        a = a + _dot((r + pe_ref[s:s + 1, :]).astype(BF16), w1_ref[s])
        b = b + _dot((r + pe_ref[CMP_STRIDE + s:CMP_STRIDE + s + 1, :]).astype(BF16), w1_ref[CMP_STRIDE + s])
    h = a + pltpu.roll(b, nch - 1, 0) + b1_ref[...]
    o_ref[...] = _dot(_gelu_tanh(h).astype(BF16), w2_ref[...])


def _compress_call(y, w1, b1, w2, pe, *, B, T):
    nch = T // CMP_STRIDE
    kv_blk = EVEN_OFF["kvb"] // NSA_HD
    return pl.pallas_call(
        functools.partial(_compress_body, nch=nch),
        grid=(2, B, NSA_KVH),
        in_specs=[pl.BlockSpec((T, NSA_HD), lambda t, b, h: (b, kv_blk + t * NSA_KVH + h)),
                  pl.BlockSpec((None, CMP_BLOCK, NSA_HD, NSA_HD), lambda t, b, h: (t, 0, 0, 0)),
                  pl.BlockSpec((None, 1, NSA_HD), lambda t, b, h: (t, 0, 0)),
                  pl.BlockSpec((None, NSA_HD, NSA_HD), lambda t, b, h: (t, 0, 0)),
                  pl.BlockSpec((None, CMP_BLOCK, NSA_HD), lambda t, b, h: (t, 0, 0))],
        out_specs=pl.BlockSpec((None, None, None, nch, NSA_HD), lambda t, b, h: (t, b, h, 0, 0)),
        out_shape=jax.ShapeDtypeStruct((2, B, NSA_KVH, nch, NSA_HD), F32),
        compiler_params=_params(("parallel", "parallel", "parallel")),
        name="nsa_compress",
    )(y, w1.astype(BF16), b1.reshape(2, 1, NSA_HD), w2.astype(BF16), pe)


def _softmax_rows(s):
    m = jnp.max(s, axis=-1, keepdims=True)
    m = jnp.where(m == NEG_INF, 0.0, m)
    p = jnp.exp(s - m)
    return p, jnp.sum(p, axis=-1, keepdims=True)


def _slc_scores(psum, width, n_slc):
    ncmp = psum.shape[1]
    d = _iota2((ncmp, width), 0) - (SEL_BLOCK // CMP_STRIDE) * _iota2((ncmp, width), 1)
    wgt = jnp.where((d == -1) | (d == 3), 1.0, jnp.where((d >= 0) & (d <= 2), 2.0, 0.0))
    wgt = jnp.where(_iota2((ncmp, width), 1) < n_slc, wgt, 0.0).astype(BF16)
    p_hi = psum.astype(BF16)
    p_lo = (psum - p_hi.astype(F32)).astype(BF16)
    return _dot(p_hi, wgt) + _dot(p_lo, wgt)


def _top_blocks(slc, cur, n_pick):
    rows, width = slc.shape
    blk = _iota2((rows, width), 1)
    forced = (blk == 0) | (blk == cur) | (blk == cur - 1)
    score = jnp.where(forced, jnp.inf, slc)
    score = jnp.where(blk > cur, NEG_INF, score)
    blk_f = blk.astype(F32)
    lane = _iota2((rows, LANES), 1)
    sel = jnp.zeros((rows, width), F32)
    picks = jnp.zeros((rows, LANES), F32)
    for i in range(n_pick):
        mx = jnp.max(score, axis=-1, keepdims=True)
        first = jnp.min(jnp.where(score == mx, blk_f, float(width)), axis=-1, keepdims=True)
        pick = blk_f == first
        sel = jnp.where(pick, 1.0, sel)
        picks = jnp.where(lane == i, first, picks)
        score = jnp.where(pick, NEG_INF, score)
    return sel, picks


def _nsa_prompt_body(q_ref, zb_ref, gb_ref, bg_ref, ks_ref, vs_ref, kw_ref, vw_ref, kc_ref, vc_ref,
                     bc_ref, bs_ref, bw_ref, o_ref, ksp, vsp, kwp, vwp, *, T):
    qi = pl.program_id(2)
    tq = Q_BLOCK
    front = T - tq
    wlen = WINDOW + tq
    n_slc = T // SEL_BLOCK

    @pl.when(qi == 0)
    def _():
        ksp[0:front, :] = jnp.zeros((front, NSA_HD), BF16)
        vsp[0:front, :] = jnp.zeros((front, NSA_HD), BF16)
        ksp[front:front + T, :] = ks_ref[...].astype(BF16)
        vsp[front:front + T, :] = vs_ref[...].astype(BF16)
        kwp[0:WINDOW, :] = jnp.zeros((WINDOW, NSA_HD), BF16)
        vwp[0:WINDOW, :] = jnp.zeros((WINDOW, NSA_HD), BF16)
        kwp[WINDOW:WINDOW + T, :] = kw_ref[...].astype(BF16)
        vwp[WINDOW:WINDOW + T, :] = vw_ref[...].astype(BF16)

    t0 = pl.multiple_of(qi * tq, tq)
    tpos = _iota2((tq, 1), 0) + t0
    q_all = q_ref[...] * (NSA_HD ** -0.5)
    qs = [q_all[:, g * NSA_HD:(g + 1) * NSA_HD].astype(BF16) for g in range(NSA_G)]

    ncmp = T // CMP_STRIDE
    vis = tpos >= _iota2((1, ncmp), 1) * CMP_STRIDE + (CMP_BLOCK - 1)
    kcb = kc_ref[...].astype(BF16)
    vcb = vc_ref[...].astype(BF16)
    psum = jnp.zeros((tq, ncmp), F32)
    o_cmp = []
    for g in range(NSA_G):
        s = jnp.where(vis, _dot_nt(qs[g], kcb) + bc_ref[g], NEG_INF)
        p, l = _softmax_rows(s)
        p = p / jnp.maximum(l, TINY)
        psum = psum + p
        o_cmp.append(_dot(p.astype(BF16), vcb))

    blk = _iota2((tq, LANES), 1)
    cur = jnp.right_shift(tpos, SEL_SHIFT)
    sel, _ = _top_blocks(_slc_scores(psum, LANES, n_slc), cur, min(N_SEL, n_slc))
    member = jnp.where(blk <= cur, sel, 0.0).astype(BF16)

    col_blk = jnp.right_shift(_iota2((LANES, T), 1), SEL_SHIFT) + (qi * (tq // SEL_BLOCK) + (tq - T) // SEL_BLOCK)
    expand = (col_blk == _iota2((LANES, T), 0)).astype(BF16)
    kpos = _iota2((1, T), 1) + (t0 + tq - T)
    allowed = (_dot(member, expand) > 0.5) & (kpos <= tpos)
    mask_s = jnp.where(allowed, 0.0, NEG_INF)
    k_s = ksp[pl.ds(t0, T), :]
    v_s = vsp[pl.ds(t0, T), :]
    o_sel = []
    for g in range(NSA_G):
        p, l = _softmax_rows(_dot_nt(qs[g], k_s) + bs_ref[g] + mask_s)
        o_sel.append(_dot(p.astype(BF16), v_s) / jnp.maximum(l, TINY))

    dist = WINDOW + _iota2((tq, wlen), 0) - _iota2((tq, wlen), 1)
    in_win = (dist >= 0) & (dist < WINDOW) & (_iota2((1, wlen), 1) + (t0 - WINDOW) >= 0)
    mask_w = jnp.where(in_win, 0.0, NEG_INF)
    k_w = kwp[pl.ds(t0, wlen), :]
    v_w = vwp[pl.ds(t0, wlen), :]
    gate = jax.nn.sigmoid(gb_ref[...] + bg_ref[...])
    zb = _silu(zb_ref[...])
    for g in range(NSA_G):
        p, l = _softmax_rows(_dot_nt(qs[g], k_w) + bw_ref[g] + mask_w)
        o_win = _dot(p.astype(BF16), v_w) / jnp.maximum(l, TINY)
        mix = (gate[:, g:g + 1] * o_cmp[g] + gate[:, NSA_G + g:NSA_G + g + 1] * o_sel[g]
               + gate[:, 2 * NSA_G + g:2 * NSA_G + g + 1] * o_win)
        sl = slice(g * NSA_HD, (g + 1) * NSA_HD)
        o_ref[:, sl] = (mix * zb[:, sl]).astype(o_ref.dtype)


def _nsa_prompt_call(y, kvcmp, bg_r, bias_c, bias_s, bias_w, *, B, T):
    nq = T // Q_BLOCK
    gw = NSA_G * NSA_HD
    kv_blk = EVEN_OFF["kvb"] // NSA_HD
    wlen = WINDOW + Q_BLOCK

    def kv_spec(j):
        return pl.BlockSpec((T, NSA_HD), lambda b, h, i: (b, kv_blk + j * NSA_KVH + h))

    def cmp_spec(t):
        return pl.BlockSpec((None, None, None, T // CMP_STRIDE, NSA_HD), lambda b, h, i: (t, b, h, 0, 0))

    return pl.pallas_call(
        functools.partial(_nsa_prompt_body, T=T),
        grid=(B, NSA_KVH, nq),
        in_specs=[pl.BlockSpec((Q_BLOCK, gw), lambda b, h, i: (b * nq + i, EVEN_OFF["qb"] // gw + h)),
                  pl.BlockSpec((Q_BLOCK, gw), lambda b, h, i: (b * nq + i, EVEN_OFF["zb"] // gw + h)),
                  pl.BlockSpec((Q_BLOCK, LANES), lambda b, h, i: (b * nq + i, EVEN_OFF["gb"] // LANES + h)),
                  pl.BlockSpec((None, 1, LANES), lambda b, h, i: (h, 0, 0)),
                  kv_spec(2), kv_spec(3), kv_spec(4), kv_spec(5), cmp_spec(0), cmp_spec(1),
                  pl.BlockSpec((None, NSA_G, Q_BLOCK, T // CMP_STRIDE), lambda b, h, i: (h, 0, i, 0)),
                  pl.BlockSpec((None, NSA_G, Q_BLOCK, T), lambda b, h, i: (h, 0, 0, 0)),
                  pl.BlockSpec((None, NSA_G, Q_BLOCK, wlen), lambda b, h, i: (h, 0, 0, 0))],
        out_specs=pl.BlockSpec((Q_BLOCK, gw), lambda b, h, i: (b * nq + i, h)),
        out_shape=jax.ShapeDtypeStruct((B * T, NSA_W), BF16),
        scratch_shapes=[pltpu.VMEM((2 * T - Q_BLOCK, NSA_HD), BF16), pltpu.VMEM((2 * T - Q_BLOCK, NSA_HD), BF16),
                        pltpu.VMEM((WINDOW + T, NSA_HD), BF16), pltpu.VMEM((WINDOW + T, NSA_HD), BF16)],
        compiler_params=_params(("arbitrary", "arbitrary", "arbitrary")),
        name="nsa_prompt",
    )(y, y, y, bg_r, y, y, y, y, kvcmp, kvcmp, bias_c, bias_s, bias_w)


CMP_PAGES = 16
CHUNKS_PER_PAGE = PAGE_SIZE // CMP_STRIDE
CHUNK_W = CMP_STRIDE * NSA_KV_W


def _cmp_pages_body(pt_ref, *refs):
    del pt_ref
    pages = refs[:CMP_PAGES]
    w_ref, pe_ref, o_ref = refs[CMP_PAGES:]
    x = jnp.concatenate([r[...] for r in pages], axis=0)
    rows = x.shape[0]
    per_head = [jnp.concatenate([x[:, s * NSA_KV_W + h * NSA_HD:s * NSA_KV_W + (h + 1) * NSA_HD]
                                 for s in range(CMP_STRIDE)], axis=1) for h in range(NSA_KVH)]
    w = w_ref[...]
    r = _dot(jnp.concatenate(per_head, axis=0).astype(BF16), w)
    pc = _dot(pe_ref[...], w)
    r = r + jnp.concatenate([pc[0:1, :NSA_HD], pc[1:2, NSA_HD:]], axis=1)
    for h in range(NSA_KVH):
        o_ref[h] = r[h * rows:(h + 1) * rows]


def _cmp_pages_call(pool, page_table, w1, pe, *, B):
    n_pages = page_table.shape[1]
    rows = CMP_PAGES * CHUNKS_PER_PAGE
    view = pool.reshape(pool.shape[0] * CHUNKS_PER_PAGE, CHUNK_W)
    w = w1.reshape(2, CMP_STRIDE, NSA_HD, NSA_HD).transpose(1, 2, 0, 3).reshape(CMP_STRIDE * NSA_HD, 2 * NSA_HD)
    pe_rows = jnp.pad(pe.reshape(2, CMP_STRIDE * NSA_HD), ((0, 6), (0, 0))).astype(BF16)

    def page_spec(i):
        return pl.BlockSpec((CHUNKS_PER_PAGE, CHUNK_W), lambda b, s, pt: (pt[b * n_pages + s * CMP_PAGES + i], 0))

    grid_spec = pltpu.PrefetchScalarGridSpec(
        num_scalar_prefetch=1,
        grid=(B, n_pages // CMP_PAGES),
        in_specs=[page_spec(i) for i in range(CMP_PAGES)]
        + [pl.BlockSpec((CMP_STRIDE * NSA_HD, 2 * NSA_HD), lambda b, s, pt: (0, 0)),
           pl.BlockSpec((8, CMP_STRIDE * NSA_HD), lambda b, s, pt: (0, 0))],
        out_specs=pl.BlockSpec((None, NSA_KVH, rows, 2 * NSA_HD), lambda b, s, pt: (b, 0, s, 0)),
    )
    return pl.pallas_call(
        _cmp_pages_body,
        grid_spec=grid_spec,
        out_shape=jax.ShapeDtypeStruct((B, NSA_KVH, n_pages * CHUNKS_PER_PAGE, 2 * NSA_HD), F32),
        compiler_params=_params(("arbitrary", "arbitrary")),
        name="nsa_cmp_pages",
    )(page_table.reshape(-1), *([view] * CMP_PAGES), w.astype(BF16), pe_rows)


SLC_LANES = 384


def _sample_q_rows(q_ref):
    q = q_ref[...] * (NSA_HD ** -0.5)
    return jnp.concatenate([q[:, g * NSA_HD:(g + 1) * NSA_HD] for g in range(NSA_G)], axis=0).astype(BF16)


def _nsa_sample_main_body(abk_ref, abv_ref, b1_ref, w2_ref, q_ref, wk_ref, wv_ref, kn_ref, vn_ref, bc_ref, bw_ref,
                          ocmp_ref, owin_ref, idx_ref, *, T, n_slc):
    tp = SAMPLE_PAD_T
    rows = NSA_G * tp
    ncmp = abk_ref.shape[0]

    def compressed(ab_ref, t):
        ab = ab_ref[...]
        h = ab[:, :NSA_HD] + pltpu.roll(ab[:, NSA_HD:], ncmp - 1, 0) + b1_ref[t]
        return _dot(_gelu_tanh(h).astype(BF16), w2_ref[t]).astype(BF16)

    kc, vc = compressed(abk_ref, 0), compressed(abv_ref, 1)
    q = _sample_q_rows(q_ref)
    step = jnp.bitwise_and(_iota2((rows, 1), 0), tp - 1)
    tpos = PAST_LEN + step
    vis = tpos >= _iota2((1, ncmp), 1) * CMP_STRIDE + (CMP_BLOCK - 1)
    p, l = _softmax_rows(jnp.where(vis, _dot_nt(q, kc) + bc_ref[...], NEG_INF))
    p = p / jnp.maximum(l, TINY)
    ocmp_ref[...] = _dot(p.astype(BF16), vc)
    psum = p[0:tp]
    for g in range(1, NSA_G):
        psum = psum + p[g * tp:(g + 1) * tp]
    cur = jnp.right_shift(PAST_LEN + _iota2((tp, 1), 0), SEL_SHIFT)
    _, picks = _top_blocks(_slc_scores(psum, SLC_LANES, n_slc), cur, N_SEL)
    idx_ref[...] = picks.astype(jnp.int32)

    wb = wk_ref.shape[0]
    wlen = bw_ref.shape[1]
    fill = jnp.zeros((wlen - wb - tp, NSA_HD), F32)
    k_all = jnp.concatenate([wk_ref[...], kn_ref[...], fill], axis=0).astype(BF16)
    v_all = jnp.concatenate([wv_ref[...], vn_ref[...], fill], axis=0).astype(BF16)
    col = _iota2((1, wlen), 1)
    dist = tpos - (PAST_LEN - wb + col)
    in_win = (dist >= 0) & (dist < WINDOW) & (col < wb + T)
    pw, lw = _softmax_rows(jnp.where(in_win, _dot_nt(q, k_all) + bw_ref[...], NEG_INF))
    owin_ref[...] = _dot(pw.astype(BF16), v_all) / jnp.maximum(lw, TINY)


def _nsa_sample_main_call(ys, abk, abv, b1, w2, wk, wv, bias_c, bias_w, *, B, T):
    tp = SAMPLE_PAD_T
    rows = NSA_G * tp
    gw = NSA_G * NSA_HD
    ncmp = abk.shape[2]
    wb = wk.shape[1]
    wlen = bias_w.shape[-1]
    kv_blk = EVEN_OFF["kvb"] // NSA_HD
    n_slc = -(-(PAST_LEN + T) // SEL_BLOCK)
    assert n_slc <= SLC_LANES and T <= tp
    ab_spec = pl.BlockSpec((None, None, ncmp, 2 * NSA_HD), lambda b, h: (b, h, 0, 0))
    win_spec = pl.BlockSpec((wb, NSA_HD), lambda b, h: (b, h))
    o_spec = pl.BlockSpec((None, None, rows, NSA_HD), lambda b, h: (b, h, 0, 0))
    return pl.pallas_call(
        functools.partial(_nsa_sample_main_body, T=T, n_slc=n_slc),
        grid=(B, NSA_KVH),
        in_specs=[ab_spec, ab_spec,
                  pl.BlockSpec((2, 1, NSA_HD), lambda b, h: (0, 0, 0)),
                  pl.BlockSpec((2, NSA_HD, NSA_HD), lambda b, h: (0, 0, 0)),
                  pl.BlockSpec((tp, gw), lambda b, h: (b, EVEN_OFF["qb"] // gw + h)),
                  win_spec, win_spec,
                  pl.BlockSpec((tp, NSA_HD), lambda b, h: (b, kv_blk + 4 * NSA_KVH + h)),
                  pl.BlockSpec((tp, NSA_HD), lambda b, h: (b, kv_blk + 5 * NSA_KVH + h)),
                  pl.BlockSpec((None, rows, ncmp), lambda b, h: (h, 0, 0)),
                  pl.BlockSpec((None, rows, wlen), lambda b, h: (h, 0, 0))],
        out_specs=[o_spec, o_spec, pl.BlockSpec((None, None, tp, LANES), lambda b, h: (b, h, 0, 0))],
        out_shape=[jax.ShapeDtypeStruct((B, NSA_KVH, rows, NSA_HD), F32),
                   jax.ShapeDtypeStruct((B, NSA_KVH, rows, NSA_HD), F32),
                   jax.ShapeDtypeStruct((B, NSA_KVH, tp, LANES), jnp.int32)],
        compiler_params=_params(("parallel", "parallel")),
        name="nsa_sample_main",
    )(abk, abv, b1.reshape(2, 1, NSA_HD), w2.astype(BF16), ys,
      wk.reshape(B * wb, NSA_KV_W), wv.reshape(B * wb, NSA_KV_W), ys, ys, bias_c, bias_w)


NEAR_BLOCKS = 3


def _nsa_sample_sel_body(idx_ref, pt_ref, q_ref, kn_ref, vn_ref, tbl_ref, ocmp_ref, owin_ref, gb_ref, bg_ref, zb_ref,
                         *refs, T):
    del pt_ref
    k_blocks = refs[:N_SEL]
    v_blocks = refs[N_SEL:2 * N_SEL]
    o_ref, osel = refs[2 * N_SEL:]
    tp = SAMPLE_PAD_T
    rows = NSA_G * tp
    b, h, t = pl.program_id(0), pl.program_id(1), pl.program_id(2)
    base = ((b * NSA_KVH + h) * T + t) * N_SEL
    first_new = PAST_LEN // SEL_BLOCK
    cur = jnp.right_shift(PAST_LEN + t, SEL_SHIFT)
    q = _sample_q_rows(q_ref)
    pad = jnp.zeros((SEL_BLOCK - tp, NSA_HD), F32)
    k_new = jnp.concatenate([kn_ref[...], pad], axis=0)
    v_new = jnp.concatenate([vn_ref[...], pad], axis=0)
    lane = _iota2((1, LANES), 1)
    low = lane < SEL_BLOCK
    within = jnp.bitwise_and(lane, SEL_BLOCK - 1)
    ks, vs, bias, kpos = [], [], [], []
    for i in range(0, N_SEL, 2):
        pair_bias, pair_pos = [], []
        for j in (i, i + 1):
            blk = idx_ref[base + j]
            is_new = blk >= first_new
            ks.append(jnp.where(is_new, k_new, k_blocks[j][...]))
            vs.append(jnp.where(is_new, v_new, v_blocks[j][...]))
            pair_bias.append(tbl_ref[jnp.clip(blk - (first_new - NEAR_BLOCKS), 0, NEAR_BLOCKS)])
            pair_pos.append(jnp.where(blk <= cur, blk * SEL_BLOCK, PAST_LEN + SEL_BLOCK * LANES) + within)
        bias.append(jnp.where(low, pair_bias[0], pair_bias[1]))
        kpos.append(jnp.where(low, pair_pos[0], pair_pos[1]))
    k_all = jnp.concatenate(ks, axis=0).astype(BF16)
    v_all = jnp.concatenate(vs, axis=0).astype(BF16)
    step = jnp.bitwise_and(_iota2((rows, 1), 0), tp - 1)
    ok = jnp.concatenate(kpos, axis=1) <= PAST_LEN + step
    p, l = _softmax_rows(jnp.where(ok, _dot_nt(q, k_all) + jnp.concatenate(bias, axis=1), NEG_INF))
    o = _dot(p.astype(BF16), v_all) / jnp.maximum(l, TINY)

    @pl.when(t == 0)
    def _():
        osel[...] = jnp.zeros_like(osel)

    osel[...] = jnp.where(step == t, o, osel[...])

    @pl.when(t == T - 1)
    def _():
        gate = jax.nn.sigmoid(gb_ref[...] + bg_ref[...])
        zb = _silu(zb_ref[...])
        for g in range(NSA_G):
            r = slice(g * tp, (g + 1) * tp)
            mix = (gate[:, g:g + 1] * ocmp_ref[r, :] + gate[:, NSA_G + g:NSA_G + g + 1] * osel[r, :]
                   + gate[:, 2 * NSA_G + g:2 * NSA_G + g + 1] * owin_ref[r, :])
            sl = slice(g * NSA_HD, (g + 1) * NSA_HD)
            o_ref[:, sl] = (mix * zb[:, sl]).astype(o_ref.dtype)


def _nsa_sample_sel_call(ys, idx, page_table, pool_k, pool_v, tbl, o_cmp, o_win, bg_r, *, B, T):
    tp = SAMPLE_PAD_T
    rows = NSA_G * tp
    gw = NSA_G * NSA_HD
    n_pages = page_table.shape[1]
    halves = PAGE_SIZE // SEL_BLOCK
    kv_blk = EVEN_OFF["kvb"] // NSA_HD
    idx_flat = idx[:, :, :T, :N_SEL].reshape(-1)
    view_k = pool_k.reshape(pool_k.shape[0] * halves, SEL_BLOCK, NSA_KV_W)
    view_v = pool_v.reshape(pool_v.shape[0] * halves, SEL_BLOCK, NSA_KV_W)

    def blk_spec(j):
        def index(b, h, t, idx_s, pt_s):
            blk = idx_s[((b * NSA_KVH + h) * T + t) * N_SEL + j]
            page = pt_s[b * n_pages + jnp.minimum(blk // halves, n_pages - 1)]
            return (page * halves + blk % halves, 0, h)
        return pl.BlockSpec((None, SEL_BLOCK, NSA_HD), index)

    o_spec = pl.BlockSpec((None, None, rows, NSA_HD), lambda b, h, t, *_: (b, h, 0, 0))
    grid_spec = pltpu.PrefetchScalarGridSpec(
        num_scalar_prefetch=2,
        grid=(B, NSA_KVH, T),
        in_specs=[pl.BlockSpec((tp, gw), lambda b, h, t, *_: (b, EVEN_OFF["qb"] // gw + h)),
                  pl.BlockSpec((tp, NSA_HD), lambda b, h, t, *_: (b, kv_blk + 2 * NSA_KVH + h)),
                  pl.BlockSpec((tp, NSA_HD), lambda b, h, t, *_: (b, kv_blk + 3 * NSA_KVH + h)),
                  pl.BlockSpec((None, NEAR_BLOCKS + 1, rows, LANES), lambda b, h, t, *_: (h, 0, 0, 0)),
                  o_spec, o_spec,
                  pl.BlockSpec((tp, LANES), lambda b, h, t, *_: (b, EVEN_OFF["gb"] // LANES + h)),
                  pl.BlockSpec((None, 1, LANES), lambda b, h, t, *_: (h, 0, 0)),
                  pl.BlockSpec((tp, gw), lambda b, h, t, *_: (b, EVEN_OFF["zb"] // gw + h))]
        + [blk_spec(j) for j in range(N_SEL)] * 2,
        out_specs=pl.BlockSpec((tp, gw), lambda b, h, t, *_: (b, h)),
        scratch_shapes=[pltpu.VMEM((rows, NSA_HD), F32)],
    )
    return pl.pallas_call(
        functools.partial(_nsa_sample_sel_body, T=T),
        grid_spec=grid_spec,
        out_shape=jax.ShapeDtypeStruct((B * tp, NSA_W), BF16),
        compiler_params=_params(("arbitrary", "arbitrary", "arbitrary")),
        name="nsa_sample_sel",
    )(idx_flat, page_table.reshape(-1), ys, ys, ys, tbl, o_cmp, o_win, ys, bg_r, ys,
      *([view_k] * N_SEL), *([view_v] * N_SEL))


def _sample_bias_tables(rel_bias, T, wb):
    tp = SAMPLE_PAD_T
    step = np.arange(tp)[:, None]
    qpos = PAST_LEN + step
    ncmp = PAST_LEN // CMP_STRIDE
    wlen = -(-(wb + tp) // LANES) * LANES
    d_c = qpos - (np.arange(ncmp)[None, :] * CMP_STRIDE + CMP_BLOCK - 1)
    d_w = qpos - (PAST_LEN - wb + np.arange(wlen)[None, :])
    first = PAST_LEN // SEL_BLOCK - NEAR_BLOCKS
    within = np.arange(LANES)[None, :] % SEL_BLOCK
    d_s = np.stack([np.full((tp, LANES), REL_MAX_DIST)]
                   + [qpos - ((first + k) * SEL_BLOCK + within) for k in range(1, NEAR_BLOCKS + 1)], axis=1)
    assert PAST_LEN - ((first + 1) * SEL_BLOCK - 1) >= REL_MAX_DIST

    def rows(tbl):
        return tbl.reshape((NSA_KVH, NSA_G * tp) + tbl.shape[3:])

    t_s = _bias_table(rel_bias, d_s.reshape(tp, -1)).reshape(NSA_KVH, NSA_G, tp, NEAR_BLOCKS + 1, LANES)
    t_s = t_s.transpose(0, 3, 1, 2, 4).reshape(NSA_KVH, NEAR_BLOCKS + 1, NSA_G * tp, LANES)
    return rows(_bias_table(rel_bias, d_c)), rows(_bias_table(rel_bias, d_w)), t_s


def _relayout_even(w):
    src = {}
    off = 0
    for name, width in (("qa", HG_W), ("fa", HG_W), ("ia", HG_W), ("za", HG_W), ("qb", NSA_W), ("kvb", 6 * NSA_KV_W),
                        ("gb", 3 * NSA_HEADS), ("zb", NSA_W), ("qm", MEM_W)):
        src[name] = (off, width)
        off += width
    cols = [w[:, src[n][0]:src[n][0] + src[n][1]] for n in ("qa", "fa", "ia", "za", "qb", "kvb", "zb", "qm")]
    g0 = src["gb"][0]
    for h in range(NSA_KVH):
        for j in range(3):
            cols.append(w[:, g0 + j * NSA_HEADS + h * NSA_G:g0 + j * NSA_HEADS + (h + 1) * NSA_G])
        cols.append(jnp.zeros((w.shape[0], LANES - 3 * NSA_G), w.dtype))
    return jnp.concatenate(cols, axis=1).astype(BF16)


def _relayout_odd(w):
    src = {}
    off = 0
    for name, width in (("q", ML_QK_W), ("k", ML_QK_W), ("v", ML_V_W), ("og", ML_V_W), ("ig", ML_HEADS),
                        ("fg", ML_HEADS), ("z", ML_V_W), ("qm", MEM_W)):
        src[name] = (off, width)
        off += width
    cols = [w[:, src[n][0]:src[n][0] + src[n][1]] for n in ("q", "k", "v", "og", "z", "qm")]
    for hg in range(ML_HEADS // ML_HB):
        for n in ("ig", "fg"):
            cols.append(w[:, src[n][0] + hg * ML_HB:src[n][0] + (hg + 1) * ML_HB])
        cols.append(jnp.zeros((w.shape[0], LANES - 2 * ML_HB), w.dtype))
    return jnp.concatenate(cols, axis=1).astype(BF16)


def _gate_bias_even(b_gate):
    g = b_gate.reshape(3, NSA_KVH, NSA_G).transpose(1, 0, 2).reshape(NSA_KVH, 1, 3 * NSA_G)
    return jnp.pad(g, ((0, 0), (0, 0), (0, LANES - 3 * NSA_G)))


def _gate_bias_odd(b_if):
    g = b_if.reshape(2, ML_HEADS // ML_HB, ML_HB).transpose(1, 0, 2).reshape(ML_HEADS // ML_HB, 1, 2 * ML_HB)
    return jnp.pad(g, ((0, 0), (0, 0), (0, LANES - 2 * ML_HB)))


def _rel_bucket(dist):
    n = jnp.maximum(dist, 0)
    exact = REL_BUCKETS // 2
    nf = jnp.maximum(n, 1).astype(F32)
    large = exact + (jnp.log(nf / exact) / math.log(REL_MAX_DIST / exact) * (REL_BUCKETS - exact)).astype(jnp.int32)
    return jnp.where(n < exact, n, jnp.minimum(large, REL_BUCKETS - 1))


def _bias_table(rel_bias, dist):
    b = rel_bias[_rel_bucket(jnp.asarray(dist, jnp.int32))].astype(F32)
    return jnp.moveaxis(b, -1, 0).reshape((NSA_KVH, NSA_G) + dist.shape)


def _prompt_bias_tables(rel_bias, T):
    r = np.arange(Q_BLOCK)[:, None]
    d_c = np.arange(T)[:, None] - (np.arange(T // CMP_STRIDE)[None, :] * CMP_STRIDE + CMP_BLOCK - 1)
    d_s = r - np.arange(T)[None, :] + (T - Q_BLOCK)
    d_w = r - np.arange(WINDOW + Q_BLOCK)[None, :] + WINDOW
    return _bias_table(rel_bias, d_c), _bias_table(rel_bias, d_s), _bias_table(rel_bias, d_w)


def _nsa_sample(ys, page_table, pk_cmp, pv_cmp, pk_sel, pv_sel, wk, wv, bg_r, w1, b1, w2, pe, rel_bias, *, B, T):
    wb = wk.shape[1]
    assert (PAST_LEN + T) // CMP_STRIDE == PAST_LEN // CMP_STRIDE
    abk = _cmp_pages_call(pk_cmp, page_table, w1[0], pe[0], B=B)
    abv = _cmp_pages_call(pv_cmp, page_table, w1[1], pe[1], B=B)
    bias_c, bias_w, tbl = _sample_bias_tables(rel_bias, T, wb)
    o_cmp, o_win, idx = _nsa_sample_main_call(ys, abk, abv, b1, w2, wk, wv, bias_c, bias_w, B=B, T=T)
    ob = _nsa_sample_sel_call(ys, idx, page_table, pk_sel, pv_sel, tbl, o_cmp, o_win, bg_r, B=B, T=T)
    y3 = ys.reshape(B, SAMPLE_PAD_T, EVEN_N)[:, :T]
    kv = y3[..., EVEN_OFF["kvb"]:EVEN_OFF["kvb"] + 6 * NSA_KV_W].reshape(B, T, 6, NSA_KVH, NSA_HD)
    kc, vc, ks, vs, kw, vw = [kv[:, :, j] for j in range(6)]
    return ob, (kc, vc, ks, vs, jnp.concatenate([wk, kw], axis=1)[:, -wb:], jnp.concatenate([wv, vw], axis=1)[:, -wb:])


def _kv_rows(y, B, T, j):
    off = EVEN_OFF["kvb"] + j * NSA_KV_W
    return y[:, off:off + NSA_KV_W].reshape(B, T, NSA_KVH, NSA_HD)


def _even_prompt(hp2d, npre, mkv_p, w_in, bg_r, w1, b1, w2, pe, lb, g_norm, w_out, rel_bias, *, B, T):
    y = _matmul(npre, w_in)
    oa, s_new = _hgrn_call(y, jnp.zeros((B, HG_HEADS, HG_DK, HG_DV), F32), lb, g_norm, B=B, T=T, L=CHUNK, valid=CHUNK)
    kvcmp = _compress_call(y, w1, b1, w2, pe, B=B, T=T)
    ob = _nsa_prompt_call(y, kvcmp, bg_r, *_prompt_bias_tables(rel_bias, T), B=B, T=T)
    om = _mem_call(y, EVEN_OFF["qm"], mkv_p, 0, mkv_p, 1, B=B, T=T)
    h_new = _outproj([oa, ob, om], w_out, hp2d)
    wb = min(WINDOW, T)
    kw, vw = _kv_rows(y, B, T, 4), _kv_rows(y, B, T, 5)
    return h_new, (_kv_rows(y, B, T, 0), _kv_rows(y, B, T, 1), _kv_rows(y, B, T, 2), _kv_rows(y, B, T, 3),
                   kw[:, -wb:], vw[:, -wb:], s_new)


def _even_sample(hs2d, nsam, mk_s, mv_s, page_table, pk_cmp, pv_cmp, pk_sel, pv_sel, wk, wv, s0,
                 w_in, bg_r, w1, b1, w2, pe, lb, g_norm, w_out, rel_bias, *, B, T):
    tp = SAMPLE_PAD_T
    y = _matmul(nsam, w_in)
    oa, s_new = _hgrn_call(y, s0, lb, g_norm, B=B, T=tp, L=tp, valid=T)
    ob, caches = _nsa_sample(y, page_table, pk_cmp, pv_cmp, pk_sel, pv_sel, wk, wv, bg_r, w1, b1, w2, pe, rel_bias,
                             B=B, T=T)
    om = _mem_call(y, EVEN_OFF["qm"], mk_s.reshape(B * N_MEM, MEM_W), 0, mv_s.reshape(B * N_MEM, MEM_W), 0, B=B, T=tp)
    return _outproj([oa, ob, om], w_out, hs2d), caches + (s_new,)


def _odd_mix(h2d, hn, k2d, k_blk, v2d, v_blk, c0, n0, m0, w_in, bif_r, g_norm, w_out, *, B, T, L, valid):
    y = _matmul(hn, w_in)
    h, c_new, n_new, m_new = _mlstm_call(y, c0, n0, m0, bif_r, g_norm, B=B, T=T, L=L, valid=valid)
    om = _mem_call(y, ODD_OFF["qm"], k2d, k_blk, v2d, v_blk, B=B, T=T)
    return _outproj([h, om], w_out, h2d), (c_new, n_new, m_new)


def _stack(lst, i):
    return jnp.stack([t[i] for t in lst])


def kernel(x_prompt, x_sample, cache_mem_k, cache_mem_v, cache_cmp_k, cache_cmp_v, cache_sel_k, cache_sel_v,
           cache_win_k, cache_win_v, state_hgrn, state_mlstm_c, state_mlstm_n, state_mlstm_m, page_table,
           mem_prompt, norm_w, mem_norm_w, final_norm_w, rel_bias, w_mem_kv, w_in_even, b_nsa_gate,
           w_cmp1, b_cmp1, w_cmp2, pe_cmp, hgrn_lb_logits, hgrn_norm_w, w_out_even, w_in_odd, b_mlstm_if,
           mlstm_norm_w, w_out_odd):
    bp, tp = x_prompt.shape[:2]
    bs, ts = x_sample.shape[:2]
    tsp = SAMPLE_PAD_T
    lbs = jnp.cumsum(jax.nn.softmax(hgrn_lb_logits.astype(F32), axis=0), axis=0)
    hp = x_prompt.reshape(bp * tp, D_MODEL)
    hs = jnp.pad(x_sample, ((0, 0), (0, tsp - ts), (0, 0))).reshape(bs * tsp, D_MODEL)
    mem2d = mem_prompt.reshape(bp * N_MEM, D_MODEL)
    mem_new, even_p, even_s, odd_p, odd_s = [], [], [], [], []
    for l in range(DEPTH):
        npre = _rmsnorm_rows(hp, norm_w[l], BF16)
        nsam = _rmsnorm_rows(hs, norm_w[l], BF16)
        mkv = _matmul(_rmsnorm_rows(mem2d, mem_norm_w[l], BF16), w_mem_kv[l].astype(BF16))
        mem_new.append((mkv[:, :MEM_W].reshape(bp, N_MEM, MEM_HEADS, MEM_HD),
                        mkv[:, MEM_W:].reshape(bp, N_MEM, MEM_HEADS, MEM_HD)))
        mk_s, mv_s = cache_mem_k[l], cache_mem_v[l]
        if l % 2 == 0:
            e = l // 2
            w_in = _relayout_even(w_in_even[e])
            w_out = w_out_even[e].astype(BF16)
            bg_r = _gate_bias_even(b_nsa_gate[e])
            cmpw = (w_cmp1[e].reshape(2, CMP_BLOCK, NSA_HD, NSA_HD), b_cmp1[e], w_cmp2[e], pe_cmp[e])
            hp, st_p = _even_prompt(hp, npre, mkv, w_in, bg_r, *cmpw, lbs[l], hgrn_norm_w[e], w_out, rel_bias,
                                    B=bp, T=tp)
            hs, st_s = _even_sample(hs, nsam, mk_s, mv_s, page_table, cache_cmp_k[e], cache_cmp_v[e], cache_sel_k[e],
                                    cache_sel_v[e], cache_win_k[e], cache_win_v[e], state_hgrn[e], w_in, bg_r,
                                    *cmpw, lbs[l], hgrn_norm_w[e], w_out, rel_bias, B=bs, T=ts)
            even_p.append(st_p)
            even_s.append(st_s)
        else:
            o = l // 2
            w_in = _relayout_odd(w_in_odd[o])
            w_out = w_out_odd[o].astype(BF16)
            bif_r = _gate_bias_odd(b_mlstm_if[o])
            hp, st_p = _odd_mix(hp, npre, mkv, 0, mkv, 1, jnp.zeros((bp, ML_HEADS, ML_DV, ML_DK), F32),
                                jnp.zeros((bp, ML_HEADS, ML_DK), F32), jnp.zeros((bp, ML_HEADS), F32),
                                w_in, bif_r, mlstm_norm_w[o], w_out, B=bp, T=tp, L=CHUNK, valid=CHUNK)
            hs, st_s = _odd_mix(hs, nsam, mk_s.reshape(bs * N_MEM, MEM_W), 0, mv_s.reshape(bs * N_MEM, MEM_W), 0,
                                state_mlstm_c[o], state_mlstm_n[o], state_mlstm_m[o],
                                w_in, bif_r, mlstm_norm_w[o], w_out, B=bs, T=tsp, L=tsp, valid=ts)
            odd_p.append(st_p)
            odd_s.append(st_s)
    y_prompt = _rmsnorm_rows(hp, final_norm_w, F32).reshape(bp, tp, D_MODEL)
    y_sample = _rmsnorm_rows(hs, final_norm_w, F32).reshape(bs, tsp, D_MODEL)[:, :ts]
    return (y_prompt, y_sample,
            _stack(mem_new, 0), _stack(mem_new, 1),
            _stack(even_p, 0), _stack(even_p, 1), _stack(even_p, 2), _stack(even_p, 3),
            _stack(even_p, 4), _stack(even_p, 5), _stack(even_p, 6),
            _stack(odd_p, 0), _stack(odd_p, 1), _stack(odd_p, 2),
            _stack(even_s, 0), _stack(even_s, 1), _stack(even_s, 2), _stack(even_s, 3),
            _stack(even_s, 4), _stack(even_s, 5), _stack(even_s, 6),
            _stack(odd_s, 0), _stack(odd_s, 1), _stack(odd_s, 2))
```

```python
import functools
import math

import jax
import jax.numpy as jnp
import numpy as np
from jax import lax
from jax.experimental import pallas as pl
from jax.experimental.pallas import tpu as pltpu

D_MODEL = 4096
DEPTH = 2
PAST_LEN = 16384
PAGE_SIZE = 128
N_MEM = 256
EPS = 1e-6
CHUNK = 64

HG_DK = 128
HG_DV = 128
HG_HEADS = D_MODEL // 2 // HG_DV
HG_W = HG_HEADS * HG_DV

NSA_HD = 128
NSA_HEADS = D_MODEL // 2 // NSA_HD
NSA_KVH = 4
NSA_G = NSA_HEADS // NSA_KVH
NSA_W = NSA_HEADS * NSA_HD
NSA_KV_W = NSA_KVH * NSA_HD
CMP_BLOCK = 32
CMP_STRIDE = 16
SEL_BLOCK = 64
SEL_SHIFT = SEL_BLOCK.bit_length() - 1
N_SEL = 16
WINDOW = 512
Q_BLOCK = 128

ML_HEADS = D_MODEL // 512
ML_DK = D_MODEL // 2 // ML_HEADS
ML_DV = D_MODEL // ML_HEADS
ML_QK_W = ML_HEADS * ML_DK
ML_V_W = ML_HEADS * ML_DV

MEM_HEADS = 4
MEM_HD = 128
MEM_W = MEM_HEADS * MEM_HD

REL_BUCKETS = 32
REL_MAX_DIST = 128

F32 = jnp.float32
BF16 = jnp.bfloat16
LANES = 128
NEG_INF = float("-inf")
TINY = float(np.finfo(np.float32).tiny)
EXP_CLAMP = 80.0
VMEM_LIMIT = 56 * 1024 * 1024

HG_HB = 4
ML_HB = 2
HG_SUB = 16
SAMPLE_PAD_T = 16

EVEN_OFF = {"qa": 0, "fa": HG_W, "ia": 2 * HG_W, "za": 3 * HG_W, "qb": 4 * HG_W, "zb": 4 * HG_W + NSA_W}
EVEN_OFF["qm"] = EVEN_OFF["zb"] + NSA_W
EVEN_OFF["gb"] = EVEN_OFF["qm"] + MEM_W
EVEN_N = EVEN_OFF["gb"] + NSA_KVH * LANES
KV_N = 6 * NSA_KV_W
ODD_OFF = {"q": 0, "k": ML_QK_W, "v": 2 * ML_QK_W, "og": 2 * ML_QK_W + ML_V_W, "z": 2 * ML_QK_W + 2 * ML_V_W}
ODD_OFF["qm"] = ODD_OFF["z"] + ML_V_W
ODD_OFF["gates"] = ODD_OFF["qm"] + MEM_W
ODD_N = ODD_OFF["gates"] + (ML_HEADS // ML_HB) * LANES


def _dot(a, b):
    return jnp.dot(a, b, preferred_element_type=F32)


def _dot_nt(a, b):
    return lax.dot_general(a, b, (((1,), (1,)), ((), ())), preferred_element_type=F32)


def _dot_tn(a, b):
    return lax.dot_general(a, b, (((0,), (0,)), ((), ())), preferred_element_type=F32)


def _iota2(shape, dim):
    return lax.broadcasted_iota(jnp.int32, shape, dim)


def _cumsum_rows(x, tri_b):
    hi = x.astype(BF16)
    r1 = x - hi.astype(F32)
    mid = r1.astype(BF16)
    lo = (r1 - mid.astype(F32)).astype(BF16)
    return _dot(tri_b, hi) + _dot(tri_b, mid) + _dot(tri_b, lo)


def _row_to_col(row, n):
    eye = _iota2((n, n), 0) == _iota2((n, n), 1)
    return jnp.sum(jnp.where(eye, row, 0.0), axis=1, keepdims=True)


def _col_to_row(col, n):
    eye = _iota2((n, n), 0) == _iota2((n, n), 1)
    return jnp.sum(jnp.where(eye, col, 0.0), axis=0, keepdims=True)


def _silu(x):
    return x * jax.nn.sigmoid(x)


def _params(sem):
    return pltpu.CompilerParams(dimension_semantics=sem, vmem_limit_bytes=VMEM_LIMIT)


def _rmsnorm_body(x_ref, w_ref, o_ref):
    x = x_ref[...].astype(F32)
    y = x * lax.rsqrt(jnp.mean(x * x, axis=-1, keepdims=True) + EPS)
    o_ref[...] = (y * w_ref[...].astype(F32)).astype(o_ref.dtype)


def _rmsnorm_rows(x2d, w, out_dtype, tm=256):
    m, d = x2d.shape
    tm = min(tm, m)
    return pl.pallas_call(
        _rmsnorm_body,
        grid=(m // tm,),
        in_specs=[pl.BlockSpec((tm, d), lambda i: (i, 0)), pl.BlockSpec((1, d), lambda i: (0, 0))],
        out_specs=pl.BlockSpec((tm, d), lambda i: (i, 0)),
        out_shape=jax.ShapeDtypeStruct((m, d), out_dtype),
        compiler_params=_params(("parallel",)),
        name="rmsnorm",
    )(x2d, w.reshape(1, d))


def _matmul_body(a_ref, b_ref, o_ref):
    o_ref[...] = _dot(a_ref[...], b_ref[...])


def _matmul(a, b, tm=1024, tn=1024):
    m, k = a.shape
    _, n = b.shape
    tm, tn = min(tm, m), min(tn, n)
    assert m % tm == 0 and n % tn == 0, (a.shape, b.shape)
    return pl.pallas_call(
        _matmul_body,
        grid=(m // tm, n // tn),
        in_specs=[pl.BlockSpec((tm, k), lambda i, j: (i, 0)), pl.BlockSpec((k, tn), lambda i, j: (0, j))],
        out_specs=pl.BlockSpec((tm, tn), lambda i, j: (i, j)),
        out_shape=jax.ShapeDtypeStruct((m, n), F32),
        compiler_params=_params(("parallel", "parallel")),
        name="matmul",
    )(a, b)


def _matmul_heads_body(a_ref, b_ref, o32_ref, o16_ref):
    acc = _dot(a_ref[...], b_ref[...])
    for h in range(MEM_HEADS):
        o32_ref[:, h, :] = acc[:, h * LANES:(h + 1) * LANES]
    o16_ref[...] = acc.astype(BF16)


def _matmul_heads(a, b, tm=1024):
    m, k = a.shape
    _, n = b.shape
    tn = MEM_HEADS * LANES
    tm = min(tm, m)
    assert m % tm == 0 and n % tn == 0, (a.shape, b.shape)
    return pl.pallas_call(
        _matmul_heads_body,
        grid=(m // tm, n // tn),
        in_specs=[pl.BlockSpec((tm, k), lambda i, j: (i, 0)), pl.BlockSpec((k, tn), lambda i, j: (0, j))],
        out_specs=[pl.BlockSpec((None, tm, MEM_HEADS, LANES), lambda i, j: (j, i, 0, 0)),
                   pl.BlockSpec((tm, tn), lambda i, j: (i, j))],
        out_shape=[jax.ShapeDtypeStruct((n // tn, m, MEM_HEADS, LANES), F32), jax.ShapeDtypeStruct((m, n), BF16)],
        compiler_params=_params(("parallel", "parallel")),
        name="matmul_heads",
    )(a, b)


def _outproj_body(*refs, widths):
    xs = refs[:len(widths)]
    w_ref, r_ref, o_ref = refs[len(widths):]
    acc = r_ref[...]
    off = 0
    for x_ref, w in zip(xs, widths):
        acc = acc + _dot(x_ref[...], w_ref[off:off + w, :])
        off += w
    o_ref[...] = acc


def _outproj(xs, w_bf16, resid, tm=1024, tn=512):
    m = resid.shape[0]
    n = w_bf16.shape[1]
    widths = tuple(x.shape[1] for x in xs)
    assert sum(widths) == w_bf16.shape[0]
    tm = min(tm, m)
    in_specs = [pl.BlockSpec((tm, w), lambda i, j: (i, 0)) for w in widths]
    in_specs += [pl.BlockSpec((w_bf16.shape[0], tn), lambda i, j: (0, j)), pl.BlockSpec((tm, tn), lambda i, j: (i, j))]
    return pl.pallas_call(
        functools.partial(_outproj_body, widths=widths),
        grid=(m // tm, n // tn),
        in_specs=in_specs,
        out_specs=pl.BlockSpec((tm, tn), lambda i, j: (i, j)),
        out_shape=jax.ShapeDtypeStruct((m, n), F32),
        compiler_params=_params(("parallel", "parallel")),
        name="outproj",
    )(*xs, w_bf16, resid)


def _hgrn_body(qa_ref, fa_ref, ia_ref, za_ref, lb_ref, gn_ref, s0_ref, o_ref, s_out, s_scr, *, L, valid):
    c = pl.program_id(2)

    @pl.when(c == 0)
    def _():
        s_scr[...] = s0_ref[...]

    lb = lb_ref[...]
    sig = jax.nn.sigmoid(fa_ref[...])
    logf = jnp.log(lb + (1.0 - lb) * sig)
    kk = (1.0 - lb) * (1.0 - sig)
    if valid < L:
        live = _iota2((L, 1), 0) < valid
        logf = jnp.where(live, logf, 0.0)
        kk = jnp.where(live, kk, 0.0)
    tri_b = (_iota2((L, L), 0) >= _iota2((L, L), 1)).astype(BF16)
    bc = _cumsum_rows(logf, tri_b)
    q = _silu(qa_ref[...])
    gate = _silu(za_ref[...])
    v = ia_ref[...]
    gn = gn_ref[...]
    for j in range(HG_HB):
        sl = slice(j * HG_DK, (j + 1) * HG_DK)
        bj, qj, kj = bc[:, sl], q[:, sl], kk[:, sl]
        vb = v[:, sl].astype(BF16)
        s_prev = s_scr[j]
        inter = _dot((qj * jnp.exp(bj)).astype(BF16), s_prev.astype(BF16))
        for i in range(L // HG_SUB):
            r0, r1 = i * HG_SUB, (i + 1) * HG_SUB
            mid = bj[r0 + HG_SUB // 2:r0 + HG_SUB // 2 + 1, :]
            qi = qj[r0:r1] * jnp.exp(jnp.minimum(bj[r0:r1] - mid, EXP_CLAMP))
            ki = kj * jnp.exp(jnp.minimum(mid - bj, EXP_CLAMP))
            att = _dot_nt(qi.astype(BF16), ki.astype(BF16))
            keep = _iota2((HG_SUB, L), 1) <= _iota2((HG_SUB, L), 0) + r0
            att = jnp.where(keep, att, 0.0)
            o_i = inter[r0:r1] + _dot(att.astype(BF16), vb)
            o_n = o_i * lax.rsqrt(jnp.mean(o_i * o_i, axis=-1, keepdims=True) + EPS) * gn
            o_ref[r0:r1, sl] = (o_n * gate[r0:r1, sl]).astype(o_ref.dtype)
        bl = bj[L - 1:L, :]
        kd = kj * jnp.exp(bl - bj)
        s_scr[j] = _row_to_col(jnp.exp(bl), HG_DK) * s_prev + _dot_tn(kd.astype(BF16), vb)

    @pl.when(c == pl.num_programs(2) - 1)
    def _():
        s_out[...] = s_scr[...]


def _hgrn_call(y, s0, lb, gn, *, B, T, L, valid):
    nc = T // L
    w = HG_HB * HG_DK

    def col(name):
        blk = EVEN_OFF[name] // w
        return pl.BlockSpec((L, w), lambda b, hg, c: (b * nc + c, blk + hg))

    state_spec = pl.BlockSpec((None, HG_HB, HG_DK, HG_DV), lambda b, hg, c: (b, hg, 0, 0))
    return pl.pallas_call(
        functools.partial(_hgrn_body, L=L, valid=valid),
        grid=(B, HG_HEADS // HG_HB, nc),
        in_specs=[col("qa"), col("fa"), col("ia"), col("za"),
                  pl.BlockSpec((1, w), lambda b, hg, c: (0, hg)),
                  pl.BlockSpec((1, HG_DV), lambda b, hg, c: (0, 0)),
                  state_spec],
        out_specs=[pl.BlockSpec((L, w), lambda b, hg, c: (b * nc + c, hg)), state_spec],
        out_shape=[jax.ShapeDtypeStruct((B * T, HG_W), BF16),
                   jax.ShapeDtypeStruct((B, HG_HEADS, HG_DK, HG_DV), F32)],
        scratch_shapes=[pltpu.VMEM((HG_HB, HG_DK, HG_DV), F32)],
        compiler_params=_params(("arbitrary", "arbitrary", "arbitrary")),
        name="hgrn2",
    )(y, y, y, y, lb.reshape(1, HG_W), gn.reshape(1, HG_DV), s0)


def _mlstm_body(q_ref, k_ref, v_ref, og_ref, z_ref, g_ref, bif_ref, gn_ref, c0_ref, n0_ref, m0_ref,
                h_ref, c_out, n_out, m_out, c_scr, n_scr, m_scr, *, L, valid):
    c = pl.program_id(2)

    @pl.when(c == 0)
    def _():
        c_scr[...] = c0_ref[...]
        n_scr[...] = n0_ref[...]
        m_scr[...] = m0_ref[...]

    gates = g_ref[...] + bif_ref[...]
    log_i = gates
    log_f = jnp.minimum(gates, 0.0) - jnp.log(1.0 + jnp.exp(-jnp.abs(gates)))
    if valid < L:
        live = _iota2((L, 1), 0) < valid
        log_i = jnp.where(live, log_i, -1e30)
        log_f = jnp.where(live, log_f, 0.0)
    tri = _iota2((L, L), 0) >= _iota2((L, L), 1)
    bcs = _cumsum_rows(log_f, tri.astype(BF16))
    for j in range(ML_HB):
        b_col = bcs[:, ML_HB + j:ML_HB + j + 1]
        i_col = log_i[:, j:j + 1]
        b_row = _col_to_row(b_col, L)
        i_row = _col_to_row(i_col, L)
        m_prev = m_scr[:, j:j + 1]
        dmat = jnp.where(tri, b_col - b_row + i_row, NEG_INF)
        inter = b_col + m_prev
        mt = jnp.maximum(inter, jnp.max(dmat, axis=1, keepdims=True))
        w_in = jnp.exp(dmat - mt)
        w_x = jnp.exp(inter - mt)
        qj = q_ref[:, j * ML_DK:(j + 1) * ML_DK]
        kj = k_ref[:, j * ML_DK:(j + 1) * ML_DK] * (ML_DK ** -0.5)
        vj = v_ref[:, j * ML_DV:(j + 1) * ML_DV]
        qb, kb = qj.astype(BF16), kj.astype(BF16)
        sw = _dot_nt(qb, kb) * w_in
        c_prev = c_scr[j]
        n_prev = n_scr[:, j * ML_DK:(j + 1) * ML_DK]
        num = w_x * _dot_nt(qb, c_prev.astype(BF16)) + _dot(sw.astype(BF16), vj.astype(BF16))
        den = w_x * jnp.sum(qj * n_prev, axis=1, keepdims=True) + jnp.sum(sw, axis=1, keepdims=True)
        h = num / jnp.maximum(jnp.abs(den), jnp.exp(-mt))
        m_last = mt[L - 1:L, :]
        b_last = b_col[L - 1:L, :]
        w_end = jnp.exp(b_last - b_col + i_col - m_last)
        d_c = jnp.exp(b_last + m_prev - m_last)
        c_scr[j] = d_c * c_prev + _dot_tn((w_end * vj).astype(BF16), kb)
        n_scr[:, j * ML_DK:(j + 1) * ML_DK] = d_c * n_prev + jnp.sum(w_end * kj, axis=0, keepdims=True)
        m_scr[:, j:j + 1] = m_last
        sv = slice(j * ML_DV, (j + 1) * ML_DV)
        h_n = h * lax.rsqrt(jnp.mean(h * h, axis=-1, keepdims=True) + EPS) * gn_ref[:, sv]
        h_ref[:, sv] = (h_n * jax.nn.sigmoid(og_ref[:, sv]) * _silu(z_ref[:, sv])).astype(h_ref.dtype)

    @pl.when(c == pl.num_programs(2) - 1)
    def _():
        c_out[...] = c_scr[...]
        n_out[...] = n_scr[...]
        m_out[...] = m_scr[...]


def _mlstm_call(y, c0, n0, m0, bif_r, gn, *, B, T, L, valid):
    nc = T // L
    ng = ML_HEADS // ML_HB
    wk, wv = ML_HB * ML_DK, ML_HB * ML_DV

    def col(name, w):
        blk = ODD_OFF[name] // w
        return pl.BlockSpec((L, w), lambda b, hg, c: (b * nc + c, blk + hg))

    c_spec = pl.BlockSpec((None, ML_HB, ML_DV, ML_DK), lambda b, hg, c: (b, hg, 0, 0))
    n_spec = pl.BlockSpec((None, 1, wk), lambda b, hg, c: (b, 0, hg))
    m_spec = pl.BlockSpec((None, None, 1, LANES), lambda b, hg, c: (b, hg, 0, 0))
    m0_r = jnp.pad(m0.reshape(B, ng, 1, ML_HB), ((0, 0), (0, 0), (0, 0), (0, LANES - ML_HB)))
    h, c_new, n_new, m_new = pl.pallas_call(
        functools.partial(_mlstm_body, L=L, valid=valid),
        grid=(B, ng, nc),
        in_specs=[col("q", wk), col("k", wk), col("v", wv), col("og", wv), col("z", wv), col("gates", LANES),
                  pl.BlockSpec((None, 1, LANES), lambda b, hg, c: (hg, 0, 0)),
                  pl.BlockSpec((1, wv), lambda b, hg, c: (0, hg)),
                  c_spec, n_spec, m_spec],
        out_specs=[pl.BlockSpec((L, wv), lambda b, hg, c: (b * nc + c, hg)), c_spec, n_spec, m_spec],
        out_shape=[jax.ShapeDtypeStruct((B * T, ML_V_W), BF16),
                   jax.ShapeDtypeStruct((B, ML_HEADS, ML_DV, ML_DK), F32),
                   jax.ShapeDtypeStruct((B, 1, ML_QK_W), F32),
                   jax.ShapeDtypeStruct((B, ng, 1, LANES), F32)],
        scratch_shapes=[pltpu.VMEM((ML_HB, ML_DV, ML_DK), F32), pltpu.VMEM((1, wk), F32), pltpu.VMEM((1, LANES), F32)],
        compiler_params=_params(("arbitrary", "arbitrary", "arbitrary")),
        name="mlstm",
    )(y, y, y, y, y, y, bif_r, gn.reshape(1, ML_V_W), c0, n0.reshape(B, 1, ML_QK_W), m0_r)
    return h, c_new, n_new.reshape(B, ML_HEADS, ML_DK), m_new[:, :, 0, :ML_HB].reshape(B, ML_HEADS)


def _mem_body(q_ref, k_ref, v_ref, o_ref):
    q = q_ref[...] * (MEM_HD ** -0.5)
    for h in range(MEM_HEADS):
        sl = slice(h * MEM_HD, (h + 1) * MEM_HD)
        s = _dot_nt(q[:, sl].astype(BF16), k_ref[:, sl].astype(BF16))
        p = jnp.exp(s - jnp.max(s, axis=-1, keepdims=True))
        o = _dot(p.astype(BF16), v_ref[:, sl].astype(BF16)) / jnp.sum(p, axis=-1, keepdims=True)
        o_ref[:, sl] = o.astype(o_ref.dtype)


def _mem_call(y, q_off, k2d, k_blk, v2d, v_blk, *, B, T, tq=256):
    tq = min(tq, T)
    nq = T // tq
    qb = q_off // MEM_W
    return pl.pallas_call(
        _mem_body,
        grid=(B, nq),
        in_specs=[pl.BlockSpec((tq, MEM_W), lambda b, i: (b * nq + i, qb)),
                  pl.BlockSpec((N_MEM, MEM_W), lambda b, i: (b, k_blk)),
                  pl.BlockSpec((N_MEM, MEM_W), lambda b, i: (b, v_blk))],
        out_specs=pl.BlockSpec((tq, MEM_W), lambda b, i: (b * nq + i, 0)),
        out_shape=jax.ShapeDtypeStruct((B * T, MEM_W), BF16),
        compiler_params=_params(("parallel", "parallel")),
        name="mem_attn",
    )(y, k2d, v2d)


def _gelu_tanh(x):
    return 0.5 * x * (1.0 + jnp.tanh(math.sqrt(2.0 / math.pi) * (x + 0.044715 * (x * x * x))))


def _compress_body(x_ref, w1_ref, b1_ref, w2_ref, pe_ref, o_ref, x32, *, nch):
    x32[...] = x_ref[...].astype(F32)
    a = jnp.zeros((nch, NSA_HD), F32)
    b = jnp.zeros((nch, NSA_HD), F32)
    for s in range(CMP_STRIDE):
        r = x32[pl.ds(s, nch, stride=CMP_STRIDE), :]
        a = a + _dot((r + pe_ref[s:s + 1, :]).astype(BF16), w1_ref[s])
        b = b + _dot((r + pe_ref[CMP_STRIDE + s:CMP_STRIDE + s + 1, :]).astype(BF16), w1_ref[CMP_STRIDE + s])
    h = a + pltpu.roll(b, nch - 1, 0) + b1_ref[...]
    o_ref[...] = _dot(_gelu_tanh(h).astype(BF16), w2_ref[...])


def _compress_call(kv16, w1, b1, w2, pe, *, B, T):
    nch = T // CMP_STRIDE
    return pl.pallas_call(
        functools.partial(_compress_body, nch=nch),
        grid=(2, B, NSA_KVH),
        in_specs=[pl.BlockSpec((T, NSA_HD), lambda t, b, h: (b, t * NSA_KVH + h)),
                  pl.BlockSpec((None, CMP_BLOCK, NSA_HD, NSA_HD), lambda t, b, h: (t, 0, 0, 0)),
                  pl.BlockSpec((None, 1, NSA_HD), lambda t, b, h: (t, 0, 0)),
                  pl.BlockSpec((None, NSA_HD, NSA_HD), lambda t, b, h: (t, 0, 0)),
                  pl.BlockSpec((None, CMP_BLOCK, NSA_HD), lambda t, b, h: (t, 0, 0))],
        out_specs=pl.BlockSpec((None, None, None, nch, NSA_HD), lambda t, b, h: (t, b, h, 0, 0)),
        out_shape=jax.ShapeDtypeStruct((2, B, NSA_KVH, nch, NSA_HD), F32),
        scratch_shapes=[pltpu.VMEM((T, NSA_HD), F32)],
        compiler_params=_params(("parallel", "parallel", "parallel")),
        name="nsa_compress",
    )(kv16, w1.astype(BF16), b1.reshape(2, 1, NSA_HD), w2.astype(BF16), pe)


def _softmax_rows(s):
    m = jnp.max(s, axis=-1, keepdims=True)
    m = jnp.where(m == NEG_INF, 0.0, m)
    p = jnp.exp(s - m)
    return p, jnp.sum(p, axis=-1, keepdims=True)


def _slc_scores(psum, width, n_slc):
    ncmp = psum.shape[1]
    d = _iota2((ncmp, width), 0) - (SEL_BLOCK // CMP_STRIDE) * _iota2((ncmp, width), 1)
    wgt = jnp.where((d == -1) | (d == 3), 1.0, jnp.where((d >= 0) & (d <= 2), 2.0, 0.0))
    wgt = jnp.where(_iota2((ncmp, width), 1) < n_slc, wgt, 0.0).astype(BF16)
    p_hi = psum.astype(BF16)
    p_lo = (psum - p_hi.astype(F32)).astype(BF16)
    return _dot(p_hi, wgt) + _dot(p_lo, wgt)


def _top_blocks(slc, cur, n_pick):
    rows, width = slc.shape
    blk = _iota2((rows, width), 1)
    forced = (blk == 0) | (blk == cur) | (blk == cur - 1)
    score = jnp.where(forced, jnp.inf, slc)
    score = jnp.where(blk > cur, NEG_INF, score)
    blk_f = blk.astype(F32)
    lane = _iota2((rows, LANES), 1)
    sel = jnp.zeros((rows, width), F32)
    picks = jnp.zeros((rows, LANES), F32)
    for i in range(n_pick):
        mx = jnp.max(score, axis=-1, keepdims=True)
        first = jnp.min(jnp.where(score == mx, blk_f, float(width)), axis=-1, keepdims=True)
        pick = blk_f == first
        sel = jnp.where(pick, 1.0, sel)
        picks = jnp.where(lane == i, first, picks)
        score = jnp.where(pick, NEG_INF, score)
    return sel, picks


def _member_by_rank(psum, tpos_row, n_slc, n_pick):
    nq, ncmp = psum.shape
    nb = -(-n_slc // 8) * 8
    d = _iota2((nb, ncmp), 1) - (SEL_BLOCK // CMP_STRIDE) * _iota2((nb, ncmp), 0)
    wgt = jnp.where((d == -1) | (d == 3), 1.0, jnp.where((d >= 0) & (d <= 2), 2.0, 0.0))
    wgt = jnp.where(_iota2((nb, ncmp), 0) < n_slc, wgt, 0.0).astype(BF16)
    p_hi = psum.astype(BF16)
    p_lo = (psum - p_hi.astype(F32)).astype(BF16)
    slc = _dot_nt(wgt, p_hi) + _dot_nt(wgt, p_lo)
    blk = _iota2((nb, nq), 0)
    cur = jnp.right_shift(tpos_row, SEL_SHIFT)
    forced = (blk == 0) | (blk == cur) | (blk == cur - 1)
    score = jnp.where(forced, jnp.inf, slc)
    score = jnp.where(blk > cur, NEG_INF, score)
    ahead = jnp.zeros((nb, nq), F32)
    for i in range(n_slc):
        s_i = score[i:i + 1, :]
        ahead = ahead + jnp.where((s_i > score) | ((s_i == score) & (blk > i)), 1.0, 0.0)
    return jnp.where((ahead < n_pick) & (blk <= cur), 1.0, 0.0)


def _nsa_prompt_body(q_ref, zb_ref, gb_ref, bg_ref, ks_ref, vs_ref, kw_ref, vw_ref, kc_ref, vc_ref,
                     bc_ref, bs_ref, bw_ref, o_ref, ksp, vsp, kwp, vwp, *, T):
    qi = pl.program_id(2)
    tq = Q_BLOCK
    front = T - tq
    wlen = WINDOW + tq
    n_slc = T // SEL_BLOCK

    @pl.when(qi == 0)
    def _():
        ksp[0:front, :] = jnp.zeros((front, NSA_HD), BF16)
        vsp[0:front, :] = jnp.zeros((front, NSA_HD), BF16)
        ksp[front:front + T, :] = ks_ref[...].astype(BF16)
        vsp[front:front + T, :] = vs_ref[...].astype(BF16)
        kwp[0:WINDOW, :] = jnp.zeros((WINDOW, NSA_HD), BF16)
        vwp[0:WINDOW, :] = jnp.zeros((WINDOW, NSA_HD), BF16)
        kwp[WINDOW:WINDOW + T, :] = kw_ref[...].astype(BF16)
        vwp[WINDOW:WINDOW + T, :] = vw_ref[...].astype(BF16)

    t0 = pl.multiple_of(qi * tq, tq)
    tpos = _iota2((tq, 1), 0) + t0
    q_all = q_ref[...] * (NSA_HD ** -0.5)
    qs = [q_all[:, g * NSA_HD:(g + 1) * NSA_HD].astype(BF16) for g in range(NSA_G)]

    ncmp = T // CMP_STRIDE
    vis = tpos >= _iota2((1, ncmp), 1) * CMP_STRIDE + (CMP_BLOCK - 1)
    kcb = kc_ref[...].astype(BF16)
    vcb = vc_ref[...].astype(BF16)
    psum = jnp.zeros((tq, ncmp), F32)
    o_cmp = []
    for g in range(NSA_G):
        s = jnp.where(vis, _dot_nt(qs[g], kcb) + bc_ref[g], NEG_INF)
        p, l = _softmax_rows(s)
        p = p / jnp.maximum(l, TINY)
        psum = psum + p
        o_cmp.append(_dot(p.astype(BF16), vcb))

    member_t = _member_by_rank(psum, _iota2((1, tq), 1) + t0, n_slc, min(N_SEL, n_slc)).astype(BF16)

    nb = member_t.shape[0]
    col_blk = jnp.right_shift(_iota2((nb, T), 1), SEL_SHIFT) + (qi * (tq // SEL_BLOCK) + (tq - T) // SEL_BLOCK)
    expand = (col_blk == _iota2((nb, T), 0)).astype(BF16)
    kpos = _iota2((1, T), 1) + (t0 + tq - T)
    allowed = (_dot_tn(member_t, expand) > 0.5) & (kpos <= tpos)
    mask_s = jnp.where(allowed, 0.0, NEG_INF)
    k_s = ksp[pl.ds(t0, T), :]
    v_s = vsp[pl.ds(t0, T), :]
    o_sel = []
    for g in range(NSA_G):
        p, l = _softmax_rows(_dot_nt(qs[g], k_s) + bs_ref[g] + mask_s)
        o_sel.append(_dot(p.astype(BF16), v_s) / jnp.maximum(l, TINY))

    dist = WINDOW + _iota2((tq, wlen), 0) - _iota2((tq, wlen), 1)
    in_win = (dist >= 0) & (dist < WINDOW) & (_iota2((1, wlen), 1) + (t0 - WINDOW) >= 0)
    mask_w = jnp.where(in_win, 0.0, NEG_INF)
    k_w = kwp[pl.ds(t0, wlen), :]
    v_w = vwp[pl.ds(t0, wlen), :]
    gate = jax.nn.sigmoid(gb_ref[...] + bg_ref[...])
    zb = _silu(zb_ref[...])
    for g in range(NSA_G):
        p, l = _softmax_rows(_dot_nt(qs[g], k_w) + bw_ref[g] + mask_w)
        o_win = _dot(p.astype(BF16), v_w) / jnp.maximum(l, TINY)
        mix = (gate[:, g:g + 1] * o_cmp[g] + gate[:, NSA_G + g:NSA_G + g + 1] * o_sel[g]
               + gate[:, 2 * NSA_G + g:2 * NSA_G + g + 1] * o_win)
        sl = slice(g * NSA_HD, (g + 1) * NSA_HD)
        o_ref[:, sl] = (mix * zb[:, sl]).astype(o_ref.dtype)


def _nsa_prompt_call(y, kv16, kvcmp, bg_r, bias_c, bias_s, bias_w, *, B, T):
    nq = T // Q_BLOCK
    gw = NSA_G * NSA_HD
    wlen = WINDOW + Q_BLOCK

    def kv_spec(j):
        return pl.BlockSpec((T, NSA_HD), lambda b, h, i: (b, j * NSA_KVH + h))

    def cmp_spec(t):
        return pl.BlockSpec((None, None, None, T // CMP_STRIDE, NSA_HD), lambda b, h, i: (t, b, h, 0, 0))

    return pl.pallas_call(
        functools.partial(_nsa_prompt_body, T=T),
        grid=(B, NSA_KVH, nq),
        in_specs=[pl.BlockSpec((Q_BLOCK, gw), lambda b, h, i: (b * nq + i, EVEN_OFF["qb"] // gw + h)),
                  pl.BlockSpec((Q_BLOCK, gw), lambda b, h, i: (b * nq + i, EVEN_OFF["zb"] // gw + h)),
                  pl.BlockSpec((Q_BLOCK, LANES), lambda b, h, i: (b * nq + i, EVEN_OFF["gb"] // LANES + h)),
                  pl.BlockSpec((None, 1, LANES), lambda b, h, i: (h, 0, 0)),
                  kv_spec(2), kv_spec(3), kv_spec(4), kv_spec(5), cmp_spec(0), cmp_spec(1),
                  pl.BlockSpec((None, NSA_G, Q_BLOCK, T // CMP_STRIDE), lambda b, h, i: (h, 0, i, 0)),
                  pl.BlockSpec((None, NSA_G, Q_BLOCK, T), lambda b, h, i: (h, 0, 0, 0)),
                  pl.BlockSpec((None, NSA_G, Q_BLOCK, wlen), lambda b, h, i: (h, 0, 0, 0))],
        out_specs=pl.BlockSpec((Q_BLOCK, gw), lambda b, h, i: (b * nq + i, h)),
        out_shape=jax.ShapeDtypeStruct((B * T, NSA_W), BF16),
        scratch_shapes=[pltpu.VMEM((2 * T - Q_BLOCK, NSA_HD), BF16), pltpu.VMEM((2 * T - Q_BLOCK, NSA_HD), BF16),
                        pltpu.VMEM((WINDOW + T, NSA_HD), BF16), pltpu.VMEM((WINDOW + T, NSA_HD), BF16)],
        compiler_params=_params(("arbitrary", "arbitrary", "arbitrary")),
        name="nsa_prompt",
    )(y, y, y, bg_r, kv16, kv16, kv16, kv16, kvcmp, kvcmp, bias_c, bias_s, bias_w)


CMP_PAGES = 16
CHUNKS_PER_PAGE = PAGE_SIZE // CMP_STRIDE
CHUNK_W = CMP_STRIDE * NSA_KV_W


def _cmp_pages_body(pt_ref, *refs):
    del pt_ref
    pages = refs[:CMP_PAGES]
    w_ref, pe_ref, o_ref = refs[CMP_PAGES:]
    x = jnp.concatenate([r[...] for r in pages], axis=0)
    rows = x.shape[0]
    per_head = [jnp.concatenate([x[:, s * NSA_KV_W + h * NSA_HD:s * NSA_KV_W + (h + 1) * NSA_HD]
                                 for s in range(CMP_STRIDE)], axis=1) for h in range(NSA_KVH)]
    w = w_ref[...]
    r = _dot(jnp.concatenate(per_head, axis=0).astype(BF16), w)
    pc = _dot(pe_ref[...], w)
    r = r + jnp.concatenate([pc[0:1, :NSA_HD], pc[1:2, NSA_HD:]], axis=1)
    for h in range(NSA_KVH):
        o_ref[h] = r[h * rows:(h + 1) * rows]


def _cmp_pages_call(pool, page_table, w1, pe, *, B):
    n_pages = page_table.shape[1]
    rows = CMP_PAGES * CHUNKS_PER_PAGE
    view = pool.reshape(pool.shape[0] * CHUNKS_PER_PAGE, CHUNK_W)
    w = w1.reshape(2, CMP_STRIDE, NSA_HD, NSA_HD).transpose(1, 2, 0, 3).reshape(CMP_STRIDE * NSA_HD, 2 * NSA_HD)
    pe_rows = jnp.pad(pe.reshape(2, CMP_STRIDE * NSA_HD), ((0, 6), (0, 0))).astype(BF16)

    def page_spec(i):
        return pl.BlockSpec((CHUNKS_PER_PAGE, CHUNK_W), lambda b, s, pt: (pt[b * n_pages + s * CMP_PAGES + i], 0))

    grid_spec = pltpu.PrefetchScalarGridSpec(
        num_scalar_prefetch=1,
        grid=(B, n_pages // CMP_PAGES),
        in_specs=[page_spec(i) for i in range(CMP_PAGES)]
        + [pl.BlockSpec((CMP_STRIDE * NSA_HD, 2 * NSA_HD), lambda b, s, pt: (0, 0)),
           pl.BlockSpec((8, CMP_STRIDE * NSA_HD), lambda b, s, pt: (0, 0))],
        out_specs=pl.BlockSpec((None, NSA_KVH, rows, 2 * NSA_HD), lambda b, s, pt: (b, 0, s, 0)),
    )
    return pl.pallas_call(
        _cmp_pages_body,
        grid_spec=grid_spec,
        out_shape=jax.ShapeDtypeStruct((B, NSA_KVH, n_pages * CHUNKS_PER_PAGE, 2 * NSA_HD), F32),
        compiler_params=_params(("arbitrary", "arbitrary")),
        name="nsa_cmp_pages",
    )(page_table.reshape(-1), *([view] * CMP_PAGES), w.astype(BF16), pe_rows)


SLC_LANES = 384


def _sample_q_rows(q_ref):
    q = q_ref[...] * (NSA_HD ** -0.5)
    return jnp.concatenate([q[:, g * NSA_HD:(g + 1) * NSA_HD] for g in range(NSA_G)], axis=0).astype(BF16)


def _nsa_sample_main_body(abk_ref, abv_ref, b1_ref, w2_ref, q_ref, wk_ref, wv_ref, kn_ref, vn_ref, bc_ref, bw_ref,
                          ocmp_ref, owin_ref, idx_ref, *, T, n_slc):
    tp = SAMPLE_PAD_T
    rows = NSA_G * tp
    ncmp = abk_ref.shape[0]

    def compressed(ab_ref, t):
        ab = ab_ref[...]
        h = ab[:, :NSA_HD] + pltpu.roll(ab[:, NSA_HD:], ncmp - 1, 0) + b1_ref[t]
        return _dot(_gelu_tanh(h).astype(BF16), w2_ref[t]).astype(BF16)

    kc, vc = compressed(abk_ref, 0), compressed(abv_ref, 1)
    q = _sample_q_rows(q_ref)
    step = jnp.bitwise_and(_iota2((rows, 1), 0), tp - 1)
    tpos = PAST_LEN + step
    vis = tpos >= _iota2((1, ncmp), 1) * CMP_STRIDE + (CMP_BLOCK - 1)
    p, l = _softmax_rows(jnp.where(vis, _dot_nt(q, kc) + bc_ref[...], NEG_INF))
    p = p / jnp.maximum(l, TINY)
    ocmp_ref[...] = _dot(p.astype(BF16), vc)
    psum = p[0:tp]
    for g in range(1, NSA_G):
        psum = psum + p[g * tp:(g + 1) * tp]
    cur = jnp.right_shift(PAST_LEN + _iota2((tp, 1), 0), SEL_SHIFT)
    _, picks = _top_blocks(_slc_scores(psum, SLC_LANES, n_slc), cur, N_SEL)
    idx_ref[...] = picks.astype(jnp.int32)

    wb = wk_ref.shape[0]
    wlen = bw_ref.shape[1]
    fill = jnp.zeros((wlen - wb - tp, NSA_HD), BF16)
    k_all = jnp.concatenate([wk_ref[...].astype(BF16), kn_ref[...], fill], axis=0)
    v_all = jnp.concatenate([wv_ref[...].astype(BF16), vn_ref[...], fill], axis=0)
    col = _iota2((1, wlen), 1)
    dist = tpos - (PAST_LEN - wb + col)
    in_win = (dist >= 0) & (dist < WINDOW) & (col < wb + T)
    pw, lw = _softmax_rows(jnp.where(in_win, _dot_nt(q, k_all) + bw_ref[...], NEG_INF))
    owin_ref[...] = _dot(pw.astype(BF16), v_all) / jnp.maximum(lw, TINY)


def _nsa_sample_main_call(ys, kv16, abk, abv, b1, w2, wk, wv, bias_c, bias_w, *, B, T):
    tp = SAMPLE_PAD_T
    rows = NSA_G * tp
    gw = NSA_G * NSA_HD
    ncmp = abk.shape[2]
    wb = wk.shape[1]
    wlen = bias_w.shape[-1]
    n_slc = -(-(PAST_LEN + T) // SEL_BLOCK)
    assert n_slc <= SLC_LANES and T <= tp
    ab_spec = pl.BlockSpec((None, None, ncmp, 2 * NSA_HD), lambda b, h: (b, h, 0, 0))
    win_spec = pl.BlockSpec((wb, NSA_HD), lambda b, h: (b, h))
    o_spec = pl.BlockSpec((None, None, rows, NSA_HD), lambda b, h: (b, h, 0, 0))
    return pl.pallas_call(
        functools.partial(_nsa_sample_main_body, T=T, n_slc=n_slc),
        grid=(B, NSA_KVH),
        in_specs=[ab_spec, ab_spec,
                  pl.BlockSpec((2, 1, NSA_HD), lambda b, h: (0, 0, 0)),
                  pl.BlockSpec((2, NSA_HD, NSA_HD), lambda b, h: (0, 0, 0)),
                  pl.BlockSpec((tp, gw), lambda b, h: (b, EVEN_OFF["qb"] // gw + h)),
                  win_spec, win_spec,
                  pl.BlockSpec((tp, NSA_HD), lambda b, h: (b, 4 * NSA_KVH + h)),
                  pl.BlockSpec((tp, NSA_HD), lambda b, h: (b, 5 * NSA_KVH + h)),
                  pl.BlockSpec((None, rows, ncmp), lambda b, h: (h, 0, 0)),
                  pl.BlockSpec((None, rows, wlen), lambda b, h: (h, 0, 0))],
        out_specs=[o_spec, o_spec, pl.BlockSpec((None, None, tp, LANES), lambda b, h: (b, h, 0, 0))],
        out_shape=[jax.ShapeDtypeStruct((B, NSA_KVH, rows, NSA_HD), F32),
                   jax.ShapeDtypeStruct((B, NSA_KVH, rows, NSA_HD), F32),
                   jax.ShapeDtypeStruct((B, NSA_KVH, tp, LANES), jnp.int32)],
        compiler_params=_params(("parallel", "parallel")),
        name="nsa_sample_main",
    )(abk, abv, b1.reshape(2, 1, NSA_HD), w2.astype(BF16), ys,
      wk.reshape(B * wb, NSA_KV_W), wv.reshape(B * wb, NSA_KV_W), kv16, kv16, bias_c, bias_w)


NEAR_BLOCKS = 3


def _nsa_sample_sel_body(idx_ref, pt_ref, q_ref, kn_ref, vn_ref, tbl_ref, ocmp_ref, owin_ref, gb_ref, bg_ref, zb_ref,
                         *refs, T):
    del pt_ref
    k_blocks = refs[:N_SEL]
    v_blocks = refs[N_SEL:2 * N_SEL]
    o_ref, osel = refs[2 * N_SEL:]
    tp = SAMPLE_PAD_T
    rows = NSA_G * tp
    b, h, t = pl.program_id(0), pl.program_id(1), pl.program_id(2)
    base = ((b * NSA_KVH + h) * T + t) * N_SEL
    first_new = PAST_LEN // SEL_BLOCK
    cur = jnp.right_shift(PAST_LEN + t, SEL_SHIFT)
    q = _sample_q_rows(q_ref)
    pad = jnp.zeros((SEL_BLOCK - tp, NSA_HD), BF16)
    k_new = jnp.concatenate([kn_ref[...], pad], axis=0)
    v_new = jnp.concatenate([vn_ref[...], pad], axis=0)
    lane = _iota2((1, LANES), 1)
    low = lane < SEL_BLOCK
    within = jnp.bitwise_and(lane, SEL_BLOCK - 1)
    ks, vs, bias, kpos = [], [], [], []
    for i in range(0, N_SEL, 2):
        pair_bias, pair_pos = [], []
        for j in (i, i + 1):
            blk = idx_ref[base + j]
            is_new = blk >= first_new
            ks.append(jnp.where(is_new, k_new, k_blocks[j][...].astype(BF16)))
            vs.append(jnp.where(is_new, v_new, v_blocks[j][...].astype(BF16)))
            pair_bias.append(tbl_ref[jnp.clip(blk - (first_new - NEAR_BLOCKS), 0, NEAR_BLOCKS)])
            pair_pos.append(jnp.where(blk <= cur, blk * SEL_BLOCK, PAST_LEN + SEL_BLOCK * LANES) + within)
        bias.append(jnp.where(low, pair_bias[0], pair_bias[1]))
        kpos.append(jnp.where(low, pair_pos[0], pair_pos[1]))
    k_all = jnp.concatenate(ks, axis=0)
    v_all = jnp.concatenate(vs, axis=0)
    step = jnp.bitwise_and(_iota2((rows, 1), 0), tp - 1)
    ok = jnp.concatenate(kpos, axis=1) <= PAST_LEN + step
    p, l = _softmax_rows(jnp.where(ok, _dot_nt(q, k_all) + jnp.concatenate(bias, axis=1), NEG_INF))
    o = _dot(p.astype(BF16), v_all) / jnp.maximum(l, TINY)

    @pl.when(t == 0)
    def _():
        osel[...] = jnp.zeros_like(osel)

    osel[...] = jnp.where(step == t, o, osel[...])

    @pl.when(t == T - 1)
    def _():
        gate = jax.nn.sigmoid(gb_ref[...] + bg_ref[...])
        zb = _silu(zb_ref[...])
        for g in range(NSA_G):
            r = slice(g * tp, (g + 1) * tp)
            mix = (gate[:, g:g + 1] * ocmp_ref[r, :] + gate[:, NSA_G + g:NSA_G + g + 1] * osel[r, :]
                   + gate[:, 2 * NSA_G + g:2 * NSA_G + g + 1] * owin_ref[r, :])
            sl = slice(g * NSA_HD, (g + 1) * NSA_HD)
            o_ref[:, sl] = (mix * zb[:, sl]).astype(o_ref.dtype)


def _nsa_sample_sel_call(ys, kv16, idx, page_table, pool_k, pool_v, tbl, o_cmp, o_win, bg_r, *, B, T):
    tp = SAMPLE_PAD_T
    rows = NSA_G * tp
    gw = NSA_G * NSA_HD
    n_pages = page_table.shape[1]
    halves = PAGE_SIZE // SEL_BLOCK
    idx_flat = idx[:, :, :T, :N_SEL].reshape(-1)
    view_k = pool_k.reshape(pool_k.shape[0] * halves, SEL_BLOCK, NSA_KV_W)
    view_v = pool_v.reshape(pool_v.shape[0] * halves, SEL_BLOCK, NSA_KV_W)

    def blk_spec(j):
        def index(b, h, t, idx_s, pt_s):
            blk = idx_s[((b * NSA_KVH + h) * T + t) * N_SEL + j]
            page = pt_s[b * n_pages + jnp.minimum(blk // halves, n_pages - 1)]
            return (page * halves + blk % halves, 0, h)
        return pl.BlockSpec((None, SEL_BLOCK, NSA_HD), index)

    o_spec = pl.BlockSpec((None, None, rows, NSA_HD), lambda b, h, t, *_: (b, h, 0, 0))
    grid_spec = pltpu.PrefetchScalarGridSpec(
        num_scalar_prefetch=2,
        grid=(B, NSA_KVH, T),
        in_specs=[pl.BlockSpec((tp, gw), lambda b, h, t, *_: (b, EVEN_OFF["qb"] // gw + h)),
                  pl.BlockSpec((tp, NSA_HD), lambda b, h, t, *_: (b, 2 * NSA_KVH + h)),
                  pl.BlockSpec((tp, NSA_HD), lambda b, h, t, *_: (b, 3 * NSA_KVH + h)),
                  pl.BlockSpec((None, NEAR_BLOCKS + 1, rows, LANES), lambda b, h, t, *_: (h, 0, 0, 0)),
                  o_spec, o_spec,
                  pl.BlockSpec((tp, LANES), lambda b, h, t, *_: (b, EVEN_OFF["gb"] // LANES + h)),
                  pl.BlockSpec((None, 1, LANES), lambda b, h, t, *_: (h, 0, 0)),
                  pl.BlockSpec((tp, gw), lambda b, h, t, *_: (b, EVEN_OFF["zb"] // gw + h))]
        + [blk_spec(j) for j in range(N_SEL)] * 2,
        out_specs=pl.BlockSpec((tp, gw), lambda b, h, t, *_: (b, h)),
        scratch_shapes=[pltpu.VMEM((rows, NSA_HD), F32)],
    )
    return pl.pallas_call(
        functools.partial(_nsa_sample_sel_body, T=T),
        grid_spec=grid_spec,
        out_shape=jax.ShapeDtypeStruct((B * tp, NSA_W), BF16),
        compiler_params=_params(("arbitrary", "arbitrary", "arbitrary")),
        name="nsa_sample_sel",
    )(idx_flat, page_table.reshape(-1), ys, kv16, kv16, tbl, o_cmp, o_win, ys, bg_r, ys,
      *([view_k] * N_SEL), *([view_v] * N_SEL))


def _sample_bias_tables(rel_bias, T, wb):
    tp = SAMPLE_PAD_T
    ncmp = PAST_LEN // CMP_STRIDE
    wlen = -(-(wb + tp) // LANES) * LANES
    first = PAST_LEN // SEL_BLOCK - NEAR_BLOCKS
    assert PAST_LEN - ((first + 1) * SEL_BLOCK - 1) >= REL_MAX_DIST
    lo, hi = -wlen, PAST_LEN + tp
    rev = _bias_line(rel_bias, lo, hi)[:, ::-1]

    def rows(tbl):
        return tbl.reshape(NSA_KVH, NSA_G * tp, tbl.shape[-1])

    t_c = jnp.stack([lax.slice_in_dim(rev, hi - 1 - (PAST_LEN + t - (CMP_BLOCK - 1)),
                                      hi - 1 - (PAST_LEN + t - (CMP_BLOCK - 1)) + CMP_STRIDE * ncmp,
                                      stride=CMP_STRIDE, axis=1) for t in range(tp)], axis=1)
    t_w = jnp.stack([rev[:, hi - 1 - (wb + t):hi - 1 - (wb + t) + wlen] for t in range(tp)], axis=1)
    far = jnp.broadcast_to(rev[:, hi - 1 - REL_MAX_DIST][:, None, None], (NSA_HEADS, tp, LANES))
    near = []
    for k in range(1, NEAR_BLOCKS + 1):
        a = PAST_LEN - (first + k) * SEL_BLOCK
        half = jnp.stack([rev[:, hi - 1 - (a + t):hi - 1 - (a + t) + SEL_BLOCK] for t in range(tp)], axis=1)
        near.append(jnp.concatenate([half, half], axis=-1))
    t_s = jnp.stack([far] + near, axis=1).reshape(NSA_KVH, NSA_G, NEAR_BLOCKS + 1, tp, LANES)
    t_s = t_s.transpose(0, 2, 1, 3, 4).reshape(NSA_KVH, NEAR_BLOCKS + 1, NSA_G * tp, LANES)
    return rows(t_c), rows(t_w), t_s


def _relayout_even(w):
    src = {}
    off = 0
    for name, width in (("qa", HG_W), ("fa", HG_W), ("ia", HG_W), ("za", HG_W), ("qb", NSA_W), ("kvb", 6 * NSA_KV_W),
                        ("gb", 3 * NSA_HEADS), ("zb", NSA_W), ("qm", MEM_W)):
        src[name] = (off, width)
        off += width
    cols = [w[:, src[n][0]:src[n][0] + src[n][1]] for n in ("qa", "fa", "ia", "za", "qb", "zb", "qm")]
    g0 = src["gb"][0]
    for h in range(NSA_KVH):
        for j in range(3):
            cols.append(w[:, g0 + j * NSA_HEADS + h * NSA_G:g0 + j * NSA_HEADS + (h + 1) * NSA_G])
        cols.append(jnp.zeros((w.shape[0], LANES - 3 * NSA_G), w.dtype))
    w_kv = w[:, src["kvb"][0]:src["kvb"][0] + src["kvb"][1]]
    return jnp.concatenate(cols, axis=1).astype(BF16), w_kv.astype(BF16)


def _relayout_odd(w):
    src = {}
    off = 0
    for name, width in (("q", ML_QK_W), ("k", ML_QK_W), ("v", ML_V_W), ("og", ML_V_W), ("ig", ML_HEADS),
                        ("fg", ML_HEADS), ("z", ML_V_W), ("qm", MEM_W)):
        src[name] = (off, width)
        off += width
    cols = [w[:, src[n][0]:src[n][0] + src[n][1]] for n in ("q", "k", "v", "og", "z", "qm")]
    for hg in range(ML_HEADS // ML_HB):
        for n in ("ig", "fg"):
            cols.append(w[:, src[n][0] + hg * ML_HB:src[n][0] + (hg + 1) * ML_HB])
        cols.append(jnp.zeros((w.shape[0], LANES - 2 * ML_HB), w.dtype))
    return jnp.concatenate(cols, axis=1).astype(BF16)


def _gate_bias_even(b_gate):
    g = b_gate.reshape(3, NSA_KVH, NSA_G).transpose(1, 0, 2).reshape(NSA_KVH, 1, 3 * NSA_G)
    return jnp.pad(g, ((0, 0), (0, 0), (0, LANES - 3 * NSA_G)))


def _gate_bias_odd(b_if):
    g = b_if.reshape(2, ML_HEADS // ML_HB, ML_HB).transpose(1, 0, 2).reshape(ML_HEADS // ML_HB, 1, 2 * ML_HB)
    return jnp.pad(g, ((0, 0), (0, 0), (0, LANES - 2 * ML_HB)))


def _rel_bucket(dist):
    n = np.maximum(dist, 0)
    exact = REL_BUCKETS // 2
    nf = np.maximum(n, 1).astype(np.float32)
    large = exact + (np.log(nf / exact) / math.log(REL_MAX_DIST / exact) * (REL_BUCKETS - exact)).astype(np.int32)
    return np.where(n < exact, n, np.minimum(large, REL_BUCKETS - 1))


def _bias_line(rel_bias, lo, hi):
    buckets = _rel_bucket(np.arange(lo, hi))
    edges = np.flatnonzero(np.diff(buckets)) + 1
    starts = np.concatenate([[0], edges])
    ends = np.concatenate([edges, [hi - lo]])
    runs = [jnp.broadcast_to(rel_bias[int(buckets[s])][None, :], (int(e - s), NSA_HEADS)) for s, e in zip(starts, ends)]
    return jnp.concatenate(runs, axis=0).T.astype(F32)


def _prompt_bias_tables(rel_bias, T):
    ncmp = T // CMP_STRIDE
    wlen = WINDOW + Q_BLOCK
    lo, hi = -(CMP_STRIDE * ncmp + CMP_BLOCK), T
    line = _bias_line(rel_bias, lo, hi)
    rev = line[:, ::-1]

    def split(tbl):
        return tbl.reshape((NSA_KVH, NSA_G) + tbl.shape[1:])

    t_c = jnp.stack([line[:, -(CMP_STRIDE * n + CMP_BLOCK - 1) - lo:-(CMP_STRIDE * n + CMP_BLOCK - 1) - lo + T]
                     for n in range(ncmp)], axis=1).swapaxes(1, 2)
    t_s = jnp.stack([rev[:, hi - 1 - (T - Q_BLOCK + r):hi - 1 - (T - Q_BLOCK + r) + T] for r in range(Q_BLOCK)], axis=1)
    t_w = jnp.stack([rev[:, hi - 1 - (WINDOW + r):hi - 1 - (WINDOW + r) + wlen] for r in range(Q_BLOCK)], axis=1)
    return split(t_c), split(t_s), split(t_w)


def _nsa_sample(ys, kv16, page_table, pk_cmp, pv_cmp, pk_sel, pv_sel, wk, wv, bg_r, w1, b1, w2, pe, rel_bias, *, B, T):
    assert (PAST_LEN + T) // CMP_STRIDE == PAST_LEN // CMP_STRIDE
    abk = _cmp_pages_call(pk_cmp, page_table, w1[0], pe[0], B=B)
    abv = _cmp_pages_call(pv_cmp, page_table, w1[1], pe[1], B=B)
    bias_c, bias_w, tbl = _sample_bias_tables(rel_bias, T, wk.shape[1])
    o_cmp, o_win, idx = _nsa_sample_main_call(ys, kv16, abk, abv, b1, w2, wk, wv, bias_c, bias_w, B=B, T=T)
    return _nsa_sample_sel_call(ys, kv16, idx, page_table, pk_sel, pv_sel, tbl, o_cmp, o_win, bg_r, B=B, T=T)


def _even_prompt(hp2d, npre, mkv16, w_in, w_kv, bg_r, w1, b1, w2, pe, lb, g_norm, w_out, rel_bias, *, B, T):
    y = _matmul(npre, w_in)
    kv32, kv16 = _matmul_heads(npre, w_kv)
    oa, s_new = _hgrn_call(y, jnp.zeros((B, HG_HEADS, HG_DK, HG_DV), F32), lb, g_norm, B=B, T=T, L=CHUNK, valid=CHUNK)
    kvcmp = _compress_call(kv16, w1, b1, w2, pe, B=B, T=T)
    ob = _nsa_prompt_call(y, kv16, kvcmp, bg_r, *_prompt_bias_tables(rel_bias, T), B=B, T=T)
    om = _mem_call(y, EVEN_OFF["qm"], mkv16, 0, mkv16, 1, B=B, T=T)
    h_new = _outproj([oa, ob, om], w_out, hp2d)
    wb = min(WINDOW, T)
    rows = kv32.reshape(6, B, T, NSA_KVH, NSA_HD)
    return h_new, (rows[0], rows[1], rows[2], rows[3], rows[4, :, -wb:], rows[5, :, -wb:], s_new)


def _even_sample(hs2d, nsam, mk_s, mv_s, page_table, pk_cmp, pv_cmp, pk_sel, pv_sel, wk, wv, s0,
                 w_in, w_kv, bg_r, w1, b1, w2, pe, lb, g_norm, w_out, rel_bias, *, B, T):
    tp = SAMPLE_PAD_T
    y = _matmul(nsam, w_in)
    kv32, kv16 = _matmul_heads(nsam, w_kv)
    oa, s_new = _hgrn_call(y, s0, lb, g_norm, B=B, T=tp, L=tp, valid=T)
    ob = _nsa_sample(y, kv16, page_table, pk_cmp, pv_cmp, pk_sel, pv_sel, wk, wv, bg_r, w1, b1, w2, pe, rel_bias,
                     B=B, T=T)
    om = _mem_call(y, EVEN_OFF["qm"], mk_s.reshape(B * N_MEM, MEM_W), 0, mv_s.reshape(B * N_MEM, MEM_W), 0, B=B, T=tp)
    rows = kv32.reshape(6, B, tp, NSA_KVH, NSA_HD)[:, :, :T]
    wb = wk.shape[1]
    win_k = jnp.concatenate([wk, rows[4]], axis=1)[:, -wb:]
    win_v = jnp.concatenate([wv, rows[5]], axis=1)[:, -wb:]
    return _outproj([oa, ob, om], w_out, hs2d), (rows[0], rows[1], rows[2], rows[3], win_k, win_v, s_new)


def _odd_mix(h2d, hn, k2d, k_blk, v2d, v_blk, c0, n0, m0, w_in, bif_r, g_norm, w_out, *, B, T, L, valid):
    y = _matmul(hn, w_in)
    h, c_new, n_new, m_new = _mlstm_call(y, c0, n0, m0, bif_r, g_norm, B=B, T=T, L=L, valid=valid)
    om = _mem_call(y, ODD_OFF["qm"], k2d, k_blk, v2d, v_blk, B=B, T=T)
    return _outproj([h, om], w_out, h2d), (c_new, n_new, m_new)


def _stack(lst, i):
    return jnp.stack([t[i] for t in lst])


def kernel(x_prompt, x_sample, cache_mem_k, cache_mem_v, cache_cmp_k, cache_cmp_v, cache_sel_k, cache_sel_v,
           cache_win_k, cache_win_v, state_hgrn, state_mlstm_c, state_mlstm_n, state_mlstm_m, page_table,
           mem_prompt, norm_w, mem_norm_w, final_norm_w, rel_bias, w_mem_kv, w_in_even, b_nsa_gate,
           w_cmp1, b_cmp1, w_cmp2, pe_cmp, hgrn_lb_logits, hgrn_norm_w, w_out_even, w_in_odd, b_mlstm_if,
           mlstm_norm_w, w_out_odd):
    bp, tp = x_prompt.shape[:2]
    bs, ts = x_sample.shape[:2]
    tsp = SAMPLE_PAD_T
    lbs = jnp.cumsum(jax.nn.softmax(hgrn_lb_logits.astype(F32), axis=0), axis=0)
    hp = x_prompt.reshape(bp * tp, D_MODEL)
    hs = jnp.pad(x_sample, ((0, 0), (0, tsp - ts), (0, 0))).reshape(bs * tsp, D_MODEL)
    mem2d = mem_prompt.reshape(bp * N_MEM, D_MODEL)
    mem_new, even_p, even_s, odd_p, odd_s = [], [], [], [], []
    for l in range(DEPTH):
        npre = _rmsnorm_rows(hp, norm_w[l], BF16)
        nsam = _rmsnorm_rows(hs, norm_w[l], BF16)
        mkv32, mkv = _matmul_heads(_rmsnorm_rows(mem2d, mem_norm_w[l], BF16), w_mem_kv[l].astype(BF16))
        mem_new.append((mkv32[0].reshape(bp, N_MEM, MEM_HEADS, MEM_HD), mkv32[1].reshape(bp, N_MEM, MEM_HEADS, MEM_HD)))
        mk_s, mv_s = cache_mem_k[l], cache_mem_v[l]
        if l % 2 == 0:
            e = l // 2
            w_in, w_kv = _relayout_even(w_in_even[e])
            w_out = w_out_even[e].astype(BF16)
            bg_r = _gate_bias_even(b_nsa_gate[e])
            cmpw = (w_cmp1[e].reshape(2, CMP_BLOCK, NSA_HD, NSA_HD), b_cmp1[e], w_cmp2[e], pe_cmp[e])
            hp, st_p = _even_prompt(hp, npre, mkv, w_in, w_kv, bg_r, *cmpw, lbs[l], hgrn_norm_w[e], w_out, rel_bias,
                                    B=bp, T=tp)
            hs, st_s = _even_sample(hs, nsam, mk_s, mv_s, page_table, cache_cmp_k[e], cache_cmp_v[e], cache_sel_k[e],
                                    cache_sel_v[e], cache_win_k[e], cache_win_v[e], state_hgrn[e], w_in, w_kv, bg_r,
                                    *cmpw, lbs[l], hgrn_norm_w[e], w_out, rel_bias, B=bs, T=ts)
            even_p.append(st_p)
            even_s.append(st_s)
        else:
            o = l // 2
            w_in = _relayout_odd(w_in_odd[o])
            w_out = w_out_odd[o].astype(BF16)
            bif_r = _gate_bias_odd(b_mlstm_if[o])
            hp, st_p = _odd_mix(hp, npre, mkv, 0, mkv, 1, jnp.zeros((bp, ML_HEADS, ML_DV, ML_DK), F32),
                                jnp.zeros((bp, ML_HEADS, ML_DK), F32), jnp.zeros((bp, ML_HEADS), F32),
                                w_in, bif_r, mlstm_norm_w[o], w_out, B=bp, T=tp, L=CHUNK, valid=CHUNK)
            hs, st_s = _odd_mix(hs, nsam, mk_s.reshape(bs * N_MEM, MEM_W), 0, mv_s.reshape(bs * N_MEM, MEM_W), 0,
                                state_mlstm_c[o], state_mlstm_n[o], state_mlstm_m[o],
                                w_in, bif_r, mlstm_norm_w[o], w_out, B=bs, T=tsp, L=tsp, valid=ts)
            odd_p.append(st_p)
            odd_s.append(st_s)
    y_prompt = _rmsnorm_rows(hp, final_norm_w, F32).reshape(bp, tp, D_MODEL)
    y_sample = _rmsnorm_rows(hs, final_norm_w, F32).reshape(bs, tsp, D_MODEL)[:, :ts]
    return (y_prompt, y_sample,
            _stack(mem_new, 0), _stack(mem_new, 1),
            _stack(even_p, 0), _stack(even_p, 1), _stack(even_p, 2), _stack(even_p, 3),
            _stack(even_p, 4), _stack(even_p, 5), _stack(even_p, 6),
            _stack(odd_p, 0), _stack(odd_p, 1), _stack(odd_p, 2),
            _stack(even_s, 0), _stack(even_s, 1), _stack(even_s, 2), _stack(even_s, 3),
            _stack(even_s, 4), _stack(even_s, 5), _stack(even_s, 6),
            _stack(odd_s, 0), _stack(odd_s, 1), _stack(odd_s, 2))
```

```python
import functools
import math

import jax
import jax.numpy as jnp
import numpy as np
from jax import lax
from jax.experimental import pallas as pl
from jax.experimental.pallas import tpu as pltpu

D_MODEL = 4096
DEPTH = 2
PAST_LEN = 16384
PAGE_SIZE = 128
N_MEM = 256
EPS = 1e-6
CHUNK = 64

HG_DK = 128
HG_DV = 128
HG_HEADS = D_MODEL // 2 // HG_DV
HG_W = HG_HEADS * HG_DV

NSA_HD = 128
NSA_HEADS = D_MODEL // 2 // NSA_HD
NSA_KVH = 4
NSA_G = NSA_HEADS // NSA_KVH
NSA_W = NSA_HEADS * NSA_HD
NSA_KV_W = NSA_KVH * NSA_HD
CMP_BLOCK = 32
CMP_STRIDE = 16
SEL_BLOCK = 64
SEL_SHIFT = SEL_BLOCK.bit_length() - 1
N_SEL = 16
WINDOW = 512
Q_BLOCK = 128

ML_HEADS = D_MODEL // 512
ML_DK = D_MODEL // 2 // ML_HEADS
ML_DV = D_MODEL // ML_HEADS
ML_QK_W = ML_HEADS * ML_DK
ML_V_W = ML_HEADS * ML_DV

MEM_HEADS = 4
MEM_HD = 128
MEM_W = MEM_HEADS * MEM_HD

REL_BUCKETS = 32
REL_MAX_DIST = 128

F32 = jnp.float32
BF16 = jnp.bfloat16
LANES = 128
NEG_INF = float("-inf")
TINY = float(np.finfo(np.float32).tiny)
EXP_CLAMP = 80.0
VMEM_LIMIT = 56 * 1024 * 1024

HG_HB = 4
ML_HB = 2
HG_SUB = 16
SAMPLE_PAD_T = 16

EVEN_A = {"qa": 0, "fa": HG_W, "ia": 2 * HG_W, "za": 3 * HG_W, "qb": 4 * HG_W}
EVEN_A_N = 4 * HG_W + NSA_W
EVEN_B = {"zb": 0, "qm": NSA_W, "gb": NSA_W + MEM_W}
EVEN_B_N = NSA_W + MEM_W + NSA_KVH * LANES
EVEN_KV_OFF = EVEN_A_N
ODD_A = {"q": 0, "k": ML_QK_W, "v": 2 * ML_QK_W, "og": 2 * ML_QK_W + ML_V_W}
ODD_A_N = 2 * ML_QK_W + 2 * ML_V_W
ODD_B = {"z": 0, "qm": ML_V_W, "gates": ML_V_W + MEM_W}
ODD_B_N = ML_V_W + MEM_W + (ML_HEADS // ML_HB) * LANES


def _dot(a, b):
    return jnp.dot(a, b, preferred_element_type=F32)


def _dot_nt(a, b):
    return lax.dot_general(a, b, (((1,), (1,)), ((), ())), preferred_element_type=F32)


def _dot_tn(a, b):
    return lax.dot_general(a, b, (((0,), (0,)), ((), ())), preferred_element_type=F32)


def _iota2(shape, dim):
    return lax.broadcasted_iota(jnp.int32, shape, dim)


def _cumsum_rows(x, tri_b):
    hi = x.astype(BF16)
    r1 = x - hi.astype(F32)
    mid = r1.astype(BF16)
    lo = (r1 - mid.astype(F32)).astype(BF16)
    return _dot(tri_b, hi) + _dot(tri_b, mid) + _dot(tri_b, lo)


def _row_to_col(row, n):
    eye = _iota2((n, n), 0) == _iota2((n, n), 1)
    return jnp.sum(jnp.where(eye, row, 0.0), axis=1, keepdims=True)


def _col_to_row(col, n):
    eye = _iota2((n, n), 0) == _iota2((n, n), 1)
    return jnp.sum(jnp.where(eye, col, 0.0), axis=0, keepdims=True)


def _silu(x):
    return x * jax.nn.sigmoid(x)


def _params(sem):
    return pltpu.CompilerParams(dimension_semantics=sem, vmem_limit_bytes=VMEM_LIMIT)


def _rmsnorm_body(x_ref, w_ref, o_ref):
    x = x_ref[...].astype(F32)
    y = x * lax.rsqrt(jnp.mean(x * x, axis=-1, keepdims=True) + EPS)
    o_ref[...] = (y * w_ref[...].astype(F32)).astype(o_ref.dtype)


def _rmsnorm_rows(x2d, w, out_dtype, tm=256):
    m, d = x2d.shape
    tm = min(tm, m)
    return pl.pallas_call(
        _rmsnorm_body,
        grid=(m // tm,),
        in_specs=[pl.BlockSpec((tm, d), lambda i: (i, 0)), pl.BlockSpec((1, d), lambda i: (0, 0))],
        out_specs=pl.BlockSpec((tm, d), lambda i: (i, 0)),
        out_shape=jax.ShapeDtypeStruct((m, d), out_dtype),
        compiler_params=_params(("parallel",)),
        name="rmsnorm",
    )(x2d, w.reshape(1, d))


def _matmul_body(a_ref, b_ref, o_ref):
    o_ref[...] = _dot(a_ref[...], b_ref[...])


def _matmul(a, b, tm=1024, tn=1024):
    m, k = a.shape
    _, n = b.shape
    tm, tn = min(tm, m), min(tn, n)
    assert m % tm == 0 and n % tn == 0, (a.shape, b.shape)
    return pl.pallas_call(
        _matmul_body,
        grid=(m // tm, n // tn),
        in_specs=[pl.BlockSpec((tm, k), lambda i, j: (i, 0)), pl.BlockSpec((k, tn), lambda i, j: (0, j))],
        out_specs=pl.BlockSpec((tm, tn), lambda i, j: (i, j)),
        out_shape=jax.ShapeDtypeStruct((m, n), F32),
        compiler_params=_params(("parallel", "parallel")),
        name="matmul",
    )(a, b)


def _matmul_heads_body(a_ref, b_ref, o32_ref, o16_ref):
    acc = _dot(a_ref[...], b_ref[...])
    for h in range(MEM_HEADS):
        o32_ref[:, h, :] = acc[:, h * LANES:(h + 1) * LANES]
    o16_ref[...] = acc.astype(BF16)


def _matmul_heads(a, b, tm=1024):
    m, k = a.shape
    n = b.shape[1]
    tm = min(tm, m)
    assert m % tm == 0 and n == MEM_HEADS * LANES, (a.shape, b.shape)
    return pl.pallas_call(
        _matmul_heads_body,
        grid=(m // tm,),
        in_specs=[pl.BlockSpec((tm, k), lambda i: (i, 0)), pl.BlockSpec((k, n), lambda i: (0, 0))],
        out_specs=[pl.BlockSpec((tm, MEM_HEADS, LANES), lambda i: (i, 0, 0)), pl.BlockSpec((tm, n), lambda i: (i, 0))],
        out_shape=[jax.ShapeDtypeStruct((m, MEM_HEADS, LANES), F32), jax.ShapeDtypeStruct((m, n), BF16)],
        compiler_params=_params(("parallel",)),
        name="matmul_heads",
    )(a, b)


def _outproj_body(*refs, widths):
    xs = refs[:len(widths)]
    w_ref, r_ref, o_ref = refs[len(widths):]
    acc = r_ref[...]
    off = 0
    for x_ref, w in zip(xs, widths):
        acc = acc + _dot(x_ref[...], w_ref[off:off + w, :])
        off += w
    o_ref[...] = acc


def _outproj(xs, w_bf16, resid, tm=1024, tn=512):
    m = resid.shape[0]
    n = w_bf16.shape[1]
    widths = tuple(x.shape[1] for x in xs)
    assert sum(widths) == w_bf16.shape[0]
    tm = min(tm, m)
    in_specs = [pl.BlockSpec((tm, w), lambda i, j: (i, 0)) for w in widths]
    in_specs += [pl.BlockSpec((w_bf16.shape[0], tn), lambda i, j: (0, j)), pl.BlockSpec((tm, tn), lambda i, j: (i, j))]
    return pl.pallas_call(
        functools.partial(_outproj_body, widths=widths),
        grid=(m // tm, n // tn),
        in_specs=in_specs,
        out_specs=pl.BlockSpec((tm, tn), lambda i, j: (i, j)),
        out_shape=jax.ShapeDtypeStruct((m, n), F32),
        compiler_params=_params(("parallel", "parallel")),
        name="outproj",
    )(*xs, w_bf16, resid)


def _hgrn_body(qa_ref, fa_ref, ia_ref, za_ref, lb_ref, gn_ref, s0_ref, o_ref, s_out, s_scr, *, L, valid):
    c = pl.program_id(2)

    @pl.when(c == 0)
    def _():
        s_scr[...] = s0_ref[...]

    lb = lb_ref[...]
    sig = jax.nn.sigmoid(fa_ref[...])
    logf = jnp.log(lb + (1.0 - lb) * sig)
    kk = (1.0 - lb) * (1.0 - sig)
    if valid < L:
        live = _iota2((L, 1), 0) < valid
        logf = jnp.where(live, logf, 0.0)
        kk = jnp.where(live, kk, 0.0)
    tri_b = (_iota2((L, L), 0) >= _iota2((L, L), 1)).astype(BF16)
    bc = _cumsum_rows(logf, tri_b)
    q = _silu(qa_ref[...])
    gate = _silu(za_ref[...])
    v = ia_ref[...]
    gn = gn_ref[...]
    nsub = L // HG_SUB
    rr = _iota2((L, nsub * L), 0)
    cc = _iota2((L, nsub * L), 1)
    keep = ((jnp.right_shift(cc, L.bit_length() - 1) == jnp.right_shift(rr, HG_SUB.bit_length() - 1))
            & (jnp.bitwise_and(cc, L - 1) <= rr))
    for j in range(HG_HB):
        sl = slice(j * HG_DK, (j + 1) * HG_DK)
        bj, qj, kj = bc[:, sl], q[:, sl], kk[:, sl]
        vb = v[:, sl].astype(BF16)
        s_prev = s_scr[j]
        inter = _dot((qj * jnp.exp(bj)).astype(BF16), s_prev.astype(BF16))
        mids = [bj[i * HG_SUB + HG_SUB // 2:i * HG_SUB + HG_SUB // 2 + 1, :] for i in range(nsub)]
        mid_rows = jnp.concatenate([jnp.broadcast_to(m, (HG_SUB, HG_DK)) for m in mids], axis=0)
        q_dec = qj * jnp.exp(jnp.minimum(bj - mid_rows, EXP_CLAMP))
        k_dec = jnp.concatenate([kj * jnp.exp(jnp.minimum(m - bj, EXP_CLAMP)) for m in mids], axis=0)
        att = jnp.where(keep, _dot_nt(q_dec.astype(BF16), k_dec.astype(BF16)), 0.0)
        o = inter + _dot(att.astype(BF16), jnp.concatenate([vb] * nsub, axis=0))
        o_n = o * lax.rsqrt(jnp.mean(o * o, axis=-1, keepdims=True) + EPS) * gn
        o_ref[:, sl] = (o_n * gate[:, sl]).astype(o_ref.dtype)
        bl = bj[L - 1:L, :]
        kd = kj * jnp.exp(bl - bj)
        s_scr[j] = _row_to_col(jnp.exp(bl), HG_DK) * s_prev + _dot_tn(kd.astype(BF16), vb)

    @pl.when(c == pl.num_programs(2) - 1)
    def _():
        s_out[...] = s_scr[...]


def _hgrn_call(y, s0, lb, gn, *, B, T, L, valid):
    nc = T // L
    w = HG_HB * HG_DK

    def col(name):
        blk = EVEN_A[name] // w
        return pl.BlockSpec((L, w), lambda b, hg, c: (b * nc + c, blk + hg))

    state_spec = pl.BlockSpec((None, HG_HB, HG_DK, HG_DV), lambda b, hg, c: (b, hg, 0, 0))
    return pl.pallas_call(
        functools.partial(_hgrn_body, L=L, valid=valid),
        grid=(B, HG_HEADS // HG_HB, nc),
        in_specs=[col("qa"), col("fa"), col("ia"), col("za"),
                  pl.BlockSpec((1, w), lambda b, hg, c: (0, hg)),
                  pl.BlockSpec((1, HG_DV), lambda b, hg, c: (0, 0)),
                  state_spec],
        out_specs=[pl.BlockSpec((L, w), lambda b, hg, c: (b * nc + c, hg)), state_spec],
        out_shape=[jax.ShapeDtypeStruct((B * T, HG_W), BF16),
                   jax.ShapeDtypeStruct((B, HG_HEADS, HG_DK, HG_DV), F32)],
        scratch_shapes=[pltpu.VMEM((HG_HB, HG_DK, HG_DV), F32)],
        compiler_params=_params(("arbitrary", "arbitrary", "arbitrary")),
        name="hgrn2",
    )(y, y, y, y, lb.reshape(1, HG_W), gn.reshape(1, HG_DV), s0)


def _mlstm_body(q_ref, k_ref, v_ref, og_ref, z_ref, g_ref, bif_ref, gn_ref, c0_ref, n0_ref, m0_ref,
                h_ref, c_out, n_out, m_out, c_scr, n_scr, m_scr, *, L, valid):
    c = pl.program_id(2)

    @pl.when(c == 0)
    def _():
        c_scr[...] = c0_ref[...]
        n_scr[...] = n0_ref[...]
        m_scr[...] = m0_ref[...]

    gates = g_ref[...] + bif_ref[...]
    log_i = gates
    log_f = jnp.minimum(gates, 0.0) - jnp.log(1.0 + jnp.exp(-jnp.abs(gates)))
    if valid < L:
        live = _iota2((L, 1), 0) < valid
        log_i = jnp.where(live, log_i, -1e30)
        log_f = jnp.where(live, log_f, 0.0)
    tri = _iota2((L, L), 0) >= _iota2((L, L), 1)
    bcs = _cumsum_rows(log_f, tri.astype(BF16))
    for j in range(ML_HB):
        b_col = bcs[:, ML_HB + j:ML_HB + j + 1]
        i_col = log_i[:, j:j + 1]
        b_row = _col_to_row(b_col, L)
        i_row = _col_to_row(i_col, L)
        m_prev = m_scr[:, j:j + 1]
        dmat = jnp.where(tri, b_col - b_row + i_row, NEG_INF)
        inter = b_col + m_prev
        mt = jnp.maximum(inter, jnp.max(dmat, axis=1, keepdims=True))
        w_in = jnp.exp(dmat - mt)
        w_x = jnp.exp(inter - mt)
        qj = q_ref[:, j * ML_DK:(j + 1) * ML_DK]
        kj = k_ref[:, j * ML_DK:(j + 1) * ML_DK] * (ML_DK ** -0.5)
        vj = v_ref[:, j * ML_DV:(j + 1) * ML_DV]
        qb, kb = qj.astype(BF16), kj.astype(BF16)
        sw = _dot_nt(qb, kb) * w_in
        c_prev = c_scr[j]
        n_prev = n_scr[:, j * ML_DK:(j + 1) * ML_DK]
        num = w_x * _dot_nt(qb, c_prev.astype(BF16)) + _dot(sw.astype(BF16), vj.astype(BF16))
        den = w_x * jnp.sum(qj * n_prev, axis=1, keepdims=True) + jnp.sum(sw, axis=1, keepdims=True)
        h = num / jnp.maximum(jnp.abs(den), jnp.exp(-mt))
        m_last = mt[L - 1:L, :]
        b_last = b_col[L - 1:L, :]
        w_end = jnp.exp(b_last - b_col + i_col - m_last)
        d_c = jnp.exp(b_last + m_prev - m_last)
        c_scr[j] = d_c * c_prev + _dot_tn((w_end * vj).astype(BF16), kb)
        n_scr[:, j * ML_DK:(j + 1) * ML_DK] = d_c * n_prev + jnp.sum(w_end * kj, axis=0, keepdims=True)
        m_scr[:, j:j + 1] = m_last
        sv = slice(j * ML_DV, (j + 1) * ML_DV)
        h_n = h * lax.rsqrt(jnp.mean(h * h, axis=-1, keepdims=True) + EPS) * gn_ref[:, sv]
        h_ref[:, sv] = (h_n * jax.nn.sigmoid(og_ref[:, sv]) * _silu(z_ref[:, sv])).astype(h_ref.dtype)

    @pl.when(c == pl.num_programs(2) - 1)
    def _():
        c_out[...] = c_scr[...]
        n_out[...] = n_scr[...]
        m_out[...] = m_scr[...]


def _mlstm_call(ya, yb, c0, n0, m0, bif_r, gn, *, B, T, L, valid):
    nc = T // L
    ng = ML_HEADS // ML_HB
    wk, wv = ML_HB * ML_DK, ML_HB * ML_DV

    def col(name, w):
        blk = (ODD_A[name] if name in ODD_A else ODD_B[name]) // w
        return pl.BlockSpec((L, w), lambda b, hg, c: (b * nc + c, blk + hg))

    c_spec = pl.BlockSpec((None, ML_HB, ML_DV, ML_DK), lambda b, hg, c: (b, hg, 0, 0))
    n_spec = pl.BlockSpec((None, 1, wk), lambda b, hg, c: (b, 0, hg))
    m_spec = pl.BlockSpec((None, None, 1, LANES), lambda b, hg, c: (b, hg, 0, 0))
    m0_r = jnp.pad(m0.reshape(B, ng, 1, ML_HB), ((0, 0), (0, 0), (0, 0), (0, LANES - ML_HB)))
    h, c_new, n_new, m_new = pl.pallas_call(
        functools.partial(_mlstm_body, L=L, valid=valid),
        grid=(B, ng, nc),
        in_specs=[col("q", wk), col("k", wk), col("v", wv), col("og", wv), col("z", wv), col("gates", LANES),
                  pl.BlockSpec((None, 1, LANES), lambda b, hg, c: (hg, 0, 0)),
                  pl.BlockSpec((1, wv), lambda b, hg, c: (0, hg)),
                  c_spec, n_spec, m_spec],
        out_specs=[pl.BlockSpec((L, wv), lambda b, hg, c: (b * nc + c, hg)), c_spec, n_spec, m_spec],
        out_shape=[jax.ShapeDtypeStruct((B * T, ML_V_W), BF16),
                   jax.ShapeDtypeStruct((B, ML_HEADS, ML_DV, ML_DK), F32),
                   jax.ShapeDtypeStruct((B, 1, ML_QK_W), F32),
                   jax.ShapeDtypeStruct((B, ng, 1, LANES), F32)],
        scratch_shapes=[pltpu.VMEM((ML_HB, ML_DV, ML_DK), F32), pltpu.VMEM((1, wk), F32), pltpu.VMEM((1, LANES), F32)],
        compiler_params=_params(("arbitrary", "arbitrary", "arbitrary")),
        name="mlstm",
    )(ya, ya, ya, ya, yb, yb, bif_r, gn.reshape(1, ML_V_W), c0, n0.reshape(B, 1, ML_QK_W), m0_r)
    return h, c_new, n_new.reshape(B, ML_HEADS, ML_DK), m_new[:, :, 0, :ML_HB].reshape(B, ML_HEADS)


def _mem_body(q_ref, k_ref, v_ref, o_ref):
    q = q_ref[...] * (MEM_HD ** -0.5)
    for h in range(MEM_HEADS):
        sl = slice(h * MEM_HD, (h + 1) * MEM_HD)
        s = _dot_nt(q[:, sl].astype(BF16), k_ref[:, sl].astype(BF16))
        p = jnp.exp(s - jnp.max(s, axis=-1, keepdims=True))
        o = _dot(p.astype(BF16), v_ref[:, sl].astype(BF16)) / jnp.sum(p, axis=-1, keepdims=True)
        o_ref[:, sl] = o.astype(o_ref.dtype)


def _mem_call(y, q_off, k2d, v2d, *, B, T, tq=256):
    tq = min(tq, T)
    nq = T // tq
    qb = q_off // MEM_W
    return pl.pallas_call(
        _mem_body,
        grid=(B, nq),
        in_specs=[pl.BlockSpec((tq, MEM_W), lambda b, i: (b * nq + i, qb)),
                  pl.BlockSpec((N_MEM, MEM_W), lambda b, i: (b, 0)),
                  pl.BlockSpec((N_MEM, MEM_W), lambda b, i: (b, 0))],
        out_specs=pl.BlockSpec((tq, MEM_W), lambda b, i: (b * nq + i, 0)),
        out_shape=jax.ShapeDtypeStruct((B * T, MEM_W), BF16),
        compiler_params=_params(("parallel", "parallel")),
        name="mem_attn",
    )(y, k2d, v2d)


def _gelu_tanh(x):
    return 0.5 * x * (1.0 + jnp.tanh(math.sqrt(2.0 / math.pi) * (x + 0.044715 * (x * x * x))))


def _compress_body(x_ref, w1_ref, b1_ref, w2_ref, pe_ref, o_ref, x32, *, nch):
    x32[...] = x_ref[...].astype(F32)
    a = jnp.zeros((nch, NSA_HD), F32)
    b = jnp.zeros((nch, NSA_HD), F32)
    for s in range(CMP_STRIDE):
        r = x32[pl.ds(s, nch, stride=CMP_STRIDE), :]
        a = a + _dot((r + pe_ref[s:s + 1, :]).astype(BF16), w1_ref[s])
        b = b + _dot((r + pe_ref[CMP_STRIDE + s:CMP_STRIDE + s + 1, :]).astype(BF16), w1_ref[CMP_STRIDE + s])
    h = a + pltpu.roll(b, nch - 1, 0) + b1_ref[...]
    o_ref[...] = _dot(_gelu_tanh(h).astype(BF16), w2_ref[...])


def _compress_call(x16, w1, b1, w2, pe, *, B, T):
    nch = T // CMP_STRIDE
    return pl.pallas_call(
        functools.partial(_compress_body, nch=nch),
        grid=(B, NSA_KVH),
        in_specs=[pl.BlockSpec((T, NSA_HD), lambda b, h: (b, h)),
                  pl.BlockSpec((CMP_BLOCK, NSA_HD, NSA_HD), lambda b, h: (0, 0, 0)),
                  pl.BlockSpec((1, NSA_HD), lambda b, h: (0, 0)),
                  pl.BlockSpec((NSA_HD, NSA_HD), lambda b, h: (0, 0)),
                  pl.BlockSpec((CMP_BLOCK, NSA_HD), lambda b, h: (0, 0))],
        out_specs=pl.BlockSpec((None, None, nch, NSA_HD), lambda b, h: (b, h, 0, 0)),
        out_shape=jax.ShapeDtypeStruct((B, NSA_KVH, nch, NSA_HD), F32),
        scratch_shapes=[pltpu.VMEM((T, NSA_HD), F32)],
        compiler_params=_params(("parallel", "parallel")),
        name="nsa_compress",
    )(x16, w1.astype(BF16), b1.reshape(1, NSA_HD), w2.astype(BF16), pe)


def _softmax_rows(s):
    m = jnp.max(s, axis=-1, keepdims=True)
    m = jnp.where(m == NEG_INF, 0.0, m)
    p = jnp.exp(s - m)
    return p, jnp.sum(p, axis=-1, keepdims=True)


def _slc_scores(psum, width, n_slc):
    ncmp = psum.shape[1]
    d = _iota2((ncmp, width), 0) - (SEL_BLOCK // CMP_STRIDE) * _iota2((ncmp, width), 1)
    wgt = jnp.where((d == -1) | (d == 3), 1.0, jnp.where((d >= 0) & (d <= 2), 2.0, 0.0))
    wgt = jnp.where(_iota2((ncmp, width), 1) < n_slc, wgt, 0.0).astype(BF16)
    p_hi = psum.astype(BF16)
    p_lo = (psum - p_hi.astype(F32)).astype(BF16)
    return _dot(p_hi, wgt) + _dot(p_lo, wgt)


def _top_blocks(slc, cur, n_pick):
    rows, width = slc.shape
    blk = _iota2((rows, width), 1)
    forced = (blk == 0) | (blk == cur) | (blk == cur - 1)
    score = jnp.where(forced, jnp.inf, slc)
    score = jnp.where(blk > cur, NEG_INF, score)
    blk_f = blk.astype(F32)
    lane = _iota2((rows, LANES), 1)
    sel = jnp.zeros((rows, width), F32)
    picks = jnp.zeros((rows, LANES), F32)
    for i in range(n_pick):
        mx = jnp.max(score, axis=-1, keepdims=True)
        first = jnp.min(jnp.where(score == mx, blk_f, float(width)), axis=-1, keepdims=True)
        pick = blk_f == first
        sel = jnp.where(pick, 1.0, sel)
        picks = jnp.where(lane == i, first, picks)
        score = jnp.where(pick, NEG_INF, score)
    return sel, picks


def _member_by_rank(psum, tpos_row, n_slc, n_pick):
    nq, ncmp = psum.shape
    nb = -(-n_slc // 8) * 8
    d = _iota2((nb, ncmp), 1) - (SEL_BLOCK // CMP_STRIDE) * _iota2((nb, ncmp), 0)
    wgt = jnp.where((d == -1) | (d == 3), 1.0, jnp.where((d >= 0) & (d <= 2), 2.0, 0.0))
    wgt = jnp.where(_iota2((nb, ncmp), 0) < n_slc, wgt, 0.0).astype(BF16)
    p_hi = psum.astype(BF16)
    p_lo = (psum - p_hi.astype(F32)).astype(BF16)
    slc = _dot_nt(wgt, p_hi) + _dot_nt(wgt, p_lo)
    blk = _iota2((nb, nq), 0)
    cur = jnp.right_shift(tpos_row, SEL_SHIFT)
    forced = (blk == 0) | (blk == cur) | (blk == cur - 1)
    score = jnp.where(forced, jnp.inf, slc)
    score = jnp.where(blk > cur, NEG_INF, score)
    ahead = jnp.zeros((nb, nq), F32)
    for i in range(n_slc):
        s_i = score[i:i + 1, :]
        ahead = ahead + jnp.where((s_i > score) | ((s_i == score) & (blk > i)), 1.0, 0.0)
    return jnp.where((ahead < n_pick) & (blk <= cur), 1.0, 0.0)


def _nsa_prompt_body(q_ref, zb_ref, gb_ref, bg_ref, ks_ref, vs_ref, kw_ref, vw_ref, kc_ref, vc_ref,
                     bc_ref, bs_ref, bw_ref, o_ref, ksp, vsp, kwp, vwp, *, T):
    qi = pl.program_id(2)
    tq = Q_BLOCK
    front = T - tq
    wlen = WINDOW + tq
    n_slc = T // SEL_BLOCK

    @pl.when(qi == 0)
    def _():
        ksp[0:front, :] = jnp.zeros((front, NSA_HD), BF16)
        vsp[0:front, :] = jnp.zeros((front, NSA_HD), BF16)
        ksp[front:front + T, :] = ks_ref[...].astype(BF16)
        vsp[front:front + T, :] = vs_ref[...].astype(BF16)
        kwp[0:WINDOW, :] = jnp.zeros((WINDOW, NSA_HD), BF16)
        vwp[0:WINDOW, :] = jnp.zeros((WINDOW, NSA_HD), BF16)
        kwp[WINDOW:WINDOW + T, :] = kw_ref[...].astype(BF16)
        vwp[WINDOW:WINDOW + T, :] = vw_ref[...].astype(BF16)

    t0 = pl.multiple_of(qi * tq, tq)
    tpos = _iota2((tq, 1), 0) + t0
    q_all = q_ref[...] * (NSA_HD ** -0.5)
    qs = [q_all[:, g * NSA_HD:(g + 1) * NSA_HD].astype(BF16) for g in range(NSA_G)]

    ncmp = T // CMP_STRIDE
    vis = tpos >= _iota2((1, ncmp), 1) * CMP_STRIDE + (CMP_BLOCK - 1)
    kcb = kc_ref[...].astype(BF16)
    vcb = vc_ref[...].astype(BF16)
    psum = jnp.zeros((tq, ncmp), F32)
    o_cmp = []
    for g in range(NSA_G):
        s = jnp.where(vis, _dot_nt(qs[g], kcb) + bc_ref[g], NEG_INF)
        p, l = _softmax_rows(s)
        p = p / jnp.maximum(l, TINY)
        psum = psum + p
        o_cmp.append(_dot(p.astype(BF16), vcb))

    member_t = _member_by_rank(psum, _iota2((1, tq), 1) + t0, n_slc, min(N_SEL, n_slc)).astype(BF16)

    nb = member_t.shape[0]
    col_blk = jnp.right_shift(_iota2((nb, T), 1), SEL_SHIFT) + (qi * (tq // SEL_BLOCK) + (tq - T) // SEL_BLOCK)
    expand = (col_blk == _iota2((nb, T), 0)).astype(BF16)
    kpos = _iota2((1, T), 1) + (t0 + tq - T)
    allowed = (_dot_tn(member_t, expand) > 0.5) & (kpos <= tpos)
    mask_s = jnp.where(allowed, 0.0, NEG_INF)
    k_s = ksp[pl.ds(t0, T), :]
    v_s = vsp[pl.ds(t0, T), :]
    o_sel = []
    for g in range(NSA_G):
        p, l = _softmax_rows(_dot_nt(qs[g], k_s) + bs_ref[g] + mask_s)
        o_sel.append(_dot(p.astype(BF16), v_s) / jnp.maximum(l, TINY))

    dist = WINDOW + _iota2((tq, wlen), 0) - _iota2((tq, wlen), 1)
    in_win = (dist >= 0) & (dist < WINDOW) & (_iota2((1, wlen), 1) + (t0 - WINDOW) >= 0)
    mask_w = jnp.where(in_win, 0.0, NEG_INF)
    k_w = kwp[pl.ds(t0, wlen), :]
    v_w = vwp[pl.ds(t0, wlen), :]
    gate = jax.nn.sigmoid(gb_ref[...] + bg_ref[...])
    zb = _silu(zb_ref[...])
    for g in range(NSA_G):
        p, l = _softmax_rows(_dot_nt(qs[g], k_w) + bw_ref[g] + mask_w)
        o_win = _dot(p.astype(BF16), v_w) / jnp.maximum(l, TINY)
        mix = (gate[:, g:g + 1] * o_cmp[g] + gate[:, NSA_G + g:NSA_G + g + 1] * o_sel[g]
               + gate[:, 2 * NSA_G + g:2 * NSA_G + g + 1] * o_win)
        sl = slice(g * NSA_HD, (g + 1) * NSA_HD)
        o_ref[:, sl] = (mix * zb[:, sl]).astype(o_ref.dtype)


def _nsa_prompt_call(ya, yb, kv16, kcmp, vcmp, bg_r, bias_c, bias_s, bias_w, *, B, T):
    nq = T // Q_BLOCK
    gw = NSA_G * NSA_HD
    wlen = WINDOW + Q_BLOCK
    kv_spec = pl.BlockSpec((T, NSA_HD), lambda b, h, i: (b, h))
    cmp_spec = pl.BlockSpec((None, None, T // CMP_STRIDE, NSA_HD), lambda b, h, i: (b, h, 0, 0))
    return pl.pallas_call(
        functools.partial(_nsa_prompt_body, T=T),
        grid=(B, NSA_KVH, nq),
        in_specs=[pl.BlockSpec((Q_BLOCK, gw), lambda b, h, i: (b * nq + i, EVEN_A["qb"] // gw + h)),
                  pl.BlockSpec((Q_BLOCK, gw), lambda b, h, i: (b * nq + i, EVEN_B["zb"] // gw + h)),
                  pl.BlockSpec((Q_BLOCK, LANES), lambda b, h, i: (b * nq + i, EVEN_B["gb"] // LANES + h)),
                  pl.BlockSpec((None, 1, LANES), lambda b, h, i: (h, 0, 0)),
                  kv_spec, kv_spec, kv_spec, kv_spec, cmp_spec, cmp_spec,
                  pl.BlockSpec((None, NSA_G, Q_BLOCK, T // CMP_STRIDE), lambda b, h, i: (h, 0, i, 0)),
                  pl.BlockSpec((None, NSA_G, Q_BLOCK, T), lambda b, h, i: (h, 0, 0, 0)),
                  pl.BlockSpec((None, NSA_G, Q_BLOCK, wlen), lambda b, h, i: (h, 0, 0, 0))],
        out_specs=pl.BlockSpec((Q_BLOCK, gw), lambda b, h, i: (b * nq + i, h)),
        out_shape=jax.ShapeDtypeStruct((B * T, NSA_W), BF16),
        scratch_shapes=[pltpu.VMEM((2 * T - Q_BLOCK, NSA_HD), BF16), pltpu.VMEM((2 * T - Q_BLOCK, NSA_HD), BF16),
                        pltpu.VMEM((WINDOW + T, NSA_HD), BF16), pltpu.VMEM((WINDOW + T, NSA_HD), BF16)],
        compiler_params=_params(("arbitrary", "arbitrary", "arbitrary")),
        name="nsa_prompt",
    )(ya, yb, yb, bg_r, *kv16, kcmp, vcmp, bias_c, bias_s, bias_w)


CMP_PAGES = 16
CHUNKS_PER_PAGE = PAGE_SIZE // CMP_STRIDE
CHUNK_W = CMP_STRIDE * NSA_KV_W


def _cmp_pages_body(pt_ref, *refs):
    del pt_ref
    pages = refs[:CMP_PAGES]
    w_ref, pe_ref, o_ref = refs[CMP_PAGES:]
    x = jnp.concatenate([r[...] for r in pages], axis=0)
    rows = x.shape[0]
    per_head = [jnp.concatenate([x[:, s * NSA_KV_W + h * NSA_HD:s * NSA_KV_W + (h + 1) * NSA_HD]
                                 for s in range(CMP_STRIDE)], axis=1) for h in range(NSA_KVH)]
    w = w_ref[...]
    r = _dot(jnp.concatenate(per_head, axis=0).astype(BF16), w)
    pc = _dot(pe_ref[...], w)
    r = r + jnp.concatenate([pc[0:1, :NSA_HD], pc[1:2, NSA_HD:]], axis=1)
    for h in range(NSA_KVH):
        o_ref[h] = r[h * rows:(h + 1) * rows]


def _cmp_pages_call(pool, page_table, w1, pe, *, B):
    n_pages = page_table.shape[1]
    rows = CMP_PAGES * CHUNKS_PER_PAGE
    view = pool.reshape(pool.shape[0] * CHUNKS_PER_PAGE, CHUNK_W)
    w = w1.reshape(2, CMP_STRIDE, NSA_HD, NSA_HD).transpose(1, 2, 0, 3).reshape(CMP_STRIDE * NSA_HD, 2 * NSA_HD)
    pe_rows = jnp.pad(pe.reshape(2, CMP_STRIDE * NSA_HD), ((0, 6), (0, 0))).astype(BF16)

    def page_spec(i):
        return pl.BlockSpec((CHUNKS_PER_PAGE, CHUNK_W), lambda b, s, pt: (pt[b * n_pages + s * CMP_PAGES + i], 0))

    grid_spec = pltpu.PrefetchScalarGridSpec(
        num_scalar_prefetch=1,
        grid=(B, n_pages // CMP_PAGES),
        in_specs=[page_spec(i) for i in range(CMP_PAGES)]
        + [pl.BlockSpec((CMP_STRIDE * NSA_HD, 2 * NSA_HD), lambda b, s, pt: (0, 0)),
           pl.BlockSpec((8, CMP_STRIDE * NSA_HD), lambda b, s, pt: (0, 0))],
        out_specs=pl.BlockSpec((None, NSA_KVH, rows, 2 * NSA_HD), lambda b, s, pt: (b, 0, s, 0)),
    )
    return pl.pallas_call(
        _cmp_pages_body,
        grid_spec=grid_spec,
        out_shape=jax.ShapeDtypeStruct((B, NSA_KVH, n_pages * CHUNKS_PER_PAGE, 2 * NSA_HD), F32),
        compiler_params=_params(("arbitrary", "arbitrary")),
        name="nsa_cmp_pages",
    )(page_table.reshape(-1), *([view] * CMP_PAGES), w.astype(BF16), pe_rows)


SLC_LANES = 384


def _sample_q_rows(q_ref):
    q = q_ref[...] * (NSA_HD ** -0.5)
    return jnp.concatenate([q[:, g * NSA_HD:(g + 1) * NSA_HD] for g in range(NSA_G)], axis=0).astype(BF16)


def _nsa_sample_main_body(abk_ref, abv_ref, b1_ref, w2_ref, q_ref, wk_ref, wv_ref, kn_ref, vn_ref, bc_ref, bw_ref,
                          ocmp_ref, owin_ref, idx_ref, *, T, n_slc):
    tp = SAMPLE_PAD_T
    rows = NSA_G * tp
    ncmp = abk_ref.shape[0]

    def compressed(ab_ref, t):
        ab = ab_ref[...]
        h = ab[:, :NSA_HD] + pltpu.roll(ab[:, NSA_HD:], ncmp - 1, 0) + b1_ref[t]
        return _dot(_gelu_tanh(h).astype(BF16), w2_ref[t]).astype(BF16)

    kc, vc = compressed(abk_ref, 0), compressed(abv_ref, 1)
    q = _sample_q_rows(q_ref)
    step = jnp.bitwise_and(_iota2((rows, 1), 0), tp - 1)
    tpos = PAST_LEN + step
    vis = tpos >= _iota2((1, ncmp), 1) * CMP_STRIDE + (CMP_BLOCK - 1)
    p, l = _softmax_rows(jnp.where(vis, _dot_nt(q, kc) + bc_ref[...], NEG_INF))
    p = p / jnp.maximum(l, TINY)
    ocmp_ref[...] = _dot(p.astype(BF16), vc)
    psum = p[0:tp]
    for g in range(1, NSA_G):
        psum = psum + p[g * tp:(g + 1) * tp]
    cur = jnp.right_shift(PAST_LEN + _iota2((tp, 1), 0), SEL_SHIFT)
    _, picks = _top_blocks(_slc_scores(psum, SLC_LANES, n_slc), cur, N_SEL)
    idx_ref[...] = picks.astype(jnp.int32)

    wb = wk_ref.shape[0]
    wlen = bw_ref.shape[1]
    fill = jnp.zeros((wlen - wb - tp, NSA_HD), BF16)
    k_all = jnp.concatenate([wk_ref[...].astype(BF16), kn_ref[...], fill], axis=0)
    v_all = jnp.concatenate([wv_ref[...].astype(BF16), vn_ref[...], fill], axis=0)
    col = _iota2((1, wlen), 1)
    dist = tpos - (PAST_LEN - wb + col)
    in_win = (dist >= 0) & (dist < WINDOW) & (col < wb + T)
    pw, lw = _softmax_rows(jnp.where(in_win, _dot_nt(q, k_all) + bw_ref[...], NEG_INF))
    owin_ref[...] = _dot(pw.astype(BF16), v_all) / jnp.maximum(lw, TINY)


def _nsa_sample_main_call(ya, kw16, vw16, abk, abv, b1, w2, wk, wv, bias_c, bias_w, *, B, T):
    tp = SAMPLE_PAD_T
    rows = NSA_G * tp
    gw = NSA_G * NSA_HD
    ncmp = abk.shape[2]
    wb = wk.shape[1]
    wlen = bias_w.shape[-1]
    n_slc = -(-(PAST_LEN + T) // SEL_BLOCK)
    assert n_slc <= SLC_LANES and T <= tp
    ab_spec = pl.BlockSpec((None, None, ncmp, 2 * NSA_HD), lambda b, h: (b, h, 0, 0))
    win_spec = pl.BlockSpec((wb, NSA_HD), lambda b, h: (b, h))
    o_spec = pl.BlockSpec((None, None, rows, NSA_HD), lambda b, h: (b, h, 0, 0))
    return pl.pallas_call(
        functools.partial(_nsa_sample_main_body, T=T, n_slc=n_slc),
        grid=(B, NSA_KVH),
        in_specs=[ab_spec, ab_spec,
                  pl.BlockSpec((2, 1, NSA_HD), lambda b, h: (0, 0, 0)),
                  pl.BlockSpec((2, NSA_HD, NSA_HD), lambda b, h: (0, 0, 0)),
                  pl.BlockSpec((tp, gw), lambda b, h: (b, EVEN_A["qb"] // gw + h)),
                  win_spec, win_spec,
                  pl.BlockSpec((tp, NSA_HD), lambda b, h: (b, h)),
                  pl.BlockSpec((tp, NSA_HD), lambda b, h: (b, h)),
                  pl.BlockSpec((None, rows, ncmp), lambda b, h: (h, 0, 0)),
                  pl.BlockSpec((None, rows, wlen), lambda b, h: (h, 0, 0))],
        out_specs=[o_spec, o_spec, pl.BlockSpec((None, None, tp, LANES), lambda b, h: (b, h, 0, 0))],
        out_shape=[jax.ShapeDtypeStruct((B, NSA_KVH, rows, NSA_HD), F32),
                   jax.ShapeDtypeStruct((B, NSA_KVH, rows, NSA_HD), F32),
                   jax.ShapeDtypeStruct((B, NSA_KVH, tp, LANES), jnp.int32)],
        compiler_params=_params(("parallel", "parallel")),
        name="nsa_sample_main",
    )(abk, abv, b1.reshape(2, 1, NSA_HD), w2.astype(BF16), ya,
      wk.reshape(B * wb, NSA_KV_W), wv.reshape(B * wb, NSA_KV_W), kw16, vw16, bias_c, bias_w)


NEAR_BLOCKS = 3


def _nsa_sample_sel_body(idx_ref, pt_ref, q_ref, kn_ref, vn_ref, tbl_ref, ocmp_ref, owin_ref, gb_ref, bg_ref, zb_ref,
                         *refs, T):
    del pt_ref
    k_blocks = refs[:N_SEL]
    v_blocks = refs[N_SEL:2 * N_SEL]
    o_ref, osel = refs[2 * N_SEL:]
    tp = SAMPLE_PAD_T
    rows = NSA_G * tp
    b, h, t = pl.program_id(0), pl.program_id(1), pl.program_id(2)
    base = ((b * NSA_KVH + h) * T + t) * N_SEL
    first_new = PAST_LEN // SEL_BLOCK
    cur = jnp.right_shift(PAST_LEN + t, SEL_SHIFT)
    q = _sample_q_rows(q_ref)
    pad = jnp.zeros((SEL_BLOCK - tp, NSA_HD), BF16)
    k_new = jnp.concatenate([kn_ref[...], pad], axis=0)
    v_new = jnp.concatenate([vn_ref[...], pad], axis=0)
    lane = _iota2((1, LANES), 1)
    low = lane < SEL_BLOCK
    within = jnp.bitwise_and(lane, SEL_BLOCK - 1)
    ks, vs, bias, kpos = [], [], [], []
    for i in range(0, N_SEL, 2):
        pair_bias, pair_pos = [], []
        for j in (i, i + 1):
            blk = idx_ref[base + j]
            is_new = blk >= first_new
            ks.append(jnp.where(is_new, k_new, k_blocks[j][...].astype(BF16)))
            vs.append(jnp.where(is_new, v_new, v_blocks[j][...].astype(BF16)))
            pair_bias.append(tbl_ref[jnp.clip(blk - (first_new - NEAR_BLOCKS), 0, NEAR_BLOCKS)])
            pair_pos.append(jnp.where(blk <= cur, blk * SEL_BLOCK, PAST_LEN + SEL_BLOCK * LANES) + within)
        bias.append(jnp.where(low, pair_bias[0], pair_bias[1]))
        kpos.append(jnp.where(low, pair_pos[0], pair_pos[1]))
    k_all = jnp.concatenate(ks, axis=0)
    v_all = jnp.concatenate(vs, axis=0)
    step = jnp.bitwise_and(_iota2((rows, 1), 0), tp - 1)
    ok = jnp.concatenate(kpos, axis=1) <= PAST_LEN + step
    p, l = _softmax_rows(jnp.where(ok, _dot_nt(q, k_all) + jnp.concatenate(bias, axis=1), NEG_INF))
    o = _dot(p.astype(BF16), v_all) / jnp.maximum(l, TINY)

    @pl.when(t == 0)
    def _():
        osel[...] = jnp.zeros_like(osel)

    osel[...] = jnp.where(step == t, o, osel[...])

    @pl.when(t == T - 1)
    def _():
        gate = jax.nn.sigmoid(gb_ref[...] + bg_ref[...])
        zb = _silu(zb_ref[...])
        for g in range(NSA_G):
            r = slice(g * tp, (g + 1) * tp)
            mix = (gate[:, g:g + 1] * ocmp_ref[r, :] + gate[:, NSA_G + g:NSA_G + g + 1] * osel[r, :]
                   + gate[:, 2 * NSA_G + g:2 * NSA_G + g + 1] * owin_ref[r, :])
            sl = slice(g * NSA_HD, (g + 1) * NSA_HD)
            o_ref[:, sl] = (mix * zb[:, sl]).astype(o_ref.dtype)


def _nsa_sample_sel_call(ya, yb, ks16, vs16, idx, page_table, pool_k, pool_v, tbl, o_cmp, o_win, bg_r, *, B, T):
    tp = SAMPLE_PAD_T
    rows = NSA_G * tp
    gw = NSA_G * NSA_HD
    n_pages = page_table.shape[1]
    halves = PAGE_SIZE // SEL_BLOCK
    idx_flat = idx[:, :, :T, :N_SEL].reshape(-1)
    view_k = pool_k.reshape(pool_k.shape[0] * halves, SEL_BLOCK, NSA_KV_W)
    view_v = pool_v.reshape(pool_v.shape[0] * halves, SEL_BLOCK, NSA_KV_W)

    def blk_spec(j):
        def index(b, h, t, idx_s, pt_s):
            blk = idx_s[((b * NSA_KVH + h) * T + t) * N_SEL + j]
            page = pt_s[b * n_pages + jnp.minimum(blk // halves, n_pages - 1)]
            return (page * halves + blk % halves, 0, h)
        return pl.BlockSpec((None, SEL_BLOCK, NSA_HD), index)

    o_spec = pl.BlockSpec((None, None, rows, NSA_HD), lambda b, h, t, *_: (b, h, 0, 0))
    grid_spec = pltpu.PrefetchScalarGridSpec(
        num_scalar_prefetch=2,
        grid=(B, NSA_KVH, T),
        in_specs=[pl.BlockSpec((tp, gw), lambda b, h, t, *_: (b, EVEN_A["qb"] // gw + h)),
                  pl.BlockSpec((tp, NSA_HD), lambda b, h, t, *_: (b, h)),
                  pl.BlockSpec((tp, NSA_HD), lambda b, h, t, *_: (b, h)),
                  pl.BlockSpec((None, NEAR_BLOCKS + 1, rows, LANES), lambda b, h, t, *_: (h, 0, 0, 0)),
                  o_spec, o_spec,
                  pl.BlockSpec((tp, LANES), lambda b, h, t, *_: (b, EVEN_B["gb"] // LANES + h)),
                  pl.BlockSpec((None, 1, LANES), lambda b, h, t, *_: (h, 0, 0)),
                  pl.BlockSpec((tp, gw), lambda b, h, t, *_: (b, EVEN_B["zb"] // gw + h))]
        + [blk_spec(j) for j in range(N_SEL)] * 2,
        out_specs=pl.BlockSpec((tp, gw), lambda b, h, t, *_: (b, h)),
        scratch_shapes=[pltpu.VMEM((rows, NSA_HD), F32)],
    )
    return pl.pallas_call(
        functools.partial(_nsa_sample_sel_body, T=T),
        grid_spec=grid_spec,
        out_shape=jax.ShapeDtypeStruct((B * tp, NSA_W), BF16),
        compiler_params=_params(("arbitrary", "arbitrary", "arbitrary")),
        name="nsa_sample_sel",
    )(idx_flat, page_table.reshape(-1), ya, ks16, vs16, tbl, o_cmp, o_win, yb, bg_r, yb,
      *([view_k] * N_SEL), *([view_v] * N_SEL))


def _sample_bias_tables(rel_bias, T, wb):
    tp = SAMPLE_PAD_T
    ncmp = PAST_LEN // CMP_STRIDE
    wlen = -(-(wb + tp) // LANES) * LANES
    first = PAST_LEN // SEL_BLOCK - NEAR_BLOCKS
    assert PAST_LEN - ((first + 1) * SEL_BLOCK - 1) >= REL_MAX_DIST
    lo, hi = -wlen, PAST_LEN + tp
    rev = _bias_line(rel_bias, lo, hi, descending=True)

    def rows(tbl):
        return tbl.reshape(NSA_KVH, NSA_G * tp, tbl.shape[-1])

    t_c = jnp.stack([lax.slice_in_dim(rev, hi - 1 - (PAST_LEN + t - (CMP_BLOCK - 1)),
                                      hi - 1 - (PAST_LEN + t - (CMP_BLOCK - 1)) + CMP_STRIDE * ncmp,
                                      stride=CMP_STRIDE, axis=1) for t in range(tp)], axis=1)
    t_w = jnp.stack([rev[:, hi - 1 - (wb + t):hi - 1 - (wb + t) + wlen] for t in range(tp)], axis=1)
    far = jnp.broadcast_to(rev[:, hi - 1 - REL_MAX_DIST][:, None, None], (NSA_HEADS, tp, LANES))
    near = []
    for k in range(1, NEAR_BLOCKS + 1):
        a = PAST_LEN - (first + k) * SEL_BLOCK
        half = jnp.stack([rev[:, hi - 1 - (a + t):hi - 1 - (a + t) + SEL_BLOCK] for t in range(tp)], axis=1)
        near.append(jnp.concatenate([half, half], axis=-1))
    t_s = jnp.stack([far] + near, axis=1).reshape(NSA_KVH, NSA_G, NEAR_BLOCKS + 1, tp, LANES)
    t_s = t_s.transpose(0, 2, 1, 3, 4).reshape(NSA_KVH, NEAR_BLOCKS + 1, NSA_G * tp, LANES)
    return rows(t_c), rows(t_w), t_s


def _split_even(w):
    kv0 = EVEN_KV_OFF
    g0 = kv0 + 6 * NSA_KV_W
    zb0 = g0 + 3 * NSA_HEADS
    cols = [w[:, zb0:zb0 + NSA_W + MEM_W]]
    for h in range(NSA_KVH):
        for j in range(3):
            cols.append(w[:, g0 + j * NSA_HEADS + h * NSA_G:g0 + j * NSA_HEADS + (h + 1) * NSA_G])
        cols.append(jnp.zeros((w.shape[0], LANES - 3 * NSA_G), w.dtype))
    w_kv = [w[:, kv0 + j * NSA_KV_W:kv0 + (j + 1) * NSA_KV_W].astype(BF16) for j in range(6)]
    return w[:, :EVEN_A_N].astype(BF16), jnp.concatenate(cols, axis=1).astype(BF16), w_kv


def _split_odd(w):
    ig0 = ODD_A_N
    fg0 = ig0 + ML_HEADS
    z0 = fg0 + ML_HEADS
    cols = [w[:, z0:z0 + ML_V_W + MEM_W]]
    for hg in range(ML_HEADS // ML_HB):
        cols.append(w[:, ig0 + hg * ML_HB:ig0 + (hg + 1) * ML_HB])
        cols.append(w[:, fg0 + hg * ML_HB:fg0 + (hg + 1) * ML_HB])
        cols.append(jnp.zeros((w.shape[0], LANES - 2 * ML_HB), w.dtype))
    return w[:, :ODD_A_N].astype(BF16), jnp.concatenate(cols, axis=1).astype(BF16)


def _gate_bias_even(b_gate):
    g = b_gate.reshape(3, NSA_KVH, NSA_G).transpose(1, 0, 2).reshape(NSA_KVH, 1, 3 * NSA_G)
    return jnp.pad(g, ((0, 0), (0, 0), (0, LANES - 3 * NSA_G)))


def _gate_bias_odd(b_if):
    g = b_if.reshape(2, ML_HEADS // ML_HB, ML_HB).transpose(1, 0, 2).reshape(ML_HEADS // ML_HB, 1, 2 * ML_HB)
    return jnp.pad(g, ((0, 0), (0, 0), (0, LANES - 2 * ML_HB)))


def _rel_bucket(dist):
    n = np.maximum(dist, 0)
    exact = REL_BUCKETS // 2
    nf = np.maximum(n, 1).astype(np.float32)
    large = exact + (np.log(nf / exact) / math.log(REL_MAX_DIST / exact) * (REL_BUCKETS - exact)).astype(np.int32)
    return np.where(n < exact, n, np.minimum(large, REL_BUCKETS - 1))


def _bias_line(rel_bias, lo, hi, descending=False):
    dist = np.arange(hi - 1, lo - 1, -1) if descending else np.arange(lo, hi)
    buckets = _rel_bucket(dist)
    edges = np.flatnonzero(np.diff(buckets)) + 1
    starts = np.concatenate([[0], edges])
    ends = np.concatenate([edges, [hi - lo]])
    bias_t = rel_bias.T.astype(F32)
    runs = [jnp.broadcast_to(bias_t[:, int(buckets[s])][:, None], (NSA_HEADS, int(e - s))) for s, e in zip(starts, ends)]
    return jnp.concatenate(runs, axis=1)


def _prompt_bias_tables(rel_bias, T):
    ncmp = T // CMP_STRIDE
    wlen = WINDOW + Q_BLOCK
    lo, hi = -(CMP_STRIDE * ncmp + CMP_BLOCK), T
    line = _bias_line(rel_bias, lo, hi)
    rev = _bias_line(rel_bias, lo, hi, descending=True)

    def split(tbl):
        return tbl.reshape((NSA_KVH, NSA_G) + tbl.shape[1:])

    t_c = jnp.stack([line[:, -(CMP_STRIDE * n + CMP_BLOCK - 1) - lo:-(CMP_STRIDE * n + CMP_BLOCK - 1) - lo + T]
                     for n in range(ncmp)], axis=1).swapaxes(1, 2)
    t_s = jnp.stack([rev[:, hi - 1 - (T - Q_BLOCK + r):hi - 1 - (T - Q_BLOCK + r) + T] for r in range(Q_BLOCK)], axis=1)
    t_w = jnp.stack([rev[:, hi - 1 - (WINDOW + r):hi - 1 - (WINDOW + r) + wlen] for r in range(Q_BLOCK)], axis=1)
    return split(t_c), split(t_s), split(t_w)


def _nsa_sample(ya, yb, kv16, page_table, pk_cmp, pv_cmp, pk_sel, pv_sel, wk, wv, bg_r, w1, b1, w2, pe, rel_bias,
                *, B, T):
    assert (PAST_LEN + T) // CMP_STRIDE == PAST_LEN // CMP_STRIDE
    abk = _cmp_pages_call(pk_cmp, page_table, w1[0], pe[0], B=B)
    abv = _cmp_pages_call(pv_cmp, page_table, w1[1], pe[1], B=B)
    bias_c, bias_w, tbl = _sample_bias_tables(rel_bias, T, wk.shape[1])
    o_cmp, o_win, idx = _nsa_sample_main_call(ya, kv16[4], kv16[5], abk, abv, b1, w2, wk, wv, bias_c, bias_w, B=B, T=T)
    return _nsa_sample_sel_call(ya, yb, kv16[2], kv16[3], idx, page_table, pk_sel, pv_sel, tbl, o_cmp, o_win, bg_r,
                                B=B, T=T)


def _kv_project(x, w_kv):
    outs = [_matmul_heads(x, w) for w in w_kv]
    return [o[0] for o in outs], [o[1] for o in outs]


def _even_prompt(hp2d, npre, mk16, mv16, w_a, w_b, w_kv, bg_r, w1, b1, w2, pe, lb, g_norm, w_out, rel_bias, *, B, T):
    ya, yb = _matmul(npre, w_a), _matmul(npre, w_b)
    kv32, kv16 = _kv_project(npre, w_kv)
    oa, s_new = _hgrn_call(ya, jnp.zeros((B, HG_HEADS, HG_DK, HG_DV), F32), lb, g_norm, B=B, T=T, L=CHUNK, valid=CHUNK)
    kcmp = _compress_call(kv16[0], w1[0], b1[0], w2[0], pe[0], B=B, T=T)
    vcmp = _compress_call(kv16[1], w1[1], b1[1], w2[1], pe[1], B=B, T=T)
    ob = _nsa_prompt_call(ya, yb, kv16[2:], kcmp, vcmp, bg_r, *_prompt_bias_tables(rel_bias, T), B=B, T=T)
    om = _mem_call(yb, EVEN_B["qm"], mk16, mv16, B=B, T=T)
    h_new = _outproj([oa, ob, om], w_out, hp2d)
    wb = min(WINDOW, T)
    rows = [r.reshape(B, T, NSA_KVH, NSA_HD) for r in kv32]
    return h_new, (rows[0], rows[1], rows[2], rows[3], rows[4][:, -wb:], rows[5][:, -wb:], s_new)


def _even_sample(hs2d, nsam, mk_s, mv_s, page_table, pk_cmp, pv_cmp, pk_sel, pv_sel, wk, wv, s0,
                 w_a, w_b, w_kv, bg_r, w1, b1, w2, pe, lb, g_norm, w_out, rel_bias, *, B, T):
    tp = SAMPLE_PAD_T
    ya, yb = _matmul(nsam, w_a), _matmul(nsam, w_b)
    kv32, kv16 = _kv_project(nsam, w_kv)
    oa, s_new = _hgrn_call(ya, s0, lb, g_norm, B=B, T=tp, L=tp, valid=T)
    ob = _nsa_sample(ya, yb, kv16, page_table, pk_cmp, pv_cmp, pk_sel, pv_sel, wk, wv, bg_r, w1, b1, w2, pe, rel_bias,
                     B=B, T=T)
    om = _mem_call(yb, EVEN_B["qm"], mk_s.reshape(B * N_MEM, MEM_W), mv_s.reshape(B * N_MEM, MEM_W), B=B, T=tp)
    rows = [r.reshape(B, tp, NSA_KVH, NSA_HD)[:, :T] for r in kv32]
    wb = wk.shape[1]
    win_k = jnp.concatenate([wk, rows[4]], axis=1)[:, -wb:]
    win_v = jnp.concatenate([wv, rows[5]], axis=1)[:, -wb:]
    return _outproj([oa, ob, om], w_out, hs2d), (rows[0], rows[1], rows[2], rows[3], win_k, win_v, s_new)


def _odd_mix(h2d, hn, k2d, v2d, c0, n0, m0, w_a, w_b, bif_r, g_norm, w_out, *, B, T, L, valid):
    ya, yb = _matmul(hn, w_a), _matmul(hn, w_b)
    h, c_new, n_new, m_new = _mlstm_call(ya, yb, c0, n0, m0, bif_r, g_norm, B=B, T=T, L=L, valid=valid)
    om = _mem_call(yb, ODD_B["qm"], k2d, v2d, B=B, T=T)
    return _outproj([h, om], w_out, h2d), (c_new, n_new, m_new)


def _stack(lst, i):
    return jnp.stack([t[i] for t in lst])


def kernel(x_prompt, x_sample, cache_mem_k, cache_mem_v, cache_cmp_k, cache_cmp_v, cache_sel_k, cache_sel_v,
           cache_win_k, cache_win_v, state_hgrn, state_mlstm_c, state_mlstm_n, state_mlstm_m, page_table,
           mem_prompt, norm_w, mem_norm_w, final_norm_w, rel_bias, w_mem_kv, w_in_even, b_nsa_gate,
           w_cmp1, b_cmp1, w_cmp2, pe_cmp, hgrn_lb_logits, hgrn_norm_w, w_out_even, w_in_odd, b_mlstm_if,
           mlstm_norm_w, w_out_odd):
    bp, tp = x_prompt.shape[:2]
    bs, ts = x_sample.shape[:2]
    tsp = SAMPLE_PAD_T
    lbs = jnp.cumsum(jax.nn.softmax(hgrn_lb_logits.astype(F32), axis=0), axis=0)
    hp = x_prompt.reshape(bp * tp, D_MODEL)
    hs = jnp.pad(x_sample, ((0, 0), (0, tsp - ts), (0, 0))).reshape(bs * tsp, D_MODEL)
    mem2d = mem_prompt.reshape(bp * N_MEM, D_MODEL)
    mem_new, even_p, even_s, odd_p, odd_s = [], [], [], [], []
    for l in range(DEPTH):
        npre = _rmsnorm_rows(hp, norm_w[l], BF16)
        nsam = _rmsnorm_rows(hs, norm_w[l], BF16)
        nmem = _rmsnorm_rows(mem2d, mem_norm_w[l], BF16)
        mk32, mk16 = _matmul_heads(nmem, w_mem_kv[l][:, :MEM_W].astype(BF16))
        mv32, mv16 = _matmul_heads(nmem, w_mem_kv[l][:, MEM_W:].astype(BF16))
        mem_new.append((mk32.reshape(bp, N_MEM, MEM_HEADS, MEM_HD), mv32.reshape(bp, N_MEM, MEM_HEADS, MEM_HD)))
        mk_s, mv_s = cache_mem_k[l], cache_mem_v[l]
        if l % 2 == 0:
            e = l // 2
            w_a, w_b, w_kv = _split_even(w_in_even[e])
            w_out = w_out_even[e].astype(BF16)
            bg_r = _gate_bias_even(b_nsa_gate[e])
            cmpw = (w_cmp1[e].reshape(2, CMP_BLOCK, NSA_HD, NSA_HD), b_cmp1[e], w_cmp2[e], pe_cmp[e])
            hp, st_p = _even_prompt(hp, npre, mk16, mv16, w_a, w_b, w_kv, bg_r, *cmpw, lbs[l], hgrn_norm_w[e], w_out,
                                    rel_bias, B=bp, T=tp)
            hs, st_s = _even_sample(hs, nsam, mk_s, mv_s, page_table, cache_cmp_k[e], cache_cmp_v[e], cache_sel_k[e],
                                    cache_sel_v[e], cache_win_k[e], cache_win_v[e], state_hgrn[e], w_a, w_b, w_kv, bg_r,
                                    *cmpw, lbs[l], hgrn_norm_w[e], w_out, rel_bias, B=bs, T=ts)
            even_p.append(st_p)
            even_s.append(st_s)
        else:
            o = l // 2
            w_a, w_b = _split_odd(w_in_odd[o])
            w_out = w_out_odd[o].astype(BF16)
            bif_r = _gate_bias_odd(b_mlstm_if[o])
            hp, st_p = _odd_mix(hp, npre, mk16, mv16, jnp.zeros((bp, ML_HEADS, ML_DV, ML_DK), F32),
                                jnp.zeros((bp, ML_HEADS, ML_DK), F32), jnp.zeros((bp, ML_HEADS), F32),
                                w_a, w_b, bif_r, mlstm_norm_w[o], w_out, B=bp, T=tp, L=CHUNK, valid=CHUNK)
            hs, st_s = _odd_mix(hs, nsam, mk_s.reshape(bs * N_MEM, MEM_W), mv_s.reshape(bs * N_MEM, MEM_W),
                                state_mlstm_c[o], state_mlstm_n[o], state_mlstm_m[o],
                                w_a, w_b, bif_r, mlstm_norm_w[o], w_out, B=bs, T=tsp, L=tsp, valid=ts)
            odd_p.append(st_p)
            odd_s.append(st_s)
    y_prompt = _rmsnorm_rows(hp, final_norm_w, F32).reshape(bp, tp, D_MODEL)
    y_sample = _rmsnorm_rows(hs, final_norm_w, F32).reshape(bs, tsp, D_MODEL)[:, :ts]
    return (y_prompt, y_sample,
            _stack(mem_new, 0), _stack(mem_new, 1),
            _stack(even_p, 0), _stack(even_p, 1), _stack(even_p, 2), _stack(even_p, 3),
            _stack(even_p, 4), _stack(even_p, 5), _stack(even_p, 6),
            _stack(odd_p, 0), _stack(odd_p, 1), _stack(odd_p, 2),
            _stack(even_s, 0), _stack(even_s, 1), _stack(even_s, 2), _stack(even_s, 3),
            _stack(even_s, 4), _stack(even_s, 5), _stack(even_s, 6),
            _stack(odd_s, 0), _stack(odd_s, 1), _stack(odd_s, 2))
```

```python
import functools
import math

import jax
import jax.numpy as jnp
import numpy as np
from jax import lax
from jax.experimental import pallas as pl
from jax.experimental.pallas import tpu as pltpu

D_MODEL = 4096
DEPTH = 2
PAST_LEN = 16384
PAGE_SIZE = 128
N_MEM = 256
EPS = 1e-6
CHUNK = 64

HG_DK = 128
HG_DV = 128
HG_HEADS = D_MODEL // 2 // HG_DV
HG_W = HG_HEADS * HG_DV

NSA_HD = 128
NSA_HEADS = D_MODEL // 2 // NSA_HD
NSA_KVH = 4
NSA_G = NSA_HEADS // NSA_KVH
NSA_W = NSA_HEADS * NSA_HD
NSA_KV_W = NSA_KVH * NSA_HD
CMP_BLOCK = 32
CMP_STRIDE = 16
SEL_BLOCK = 64
SEL_SHIFT = SEL_BLOCK.bit_length() - 1
N_SEL = 16
WINDOW = 512
Q_BLOCK = 128

ML_HEADS = D_MODEL // 512
ML_DK = D_MODEL // 2 // ML_HEADS
ML_DV = D_MODEL // ML_HEADS
ML_QK_W = ML_HEADS * ML_DK
ML_V_W = ML_HEADS * ML_DV

MEM_HEADS = 4
MEM_HD = 128
MEM_W = MEM_HEADS * MEM_HD

REL_BUCKETS = 32
REL_MAX_DIST = 128

F32 = jnp.float32
BF16 = jnp.bfloat16
LANES = 128
NEG_INF = float("-inf")
TINY = float(np.finfo(np.float32).tiny)
EXP_CLAMP = 80.0
VMEM_LIMIT = 56 * 1024 * 1024

HG_HB = 4
ML_HB = 2
HG_SUB = 16
SAMPLE_PAD_T = 16

EVEN_A = {"qa": 0, "fa": HG_W, "ia": 2 * HG_W, "za": 3 * HG_W, "qb": 4 * HG_W}
EVEN_A_N = 4 * HG_W + NSA_W
EVEN_B = {"zb": 0, "qm": NSA_W, "gb": NSA_W + MEM_W}
EVEN_B_N = NSA_W + MEM_W + NSA_KVH * LANES
EVEN_KV_OFF = EVEN_A_N
ODD_A = {"q": 0, "k": ML_QK_W, "v": 2 * ML_QK_W, "og": 2 * ML_QK_W + ML_V_W}
ODD_A_N = 2 * ML_QK_W + 2 * ML_V_W
ODD_B = {"z": 0, "qm": ML_V_W, "gates": ML_V_W + MEM_W}
ODD_B_N = ML_V_W + MEM_W + (ML_HEADS // ML_HB) * LANES


def _dot(a, b):
    return jnp.dot(a, b, preferred_element_type=F32)


def _dot_nt(a, b):
    return lax.dot_general(a, b, (((1,), (1,)), ((), ())), preferred_element_type=F32)


def _dot_tn(a, b):
    return lax.dot_general(a, b, (((0,), (0,)), ((), ())), preferred_element_type=F32)


def _iota2(shape, dim):
    return lax.broadcasted_iota(jnp.int32, shape, dim)


def _cumsum_rows(x, tri_b):
    hi = x.astype(BF16)
    r1 = x - hi.astype(F32)
    mid = r1.astype(BF16)
    lo = (r1 - mid.astype(F32)).astype(BF16)
    return _dot(tri_b, hi) + _dot(tri_b, mid) + _dot(tri_b, lo)


def _row_to_col(row, n):
    eye = _iota2((n, n), 0) == _iota2((n, n), 1)
    return jnp.sum(jnp.where(eye, row, 0.0), axis=1, keepdims=True)


def _col_to_row(col, n):
    eye = _iota2((n, n), 0) == _iota2((n, n), 1)
    return jnp.sum(jnp.where(eye, col, 0.0), axis=0, keepdims=True)


def _silu(x):
    return x * jax.nn.sigmoid(x)


def _params(sem):
    return pltpu.CompilerParams(dimension_semantics=sem, vmem_limit_bytes=VMEM_LIMIT)


def _rmsnorm_body(x_ref, w_ref, o_ref):
    x = x_ref[...].astype(F32)
    y = x * lax.rsqrt(jnp.mean(x * x, axis=-1, keepdims=True) + EPS)
    o_ref[...] = (y * w_ref[...].astype(F32)).astype(o_ref.dtype)


def _rmsnorm_rows(x2d, w, out_dtype, tm=256):
    m, d = x2d.shape
    tm = min(tm, m)
    return pl.pallas_call(
        _rmsnorm_body,
        grid=(m // tm,),
        in_specs=[pl.BlockSpec((tm, d), lambda i: (i, 0)), pl.BlockSpec((1, d), lambda i: (0, 0))],
        out_specs=pl.BlockSpec((tm, d), lambda i: (i, 0)),
        out_shape=jax.ShapeDtypeStruct((m, d), out_dtype),
        compiler_params=_params(("parallel",)),
        name="rmsnorm",
    )(x2d, w.reshape(1, d))


def _matmul_body(a_ref, b_ref, o_ref):
    o_ref[...] = _dot(a_ref[...], b_ref[...])


def _matmul(a, b, tm=1024, tn=1024):
    m, k = a.shape
    _, n = b.shape
    tm, tn = min(tm, m), min(tn, n)
    assert m % tm == 0 and n % tn == 0, (a.shape, b.shape)
    return pl.pallas_call(
        _matmul_body,
        grid=(m // tm, n // tn),
        in_specs=[pl.BlockSpec((tm, k), lambda i, j: (i, 0)), pl.BlockSpec((k, tn), lambda i, j: (0, j))],
        out_specs=pl.BlockSpec((tm, tn), lambda i, j: (i, j)),
        out_shape=jax.ShapeDtypeStruct((m, n), F32),
        compiler_params=_params(("parallel", "parallel")),
        name="matmul",
    )(a, b)


def _matmul_heads_body(a_ref, b_ref, o32_ref, o16_ref):
    acc = _dot(a_ref[...], b_ref[...])
    for h in range(MEM_HEADS):
        o32_ref[:, h, :] = acc[:, h * LANES:(h + 1) * LANES]
    o16_ref[...] = acc.astype(BF16)


def _matmul_heads(a, b, tm=1024):
    m, k = a.shape
    n = b.shape[1]
    tm = min(tm, m)
    assert m % tm == 0 and n == MEM_HEADS * LANES, (a.shape, b.shape)
    return pl.pallas_call(
        _matmul_heads_body,
        grid=(m // tm,),
        in_specs=[pl.BlockSpec((tm, k), lambda i: (i, 0)), pl.BlockSpec((k, n), lambda i: (0, 0))],
        out_specs=[pl.BlockSpec((tm, MEM_HEADS, LANES), lambda i: (i, 0, 0)), pl.BlockSpec((tm, n), lambda i: (i, 0))],
        out_shape=[jax.ShapeDtypeStruct((m, MEM_HEADS, LANES), F32), jax.ShapeDtypeStruct((m, n), BF16)],
        compiler_params=_params(("parallel",)),
        name="matmul_heads",
    )(a, b)


def _outproj_body(*refs, widths):
    xs = refs[:len(widths)]
    w_ref, r_ref, o_ref = refs[len(widths):]
    acc = r_ref[...]
    off = 0
    for x_ref, w in zip(xs, widths):
        acc = acc + _dot(x_ref[...], w_ref[off:off + w, :])
        off += w
    o_ref[...] = acc


def _outproj(xs, w_bf16, resid, tm=1024, tn=512):
    m = resid.shape[0]
    n = w_bf16.shape[1]
    widths = tuple(x.shape[1] for x in xs)
    assert sum(widths) == w_bf16.shape[0]
    tm = min(tm, m)
    in_specs = [pl.BlockSpec((tm, w), lambda i, j: (i, 0)) for w in widths]
    in_specs += [pl.BlockSpec((w_bf16.shape[0], tn), lambda i, j: (0, j)), pl.BlockSpec((tm, tn), lambda i, j: (i, j))]
    return pl.pallas_call(
        functools.partial(_outproj_body, widths=widths),
        grid=(m // tm, n // tn),
        in_specs=in_specs,
        out_specs=pl.BlockSpec((tm, tn), lambda i, j: (i, j)),
        out_shape=jax.ShapeDtypeStruct((m, n), F32),
        compiler_params=_params(("parallel", "parallel")),
        name="outproj",
    )(*xs, w_bf16, resid)


def _hgrn_body(qa_ref, fa_ref, ia_ref, za_ref, lb_ref, gn_ref, s0_ref, o_ref, s_out, s_scr, *, L, valid):
    c = pl.program_id(2)

    @pl.when(c == 0)
    def _():
        s_scr[...] = s0_ref[...]

    lb = lb_ref[...]
    sig = jax.nn.sigmoid(fa_ref[...])
    logf = jnp.log(lb + (1.0 - lb) * sig)
    kk = (1.0 - lb) * (1.0 - sig)
    if valid < L:
        live = _iota2((L, 1), 0) < valid
        logf = jnp.where(live, logf, 0.0)
        kk = jnp.where(live, kk, 0.0)
    tri_b = (_iota2((L, L), 0) >= _iota2((L, L), 1)).astype(BF16)
    bc = _cumsum_rows(logf, tri_b)
    q = _silu(qa_ref[...])
    gate = _silu(za_ref[...])
    v = ia_ref[...]
    gn = gn_ref[...]
    nsub = L // HG_SUB
    rr = _iota2((L, nsub * L), 0)
    cc = _iota2((L, nsub * L), 1)
    keep = ((jnp.right_shift(cc, L.bit_length() - 1) == jnp.right_shift(rr, HG_SUB.bit_length() - 1))
            & (jnp.bitwise_and(cc, L - 1) <= rr))
    for j in range(HG_HB):
        sl = slice(j * HG_DK, (j + 1) * HG_DK)
        bj, qj, kj = bc[:, sl], q[:, sl], kk[:, sl]
        vb = v[:, sl].astype(BF16)
        s_prev = s_scr[j]
        inter = _dot((qj * jnp.exp(bj)).astype(BF16), s_prev.astype(BF16))
        mids = [bj[i * HG_SUB + HG_SUB // 2:i * HG_SUB + HG_SUB // 2 + 1, :] for i in range(nsub)]
        mid_rows = jnp.concatenate([jnp.broadcast_to(m, (HG_SUB, HG_DK)) for m in mids], axis=0)
        q_dec = qj * jnp.exp(jnp.minimum(bj - mid_rows, EXP_CLAMP))
        k_dec = jnp.concatenate([kj * jnp.exp(jnp.minimum(m - bj, EXP_CLAMP)) for m in mids], axis=0)
        att = jnp.where(keep, _dot_nt(q_dec.astype(BF16), k_dec.astype(BF16)), 0.0)
        o = inter + _dot(att.astype(BF16), jnp.concatenate([vb] * nsub, axis=0))
        o_n = o * lax.rsqrt(jnp.mean(o * o, axis=-1, keepdims=True) + EPS) * gn
        o_ref[:, sl] = (o_n * gate[:, sl]).astype(o_ref.dtype)
        bl = bj[L - 1:L, :]
        kd = kj * jnp.exp(bl - bj)
        s_scr[j] = _row_to_col(jnp.exp(bl), HG_DK) * s_prev + _dot_tn(kd.astype(BF16), vb)

    @pl.when(c == pl.num_programs(2) - 1)
    def _():
        s_out[...] = s_scr[...]


def _hgrn_call(y, s0, lb, gn, *, B, T, L, valid):
    nc = T // L
    w = HG_HB * HG_DK

    def col(name):
        blk = EVEN_A[name] // w
        return pl.BlockSpec((L, w), lambda b, hg, c: (b * nc + c, blk + hg))

    state_spec = pl.BlockSpec((None, HG_HB, HG_DK, HG_DV), lambda b, hg, c: (b, hg, 0, 0))
    return pl.pallas_call(
        functools.partial(_hgrn_body, L=L, valid=valid),
        grid=(B, HG_HEADS // HG_HB, nc),
        in_specs=[col("qa"), col("fa"), col("ia"), col("za"),
                  pl.BlockSpec((1, w), lambda b, hg, c: (0, hg)),
                  pl.BlockSpec((1, HG_DV), lambda b, hg, c: (0, 0)),
                  state_spec],
        out_specs=[pl.BlockSpec((L, w), lambda b, hg, c: (b * nc + c, hg)), state_spec],
        out_shape=[jax.ShapeDtypeStruct((B * T, HG_W), BF16),
                   jax.ShapeDtypeStruct((B, HG_HEADS, HG_DK, HG_DV), F32)],
        scratch_shapes=[pltpu.VMEM((HG_HB, HG_DK, HG_DV), F32)],
        compiler_params=_params(("arbitrary", "arbitrary", "arbitrary")),
        name="hgrn2",
    )(y, y, y, y, lb.reshape(1, HG_W), gn.reshape(1, HG_DV), s0)


def _mlstm_body(q_ref, k_ref, v_ref, og_ref, z_ref, g_ref, bif_ref, gn_ref, c0_ref, n0_ref, m0_ref,
                h_ref, c_out, n_out, m_out, c_scr, n_scr, m_scr, *, L, valid):
    c = pl.program_id(2)

    @pl.when(c == 0)
    def _():
        c_scr[...] = c0_ref[...]
        n_scr[...] = n0_ref[...]
        m_scr[...] = m0_ref[...]

    gates = g_ref[...] + bif_ref[...]
    log_i = gates
    log_f = jnp.minimum(gates, 0.0) - jnp.log(1.0 + jnp.exp(-jnp.abs(gates)))
    if valid < L:
        live = _iota2((L, 1), 0) < valid
        log_i = jnp.where(live, log_i, -1e30)
        log_f = jnp.where(live, log_f, 0.0)
    tri = _iota2((L, L), 0) >= _iota2((L, L), 1)
    bcs = _cumsum_rows(log_f, tri.astype(BF16))
    for j in range(ML_HB):
        b_col = bcs[:, ML_HB + j:ML_HB + j + 1]
        i_col = log_i[:, j:j + 1]
        b_row = _col_to_row(b_col, L)
        i_row = _col_to_row(i_col, L)
        m_prev = m_scr[:, j:j + 1]
        dmat = jnp.where(tri, b_col - b_row + i_row, NEG_INF)
        inter = b_col + m_prev
        mt = jnp.maximum(inter, jnp.max(dmat, axis=1, keepdims=True))
        w_in = jnp.exp(dmat - mt)
        w_x = jnp.exp(inter - mt)
        qj = q_ref[:, j * ML_DK:(j + 1) * ML_DK]
        kj = k_ref[:, j * ML_DK:(j + 1) * ML_DK] * (ML_DK ** -0.5)
        vj = v_ref[:, j * ML_DV:(j + 1) * ML_DV]
        qb, kb = qj.astype(BF16), kj.astype(BF16)
        sw = _dot_nt(qb, kb) * w_in
        c_prev = c_scr[j]
        n_prev = n_scr[:, j * ML_DK:(j + 1) * ML_DK]
        num = w_x * _dot_nt(qb, c_prev.astype(BF16)) + _dot(sw.astype(BF16), vj.astype(BF16))
        den = w_x * jnp.sum(qj * n_prev, axis=1, keepdims=True) + jnp.sum(sw, axis=1, keepdims=True)
        h = num / jnp.maximum(jnp.abs(den), jnp.exp(-mt))
        m_last = mt[L - 1:L, :]
        b_last = b_col[L - 1:L, :]
        w_end = jnp.exp(b_last - b_col + i_col - m_last)
        d_c = jnp.exp(b_last + m_prev - m_last)
        c_scr[j] = d_c * c_prev + _dot_tn((w_end * vj).astype(BF16), kb)
        n_scr[:, j * ML_DK:(j + 1) * ML_DK] = d_c * n_prev + jnp.sum(w_end * kj, axis=0, keepdims=True)
        m_scr[:, j:j + 1] = m_last
        sv = slice(j * ML_DV, (j + 1) * ML_DV)
        h_n = h * lax.rsqrt(jnp.mean(h * h, axis=-1, keepdims=True) + EPS) * gn_ref[:, sv]
        h_ref[:, sv] = (h_n * jax.nn.sigmoid(og_ref[:, sv]) * _silu(z_ref[:, sv])).astype(h_ref.dtype)

    @pl.when(c == pl.num_programs(2) - 1)
    def _():
        c_out[...] = c_scr[...]
        n_out[...] = n_scr[...]
        m_out[...] = m_scr[...]


def _mlstm_call(ya, yb, c0, n0, m0, bif_r, gn, *, B, T, L, valid):
    nc = T // L
    ng = ML_HEADS // ML_HB
    wk, wv = ML_HB * ML_DK, ML_HB * ML_DV

    def col(name, w):
        blk = (ODD_A[name] if name in ODD_A else ODD_B[name]) // w
        return pl.BlockSpec((L, w), lambda b, hg, c: (b * nc + c, blk + hg))

    c_spec = pl.BlockSpec((None, ML_HB, ML_DV, ML_DK), lambda b, hg, c: (b, hg, 0, 0))
    n_spec = pl.BlockSpec((None, 1, wk), lambda b, hg, c: (b, 0, hg))
    m_spec = pl.BlockSpec((None, None, 1, LANES), lambda b, hg, c: (b, hg, 0, 0))
    m0_r = jnp.pad(m0.reshape(B, ng, 1, ML_HB), ((0, 0), (0, 0), (0, 0), (0, LANES - ML_HB)))
    h, c_new, n_new, m_new = pl.pallas_call(
        functools.partial(_mlstm_body, L=L, valid=valid),
        grid=(B, ng, nc),
        in_specs=[col("q", wk), col("k", wk), col("v", wv), col("og", wv), col("z", wv), col("gates", LANES),
                  pl.BlockSpec((None, 1, LANES), lambda b, hg, c: (hg, 0, 0)),
                  pl.BlockSpec((1, wv), lambda b, hg, c: (0, hg)),
                  c_spec, n_spec, m_spec],
        out_specs=[pl.BlockSpec((L, wv), lambda b, hg, c: (b * nc + c, hg)), c_spec, n_spec, m_spec],
        out_shape=[jax.ShapeDtypeStruct((B * T, ML_V_W), BF16),
                   jax.ShapeDtypeStruct((B, ML_HEADS, ML_DV, ML_DK), F32),
                   jax.ShapeDtypeStruct((B, 1, ML_QK_W), F32),
                   jax.ShapeDtypeStruct((B, ng, 1, LANES), F32)],
        scratch_shapes=[pltpu.VMEM((ML_HB, ML_DV, ML_DK), F32), pltpu.VMEM((1, wk), F32), pltpu.VMEM((1, LANES), F32)],
        compiler_params=_params(("arbitrary", "arbitrary", "arbitrary")),
        name="mlstm",
    )(ya, ya, ya, ya, yb, yb, bif_r, gn.reshape(1, ML_V_W), c0, n0.reshape(B, 1, ML_QK_W), m0_r)
    return h, c_new, n_new.reshape(B, ML_HEADS, ML_DK), m_new[:, :, 0, :ML_HB].reshape(B, ML_HEADS)


def _mem_body(q_ref, k_ref, v_ref, o_ref):
    q = q_ref[...] * (MEM_HD ** -0.5)
    for h in range(MEM_HEADS):
        sl = slice(h * MEM_HD, (h + 1) * MEM_HD)
        s = _dot_nt(q[:, sl].astype(BF16), k_ref[:, sl].astype(BF16))
        p = jnp.exp(s - jnp.max(s, axis=-1, keepdims=True))
        o = _dot(p.astype(BF16), v_ref[:, sl].astype(BF16)) / jnp.sum(p, axis=-1, keepdims=True)
        o_ref[:, sl] = o.astype(o_ref.dtype)


def _mem_call(y, q_off, k2d, v2d, *, B, T, tq=256):
    tq = min(tq, T)
    nq = T // tq
    qb = q_off // MEM_W
    return pl.pallas_call(
        _mem_body,
        grid=(B, nq),
        in_specs=[pl.BlockSpec((tq, MEM_W), lambda b, i: (b * nq + i, qb)),
                  pl.BlockSpec((N_MEM, MEM_W), lambda b, i: (b, 0)),
                  pl.BlockSpec((N_MEM, MEM_W), lambda b, i: (b, 0))],
        out_specs=pl.BlockSpec((tq, MEM_W), lambda b, i: (b * nq + i, 0)),
        out_shape=jax.ShapeDtypeStruct((B * T, MEM_W), BF16),
        compiler_params=_params(("parallel", "parallel")),
        name="mem_attn",
    )(y, k2d, v2d)


def _gelu_tanh(x):
    return 0.5 * x * (1.0 + jnp.tanh(math.sqrt(2.0 / math.pi) * (x + 0.044715 * (x * x * x))))


def _compress_body(x_ref, w1_ref, b1_ref, w2_ref, pe_ref, o_ref, x32, *, nch):
    x32[...] = x_ref[...].astype(F32)
    a = jnp.zeros((nch, NSA_HD), F32)
    b = jnp.zeros((nch, NSA_HD), F32)
    for s in range(CMP_STRIDE):
        r = x32[pl.ds(s, nch, stride=CMP_STRIDE), :]
        a = a + _dot((r + pe_ref[s:s + 1, :]).astype(BF16), w1_ref[s])
        b = b + _dot((r + pe_ref[CMP_STRIDE + s:CMP_STRIDE + s + 1, :]).astype(BF16), w1_ref[CMP_STRIDE + s])
    h = a + pltpu.roll(b, nch - 1, 0) + b1_ref[...]
    o_ref[...] = _dot(_gelu_tanh(h).astype(BF16), w2_ref[...])


def _compress_call(x16, w1, b1, w2, pe, *, B, T):
    nch = T // CMP_STRIDE
    return pl.pallas_call(
        functools.partial(_compress_body, nch=nch),
        grid=(B, NSA_KVH),
        in_specs=[pl.BlockSpec((T, NSA_HD), lambda b, h: (b, h)),
                  pl.BlockSpec((CMP_BLOCK, NSA_HD, NSA_HD), lambda b, h: (0, 0, 0)),
                  pl.BlockSpec((1, NSA_HD), lambda b, h: (0, 0)),
                  pl.BlockSpec((NSA_HD, NSA_HD), lambda b, h: (0, 0)),
                  pl.BlockSpec((CMP_BLOCK, NSA_HD), lambda b, h: (0, 0))],
        out_specs=pl.BlockSpec((None, None, nch, NSA_HD), lambda b, h: (b, h, 0, 0)),
        out_shape=jax.ShapeDtypeStruct((B, NSA_KVH, nch, NSA_HD), F32),
        scratch_shapes=[pltpu.VMEM((T, NSA_HD), F32)],
        compiler_params=_params(("parallel", "parallel")),
        name="nsa_compress",
    )(x16, w1.astype(BF16), b1.reshape(1, NSA_HD), w2.astype(BF16), pe)


def _softmax_rows(s):
    m = jnp.max(s, axis=-1, keepdims=True)
    m = jnp.where(m == NEG_INF, 0.0, m)
    p = jnp.exp(s - m)
    return p, jnp.sum(p, axis=-1, keepdims=True)


def _slc_scores(psum, width, n_slc):
    ncmp = psum.shape[1]
    d = _iota2((ncmp, width), 0) - (SEL_BLOCK // CMP_STRIDE) * _iota2((ncmp, width), 1)
    wgt = jnp.where((d == -1) | (d == 3), 1.0, jnp.where((d >= 0) & (d <= 2), 2.0, 0.0))
    wgt = jnp.where(_iota2((ncmp, width), 1) < n_slc, wgt, 0.0).astype(BF16)
    p_hi = psum.astype(BF16)
    p_lo = (psum - p_hi.astype(F32)).astype(BF16)
    return _dot(p_hi, wgt) + _dot(p_lo, wgt)


def _top_blocks(slc, cur, n_pick):
    rows, width = slc.shape
    blk = _iota2((rows, width), 1)
    forced = (blk == 0) | (blk == cur) | (blk == cur - 1)
    score = jnp.where(forced, jnp.inf, slc)
    score = jnp.where(blk > cur, NEG_INF, score)
    blk_f = blk.astype(F32)
    lane = _iota2((rows, LANES), 1)
    sel = jnp.zeros((rows, width), F32)
    picks = jnp.zeros((rows, LANES), F32)
    for i in range(n_pick):
        mx = jnp.max(score, axis=-1, keepdims=True)
        first = jnp.min(jnp.where(score == mx, blk_f, float(width)), axis=-1, keepdims=True)
        pick = blk_f == first
        sel = jnp.where(pick, 1.0, sel)
        picks = jnp.where(lane == i, first, picks)
        score = jnp.where(pick, NEG_INF, score)
    return sel, picks


def _member_by_rank(psum, tpos_row, n_slc, n_pick):
    nq, ncmp = psum.shape
    nb = -(-n_slc // 8) * 8
    d = _iota2((nb, ncmp), 1) - (SEL_BLOCK // CMP_STRIDE) * _iota2((nb, ncmp), 0)
    wgt = jnp.where((d == -1) | (d == 3), 1.0, jnp.where((d >= 0) & (d <= 2), 2.0, 0.0))
    wgt = jnp.where(_iota2((nb, ncmp), 0) < n_slc, wgt, 0.0).astype(BF16)
    p_hi = psum.astype(BF16)
    p_lo = (psum - p_hi.astype(F32)).astype(BF16)
    slc = _dot_nt(wgt, p_hi) + _dot_nt(wgt, p_lo)
    blk = _iota2((nb, nq), 0)
    cur = jnp.right_shift(tpos_row, SEL_SHIFT)
    forced = (blk == 0) | (blk == cur) | (blk == cur - 1)
    score = jnp.where(forced, jnp.inf, slc)
    score = jnp.where(blk > cur, NEG_INF, score)
    ahead = jnp.zeros((nb, nq), F32)
    for i in range(n_slc):
        s_i = score[i:i + 1, :]
        ahead = ahead + jnp.where((s_i > score) | ((s_i == score) & (blk > i)), 1.0, 0.0)
    return jnp.where((ahead < n_pick) & (blk <= cur), 1.0, 0.0)


def _nsa_prompt_body(q_ref, zb_ref, gb_ref, bg_ref, ks_ref, vs_ref, kw_ref, vw_ref, kc_ref, vc_ref,
                     bc_ref, bs_ref, bw_ref, o_ref, ksp, vsp, kwp, vwp, *, T):
    qi = pl.program_id(2)
    tq = Q_BLOCK
    front = T - tq
    wlen = WINDOW + tq
    n_slc = T // SEL_BLOCK

    @pl.when(qi == 0)
    def _():
        ksp[0:front, :] = jnp.zeros((front, NSA_HD), BF16)
        vsp[0:front, :] = jnp.zeros((front, NSA_HD), BF16)
        ksp[front:front + T, :] = ks_ref[...].astype(BF16)
        vsp[front:front + T, :] = vs_ref[...].astype(BF16)
        kwp[0:WINDOW, :] = jnp.zeros((WINDOW, NSA_HD), BF16)
        vwp[0:WINDOW, :] = jnp.zeros((WINDOW, NSA_HD), BF16)
        kwp[WINDOW:WINDOW + T, :] = kw_ref[...].astype(BF16)
        vwp[WINDOW:WINDOW + T, :] = vw_ref[...].astype(BF16)

    t0 = pl.multiple_of(qi * tq, tq)
    tpos = _iota2((tq, 1), 0) + t0
    q_all = q_ref[...] * (NSA_HD ** -0.5)
    qs = [q_all[:, g * NSA_HD:(g + 1) * NSA_HD].astype(BF16) for g in range(NSA_G)]

    ncmp = T // CMP_STRIDE
    vis = tpos >= _iota2((1, ncmp), 1) * CMP_STRIDE + (CMP_BLOCK - 1)
    kcb = kc_ref[...].astype(BF16)
    vcb = vc_ref[...].astype(BF16)
    psum = jnp.zeros((tq, ncmp), F32)
    o_cmp = []
    for g in range(NSA_G):
        s = jnp.where(vis, _dot_nt(qs[g], kcb) + bc_ref[g], NEG_INF)
        p, l = _softmax_rows(s)
        p = p / jnp.maximum(l, TINY)
        psum = psum + p
        o_cmp.append(_dot(p.astype(BF16), vcb))

    member_t = _member_by_rank(psum, _iota2((1, tq), 1) + t0, n_slc, min(N_SEL, n_slc)).astype(BF16)

    nb = member_t.shape[0]
    col_blk = jnp.right_shift(_iota2((nb, T), 1), SEL_SHIFT) + (qi * (tq // SEL_BLOCK) + (tq - T) // SEL_BLOCK)
    expand = (col_blk == _iota2((nb, T), 0)).astype(BF16)
    kpos = _iota2((1, T), 1) + (t0 + tq - T)
    allowed = (_dot_tn(member_t, expand) > 0.5) & (kpos <= tpos)
    mask_s = jnp.where(allowed, 0.0, NEG_INF)
    k_s = ksp[pl.ds(t0, T), :]
    v_s = vsp[pl.ds(t0, T), :]
    o_sel = []
    for g in range(NSA_G):
        p, l = _softmax_rows(_dot_nt(qs[g], k_s) + bs_ref[g] + mask_s)
        o_sel.append(_dot(p.astype(BF16), v_s) / jnp.maximum(l, TINY))

    dist = WINDOW + _iota2((tq, wlen), 0) - _iota2((tq, wlen), 1)
    in_win = (dist >= 0) & (dist < WINDOW) & (_iota2((1, wlen), 1) + (t0 - WINDOW) >= 0)
    mask_w = jnp.where(in_win, 0.0, NEG_INF)
    k_w = kwp[pl.ds(t0, wlen), :]
    v_w = vwp[pl.ds(t0, wlen), :]
    gate = jax.nn.sigmoid(gb_ref[...] + bg_ref[...])
    zb = _silu(zb_ref[...])
    for g in range(NSA_G):
        p, l = _softmax_rows(_dot_nt(qs[g], k_w) + bw_ref[g] + mask_w)
        o_win = _dot(p.astype(BF16), v_w) / jnp.maximum(l, TINY)
        mix = (gate[:, g:g + 1] * o_cmp[g] + gate[:, NSA_G + g:NSA_G + g + 1] * o_sel[g]
               + gate[:, 2 * NSA_G + g:2 * NSA_G + g + 1] * o_win)
        sl = slice(g * NSA_HD, (g + 1) * NSA_HD)
        o_ref[:, sl] = (mix * zb[:, sl]).astype(o_ref.dtype)


def _nsa_prompt_call(ya, yb, kv16, kcmp, vcmp, bg_r, bias_c, bias_s, bias_w, *, B, T):
    nq = T // Q_BLOCK
    gw = NSA_G * NSA_HD
    wlen = WINDOW + Q_BLOCK
    kv_spec = pl.BlockSpec((T, NSA_HD), lambda b, h, i: (b, h))
    cmp_spec = pl.BlockSpec((None, None, T // CMP_STRIDE, NSA_HD), lambda b, h, i: (b, h, 0, 0))
    return pl.pallas_call(
        functools.partial(_nsa_prompt_body, T=T),
        grid=(B, NSA_KVH, nq),
        in_specs=[pl.BlockSpec((Q_BLOCK, gw), lambda b, h, i: (b * nq + i, EVEN_A["qb"] // gw + h)),
                  pl.BlockSpec((Q_BLOCK, gw), lambda b, h, i: (b * nq + i, EVEN_B["zb"] // gw + h)),
                  pl.BlockSpec((Q_BLOCK, LANES), lambda b, h, i: (b * nq + i, EVEN_B["gb"] // LANES + h)),
                  pl.BlockSpec((None, 1, LANES), lambda b, h, i: (h, 0, 0)),
                  kv_spec, kv_spec, kv_spec, kv_spec, cmp_spec, cmp_spec,
                  pl.BlockSpec((None, NSA_G, Q_BLOCK, T // CMP_STRIDE), lambda b, h, i: (h, 0, i, 0)),
                  pl.BlockSpec((None, NSA_G, Q_BLOCK, T), lambda b, h, i: (h, 0, 0, 0)),
                  pl.BlockSpec((None, NSA_G, Q_BLOCK, wlen), lambda b, h, i: (h, 0, 0, 0))],
        out_specs=pl.BlockSpec((Q_BLOCK, gw), lambda b, h, i: (b * nq + i, h)),
        out_shape=jax.ShapeDtypeStruct((B * T, NSA_W), BF16),
        scratch_shapes=[pltpu.VMEM((2 * T - Q_BLOCK, NSA_HD), BF16), pltpu.VMEM((2 * T - Q_BLOCK, NSA_HD), BF16),
                        pltpu.VMEM((WINDOW + T, NSA_HD), BF16), pltpu.VMEM((WINDOW + T, NSA_HD), BF16)],
        compiler_params=_params(("arbitrary", "arbitrary", "arbitrary")),
        name="nsa_prompt",
    )(ya, yb, yb, bg_r, *kv16, kcmp, vcmp, bias_c, bias_s, bias_w)


CMP_PAGES = 16
CHUNKS_PER_PAGE = PAGE_SIZE // CMP_STRIDE
PAGE_ROWS = PAGE_SIZE * NSA_KVH


def _pool_rows(pool):
    return pool.reshape(pool.shape[0] * PAGE_ROWS, NSA_HD)


def _cmp_pages_body(pt_ref, *refs):
    del pt_ref
    pages = refs[:CMP_PAGES]
    w_ref, pe_ref, o_ref = refs[CMP_PAGES:]
    rows = CMP_PAGES * CHUNKS_PER_PAGE
    per_head = [jnp.concatenate(
        [jnp.concatenate([pg[pl.ds(NSA_KVH * s + h, CHUNKS_PER_PAGE, stride=CMP_STRIDE * NSA_KVH), :]
                          for s in range(CMP_STRIDE)], axis=1) for pg in pages], axis=0) for h in range(NSA_KVH)]
    w = w_ref[...]
    r = _dot(jnp.concatenate(per_head, axis=0).astype(BF16), w)
    pc = _dot(pe_ref[...], w)
    r = r + jnp.concatenate([pc[0:1, :NSA_HD], pc[1:2, NSA_HD:]], axis=1)
    for h in range(NSA_KVH):
        o_ref[h] = r[h * rows:(h + 1) * rows]


def _cmp_pages_call(pool, page_table, w1, pe, *, B):
    n_pages = page_table.shape[1]
    rows = CMP_PAGES * CHUNKS_PER_PAGE
    view = _pool_rows(pool)
    w = w1.reshape(2, CMP_STRIDE, NSA_HD, NSA_HD).transpose(1, 2, 0, 3).reshape(CMP_STRIDE * NSA_HD, 2 * NSA_HD)
    pe_rows = jnp.pad(pe.reshape(2, CMP_STRIDE * NSA_HD), ((0, 6), (0, 0))).astype(BF16)

    def page_spec(i):
        return pl.BlockSpec((PAGE_ROWS, NSA_HD), lambda b, s, pt: (pt[b * n_pages + s * CMP_PAGES + i], 0))

    grid_spec = pltpu.PrefetchScalarGridSpec(
        num_scalar_prefetch=1,
        grid=(B, n_pages // CMP_PAGES),
        in_specs=[page_spec(i) for i in range(CMP_PAGES)]
        + [pl.BlockSpec((CMP_STRIDE * NSA_HD, 2 * NSA_HD), lambda b, s, pt: (0, 0)),
           pl.BlockSpec((8, CMP_STRIDE * NSA_HD), lambda b, s, pt: (0, 0))],
        out_specs=pl.BlockSpec((None, NSA_KVH, rows, 2 * NSA_HD), lambda b, s, pt: (b, 0, s, 0)),
    )
    return pl.pallas_call(
        _cmp_pages_body,
        grid_spec=grid_spec,
        out_shape=jax.ShapeDtypeStruct((B, NSA_KVH, n_pages * CHUNKS_PER_PAGE, 2 * NSA_HD), F32),
        compiler_params=_params(("arbitrary", "arbitrary")),
        name="nsa_cmp_pages",
    )(page_table.reshape(-1), *([view] * CMP_PAGES), w.astype(BF16), pe_rows)


SLC_LANES = 384


def _sample_q_rows(q_ref):
    q = q_ref[...] * (NSA_HD ** -0.5)
    return jnp.concatenate([q[:, g * NSA_HD:(g + 1) * NSA_HD] for g in range(NSA_G)], axis=0).astype(BF16)


def _nsa_sample_main_body(abk_ref, abv_ref, b1_ref, w2_ref, q_ref, wk_ref, wv_ref, kn_ref, vn_ref, bc_ref, bw_ref,
                          ocmp_ref, owin_ref, idx_ref, *, T, n_slc):
    tp = SAMPLE_PAD_T
    rows = NSA_G * tp
    ncmp = abk_ref.shape[0]

    def compressed(ab_ref, t):
        ab = ab_ref[...]
        h = ab[:, :NSA_HD] + pltpu.roll(ab[:, NSA_HD:], ncmp - 1, 0) + b1_ref[t]
        return _dot(_gelu_tanh(h).astype(BF16), w2_ref[t]).astype(BF16)

    kc, vc = compressed(abk_ref, 0), compressed(abv_ref, 1)
    q = _sample_q_rows(q_ref)
    step = jnp.bitwise_and(_iota2((rows, 1), 0), tp - 1)
    tpos = PAST_LEN + step
    vis = tpos >= _iota2((1, ncmp), 1) * CMP_STRIDE + (CMP_BLOCK - 1)
    p, l = _softmax_rows(jnp.where(vis, _dot_nt(q, kc) + bc_ref[...], NEG_INF))
    p = p / jnp.maximum(l, TINY)
    ocmp_ref[...] = _dot(p.astype(BF16), vc)
    psum = p[0:tp]
    for g in range(1, NSA_G):
        psum = psum + p[g * tp:(g + 1) * tp]
    cur = jnp.right_shift(PAST_LEN + _iota2((tp, 1), 0), SEL_SHIFT)
    _, picks = _top_blocks(_slc_scores(psum, SLC_LANES, n_slc), cur, N_SEL)
    idx_ref[...] = picks.astype(jnp.int32)

    wb = wk_ref.shape[0] // NSA_KVH
    wlen = bw_ref.shape[1]
    fill = jnp.zeros((wlen - wb - tp, NSA_HD), BF16)
    head = pl.program_id(1)
    k_all = jnp.concatenate([wk_ref[pl.ds(head, wb, stride=NSA_KVH), :].astype(BF16), kn_ref[...], fill], axis=0)
    v_all = jnp.concatenate([wv_ref[pl.ds(head, wb, stride=NSA_KVH), :].astype(BF16), vn_ref[...], fill], axis=0)
    col = _iota2((1, wlen), 1)
    dist = tpos - (PAST_LEN - wb + col)
    in_win = (dist >= 0) & (dist < WINDOW) & (col < wb + T)
    pw, lw = _softmax_rows(jnp.where(in_win, _dot_nt(q, k_all) + bw_ref[...], NEG_INF))
    owin_ref[...] = _dot(pw.astype(BF16), v_all) / jnp.maximum(lw, TINY)


def _nsa_sample_main_call(ya, kw16, vw16, abk, abv, b1, w2, wk, wv, bias_c, bias_w, *, B, T):
    tp = SAMPLE_PAD_T
    rows = NSA_G * tp
    gw = NSA_G * NSA_HD
    ncmp = abk.shape[2]
    wb = wk.shape[1]
    wlen = bias_w.shape[-1]
    n_slc = -(-(PAST_LEN + T) // SEL_BLOCK)
    assert n_slc <= SLC_LANES and T <= tp
    ab_spec = pl.BlockSpec((None, None, ncmp, 2 * NSA_HD), lambda b, h: (b, h, 0, 0))
    win_spec = pl.BlockSpec((wb * NSA_KVH, NSA_HD), lambda b, h: (b, 0))
    o_spec = pl.BlockSpec((None, None, rows, NSA_HD), lambda b, h: (b, h, 0, 0))
    return pl.pallas_call(
        functools.partial(_nsa_sample_main_body, T=T, n_slc=n_slc),
        grid=(B, NSA_KVH),
        in_specs=[ab_spec, ab_spec,
                  pl.BlockSpec((2, 1, NSA_HD), lambda b, h: (0, 0, 0)),
                  pl.BlockSpec((2, NSA_HD, NSA_HD), lambda b, h: (0, 0, 0)),
                  pl.BlockSpec((tp, gw), lambda b, h: (b, EVEN_A["qb"] // gw + h)),
                  win_spec, win_spec,
                  pl.BlockSpec((tp, NSA_HD), lambda b, h: (b, h)),
                  pl.BlockSpec((tp, NSA_HD), lambda b, h: (b, h)),
                  pl.BlockSpec((None, rows, ncmp), lambda b, h: (h, 0, 0)),
                  pl.BlockSpec((None, rows, wlen), lambda b, h: (h, 0, 0))],
        out_specs=[o_spec, o_spec, pl.BlockSpec((None, None, tp, LANES), lambda b, h: (b, h, 0, 0))],
        out_shape=[jax.ShapeDtypeStruct((B, NSA_KVH, rows, NSA_HD), F32),
                   jax.ShapeDtypeStruct((B, NSA_KVH, rows, NSA_HD), F32),
                   jax.ShapeDtypeStruct((B, NSA_KVH, tp, LANES), jnp.int32)],
        compiler_params=_params(("parallel", "parallel")),
        name="nsa_sample_main",
    )(abk, abv, b1.reshape(2, 1, NSA_HD), w2.astype(BF16), ya,
      wk.reshape(B * wb * NSA_KVH, NSA_HD), wv.reshape(B * wb * NSA_KVH, NSA_HD), kw16, vw16, bias_c, bias_w)


NEAR_BLOCKS = 3


def _nsa_sample_sel_body(idx_ref, pt_ref, q_ref, kn_ref, vn_ref, tbl_ref, ocmp_ref, owin_ref, gb_ref, bg_ref, zb_ref,
                         *refs, T):
    del pt_ref
    k_blocks = refs[:N_SEL]
    v_blocks = refs[N_SEL:2 * N_SEL]
    o_ref, osel = refs[2 * N_SEL:]
    tp = SAMPLE_PAD_T
    rows = NSA_G * tp
    b, h, t = pl.program_id(0), pl.program_id(1), pl.program_id(2)
    base = ((b * NSA_KVH + h) * T + t) * N_SEL
    first_new = PAST_LEN // SEL_BLOCK
    cur = jnp.right_shift(PAST_LEN + t, SEL_SHIFT)
    q = _sample_q_rows(q_ref)
    pad = jnp.zeros((SEL_BLOCK - tp, NSA_HD), BF16)
    k_new = jnp.concatenate([kn_ref[...], pad], axis=0)
    v_new = jnp.concatenate([vn_ref[...], pad], axis=0)
    lane = _iota2((1, LANES), 1)
    low = lane < SEL_BLOCK
    within = jnp.bitwise_and(lane, SEL_BLOCK - 1)
    ks, vs, bias, kpos = [], [], [], []
    for i in range(0, N_SEL, 2):
        pair_bias, pair_pos = [], []
        for j in (i, i + 1):
            blk = idx_ref[base + j]
            is_new = blk >= first_new
            ks.append(jnp.where(is_new, k_new, k_blocks[j][pl.ds(h, SEL_BLOCK, stride=NSA_KVH), :].astype(BF16)))
            vs.append(jnp.where(is_new, v_new, v_blocks[j][pl.ds(h, SEL_BLOCK, stride=NSA_KVH), :].astype(BF16)))
            pair_bias.append(tbl_ref[jnp.clip(blk - (first_new - NEAR_BLOCKS), 0, NEAR_BLOCKS)])
            pair_pos.append(jnp.where(blk <= cur, blk * SEL_BLOCK, PAST_LEN + SEL_BLOCK * LANES) + within)
        bias.append(jnp.where(low, pair_bias[0], pair_bias[1]))
        kpos.append(jnp.where(low, pair_pos[0], pair_pos[1]))
    k_all = jnp.concatenate(ks, axis=0)
    v_all = jnp.concatenate(vs, axis=0)
    step = jnp.bitwise_and(_iota2((rows, 1), 0), tp - 1)
    ok = jnp.concatenate(kpos, axis=1) <= PAST_LEN + step
    p, l = _softmax_rows(jnp.where(ok, _dot_nt(q, k_all) + jnp.concatenate(bias, axis=1), NEG_INF))
    o = _dot(p.astype(BF16), v_all) / jnp.maximum(l, TINY)

    @pl.when(t == 0)
    def _():
        osel[...] = jnp.zeros_like(osel)

    osel[...] = jnp.where(step == t, o, osel[...])

    @pl.when(t == T - 1)
    def _():
        gate = jax.nn.sigmoid(gb_ref[...] + bg_ref[...])
        zb = _silu(zb_ref[...])
        for g in range(NSA_G):
            r = slice(g * tp, (g + 1) * tp)
            mix = (gate[:, g:g + 1] * ocmp_ref[r, :] + gate[:, NSA_G + g:NSA_G + g + 1] * osel[r, :]
                   + gate[:, 2 * NSA_G + g:2 * NSA_G + g + 1] * owin_ref[r, :])
            sl = slice(g * NSA_HD, (g + 1) * NSA_HD)
            o_ref[:, sl] = (mix * zb[:, sl]).astype(o_ref.dtype)


def _nsa_sample_sel_call(ya, yb, ks16, vs16, idx, page_table, pool_k, pool_v, tbl, o_cmp, o_win, bg_r, *, B, T):
    tp = SAMPLE_PAD_T
    rows = NSA_G * tp
    gw = NSA_G * NSA_HD
    n_pages = page_table.shape[1]
    halves = PAGE_SIZE // SEL_BLOCK
    idx_flat = idx[:, :, :T, :N_SEL].reshape(-1)
    view_k, view_v = _pool_rows(pool_k), _pool_rows(pool_v)

    def blk_spec(j):
        def index(b, h, t, idx_s, pt_s):
            blk = idx_s[((b * NSA_KVH + h) * T + t) * N_SEL + j]
            page = pt_s[b * n_pages + jnp.minimum(blk // halves, n_pages - 1)]
            return (page * halves + blk % halves, 0)
        return pl.BlockSpec((SEL_BLOCK * NSA_KVH, NSA_HD), index)

    o_spec = pl.BlockSpec((None, None, rows, NSA_HD), lambda b, h, t, *_: (b, h, 0, 0))
    grid_spec = pltpu.PrefetchScalarGridSpec(
        num_scalar_prefetch=2,
        grid=(B, NSA_KVH, T),
        in_specs=[pl.BlockSpec((tp, gw), lambda b, h, t, *_: (b, EVEN_A["qb"] // gw + h)),
                  pl.BlockSpec((tp, NSA_HD), lambda b, h, t, *_: (b, h)),
                  pl.BlockSpec((tp, NSA_HD), lambda b, h, t, *_: (b, h)),
                  pl.BlockSpec((None, NEAR_BLOCKS + 1, rows, LANES), lambda b, h, t, *_: (h, 0, 0, 0)),
                  o_spec, o_spec,
                  pl.BlockSpec((tp, LANES), lambda b, h, t, *_: (b, EVEN_B["gb"] // LANES + h)),
                  pl.BlockSpec((None, 1, LANES), lambda b, h, t, *_: (h, 0, 0)),
                  pl.BlockSpec((tp, gw), lambda b, h, t, *_: (b, EVEN_B["zb"] // gw + h))]
        + [blk_spec(j) for j in range(N_SEL)] * 2,
        out_specs=pl.BlockSpec((tp, gw), lambda b, h, t, *_: (b, h)),
        scratch_shapes=[pltpu.VMEM((rows, NSA_HD), F32)],
    )
    return pl.pallas_call(
        functools.partial(_nsa_sample_sel_body, T=T),
        grid_spec=grid_spec,
        out_shape=jax.ShapeDtypeStruct((B * tp, NSA_W), BF16),
        compiler_params=_params(("arbitrary", "arbitrary", "arbitrary")),
        name="nsa_sample_sel",
    )(idx_flat, page_table.reshape(-1), ya, ks16, vs16, tbl, o_cmp, o_win, yb, bg_r, yb,
      *([view_k] * N_SEL), *([view_v] * N_SEL))


def _sample_bias_tables(rel_bias, T, wb):
    tp = SAMPLE_PAD_T
    ncmp = PAST_LEN // CMP_STRIDE
    wlen = -(-(wb + tp) // LANES) * LANES
    first = PAST_LEN // SEL_BLOCK - NEAR_BLOCKS
    assert PAST_LEN - ((first + 1) * SEL_BLOCK - 1) >= REL_MAX_DIST
    lo, hi = -wlen, PAST_LEN + tp
    rev = _bias_line(rel_bias, lo, hi, descending=True)

    def rows(tbl):
        return tbl.reshape(NSA_KVH, NSA_G * tp, tbl.shape[-1])

    t_c = _toeplitz(rev, hi - 1 - (PAST_LEN - (CMP_BLOCK - 1)), tp, CMP_STRIDE * ncmp)[:, :, ::CMP_STRIDE]
    t_w = _toeplitz(rev, hi - 1 - wb, tp, wlen)
    far = jnp.broadcast_to(rev[:, hi - 1 - REL_MAX_DIST][:, None, None], (NSA_HEADS, tp, LANES))
    near = []
    for k in range(1, NEAR_BLOCKS + 1):
        half = _toeplitz(rev, hi - 1 - (PAST_LEN - (first + k) * SEL_BLOCK), tp, SEL_BLOCK)
        near.append(jnp.concatenate([half, half], axis=-1))
    t_s = jnp.stack([far] + near, axis=1).reshape(NSA_KVH, NSA_G, NEAR_BLOCKS + 1, tp, LANES)
    t_s = t_s.transpose(0, 2, 1, 3, 4).reshape(NSA_KVH, NEAR_BLOCKS + 1, NSA_G * tp, LANES)
    return rows(t_c), rows(t_w), t_s


def _split_even(w):
    kv0 = EVEN_KV_OFF
    g0 = kv0 + 6 * NSA_KV_W
    zb0 = g0 + 3 * NSA_HEADS
    cols = [w[:, zb0:zb0 + NSA_W + MEM_W]]
    for h in range(NSA_KVH):
        for j in range(3):
            cols.append(w[:, g0 + j * NSA_HEADS + h * NSA_G:g0 + j * NSA_HEADS + (h + 1) * NSA_G])
        cols.append(jnp.zeros((w.shape[0], LANES - 3 * NSA_G), w.dtype))
    w_kv = [w[:, kv0 + j * NSA_KV_W:kv0 + (j + 1) * NSA_KV_W].astype(BF16) for j in range(6)]
    return w[:, :EVEN_A_N].astype(BF16), jnp.concatenate(cols, axis=1).astype(BF16), w_kv


def _split_odd(w):
    ig0 = ODD_A_N
    fg0 = ig0 + ML_HEADS
    z0 = fg0 + ML_HEADS
    cols = [w[:, z0:z0 + ML_V_W + MEM_W]]
    for hg in range(ML_HEADS // ML_HB):
        cols.append(w[:, ig0 + hg * ML_HB:ig0 + (hg + 1) * ML_HB])
        cols.append(w[:, fg0 + hg * ML_HB:fg0 + (hg + 1) * ML_HB])
        cols.append(jnp.zeros((w.shape[0], LANES - 2 * ML_HB), w.dtype))
    return w[:, :ODD_A_N].astype(BF16), jnp.concatenate(cols, axis=1).astype(BF16)


def _gate_bias_even(b_gate):
    g = b_gate.reshape(3, NSA_KVH, NSA_G).transpose(1, 0, 2).reshape(NSA_KVH, 1, 3 * NSA_G)
    return jnp.pad(g, ((0, 0), (0, 0), (0, LANES - 3 * NSA_G)))


def _gate_bias_odd(b_if):
    g = b_if.reshape(2, ML_HEADS // ML_HB, ML_HB).transpose(1, 0, 2).reshape(ML_HEADS // ML_HB, 1, 2 * ML_HB)
    return jnp.pad(g, ((0, 0), (0, 0), (0, LANES - 2 * ML_HB)))


def _rel_bucket(dist):
    n = np.maximum(dist, 0)
    exact = REL_BUCKETS // 2
    nf = np.maximum(n, 1).astype(np.float32)
    large = exact + (np.log(nf / exact) / math.log(REL_MAX_DIST / exact) * (REL_BUCKETS - exact)).astype(np.int32)
    return np.where(n < exact, n, np.minimum(large, REL_BUCKETS - 1))


def _bias_line(rel_bias, lo, hi, descending=False):
    dist = np.arange(hi - 1, lo - 1, -1) if descending else np.arange(lo, hi)
    buckets = _rel_bucket(dist)
    edges = np.flatnonzero(np.diff(buckets)) + 1
    starts = np.concatenate([[0], edges])
    ends = np.concatenate([edges, [hi - lo]])
    bias_t = rel_bias.T.astype(F32)
    runs = [jnp.broadcast_to(bias_t[:, int(buckets[s])][:, None], (NSA_HEADS, int(e - s))) for s, e in zip(starts, ends)]
    return jnp.concatenate(runs, axis=1)


def _skew_rows(v, rows, step, cols):
    n = v.shape[1]
    reps = -(-rows * (n + step) // n)
    return jnp.tile(v, (1, reps))[:, :rows * (n + step)].reshape(v.shape[0], rows, n + step)[:, :, :cols]


def _toeplitz(rev, start, rows, cols):
    seg = rev[:, start - (rows - 1):start + cols]
    return _skew_rows(jnp.roll(seg, -(rows - 1), axis=1), rows, -1, cols)


def _prompt_bias_tables(rel_bias, T):
    ncmp = T // CMP_STRIDE
    wlen = WINDOW + Q_BLOCK
    lo, hi = -(CMP_STRIDE * ncmp + CMP_BLOCK), T
    line = _bias_line(rel_bias, lo, hi)
    rev = _bias_line(rel_bias, lo, hi, descending=True)

    def split(tbl):
        return tbl.reshape((NSA_KVH, NSA_G) + tbl.shape[1:])

    first = -(CMP_STRIDE * (ncmp - 1) + CMP_BLOCK - 1) - lo
    seg = line[:, first:first + T + CMP_STRIDE * (ncmp - 1)]
    t_c = _skew_rows(seg, ncmp, CMP_STRIDE, T)[:, ::-1].swapaxes(1, 2)
    t_s = _toeplitz(rev, hi - 1 - (T - Q_BLOCK), Q_BLOCK, T)
    t_w = _toeplitz(rev, hi - 1 - WINDOW, Q_BLOCK, wlen)
    return split(t_c), split(t_s), split(t_w)


def _nsa_sample(ya, yb, kv16, page_table, pk_cmp, pv_cmp, pk_sel, pv_sel, wk, wv, bg_r, w1, b1, w2, pe, rel_bias,
                *, B, T):
    assert (PAST_LEN + T) // CMP_STRIDE == PAST_LEN // CMP_STRIDE
    abk = _cmp_pages_call(pk_cmp, page_table, w1[0], pe[0], B=B)
    abv = _cmp_pages_call(pv_cmp, page_table, w1[1], pe[1], B=B)
    bias_c, bias_w, tbl = _sample_bias_tables(rel_bias, T, wk.shape[1])
    o_cmp, o_win, idx = _nsa_sample_main_call(ya, kv16[4], kv16[5], abk, abv, b1, w2, wk, wv, bias_c, bias_w, B=B, T=T)
    return _nsa_sample_sel_call(ya, yb, kv16[2], kv16[3], idx, page_table, pk_sel, pv_sel, tbl, o_cmp, o_win, bg_r,
                                B=B, T=T)


def _kv_project(x, w_kv):
    outs = [_matmul_heads(x, w) for w in w_kv]
    return [o[0] for o in outs], [o[1] for o in outs]


def _even_prompt(hp2d, npre, mk16, mv16, w_a, w_b, w_kv, bg_r, w1, b1, w2, pe, lb, g_norm, w_out, rel_bias, *, B, T):
    ya, yb = _matmul(npre, w_a), _matmul(npre, w_b)
    kv32, kv16 = _kv_project(npre, w_kv)
    oa, s_new = _hgrn_call(ya, jnp.zeros((B, HG_HEADS, HG_DK, HG_DV), F32), lb, g_norm, B=B, T=T, L=CHUNK, valid=CHUNK)
    kcmp = _compress_call(kv16[0], w1[0], b1[0], w2[0], pe[0], B=B, T=T)
    vcmp = _compress_call(kv16[1], w1[1], b1[1], w2[1], pe[1], B=B, T=T)
    ob = _nsa_prompt_call(ya, yb, kv16[2:], kcmp, vcmp, bg_r, *_prompt_bias_tables(rel_bias, T), B=B, T=T)
    om = _mem_call(yb, EVEN_B["qm"], mk16, mv16, B=B, T=T)
    h_new = _outproj([oa, ob, om], w_out, hp2d)
    wb = min(WINDOW, T)
    rows = [r.reshape(B, T, NSA_KVH, NSA_HD) for r in kv32]
    return h_new, (rows[0], rows[1], rows[2], rows[3], rows[4][:, -wb:], rows[5][:, -wb:], s_new)


def _even_sample(hs2d, nsam, mk_s, mv_s, page_table, pk_cmp, pv_cmp, pk_sel, pv_sel, wk, wv, s0,
                 w_a, w_b, w_kv, bg_r, w1, b1, w2, pe, lb, g_norm, w_out, rel_bias, *, B, T):
    tp = SAMPLE_PAD_T
    ya, yb = _matmul(nsam, w_a), _matmul(nsam, w_b)
    kv32, kv16 = _kv_project(nsam, w_kv)
    oa, s_new = _hgrn_call(ya, s0, lb, g_norm, B=B, T=tp, L=tp, valid=T)
    ob = _nsa_sample(ya, yb, kv16, page_table, pk_cmp, pv_cmp, pk_sel, pv_sel, wk, wv, bg_r, w1, b1, w2, pe, rel_bias,
                     B=B, T=T)
    om = _mem_call(yb, EVEN_B["qm"], mk_s.reshape(B * N_MEM, MEM_W), mv_s.reshape(B * N_MEM, MEM_W), B=B, T=tp)
    rows = [r.reshape(B, tp, NSA_KVH, NSA_HD)[:, :T] for r in kv32]
    wb = wk.shape[1]
    win_k = jnp.concatenate([wk, rows[4]], axis=1)[:, -wb:]
    win_v = jnp.concatenate([wv, rows[5]], axis=1)[:, -wb:]
    return _outproj([oa, ob, om], w_out, hs2d), (rows[0], rows[1], rows[2], rows[3], win_k, win_v, s_new)


def _odd_mix(h2d, hn, k2d, v2d, c0, n0, m0, w_a, w_b, bif_r, g_norm, w_out, *, B, T, L, valid):
    ya, yb = _matmul(hn, w_a), _matmul(hn, w_b)
    h, c_new, n_new, m_new = _mlstm_call(ya, yb, c0, n0, m0, bif_r, g_norm, B=B, T=T, L=L, valid=valid)
    om = _mem_call(yb, ODD_B["qm"], k2d, v2d, B=B, T=T)
    return _outproj([h, om], w_out, h2d), (c_new, n_new, m_new)


def _stack(lst, i):
    return jnp.stack([t[i] for t in lst])


def kernel(x_prompt, x_sample, cache_mem_k, cache_mem_v, cache_cmp_k, cache_cmp_v, cache_sel_k, cache_sel_v,
           cache_win_k, cache_win_v, state_hgrn, state_mlstm_c, state_mlstm_n, state_mlstm_m, page_table,
           mem_prompt, norm_w, mem_norm_w, final_norm_w, rel_bias, w_mem_kv, w_in_even, b_nsa_gate,
           w_cmp1, b_cmp1, w_cmp2, pe_cmp, hgrn_lb_logits, hgrn_norm_w, w_out_even, w_in_odd, b_mlstm_if,
           mlstm_norm_w, w_out_odd):
    bp, tp = x_prompt.shape[:2]
    bs, ts = x_sample.shape[:2]
    tsp = SAMPLE_PAD_T
    lbs = jnp.cumsum(jax.nn.softmax(hgrn_lb_logits.astype(F32), axis=0), axis=0)
    hp = x_prompt.reshape(bp * tp, D_MODEL)
    hs = jnp.pad(x_sample, ((0, 0), (0, tsp - ts), (0, 0))).reshape(bs * tsp, D_MODEL)
    mem2d = mem_prompt.reshape(bp * N_MEM, D_MODEL)
    mem_new, even_p, even_s, odd_p, odd_s = [], [], [], [], []
    for l in range(DEPTH):
        npre = _rmsnorm_rows(hp, norm_w[l], BF16)
        nsam = _rmsnorm_rows(hs, norm_w[l], BF16)
        nmem = _rmsnorm_rows(mem2d, mem_norm_w[l], BF16)
        mk32, mk16 = _matmul_heads(nmem, w_mem_kv[l][:, :MEM_W].astype(BF16))
        mv32, mv16 = _matmul_heads(nmem, w_mem_kv[l][:, MEM_W:].astype(BF16))
        mem_new.append((mk32.reshape(bp, N_MEM, MEM_HEADS, MEM_HD), mv32.reshape(bp, N_MEM, MEM_HEADS, MEM_HD)))
        mk_s, mv_s = cache_mem_k[l], cache_mem_v[l]
        if l % 2 == 0:
            e = l // 2
            w_a, w_b, w_kv = _split_even(w_in_even[e])
            w_out = w_out_even[e].astype(BF16)
            bg_r = _gate_bias_even(b_nsa_gate[e])
            cmpw = (w_cmp1[e].reshape(2, CMP_BLOCK, NSA_HD, NSA_HD), b_cmp1[e], w_cmp2[e], pe_cmp[e])
            hp, st_p = _even_prompt(hp, npre, mk16, mv16, w_a, w_b, w_kv, bg_r, *cmpw, lbs[l], hgrn_norm_w[e], w_out,
                                    rel_bias, B=bp, T=tp)
            hs, st_s = _even_sample(hs, nsam, mk_s, mv_s, page_table, cache_cmp_k[e], cache_cmp_v[e], cache_sel_k[e],
                                    cache_sel_v[e], cache_win_k[e], cache_win_v[e], state_hgrn[e], w_a, w_b, w_kv, bg_r,
                                    *cmpw, lbs[l], hgrn_norm_w[e], w_out, rel_bias, B=bs, T=ts)
            even_p.append(st_p)
            even_s.append(st_s)
        else:
            o = l // 2
            w_a, w_b = _split_odd(w_in_odd[o])
            w_out = w_out_odd[o].astype(BF16)
            bif_r = _gate_bias_odd(b_mlstm_if[o])
            hp, st_p = _odd_mix(hp, npre, mk16, mv16, jnp.zeros((bp, ML_HEADS, ML_DV, ML_DK), F32),
                                jnp.zeros((bp, ML_HEADS, ML_DK), F32), jnp.zeros((bp, ML_HEADS), F32),
                                w_a, w_b, bif_r, mlstm_norm_w[o], w_out, B=bp, T=tp, L=CHUNK, valid=CHUNK)
            hs, st_s = _odd_mix(hs, nsam, mk_s.reshape(bs * N_MEM, MEM_W), mv_s.reshape(bs * N_MEM, MEM_W),
                                state_mlstm_c[o], state_mlstm_n[o], state_mlstm_m[o],
                                w_a, w_b, bif_r, mlstm_norm_w[o], w_out, B=bs, T=tsp, L=tsp, valid=ts)
            odd_p.append(st_p)
            odd_s.append(st_s)
    y_prompt = _rmsnorm_rows(hp, final_norm_w, F32).reshape(bp, tp, D_MODEL)
    y_sample = _rmsnorm_rows(hs, final_norm_w, F32).reshape(bs, tsp, D_MODEL)[:, :ts]
    return (y_prompt, y_sample,
            _stack(mem_new, 0), _stack(mem_new, 1),
            _stack(even_p, 0), _stack(even_p, 1), _stack(even_p, 2), _stack(even_p, 3),
            _stack(even_p, 4), _stack(even_p, 5), _stack(even_p, 6),
            _stack(odd_p, 0), _stack(odd_p, 1), _stack(odd_p, 2),
            _stack(even_s, 0), _stack(even_s, 1), _stack(even_s, 2), _stack(even_s, 3),
            _stack(even_s, 4), _stack(even_s, 5), _stack(even_s, 6),
            _stack(odd_s, 0), _stack(odd_s, 1), _stack(odd_s, 2))
```

```python
import functools
import math

import jax
import jax.numpy as jnp
import numpy as np
from jax import lax
from jax.experimental import pallas as pl
from jax.experimental.pallas import tpu as pltpu

D_MODEL = 4096
DEPTH = 2
PAST_LEN = 16384
PAGE_SIZE = 128
N_MEM = 256
EPS = 1e-6
CHUNK = 64

HG_DK = 128
HG_DV = 128
HG_HEADS = D_MODEL // 2 // HG_DV
HG_W = HG_HEADS * HG_DV

NSA_HD = 128
NSA_HEADS = D_MODEL // 2 // NSA_HD
NSA_KVH = 4
NSA_G = NSA_HEADS // NSA_KVH
NSA_W = NSA_HEADS * NSA_HD
NSA_KV_W = NSA_KVH * NSA_HD
CMP_BLOCK = 32
CMP_STRIDE = 16
SEL_BLOCK = 64
SEL_SHIFT = SEL_BLOCK.bit_length() - 1
N_SEL = 16
WINDOW = 512
Q_BLOCK = 128

ML_HEADS = D_MODEL // 512
ML_DK = D_MODEL // 2 // ML_HEADS
ML_DV = D_MODEL // ML_HEADS
ML_QK_W = ML_HEADS * ML_DK
ML_V_W = ML_HEADS * ML_DV

MEM_HEADS = 4
MEM_HD = 128
MEM_W = MEM_HEADS * MEM_HD

REL_BUCKETS = 32
REL_MAX_DIST = 128

F32 = jnp.float32
BF16 = jnp.bfloat16
LANES = 128
NEG_INF = float("-inf")
TINY = float(np.finfo(np.float32).tiny)
EXP_CLAMP = 80.0
VMEM_LIMIT = 56 * 1024 * 1024

HG_HB = 4
ML_HB = 2
ML_CHUNK = 128
HG_SUB = 16
SAMPLE_PAD_T = 16

EVEN_A = {"qa": 0, "fa": HG_W, "ia": 2 * HG_W, "za": 3 * HG_W, "qb": 4 * HG_W}
EVEN_A_N = 4 * HG_W + NSA_W
EVEN_B = {"zb": 0, "qm": NSA_W, "gb": NSA_W + MEM_W}
EVEN_B_N = NSA_W + MEM_W + NSA_KVH * LANES
EVEN_KV_OFF = EVEN_A_N
ODD_A = {"q": 0, "k": ML_QK_W, "v": 2 * ML_QK_W, "og": 2 * ML_QK_W + ML_V_W}
ODD_A_N = 2 * ML_QK_W + 2 * ML_V_W
ODD_B = {"z": 0, "qm": ML_V_W, "gates": ML_V_W + MEM_W}
ODD_B_N = ML_V_W + MEM_W + (ML_HEADS // ML_HB) * LANES


def _dot(a, b):
    return jnp.dot(a, b, preferred_element_type=F32)


def _dot_nt(a, b):
    return lax.dot_general(a, b, (((1,), (1,)), ((), ())), preferred_element_type=F32)


def _dot_tn(a, b):
    return lax.dot_general(a, b, (((0,), (0,)), ((), ())), preferred_element_type=F32)


def _iota2(shape, dim):
    return lax.broadcasted_iota(jnp.int32, shape, dim)


def _cumsum_rows(x, tri_b):
    hi = x.astype(BF16)
    r1 = x - hi.astype(F32)
    mid = r1.astype(BF16)
    lo = (r1 - mid.astype(F32)).astype(BF16)
    return _dot(tri_b, hi) + _dot(tri_b, mid) + _dot(tri_b, lo)


def _row_to_col(row, n):
    eye = _iota2((n, n), 0) == _iota2((n, n), 1)
    return jnp.sum(jnp.where(eye, row, 0.0), axis=1, keepdims=True)


def _col_to_row(col, n):
    eye = _iota2((n, n), 0) == _iota2((n, n), 1)
    return jnp.sum(jnp.where(eye, col, 0.0), axis=0, keepdims=True)


def _silu(x):
    return x * jax.nn.sigmoid(x)


def _params(sem):
    return pltpu.CompilerParams(dimension_semantics=sem, vmem_limit_bytes=VMEM_LIMIT)


def _rmsnorm_body(x_ref, w_ref, o_ref):
    x = x_ref[...].astype(F32)
    y = x * lax.rsqrt(jnp.mean(x * x, axis=-1, keepdims=True) + EPS)
    o_ref[...] = (y * w_ref[...].astype(F32)).astype(o_ref.dtype)


def _rmsnorm_rows(x2d, w, out_dtype, tm=256):
    m, d = x2d.shape
    tm = min(tm, m)
    return pl.pallas_call(
        _rmsnorm_body,
        grid=(m // tm,),
        in_specs=[pl.BlockSpec((tm, d), lambda i: (i, 0)), pl.BlockSpec((1, d), lambda i: (0, 0))],
        out_specs=pl.BlockSpec((tm, d), lambda i: (i, 0)),
        out_shape=jax.ShapeDtypeStruct((m, d), out_dtype),
        compiler_params=_params(("parallel",)),
        name="rmsnorm",
    )(x2d, w.reshape(1, d))


def _matmul_body(a_ref, b_ref, o_ref):
    o_ref[...] = _dot(a_ref[...], b_ref[...])


def _matmul(a, b, tm=1024, tn=1024):
    m, k = a.shape
    _, n = b.shape
    tm, tn = min(tm, m), min(tn, n)
    assert m % tm == 0 and n % tn == 0, (a.shape, b.shape)
    return pl.pallas_call(
        _matmul_body,
        grid=(m // tm, n // tn),
        in_specs=[pl.BlockSpec((tm, k), lambda i, j: (i, 0)), pl.BlockSpec((k, tn), lambda i, j: (0, j))],
        out_specs=pl.BlockSpec((tm, tn), lambda i, j: (i, j)),
        out_shape=jax.ShapeDtypeStruct((m, n), F32),
        compiler_params=_params(("parallel", "parallel")),
        name="matmul",
    )(a, b)


def _matmul_heads_body(a_ref, b_ref, o32_ref, o16_ref):
    acc = _dot(a_ref[...], b_ref[...])
    for h in range(MEM_HEADS):
        o32_ref[:, h, :] = acc[:, h * LANES:(h + 1) * LANES]
    o16_ref[...] = acc.astype(BF16)


def _matmul_heads(a, b, tm=1024):
    m, k = a.shape
    n = b.shape[1]
    tm = min(tm, m)
    assert m % tm == 0 and n == MEM_HEADS * LANES, (a.shape, b.shape)
    return pl.pallas_call(
        _matmul_heads_body,
        grid=(m // tm,),
        in_specs=[pl.BlockSpec((tm, k), lambda i: (i, 0)), pl.BlockSpec((k, n), lambda i: (0, 0))],
        out_specs=[pl.BlockSpec((tm, MEM_HEADS, LANES), lambda i: (i, 0, 0)), pl.BlockSpec((tm, n), lambda i: (i, 0))],
        out_shape=[jax.ShapeDtypeStruct((m, MEM_HEADS, LANES), F32), jax.ShapeDtypeStruct((m, n), BF16)],
        compiler_params=_params(("parallel",)),
        name="matmul_heads",
    )(a, b)


def _outproj_body(*refs, widths):
    xs = refs[:len(widths)]
    w_ref, r_ref, o_ref = refs[len(widths):]
    acc = r_ref[...]
    off = 0
    for x_ref, w in zip(xs, widths):
        acc = acc + _dot(x_ref[...], w_ref[off:off + w, :])
        off += w
    o_ref[...] = acc


def _outproj(xs, w_bf16, resid, tm=1024, tn=512):
    m = resid.shape[0]
    n = w_bf16.shape[1]
    widths = tuple(x.shape[1] for x in xs)
    assert sum(widths) == w_bf16.shape[0]
    tm = min(tm, m)
    in_specs = [pl.BlockSpec((tm, w), lambda i, j: (i, 0)) for w in widths]
    in_specs += [pl.BlockSpec((w_bf16.shape[0], tn), lambda i, j: (0, j)), pl.BlockSpec((tm, tn), lambda i, j: (i, j))]
    return pl.pallas_call(
        functools.partial(_outproj_body, widths=widths),
        grid=(m // tm, n // tn),
        in_specs=in_specs,
        out_specs=pl.BlockSpec((tm, tn), lambda i, j: (i, j)),
        out_shape=jax.ShapeDtypeStruct((m, n), F32),
        compiler_params=_params(("parallel", "parallel")),
        name="outproj",
    )(*xs, w_bf16, resid)


def _hgrn_body(qa_ref, fa_ref, ia_ref, za_ref, lb_ref, gn_ref, s0_ref, o_ref, s_out, s_scr, *, L, valid):
    c = pl.program_id(2)

    @pl.when(c == 0)
    def _():
        s_scr[...] = s0_ref[...]

    lb = lb_ref[...]
    sig = jax.nn.sigmoid(fa_ref[...])
    logf = jnp.log(lb + (1.0 - lb) * sig)
    kk = (1.0 - lb) * (1.0 - sig)
    if valid < L:
        live = _iota2((L, 1), 0) < valid
        logf = jnp.where(live, logf, 0.0)
        kk = jnp.where(live, kk, 0.0)
    tri_b = (_iota2((L, L), 0) >= _iota2((L, L), 1)).astype(BF16)
    bc = _cumsum_rows(logf, tri_b)
    q = _silu(qa_ref[...])
    gate = _silu(za_ref[...])
    v = ia_ref[...]
    gn = gn_ref[...]
    nsub = L // HG_SUB
    rr = _iota2((L, nsub * L), 0)
    cc = _iota2((L, nsub * L), 1)
    keep = ((jnp.right_shift(cc, L.bit_length() - 1) == jnp.right_shift(rr, HG_SUB.bit_length() - 1))
            & (jnp.bitwise_and(cc, L - 1) <= rr))
    for j in range(HG_HB):
        sl = slice(j * HG_DK, (j + 1) * HG_DK)
        bj, qj, kj = bc[:, sl], q[:, sl], kk[:, sl]
        vb = v[:, sl].astype(BF16)
        s_prev = s_scr[j]
        inter = _dot((qj * jnp.exp(bj)).astype(BF16), s_prev.astype(BF16))
        mids = [bj[i * HG_SUB + HG_SUB // 2:i * HG_SUB + HG_SUB // 2 + 1, :] for i in range(nsub)]
        mid_rows = jnp.concatenate([jnp.broadcast_to(m, (HG_SUB, HG_DK)) for m in mids], axis=0)
        q_dec = qj * jnp.exp(jnp.minimum(bj - mid_rows, EXP_CLAMP))
        k_dec = jnp.concatenate([kj * jnp.exp(jnp.minimum(m - bj, EXP_CLAMP)) for m in mids], axis=0)
        att = jnp.where(keep, _dot_nt(q_dec.astype(BF16), k_dec.astype(BF16)), 0.0)
        o = inter + _dot(att.astype(BF16), jnp.concatenate([vb] * nsub, axis=0))
        o_n = o * lax.rsqrt(jnp.mean(o * o, axis=-1, keepdims=True) + EPS) * gn
        o_ref[:, sl] = (o_n * gate[:, sl]).astype(o_ref.dtype)
        bl = bj[L - 1:L, :]
        kd = kj * jnp.exp(bl - bj)
        s_scr[j] = _row_to_col(jnp.exp(bl), HG_DK) * s_prev + _dot_tn(kd.astype(BF16), vb)

    @pl.when(c == pl.num_programs(2) - 1)
    def _():
        s_out[...] = s_scr[...]


def _hgrn_call(y, s0, lb, gn, *, B, T, L, valid):
    nc = T // L
    w = HG_HB * HG_DK

    def col(name):
        blk = EVEN_A[name] // w
        return pl.BlockSpec((L, w), lambda b, hg, c: (b * nc + c, blk + hg))

    state_spec = pl.BlockSpec((None, HG_HB, HG_DK, HG_DV), lambda b, hg, c: (b, hg, 0, 0))
    return pl.pallas_call(
        functools.partial(_hgrn_body, L=L, valid=valid),
        grid=(B, HG_HEADS // HG_HB, nc),
        in_specs=[col("qa"), col("fa"), col("ia"), col("za"),
                  pl.BlockSpec((1, w), lambda b, hg, c: (0, hg)),
                  pl.BlockSpec((1, HG_DV), lambda b, hg, c: (0, 0)),
                  state_spec],
        out_specs=[pl.BlockSpec((L, w), lambda b, hg, c: (b * nc + c, hg)), state_spec],
        out_shape=[jax.ShapeDtypeStruct((B * T, HG_W), BF16),
                   jax.ShapeDtypeStruct((B, HG_HEADS, HG_DK, HG_DV), F32)],
        scratch_shapes=[pltpu.VMEM((HG_HB, HG_DK, HG_DV), F32)],
        compiler_params=_params(("arbitrary", "arbitrary", "arbitrary")),
        name="hgrn2",
    )(y, y, y, y, lb.reshape(1, HG_W), gn.reshape(1, HG_DV), s0)


def _mlstm_body(q_ref, k_ref, v_ref, og_ref, z_ref, g_ref, bif_ref, gn_ref, c0_ref, n0_ref, m0_ref,
                h_ref, c_out, n_out, m_out, c_scr, n_scr, m_scr, *, L, valid):
    c = pl.program_id(2)

    @pl.when(c == 0)
    def _():
        c_scr[...] = c0_ref[...]
        n_scr[...] = n0_ref[...]
        m_scr[...] = m0_ref[...]

    gates = g_ref[...] + bif_ref[...]
    log_i = gates
    log_f = jnp.minimum(gates, 0.0) - jnp.log(1.0 + jnp.exp(-jnp.abs(gates)))
    if valid < L:
        live = _iota2((L, 1), 0) < valid
        log_i = jnp.where(live, log_i, -1e30)
        log_f = jnp.where(live, log_f, 0.0)
    tri = _iota2((L, L), 0) >= _iota2((L, L), 1)
    bcs = _cumsum_rows(log_f, tri.astype(BF16))
    for j in range(ML_HB):
        b_col = bcs[:, ML_HB + j:ML_HB + j + 1]
        i_col = log_i[:, j:j + 1]
        b_row = _col_to_row(b_col, L)
        i_row = _col_to_row(i_col, L)
        m_prev = m_scr[:, j:j + 1]
        dmat = jnp.where(tri, b_col - b_row + i_row, NEG_INF)
        inter = b_col + m_prev
        mt = jnp.maximum(inter, jnp.max(dmat, axis=1, keepdims=True))
        w_in = jnp.exp(dmat - mt)
        w_x = jnp.exp(inter - mt)
        qj = q_ref[:, j * ML_DK:(j + 1) * ML_DK]
        kj = k_ref[:, j * ML_DK:(j + 1) * ML_DK] * (ML_DK ** -0.5)
        vj = v_ref[:, j * ML_DV:(j + 1) * ML_DV]
        qb, kb = qj.astype(BF16), kj.astype(BF16)
        sw = _dot_nt(qb, kb) * w_in
        c_prev = c_scr[j]
        n_prev = n_scr[:, j * ML_DK:(j + 1) * ML_DK]
        num = w_x * _dot_nt(qb, c_prev.astype(BF16)) + _dot(sw.astype(BF16), vj.astype(BF16))
        den = w_x * jnp.sum(qj * n_prev, axis=1, keepdims=True) + jnp.sum(sw, axis=1, keepdims=True)
        h = num / jnp.maximum(jnp.abs(den), jnp.exp(-mt))
        m_last = mt[L - 1:L, :]
        b_last = b_col[L - 1:L, :]
        w_end = jnp.exp(b_last - b_col + i_col - m_last)
        d_c = jnp.exp(b_last + m_prev - m_last)
        c_scr[j] = d_c * c_prev + _dot_tn((w_end * vj).astype(BF16), kb)
        n_scr[:, j * ML_DK:(j + 1) * ML_DK] = d_c * n_prev + jnp.sum(w_end * kj, axis=0, keepdims=True)
        m_scr[:, j:j + 1] = m_last
        sv = slice(j * ML_DV, (j + 1) * ML_DV)
        h_n = h * lax.rsqrt(jnp.mean(h * h, axis=-1, keepdims=True) + EPS) * gn_ref[:, sv]
        h_ref[:, sv] = (h_n * jax.nn.sigmoid(og_ref[:, sv]) * _silu(z_ref[:, sv])).astype(h_ref.dtype)

    @pl.when(c == pl.num_programs(2) - 1)
    def _():
        c_out[...] = c_scr[...]
        n_out[...] = n_scr[...]
        m_out[...] = m_scr[...]


def _mlstm_call(ya, yb, c0, n0, m0, bif_r, gn, *, B, T, L, valid):
    nc = T // L
    ng = ML_HEADS // ML_HB
    wk, wv = ML_HB * ML_DK, ML_HB * ML_DV

    def col(name, w):
        blk = (ODD_A[name] if name in ODD_A else ODD_B[name]) // w
        return pl.BlockSpec((L, w), lambda b, hg, c: (b * nc + c, blk + hg))

    c_spec = pl.BlockSpec((None, ML_HB, ML_DV, ML_DK), lambda b, hg, c: (b, hg, 0, 0))
    n_spec = pl.BlockSpec((None, 1, wk), lambda b, hg, c: (b, 0, hg))
    m_spec = pl.BlockSpec((None, None, 1, LANES), lambda b, hg, c: (b, hg, 0, 0))
    m0_r = jnp.pad(m0.reshape(B, ng, 1, ML_HB), ((0, 0), (0, 0), (0, 0), (0, LANES - ML_HB)))
    h, c_new, n_new, m_new = pl.pallas_call(
        functools.partial(_mlstm_body, L=L, valid=valid),
        grid=(B, ng, nc),
        in_specs=[col("q", wk), col("k", wk), col("v", wv), col("og", wv), col("z", wv), col("gates", LANES),
                  pl.BlockSpec((None, 1, LANES), lambda b, hg, c: (hg, 0, 0)),
                  pl.BlockSpec((1, wv), lambda b, hg, c: (0, hg)),
                  c_spec, n_spec, m_spec],
        out_specs=[pl.BlockSpec((L, wv), lambda b, hg, c: (b * nc + c, hg)), c_spec, n_spec, m_spec],
        out_shape=[jax.ShapeDtypeStruct((B * T, ML_V_W), BF16),
                   jax.ShapeDtypeStruct((B, ML_HEADS, ML_DV, ML_DK), F32),
                   jax.ShapeDtypeStruct((B, 1, ML_QK_W), F32),
                   jax.ShapeDtypeStruct((B, ng, 1, LANES), F32)],
        scratch_shapes=[pltpu.VMEM((ML_HB, ML_DV, ML_DK), F32), pltpu.VMEM((1, wk), F32), pltpu.VMEM((1, LANES), F32)],
        compiler_params=_params(("arbitrary", "arbitrary", "arbitrary")),
        name="mlstm",
    )(ya, ya, ya, ya, yb, yb, bif_r, gn.reshape(1, ML_V_W), c0, n0.reshape(B, 1, ML_QK_W), m0_r)
    return h, c_new, n_new.reshape(B, ML_HEADS, ML_DK), m_new[:, :, 0, :ML_HB].reshape(B, ML_HEADS)


def _mem_body(q_ref, k_ref, v_ref, o_ref):
    q = q_ref[...] * (MEM_HD ** -0.5)
    for h in range(MEM_HEADS):
        sl = slice(h * MEM_HD, (h + 1) * MEM_HD)
        s = _dot_nt(q[:, sl].astype(BF16), k_ref[:, sl].astype(BF16))
        p = jnp.exp(s - jnp.max(s, axis=-1, keepdims=True))
        o = _dot(p.astype(BF16), v_ref[:, sl].astype(BF16)) / jnp.sum(p, axis=-1, keepdims=True)
        o_ref[:, sl] = o.astype(o_ref.dtype)


def _mem_call(y, q_off, k2d, v2d, *, B, T, tq=256):
    tq = min(tq, T)
    nq = T // tq
    qb = q_off // MEM_W
    return pl.pallas_call(
        _mem_body,
        grid=(B, nq),
        in_specs=[pl.BlockSpec((tq, MEM_W), lambda b, i: (b * nq + i, qb)),
                  pl.BlockSpec((N_MEM, MEM_W), lambda b, i: (b, 0)),
                  pl.BlockSpec((N_MEM, MEM_W), lambda b, i: (b, 0))],
        out_specs=pl.BlockSpec((tq, MEM_W), lambda b, i: (b * nq + i, 0)),
        out_shape=jax.ShapeDtypeStruct((B * T, MEM_W), BF16),
        compiler_params=_params(("parallel", "parallel")),
        name="mem_attn",
    )(y, k2d, v2d)


def _gelu_tanh(x):
    return 0.5 * x * (1.0 + jnp.tanh(math.sqrt(2.0 / math.pi) * (x + 0.044715 * (x * x * x))))


def _compress_body(x_ref, w1_ref, b1_ref, w2_ref, pe_ref, o_ref, x32, *, nch):
    x32[...] = x_ref[...].astype(F32)
    a = jnp.zeros((nch, NSA_HD), F32)
    b = jnp.zeros((nch, NSA_HD), F32)
    for s in range(CMP_STRIDE):
        r = x32[pl.ds(s, nch, stride=CMP_STRIDE), :]
        a = a + _dot((r + pe_ref[s:s + 1, :]).astype(BF16), w1_ref[s])
        b = b + _dot((r + pe_ref[CMP_STRIDE + s:CMP_STRIDE + s + 1, :]).astype(BF16), w1_ref[CMP_STRIDE + s])
    h = a + pltpu.roll(b, nch - 1, 0) + b1_ref[...]
    o_ref[...] = _dot(_gelu_tanh(h).astype(BF16), w2_ref[...])


def _compress_call(x16, w1, b1, w2, pe, *, B, T):
    nch = T // CMP_STRIDE
    return pl.pallas_call(
        functools.partial(_compress_body, nch=nch),
        grid=(B, NSA_KVH),
        in_specs=[pl.BlockSpec((T, NSA_HD), lambda b, h: (b, h)),
                  pl.BlockSpec((CMP_BLOCK, NSA_HD, NSA_HD), lambda b, h: (0, 0, 0)),
                  pl.BlockSpec((1, NSA_HD), lambda b, h: (0, 0)),
                  pl.BlockSpec((NSA_HD, NSA_HD), lambda b, h: (0, 0)),
                  pl.BlockSpec((CMP_BLOCK, NSA_HD), lambda b, h: (0, 0))],
        out_specs=pl.BlockSpec((None, None, nch, NSA_HD), lambda b, h: (b, h, 0, 0)),
        out_shape=jax.ShapeDtypeStruct((B, NSA_KVH, nch, NSA_HD), F32),
        scratch_shapes=[pltpu.VMEM((T, NSA_HD), F32)],
        compiler_params=_params(("parallel", "parallel")),
        name="nsa_compress",
    )(x16, w1.astype(BF16), b1.reshape(1, NSA_HD), w2.astype(BF16), pe)


def _softmax_rows(s):
    m = jnp.max(s, axis=-1, keepdims=True)
    m = jnp.where(m == NEG_INF, 0.0, m)
    p = jnp.exp(s - m)
    return p, jnp.sum(p, axis=-1, keepdims=True)


def _slc_scores(psum, width, n_slc):
    ncmp = psum.shape[1]
    d = _iota2((ncmp, width), 0) - (SEL_BLOCK // CMP_STRIDE) * _iota2((ncmp, width), 1)
    wgt = jnp.where((d == -1) | (d == 3), 1.0, jnp.where((d >= 0) & (d <= 2), 2.0, 0.0))
    wgt = jnp.where(_iota2((ncmp, width), 1) < n_slc, wgt, 0.0).astype(BF16)
    p_hi = psum.astype(BF16)
    p_lo = (psum - p_hi.astype(F32)).astype(BF16)
    return _dot(p_hi, wgt) + _dot(p_lo, wgt)


def _top_blocks(slc, cur, n_pick):
    rows, width = slc.shape
    blk = _iota2((rows, width), 1)
    forced = (blk == 0) | (blk == cur) | (blk == cur - 1)
    score = jnp.where(forced, jnp.inf, slc)
    score = jnp.where(blk > cur, NEG_INF, score)
    blk_f = blk.astype(F32)
    lane = _iota2((rows, LANES), 1)
    sel = jnp.zeros((rows, width), F32)
    picks = jnp.zeros((rows, LANES), F32)
    for i in range(n_pick):
        mx = jnp.max(score, axis=-1, keepdims=True)
        first = jnp.min(jnp.where(score == mx, blk_f, float(width)), axis=-1, keepdims=True)
        pick = blk_f == first
        sel = jnp.where(pick, 1.0, sel)
        picks = jnp.where(lane == i, first, picks)
        score = jnp.where(pick, NEG_INF, score)
    return sel, picks


def _member_by_rank(psum, tpos_row, n_slc, n_pick):
    nq, ncmp = psum.shape
    nb = -(-n_slc // 8) * 8
    d = _iota2((nb, ncmp), 1) - (SEL_BLOCK // CMP_STRIDE) * _iota2((nb, ncmp), 0)
    wgt = jnp.where((d == -1) | (d == 3), 1.0, jnp.where((d >= 0) & (d <= 2), 2.0, 0.0))
    wgt = jnp.where(_iota2((nb, ncmp), 0) < n_slc, wgt, 0.0).astype(BF16)
    p_hi = psum.astype(BF16)
    p_lo = (psum - p_hi.astype(F32)).astype(BF16)
    slc = _dot_nt(wgt, p_hi) + _dot_nt(wgt, p_lo)
    blk = _iota2((nb, nq), 0)
    cur = jnp.right_shift(tpos_row, SEL_SHIFT)
    forced = (blk == 0) | (blk == cur) | (blk == cur - 1)
    score = jnp.where(forced, jnp.inf, slc)
    score = jnp.where(blk > cur, NEG_INF, score)
    ahead = jnp.zeros((nb, nq), F32)
    for i in range(n_slc):
        s_i = score[i:i + 1, :]
        ahead = ahead + jnp.where((s_i > score) | ((s_i == score) & (blk > i)), 1.0, 0.0)
    return jnp.where((ahead < n_pick) & (blk <= cur), 1.0, 0.0)


def _nsa_prompt_body(q_ref, zb_ref, gb_ref, bg_ref, ks_ref, vs_ref, kw_ref, vw_ref, kc_ref, vc_ref,
                     bc_ref, bs_ref, bw_ref, o_ref, ksp, vsp, kwp, vwp, osel, *, T):
    qi = pl.program_id(2)
    tq = Q_BLOCK
    front = T - tq
    wlen = WINDOW + tq
    n_slc = T // SEL_BLOCK

    @pl.when(qi == 0)
    def _():
        ksp[0:front, :] = jnp.zeros((front, NSA_HD), BF16)
        vsp[0:front, :] = jnp.zeros((front, NSA_HD), BF16)
        ksp[front:front + T, :] = ks_ref[...].astype(BF16)
        vsp[front:front + T, :] = vs_ref[...].astype(BF16)
        kwp[0:WINDOW, :] = jnp.zeros((WINDOW, NSA_HD), BF16)
        vwp[0:WINDOW, :] = jnp.zeros((WINDOW, NSA_HD), BF16)
        kwp[WINDOW:WINDOW + T, :] = kw_ref[...].astype(BF16)
        vwp[WINDOW:WINDOW + T, :] = vw_ref[...].astype(BF16)

    t0 = pl.multiple_of(qi * tq, tq)
    tpos = _iota2((tq, 1), 0) + t0
    q_all = q_ref[...] * (NSA_HD ** -0.5)
    qs = [q_all[:, g * NSA_HD:(g + 1) * NSA_HD].astype(BF16) for g in range(NSA_G)]

    ncmp = T // CMP_STRIDE
    vis = tpos >= _iota2((1, ncmp), 1) * CMP_STRIDE + (CMP_BLOCK - 1)
    kcb = kc_ref[...].astype(BF16)
    vcb = vc_ref[...].astype(BF16)
    psum = jnp.zeros((tq, ncmp), F32)
    o_cmp = []
    for g in range(NSA_G):
        s = jnp.where(vis, _dot_nt(qs[g], kcb) + bc_ref[g], NEG_INF)
        p, l = _softmax_rows(s)
        p = p / jnp.maximum(l, TINY)
        psum = psum + p
        o_cmp.append(_dot(p.astype(BF16), vcb))

    member_t = _member_by_rank(psum, _iota2((1, tq), 1) + t0, n_slc, min(N_SEL, n_slc)).astype(BF16)

    nb = member_t.shape[0]
    n_win = SEL_WINDOWS if T % (SEL_WINDOWS * tq) == 0 else 1
    for i in range(n_win):
        w_prev, w = T * i // n_win, T * (i + 1) // n_win

        @pl.when((qi >= w_prev // tq) & (qi < w // tq))
        def _(w=w):
            off = T - w
            col_blk = (jnp.right_shift(_iota2((nb, w), 1) + off, SEL_SHIFT)
                       + (qi * (tq // SEL_BLOCK) + (tq - T) // SEL_BLOCK))
            expand = (col_blk == _iota2((nb, w), 0)).astype(BF16)
            kpos = _iota2((1, w), 1) + (t0 + tq - w)
            allowed = (_dot_tn(member_t, expand) > 0.5) & (kpos <= tpos)
            mask_s = jnp.where(allowed, 0.0, NEG_INF)
            k_s = ksp[pl.ds(t0 + off, w), :]
            v_s = vsp[pl.ds(t0 + off, w), :]
            for g in range(NSA_G):
                p, l = _softmax_rows(_dot_nt(qs[g], k_s) + bs_ref[g, :, off:] + mask_s)
                osel[g] = _dot(p.astype(BF16), v_s) / jnp.maximum(l, TINY)

    dist = WINDOW + _iota2((tq, wlen), 0) - _iota2((tq, wlen), 1)
    in_win = (dist >= 0) & (dist < WINDOW) & (_iota2((1, wlen), 1) + (t0 - WINDOW) >= 0)
    mask_w = jnp.where(in_win, 0.0, NEG_INF)
    k_w = kwp[pl.ds(t0, wlen), :]
    v_w = vwp[pl.ds(t0, wlen), :]
    gate = jax.nn.sigmoid(gb_ref[...] + bg_ref[...])
    zb = _silu(zb_ref[...])
    for g in range(NSA_G):
        p, l = _softmax_rows(_dot_nt(qs[g], k_w) + bw_ref[g] + mask_w)
        o_win = _dot(p.astype(BF16), v_w) / jnp.maximum(l, TINY)
        mix = (gate[:, g:g + 1] * o_cmp[g] + gate[:, NSA_G + g:NSA_G + g + 1] * osel[g]
               + gate[:, 2 * NSA_G + g:2 * NSA_G + g + 1] * o_win)
        sl = slice(g * NSA_HD, (g + 1) * NSA_HD)
        o_ref[:, sl] = (mix * zb[:, sl]).astype(o_ref.dtype)


def _nsa_prompt_call(ya, yb, kv16, kcmp, vcmp, bg_r, bias_c, bias_s, bias_w, *, B, T):
    nq = T // Q_BLOCK
    gw = NSA_G * NSA_HD
    wlen = WINDOW + Q_BLOCK
    kv_spec = pl.BlockSpec((T, NSA_HD), lambda b, h, i: (b, h))
    cmp_spec = pl.BlockSpec((None, None, T // CMP_STRIDE, NSA_HD), lambda b, h, i: (b, h, 0, 0))
    return pl.pallas_call(
        functools.partial(_nsa_prompt_body, T=T),
        grid=(B, NSA_KVH, nq),
        in_specs=[pl.BlockSpec((Q_BLOCK, gw), lambda b, h, i: (b * nq + i, EVEN_A["qb"] // gw + h)),
                  pl.BlockSpec((Q_BLOCK, gw), lambda b, h, i: (b * nq + i, EVEN_B["zb"] // gw + h)),
                  pl.BlockSpec((Q_BLOCK, LANES), lambda b, h, i: (b * nq + i, EVEN_B["gb"] // LANES + h)),
                  pl.BlockSpec((None, 1, LANES), lambda b, h, i: (h, 0, 0)),
                  kv_spec, kv_spec, kv_spec, kv_spec, cmp_spec, cmp_spec,
                  pl.BlockSpec((None, NSA_G, Q_BLOCK, T // CMP_STRIDE), lambda b, h, i: (h, 0, i, 0)),
                  pl.BlockSpec((None, NSA_G, Q_BLOCK, T), lambda b, h, i: (h, 0, 0, 0)),
                  pl.BlockSpec((None, NSA_G, Q_BLOCK, wlen), lambda b, h, i: (h, 0, 0, 0))],
        out_specs=pl.BlockSpec((Q_BLOCK, gw), lambda b, h, i: (b * nq + i, h)),
        out_shape=jax.ShapeDtypeStruct((B * T, NSA_W), BF16),
        scratch_shapes=[pltpu.VMEM((2 * T - Q_BLOCK, NSA_HD), BF16), pltpu.VMEM((2 * T - Q_BLOCK, NSA_HD), BF16),
                        pltpu.VMEM((WINDOW + T, NSA_HD), BF16), pltpu.VMEM((WINDOW + T, NSA_HD), BF16),
                        pltpu.VMEM((NSA_G, Q_BLOCK, NSA_HD), F32)],
        compiler_params=_params(("arbitrary", "arbitrary", "arbitrary")),
        name="nsa_prompt",
    )(ya, yb, yb, bg_r, *kv16, kcmp, vcmp, bias_c, bias_s, bias_w)


CMP_PAGES = 16
CHUNKS_PER_PAGE = PAGE_SIZE // CMP_STRIDE
PAGE_ROWS = PAGE_SIZE * NSA_KVH


def _pool_rows(pool):
    return pool.reshape(pool.shape[0] * PAGE_ROWS, NSA_HD)


def _cmp_pages_body(pt_ref, *refs):
    del pt_ref
    pages = refs[:CMP_PAGES]
    w_ref, pe_ref, o_ref = refs[CMP_PAGES:]
    rows = CMP_PAGES * CHUNKS_PER_PAGE
    per_head = [jnp.concatenate(
        [jnp.concatenate([pg[pl.ds(NSA_KVH * s + h, CHUNKS_PER_PAGE, stride=CMP_STRIDE * NSA_KVH), :]
                          for s in range(CMP_STRIDE)], axis=1) for pg in pages], axis=0) for h in range(NSA_KVH)]
    w = w_ref[...]
    r = _dot(jnp.concatenate(per_head, axis=0).astype(BF16), w)
    pc = _dot(pe_ref[...], w)
    r = r + jnp.concatenate([pc[0:1, :NSA_HD], pc[1:2, NSA_HD:]], axis=1)
    for h in range(NSA_KVH):
        o_ref[h] = r[h * rows:(h + 1) * rows]


def _cmp_pages_call(pool, page_table, w1, pe, *, B):
    n_pages = page_table.shape[1]
    rows = CMP_PAGES * CHUNKS_PER_PAGE
    view = _pool_rows(pool)
    w = w1.reshape(2, CMP_STRIDE, NSA_HD, NSA_HD).transpose(1, 2, 0, 3).reshape(CMP_STRIDE * NSA_HD, 2 * NSA_HD)
    pe_rows = jnp.pad(pe.reshape(2, CMP_STRIDE * NSA_HD), ((0, 6), (0, 0))).astype(BF16)

    def page_spec(i):
        return pl.BlockSpec((PAGE_ROWS, NSA_HD), lambda b, s, pt: (pt[b * n_pages + s * CMP_PAGES + i], 0))

    grid_spec = pltpu.PrefetchScalarGridSpec(
        num_scalar_prefetch=1,
        grid=(B, n_pages // CMP_PAGES),
        in_specs=[page_spec(i) for i in range(CMP_PAGES)]
        + [pl.BlockSpec((CMP_STRIDE * NSA_HD, 2 * NSA_HD), lambda b, s, pt: (0, 0)),
           pl.BlockSpec((8, CMP_STRIDE * NSA_HD), lambda b, s, pt: (0, 0))],
        out_specs=pl.BlockSpec((None, NSA_KVH, rows, 2 * NSA_HD), lambda b, s, pt: (b, 0, s, 0)),
    )
    return pl.pallas_call(
        _cmp_pages_body,
        grid_spec=grid_spec,
        out_shape=jax.ShapeDtypeStruct((B, NSA_KVH, n_pages * CHUNKS_PER_PAGE, 2 * NSA_HD), F32),
        compiler_params=_params(("arbitrary", "arbitrary")),
        name="nsa_cmp_pages",
    )(page_table.reshape(-1), *([view] * CMP_PAGES), w.astype(BF16), pe_rows)


SEL_WINDOWS = 4
SLC_LANES = 384


def _sample_q_rows(q_ref):
    q = q_ref[...] * (NSA_HD ** -0.5)
    return jnp.concatenate([q[:, g * NSA_HD:(g + 1) * NSA_HD] for g in range(NSA_G)], axis=0).astype(BF16)


def _nsa_sample_main_body(abk_ref, abv_ref, b1_ref, w2_ref, q_ref, wk_ref, wv_ref, kn_ref, vn_ref, bc_ref, bw_ref,
                          ocmp_ref, owin_ref, idx_ref, *, T, n_slc):
    tp = SAMPLE_PAD_T
    rows = NSA_G * tp
    ncmp = abk_ref.shape[0]

    def compressed(ab_ref, t):
        ab = ab_ref[...]
        h = ab[:, :NSA_HD] + pltpu.roll(ab[:, NSA_HD:], ncmp - 1, 0) + b1_ref[t]
        return _dot(_gelu_tanh(h).astype(BF16), w2_ref[t]).astype(BF16)

    kc, vc = compressed(abk_ref, 0), compressed(abv_ref, 1)
    q = _sample_q_rows(q_ref)
    step = jnp.bitwise_and(_iota2((rows, 1), 0), tp - 1)
    tpos = PAST_LEN + step
    vis = tpos >= _iota2((1, ncmp), 1) * CMP_STRIDE + (CMP_BLOCK - 1)
    p, l = _softmax_rows(jnp.where(vis, _dot_nt(q, kc) + bc_ref[...], NEG_INF))
    p = p / jnp.maximum(l, TINY)
    ocmp_ref[...] = _dot(p.astype(BF16), vc)
    psum = p[0:tp]
    for g in range(1, NSA_G):
        psum = psum + p[g * tp:(g + 1) * tp]
    cur = jnp.right_shift(PAST_LEN + _iota2((tp, 1), 0), SEL_SHIFT)
    _, picks = _top_blocks(_slc_scores(psum, SLC_LANES, n_slc), cur, N_SEL)
    idx_ref[...] = picks.astype(jnp.int32)

    wb = wk_ref.shape[0] // NSA_KVH
    wlen = bw_ref.shape[1]
    fill = jnp.zeros((wlen - wb - tp, NSA_HD), BF16)
    head = pl.program_id(1)
    k_all = jnp.concatenate([wk_ref[pl.ds(head, wb, stride=NSA_KVH), :].astype(BF16), kn_ref[...], fill], axis=0)
    v_all = jnp.concatenate([wv_ref[pl.ds(head, wb, stride=NSA_KVH), :].astype(BF16), vn_ref[...], fill], axis=0)
    col = _iota2((1, wlen), 1)
    dist = tpos - (PAST_LEN - wb + col)
    in_win = (dist >= 0) & (dist < WINDOW) & (col < wb + T)
    pw, lw = _softmax_rows(jnp.where(in_win, _dot_nt(q, k_all) + bw_ref[...], NEG_INF))
    owin_ref[...] = _dot(pw.astype(BF16), v_all) / jnp.maximum(lw, TINY)


def _nsa_sample_main_call(ya, kw16, vw16, abk, abv, b1, w2, wk, wv, bias_c, bias_w, *, B, T):
    tp = SAMPLE_PAD_T
    rows = NSA_G * tp
    gw = NSA_G * NSA_HD
    ncmp = abk.shape[2]
    wb = wk.shape[1]
    wlen = bias_w.shape[-1]
    n_slc = -(-(PAST_LEN + T) // SEL_BLOCK)
    assert n_slc <= SLC_LANES and T <= tp
    ab_spec = pl.BlockSpec((None, None, ncmp, 2 * NSA_HD), lambda b, h: (b, h, 0, 0))
    win_spec = pl.BlockSpec((wb * NSA_KVH, NSA_HD), lambda b, h: (b, 0))
    o_spec = pl.BlockSpec((None, None, rows, NSA_HD), lambda b, h: (b, h, 0, 0))
    return pl.pallas_call(
        functools.partial(_nsa_sample_main_body, T=T, n_slc=n_slc),
        grid=(B, NSA_KVH),
        in_specs=[ab_spec, ab_spec,
                  pl.BlockSpec((2, 1, NSA_HD), lambda b, h: (0, 0, 0)),
                  pl.BlockSpec((2, NSA_HD, NSA_HD), lambda b, h: (0, 0, 0)),
                  pl.BlockSpec((tp, gw), lambda b, h: (b, EVEN_A["qb"] // gw + h)),
                  win_spec, win_spec,
                  pl.BlockSpec((tp, NSA_HD), lambda b, h: (b, h)),
                  pl.BlockSpec((tp, NSA_HD), lambda b, h: (b, h)),
                  pl.BlockSpec((None, rows, ncmp), lambda b, h: (h, 0, 0)),
                  pl.BlockSpec((None, rows, wlen), lambda b, h: (h, 0, 0))],
        out_specs=[o_spec, o_spec, pl.BlockSpec((None, None, tp, LANES), lambda b, h: (b, h, 0, 0))],
        out_shape=[jax.ShapeDtypeStruct((B, NSA_KVH, rows, NSA_HD), F32),
                   jax.ShapeDtypeStruct((B, NSA_KVH, rows, NSA_HD), F32),
                   jax.ShapeDtypeStruct((B, NSA_KVH, tp, LANES), jnp.int32)],
        compiler_params=_params(("parallel", "parallel")),
        name="nsa_sample_main",
    )(abk, abv, b1.reshape(2, 1, NSA_HD), w2.astype(BF16), ya,
      wk.reshape(B * wb * NSA_KVH, NSA_HD), wv.reshape(B * wb * NSA_KVH, NSA_HD), kw16, vw16, bias_c, bias_w)


NEAR_BLOCKS = 3


def _nsa_sample_sel_body(idx_ref, pt_ref, q_ref, kn_ref, vn_ref, tbl_ref, ocmp_ref, owin_ref, gb_ref, bg_ref, zb_ref,
                         *refs, T):
    del pt_ref
    k_blocks = refs[:N_SEL]
    v_blocks = refs[N_SEL:2 * N_SEL]
    o_ref, osel = refs[2 * N_SEL:]
    tp = SAMPLE_PAD_T
    rows = NSA_G * tp
    b, h, t = pl.program_id(0), pl.program_id(1), pl.program_id(2)
    base = ((b * NSA_KVH + h) * T + t) * N_SEL
    first_new = PAST_LEN // SEL_BLOCK
    cur = jnp.right_shift(PAST_LEN + t, SEL_SHIFT)
    q = _sample_q_rows(q_ref)
    pad = jnp.zeros((SEL_BLOCK - tp, NSA_HD), BF16)
    k_new = jnp.concatenate([kn_ref[...], pad], axis=0)
    v_new = jnp.concatenate([vn_ref[...], pad], axis=0)
    lane = _iota2((1, LANES), 1)
    low = lane < SEL_BLOCK
    within = jnp.bitwise_and(lane, SEL_BLOCK - 1)
    ks, vs, bias, kpos = [], [], [], []
    for i in range(0, N_SEL, 2):
        pair_bias, pair_pos = [], []
        for j in (i, i + 1):
            blk = idx_ref[base + j]
            is_new = blk >= first_new
            ks.append(jnp.where(is_new, k_new, k_blocks[j][pl.ds(h, SEL_BLOCK, stride=NSA_KVH), :].astype(BF16)))
            vs.append(jnp.where(is_new, v_new, v_blocks[j][pl.ds(h, SEL_BLOCK, stride=NSA_KVH), :].astype(BF16)))
            pair_bias.append(tbl_ref[jnp.clip(blk - (first_new - NEAR_BLOCKS), 0, NEAR_BLOCKS)])
            pair_pos.append(jnp.where(blk <= cur, blk * SEL_BLOCK, PAST_LEN + SEL_BLOCK * LANES) + within)
        bias.append(jnp.where(low, pair_bias[0], pair_bias[1]))
        kpos.append(jnp.where(low, pair_pos[0], pair_pos[1]))
    k_all = jnp.concatenate(ks, axis=0)
    v_all = jnp.concatenate(vs, axis=0)
    step = jnp.bitwise_and(_iota2((rows, 1), 0), tp - 1)
    ok = jnp.concatenate(kpos, axis=1) <= PAST_LEN + step
    p, l = _softmax_rows(jnp.where(ok, _dot_nt(q, k_all) + jnp.concatenate(bias, axis=1), NEG_INF))
    o = _dot(p.astype(BF16), v_all) / jnp.maximum(l, TINY)

    @pl.when(t == 0)
    def _():
        osel[...] = jnp.zeros_like(osel)

    osel[...] = jnp.where(step == t, o, osel[...])

    @pl.when(t == T - 1)
    def _():
        gate = jax.nn.sigmoid(gb_ref[...] + bg_ref[...])
        zb = _silu(zb_ref[...])
        for g in range(NSA_G):
            r = slice(g * tp, (g + 1) * tp)
            mix = (gate[:, g:g + 1] * ocmp_ref[r, :] + gate[:, NSA_G + g:NSA_G + g + 1] * osel[r, :]
                   + gate[:, 2 * NSA_G + g:2 * NSA_G + g + 1] * owin_ref[r, :])
            sl = slice(g * NSA_HD, (g + 1) * NSA_HD)
            o_ref[:, sl] = (mix * zb[:, sl]).astype(o_ref.dtype)


def _nsa_sample_sel_call(ya, yb, ks16, vs16, idx, page_table, pool_k, pool_v, tbl, o_cmp, o_win, bg_r, *, B, T):
    tp = SAMPLE_PAD_T
    rows = NSA_G * tp
    gw = NSA_G * NSA_HD
    n_pages = page_table.shape[1]
    halves = PAGE_SIZE // SEL_BLOCK
    idx_flat = idx[:, :, :T, :N_SEL].reshape(-1)
    view_k, view_v = _pool_rows(pool_k), _pool_rows(pool_v)

    def blk_spec(j):
        def index(b, h, t, idx_s, pt_s):
            blk = idx_s[((b * NSA_KVH + h) * T + t) * N_SEL + j]
            page = pt_s[b * n_pages + jnp.minimum(blk // halves, n_pages - 1)]
            return (page * halves + blk % halves, 0)
        return pl.BlockSpec((SEL_BLOCK * NSA_KVH, NSA_HD), index)

    o_spec = pl.BlockSpec((None, None, rows, NSA_HD), lambda b, h, t, *_: (b, h, 0, 0))
    grid_spec = pltpu.PrefetchScalarGridSpec(
        num_scalar_prefetch=2,
        grid=(B, NSA_KVH, T),
        in_specs=[pl.BlockSpec((tp, gw), lambda b, h, t, *_: (b, EVEN_A["qb"] // gw + h)),
                  pl.BlockSpec((tp, NSA_HD), lambda b, h, t, *_: (b, h)),
                  pl.BlockSpec((tp, NSA_HD), lambda b, h, t, *_: (b, h)),
                  pl.BlockSpec((None, NEAR_BLOCKS + 1, rows, LANES), lambda b, h, t, *_: (h, 0, 0, 0)),
                  o_spec, o_spec,
                  pl.BlockSpec((tp, LANES), lambda b, h, t, *_: (b, EVEN_B["gb"] // LANES + h)),
                  pl.BlockSpec((None, 1, LANES), lambda b, h, t, *_: (h, 0, 0)),
                  pl.BlockSpec((tp, gw), lambda b, h, t, *_: (b, EVEN_B["zb"] // gw + h))]
        + [blk_spec(j) for j in range(N_SEL)] * 2,
        out_specs=pl.BlockSpec((tp, gw), lambda b, h, t, *_: (b, h)),
        scratch_shapes=[pltpu.VMEM((rows, NSA_HD), F32)],
    )
    return pl.pallas_call(
        functools.partial(_nsa_sample_sel_body, T=T),
        grid_spec=grid_spec,
        out_shape=jax.ShapeDtypeStruct((B * tp, NSA_W), BF16),
        compiler_params=_params(("arbitrary", "arbitrary", "arbitrary")),
        name="nsa_sample_sel",
    )(idx_flat, page_table.reshape(-1), ya, ks16, vs16, tbl, o_cmp, o_win, yb, bg_r, yb,
      *([view_k] * N_SEL), *([view_v] * N_SEL))


def _sample_bias_tables(rel_bias, T, wb):
    tp = SAMPLE_PAD_T
    ncmp = PAST_LEN // CMP_STRIDE
    wlen = -(-(wb + tp) // LANES) * LANES
    first = PAST_LEN // SEL_BLOCK - NEAR_BLOCKS
    assert PAST_LEN - ((first + 1) * SEL_BLOCK - 1) >= REL_MAX_DIST
    lo, hi = -wlen, PAST_LEN + tp
    rev = _bias_line(rel_bias, lo, hi, descending=True)

    def rows(tbl):
        return tbl.reshape(NSA_KVH, NSA_G * tp, tbl.shape[-1])

    t_c = _toeplitz(rev, hi - 1 - (PAST_LEN - (CMP_BLOCK - 1)), tp, CMP_STRIDE * ncmp)[:, :, ::CMP_STRIDE]
    t_w = _toeplitz(rev, hi - 1 - wb, tp, wlen)
    far = jnp.broadcast_to(rev[:, hi - 1 - REL_MAX_DIST][:, None, None], (NSA_HEADS, tp, LANES))
    near = []
    for k in range(1, NEAR_BLOCKS + 1):
        half = _toeplitz(rev, hi - 1 - (PAST_LEN - (first + k) * SEL_BLOCK), tp, SEL_BLOCK)
        near.append(jnp.concatenate([half, half], axis=-1))
    t_s = jnp.stack([far] + near, axis=1).reshape(NSA_KVH, NSA_G, NEAR_BLOCKS + 1, tp, LANES)
    t_s = t_s.transpose(0, 2, 1, 3, 4).reshape(NSA_KVH, NEAR_BLOCKS + 1, NSA_G * tp, LANES)
    return rows(t_c), rows(t_w), t_s


def _split_even(w):
    kv0 = EVEN_KV_OFF
    g0 = kv0 + 6 * NSA_KV_W
    zb0 = g0 + 3 * NSA_HEADS
    cols = [w[:, zb0:zb0 + NSA_W + MEM_W]]
    for h in range(NSA_KVH):
        for j in range(3):
            cols.append(w[:, g0 + j * NSA_HEADS + h * NSA_G:g0 + j * NSA_HEADS + (h + 1) * NSA_G])
        cols.append(jnp.zeros((w.shape[0], LANES - 3 * NSA_G), w.dtype))
    w_kv = [w[:, kv0 + j * NSA_KV_W:kv0 + (j + 1) * NSA_KV_W].astype(BF16) for j in range(6)]
    return w[:, :EVEN_A_N].astype(BF16), jnp.concatenate(cols, axis=1).astype(BF16), w_kv


def _split_odd(w):
    ig0 = ODD_A_N
    fg0 = ig0 + ML_HEADS
    z0 = fg0 + ML_HEADS
    cols = [w[:, z0:z0 + ML_V_W + MEM_W]]
    for hg in range(ML_HEADS // ML_HB):
        cols.append(w[:, ig0 + hg * ML_HB:ig0 + (hg + 1) * ML_HB])
        cols.append(w[:, fg0 + hg * ML_HB:fg0 + (hg + 1) * ML_HB])
        cols.append(jnp.zeros((w.shape[0], LANES - 2 * ML_HB), w.dtype))
    return w[:, :ODD_A_N].astype(BF16), jnp.concatenate(cols, axis=1).astype(BF16)


def _gate_bias_even(b_gate):
    g = b_gate.reshape(3, NSA_KVH, NSA_G).transpose(1, 0, 2).reshape(NSA_KVH, 1, 3 * NSA_G)
    return jnp.pad(g, ((0, 0), (0, 0), (0, LANES - 3 * NSA_G)))


def _gate_bias_odd(b_if):
    g = b_if.reshape(2, ML_HEADS // ML_HB, ML_HB).transpose(1, 0, 2).reshape(ML_HEADS // ML_HB, 1, 2 * ML_HB)
    return jnp.pad(g, ((0, 0), (0, 0), (0, LANES - 2 * ML_HB)))


def _rel_bucket(dist):
    n = np.maximum(dist, 0)
    exact = REL_BUCKETS // 2
    nf = np.maximum(n, 1).astype(np.float32)
    large = exact + (np.log(nf / exact) / math.log(REL_MAX_DIST / exact) * (REL_BUCKETS - exact)).astype(np.int32)
    return np.where(n < exact, n, np.minimum(large, REL_BUCKETS - 1))


def _bias_line(rel_bias, lo, hi, descending=False):
    dist = np.arange(hi - 1, lo - 1, -1) if descending else np.arange(lo, hi)
    buckets = _rel_bucket(dist)
    edges = np.flatnonzero(np.diff(buckets)) + 1
    starts = np.concatenate([[0], edges])
    ends = np.concatenate([edges, [hi - lo]])
    bias_t = rel_bias.T.astype(F32)
    runs = [jnp.broadcast_to(bias_t[:, int(buckets[s])][:, None], (NSA_HEADS, int(e - s))) for s, e in zip(starts, ends)]
    return jnp.concatenate(runs, axis=1)


def _skew_rows(v, rows, step, cols):
    n = v.shape[1]
    reps = -(-rows * (n + step) // n)
    return jnp.tile(v, (1, reps))[:, :rows * (n + step)].reshape(v.shape[0], rows, n + step)[:, :, :cols]


def _toeplitz(rev, start, rows, cols):
    seg = rev[:, start - (rows - 1):start + cols]
    return _skew_rows(jnp.roll(seg, -(rows - 1), axis=1), rows, -1, cols)


def _prompt_bias_tables(rel_bias, T):
    ncmp = T // CMP_STRIDE
    wlen = WINDOW + Q_BLOCK
    lo, hi = -(CMP_STRIDE * ncmp + CMP_BLOCK), T
    line = _bias_line(rel_bias, lo, hi)
    rev = _bias_line(rel_bias, lo, hi, descending=True)

    def split(tbl):
        return tbl.reshape((NSA_KVH, NSA_G) + tbl.shape[1:])

    back = CMP_STRIDE * (ncmp - 1)
    first = -(back + CMP_BLOCK - 1) - lo
    seg = line[:, first:first + T + back]
    t_c = _skew_rows(jnp.roll(seg, -back, axis=1), ncmp, -CMP_STRIDE, T).swapaxes(1, 2)
    t_s = _toeplitz(rev, hi - 1 - (T - Q_BLOCK), Q_BLOCK, T)
    t_w = _toeplitz(rev, hi - 1 - WINDOW, Q_BLOCK, wlen)
    return split(t_c), split(t_s), split(t_w)


def _nsa_sample(ya, yb, kv16, page_table, pk_cmp, pv_cmp, pk_sel, pv_sel, wk, wv, bg_r, w1, b1, w2, pe, rel_bias,
                *, B, T):
    assert (PAST_LEN + T) // CMP_STRIDE == PAST_LEN // CMP_STRIDE
    abk = _cmp_pages_call(pk_cmp, page_table, w1[0], pe[0], B=B)
    abv = _cmp_pages_call(pv_cmp, page_table, w1[1], pe[1], B=B)
    bias_c, bias_w, tbl = _sample_bias_tables(rel_bias, T, wk.shape[1])
    o_cmp, o_win, idx = _nsa_sample_main_call(ya, kv16[4], kv16[5], abk, abv, b1, w2, wk, wv, bias_c, bias_w, B=B, T=T)
    return _nsa_sample_sel_call(ya, yb, kv16[2], kv16[3], idx, page_table, pk_sel, pv_sel, tbl, o_cmp, o_win, bg_r,
                                B=B, T=T)


def _kv_project(x, w_kv):
    outs = [_matmul_heads(x, w) for w in w_kv]
    return [o[0] for o in outs], [o[1] for o in outs]


def _even_prompt(hp2d, npre, mk16, mv16, w_a, w_b, w_kv, bg_r, w1, b1, w2, pe, lb, g_norm, w_out, rel_bias, *, B, T):
    ya, yb = _matmul(npre, w_a), _matmul(npre, w_b)
    kv32, kv16 = _kv_project(npre, w_kv)
    oa, s_new = _hgrn_call(ya, jnp.zeros((B, HG_HEADS, HG_DK, HG_DV), F32), lb, g_norm, B=B, T=T, L=CHUNK, valid=CHUNK)
    kcmp = _compress_call(kv16[0], w1[0], b1[0], w2[0], pe[0], B=B, T=T)
    vcmp = _compress_call(kv16[1], w1[1], b1[1], w2[1], pe[1], B=B, T=T)
    ob = _nsa_prompt_call(ya, yb, kv16[2:], kcmp, vcmp, bg_r, *_prompt_bias_tables(rel_bias, T), B=B, T=T)
    om = _mem_call(yb, EVEN_B["qm"], mk16, mv16, B=B, T=T)
    h_new = _outproj([oa, ob, om], w_out, hp2d)
    wb = min(WINDOW, T)
    rows = [r.reshape(B, T, NSA_KVH, NSA_HD) for r in kv32]
    return h_new, (rows[0], rows[1], rows[2], rows[3], rows[4][:, -wb:], rows[5][:, -wb:], s_new)


def _even_sample(hs2d, nsam, mk_s, mv_s, page_table, pk_cmp, pv_cmp, pk_sel, pv_sel, wk, wv, s0,
                 w_a, w_b, w_kv, bg_r, w1, b1, w2, pe, lb, g_norm, w_out, rel_bias, *, B, T):
    tp = SAMPLE_PAD_T
    ya, yb = _matmul(nsam, w_a), _matmul(nsam, w_b)
    kv32, kv16 = _kv_project(nsam, w_kv)
    oa, s_new = _hgrn_call(ya, s0, lb, g_norm, B=B, T=tp, L=tp, valid=T)
    ob = _nsa_sample(ya, yb, kv16, page_table, pk_cmp, pv_cmp, pk_sel, pv_sel, wk, wv, bg_r, w1, b1, w2, pe, rel_bias,
                     B=B, T=T)
    om = _mem_call(yb, EVEN_B["qm"], mk_s.reshape(B * N_MEM, MEM_W), mv_s.reshape(B * N_MEM, MEM_W), B=B, T=tp)
    rows = [r.reshape(B, tp, NSA_KVH, NSA_HD)[:, :T] for r in kv32]
    wb = wk.shape[1]
    win_k = jnp.concatenate([wk, rows[4]], axis=1)[:, -wb:]
    win_v = jnp.concatenate([wv, rows[5]], axis=1)[:, -wb:]
    return _outproj([oa, ob, om], w_out, hs2d), (rows[0], rows[1], rows[2], rows[3], win_k, win_v, s_new)


def _odd_mix(h2d, hn, k2d, v2d, c0, n0, m0, w_a, w_b, bif_r, g_norm, w_out, *, B, T, L, valid):
    ya, yb = _matmul(hn, w_a), _matmul(hn, w_b)
    h, c_new, n_new, m_new = _mlstm_call(ya, yb, c0, n0, m0, bif_r, g_norm, B=B, T=T, L=L, valid=valid)
    om = _mem_call(yb, ODD_B["qm"], k2d, v2d, B=B, T=T)
    return _outproj([h, om], w_out, h2d), (c_new, n_new, m_new)


def _stack(lst, i):
    return jnp.stack([t[i] for t in lst])


def kernel(x_prompt, x_sample, cache_mem_k, cache_mem_v, cache_cmp_k, cache_cmp_v, cache_sel_k, cache_sel_v,
           cache_win_k, cache_win_v, state_hgrn, state_mlstm_c, state_mlstm_n, state_mlstm_m, page_table,
           mem_prompt, norm_w, mem_norm_w, final_norm_w, rel_bias, w_mem_kv, w_in_even, b_nsa_gate,
           w_cmp1, b_cmp1, w_cmp2, pe_cmp, hgrn_lb_logits, hgrn_norm_w, w_out_even, w_in_odd, b_mlstm_if,
           mlstm_norm_w, w_out_odd):
    bp, tp = x_prompt.shape[:2]
    bs, ts = x_sample.shape[:2]
    tsp = SAMPLE_PAD_T
    lbs = jnp.cumsum(jax.nn.softmax(hgrn_lb_logits.astype(F32), axis=0), axis=0)
    hp = x_prompt.reshape(bp * tp, D_MODEL)
    hs = jnp.pad(x_sample, ((0, 0), (0, tsp - ts), (0, 0))).reshape(bs * tsp, D_MODEL)
    mem2d = mem_prompt.reshape(bp * N_MEM, D_MODEL)
    mem_new, even_p, even_s, odd_p, odd_s = [], [], [], [], []
    for l in range(DEPTH):
        npre = _rmsnorm_rows(hp, norm_w[l], BF16)
        nsam = _rmsnorm_rows(hs, norm_w[l], BF16)
        nmem = _rmsnorm_rows(mem2d, mem_norm_w[l], BF16)
        mk32, mk16 = _matmul_heads(nmem, w_mem_kv[l][:, :MEM_W].astype(BF16))
        mv32, mv16 = _matmul_heads(nmem, w_mem_kv[l][:, MEM_W:].astype(BF16))
        mem_new.append((mk32.reshape(bp, N_MEM, MEM_HEADS, MEM_HD), mv32.reshape(bp, N_MEM, MEM_HEADS, MEM_HD)))
        mk_s, mv_s = cache_mem_k[l], cache_mem_v[l]
        if l % 2 == 0:
            e = l // 2
            w_a, w_b, w_kv = _split_even(w_in_even[e])
            w_out = w_out_even[e].astype(BF16)
            bg_r = _gate_bias_even(b_nsa_gate[e])
            cmpw = (w_cmp1[e].reshape(2, CMP_BLOCK, NSA_HD, NSA_HD), b_cmp1[e], w_cmp2[e], pe_cmp[e])
            hp, st_p = _even_prompt(hp, npre, mk16, mv16, w_a, w_b, w_kv, bg_r, *cmpw, lbs[l], hgrn_norm_w[e], w_out,
                                    rel_bias, B=bp, T=tp)
            hs, st_s = _even_sample(hs, nsam, mk_s, mv_s, page_table, cache_cmp_k[e], cache_cmp_v[e], cache_sel_k[e],
                                    cache_sel_v[e], cache_win_k[e], cache_win_v[e], state_hgrn[e], w_a, w_b, w_kv, bg_r,
                                    *cmpw, lbs[l], hgrn_norm_w[e], w_out, rel_bias, B=bs, T=ts)
            even_p.append(st_p)
            even_s.append(st_s)
        else:
            o = l // 2
            w_a, w_b = _split_odd(w_in_odd[o])
            w_out = w_out_odd[o].astype(BF16)
            bif_r = _gate_bias_odd(b_mlstm_if[o])
            hp, st_p = _odd_mix(hp, npre, mk16, mv16, jnp.zeros((bp, ML_HEADS, ML_DV, ML_DK), F32),
                                jnp.zeros((bp, ML_HEADS, ML_DK), F32), jnp.zeros((bp, ML_HEADS), F32),
                                w_a, w_b, bif_r, mlstm_norm_w[o], w_out, B=bp, T=tp, L=ML_CHUNK, valid=ML_CHUNK)
            hs, st_s = _odd_mix(hs, nsam, mk_s.reshape(bs * N_MEM, MEM_W), mv_s.reshape(bs * N_MEM, MEM_W),
                                state_mlstm_c[o], state_mlstm_n[o], state_mlstm_m[o],
                                w_a, w_b, bif_r, mlstm_norm_w[o], w_out, B=bs, T=tsp, L=tsp, valid=ts)
            odd_p.append(st_p)
            odd_s.append(st_s)
    y_prompt = _rmsnorm_rows(hp, final_norm_w, F32).reshape(bp, tp, D_MODEL)
    y_sample = _rmsnorm_rows(hs, final_norm_w, F32).reshape(bs, tsp, D_MODEL)[:, :ts]
    return (y_prompt, y_sample,
            _stack(mem_new, 0), _stack(mem_new, 1),
            _stack(even_p, 0), _stack(even_p, 1), _stack(even_p, 2), _stack(even_p, 3),
            _stack(even_p, 4), _stack(even_p, 5), _stack(even_p, 6),
            _stack(odd_p, 0), _stack(odd_p, 1), _stack(odd_p, 2),
            _stack(even_s, 0), _stack(even_s, 1), _stack(even_s, 2), _stack(even_s, 3),
            _stack(even_s, 4), _stack(even_s, 5), _stack(even_s, 6),
            _stack(odd_s, 0), _stack(odd_s, 1), _stack(odd_s, 2))
```

```python
import functools
import math

import jax
import jax.numpy as jnp
import numpy as np
from jax import lax
from jax.experimental import pallas as pl
from jax.experimental.pallas import tpu as pltpu

D_MODEL = 4096
DEPTH = 2
PAST_LEN = 16384
PAGE_SIZE = 128
N_MEM = 256
EPS = 1e-6
CHUNK = 64

HG_DK = 128
HG_DV = 128
HG_HEADS = D_MODEL // 2 // HG_DV
HG_W = HG_HEADS * HG_DV

NSA_HD = 128
NSA_HEADS = D_MODEL // 2 // NSA_HD
NSA_KVH = 4
NSA_G = NSA_HEADS // NSA_KVH
NSA_W = NSA_HEADS * NSA_HD
NSA_KV_W = NSA_KVH * NSA_HD
CMP_BLOCK = 32
CMP_STRIDE = 16
SEL_BLOCK = 64
SEL_SHIFT = SEL_BLOCK.bit_length() - 1
N_SEL = 16
WINDOW = 512
Q_BLOCK = 128

ML_HEADS = D_MODEL // 512
ML_DK = D_MODEL // 2 // ML_HEADS
ML_DV = D_MODEL // ML_HEADS
ML_QK_W = ML_HEADS * ML_DK
ML_V_W = ML_HEADS * ML_DV

MEM_HEADS = 4
MEM_HD = 128
MEM_W = MEM_HEADS * MEM_HD

REL_BUCKETS = 32
REL_MAX_DIST = 128

F32 = jnp.float32
BF16 = jnp.bfloat16
LANES = 128
NEG_INF = float("-inf")
TINY = float(np.finfo(np.float32).tiny)
EXP_CLAMP = 80.0
VMEM_LIMIT = 56 * 1024 * 1024

HG_HB = 8
ML_HB = 2
ML_CHUNK = 256
W_TILE_N = 512
HG_SUB = 16
SAMPLE_PAD_T = 16

EVEN_A = {"qa": 0, "fa": HG_W, "ia": 2 * HG_W, "za": 3 * HG_W, "qb": 4 * HG_W}
EVEN_A_N = 4 * HG_W + NSA_W
EVEN_B = {"zb": 0, "qm": NSA_W, "gb": NSA_W + MEM_W}
EVEN_B_N = NSA_W + MEM_W + NSA_KVH * LANES
EVEN_KV_OFF = EVEN_A_N
ODD_A = {"q": 0, "k": ML_QK_W, "v": 2 * ML_QK_W, "og": 2 * ML_QK_W + ML_V_W}
ODD_A_N = 2 * ML_QK_W + 2 * ML_V_W
ODD_B = {"z": 0, "qm": ML_V_W, "gates": ML_V_W + MEM_W}
ODD_B_N = ML_V_W + MEM_W + (ML_HEADS // ML_HB) * LANES


def _dot(a, b):
    return jnp.dot(a, b, preferred_element_type=F32)


def _dot_nt(a, b):
    return lax.dot_general(a, b, (((1,), (1,)), ((), ())), preferred_element_type=F32)


def _dot_tn(a, b):
    return lax.dot_general(a, b, (((0,), (0,)), ((), ())), preferred_element_type=F32)


def _iota2(shape, dim):
    return lax.broadcasted_iota(jnp.int32, shape, dim)


def _cumsum_rows(x, tri_b):
    hi = x.astype(BF16)
    r1 = x - hi.astype(F32)
    mid = r1.astype(BF16)
    lo = (r1 - mid.astype(F32)).astype(BF16)
    return _dot(tri_b, hi) + _dot(tri_b, mid) + _dot(tri_b, lo)


def _row_to_col(row, n):
    eye = _iota2((n, n), 0) == _iota2((n, n), 1)
    return jnp.sum(jnp.where(eye, row, 0.0), axis=1, keepdims=True)


def _col_to_row(col, n):
    eye = _iota2((n, n), 0) == _iota2((n, n), 1)
    return jnp.sum(jnp.where(eye, col, 0.0), axis=0, keepdims=True)


def _silu(x):
    return x * jax.nn.sigmoid(x)


def _params(sem):
    return pltpu.CompilerParams(dimension_semantics=sem, vmem_limit_bytes=VMEM_LIMIT)


def _rmsnorm_body(x_ref, w_ref, o_ref):
    x = x_ref[...].astype(F32)
    y = x * lax.rsqrt(jnp.mean(x * x, axis=-1, keepdims=True) + EPS)
    o_ref[...] = (y * w_ref[...].astype(F32)).astype(o_ref.dtype)


def _rmsnorm_rows(x2d, w, out_dtype, tm=256):
    m, d = x2d.shape
    tm = min(tm, m)
    return pl.pallas_call(
        _rmsnorm_body,
        grid=(m // tm,),
        in_specs=[pl.BlockSpec((tm, d), lambda i: (i, 0)), pl.BlockSpec((1, d), lambda i: (0, 0))],
        out_specs=pl.BlockSpec((tm, d), lambda i: (i, 0)),
        out_shape=jax.ShapeDtypeStruct((m, d), out_dtype),
        compiler_params=_params(("parallel",)),
        name="rmsnorm",
    )(x2d, w.reshape(1, d))


def _matmul_body(a_ref, b_ref, o_ref):
    o_ref[...] = _dot(a_ref[...], b_ref[...].astype(BF16))


def _matmul(a, b, tm=1024, tn=1024, cols=None):
    m, k = a.shape
    first, n = cols or (0, b.shape[1])
    tm, tn = min(tm, m), min(tn, n)
    assert m % tm == 0 and n % tn == 0 and first % tn == 0, (a.shape, b.shape, cols)
    j0 = first // tn
    return pl.pallas_call(
        _matmul_body,
        grid=(m // tm, n // tn),
        in_specs=[pl.BlockSpec((tm, k), lambda i, j: (i, 0)), pl.BlockSpec((k, tn), lambda i, j: (0, j0 + j))],
        out_specs=pl.BlockSpec((tm, tn), lambda i, j: (i, j)),
        out_shape=jax.ShapeDtypeStruct((m, n), F32),
        compiler_params=_params(("parallel", "parallel")),
        name="matmul",
    )(a, b)


def _matmul_heads_body(a_ref, b_ref, o32_ref, o16_ref):
    acc = _dot(a_ref[...], b_ref[...].astype(BF16))
    for h in range(MEM_HEADS):
        o32_ref[:, h, :] = acc[:, h * LANES:(h + 1) * LANES]
    o16_ref[...] = acc.astype(BF16)


def _matmul_heads(a, b, first=0, tm=1024):
    m, k = a.shape
    n = MEM_HEADS * LANES
    tm = min(tm, m)
    assert m % tm == 0 and first % n == 0, (a.shape, b.shape, first)
    j0 = first // n
    return pl.pallas_call(
        _matmul_heads_body,
        grid=(m // tm,),
        in_specs=[pl.BlockSpec((tm, k), lambda i: (i, 0)), pl.BlockSpec((k, n), lambda i: (0, j0))],
        out_specs=[pl.BlockSpec((tm, MEM_HEADS, LANES), lambda i: (i, 0, 0)), pl.BlockSpec((tm, n), lambda i: (i, 0))],
        out_shape=[jax.ShapeDtypeStruct((m, MEM_HEADS, LANES), F32), jax.ShapeDtypeStruct((m, n), BF16)],
        compiler_params=_params(("parallel",)),
        name="matmul_heads",
    )(a, b)


def _outproj_body(*refs, widths):
    xs = refs[:len(widths)]
    w_ref, r_ref, o_ref = refs[len(widths):]
    acc = r_ref[...]
    off = 0
    for x_ref, w in zip(xs, widths):
        acc = acc + _dot(x_ref[...], w_ref[off:off + w, :])
        off += w
    o_ref[...] = acc


def _outproj(xs, w_bf16, resid, tm=1024, tn=512):
    m = resid.shape[0]
    n = w_bf16.shape[1]
    widths = tuple(x.shape[1] for x in xs)
    assert sum(widths) == w_bf16.shape[0]
    tm = min(tm, m)
    in_specs = [pl.BlockSpec((tm, w), lambda i, j: (i, 0)) for w in widths]
    in_specs += [pl.BlockSpec((w_bf16.shape[0], tn), lambda i, j: (0, j)), pl.BlockSpec((tm, tn), lambda i, j: (i, j))]
    return pl.pallas_call(
        functools.partial(_outproj_body, widths=widths),
        grid=(m // tm, n // tn),
        in_specs=in_specs,
        out_specs=pl.BlockSpec((tm, tn), lambda i, j: (i, j)),
        out_shape=jax.ShapeDtypeStruct((m, n), F32),
        compiler_params=_params(("parallel", "parallel")),
        name="outproj",
    )(*xs, w_bf16, resid)


def _hgrn_body(qa_ref, fa_ref, ia_ref, za_ref, lb_ref, gn_ref, s0_ref, o_ref, s_out, s_scr, *, L, valid):
    c = pl.program_id(2)

    @pl.when(c == 0)
    def _():
        s_scr[...] = s0_ref[...]

    lb = lb_ref[...]
    sig = jax.nn.sigmoid(fa_ref[...])
    logf = jnp.log(lb + (1.0 - lb) * sig)
    kk = (1.0 - lb) * (1.0 - sig)
    if valid < L:
        live = _iota2((L, 1), 0) < valid
        logf = jnp.where(live, logf, 0.0)
        kk = jnp.where(live, kk, 0.0)
    tri_b = (_iota2((L, L), 0) >= _iota2((L, L), 1)).astype(BF16)
    bc = _cumsum_rows(logf, tri_b)
    q = _silu(qa_ref[...])
    gate = _silu(za_ref[...])
    v = ia_ref[...]
    gn = gn_ref[...]
    nsub = L // HG_SUB
    rr = _iota2((L, nsub * L), 0)
    cc = _iota2((L, nsub * L), 1)
    keep = ((jnp.right_shift(cc, L.bit_length() - 1) == jnp.right_shift(rr, HG_SUB.bit_length() - 1))
            & (jnp.bitwise_and(cc, L - 1) <= rr))
    for j in range(HG_HB):
        sl = slice(j * HG_DK, (j + 1) * HG_DK)
        bj, qj, kj = bc[:, sl], q[:, sl], kk[:, sl]
        vb = v[:, sl].astype(BF16)
        s_prev = s_scr[j]
        inter = _dot((qj * jnp.exp(bj)).astype(BF16), s_prev.astype(BF16))
        mids = [bj[i * HG_SUB + HG_SUB // 2:i * HG_SUB + HG_SUB // 2 + 1, :] for i in range(nsub)]
        mid_rows = jnp.concatenate([jnp.broadcast_to(m, (HG_SUB, HG_DK)) for m in mids], axis=0)
        q_dec = qj * jnp.exp(jnp.minimum(bj - mid_rows, EXP_CLAMP))
        k_dec = jnp.concatenate([kj * jnp.exp(jnp.minimum(m - bj, EXP_CLAMP)) for m in mids], axis=0)
        att = jnp.where(keep, _dot_nt(q_dec.astype(BF16), k_dec.astype(BF16)), 0.0)
        o = inter + _dot(att.astype(BF16), jnp.concatenate([vb] * nsub, axis=0))
        o_n = o * lax.rsqrt(jnp.mean(o * o, axis=-1, keepdims=True) + EPS) * gn
        o_ref[:, sl] = (o_n * gate[:, sl]).astype(o_ref.dtype)
        bl = bj[L - 1:L, :]
        kd = kj * jnp.exp(bl - bj)
        s_scr[j] = _row_to_col(jnp.exp(bl), HG_DK) * s_prev + _dot_tn(kd.astype(BF16), vb)

    @pl.when(c == pl.num_programs(2) - 1)
    def _():
        s_out[...] = s_scr[...]


def _hgrn_call(y, s0, lb, gn, *, B, T, L, valid):
    nc = T // L
    w = HG_HB * HG_DK

    def col(name):
        blk = EVEN_A[name] // w
        return pl.BlockSpec((L, w), lambda b, hg, c: (b * nc + c, blk + hg))

    state_spec = pl.BlockSpec((None, HG_HB, HG_DK, HG_DV), lambda b, hg, c: (b, hg, 0, 0))
    return pl.pallas_call(
        functools.partial(_hgrn_body, L=L, valid=valid),
        grid=(B, HG_HEADS // HG_HB, nc),
        in_specs=[col("qa"), col("fa"), col("ia"), col("za"),
                  pl.BlockSpec((1, w), lambda b, hg, c: (0, hg)),
                  pl.BlockSpec((1, HG_DV), lambda b, hg, c: (0, 0)),
                  state_spec],
        out_specs=[pl.BlockSpec((L, w), lambda b, hg, c: (b * nc + c, hg)), state_spec],
        out_shape=[jax.ShapeDtypeStruct((B * T, HG_W), BF16),
                   jax.ShapeDtypeStruct((B, HG_HEADS, HG_DK, HG_DV), F32)],
        scratch_shapes=[pltpu.VMEM((HG_HB, HG_DK, HG_DV), F32)],
        compiler_params=_params(("arbitrary", "arbitrary", "arbitrary")),
        name="hgrn2",
    )(y, y, y, y, lb.reshape(1, HG_W), gn.reshape(1, HG_DV), s0)


def _mlstm_body(q_ref, k_ref, v_ref, og_ref, z_ref, g_ref, bif_ref, gn_ref, c0_ref, n0_ref, m0_ref,
                h_ref, c_out, n_out, m_out, c_scr, n_scr, m_scr, *, L, valid):
    c = pl.program_id(2)

    @pl.when(c == 0)
    def _():
        c_scr[...] = c0_ref[...]
        n_scr[...] = n0_ref[...]
        m_scr[...] = m0_ref[...]

    gates = g_ref[...] + bif_ref[...]
    log_i = gates
    log_f = jnp.minimum(gates, 0.0) - jnp.log(1.0 + jnp.exp(-jnp.abs(gates)))
    if valid < L:
        live = _iota2((L, 1), 0) < valid
        log_i = jnp.where(live, log_i, -1e30)
        log_f = jnp.where(live, log_f, 0.0)
    tri = _iota2((L, L), 0) >= _iota2((L, L), 1)
    bcs = _cumsum_rows(log_f, tri.astype(BF16))
    for j in range(ML_HB):
        b_col = bcs[:, ML_HB + j:ML_HB + j + 1]
        i_col = log_i[:, j:j + 1]
        b_row = _col_to_row(b_col, L)
        i_row = _col_to_row(i_col, L)
        m_prev = m_scr[:, j:j + 1]
        dmat = jnp.where(tri, b_col - b_row + i_row, NEG_INF)
        inter = b_col + m_prev
        mt = jnp.maximum(inter, jnp.max(dmat, axis=1, keepdims=True))
        w_in = jnp.exp(dmat - mt)
        w_x = jnp.exp(inter - mt)
        qj = q_ref[:, j * ML_DK:(j + 1) * ML_DK]
        kj = k_ref[:, j * ML_DK:(j + 1) * ML_DK] * (ML_DK ** -0.5)
        vj = v_ref[:, j * ML_DV:(j + 1) * ML_DV]
        qb, kb = qj.astype(BF16), kj.astype(BF16)
        sw = _dot_nt(qb, kb) * w_in
        c_prev = c_scr[j]
        n_prev = n_scr[:, j * ML_DK:(j + 1) * ML_DK]
        num = w_x * _dot_nt(qb, c_prev.astype(BF16)) + _dot(sw.astype(BF16), vj.astype(BF16))
        den = w_x * jnp.sum(qj * n_prev, axis=1, keepdims=True) + jnp.sum(sw, axis=1, keepdims=True)
        h = num / jnp.maximum(jnp.abs(den), jnp.exp(-mt))
        m_last = mt[L - 1:L, :]
        b_last = b_col[L - 1:L, :]
        w_end = jnp.exp(b_last - b_col + i_col - m_last)
        d_c = jnp.exp(b_last + m_prev - m_last)
        c_scr[j] = d_c * c_prev + _dot_tn((w_end * vj).astype(BF16), kb)
        n_scr[:, j * ML_DK:(j + 1) * ML_DK] = d_c * n_prev + jnp.sum(w_end * kj, axis=0, keepdims=True)
        m_scr[:, j:j + 1] = m_last
        sv = slice(j * ML_DV, (j + 1) * ML_DV)
        h_n = h * lax.rsqrt(jnp.mean(h * h, axis=-1, keepdims=True) + EPS) * gn_ref[:, sv]
        h_ref[:, sv] = (h_n * jax.nn.sigmoid(og_ref[:, sv]) * _silu(z_ref[:, sv])).astype(h_ref.dtype)

    @pl.when(c == pl.num_programs(2) - 1)
    def _():
        c_out[...] = c_scr[...]
        n_out[...] = n_scr[...]
        m_out[...] = m_scr[...]


def _mlstm_call(ya, yb, c0, n0, m0, bif_r, gn, *, B, T, L, valid):
    nc = T // L
    ng = ML_HEADS // ML_HB
    wk, wv = ML_HB * ML_DK, ML_HB * ML_DV

    def col(name, w):
        blk = (ODD_A[name] if name in ODD_A else ODD_B[name]) // w
        return pl.BlockSpec((L, w), lambda b, hg, c: (b * nc + c, blk + hg))

    c_spec = pl.BlockSpec((None, ML_HB, ML_DV, ML_DK), lambda b, hg, c: (b, hg, 0, 0))
    n_spec = pl.BlockSpec((None, 1, wk), lambda b, hg, c: (b, 0, hg))
    m_spec = pl.BlockSpec((None, None, 1, LANES), lambda b, hg, c: (b, hg, 0, 0))
    m0_r = jnp.pad(m0.reshape(B, ng, 1, ML_HB), ((0, 0), (0, 0), (0, 0), (0, LANES - ML_HB)))
    h, c_new, n_new, m_new = pl.pallas_call(
        functools.partial(_mlstm_body, L=L, valid=valid),
        grid=(B, ng, nc),
        in_specs=[col("q", wk), col("k", wk), col("v", wv), col("og", wv), col("z", wv), col("gates", LANES),
                  pl.BlockSpec((None, 1, LANES), lambda b, hg, c: (hg, 0, 0)),
                  pl.BlockSpec((1, wv), lambda b, hg, c: (0, hg)),
                  c_spec, n_spec, m_spec],
        out_specs=[pl.BlockSpec((L, wv), lambda b, hg, c: (b * nc + c, hg)), c_spec, n_spec, m_spec],
        out_shape=[jax.ShapeDtypeStruct((B * T, ML_V_W), BF16),
                   jax.ShapeDtypeStruct((B, ML_HEADS, ML_DV, ML_DK), F32),
                   jax.ShapeDtypeStruct((B, 1, ML_QK_W), F32),
                   jax.ShapeDtypeStruct((B, ng, 1, LANES), F32)],
        scratch_shapes=[pltpu.VMEM((ML_HB, ML_DV, ML_DK), F32), pltpu.VMEM((1, wk), F32), pltpu.VMEM((1, LANES), F32)],
        compiler_params=_params(("arbitrary", "arbitrary", "arbitrary")),
        name="mlstm",
    )(ya, ya, ya, ya, yb, yb, bif_r, gn.reshape(1, ML_V_W), c0, n0.reshape(B, 1, ML_QK_W), m0_r)
    return h, c_new, n_new.reshape(B, ML_HEADS, ML_DK), m_new[:, :, 0, :ML_HB].reshape(B, ML_HEADS)


def _mem_body(q_ref, k_ref, v_ref, o_ref):
    q = q_ref[...] * (MEM_HD ** -0.5)
    for h in range(MEM_HEADS):
        sl = slice(h * MEM_HD, (h + 1) * MEM_HD)
        s = _dot_nt(q[:, sl].astype(BF16), k_ref[:, sl].astype(BF16))
        p = jnp.exp(s - jnp.max(s, axis=-1, keepdims=True))
        o = _dot(p.astype(BF16), v_ref[:, sl].astype(BF16)) / jnp.sum(p, axis=-1, keepdims=True)
        o_ref[:, sl] = o.astype(o_ref.dtype)


def _mem_call(y, q_off, k2d, v2d, *, B, T, tq=256):
    tq = min(tq, T)
    nq = T // tq
    qb = q_off // MEM_W
    return pl.pallas_call(
        _mem_body,
        grid=(B, nq),
        in_specs=[pl.BlockSpec((tq, MEM_W), lambda b, i: (b * nq + i, qb)),
                  pl.BlockSpec((N_MEM, MEM_W), lambda b, i: (b, 0)),
                  pl.BlockSpec((N_MEM, MEM_W), lambda b, i: (b, 0))],
        out_specs=pl.BlockSpec((tq, MEM_W), lambda b, i: (b * nq + i, 0)),
        out_shape=jax.ShapeDtypeStruct((B * T, MEM_W), BF16),
        compiler_params=_params(("parallel", "parallel")),
        name="mem_attn",
    )(y, k2d, v2d)


def _gelu_tanh(x):
    return 0.5 * x * (1.0 + jnp.tanh(math.sqrt(2.0 / math.pi) * (x + 0.044715 * (x * x * x))))


def _compress_body(x_ref, w1_ref, b1_ref, w2_ref, pe_ref, o_ref, x32, *, nch):
    x32[...] = x_ref[...].astype(F32)
    a = jnp.zeros((nch, NSA_HD), F32)
    b = jnp.zeros((nch, NSA_HD), F32)
    for s in range(CMP_STRIDE):
        r = x32[pl.ds(s, nch, stride=CMP_STRIDE), :]
        a = a + _dot((r + pe_ref[s:s + 1, :]).astype(BF16), w1_ref[s])
        b = b + _dot((r + pe_ref[CMP_STRIDE + s:CMP_STRIDE + s + 1, :]).astype(BF16), w1_ref[CMP_STRIDE + s])
    h = a + pltpu.roll(b, nch - 1, 0) + b1_ref[...]
    o_ref[...] = _dot(_gelu_tanh(h).astype(BF16), w2_ref[...])


def _compress_call(x16, w1, b1, w2, pe, *, B, T):
    nch = T // CMP_STRIDE
    return pl.pallas_call(
        functools.partial(_compress_body, nch=nch),
        grid=(B, NSA_KVH),
        in_specs=[pl.BlockSpec((T, NSA_HD), lambda b, h: (b, h)),
                  pl.BlockSpec((CMP_BLOCK, NSA_HD, NSA_HD), lambda b, h: (0, 0, 0)),
                  pl.BlockSpec((1, NSA_HD), lambda b, h: (0, 0)),
                  pl.BlockSpec((NSA_HD, NSA_HD), lambda b, h: (0, 0)),
                  pl.BlockSpec((CMP_BLOCK, NSA_HD), lambda b, h: (0, 0))],
        out_specs=pl.BlockSpec((None, None, nch, NSA_HD), lambda b, h: (b, h, 0, 0)),
        out_shape=jax.ShapeDtypeStruct((B, NSA_KVH, nch, NSA_HD), F32),
        scratch_shapes=[pltpu.VMEM((T, NSA_HD), F32)],
        compiler_params=_params(("parallel", "parallel")),
        name="nsa_compress",
    )(x16, w1.astype(BF16), b1.reshape(1, NSA_HD), w2.astype(BF16), pe)


def _softmax_rows(s):
    m = jnp.max(s, axis=-1, keepdims=True)
    m = jnp.where(m == NEG_INF, 0.0, m)
    p = jnp.exp(s - m)
    return p, jnp.sum(p, axis=-1, keepdims=True)


def _slc_scores(psum, width, n_slc):
    ncmp = psum.shape[1]
    d = _iota2((ncmp, width), 0) - (SEL_BLOCK // CMP_STRIDE) * _iota2((ncmp, width), 1)
    wgt = jnp.where((d == -1) | (d == 3), 1.0, jnp.where((d >= 0) & (d <= 2), 2.0, 0.0))
    wgt = jnp.where(_iota2((ncmp, width), 1) < n_slc, wgt, 0.0).astype(BF16)
    p_hi = psum.astype(BF16)
    p_lo = (psum - p_hi.astype(F32)).astype(BF16)
    return _dot(p_hi, wgt) + _dot(p_lo, wgt)


def _top_blocks(slc, cur, n_pick):
    rows, width = slc.shape
    blk = _iota2((rows, width), 1)
    forced = (blk == 0) | (blk == cur) | (blk == cur - 1)
    score = jnp.where(forced, jnp.inf, slc)
    score = jnp.where(blk > cur, NEG_INF, score)
    blk_f = blk.astype(F32)
    lane = _iota2((rows, LANES), 1)
    sel = jnp.zeros((rows, width), F32)
    picks = jnp.zeros((rows, LANES), F32)
    for i in range(n_pick):
        mx = jnp.max(score, axis=-1, keepdims=True)
        first = jnp.min(jnp.where(score == mx, blk_f, float(width)), axis=-1, keepdims=True)
        pick = blk_f == first
        sel = jnp.where(pick, 1.0, sel)
        picks = jnp.where(lane == i, first, picks)
        score = jnp.where(pick, NEG_INF, score)
    return sel, picks


def _member_by_rank(psum, tpos_row, n_slc, n_pick):
    nq, ncmp = psum.shape
    nb = -(-n_slc // 8) * 8
    d = _iota2((nb, ncmp), 1) - (SEL_BLOCK // CMP_STRIDE) * _iota2((nb, ncmp), 0)
    wgt = jnp.where((d == -1) | (d == 3), 1.0, jnp.where((d >= 0) & (d <= 2), 2.0, 0.0))
    wgt = jnp.where(_iota2((nb, ncmp), 0) < n_slc, wgt, 0.0).astype(BF16)
    p_hi = psum.astype(BF16)
    p_lo = (psum - p_hi.astype(F32)).astype(BF16)
    slc = _dot_nt(wgt, p_hi) + _dot_nt(wgt, p_lo)
    blk = _iota2((nb, nq), 0)
    cur = jnp.right_shift(tpos_row, SEL_SHIFT)
    forced = (blk == 0) | (blk == cur) | (blk == cur - 1)
    score = jnp.where(forced, jnp.inf, slc)
    score = jnp.where(blk > cur, NEG_INF, score)
    ahead = jnp.zeros((nb, nq), F32)
    for i in range(n_slc):
        s_i = score[i:i + 1, :]
        ahead = ahead + jnp.where((s_i > score) | ((s_i == score) & (blk > i)), 1.0, 0.0)
    return jnp.where((ahead < n_pick) & (blk <= cur), 1.0, 0.0)


def _nsa_prompt_body(q_ref, zb_ref, gb_ref, bg_ref, ks_ref, vs_ref, kw_ref, vw_ref, kc_ref, vc_ref,
                     bc_ref, bs_ref, bw_ref, o_ref, ksp, vsp, kwp, vwp, osel, *, T):
    qi = pl.program_id(2)
    tq = Q_BLOCK
    front = T - tq
    wlen = WINDOW + tq
    n_slc = T // SEL_BLOCK

    @pl.when(qi == 0)
    def _():
        ksp[0:front, :] = jnp.zeros((front, NSA_HD), BF16)
        vsp[0:front, :] = jnp.zeros((front, NSA_HD), BF16)
        ksp[front:front + T, :] = ks_ref[...].astype(BF16)
        vsp[front:front + T, :] = vs_ref[...].astype(BF16)
        kwp[0:WINDOW, :] = jnp.zeros((WINDOW, NSA_HD), BF16)
        vwp[0:WINDOW, :] = jnp.zeros((WINDOW, NSA_HD), BF16)
        kwp[WINDOW:WINDOW + T, :] = kw_ref[...].astype(BF16)
        vwp[WINDOW:WINDOW + T, :] = vw_ref[...].astype(BF16)

    t0 = pl.multiple_of(qi * tq, tq)
    tpos = _iota2((tq, 1), 0) + t0
    q_all = q_ref[...] * (NSA_HD ** -0.5)
    qs = [q_all[:, g * NSA_HD:(g + 1) * NSA_HD].astype(BF16) for g in range(NSA_G)]

    ncmp = T // CMP_STRIDE
    vis = tpos >= _iota2((1, ncmp), 1) * CMP_STRIDE + (CMP_BLOCK - 1)
    kcb = kc_ref[...].astype(BF16)
    vcb = vc_ref[...].astype(BF16)
    psum = jnp.zeros((tq, ncmp), F32)
    o_cmp = []
    for g in range(NSA_G):
        s = jnp.where(vis, _dot_nt(qs[g], kcb) + bc_ref[g], NEG_INF)
        p, l = _softmax_rows(s)
        p = p / jnp.maximum(l, TINY)
        psum = psum + p
        o_cmp.append(_dot(p.astype(BF16), vcb))

    member_t = _member_by_rank(psum, _iota2((1, tq), 1) + t0, n_slc, min(N_SEL, n_slc)).astype(BF16)

    nb = member_t.shape[0]
    n_win = SEL_WINDOWS if T % (SEL_WINDOWS * tq) == 0 else 1
    for i in range(n_win):
        w_prev, w = T * i // n_win, T * (i + 1) // n_win

        @pl.when((qi >= w_prev // tq) & (qi < w // tq))
        def _(w=w):
            off = T - w
            col_blk = (jnp.right_shift(_iota2((nb, w), 1) + off, SEL_SHIFT)
                       + (qi * (tq // SEL_BLOCK) + (tq - T) // SEL_BLOCK))
            expand = (col_blk == _iota2((nb, w), 0)).astype(BF16)
            kpos = _iota2((1, w), 1) + (t0 + tq - w)
            allowed = (_dot_tn(member_t, expand) > 0.5) & (kpos <= tpos)
            mask_s = jnp.where(allowed, 0.0, NEG_INF)
            k_s = ksp[pl.ds(t0 + off, w), :]
            v_s = vsp[pl.ds(t0 + off, w), :]
            for g in range(NSA_G):
                p, l = _softmax_rows(_dot_nt(qs[g], k_s) + bs_ref[g, :, off:] + mask_s)
                osel[g] = _dot(p.astype(BF16), v_s) / jnp.maximum(l, TINY)

    dist = WINDOW + _iota2((tq, wlen), 0) - _iota2((tq, wlen), 1)
    in_win = (dist >= 0) & (dist < WINDOW) & (_iota2((1, wlen), 1) + (t0 - WINDOW) >= 0)
    mask_w = jnp.where(in_win, 0.0, NEG_INF)
    k_w = kwp[pl.ds(t0, wlen), :]
    v_w = vwp[pl.ds(t0, wlen), :]
    gate = jax.nn.sigmoid(gb_ref[...] + bg_ref[...])
    zb = _silu(zb_ref[...])
    for g in range(NSA_G):
        p, l = _softmax_rows(_dot_nt(qs[g], k_w) + bw_ref[g] + mask_w)
        o_win = _dot(p.astype(BF16), v_w) / jnp.maximum(l, TINY)
        mix = (gate[:, g:g + 1] * o_cmp[g] + gate[:, NSA_G + g:NSA_G + g + 1] * osel[g]
               + gate[:, 2 * NSA_G + g:2 * NSA_G + g + 1] * o_win)
        sl = slice(g * NSA_HD, (g + 1) * NSA_HD)
        o_ref[:, sl] = (mix * zb[:, sl]).astype(o_ref.dtype)


def _nsa_prompt_call(ya, yb, kv16, kcmp, vcmp, bg_r, bias_c, bias_s, bias_w, *, B, T):
    nq = T // Q_BLOCK
    gw = NSA_G * NSA_HD
    wlen = WINDOW + Q_BLOCK
    kv_spec = pl.BlockSpec((T, NSA_HD), lambda b, h, i: (b, h))
    cmp_spec = pl.BlockSpec((None, None, T // CMP_STRIDE, NSA_HD), lambda b, h, i: (b, h, 0, 0))
    return pl.pallas_call(
        functools.partial(_nsa_prompt_body, T=T),
        grid=(B, NSA_KVH, nq),
        in_specs=[pl.BlockSpec((Q_BLOCK, gw), lambda b, h, i: (b * nq + i, EVEN_A["qb"] // gw + h)),
                  pl.BlockSpec((Q_BLOCK, gw), lambda b, h, i: (b * nq + i, EVEN_B["zb"] // gw + h)),
                  pl.BlockSpec((Q_BLOCK, LANES), lambda b, h, i: (b * nq + i, EVEN_B["gb"] // LANES + h)),
                  pl.BlockSpec((None, 1, LANES), lambda b, h, i: (h, 0, 0)),
                  kv_spec, kv_spec, kv_spec, kv_spec, cmp_spec, cmp_spec,
                  pl.BlockSpec((None, NSA_G, Q_BLOCK, T // CMP_STRIDE), lambda b, h, i: (h, 0, i, 0)),
                  pl.BlockSpec((None, NSA_G, Q_BLOCK, T), lambda b, h, i: (h, 0, 0, 0)),
                  pl.BlockSpec((None, NSA_G, Q_BLOCK, wlen), lambda b, h, i: (h, 0, 0, 0))],
        out_specs=pl.BlockSpec((Q_BLOCK, gw), lambda b, h, i: (b * nq + i, h)),
        out_shape=jax.ShapeDtypeStruct((B * T, NSA_W), BF16),
        scratch_shapes=[pltpu.VMEM((2 * T - Q_BLOCK, NSA_HD), BF16), pltpu.VMEM((2 * T - Q_BLOCK, NSA_HD), BF16),
                        pltpu.VMEM((WINDOW + T, NSA_HD), BF16), pltpu.VMEM((WINDOW + T, NSA_HD), BF16),
                        pltpu.VMEM((NSA_G, Q_BLOCK, NSA_HD), F32)],
        compiler_params=_params(("arbitrary", "arbitrary", "arbitrary")),
        name="nsa_prompt",
    )(ya, yb, yb, bg_r, *kv16, kcmp, vcmp, bias_c, bias_s, bias_w)


CMP_PAGES = 16
CHUNKS_PER_PAGE = PAGE_SIZE // CMP_STRIDE
PAGE_ROWS = PAGE_SIZE * NSA_KVH


def _pool_rows(pool):
    return pool.reshape(pool.shape[0] * PAGE_ROWS, NSA_HD)


def _cmp_pages_body(pt_ref, *refs):
    del pt_ref
    pages = refs[:CMP_PAGES]
    w_ref, pe_ref, o_ref = refs[CMP_PAGES:]
    rows = CMP_PAGES * CHUNKS_PER_PAGE
    per_head = [jnp.concatenate(
        [jnp.concatenate([pg[pl.ds(NSA_KVH * s + h, CHUNKS_PER_PAGE, stride=CMP_STRIDE * NSA_KVH), :]
                          for s in range(CMP_STRIDE)], axis=1) for pg in pages], axis=0) for h in range(NSA_KVH)]
    w = w_ref[...]
    r = _dot(jnp.concatenate(per_head, axis=0).astype(BF16), w)
    pc = _dot(pe_ref[...], w)
    r = r + jnp.concatenate([pc[0:1, :NSA_HD], pc[1:2, NSA_HD:]], axis=1)
    for h in range(NSA_KVH):
        o_ref[h] = r[h * rows:(h + 1) * rows]


def _cmp_pages_call(pool, page_table, w1, pe, *, B):
    n_pages = page_table.shape[1]
    rows = CMP_PAGES * CHUNKS_PER_PAGE
    view = _pool_rows(pool)
    w = w1.reshape(2, CMP_STRIDE, NSA_HD, NSA_HD).transpose(1, 2, 0, 3).reshape(CMP_STRIDE * NSA_HD, 2 * NSA_HD)
    pe_rows = jnp.pad(pe.reshape(2, CMP_STRIDE * NSA_HD), ((0, 6), (0, 0))).astype(BF16)

    def page_spec(i):
        return pl.BlockSpec((PAGE_ROWS, NSA_HD), lambda b, s, pt: (pt[b * n_pages + s * CMP_PAGES + i], 0))

    grid_spec = pltpu.PrefetchScalarGridSpec(
        num_scalar_prefetch=1,
        grid=(B, n_pages // CMP_PAGES),
        in_specs=[page_spec(i) for i in range(CMP_PAGES)]
        + [pl.BlockSpec((CMP_STRIDE * NSA_HD, 2 * NSA_HD), lambda b, s, pt: (0, 0)),
           pl.BlockSpec((8, CMP_STRIDE * NSA_HD), lambda b, s, pt: (0, 0))],
        out_specs=pl.BlockSpec((None, NSA_KVH, rows, 2 * NSA_HD), lambda b, s, pt: (b, 0, s, 0)),
    )
    return pl.pallas_call(
        _cmp_pages_body,
        grid_spec=grid_spec,
        out_shape=jax.ShapeDtypeStruct((B, NSA_KVH, n_pages * CHUNKS_PER_PAGE, 2 * NSA_HD), F32),
        compiler_params=_params(("arbitrary", "arbitrary")),
        name="nsa_cmp_pages",
    )(page_table.reshape(-1), *([view] * CMP_PAGES), w.astype(BF16), pe_rows)


SEL_WINDOWS = 4
SLC_LANES = 384


def _sample_q_rows(q_ref):
    q = q_ref[...] * (NSA_HD ** -0.5)
    return jnp.concatenate([q[:, g * NSA_HD:(g + 1) * NSA_HD] for g in range(NSA_G)], axis=0).astype(BF16)


def _nsa_sample_main_body(abk_ref, abv_ref, b1_ref, w2_ref, q_ref, wk_ref, wv_ref, kn_ref, vn_ref, bc_ref, bw_ref,
                          ocmp_ref, owin_ref, idx_ref, *, T, n_slc):
    tp = SAMPLE_PAD_T
    rows = NSA_G * tp
    ncmp = abk_ref.shape[0]

    def compressed(ab_ref, t):
        ab = ab_ref[...]
        h = ab[:, :NSA_HD] + pltpu.roll(ab[:, NSA_HD:], ncmp - 1, 0) + b1_ref[t]
        return _dot(_gelu_tanh(h).astype(BF16), w2_ref[t]).astype(BF16)

    kc, vc = compressed(abk_ref, 0), compressed(abv_ref, 1)
    q = _sample_q_rows(q_ref)
    step = jnp.bitwise_and(_iota2((rows, 1), 0), tp - 1)
    tpos = PAST_LEN + step
    vis = tpos >= _iota2((1, ncmp), 1) * CMP_STRIDE + (CMP_BLOCK - 1)
    p, l = _softmax_rows(jnp.where(vis, _dot_nt(q, kc) + bc_ref[...], NEG_INF))
    p = p / jnp.maximum(l, TINY)
    ocmp_ref[...] = _dot(p.astype(BF16), vc)
    psum = p[0:tp]
    for g in range(1, NSA_G):
        psum = psum + p[g * tp:(g + 1) * tp]
    cur = jnp.right_shift(PAST_LEN + _iota2((tp, 1), 0), SEL_SHIFT)
    _, picks = _top_blocks(_slc_scores(psum, SLC_LANES, n_slc), cur, N_SEL)
    idx_ref[...] = picks.astype(jnp.int32)

    wb = wk_ref.shape[0] // NSA_KVH
    wlen = bw_ref.shape[1]
    fill = jnp.zeros((wlen - wb - tp, NSA_HD), BF16)
    head = pl.program_id(1)
    k_all = jnp.concatenate([wk_ref[pl.ds(head, wb, stride=NSA_KVH), :].astype(BF16), kn_ref[...], fill], axis=0)
    v_all = jnp.concatenate([wv_ref[pl.ds(head, wb, stride=NSA_KVH), :].astype(BF16), vn_ref[...], fill], axis=0)
    col = _iota2((1, wlen), 1)
    dist = tpos - (PAST_LEN - wb + col)
    in_win = (dist >= 0) & (dist < WINDOW) & (col < wb + T)
    pw, lw = _softmax_rows(jnp.where(in_win, _dot_nt(q, k_all) + bw_ref[...], NEG_INF))
    owin_ref[...] = _dot(pw.astype(BF16), v_all) / jnp.maximum(lw, TINY)


def _nsa_sample_main_call(ya, kw16, vw16, abk, abv, b1, w2, wk, wv, bias_c, bias_w, *, B, T):
    tp = SAMPLE_PAD_T
    rows = NSA_G * tp
    gw = NSA_G * NSA_HD
    ncmp = abk.shape[2]
    wb = wk.shape[1]
    wlen = bias_w.shape[-1]
    n_slc = -(-(PAST_LEN + T) // SEL_BLOCK)
    assert n_slc <= SLC_LANES and T <= tp
    ab_spec = pl.BlockSpec((None, None, ncmp, 2 * NSA_HD), lambda b, h: (b, h, 0, 0))
    win_spec = pl.BlockSpec((wb * NSA_KVH, NSA_HD), lambda b, h: (b, 0))
    o_spec = pl.BlockSpec((None, None, rows, NSA_HD), lambda b, h: (b, h, 0, 0))
    return pl.pallas_call(
        functools.partial(_nsa_sample_main_body, T=T, n_slc=n_slc),
        grid=(B, NSA_KVH),
        in_specs=[ab_spec, ab_spec,
                  pl.BlockSpec((2, 1, NSA_HD), lambda b, h: (0, 0, 0)),
                  pl.BlockSpec((2, NSA_HD, NSA_HD), lambda b, h: (0, 0, 0)),
                  pl.BlockSpec((tp, gw), lambda b, h: (b, EVEN_A["qb"] // gw + h)),
                  win_spec, win_spec,
                  pl.BlockSpec((tp, NSA_HD), lambda b, h: (b, h)),
                  pl.BlockSpec((tp, NSA_HD), lambda b, h: (b, h)),
                  pl.BlockSpec((None, rows, ncmp), lambda b, h: (h, 0, 0)),
                  pl.BlockSpec((None, rows, wlen), lambda b, h: (h, 0, 0))],
        out_specs=[o_spec, o_spec, pl.BlockSpec((None, None, tp, LANES), lambda b, h: (b, h, 0, 0))],
        out_shape=[jax.ShapeDtypeStruct((B, NSA_KVH, rows, NSA_HD), F32),
                   jax.ShapeDtypeStruct((B, NSA_KVH, rows, NSA_HD), F32),
                   jax.ShapeDtypeStruct((B, NSA_KVH, tp, LANES), jnp.int32)],
        compiler_params=_params(("parallel", "parallel")),
        name="nsa_sample_main",
    )(abk, abv, b1.reshape(2, 1, NSA_HD), w2.astype(BF16), ya,
      wk.reshape(B * wb * NSA_KVH, NSA_HD), wv.reshape(B * wb * NSA_KVH, NSA_HD), kw16, vw16, bias_c, bias_w)


NEAR_BLOCKS = 3


def _nsa_sample_sel_body(idx_ref, pt_ref, q_ref, kn_ref, vn_ref, tbl_ref, ocmp_ref, owin_ref, gb_ref, bg_ref, zb_ref,
                         *refs, T):
    del pt_ref
    k_blocks = refs[:N_SEL]
    v_blocks = refs[N_SEL:2 * N_SEL]
    o_ref, osel = refs[2 * N_SEL:]
    tp = SAMPLE_PAD_T
    rows = NSA_G * tp
    b, h, t = pl.program_id(0), pl.program_id(1), pl.program_id(2)
    base = ((b * NSA_KVH + h) * T + t) * N_SEL
    first_new = PAST_LEN // SEL_BLOCK
    cur = jnp.right_shift(PAST_LEN + t, SEL_SHIFT)
    q = _sample_q_rows(q_ref)
    pad = jnp.zeros((SEL_BLOCK - tp, NSA_HD), BF16)
    k_new = jnp.concatenate([kn_ref[...], pad], axis=0)
    v_new = jnp.concatenate([vn_ref[...], pad], axis=0)
    lane = _iota2((1, LANES), 1)
    low = lane < SEL_BLOCK
    within = jnp.bitwise_and(lane, SEL_BLOCK - 1)
    ks, vs, bias, kpos = [], [], [], []
    for i in range(0, N_SEL, 2):
        pair_bias, pair_pos = [], []
        for j in (i, i + 1):
            blk = idx_ref[base + j]
            is_new = blk >= first_new
            ks.append(jnp.where(is_new, k_new, k_blocks[j][pl.ds(h, SEL_BLOCK, stride=NSA_KVH), :].astype(BF16)))
            vs.append(jnp.where(is_new, v_new, v_blocks[j][pl.ds(h, SEL_BLOCK, stride=NSA_KVH), :].astype(BF16)))
            pair_bias.append(tbl_ref[jnp.clip(blk - (first_new - NEAR_BLOCKS), 0, NEAR_BLOCKS)])
            pair_pos.append(jnp.where(blk <= cur, blk * SEL_BLOCK, PAST_LEN + SEL_BLOCK * LANES) + within)
        bias.append(jnp.where(low, pair_bias[0], pair_bias[1]))
        kpos.append(jnp.where(low, pair_pos[0], pair_pos[1]))
    k_all = jnp.concatenate(ks, axis=0)
    v_all = jnp.concatenate(vs, axis=0)
    step = jnp.bitwise_and(_iota2((rows, 1), 0), tp - 1)
    ok = jnp.concatenate(kpos, axis=1) <= PAST_LEN + step
    p, l = _softmax_rows(jnp.where(ok, _dot_nt(q, k_all) + jnp.concatenate(bias, axis=1), NEG_INF))
    o = _dot(p.astype(BF16), v_all) / jnp.maximum(l, TINY)

    @pl.when(t == 0)
    def _():
        osel[...] = jnp.zeros_like(osel)

    osel[...] = jnp.where(step == t, o, osel[...])

    @pl.when(t == T - 1)
    def _():
        gate = jax.nn.sigmoid(gb_ref[...] + bg_ref[...])
        zb = _silu(zb_ref[...])
        for g in range(NSA_G):
            r = slice(g * tp, (g + 1) * tp)
            mix = (gate[:, g:g + 1] * ocmp_ref[r, :] + gate[:, NSA_G + g:NSA_G + g + 1] * osel[r, :]
                   + gate[:, 2 * NSA_G + g:2 * NSA_G + g + 1] * owin_ref[r, :])
            sl = slice(g * NSA_HD, (g + 1) * NSA_HD)
            o_ref[:, sl] = (mix * zb[:, sl]).astype(o_ref.dtype)


def _nsa_sample_sel_call(ya, yb, ks16, vs16, idx, page_table, pool_k, pool_v, tbl, o_cmp, o_win, bg_r, *, B, T):
    tp = SAMPLE_PAD_T
    rows = NSA_G * tp
    gw = NSA_G * NSA_HD
    n_pages = page_table.shape[1]
    halves = PAGE_SIZE // SEL_BLOCK
    idx_flat = idx[:, :, :T, :N_SEL].reshape(-1)
    view_k, view_v = _pool_rows(pool_k), _pool_rows(pool_v)

    def blk_spec(j):
        def index(b, h, t, idx_s, pt_s):
            blk = idx_s[((b * NSA_KVH + h) * T + t) * N_SEL + j]
            page = pt_s[b * n_pages + jnp.minimum(blk // halves, n_pages - 1)]
            return (page * halves + blk % halves, 0)
        return pl.BlockSpec((SEL_BLOCK * NSA_KVH, NSA_HD), index)

    o_spec = pl.BlockSpec((None, None, rows, NSA_HD), lambda b, h, t, *_: (b, h, 0, 0))
    grid_spec = pltpu.PrefetchScalarGridSpec(
        num_scalar_prefetch=2,
        grid=(B, NSA_KVH, T),
        in_specs=[pl.BlockSpec((tp, gw), lambda b, h, t, *_: (b, EVEN_A["qb"] // gw + h)),
                  pl.BlockSpec((tp, NSA_HD), lambda b, h, t, *_: (b, h)),
                  pl.BlockSpec((tp, NSA_HD), lambda b, h, t, *_: (b, h)),
                  pl.BlockSpec((None, NEAR_BLOCKS + 1, rows, LANES), lambda b, h, t, *_: (h, 0, 0, 0)),
                  o_spec, o_spec,
                  pl.BlockSpec((tp, LANES), lambda b, h, t, *_: (b, EVEN_B["gb"] // LANES + h)),
                  pl.BlockSpec((None, 1, LANES), lambda b, h, t, *_: (h, 0, 0)),
                  pl.BlockSpec((tp, gw), lambda b, h, t, *_: (b, EVEN_B["zb"] // gw + h))]
        + [blk_spec(j) for j in range(N_SEL)] * 2,
        out_specs=pl.BlockSpec((tp, gw), lambda b, h, t, *_: (b, h)),
        scratch_shapes=[pltpu.VMEM((rows, NSA_HD), F32)],
    )
    return pl.pallas_call(
        functools.partial(_nsa_sample_sel_body, T=T),
        grid_spec=grid_spec,
        out_shape=jax.ShapeDtypeStruct((B * tp, NSA_W), BF16),
        compiler_params=_params(("arbitrary", "arbitrary", "arbitrary")),
        name="nsa_sample_sel",
    )(idx_flat, page_table.reshape(-1), ya, ks16, vs16, tbl, o_cmp, o_win, yb, bg_r, yb,
      *([view_k] * N_SEL), *([view_v] * N_SEL))


def _sample_bias_tables(rel_bias, T, wb):
    tp = SAMPLE_PAD_T
    ncmp = PAST_LEN // CMP_STRIDE
    wlen = -(-(wb + tp) // LANES) * LANES
    first = PAST_LEN // SEL_BLOCK - NEAR_BLOCKS
    assert PAST_LEN - ((first + 1) * SEL_BLOCK - 1) >= REL_MAX_DIST
    lo, hi = -wlen, PAST_LEN + tp
    rev = _bias_line(rel_bias, lo, hi, descending=True)

    def rows(tbl):
        return tbl.reshape(NSA_KVH, NSA_G * tp, tbl.shape[-1])

    t_c = _toeplitz(rev, hi - 1 - (PAST_LEN - (CMP_BLOCK - 1)), tp, CMP_STRIDE * ncmp)[:, :, ::CMP_STRIDE]
    t_w = _toeplitz(rev, hi - 1 - wb, tp, wlen)
    far = jnp.broadcast_to(rev[:, hi - 1 - REL_MAX_DIST][:, None, None], (NSA_HEADS, tp, LANES))
    near = []
    for k in range(1, NEAR_BLOCKS + 1):
        half = _toeplitz(rev, hi - 1 - (PAST_LEN - (first + k) * SEL_BLOCK), tp, SEL_BLOCK)
        near.append(jnp.concatenate([half, half], axis=-1))
    t_s = jnp.stack([far] + near, axis=1).reshape(NSA_KVH, NSA_G, NEAR_BLOCKS + 1, tp, LANES)
    t_s = t_s.transpose(0, 2, 1, 3, 4).reshape(NSA_KVH, NEAR_BLOCKS + 1, NSA_G * tp, LANES)
    return rows(t_c), rows(t_w), t_s


def _tail_even(w):
    g0 = EVEN_KV_OFF + 6 * NSA_KV_W
    zb0 = g0 + 3 * NSA_HEADS
    cols = [w[:, zb0:zb0 + NSA_W + MEM_W]]
    for h in range(NSA_KVH):
        for j in range(3):
            cols.append(w[:, g0 + j * NSA_HEADS + h * NSA_G:g0 + j * NSA_HEADS + (h + 1) * NSA_G])
        cols.append(jnp.zeros((w.shape[0], LANES - 3 * NSA_G), w.dtype))
    return jnp.concatenate(cols, axis=1).astype(BF16)


def _tail_odd(w):
    ig0 = ODD_A_N
    fg0 = ig0 + ML_HEADS
    z0 = fg0 + ML_HEADS
    cols = [w[:, z0:z0 + ML_V_W + MEM_W]]
    for hg in range(ML_HEADS // ML_HB):
        cols.append(w[:, ig0 + hg * ML_HB:ig0 + (hg + 1) * ML_HB])
        cols.append(w[:, fg0 + hg * ML_HB:fg0 + (hg + 1) * ML_HB])
        cols.append(jnp.zeros((w.shape[0], LANES - 2 * ML_HB), w.dtype))
    return jnp.concatenate(cols, axis=1).astype(BF16)


def _gate_bias_even(b_gate):
    g = b_gate.reshape(3, NSA_KVH, NSA_G).transpose(1, 0, 2).reshape(NSA_KVH, 1, 3 * NSA_G)
    return jnp.pad(g, ((0, 0), (0, 0), (0, LANES - 3 * NSA_G)))


def _gate_bias_odd(b_if):
    g = b_if.reshape(2, ML_HEADS // ML_HB, ML_HB).transpose(1, 0, 2).reshape(ML_HEADS // ML_HB, 1, 2 * ML_HB)
    return jnp.pad(g, ((0, 0), (0, 0), (0, LANES - 2 * ML_HB)))


def _rel_bucket(dist):
    n = np.maximum(dist, 0)
    exact = REL_BUCKETS // 2
    nf = np.maximum(n, 1).astype(np.float32)
    large = exact + (np.log(nf / exact) / math.log(REL_MAX_DIST / exact) * (REL_BUCKETS - exact)).astype(np.int32)
    return np.where(n < exact, n, np.minimum(large, REL_BUCKETS - 1))


def _bias_line(rel_bias, lo, hi, descending=False):
    dist = np.arange(hi - 1, lo - 1, -1) if descending else np.arange(lo, hi)
    buckets = _rel_bucket(dist)
    edges = np.flatnonzero(np.diff(buckets)) + 1
    starts = np.concatenate([[0], edges])
    ends = np.concatenate([edges, [hi - lo]])
    bias_t = rel_bias.T.astype(F32)
    runs = [jnp.broadcast_to(bias_t[:, int(buckets[s])][:, None], (NSA_HEADS, int(e - s))) for s, e in zip(starts, ends)]
    return jnp.concatenate(runs, axis=1)


def _skew_rows(v, rows, step, cols):
    n = v.shape[1]
    reps = -(-rows * (n + step) // n)
    return jnp.tile(v, (1, reps))[:, :rows * (n + step)].reshape(v.shape[0], rows, n + step)[:, :, :cols]


def _toeplitz(rev, start, rows, cols):
    seg = rev[:, start - (rows - 1):start + cols]
    return _skew_rows(jnp.roll(seg, -(rows - 1), axis=1), rows, -1, cols)


def _prompt_bias_tables(rel_bias, T):
    ncmp = T // CMP_STRIDE
    wlen = WINDOW + Q_BLOCK
    lo, hi = -(CMP_STRIDE * ncmp + CMP_BLOCK), T
    line = _bias_line(rel_bias, lo, hi)
    rev = _bias_line(rel_bias, lo, hi, descending=True)

    def split(tbl):
        return tbl.reshape((NSA_KVH, NSA_G) + tbl.shape[1:])

    back = CMP_STRIDE * (ncmp - 1)
    first = -(back + CMP_BLOCK - 1) - lo
    seg = line[:, first:first + T + back]
    t_c = _skew_rows(jnp.roll(seg, -back, axis=1), ncmp, -CMP_STRIDE, T).swapaxes(1, 2)
    t_s = _toeplitz(rev, hi - 1 - (T - Q_BLOCK), Q_BLOCK, T)
    t_w = _toeplitz(rev, hi - 1 - WINDOW, Q_BLOCK, wlen)
    return split(t_c), split(t_s), split(t_w)


def _nsa_sample(ya, yb, kv16, page_table, pk_cmp, pv_cmp, pk_sel, pv_sel, wk, wv, bg_r, w1, b1, w2, pe, rel_bias,
                *, B, T):
    assert (PAST_LEN + T) // CMP_STRIDE == PAST_LEN // CMP_STRIDE
    abk = _cmp_pages_call(pk_cmp, page_table, w1[0], pe[0], B=B)
    abv = _cmp_pages_call(pv_cmp, page_table, w1[1], pe[1], B=B)
    bias_c, bias_w, tbl = _sample_bias_tables(rel_bias, T, wk.shape[1])
    o_cmp, o_win, idx = _nsa_sample_main_call(ya, kv16[4], kv16[5], abk, abv, b1, w2, wk, wv, bias_c, bias_w, B=B, T=T)
    return _nsa_sample_sel_call(ya, yb, kv16[2], kv16[3], idx, page_table, pk_sel, pv_sel, tbl, o_cmp, o_win, bg_r,
                                B=B, T=T)


def _kv_project(x, w_in):
    outs = [_matmul_heads(x, w_in, first=EVEN_KV_OFF + j * NSA_KV_W) for j in range(6)]
    return [o[0] for o in outs], [o[1] for o in outs]


def _even_prompt(hp2d, npre, mk16, mv16, w_in, w_b, bg_r, w1, b1, w2, pe, lb, g_norm, w_out, rel_bias, *, B, T):
    ya, yb = _matmul(npre, w_in, tn=W_TILE_N, cols=(0, EVEN_A_N)), _matmul(npre, w_b)
    kv32, kv16 = _kv_project(npre, w_in)
    oa, s_new = _hgrn_call(ya, jnp.zeros((B, HG_HEADS, HG_DK, HG_DV), F32), lb, g_norm, B=B, T=T, L=CHUNK, valid=CHUNK)
    kcmp = _compress_call(kv16[0], w1[0], b1[0], w2[0], pe[0], B=B, T=T)
    vcmp = _compress_call(kv16[1], w1[1], b1[1], w2[1], pe[1], B=B, T=T)
    ob = _nsa_prompt_call(ya, yb, kv16[2:], kcmp, vcmp, bg_r, *_prompt_bias_tables(rel_bias, T), B=B, T=T)
    om = _mem_call(yb, EVEN_B["qm"], mk16, mv16, B=B, T=T)
    h_new = _outproj([oa, ob, om], w_out, hp2d)
    wb = min(WINDOW, T)
    rows = [r.reshape(B, T, NSA_KVH, NSA_HD) for r in kv32]
    return h_new, (rows[0], rows[1], rows[2], rows[3], rows[4][:, -wb:], rows[5][:, -wb:], s_new)


def _even_sample(hs2d, nsam, mk_s, mv_s, page_table, pk_cmp, pv_cmp, pk_sel, pv_sel, wk, wv, s0,
                 w_in, w_b, bg_r, w1, b1, w2, pe, lb, g_norm, w_out, rel_bias, *, B, T):
    tp = SAMPLE_PAD_T
    ya, yb = _matmul(nsam, w_in, tn=W_TILE_N, cols=(0, EVEN_A_N)), _matmul(nsam, w_b)
    kv32, kv16 = _kv_project(nsam, w_in)
    oa, s_new = _hgrn_call(ya, s0, lb, g_norm, B=B, T=tp, L=tp, valid=T)
    ob = _nsa_sample(ya, yb, kv16, page_table, pk_cmp, pv_cmp, pk_sel, pv_sel, wk, wv, bg_r, w1, b1, w2, pe, rel_bias,
                     B=B, T=T)
    om = _mem_call(yb, EVEN_B["qm"], mk_s.reshape(B * N_MEM, MEM_W), mv_s.reshape(B * N_MEM, MEM_W), B=B, T=tp)
    rows = [r.reshape(B, tp, NSA_KVH, NSA_HD)[:, :T] for r in kv32]
    wb = wk.shape[1]
    win_k = jnp.concatenate([wk, rows[4]], axis=1)[:, -wb:]
    win_v = jnp.concatenate([wv, rows[5]], axis=1)[:, -wb:]
    return _outproj([oa, ob, om], w_out, hs2d), (rows[0], rows[1], rows[2], rows[3], win_k, win_v, s_new)


def _odd_mix(h2d, hn, k2d, v2d, c0, n0, m0, w_in, w_b, bif_r, g_norm, w_out, *, B, T, L, valid):
    ya, yb = _matmul(hn, w_in, tn=W_TILE_N, cols=(0, ODD_A_N)), _matmul(hn, w_b)
    h, c_new, n_new, m_new = _mlstm_call(ya, yb, c0, n0, m0, bif_r, g_norm, B=B, T=T, L=L, valid=valid)
    om = _mem_call(yb, ODD_B["qm"], k2d, v2d, B=B, T=T)
    return _outproj([h, om], w_out, h2d), (c_new, n_new, m_new)


def _stack(lst, i):
    return jnp.stack([t[i] for t in lst])


def kernel(x_prompt, x_sample, cache_mem_k, cache_mem_v, cache_cmp_k, cache_cmp_v, cache_sel_k, cache_sel_v,
           cache_win_k, cache_win_v, state_hgrn, state_mlstm_c, state_mlstm_n, state_mlstm_m, page_table,
           mem_prompt, norm_w, mem_norm_w, final_norm_w, rel_bias, w_mem_kv, w_in_even, b_nsa_gate,
           w_cmp1, b_cmp1, w_cmp2, pe_cmp, hgrn_lb_logits, hgrn_norm_w, w_out_even, w_in_odd, b_mlstm_if,
           mlstm_norm_w, w_out_odd):
    bp, tp = x_prompt.shape[:2]
    bs, ts = x_sample.shape[:2]
    tsp = SAMPLE_PAD_T
    lbs = jnp.cumsum(jax.nn.softmax(hgrn_lb_logits.astype(F32), axis=0), axis=0)
    hp = x_prompt.reshape(bp * tp, D_MODEL)
    hs = jnp.pad(x_sample, ((0, 0), (0, tsp - ts), (0, 0))).reshape(bs * tsp, D_MODEL)
    mem2d = mem_prompt.reshape(bp * N_MEM, D_MODEL)
    mem_new, even_p, even_s, odd_p, odd_s = [], [], [], [], []
    for l in range(DEPTH):
        npre = _rmsnorm_rows(hp, norm_w[l], BF16)
        nsam = _rmsnorm_rows(hs, norm_w[l], BF16)
        nmem = _rmsnorm_rows(mem2d, mem_norm_w[l], BF16)
        mk32, mk16 = _matmul_heads(nmem, w_mem_kv[l], first=0)
        mv32, mv16 = _matmul_heads(nmem, w_mem_kv[l], first=MEM_W)
        mem_new.append((mk32.reshape(bp, N_MEM, MEM_HEADS, MEM_HD), mv32.reshape(bp, N_MEM, MEM_HEADS, MEM_HD)))
        mk_s, mv_s = cache_mem_k[l], cache_mem_v[l]
        if l % 2 == 0:
            e = l // 2
            w_in, w_b = w_in_even[e], _tail_even(w_in_even[e])
            w_out = w_out_even[e].astype(BF16)
            bg_r = _gate_bias_even(b_nsa_gate[e])
            cmpw = (w_cmp1[e].reshape(2, CMP_BLOCK, NSA_HD, NSA_HD), b_cmp1[e], w_cmp2[e], pe_cmp[e])
            hp, st_p = _even_prompt(hp, npre, mk16, mv16, w_in, w_b, bg_r, *cmpw, lbs[l], hgrn_norm_w[e], w_out,
                                    rel_bias, B=bp, T=tp)
            hs, st_s = _even_sample(hs, nsam, mk_s, mv_s, page_table, cache_cmp_k[e], cache_cmp_v[e], cache_sel_k[e],
                                    cache_sel_v[e], cache_win_k[e], cache_win_v[e], state_hgrn[e], w_in, w_b, bg_r,
                                    *cmpw, lbs[l], hgrn_norm_w[e], w_out, rel_bias, B=bs, T=ts)
            even_p.append(st_p)
            even_s.append(st_s)
        else:
            o = l // 2
            w_in, w_b = w_in_odd[o], _tail_odd(w_in_odd[o])
            w_out = w_out_odd[o].astype(BF16)
            bif_r = _gate_bias_odd(b_mlstm_if[o])
            hp, st_p = _odd_mix(hp, npre, mk16, mv16, jnp.zeros((bp, ML_HEADS, ML_DV, ML_DK), F32),
                                jnp.zeros((bp, ML_HEADS, ML_DK), F32), jnp.zeros((bp, ML_HEADS), F32),
                                w_in, w_b, bif_r, mlstm_norm_w[o], w_out, B=bp, T=tp, L=ML_CHUNK, valid=ML_CHUNK)
            hs, st_s = _odd_mix(hs, nsam, mk_s.reshape(bs * N_MEM, MEM_W), mv_s.reshape(bs * N_MEM, MEM_W),
                                state_mlstm_c[o], state_mlstm_n[o], state_mlstm_m[o],
                                w_in, w_b, bif_r, mlstm_norm_w[o], w_out, B=bs, T=tsp, L=tsp, valid=ts)
            odd_p.append(st_p)
            odd_s.append(st_s)
    y_prompt = _rmsnorm_rows(hp, final_norm_w, F32).reshape(bp, tp, D_MODEL)
    y_sample = _rmsnorm_rows(hs, final_norm_w, F32).reshape(bs, tsp, D_MODEL)[:, :ts]
    return (y_prompt, y_sample,
            _stack(mem_new, 0), _stack(mem_new, 1),
            _stack(even_p, 0), _stack(even_p, 1), _stack(even_p, 2), _stack(even_p, 3),
            _stack(even_p, 4), _stack(even_p, 5), _stack(even_p, 6),
            _stack(odd_p, 0), _stack(odd_p, 1), _stack(odd_p, 2),
            _stack(even_s, 0), _stack(even_s, 1), _stack(even_s, 2), _stack(even_s, 3),
            _stack(even_s, 4), _stack(even_s, 5), _stack(even_s, 6),
            _stack(odd_s, 0), _stack(odd_s, 1), _stack(odd_s, 2))
```

```python
import functools
import math

import jax
import jax.numpy as jnp
import numpy as np
from jax import lax
from jax.experimental import pallas as pl
from jax.experimental.pallas import tpu as pltpu

D_MODEL = 4096
DEPTH = 2
PAST_LEN = 16384
PAGE_SIZE = 128
N_MEM = 256
EPS = 1e-6
CHUNK = 64

HG_DK = 128
HG_DV = 128
HG_HEADS = D_MODEL // 2 // HG_DV
HG_W = HG_HEADS * HG_DV

NSA_HD = 128
NSA_HEADS = D_MODEL // 2 // NSA_HD
NSA_KVH = 4
NSA_G = NSA_HEADS // NSA_KVH
NSA_W = NSA_HEADS * NSA_HD
NSA_KV_W = NSA_KVH * NSA_HD
CMP_BLOCK = 32
CMP_STRIDE = 16
SEL_BLOCK = 64
SEL_SHIFT = SEL_BLOCK.bit_length() - 1
N_SEL = 16
WINDOW = 512
Q_BLOCK = 128

ML_HEADS = D_MODEL // 512
ML_DK = D_MODEL // 2 // ML_HEADS
ML_DV = D_MODEL // ML_HEADS
ML_QK_W = ML_HEADS * ML_DK
ML_V_W = ML_HEADS * ML_DV

MEM_HEADS = 4
MEM_HD = 128
MEM_W = MEM_HEADS * MEM_HD

REL_BUCKETS = 32
REL_MAX_DIST = 128

F32 = jnp.float32
BF16 = jnp.bfloat16
LANES = 128
NEG_INF = float("-inf")
TINY = float(np.finfo(np.float32).tiny)
EXP_CLAMP = 80.0
VMEM_LIMIT = 56 * 1024 * 1024

HG_HB = 8
ML_HB = 2
ML_CHUNK = 256
W_TILE_N = 512
HG_SUB = 16
SAMPLE_PAD_T = 16

MM_TILE_N = 1024
EVEN_A = {"qa": 0, "fa": HG_W, "ia": 2 * HG_W, "za": 3 * HG_W, "qb": 4 * HG_W}
EVEN_A_N = 4 * HG_W + NSA_W
EVEN_B = {"zb": 0, "qm": NSA_W, "gb": NSA_W + MEM_W}
EVEN_B_N = -(-(NSA_W + MEM_W + LANES) // MM_TILE_N) * MM_TILE_N
EVEN_KV_OFF = EVEN_A_N
ODD_A = {"q": 0, "k": ML_QK_W, "v": 2 * ML_QK_W, "og": 2 * ML_QK_W + ML_V_W}
ODD_A_N = 2 * ML_QK_W + 2 * ML_V_W
ODD_B = {"z": 0, "qm": ML_V_W, "gates": ML_V_W + MEM_W}
ODD_B_N = -(-(ML_V_W + MEM_W + LANES) // MM_TILE_N) * MM_TILE_N


def _dot(a, b):
    return jnp.dot(a, b, preferred_element_type=F32)


def _dot_nt(a, b):
    return lax.dot_general(a, b, (((1,), (1,)), ((), ())), preferred_element_type=F32)


def _dot_tn(a, b):
    return lax.dot_general(a, b, (((0,), (0,)), ((), ())), preferred_element_type=F32)


def _iota2(shape, dim):
    return lax.broadcasted_iota(jnp.int32, shape, dim)


def _cumsum_rows(x, tri_b):
    hi = x.astype(BF16)
    r1 = x - hi.astype(F32)
    mid = r1.astype(BF16)
    lo = (r1 - mid.astype(F32)).astype(BF16)
    return _dot(tri_b, hi) + _dot(tri_b, mid) + _dot(tri_b, lo)


def _row_to_col(row, n):
    eye = _iota2((n, n), 0) == _iota2((n, n), 1)
    return jnp.sum(jnp.where(eye, row, 0.0), axis=1, keepdims=True)


def _col_to_row(col, n):
    eye = _iota2((n, n), 0) == _iota2((n, n), 1)
    return jnp.sum(jnp.where(eye, col, 0.0), axis=0, keepdims=True)


def _lane_col(x, idx):
    return jnp.sum(jnp.where(_iota2(x.shape, 1) == idx, x, 0.0), axis=1, keepdims=True)


def _silu(x):
    return x * jax.nn.sigmoid(x)


def _params(sem):
    return pltpu.CompilerParams(dimension_semantics=sem, vmem_limit_bytes=VMEM_LIMIT)


def _rmsnorm_body(x_ref, w_ref, o_ref):
    x = x_ref[...].astype(F32)
    y = x * lax.rsqrt(jnp.mean(x * x, axis=-1, keepdims=True) + EPS)
    o_ref[...] = (y * w_ref[...].astype(F32)).astype(o_ref.dtype)


def _rmsnorm_rows(x2d, w, out_dtype, tm=256):
    m, d = x2d.shape
    tm = min(tm, m)
    return pl.pallas_call(
        _rmsnorm_body,
        grid=(m // tm,),
        in_specs=[pl.BlockSpec((tm, d), lambda i: (i, 0)), pl.BlockSpec((1, d), lambda i: (0, 0))],
        out_specs=pl.BlockSpec((tm, d), lambda i: (i, 0)),
        out_shape=jax.ShapeDtypeStruct((m, d), out_dtype),
        compiler_params=_params(("parallel",)),
        name="rmsnorm",
    )(x2d, w.reshape(1, d))


def _matmul_body(a_ref, b_ref, o_ref):
    o_ref[...] = _dot(a_ref[...], b_ref[...].astype(BF16))


def _matmul(a, b, tm=1024, tn=1024, cols=None):
    m, k = a.shape
    first, n = cols or (0, b.shape[1])
    tm, tn = min(tm, m), min(tn, n)
    assert m % tm == 0 and n % tn == 0 and first % tn == 0, (a.shape, b.shape, cols)
    j0 = first // tn
    return pl.pallas_call(
        _matmul_body,
        grid=(m // tm, n // tn),
        in_specs=[pl.BlockSpec((tm, k), lambda i, j: (i, 0)), pl.BlockSpec((k, tn), lambda i, j: (0, j0 + j))],
        out_specs=pl.BlockSpec((tm, tn), lambda i, j: (i, j)),
        out_shape=jax.ShapeDtypeStruct((m, n), F32),
        compiler_params=_params(("parallel", "parallel")),
        name="matmul",
    )(a, b)


def _tail_body(*refs, nblk, shift, main):
    o_ref = refs[nblk]
    x = jnp.concatenate([r[...] for r in refs[:nblk]], axis=1)
    rows, out_w = o_ref.shape
    gates = jnp.where(_iota2((rows, LANES), 1) < shift, x[:, :LANES], 0.0)
    fill = jnp.zeros((rows, out_w - main - LANES), F32)
    o_ref[...] = jnp.concatenate([x[:, shift:shift + main], gates, fill], axis=1).astype(o_ref.dtype)


def _tail_relayout(w, first, shift, main, out_w, tk=512):
    k, n = w.shape
    assert first % LANES == 0 and first + shift + main == n and shift < LANES
    nblk = -(-(n - first) // LANES)
    c0 = first // LANES
    return pl.pallas_call(
        functools.partial(_tail_body, nblk=nblk, shift=shift, main=main),
        grid=(k // tk,),
        in_specs=[pl.BlockSpec((tk, LANES), lambda i, j=j: (i, c0 + j)) for j in range(nblk)],
        out_specs=pl.BlockSpec((tk, out_w), lambda i: (i, 0)),
        out_shape=jax.ShapeDtypeStruct((k, out_w), BF16),
        compiler_params=_params(("parallel",)),
        name="tail_relayout",
    )(*([w] * nblk))


def _matmul_heads_body(a_ref, b_ref, o32_ref, o16_ref):
    acc = _dot(a_ref[...], b_ref[...].astype(BF16))
    for h in range(MEM_HEADS):
        o32_ref[:, h, :] = acc[:, h * LANES:(h + 1) * LANES]
    o16_ref[...] = acc.astype(BF16)


def _matmul_heads(a, b, first=0, tm=1024):
    m, k = a.shape
    n = MEM_HEADS * LANES
    tm = min(tm, m)
    assert m % tm == 0 and first % n == 0, (a.shape, b.shape, first)
    j0 = first // n
    return pl.pallas_call(
        _matmul_heads_body,
        grid=(m // tm,),
        in_specs=[pl.BlockSpec((tm, k), lambda i: (i, 0)), pl.BlockSpec((k, n), lambda i: (0, j0))],
        out_specs=[pl.BlockSpec((tm, MEM_HEADS, LANES), lambda i: (i, 0, 0)), pl.BlockSpec((tm, n), lambda i: (i, 0))],
        out_shape=[jax.ShapeDtypeStruct((m, MEM_HEADS, LANES), F32), jax.ShapeDtypeStruct((m, n), BF16)],
        compiler_params=_params(("parallel",)),
        name="matmul_heads",
    )(a, b)


def _outproj_body(*refs, widths):
    xs = refs[:len(widths)]
    w_ref, r_ref, o_ref = refs[len(widths):]
    acc = r_ref[...]
    off = 0
    for x_ref, w in zip(xs, widths):
        acc = acc + _dot(x_ref[...], w_ref[off:off + w, :])
        off += w
    o_ref[...] = acc


def _outproj(xs, w_bf16, resid, tm=1024, tn=512):
    m = resid.shape[0]
    n = w_bf16.shape[1]
    widths = tuple(x.shape[1] for x in xs)
    assert sum(widths) == w_bf16.shape[0]
    tm = min(tm, m)
    in_specs = [pl.BlockSpec((tm, w), lambda i, j: (i, 0)) for w in widths]
    in_specs += [pl.BlockSpec((w_bf16.shape[0], tn), lambda i, j: (0, j)), pl.BlockSpec((tm, tn), lambda i, j: (i, j))]
    return pl.pallas_call(
        functools.partial(_outproj_body, widths=widths),
        grid=(m // tm, n // tn),
        in_specs=in_specs,
        out_specs=pl.BlockSpec((tm, tn), lambda i, j: (i, j)),
        out_shape=jax.ShapeDtypeStruct((m, n), F32),
        compiler_params=_params(("parallel", "parallel")),
        name="outproj",
    )(*xs, w_bf16, resid)


def _hgrn_body(qa_ref, fa_ref, ia_ref, za_ref, lb_ref, gn_ref, s0_ref, o_ref, s_out, s_scr, *, L, valid):
    c = pl.program_id(2)

    @pl.when(c == 0)
    def _():
        s_scr[...] = s0_ref[...]

    lb = lb_ref[...]
    sig = jax.nn.sigmoid(fa_ref[...])
    logf = jnp.log(lb + (1.0 - lb) * sig)
    kk = (1.0 - lb) * (1.0 - sig)
    if valid < L:
        live = _iota2((L, 1), 0) < valid
        logf = jnp.where(live, logf, 0.0)
        kk = jnp.where(live, kk, 0.0)
    tri_b = (_iota2((L, L), 0) >= _iota2((L, L), 1)).astype(BF16)
    bc = _cumsum_rows(logf, tri_b)
    q = _silu(qa_ref[...])
    gate = _silu(za_ref[...])
    v = ia_ref[...]
    gn = gn_ref[...]
    nsub = L // HG_SUB
    rr = _iota2((L, nsub * L), 0)
    cc = _iota2((L, nsub * L), 1)
    keep = ((jnp.right_shift(cc, L.bit_length() - 1) == jnp.right_shift(rr, HG_SUB.bit_length() - 1))
            & (jnp.bitwise_and(cc, L - 1) <= rr))
    for j in range(HG_HB):
        sl = slice(j * HG_DK, (j + 1) * HG_DK)
        bj, qj, kj = bc[:, sl], q[:, sl], kk[:, sl]
        vb = v[:, sl].astype(BF16)
        s_prev = s_scr[j]
        inter = _dot((qj * jnp.exp(bj)).astype(BF16), s_prev.astype(BF16))
        mids = [bj[i * HG_SUB + HG_SUB // 2:i * HG_SUB + HG_SUB // 2 + 1, :] for i in range(nsub)]
        mid_rows = jnp.concatenate([jnp.broadcast_to(m, (HG_SUB, HG_DK)) for m in mids], axis=0)
        q_dec = qj * jnp.exp(jnp.minimum(bj - mid_rows, EXP_CLAMP))
        k_dec = jnp.concatenate([kj * jnp.exp(jnp.minimum(m - bj, EXP_CLAMP)) for m in mids], axis=0)
        att = jnp.where(keep, _dot_nt(q_dec.astype(BF16), k_dec.astype(BF16)), 0.0)
        o = inter + _dot(att.astype(BF16), jnp.concatenate([vb] * nsub, axis=0))
        o_n = o * lax.rsqrt(jnp.mean(o * o, axis=-1, keepdims=True) + EPS) * gn
        o_ref[:, sl] = (o_n * gate[:, sl]).astype(o_ref.dtype)
        bl = bj[L - 1:L, :]
        kd = kj * jnp.exp(bl - bj)
        s_scr[j] = _row_to_col(jnp.exp(bl), HG_DK) * s_prev + _dot_tn(kd.astype(BF16), vb)

    @pl.when(c == pl.num_programs(2) - 1)
    def _():
        s_out[...] = s_scr[...]


def _hgrn_call(y, s0, lb, gn, *, B, T, L, valid):
    nc = T // L
    w = HG_HB * HG_DK

    def col(name):
        blk = EVEN_A[name] // w
        return pl.BlockSpec((L, w), lambda b, hg, c: (b * nc + c, blk + hg))

    state_spec = pl.BlockSpec((None, HG_HB, HG_DK, HG_DV), lambda b, hg, c: (b, hg, 0, 0))
    return pl.pallas_call(
        functools.partial(_hgrn_body, L=L, valid=valid),
        grid=(B, HG_HEADS // HG_HB, nc),
        in_specs=[col("qa"), col("fa"), col("ia"), col("za"),
                  pl.BlockSpec((1, w), lambda b, hg, c: (0, hg)),
                  pl.BlockSpec((1, HG_DV), lambda b, hg, c: (0, 0)),
                  state_spec],
        out_specs=[pl.BlockSpec((L, w), lambda b, hg, c: (b * nc + c, hg)), state_spec],
        out_shape=[jax.ShapeDtypeStruct((B * T, HG_W), BF16),
                   jax.ShapeDtypeStruct((B, HG_HEADS, HG_DK, HG_DV), F32)],
        scratch_shapes=[pltpu.VMEM((HG_HB, HG_DK, HG_DV), F32)],
        compiler_params=_params(("arbitrary", "arbitrary", "arbitrary")),
        name="hgrn2",
    )(y, y, y, y, lb.reshape(1, HG_W), gn.reshape(1, HG_DV), s0)


def _mlstm_body(q_ref, k_ref, v_ref, og_ref, z_ref, g_ref, bif_ref, gn_ref, c0_ref, n0_ref, m0_ref,
                h_ref, c_out, n_out, m_out, c_scr, n_scr, m_scr, *, L, valid):
    c = pl.program_id(2)

    @pl.when(c == 0)
    def _():
        c_scr[...] = c0_ref[...]
        n_scr[...] = n0_ref[...]
        m_scr[...] = m0_ref[...]

    gates = g_ref[...] + bif_ref[...]
    log_i = gates
    log_f = jnp.minimum(gates, 0.0) - jnp.log(1.0 + jnp.exp(-jnp.abs(gates)))
    if valid < L:
        live = _iota2((L, 1), 0) < valid
        log_i = jnp.where(live, log_i, -1e30)
        log_f = jnp.where(live, log_f, 0.0)
    tri = _iota2((L, L), 0) >= _iota2((L, L), 1)
    bcs = _cumsum_rows(log_f, tri.astype(BF16))
    for j in range(ML_HB):
        head = pl.program_id(1) * ML_HB + j
        b_col = _lane_col(bcs, ML_HEADS + head)
        i_col = _lane_col(log_i, head)
        b_row = _col_to_row(b_col, L)
        i_row = _col_to_row(i_col, L)
        m_prev = m_scr[:, j:j + 1]
        dmat = jnp.where(tri, b_col - b_row + i_row, NEG_INF)
        inter = b_col + m_prev
        mt = jnp.maximum(inter, jnp.max(dmat, axis=1, keepdims=True))
        w_in = jnp.exp(dmat - mt)
        w_x = jnp.exp(inter - mt)
        qj = q_ref[:, j * ML_DK:(j + 1) * ML_DK]
        kj = k_ref[:, j * ML_DK:(j + 1) * ML_DK] * (ML_DK ** -0.5)
        vj = v_ref[:, j * ML_DV:(j + 1) * ML_DV]
        qb, kb = qj.astype(BF16), kj.astype(BF16)
        sw = _dot_nt(qb, kb) * w_in
        c_prev = c_scr[j]
        n_prev = n_scr[:, j * ML_DK:(j + 1) * ML_DK]
        num = w_x * _dot_nt(qb, c_prev.astype(BF16)) + _dot(sw.astype(BF16), vj.astype(BF16))
        den = w_x * jnp.sum(qj * n_prev, axis=1, keepdims=True) + jnp.sum(sw, axis=1, keepdims=True)
        h = num / jnp.maximum(jnp.abs(den), jnp.exp(-mt))
        m_last = mt[L - 1:L, :]
        b_last = b_col[L - 1:L, :]
        w_end = jnp.exp(b_last - b_col + i_col - m_last)
        d_c = jnp.exp(b_last + m_prev - m_last)
        c_scr[j] = d_c * c_prev + _dot_tn((w_end * vj).astype(BF16), kb)
        n_scr[:, j * ML_DK:(j + 1) * ML_DK] = d_c * n_prev + jnp.sum(w_end * kj, axis=0, keepdims=True)
        m_scr[:, j:j + 1] = m_last
        sv = slice(j * ML_DV, (j + 1) * ML_DV)
        h_n = h * lax.rsqrt(jnp.mean(h * h, axis=-1, keepdims=True) + EPS) * gn_ref[:, sv]
        h_ref[:, sv] = (h_n * jax.nn.sigmoid(og_ref[:, sv]) * _silu(z_ref[:, sv])).astype(h_ref.dtype)

    @pl.when(c == pl.num_programs(2) - 1)
    def _():
        c_out[...] = c_scr[...]
        n_out[...] = n_scr[...]
        m_out[...] = m_scr[...]


def _mlstm_call(ya, yb, c0, n0, m0, bif_r, gn, *, B, T, L, valid):
    nc = T // L
    ng = ML_HEADS // ML_HB
    wk, wv = ML_HB * ML_DK, ML_HB * ML_DV

    def col(name, w):
        blk = (ODD_A[name] if name in ODD_A else ODD_B[name]) // w
        return pl.BlockSpec((L, w), lambda b, hg, c: (b * nc + c, blk + hg))

    c_spec = pl.BlockSpec((None, ML_HB, ML_DV, ML_DK), lambda b, hg, c: (b, hg, 0, 0))
    n_spec = pl.BlockSpec((None, 1, wk), lambda b, hg, c: (b, 0, hg))
    m_spec = pl.BlockSpec((None, None, 1, LANES), lambda b, hg, c: (b, hg, 0, 0))
    m0_r = jnp.pad(m0.reshape(B, ng, 1, ML_HB), ((0, 0), (0, 0), (0, 0), (0, LANES - ML_HB)))
    h, c_new, n_new, m_new = pl.pallas_call(
        functools.partial(_mlstm_body, L=L, valid=valid),
        grid=(B, ng, nc),
        in_specs=[col("q", wk), col("k", wk), col("v", wv), col("og", wv), col("z", wv),
                  pl.BlockSpec((L, LANES), lambda b, hg, c: (b * nc + c, ODD_B["gates"] // LANES)),
                  pl.BlockSpec((1, LANES), lambda b, hg, c: (0, 0)),
                  pl.BlockSpec((1, wv), lambda b, hg, c: (0, hg)),
                  c_spec, n_spec, m_spec],
        out_specs=[pl.BlockSpec((L, wv), lambda b, hg, c: (b * nc + c, hg)), c_spec, n_spec, m_spec],
        out_shape=[jax.ShapeDtypeStruct((B * T, ML_V_W), BF16),
                   jax.ShapeDtypeStruct((B, ML_HEADS, ML_DV, ML_DK), F32),
                   jax.ShapeDtypeStruct((B, 1, ML_QK_W), F32),
                   jax.ShapeDtypeStruct((B, ng, 1, LANES), F32)],
        scratch_shapes=[pltpu.VMEM((ML_HB, ML_DV, ML_DK), F32), pltpu.VMEM((1, wk), F32), pltpu.VMEM((1, LANES), F32)],
        compiler_params=_params(("arbitrary", "arbitrary", "arbitrary")),
        name="mlstm",
    )(ya, ya, ya, ya, yb, yb, bif_r, gn.reshape(1, ML_V_W), c0, n0.reshape(B, 1, ML_QK_W), m0_r)
    return h, c_new, n_new.reshape(B, ML_HEADS, ML_DK), m_new[:, :, 0, :ML_HB].reshape(B, ML_HEADS)


def _mem_body(q_ref, k_ref, v_ref, o_ref):
    q = q_ref[...] * (MEM_HD ** -0.5)
    for h in range(MEM_HEADS):
        sl = slice(h * MEM_HD, (h + 1) * MEM_HD)
        s = _dot_nt(q[:, sl].astype(BF16), k_ref[:, sl].astype(BF16))
        p = jnp.exp(s - jnp.max(s, axis=-1, keepdims=True))
        o = _dot(p.astype(BF16), v_ref[:, sl].astype(BF16)) / jnp.sum(p, axis=-1, keepdims=True)
        o_ref[:, sl] = o.astype(o_ref.dtype)


def _mem_call(y, q_off, k2d, v2d, *, B, T, tq=256):
    tq = min(tq, T)
    nq = T // tq
    qb = q_off // MEM_W
    return pl.pallas_call(
        _mem_body,
        grid=(B, nq),
        in_specs=[pl.BlockSpec((tq, MEM_W), lambda b, i: (b * nq + i, qb)),
                  pl.BlockSpec((N_MEM, MEM_W), lambda b, i: (b, 0)),
                  pl.BlockSpec((N_MEM, MEM_W), lambda b, i: (b, 0))],
        out_specs=pl.BlockSpec((tq, MEM_W), lambda b, i: (b * nq + i, 0)),
        out_shape=jax.ShapeDtypeStruct((B * T, MEM_W), BF16),
        compiler_params=_params(("parallel", "parallel")),
        name="mem_attn",
    )(y, k2d, v2d)


def _gelu_tanh(x):
    return 0.5 * x * (1.0 + jnp.tanh(math.sqrt(2.0 / math.pi) * (x + 0.044715 * (x * x * x))))


def _compress_body(x_ref, w1_ref, b1_ref, w2_ref, pe_ref, o_ref, x32, *, nch):
    x32[...] = x_ref[...].astype(F32)
    a = jnp.zeros((nch, NSA_HD), F32)
    b = jnp.zeros((nch, NSA_HD), F32)
    for s in range(CMP_STRIDE):
        r = x32[pl.ds(s, nch, stride=CMP_STRIDE), :]
        a = a + _dot((r + pe_ref[s:s + 1, :]).astype(BF16), w1_ref[s])
        b = b + _dot((r + pe_ref[CMP_STRIDE + s:CMP_STRIDE + s + 1, :]).astype(BF16), w1_ref[CMP_STRIDE + s])
    h = a + pltpu.roll(b, nch - 1, 0) + b1_ref[...]
    o_ref[...] = _dot(_gelu_tanh(h).astype(BF16), w2_ref[...])


def _compress_call(x16, w1, b1, w2, pe, *, B, T):
    nch = T // CMP_STRIDE
    return pl.pallas_call(
        functools.partial(_compress_body, nch=nch),
        grid=(B, NSA_KVH),
        in_specs=[pl.BlockSpec((T, NSA_HD), lambda b, h: (b, h)),
                  pl.BlockSpec((CMP_BLOCK, NSA_HD, NSA_HD), lambda b, h: (0, 0, 0)),
                  pl.BlockSpec((1, NSA_HD), lambda b, h: (0, 0)),
                  pl.BlockSpec((NSA_HD, NSA_HD), lambda b, h: (0, 0)),
                  pl.BlockSpec((CMP_BLOCK, NSA_HD), lambda b, h: (0, 0))],
        out_specs=pl.BlockSpec((None, None, nch, NSA_HD), lambda b, h: (b, h, 0, 0)),
        out_shape=jax.ShapeDtypeStruct((B, NSA_KVH, nch, NSA_HD), F32),
        scratch_shapes=[pltpu.VMEM((T, NSA_HD), F32)],
        compiler_params=_params(("parallel", "parallel")),
        name="nsa_compress",
    )(x16, w1.astype(BF16), b1.reshape(1, NSA_HD), w2.astype(BF16), pe)


def _softmax_rows(s):
    m = jnp.max(s, axis=-1, keepdims=True)
    m = jnp.where(m == NEG_INF, 0.0, m)
    p = jnp.exp(s - m)
    return p, jnp.sum(p, axis=-1, keepdims=True)


def _slc_scores(psum, width, n_slc):
    ncmp = psum.shape[1]
    d = _iota2((ncmp, width), 0) - (SEL_BLOCK // CMP_STRIDE) * _iota2((ncmp, width), 1)
    wgt = jnp.where((d == -1) | (d == 3), 1.0, jnp.where((d >= 0) & (d <= 2), 2.0, 0.0))
    wgt = jnp.where(_iota2((ncmp, width), 1) < n_slc, wgt, 0.0).astype(BF16)
    p_hi = psum.astype(BF16)
    p_lo = (psum - p_hi.astype(F32)).astype(BF16)
    return _dot(p_hi, wgt) + _dot(p_lo, wgt)


def _top_blocks(slc, cur, n_pick):
    rows, width = slc.shape
    blk = _iota2((rows, width), 1)
    forced = (blk == 0) | (blk == cur) | (blk == cur - 1)
    score = jnp.where(forced, jnp.inf, slc)
    score = jnp.where(blk > cur, NEG_INF, score)
    blk_f = blk.astype(F32)
    lane = _iota2((rows, LANES), 1)
    sel = jnp.zeros((rows, width), F32)
    picks = jnp.zeros((rows, LANES), F32)
    for i in range(n_pick):
        mx = jnp.max(score, axis=-1, keepdims=True)
        first = jnp.min(jnp.where(score == mx, blk_f, float(width)), axis=-1, keepdims=True)
        pick = blk_f == first
        sel = jnp.where(pick, 1.0, sel)
        picks = jnp.where(lane == i, first, picks)
        score = jnp.where(pick, NEG_INF, score)
    return sel, picks


def _member_by_rank(psum, tpos_row, n_slc, n_pick):
    nq, ncmp = psum.shape
    nb = -(-n_slc // 8) * 8
    d = _iota2((nb, ncmp), 1) - (SEL_BLOCK // CMP_STRIDE) * _iota2((nb, ncmp), 0)
    wgt = jnp.where((d == -1) | (d == 3), 1.0, jnp.where((d >= 0) & (d <= 2), 2.0, 0.0))
    wgt = jnp.where(_iota2((nb, ncmp), 0) < n_slc, wgt, 0.0).astype(BF16)
    p_hi = psum.astype(BF16)
    p_lo = (psum - p_hi.astype(F32)).astype(BF16)
    slc = _dot_nt(wgt, p_hi) + _dot_nt(wgt, p_lo)
    blk = _iota2((nb, nq), 0)
    cur = jnp.right_shift(tpos_row, SEL_SHIFT)
    forced = (blk == 0) | (blk == cur) | (blk == cur - 1)
    score = jnp.where(forced, jnp.inf, slc)
    score = jnp.where(blk > cur, NEG_INF, score)
    ahead = jnp.zeros((nb, nq), F32)
    for i in range(n_slc):
        s_i = score[i:i + 1, :]
        ahead = ahead + jnp.where((s_i > score) | ((s_i == score) & (blk > i)), 1.0, 0.0)
    return jnp.where((ahead < n_pick) & (blk <= cur), 1.0, 0.0)


def _nsa_prompt_body(q_ref, zb_ref, gb_ref, bg_ref, ks_ref, vs_ref, kw_ref, vw_ref, kc_ref, vc_ref,
                     bc_ref, bs_ref, bw_ref, o_ref, ksp, vsp, kwp, vwp, osel, *, T):
    qi = pl.program_id(2)
    tq = Q_BLOCK
    front = T - tq
    wlen = WINDOW + tq
    n_slc = T // SEL_BLOCK

    @pl.when(qi == 0)
    def _():
        ksp[0:front, :] = jnp.zeros((front, NSA_HD), BF16)
        vsp[0:front, :] = jnp.zeros((front, NSA_HD), BF16)
        ksp[front:front + T, :] = ks_ref[...].astype(BF16)
        vsp[front:front + T, :] = vs_ref[...].astype(BF16)
        kwp[0:WINDOW, :] = jnp.zeros((WINDOW, NSA_HD), BF16)
        vwp[0:WINDOW, :] = jnp.zeros((WINDOW, NSA_HD), BF16)
        kwp[WINDOW:WINDOW + T, :] = kw_ref[...].astype(BF16)
        vwp[WINDOW:WINDOW + T, :] = vw_ref[...].astype(BF16)

    t0 = pl.multiple_of(qi * tq, tq)
    tpos = _iota2((tq, 1), 0) + t0
    q_all = q_ref[...] * (NSA_HD ** -0.5)
    qs = [q_all[:, g * NSA_HD:(g + 1) * NSA_HD].astype(BF16) for g in range(NSA_G)]

    ncmp = T // CMP_STRIDE
    vis = tpos >= _iota2((1, ncmp), 1) * CMP_STRIDE + (CMP_BLOCK - 1)
    kcb = kc_ref[...].astype(BF16)
    vcb = vc_ref[...].astype(BF16)
    psum = jnp.zeros((tq, ncmp), F32)
    o_cmp = []
    for g in range(NSA_G):
        s = jnp.where(vis, _dot_nt(qs[g], kcb) + bc_ref[g], NEG_INF)
        p, l = _softmax_rows(s)
        p = p / jnp.maximum(l, TINY)
        psum = psum + p
        o_cmp.append(_dot(p.astype(BF16), vcb))

    member_t = _member_by_rank(psum, _iota2((1, tq), 1) + t0, n_slc, min(N_SEL, n_slc)).astype(BF16)

    nb = member_t.shape[0]
    n_win = SEL_WINDOWS if T % (SEL_WINDOWS * tq) == 0 else 1
    for i in range(n_win):
        w_prev, w = T * i // n_win, T * (i + 1) // n_win

        @pl.when((qi >= w_prev // tq) & (qi < w // tq))
        def _(w=w):
            off = T - w
            col_blk = (jnp.right_shift(_iota2((nb, w), 1) + off, SEL_SHIFT)
                       + (qi * (tq // SEL_BLOCK) + (tq - T) // SEL_BLOCK))
            expand = (col_blk == _iota2((nb, w), 0)).astype(BF16)
            kpos = _iota2((1, w), 1) + (t0 + tq - w)
            allowed = (_dot_tn(member_t, expand) > 0.5) & (kpos <= tpos)
            mask_s = jnp.where(allowed, 0.0, NEG_INF)
            k_s = ksp[pl.ds(t0 + off, w), :]
            v_s = vsp[pl.ds(t0 + off, w), :]
            for g in range(NSA_G):
                p, l = _softmax_rows(_dot_nt(qs[g], k_s) + bs_ref[g, :, off:] + mask_s)
                osel[g] = _dot(p.astype(BF16), v_s) / jnp.maximum(l, TINY)

    dist = WINDOW + _iota2((tq, wlen), 0) - _iota2((tq, wlen), 1)
    in_win = (dist >= 0) & (dist < WINDOW) & (_iota2((1, wlen), 1) + (t0 - WINDOW) >= 0)
    mask_w = jnp.where(in_win, 0.0, NEG_INF)
    k_w = kwp[pl.ds(t0, wlen), :]
    v_w = vwp[pl.ds(t0, wlen), :]
    gate = jax.nn.sigmoid(gb_ref[...] + bg_ref[...])
    zb = _silu(zb_ref[...])
    for g in range(NSA_G):
        p, l = _softmax_rows(_dot_nt(qs[g], k_w) + bw_ref[g] + mask_w)
        o_win = _dot(p.astype(BF16), v_w) / jnp.maximum(l, TINY)
        head = pl.program_id(1) * NSA_G + g
        mix = (_lane_col(gate, head) * o_cmp[g] + _lane_col(gate, NSA_HEADS + head) * osel[g]
               + _lane_col(gate, 2 * NSA_HEADS + head) * o_win)
        sl = slice(g * NSA_HD, (g + 1) * NSA_HD)
        o_ref[:, sl] = (mix * zb[:, sl]).astype(o_ref.dtype)


def _nsa_prompt_call(ya, yb, kv16, kcmp, vcmp, bg_r, bias_c, bias_s, bias_w, *, B, T):
    nq = T // Q_BLOCK
    gw = NSA_G * NSA_HD
    wlen = WINDOW + Q_BLOCK
    kv_spec = pl.BlockSpec((T, NSA_HD), lambda b, h, i: (b, h))
    cmp_spec = pl.BlockSpec((None, None, T // CMP_STRIDE, NSA_HD), lambda b, h, i: (b, h, 0, 0))
    return pl.pallas_call(
        functools.partial(_nsa_prompt_body, T=T),
        grid=(B, NSA_KVH, nq),
        in_specs=[pl.BlockSpec((Q_BLOCK, gw), lambda b, h, i: (b * nq + i, EVEN_A["qb"] // gw + h)),
                  pl.BlockSpec((Q_BLOCK, gw), lambda b, h, i: (b * nq + i, EVEN_B["zb"] // gw + h)),
                  pl.BlockSpec((Q_BLOCK, LANES), lambda b, h, i: (b * nq + i, EVEN_B["gb"] // LANES)),
                  pl.BlockSpec((1, LANES), lambda b, h, i: (0, 0)),
                  kv_spec, kv_spec, kv_spec, kv_spec, cmp_spec, cmp_spec,
                  pl.BlockSpec((None, NSA_G, Q_BLOCK, T // CMP_STRIDE), lambda b, h, i: (h, 0, i, 0)),
                  pl.BlockSpec((None, NSA_G, Q_BLOCK, T), lambda b, h, i: (h, 0, 0, 0)),
                  pl.BlockSpec((None, NSA_G, Q_BLOCK, wlen), lambda b, h, i: (h, 0, 0, 0))],
        out_specs=pl.BlockSpec((Q_BLOCK, gw), lambda b, h, i: (b * nq + i, h)),
        out_shape=jax.ShapeDtypeStruct((B * T, NSA_W), BF16),
        scratch_shapes=[pltpu.VMEM((2 * T - Q_BLOCK, NSA_HD), BF16), pltpu.VMEM((2 * T - Q_BLOCK, NSA_HD), BF16),
                        pltpu.VMEM((WINDOW + T, NSA_HD), BF16), pltpu.VMEM((WINDOW + T, NSA_HD), BF16),
                        pltpu.VMEM((NSA_G, Q_BLOCK, NSA_HD), F32)],
        compiler_params=_params(("arbitrary", "arbitrary", "arbitrary")),
        name="nsa_prompt",
    )(ya, yb, yb, bg_r, *kv16, kcmp, vcmp, bias_c, bias_s, bias_w)


CMP_PAGES = 16
CHUNKS_PER_PAGE = PAGE_SIZE // CMP_STRIDE
PAGE_ROWS = PAGE_SIZE * NSA_KVH


def _pool_rows(pool):
    return pool.reshape(pool.shape[0] * PAGE_ROWS, NSA_HD)


def _cmp_pages_body(pt_ref, *refs):
    del pt_ref
    pages = refs[:CMP_PAGES]
    w_ref, pe_ref, o_ref = refs[CMP_PAGES:]
    rows = CMP_PAGES * CHUNKS_PER_PAGE
    per_head = [jnp.concatenate(
        [jnp.concatenate([pg[pl.ds(NSA_KVH * s + h, CHUNKS_PER_PAGE, stride=CMP_STRIDE * NSA_KVH), :]
                          for s in range(CMP_STRIDE)], axis=1) for pg in pages], axis=0) for h in range(NSA_KVH)]
    w = w_ref[...]
    r = _dot(jnp.concatenate(per_head, axis=0).astype(BF16), w)
    pc = _dot(pe_ref[...], w)
    r = r + jnp.concatenate([pc[0:1, :NSA_HD], pc[1:2, NSA_HD:]], axis=1)
    for h in range(NSA_KVH):
        o_ref[h] = r[h * rows:(h + 1) * rows]


def _cmp_pages_call(pool, page_table, w1, pe, *, B):
    n_pages = page_table.shape[1]
    rows = CMP_PAGES * CHUNKS_PER_PAGE
    view = _pool_rows(pool)
    w = w1.reshape(2, CMP_STRIDE, NSA_HD, NSA_HD).transpose(1, 2, 0, 3).reshape(CMP_STRIDE * NSA_HD, 2 * NSA_HD)
    pe_rows = jnp.pad(pe.reshape(2, CMP_STRIDE * NSA_HD), ((0, 6), (0, 0))).astype(BF16)

    def page_spec(i):
        return pl.BlockSpec((PAGE_ROWS, NSA_HD), lambda b, s, pt: (pt[b * n_pages + s * CMP_PAGES + i], 0))

    grid_spec = pltpu.PrefetchScalarGridSpec(
        num_scalar_prefetch=1,
        grid=(B, n_pages // CMP_PAGES),
        in_specs=[page_spec(i) for i in range(CMP_PAGES)]
        + [pl.BlockSpec((CMP_STRIDE * NSA_HD, 2 * NSA_HD), lambda b, s, pt: (0, 0)),
           pl.BlockSpec((8, CMP_STRIDE * NSA_HD), lambda b, s, pt: (0, 0))],
        out_specs=pl.BlockSpec((None, NSA_KVH, rows, 2 * NSA_HD), lambda b, s, pt: (b, 0, s, 0)),
    )
    return pl.pallas_call(
        _cmp_pages_body,
        grid_spec=grid_spec,
        out_shape=jax.ShapeDtypeStruct((B, NSA_KVH, n_pages * CHUNKS_PER_PAGE, 2 * NSA_HD), F32),
        compiler_params=_params(("arbitrary", "arbitrary")),
        name="nsa_cmp_pages",
    )(page_table.reshape(-1), *([view] * CMP_PAGES), w.astype(BF16), pe_rows)


SEL_WINDOWS = 4
SLC_LANES = 384


def _sample_q_rows(q_ref):
    q = q_ref[...] * (NSA_HD ** -0.5)
    return jnp.concatenate([q[:, g * NSA_HD:(g + 1) * NSA_HD] for g in range(NSA_G)], axis=0).astype(BF16)


def _nsa_sample_main_body(abk_ref, abv_ref, b1_ref, w2_ref, q_ref, wk_ref, wv_ref, kn_ref, vn_ref, bc_ref, bw_ref,
                          ocmp_ref, owin_ref, idx_ref, *, T, n_slc):
    tp = SAMPLE_PAD_T
    rows = NSA_G * tp
    ncmp = abk_ref.shape[0]

    def compressed(ab_ref, t):
        ab = ab_ref[...]
        h = ab[:, :NSA_HD] + pltpu.roll(ab[:, NSA_HD:], ncmp - 1, 0) + b1_ref[t]
        return _dot(_gelu_tanh(h).astype(BF16), w2_ref[t]).astype(BF16)

    kc, vc = compressed(abk_ref, 0), compressed(abv_ref, 1)
    q = _sample_q_rows(q_ref)
    step = jnp.bitwise_and(_iota2((rows, 1), 0), tp - 1)
    tpos = PAST_LEN + step
    vis = tpos >= _iota2((1, ncmp), 1) * CMP_STRIDE + (CMP_BLOCK - 1)
    p, l = _softmax_rows(jnp.where(vis, _dot_nt(q, kc) + bc_ref[...], NEG_INF))
    p = p / jnp.maximum(l, TINY)
    ocmp_ref[...] = _dot(p.astype(BF16), vc)
    psum = p[0:tp]
    for g in range(1, NSA_G):
        psum = psum + p[g * tp:(g + 1) * tp]
    cur = jnp.right_shift(PAST_LEN + _iota2((tp, 1), 0), SEL_SHIFT)
    _, picks = _top_blocks(_slc_scores(psum, SLC_LANES, n_slc), cur, N_SEL)
    idx_ref[...] = picks.astype(jnp.int32)

    wb = wk_ref.shape[0] // NSA_KVH
    wlen = bw_ref.shape[1]
    fill = jnp.zeros((wlen - wb - tp, NSA_HD), BF16)
    head = pl.program_id(1)
    k_all = jnp.concatenate([wk_ref[pl.ds(head, wb, stride=NSA_KVH), :].astype(BF16), kn_ref[...], fill], axis=0)
    v_all = jnp.concatenate([wv_ref[pl.ds(head, wb, stride=NSA_KVH), :].astype(BF16), vn_ref[...], fill], axis=0)
    col = _iota2((1, wlen), 1)
    dist = tpos - (PAST_LEN - wb + col)
    in_win = (dist >= 0) & (dist < WINDOW) & (col < wb + T)
    pw, lw = _softmax_rows(jnp.where(in_win, _dot_nt(q, k_all) + bw_ref[...], NEG_INF))
    owin_ref[...] = _dot(pw.astype(BF16), v_all) / jnp.maximum(lw, TINY)


def _nsa_sample_main_call(ya, kw16, vw16, abk, abv, b1, w2, wk, wv, bias_c, bias_w, *, B, T):
    tp = SAMPLE_PAD_T
    rows = NSA_G * tp
    gw = NSA_G * NSA_HD
    ncmp = abk.shape[2]
    wb = wk.shape[1]
    wlen = bias_w.shape[-1]
    n_slc = -(-(PAST_LEN + T) // SEL_BLOCK)
    assert n_slc <= SLC_LANES and T <= tp
    ab_spec = pl.BlockSpec((None, None, ncmp, 2 * NSA_HD), lambda b, h: (b, h, 0, 0))
    win_spec = pl.BlockSpec((wb * NSA_KVH, NSA_HD), lambda b, h: (b, 0))
    o_spec = pl.BlockSpec((None, None, rows, NSA_HD), lambda b, h: (b, h, 0, 0))
    return pl.pallas_call(
        functools.partial(_nsa_sample_main_body, T=T, n_slc=n_slc),
        grid=(B, NSA_KVH),
        in_specs=[ab_spec, ab_spec,
                  pl.BlockSpec((2, 1, NSA_HD), lambda b, h: (0, 0, 0)),
                  pl.BlockSpec((2, NSA_HD, NSA_HD), lambda b, h: (0, 0, 0)),
                  pl.BlockSpec((tp, gw), lambda b, h: (b, EVEN_A["qb"] // gw + h)),
                  win_spec, win_spec,
                  pl.BlockSpec((tp, NSA_HD), lambda b, h: (b, h)),
                  pl.BlockSpec((tp, NSA_HD), lambda b, h: (b, h)),
                  pl.BlockSpec((None, rows, ncmp), lambda b, h: (h, 0, 0)),
                  pl.BlockSpec((None, rows, wlen), lambda b, h: (h, 0, 0))],
        out_specs=[o_spec, o_spec, pl.BlockSpec((None, None, tp, LANES), lambda b, h: (b, h, 0, 0))],
        out_shape=[jax.ShapeDtypeStruct((B, NSA_KVH, rows, NSA_HD), F32),
                   jax.ShapeDtypeStruct((B, NSA_KVH, rows, NSA_HD), F32),
                   jax.ShapeDtypeStruct((B, NSA_KVH, tp, LANES), jnp.int32)],
        compiler_params=_params(("parallel", "parallel")),
        name="nsa_sample_main",
    )(abk, abv, b1.reshape(2, 1, NSA_HD), w2.astype(BF16), ya,
      wk.reshape(B * wb * NSA_KVH, NSA_HD), wv.reshape(B * wb * NSA_KVH, NSA_HD), kw16, vw16, bias_c, bias_w)


NEAR_BLOCKS = 3


def _nsa_sample_sel_body(idx_ref, pt_ref, q_ref, kn_ref, vn_ref, tbl_ref, ocmp_ref, owin_ref, gb_ref, bg_ref, zb_ref,
                         *refs, T):
    del pt_ref
    k_blocks = refs[:N_SEL]
    v_blocks = refs[N_SEL:2 * N_SEL]
    o_ref, osel = refs[2 * N_SEL:]
    tp = SAMPLE_PAD_T
    rows = NSA_G * tp
    b, h, t = pl.program_id(0), pl.program_id(1), pl.program_id(2)
    base = ((b * NSA_KVH + h) * T + t) * N_SEL
    first_new = PAST_LEN // SEL_BLOCK
    cur = jnp.right_shift(PAST_LEN + t, SEL_SHIFT)
    q = _sample_q_rows(q_ref)
    pad = jnp.zeros((SEL_BLOCK - tp, NSA_HD), BF16)
    k_new = jnp.concatenate([kn_ref[...], pad], axis=0)
    v_new = jnp.concatenate([vn_ref[...], pad], axis=0)
    lane = _iota2((1, LANES), 1)
    low = lane < SEL_BLOCK
    within = jnp.bitwise_and(lane, SEL_BLOCK - 1)
    ks, vs, bias, kpos = [], [], [], []
    for i in range(0, N_SEL, 2):
        pair_bias, pair_pos = [], []
        for j in (i, i + 1):
            blk = idx_ref[base + j]
            is_new = blk >= first_new
            ks.append(jnp.where(is_new, k_new, k_blocks[j][pl.ds(h, SEL_BLOCK, stride=NSA_KVH), :].astype(BF16)))
            vs.append(jnp.where(is_new, v_new, v_blocks[j][pl.ds(h, SEL_BLOCK, stride=NSA_KVH), :].astype(BF16)))
            pair_bias.append(tbl_ref[jnp.clip(blk - (first_new - NEAR_BLOCKS), 0, NEAR_BLOCKS)])
            pair_pos.append(jnp.where(blk <= cur, blk * SEL_BLOCK, PAST_LEN + SEL_BLOCK * LANES) + within)
        bias.append(jnp.where(low, pair_bias[0], pair_bias[1]))
        kpos.append(jnp.where(low, pair_pos[0], pair_pos[1]))
    k_all = jnp.concatenate(ks, axis=0)
    v_all = jnp.concatenate(vs, axis=0)
    step = jnp.bitwise_and(_iota2((rows, 1), 0), tp - 1)
    ok = jnp.concatenate(kpos, axis=1) <= PAST_LEN + step
    p, l = _softmax_rows(jnp.where(ok, _dot_nt(q, k_all) + jnp.concatenate(bias, axis=1), NEG_INF))
    o = _dot(p.astype(BF16), v_all) / jnp.maximum(l, TINY)

    @pl.when(t == 0)
    def _():
        osel[...] = jnp.zeros_like(osel)

    osel[...] = jnp.where(step == t, o, osel[...])

    @pl.when(t == T - 1)
    def _():
        gate = jax.nn.sigmoid(gb_ref[...] + bg_ref[...])
        zb = _silu(zb_ref[...])
        for g in range(NSA_G):
            r = slice(g * tp, (g + 1) * tp)
            head = h * NSA_G + g
            mix = (_lane_col(gate, head) * ocmp_ref[r, :] + _lane_col(gate, NSA_HEADS + head) * osel[r, :]
                   + _lane_col(gate, 2 * NSA_HEADS + head) * owin_ref[r, :])
            sl = slice(g * NSA_HD, (g + 1) * NSA_HD)
            o_ref[:, sl] = (mix * zb[:, sl]).astype(o_ref.dtype)


def _nsa_sample_sel_call(ya, yb, ks16, vs16, idx, page_table, pool_k, pool_v, tbl, o_cmp, o_win, bg_r, *, B, T):
    tp = SAMPLE_PAD_T
    rows = NSA_G * tp
    gw = NSA_G * NSA_HD
    n_pages = page_table.shape[1]
    halves = PAGE_SIZE // SEL_BLOCK
    idx_flat = idx[:, :, :T, :N_SEL].reshape(-1)
    view_k, view_v = _pool_rows(pool_k), _pool_rows(pool_v)

    def blk_spec(j):
        def index(b, h, t, idx_s, pt_s):
            blk = idx_s[((b * NSA_KVH + h) * T + t) * N_SEL + j]
            page = pt_s[b * n_pages + jnp.minimum(blk // halves, n_pages - 1)]
            return (page * halves + blk % halves, 0)
        return pl.BlockSpec((SEL_BLOCK * NSA_KVH, NSA_HD), index)

    o_spec = pl.BlockSpec((None, None, rows, NSA_HD), lambda b, h, t, *_: (b, h, 0, 0))
    grid_spec = pltpu.PrefetchScalarGridSpec(
        num_scalar_prefetch=2,
        grid=(B, NSA_KVH, T),
        in_specs=[pl.BlockSpec((tp, gw), lambda b, h, t, *_: (b, EVEN_A["qb"] // gw + h)),
                  pl.BlockSpec((tp, NSA_HD), lambda b, h, t, *_: (b, h)),
                  pl.BlockSpec((tp, NSA_HD), lambda b, h, t, *_: (b, h)),
                  pl.BlockSpec((None, NEAR_BLOCKS + 1, rows, LANES), lambda b, h, t, *_: (h, 0, 0, 0)),
                  o_spec, o_spec,
                  pl.BlockSpec((tp, LANES), lambda b, h, t, *_: (b, EVEN_B["gb"] // LANES)),
                  pl.BlockSpec((1, LANES), lambda b, h, t, *_: (0, 0)),
                  pl.BlockSpec((tp, gw), lambda b, h, t, *_: (b, EVEN_B["zb"] // gw + h))]
        + [blk_spec(j) for j in range(N_SEL)] * 2,
        out_specs=pl.BlockSpec((tp, gw), lambda b, h, t, *_: (b, h)),
        scratch_shapes=[pltpu.VMEM((rows, NSA_HD), F32)],
    )
    return pl.pallas_call(
        functools.partial(_nsa_sample_sel_body, T=T),
        grid_spec=grid_spec,
        out_shape=jax.ShapeDtypeStruct((B * tp, NSA_W), BF16),
        compiler_params=_params(("arbitrary", "arbitrary", "arbitrary")),
        name="nsa_sample_sel",
    )(idx_flat, page_table.reshape(-1), ya, ks16, vs16, tbl, o_cmp, o_win, yb, bg_r, yb,
      *([view_k] * N_SEL), *([view_v] * N_SEL))


def _sample_bias_tables(rel_bias, T, wb):
    tp = SAMPLE_PAD_T
    ncmp = PAST_LEN // CMP_STRIDE
    wlen = -(-(wb + tp) // LANES) * LANES
    first = PAST_LEN // SEL_BLOCK - NEAR_BLOCKS
    assert PAST_LEN - ((first + 1) * SEL_BLOCK - 1) >= REL_MAX_DIST
    lo, hi = -wlen, PAST_LEN + tp
    rev = _bias_line(rel_bias, lo, hi, descending=True)

    def rows(tbl):
        return tbl.reshape(NSA_KVH, NSA_G * tp, tbl.shape[-1])

    t_c = _toeplitz(rev, hi - 1 - (PAST_LEN - (CMP_BLOCK - 1)), tp, CMP_STRIDE * ncmp)[:, :, ::CMP_STRIDE]
    t_w = _toeplitz(rev, hi - 1 - wb, tp, wlen)
    far = jnp.broadcast_to(rev[:, hi - 1 - REL_MAX_DIST][:, None, None], (NSA_HEADS, tp, LANES))
    near = []
    for k in range(1, NEAR_BLOCKS + 1):
        half = _toeplitz(rev, hi - 1 - (PAST_LEN - (first + k) * SEL_BLOCK), tp, SEL_BLOCK)
        near.append(jnp.concatenate([half, half], axis=-1))
    t_s = jnp.stack([far] + near, axis=1).reshape(NSA_KVH, NSA_G, NEAR_BLOCKS + 1, tp, LANES)
    t_s = t_s.transpose(0, 2, 1, 3, 4).reshape(NSA_KVH, NEAR_BLOCKS + 1, NSA_G * tp, LANES)
    return rows(t_c), rows(t_w), t_s


def _tail_even(w):
    return _tail_relayout(w, EVEN_KV_OFF + 6 * NSA_KV_W, 3 * NSA_HEADS, NSA_W + MEM_W, EVEN_B_N)


def _tail_odd(w):
    return _tail_relayout(w, ODD_A_N, 2 * ML_HEADS, ML_V_W + MEM_W, ODD_B_N)


def _gate_bias_even(b_gate):
    return jnp.pad(b_gate, (0, LANES - 3 * NSA_HEADS)).reshape(1, LANES)


def _gate_bias_odd(b_if):
    return jnp.pad(b_if.reshape(2 * ML_HEADS), (0, LANES - 2 * ML_HEADS)).reshape(1, LANES)


def _rel_bucket(dist):
    n = np.maximum(dist, 0)
    exact = REL_BUCKETS // 2
    nf = np.maximum(n, 1).astype(np.float32)
    large = exact + (np.log(nf / exact) / math.log(REL_MAX_DIST / exact) * (REL_BUCKETS - exact)).astype(np.int32)
    return np.where(n < exact, n, np.minimum(large, REL_BUCKETS - 1))


def _bias_line(rel_bias, lo, hi, descending=False):
    dist = np.arange(hi - 1, lo - 1, -1) if descending else np.arange(lo, hi)
    buckets = _rel_bucket(dist)
    edges = np.flatnonzero(np.diff(buckets)) + 1
    starts = np.concatenate([[0], edges])
    ends = np.concatenate([edges, [hi - lo]])
    bias_t = rel_bias.T.astype(F32)
    runs = [jnp.broadcast_to(bias_t[:, int(buckets[s])][:, None], (NSA_HEADS, int(e - s))) for s, e in zip(starts, ends)]
    return jnp.concatenate(runs, axis=1)


def _skew_rows(v, rows, step, cols):
    n = v.shape[1]
    reps = -(-rows * (n + step) // n)
    return jnp.tile(v, (1, reps))[:, :rows * (n + step)].reshape(v.shape[0], rows, n + step)[:, :, :cols]


def _toeplitz(rev, start, rows, cols):
    seg = rev[:, start - (rows - 1):start + cols]
    return _skew_rows(jnp.roll(seg, -(rows - 1), axis=1), rows, -1, cols)


def _prompt_bias_tables(rel_bias, T):
    ncmp = T // CMP_STRIDE
    wlen = WINDOW + Q_BLOCK
    lo, hi = -(CMP_STRIDE * ncmp + CMP_BLOCK), T
    line = _bias_line(rel_bias, lo, hi)
    rev = _bias_line(rel_bias, lo, hi, descending=True)

    def split(tbl):
        return tbl.reshape((NSA_KVH, NSA_G) + tbl.shape[1:])

    back = CMP_STRIDE * (ncmp - 1)
    first = -(back + CMP_BLOCK - 1) - lo
    seg = line[:, first:first + T + back]
    t_c = _skew_rows(jnp.roll(seg, -back, axis=1), ncmp, -CMP_STRIDE, T).swapaxes(1, 2)
    t_s = _toeplitz(rev, hi - 1 - (T - Q_BLOCK), Q_BLOCK, T)
    t_w = _toeplitz(rev, hi - 1 - WINDOW, Q_BLOCK, wlen)
    return split(t_c), split(t_s), split(t_w)


def _nsa_sample(ya, yb, kv16, page_table, pk_cmp, pv_cmp, pk_sel, pv_sel, wk, wv, bg_r, w1, b1, w2, pe, rel_bias,
                *, B, T):
    assert (PAST_LEN + T) // CMP_STRIDE == PAST_LEN // CMP_STRIDE
    abk = _cmp_pages_call(pk_cmp, page_table, w1[0], pe[0], B=B)
    abv = _cmp_pages_call(pv_cmp, page_table, w1[1], pe[1], B=B)
    bias_c, bias_w, tbl = _sample_bias_tables(rel_bias, T, wk.shape[1])
    o_cmp, o_win, idx = _nsa_sample_main_call(ya, kv16[4], kv16[5], abk, abv, b1, w2, wk, wv, bias_c, bias_w, B=B, T=T)
    return _nsa_sample_sel_call(ya, yb, kv16[2], kv16[3], idx, page_table, pk_sel, pv_sel, tbl, o_cmp, o_win, bg_r,
                                B=B, T=T)


def _kv_project(x, w_in):
    outs = [_matmul_heads(x, w_in, first=EVEN_KV_OFF + j * NSA_KV_W) for j in range(6)]
    return [o[0] for o in outs], [o[1] for o in outs]


def _even_prompt(hp2d, npre, mk16, mv16, w_in, w_b, bg_r, w1, b1, w2, pe, lb, g_norm, w_out, rel_bias, *, B, T):
    ya, yb = _matmul(npre, w_in, tn=W_TILE_N, cols=(0, EVEN_A_N)), _matmul(npre, w_b)
    kv32, kv16 = _kv_project(npre, w_in)
    oa, s_new = _hgrn_call(ya, jnp.zeros((B, HG_HEADS, HG_DK, HG_DV), F32), lb, g_norm, B=B, T=T, L=CHUNK, valid=CHUNK)
    kcmp = _compress_call(kv16[0], w1[0], b1[0], w2[0], pe[0], B=B, T=T)
    vcmp = _compress_call(kv16[1], w1[1], b1[1], w2[1], pe[1], B=B, T=T)
    ob = _nsa_prompt_call(ya, yb, kv16[2:], kcmp, vcmp, bg_r, *_prompt_bias_tables(rel_bias, T), B=B, T=T)
    om = _mem_call(yb, EVEN_B["qm"], mk16, mv16, B=B, T=T)
    h_new = _outproj([oa, ob, om], w_out, hp2d)
    wb = min(WINDOW, T)
    rows = [r.reshape(B, T, NSA_KVH, NSA_HD) for r in kv32]
    return h_new, (rows[0], rows[1], rows[2], rows[3], rows[4][:, -wb:], rows[5][:, -wb:], s_new)


def _even_sample(hs2d, nsam, mk_s, mv_s, page_table, pk_cmp, pv_cmp, pk_sel, pv_sel, wk, wv, s0,
                 w_in, w_b, bg_r, w1, b1, w2, pe, lb, g_norm, w_out, rel_bias, *, B, T):
    tp = SAMPLE_PAD_T
    ya, yb = _matmul(nsam, w_in, tn=W_TILE_N, cols=(0, EVEN_A_N)), _matmul(nsam, w_b)
    kv32, kv16 = _kv_project(nsam, w_in)
    oa, s_new = _hgrn_call(ya, s0, lb, g_norm, B=B, T=tp, L=tp, valid=T)
    ob = _nsa_sample(ya, yb, kv16, page_table, pk_cmp, pv_cmp, pk_sel, pv_sel, wk, wv, bg_r, w1, b1, w2, pe, rel_bias,
                     B=B, T=T)
    om = _mem_call(yb, EVEN_B["qm"], mk_s.reshape(B * N_MEM, MEM_W), mv_s.reshape(B * N_MEM, MEM_W), B=B, T=tp)
    rows = [r.reshape(B, tp, NSA_KVH, NSA_HD)[:, :T] for r in kv32]
    wb = wk.shape[1]
    win_k = jnp.concatenate([wk, rows[4]], axis=1)[:, -wb:]
    win_v = jnp.concatenate([wv, rows[5]], axis=1)[:, -wb:]
    return _outproj([oa, ob, om], w_out, hs2d), (rows[0], rows[1], rows[2], rows[3], win_k, win_v, s_new)


def _odd_mix(h2d, hn, k2d, v2d, c0, n0, m0, w_in, w_b, bif_r, g_norm, w_out, *, B, T, L, valid):
    ya, yb = _matmul(hn, w_in, tn=W_TILE_N, cols=(0, ODD_A_N)), _matmul(hn, w_b)
    h, c_new, n_new, m_new = _mlstm_call(ya, yb, c0, n0, m0, bif_r, g_norm, B=B, T=T, L=L, valid=valid)
    om = _mem_call(yb, ODD_B["qm"], k2d, v2d, B=B, T=T)
    return _outproj([h, om], w_out, h2d), (c_new, n_new, m_new)


def _stack(lst, i):
    return jnp.stack([t[i] for t in lst])


def kernel(x_prompt, x_sample, cache_mem_k, cache_mem_v, cache_cmp_k, cache_cmp_v, cache_sel_k, cache_sel_v,
           cache_win_k, cache_win_v, state_hgrn, state_mlstm_c, state_mlstm_n, state_mlstm_m, page_table,
           mem_prompt, norm_w, mem_norm_w, final_norm_w, rel_bias, w_mem_kv, w_in_even, b_nsa_gate,
           w_cmp1, b_cmp1, w_cmp2, pe_cmp, hgrn_lb_logits, hgrn_norm_w, w_out_even, w_in_odd, b_mlstm_if,
           mlstm_norm_w, w_out_odd):
    bp, tp = x_prompt.shape[:2]
    bs, ts = x_sample.shape[:2]
    tsp = SAMPLE_PAD_T
    lbs = jnp.cumsum(jax.nn.softmax(hgrn_lb_logits.astype(F32), axis=0), axis=0)
    hp = x_prompt.reshape(bp * tp, D_MODEL)
    hs = jnp.pad(x_sample, ((0, 0), (0, tsp - ts), (0, 0))).reshape(bs * tsp, D_MODEL)
    mem2d = mem_prompt.reshape(bp * N_MEM, D_MODEL)
    mem_new, even_p, even_s, odd_p, odd_s = [], [], [], [], []
    for l in range(DEPTH):
        npre = _rmsnorm_rows(hp, norm_w[l], BF16)
        nsam = _rmsnorm_rows(hs, norm_w[l], BF16)
        nmem = _rmsnorm_rows(mem2d, mem_norm_w[l], BF16)
        mk32, mk16 = _matmul_heads(nmem, w_mem_kv[l], first=0)
        mv32, mv16 = _matmul_heads(nmem, w_mem_kv[l], first=MEM_W)
        mem_new.append((mk32.reshape(bp, N_MEM, MEM_HEADS, MEM_HD), mv32.reshape(bp, N_MEM, MEM_HEADS, MEM_HD)))
        mk_s, mv_s = cache_mem_k[l], cache_mem_v[l]
        if l % 2 == 0:
            e = l // 2
            w_in, w_b = w_in_even[e], _tail_even(w_in_even[e])
            w_out = w_out_even[e].astype(BF16)
            bg_r = _gate_bias_even(b_nsa_gate[e])
            cmpw = (w_cmp1[e].reshape(2, CMP_BLOCK, NSA_HD, NSA_HD), b_cmp1[e], w_cmp2[e], pe_cmp[e])
            hp, st_p = _even_prompt(hp, npre, mk16, mv16, w_in, w_b, bg_r, *cmpw, lbs[l], hgrn_norm_w[e], w_out,
                                    rel_bias, B=bp, T=tp)
            hs, st_s = _even_sample(hs, nsam, mk_s, mv_s, page_table, cache_cmp_k[e], cache_cmp_v[e], cache_sel_k[e],
                                    cache_sel_v[e], cache_win_k[e], cache_win_v[e], state_hgrn[e], w_in, w_b, bg_r,
                                    *cmpw, lbs[l], hgrn_norm_w[e], w_out, rel_bias, B=bs, T=ts)
            even_p.append(st_p)
            even_s.append(st_s)
        else:
            o = l // 2
            w_in, w_b = w_in_odd[o], _tail_odd(w_in_odd[o])
            w_out = w_out_odd[o].astype(BF16)
            bif_r = _gate_bias_odd(b_mlstm_if[o])
            hp, st_p = _odd_mix(hp, npre, mk16, mv16, jnp.zeros((bp, ML_HEADS, ML_DV, ML_DK), F32),
                                jnp.zeros((bp, ML_HEADS, ML_DK), F32), jnp.zeros((bp, ML_HEADS), F32),
                                w_in, w_b, bif_r, mlstm_norm_w[o], w_out, B=bp, T=tp, L=ML_CHUNK, valid=ML_CHUNK)
            hs, st_s = _odd_mix(hs, nsam, mk_s.reshape(bs * N_MEM, MEM_W), mv_s.reshape(bs * N_MEM, MEM_W),
                                state_mlstm_c[o], state_mlstm_n[o], state_mlstm_m[o],
                                w_in, w_b, bif_r, mlstm_norm_w[o], w_out, B=bs, T=tsp, L=tsp, valid=ts)
            odd_p.append(st_p)
            odd_s.append(st_s)
    y_prompt = _rmsnorm_rows(hp, final_norm_w, F32).reshape(bp, tp, D_MODEL)
    y_sample = _rmsnorm_rows(hs, final_norm_w, F32).reshape(bs, tsp, D_MODEL)[:, :ts]
    return (y_prompt, y_sample,
            _stack(mem_new, 0), _stack(mem_new, 1),
            _stack(even_p, 0), _stack(even_p, 1), _stack(even_p, 2), _stack(even_p, 3),
            _stack(even_p, 4), _stack(even_p, 5), _stack(even_p, 6),
            _stack(odd_p, 0), _stack(odd_p, 1), _stack(odd_p, 2),
            _stack(even_s, 0), _stack(even_s, 1), _stack(even_s, 2), _stack(even_s, 3),
            _stack(even_s, 4), _stack(even_s, 5), _stack(even_s, 6),
            _stack(odd_s, 0), _stack(odd_s, 1), _stack(odd_s, 2))
```

```python
import functools
import math

import jax
import jax.numpy as jnp
import numpy as np
from jax import lax
from jax.experimental import pallas as pl
from jax.experimental.pallas import tpu as pltpu

D_MODEL = 4096
DEPTH = 2
PAST_LEN = 16384
PAGE_SIZE = 128
N_MEM = 256
EPS = 1e-6
CHUNK = 64

HG_DK = 128
HG_DV = 128
HG_HEADS = D_MODEL // 2 // HG_DV
HG_W = HG_HEADS * HG_DV

NSA_HD = 128
NSA_HEADS = D_MODEL // 2 // NSA_HD
NSA_KVH = 4
NSA_G = NSA_HEADS // NSA_KVH
NSA_W = NSA_HEADS * NSA_HD
NSA_KV_W = NSA_KVH * NSA_HD
CMP_BLOCK = 32
CMP_STRIDE = 16
SEL_BLOCK = 64
SEL_SHIFT = SEL_BLOCK.bit_length() - 1
N_SEL = 16
WINDOW = 512
Q_BLOCK = 128

ML_HEADS = D_MODEL // 512
ML_DK = D_MODEL // 2 // ML_HEADS
ML_DV = D_MODEL // ML_HEADS
ML_QK_W = ML_HEADS * ML_DK
ML_V_W = ML_HEADS * ML_DV

MEM_HEADS = 4
MEM_HD = 128
MEM_W = MEM_HEADS * MEM_HD

REL_BUCKETS = 32
REL_MAX_DIST = 128

F32 = jnp.float32
BF16 = jnp.bfloat16
LANES = 128
NEG_INF = float("-inf")
TINY = float(np.finfo(np.float32).tiny)
EXP_CLAMP = 80.0
VMEM_LIMIT = 56 * 1024 * 1024

HG_HB = 8
ML_HB = 2
ML_CHUNK = 256
W_TILE_N = 512
HG_SUB = 16
SAMPLE_PAD_T = 16

MM_TILE_N = 1024
EVEN_A = {"qa": 0, "fa": HG_W, "ia": 2 * HG_W, "za": 3 * HG_W, "qb": 4 * HG_W}
EVEN_A_N = 4 * HG_W + NSA_W
EVEN_B = {"zb": 0, "qm": NSA_W, "gb": NSA_W + MEM_W}
EVEN_B_N = -(-(NSA_W + MEM_W + LANES) // MM_TILE_N) * MM_TILE_N
EVEN_KV_OFF = EVEN_A_N
ODD_A = {"q": 0, "k": ML_QK_W, "v": 2 * ML_QK_W, "og": 2 * ML_QK_W + ML_V_W}
ODD_A_N = 2 * ML_QK_W + 2 * ML_V_W
ODD_B = {"z": 0, "qm": ML_V_W, "gates": ML_V_W + MEM_W}
ODD_B_N = -(-(ML_V_W + MEM_W + LANES) // MM_TILE_N) * MM_TILE_N


def _dot(a, b):
    return jnp.dot(a, b, preferred_element_type=F32)


def _dot_nt(a, b):
    return lax.dot_general(a, b, (((1,), (1,)), ((), ())), preferred_element_type=F32)


def _dot_tn(a, b):
    return lax.dot_general(a, b, (((0,), (0,)), ((), ())), preferred_element_type=F32)


def _iota2(shape, dim):
    return lax.broadcasted_iota(jnp.int32, shape, dim)


def _cumsum_rows(x, tri_b):
    hi = x.astype(BF16)
    r1 = x - hi.astype(F32)
    mid = r1.astype(BF16)
    lo = (r1 - mid.astype(F32)).astype(BF16)
    return _dot(tri_b, hi) + _dot(tri_b, mid) + _dot(tri_b, lo)


def _row_to_col(row, n):
    eye = _iota2((n, n), 0) == _iota2((n, n), 1)
    return jnp.sum(jnp.where(eye, row, 0.0), axis=1, keepdims=True)


def _col_to_row(col, n):
    eye = _iota2((n, n), 0) == _iota2((n, n), 1)
    return jnp.sum(jnp.where(eye, col, 0.0), axis=0, keepdims=True)


def _lane_col(x, idx):
    return jnp.sum(jnp.where(_iota2(x.shape, 1) == idx, x, 0.0), axis=1, keepdims=True)


def _silu(x):
    return x * jax.nn.sigmoid(x)


def _params(sem):
    return pltpu.CompilerParams(dimension_semantics=sem, vmem_limit_bytes=VMEM_LIMIT)


def _rmsnorm_body(x_ref, w_ref, o_ref):
    x = x_ref[...].astype(F32)
    y = x * lax.rsqrt(jnp.mean(x * x, axis=-1, keepdims=True) + EPS)
    o_ref[...] = (y * w_ref[...].astype(F32)).astype(o_ref.dtype)


def _rmsnorm_rows(x2d, w, out_dtype, tm=256):
    m, d = x2d.shape
    tm = min(tm, m)
    return pl.pallas_call(
        _rmsnorm_body,
        grid=(m // tm,),
        in_specs=[pl.BlockSpec((tm, d), lambda i: (i, 0)), pl.BlockSpec((1, d), lambda i: (0, 0))],
        out_specs=pl.BlockSpec((tm, d), lambda i: (i, 0)),
        out_shape=jax.ShapeDtypeStruct((m, d), out_dtype),
        compiler_params=_params(("parallel",)),
        name="rmsnorm",
    )(x2d, w.reshape(1, d))


def _matmul_nt_body(a_ref, bt_ref, o_ref):
    o_ref[...] = _dot_nt(a_ref[...], bt_ref[...].astype(BF16))


def _matmul_nt(a, bt, tm=1024, tn=MM_TILE_N, rows=None):
    m, k = a.shape
    first, n = rows or (0, bt.shape[0])
    tm, tn = min(tm, m), min(tn, n)
    assert m % tm == 0 and n % tn == 0 and first % tn == 0, (a.shape, bt.shape, rows)
    j0 = first // tn
    return pl.pallas_call(
        _matmul_nt_body,
        grid=(m // tm, n // tn),
        in_specs=[pl.BlockSpec((tm, k), lambda i, j: (i, 0)), pl.BlockSpec((tn, k), lambda i, j: (j0 + j, 0))],
        out_specs=pl.BlockSpec((tm, tn), lambda i, j: (i, j)),
        out_shape=jax.ShapeDtypeStruct((m, n), F32),
        compiler_params=_params(("parallel", "parallel")),
        name="matmul",
    )(a, bt)


def _tail_body(lo_ref, hi_ref, gate_ref, o_ref, *, shift, n_main):
    i = pl.program_id(0)
    main = jnp.concatenate([lo_ref[shift:, :], hi_ref[:shift, :]], axis=0)
    gates = jnp.where(_iota2((LANES, 1), 0) < shift, gate_ref[...], 0.0)
    o_ref[...] = jnp.where(i < n_main, main, jnp.where(i == n_main, gates, 0.0)).astype(o_ref.dtype)


def _tail_relayout(wt, first, shift, main, out_rows):
    n, k = wt.shape
    assert first % LANES == 0 and main % LANES == 0 and out_rows % LANES == 0 and shift % 8 == 0 and shift < LANES
    assert first + shift + main == n
    c0, n_main = first // LANES, main // LANES
    return pl.pallas_call(
        functools.partial(_tail_body, shift=shift, n_main=n_main),
        grid=(out_rows // LANES,),
        in_specs=[pl.BlockSpec((LANES, k), lambda i: (c0 + jnp.minimum(i, n_main - 1), 0)),
                  pl.BlockSpec((LANES, k), lambda i: (c0 + jnp.minimum(i, n_main - 1) + 1, 0)),
                  pl.BlockSpec((LANES, k), lambda i: (c0, 0))],
        out_specs=pl.BlockSpec((LANES, k), lambda i: (i, 0)),
        out_shape=jax.ShapeDtypeStruct((out_rows, k), BF16),
        compiler_params=_params(("parallel",)),
        name="tail_relayout",
    )(wt, wt, wt)


def _matmul_heads_body(a_ref, b_ref, o32_ref, o16_ref, *, transposed):
    b = b_ref[...].astype(BF16)
    acc = _dot_nt(a_ref[...], b) if transposed else _dot(a_ref[...], b)
    for h in range(MEM_HEADS):
        o32_ref[:, h, :] = acc[:, h * LANES:(h + 1) * LANES]
    o16_ref[...] = acc.astype(BF16)


def _matmul_heads(a, b, first=0, transposed=False, tm=1024):
    m, k = a.shape
    n = MEM_HEADS * LANES
    tm = min(tm, m)
    assert m % tm == 0 and first % n == 0, (a.shape, b.shape, first)
    j0 = first // n
    b_spec = pl.BlockSpec((n, k), lambda i: (j0, 0)) if transposed else pl.BlockSpec((k, n), lambda i: (0, j0))
    return pl.pallas_call(
        functools.partial(_matmul_heads_body, transposed=transposed),
        grid=(m // tm,),
        in_specs=[pl.BlockSpec((tm, k), lambda i: (i, 0)), b_spec],
        out_specs=[pl.BlockSpec((tm, MEM_HEADS, LANES), lambda i: (i, 0, 0)), pl.BlockSpec((tm, n), lambda i: (i, 0))],
        out_shape=[jax.ShapeDtypeStruct((m, MEM_HEADS, LANES), F32), jax.ShapeDtypeStruct((m, n), BF16)],
        compiler_params=_params(("parallel",)),
        name="matmul_heads",
    )(a, b)


def _outproj_body(*refs, widths):
    xs = refs[:len(widths)]
    w_ref, r_ref, o_ref = refs[len(widths):]
    acc = r_ref[...]
    off = 0
    for x_ref, w in zip(xs, widths):
        acc = acc + _dot(x_ref[...], w_ref[off:off + w, :])
        off += w
    o_ref[...] = acc


def _outproj(xs, w_bf16, resid, tm=1024, tn=512):
    m = resid.shape[0]
    n = w_bf16.shape[1]
    widths = tuple(x.shape[1] for x in xs)
    assert sum(widths) == w_bf16.shape[0]
    tm = min(tm, m)
    in_specs = [pl.BlockSpec((tm, w), lambda i, j: (i, 0)) for w in widths]
    in_specs += [pl.BlockSpec((w_bf16.shape[0], tn), lambda i, j: (0, j)), pl.BlockSpec((tm, tn), lambda i, j: (i, j))]
    return pl.pallas_call(
        functools.partial(_outproj_body, widths=widths),
        grid=(m // tm, n // tn),
        in_specs=in_specs,
        out_specs=pl.BlockSpec((tm, tn), lambda i, j: (i, j)),
        out_shape=jax.ShapeDtypeStruct((m, n), F32),
        compiler_params=_params(("parallel", "parallel")),
        name="outproj",
    )(*xs, w_bf16, resid)


def _hgrn_body(qa_ref, fa_ref, ia_ref, za_ref, lb_ref, gn_ref, s0_ref, o_ref, s_out, s_scr, *, L, valid):
    c = pl.program_id(2)

    @pl.when(c == 0)
    def _():
        s_scr[...] = s0_ref[...]

    lb = lb_ref[...]
    sig = jax.nn.sigmoid(fa_ref[...])
    logf = jnp.log(lb + (1.0 - lb) * sig)
    kk = (1.0 - lb) * (1.0 - sig)
    if valid < L:
        live = _iota2((L, 1), 0) < valid
        logf = jnp.where(live, logf, 0.0)
        kk = jnp.where(live, kk, 0.0)
    tri_b = (_iota2((L, L), 0) >= _iota2((L, L), 1)).astype(BF16)
    bc = _cumsum_rows(logf, tri_b)
    q = _silu(qa_ref[...])
    gate = _silu(za_ref[...])
    v = ia_ref[...]
    gn = gn_ref[...]
    nsub = L // HG_SUB
    rr = _iota2((L, nsub * L), 0)
    cc = _iota2((L, nsub * L), 1)
    keep = ((jnp.right_shift(cc, L.bit_length() - 1) == jnp.right_shift(rr, HG_SUB.bit_length() - 1))
            & (jnp.bitwise_and(cc, L - 1) <= rr))
    for j in range(HG_HB):
        sl = slice(j * HG_DK, (j + 1) * HG_DK)
        bj, qj, kj = bc[:, sl], q[:, sl], kk[:, sl]
        vb = v[:, sl].astype(BF16)
        s_prev = s_scr[j]
        inter = _dot((qj * jnp.exp(bj)).astype(BF16), s_prev.astype(BF16))
        mids = [bj[i * HG_SUB + HG_SUB // 2:i * HG_SUB + HG_SUB // 2 + 1, :] for i in range(nsub)]
        mid_rows = jnp.concatenate([jnp.broadcast_to(m, (HG_SUB, HG_DK)) for m in mids], axis=0)
        q_dec = qj * jnp.exp(jnp.minimum(bj - mid_rows, EXP_CLAMP))
        k_dec = jnp.concatenate([kj * jnp.exp(jnp.minimum(m - bj, EXP_CLAMP)) for m in mids], axis=0)
        att = jnp.where(keep, _dot_nt(q_dec.astype(BF16), k_dec.astype(BF16)), 0.0)
        o = inter + _dot(att.astype(BF16), jnp.concatenate([vb] * nsub, axis=0))
        o_n = o * lax.rsqrt(jnp.mean(o * o, axis=-1, keepdims=True) + EPS) * gn
        o_ref[:, sl] = (o_n * gate[:, sl]).astype(o_ref.dtype)
        bl = bj[L - 1:L, :]
        kd = kj * jnp.exp(bl - bj)
        s_scr[j] = _row_to_col(jnp.exp(bl), HG_DK) * s_prev + _dot_tn(kd.astype(BF16), vb)

    @pl.when(c == pl.num_programs(2) - 1)
    def _():
        s_out[...] = s_scr[...]


def _hgrn_call(y, s0, lb, gn, *, B, T, L, valid):
    nc = T // L
    w = HG_HB * HG_DK

    def col(name):
        blk = EVEN_A[name] // w
        return pl.BlockSpec((L, w), lambda b, hg, c: (b * nc + c, blk + hg))

    state_spec = pl.BlockSpec((None, HG_HB, HG_DK, HG_DV), lambda b, hg, c: (b, hg, 0, 0))
    return pl.pallas_call(
        functools.partial(_hgrn_body, L=L, valid=valid),
        grid=(B, HG_HEADS // HG_HB, nc),
        in_specs=[col("qa"), col("fa"), col("ia"), col("za"),
                  pl.BlockSpec((1, w), lambda b, hg, c: (0, hg)),
                  pl.BlockSpec((1, HG_DV), lambda b, hg, c: (0, 0)),
                  state_spec],
        out_specs=[pl.BlockSpec((L, w), lambda b, hg, c: (b * nc + c, hg)), state_spec],
        out_shape=[jax.ShapeDtypeStruct((B * T, HG_W), BF16),
                   jax.ShapeDtypeStruct((B, HG_HEADS, HG_DK, HG_DV), F32)],
        scratch_shapes=[pltpu.VMEM((HG_HB, HG_DK, HG_DV), F32)],
        compiler_params=_params(("arbitrary", "arbitrary", "arbitrary")),
        name="hgrn2",
    )(y, y, y, y, lb.reshape(1, HG_W), gn.reshape(1, HG_DV), s0)


def _mlstm_body(q_ref, k_ref, v_ref, og_ref, z_ref, g_ref, bif_ref, gn_ref, c0_ref, n0_ref, m0_ref,
                h_ref, c_out, n_out, m_out, c_scr, n_scr, m_scr, *, L, valid):
    c = pl.program_id(2)

    @pl.when(c == 0)
    def _():
        c_scr[...] = c0_ref[...]
        n_scr[...] = n0_ref[...]
        m_scr[...] = m0_ref[...]

    gates = g_ref[...] + bif_ref[...]
    log_i = gates
    log_f = jnp.minimum(gates, 0.0) - jnp.log(1.0 + jnp.exp(-jnp.abs(gates)))
    if valid < L:
        live = _iota2((L, 1), 0) < valid
        log_i = jnp.where(live, log_i, -1e30)
        log_f = jnp.where(live, log_f, 0.0)
    tri = _iota2((L, L), 0) >= _iota2((L, L), 1)
    bcs = _cumsum_rows(log_f, tri.astype(BF16))
    for j in range(ML_HB):
        head = pl.program_id(1) * ML_HB + j
        b_col = _lane_col(bcs, ML_HEADS + head)
        i_col = _lane_col(log_i, head)
        b_row = _col_to_row(b_col, L)
        i_row = _col_to_row(i_col, L)
        m_prev = m_scr[:, j:j + 1]
        dmat = jnp.where(tri, b_col - b_row + i_row, NEG_INF)
        inter = b_col + m_prev
        mt = jnp.maximum(inter, jnp.max(dmat, axis=1, keepdims=True))
        w_in = jnp.exp(dmat - mt)
        w_x = jnp.exp(inter - mt)
        qj = q_ref[:, j * ML_DK:(j + 1) * ML_DK]
        kj = k_ref[:, j * ML_DK:(j + 1) * ML_DK] * (ML_DK ** -0.5)
        vj = v_ref[:, j * ML_DV:(j + 1) * ML_DV]
        qb, kb = qj.astype(BF16), kj.astype(BF16)
        sw = _dot_nt(qb, kb) * w_in
        c_prev = c_scr[j]
        n_prev = n_scr[:, j * ML_DK:(j + 1) * ML_DK]
        num = w_x * _dot_nt(qb, c_prev.astype(BF16)) + _dot(sw.astype(BF16), vj.astype(BF16))
        den = w_x * jnp.sum(qj * n_prev, axis=1, keepdims=True) + jnp.sum(sw, axis=1, keepdims=True)
        h = num / jnp.maximum(jnp.abs(den), jnp.exp(-mt))
        m_last = mt[L - 1:L, :]
        b_last = b_col[L - 1:L, :]
        w_end = jnp.exp(b_last - b_col + i_col - m_last)
        d_c = jnp.exp(b_last + m_prev - m_last)
        c_scr[j] = d_c * c_prev + _dot_tn((w_end * vj).astype(BF16), kb)
        n_scr[:, j * ML_DK:(j + 1) * ML_DK] = d_c * n_prev + jnp.sum(w_end * kj, axis=0, keepdims=True)
        m_scr[:, j:j + 1] = m_last
        sv = slice(j * ML_DV, (j + 1) * ML_DV)
        h_n = h * lax.rsqrt(jnp.mean(h * h, axis=-1, keepdims=True) + EPS) * gn_ref[:, sv]
        h_ref[:, sv] = (h_n * jax.nn.sigmoid(og_ref[:, sv]) * _silu(z_ref[:, sv])).astype(h_ref.dtype)

    @pl.when(c == pl.num_programs(2) - 1)
    def _():
        c_out[...] = c_scr[...]
        n_out[...] = n_scr[...]
        m_out[...] = m_scr[...]


def _mlstm_call(ya, yb, c0, n0, m0, bif_r, gn, *, B, T, L, valid):
    nc = T // L
    ng = ML_HEADS // ML_HB
    wk, wv = ML_HB * ML_DK, ML_HB * ML_DV

    def col(name, w):
        blk = (ODD_A[name] if name in ODD_A else ODD_B[name]) // w
        return pl.BlockSpec((L, w), lambda b, hg, c: (b * nc + c, blk + hg))

    c_spec = pl.BlockSpec((None, ML_HB, ML_DV, ML_DK), lambda b, hg, c: (b, hg, 0, 0))
    n_spec = pl.BlockSpec((None, 1, wk), lambda b, hg, c: (b, 0, hg))
    m_spec = pl.BlockSpec((None, None, 1, LANES), lambda b, hg, c: (b, hg, 0, 0))
    m0_r = jnp.pad(m0.reshape(B, ng, 1, ML_HB), ((0, 0), (0, 0), (0, 0), (0, LANES - ML_HB)))
    h, c_new, n_new, m_new = pl.pallas_call(
        functools.partial(_mlstm_body, L=L, valid=valid),
        grid=(B, ng, nc),
        in_specs=[col("q", wk), col("k", wk), col("v", wv), col("og", wv), col("z", wv),
                  pl.BlockSpec((L, LANES), lambda b, hg, c: (b * nc + c, ODD_B["gates"] // LANES)),
                  pl.BlockSpec((1, LANES), lambda b, hg, c: (0, 0)),
                  pl.BlockSpec((1, wv), lambda b, hg, c: (0, hg)),
                  c_spec, n_spec, m_spec],
        out_specs=[pl.BlockSpec((L, wv), lambda b, hg, c: (b * nc + c, hg)), c_spec, n_spec, m_spec],
        out_shape=[jax.ShapeDtypeStruct((B * T, ML_V_W), BF16),
                   jax.ShapeDtypeStruct((B, ML_HEADS, ML_DV, ML_DK), F32),
                   jax.ShapeDtypeStruct((B, 1, ML_QK_W), F32),
                   jax.ShapeDtypeStruct((B, ng, 1, LANES), F32)],
        scratch_shapes=[pltpu.VMEM((ML_HB, ML_DV, ML_DK), F32), pltpu.VMEM((1, wk), F32), pltpu.VMEM((1, LANES), F32)],
        compiler_params=_params(("arbitrary", "arbitrary", "arbitrary")),
        name="mlstm",
    )(ya, ya, ya, ya, yb, yb, bif_r, gn.reshape(1, ML_V_W), c0, n0.reshape(B, 1, ML_QK_W), m0_r)
    return h, c_new, n_new.reshape(B, ML_HEADS, ML_DK), m_new[:, :, 0, :ML_HB].reshape(B, ML_HEADS)


def _mem_body(q_ref, k_ref, v_ref, o_ref):
    q = q_ref[...] * (MEM_HD ** -0.5)
    for h in range(MEM_HEADS):
        sl = slice(h * MEM_HD, (h + 1) * MEM_HD)
        s = _dot_nt(q[:, sl].astype(BF16), k_ref[:, sl].astype(BF16))
        p = jnp.exp(s - jnp.max(s, axis=-1, keepdims=True))
        o = _dot(p.astype(BF16), v_ref[:, sl].astype(BF16)) / jnp.sum(p, axis=-1, keepdims=True)
        o_ref[:, sl] = o.astype(o_ref.dtype)


def _mem_call(y, q_off, k2d, v2d, *, B, T, tq=256):
    tq = min(tq, T)
    nq = T // tq
    qb = q_off // MEM_W
    return pl.pallas_call(
        _mem_body,
        grid=(B, nq),
        in_specs=[pl.BlockSpec((tq, MEM_W), lambda b, i: (b * nq + i, qb)),
                  pl.BlockSpec((N_MEM, MEM_W), lambda b, i: (b, 0)),
                  pl.BlockSpec((N_MEM, MEM_W), lambda b, i: (b, 0))],
        out_specs=pl.BlockSpec((tq, MEM_W), lambda b, i: (b * nq + i, 0)),
        out_shape=jax.ShapeDtypeStruct((B * T, MEM_W), BF16),
        compiler_params=_params(("parallel", "parallel")),
        name="mem_attn",
    )(y, k2d, v2d)


def _gelu_tanh(x):
    return 0.5 * x * (1.0 + jnp.tanh(math.sqrt(2.0 / math.pi) * (x + 0.044715 * (x * x * x))))


def _compress_body(x_ref, w1_ref, b1_ref, w2_ref, pe_ref, o_ref, x32, *, nch):
    x32[...] = x_ref[...].astype(F32)
    a = jnp.zeros((nch, NSA_HD), F32)
    b = jnp.zeros((nch, NSA_HD), F32)
    for s in range(CMP_STRIDE):
        r = x32[pl.ds(s, nch, stride=CMP_STRIDE), :]
        a = a + _dot((r + pe_ref[s:s + 1, :]).astype(BF16), w1_ref[s])
        b = b + _dot((r + pe_ref[CMP_STRIDE + s:CMP_STRIDE + s + 1, :]).astype(BF16), w1_ref[CMP_STRIDE + s])
    h = a + pltpu.roll(b, nch - 1, 0) + b1_ref[...]
    o_ref[...] = _dot(_gelu_tanh(h).astype(BF16), w2_ref[...])


def _compress_call(x16, w1, b1, w2, pe, *, B, T):
    nch = T // CMP_STRIDE
    return pl.pallas_call(
        functools.partial(_compress_body, nch=nch),
        grid=(B, NSA_KVH),
        in_specs=[pl.BlockSpec((T, NSA_HD), lambda b, h: (b, h)),
                  pl.BlockSpec((CMP_BLOCK, NSA_HD, NSA_HD), lambda b, h: (0, 0, 0)),
                  pl.BlockSpec((1, NSA_HD), lambda b, h: (0, 0)),
                  pl.BlockSpec((NSA_HD, NSA_HD), lambda b, h: (0, 0)),
                  pl.BlockSpec((CMP_BLOCK, NSA_HD), lambda b, h: (0, 0))],
        out_specs=pl.BlockSpec((None, None, nch, NSA_HD), lambda b, h: (b, h, 0, 0)),
        out_shape=jax.ShapeDtypeStruct((B, NSA_KVH, nch, NSA_HD), F32),
        scratch_shapes=[pltpu.VMEM((T, NSA_HD), F32)],
        compiler_params=_params(("parallel", "parallel")),
        name="nsa_compress",
    )(x16, w1.astype(BF16), b1.reshape(1, NSA_HD), w2.astype(BF16), pe)


def _softmax_rows(s):
    m = jnp.max(s, axis=-1, keepdims=True)
    m = jnp.where(m == NEG_INF, 0.0, m)
    p = jnp.exp(s - m)
    return p, jnp.sum(p, axis=-1, keepdims=True)


def _slc_scores(psum, width, n_slc):
    ncmp = psum.shape[1]
    d = _iota2((ncmp, width), 0) - (SEL_BLOCK // CMP_STRIDE) * _iota2((ncmp, width), 1)
    wgt = jnp.where((d == -1) | (d == 3), 1.0, jnp.where((d >= 0) & (d <= 2), 2.0, 0.0))
    wgt = jnp.where(_iota2((ncmp, width), 1) < n_slc, wgt, 0.0).astype(BF16)
    p_hi = psum.astype(BF16)
    p_lo = (psum - p_hi.astype(F32)).astype(BF16)
    return _dot(p_hi, wgt) + _dot(p_lo, wgt)


def _top_blocks(slc, cur, n_pick):
    rows, width = slc.shape
    blk = _iota2((rows, width), 1)
    forced = (blk == 0) | (blk == cur) | (blk == cur - 1)
    score = jnp.where(forced, jnp.inf, slc)
    score = jnp.where(blk > cur, NEG_INF, score)
    blk_f = blk.astype(F32)
    lane = _iota2((rows, LANES), 1)
    sel = jnp.zeros((rows, width), F32)
    picks = jnp.zeros((rows, LANES), F32)
    for i in range(n_pick):
        mx = jnp.max(score, axis=-1, keepdims=True)
        first = jnp.min(jnp.where(score == mx, blk_f, float(width)), axis=-1, keepdims=True)
        pick = blk_f == first
        sel = jnp.where(pick, 1.0, sel)
        picks = jnp.where(lane == i, first, picks)
        score = jnp.where(pick, NEG_INF, score)
    return sel, picks


def _member_by_rank(psum, tpos_row, n_slc, n_pick):
    nq, ncmp = psum.shape
    nb = -(-n_slc // 8) * 8
    d = _iota2((nb, ncmp), 1) - (SEL_BLOCK // CMP_STRIDE) * _iota2((nb, ncmp), 0)
    wgt = jnp.where((d == -1) | (d == 3), 1.0, jnp.where((d >= 0) & (d <= 2), 2.0, 0.0))
    wgt = jnp.where(_iota2((nb, ncmp), 0) < n_slc, wgt, 0.0).astype(BF16)
    p_hi = psum.astype(BF16)
    p_lo = (psum - p_hi.astype(F32)).astype(BF16)
    slc = _dot_nt(wgt, p_hi) + _dot_nt(wgt, p_lo)
    blk = _iota2((nb, nq), 0)
    cur = jnp.right_shift(tpos_row, SEL_SHIFT)
    forced = (blk == 0) | (blk == cur) | (blk == cur - 1)
    score = jnp.where(forced, jnp.inf, slc)
    score = jnp.where(blk > cur, NEG_INF, score)
    ahead = jnp.zeros((nb, nq), F32)
    for i in range(n_slc):
        s_i = score[i:i + 1, :]
        ahead = ahead + jnp.where((s_i > score) | ((s_i == score) & (blk > i)), 1.0, 0.0)
    return jnp.where((ahead < n_pick) & (blk <= cur), 1.0, 0.0)


def _nsa_prompt_body(q_ref, zb_ref, gb_ref, bg_ref, ks_ref, vs_ref, kw_ref, vw_ref, kc_ref, vc_ref,
                     bc_ref, bs_ref, bw_ref, o_ref, ksp, vsp, kwp, vwp, osel, *, T):
    qi = pl.program_id(2)
    tq = Q_BLOCK
    front = T - tq
    wlen = WINDOW + tq
    n_slc = T // SEL_BLOCK

    @pl.when(qi == 0)
    def _():
        ksp[0:front, :] = jnp.zeros((front, NSA_HD), BF16)
        vsp[0:front, :] = jnp.zeros((front, NSA_HD), BF16)
        ksp[front:front + T, :] = ks_ref[...].astype(BF16)
        vsp[front:front + T, :] = vs_ref[...].astype(BF16)
        kwp[0:WINDOW, :] = jnp.zeros((WINDOW, NSA_HD), BF16)
        vwp[0:WINDOW, :] = jnp.zeros((WINDOW, NSA_HD), BF16)
        kwp[WINDOW:WINDOW + T, :] = kw_ref[...].astype(BF16)
        vwp[WINDOW:WINDOW + T, :] = vw_ref[...].astype(BF16)

    t0 = pl.multiple_of(qi * tq, tq)
    tpos = _iota2((tq, 1), 0) + t0
    q_all = q_ref[...] * (NSA_HD ** -0.5)
    qs = [q_all[:, g * NSA_HD:(g + 1) * NSA_HD].astype(BF16) for g in range(NSA_G)]

    ncmp = T // CMP_STRIDE
    vis = tpos >= _iota2((1, ncmp), 1) * CMP_STRIDE + (CMP_BLOCK - 1)
    kcb = kc_ref[...].astype(BF16)
    vcb = vc_ref[...].astype(BF16)
    psum = jnp.zeros((tq, ncmp), F32)
    o_cmp = []
    for g in range(NSA_G):
        s = jnp.where(vis, _dot_nt(qs[g], kcb) + bc_ref[g], NEG_INF)
        p, l = _softmax_rows(s)
        p = p / jnp.maximum(l, TINY)
        psum = psum + p
        o_cmp.append(_dot(p.astype(BF16), vcb))

    member_t = _member_by_rank(psum, _iota2((1, tq), 1) + t0, n_slc, min(N_SEL, n_slc)).astype(BF16)

    nb = member_t.shape[0]
    n_win = SEL_WINDOWS if T % (SEL_WINDOWS * tq) == 0 else 1
    for i in range(n_win):
        w_prev, w = T * i // n_win, T * (i + 1) // n_win

        @pl.when((qi >= w_prev // tq) & (qi < w // tq))
        def _(w=w):
            off = T - w
            col_blk = (jnp.right_shift(_iota2((nb, w), 1) + off, SEL_SHIFT)
                       + (qi * (tq // SEL_BLOCK) + (tq - T) // SEL_BLOCK))
            expand = (col_blk == _iota2((nb, w), 0)).astype(BF16)
            kpos = _iota2((1, w), 1) + (t0 + tq - w)
            allowed = (_dot_tn(member_t, expand) > 0.5) & (kpos <= tpos)
            mask_s = jnp.where(allowed, 0.0, NEG_INF)
            k_s = ksp[pl.ds(t0 + off, w), :]
            v_s = vsp[pl.ds(t0 + off, w), :]
            for g in range(NSA_G):
                p, l = _softmax_rows(_dot_nt(qs[g], k_s) + bs_ref[g, :, off:] + mask_s)
                osel[g] = _dot(p.astype(BF16), v_s) / jnp.maximum(l, TINY)

    dist = WINDOW + _iota2((tq, wlen), 0) - _iota2((tq, wlen), 1)
    in_win = (dist >= 0) & (dist < WINDOW) & (_iota2((1, wlen), 1) + (t0 - WINDOW) >= 0)
    mask_w = jnp.where(in_win, 0.0, NEG_INF)
    k_w = kwp[pl.ds(t0, wlen), :]
    v_w = vwp[pl.ds(t0, wlen), :]
    gate = jax.nn.sigmoid(gb_ref[...] + bg_ref[...])
    zb = _silu(zb_ref[...])
    for g in range(NSA_G):
        p, l = _softmax_rows(_dot_nt(qs[g], k_w) + bw_ref[g] + mask_w)
        o_win = _dot(p.astype(BF16), v_w) / jnp.maximum(l, TINY)
        head = pl.program_id(1) * NSA_G + g
        mix = (_lane_col(gate, head) * o_cmp[g] + _lane_col(gate, NSA_HEADS + head) * osel[g]
               + _lane_col(gate, 2 * NSA_HEADS + head) * o_win)
        sl = slice(g * NSA_HD, (g + 1) * NSA_HD)
        o_ref[:, sl] = (mix * zb[:, sl]).astype(o_ref.dtype)


def _nsa_prompt_call(ya, yb, kv16, kcmp, vcmp, bg_r, bias_c, bias_s, bias_w, *, B, T):
    nq = T // Q_BLOCK
    gw = NSA_G * NSA_HD
    wlen = WINDOW + Q_BLOCK
    kv_spec = pl.BlockSpec((T, NSA_HD), lambda b, h, i: (b, h))
    cmp_spec = pl.BlockSpec((None, None, T // CMP_STRIDE, NSA_HD), lambda b, h, i: (b, h, 0, 0))
    return pl.pallas_call(
        functools.partial(_nsa_prompt_body, T=T),
        grid=(B, NSA_KVH, nq),
        in_specs=[pl.BlockSpec((Q_BLOCK, gw), lambda b, h, i: (b * nq + i, EVEN_A["qb"] // gw + h)),
                  pl.BlockSpec((Q_BLOCK, gw), lambda b, h, i: (b * nq + i, EVEN_B["zb"] // gw + h)),
                  pl.BlockSpec((Q_BLOCK, LANES), lambda b, h, i: (b * nq + i, EVEN_B["gb"] // LANES)),
                  pl.BlockSpec((1, LANES), lambda b, h, i: (0, 0)),
                  kv_spec, kv_spec, kv_spec, kv_spec, cmp_spec, cmp_spec,
                  pl.BlockSpec((None, NSA_G, Q_BLOCK, T // CMP_STRIDE), lambda b, h, i: (h, 0, i, 0)),
                  pl.BlockSpec((None, NSA_G, Q_BLOCK, T), lambda b, h, i: (h, 0, 0, 0)),
                  pl.BlockSpec((None, NSA_G, Q_BLOCK, wlen), lambda b, h, i: (h, 0, 0, 0))],
        out_specs=pl.BlockSpec((Q_BLOCK, gw), lambda b, h, i: (b * nq + i, h)),
        out_shape=jax.ShapeDtypeStruct((B * T, NSA_W), BF16),
        scratch_shapes=[pltpu.VMEM((2 * T - Q_BLOCK, NSA_HD), BF16), pltpu.VMEM((2 * T - Q_BLOCK, NSA_HD), BF16),
                        pltpu.VMEM((WINDOW + T, NSA_HD), BF16), pltpu.VMEM((WINDOW + T, NSA_HD), BF16),
                        pltpu.VMEM((NSA_G, Q_BLOCK, NSA_HD), F32)],
        compiler_params=_params(("arbitrary", "arbitrary", "arbitrary")),
        name="nsa_prompt",
    )(ya, yb, yb, bg_r, *kv16, kcmp, vcmp, bias_c, bias_s, bias_w)


CMP_PAGES = 16
CHUNKS_PER_PAGE = PAGE_SIZE // CMP_STRIDE
PAGE_ROWS = PAGE_SIZE * NSA_KVH


def _pool_rows(pool):
    return pool.reshape(pool.shape[0] * PAGE_ROWS, NSA_HD)


def _cmp_pages_body(pt_ref, *refs):
    del pt_ref
    pages = refs[:CMP_PAGES]
    w_ref, pe_ref, o_ref = refs[CMP_PAGES:]
    rows = CMP_PAGES * CHUNKS_PER_PAGE
    per_head = [jnp.concatenate(
        [jnp.concatenate([pg[pl.ds(NSA_KVH * s + h, CHUNKS_PER_PAGE, stride=CMP_STRIDE * NSA_KVH), :]
                          for s in range(CMP_STRIDE)], axis=1) for pg in pages], axis=0) for h in range(NSA_KVH)]
    w = w_ref[...]
    r = _dot(jnp.concatenate(per_head, axis=0).astype(BF16), w)
    pc = _dot(pe_ref[...], w)
    r = r + jnp.concatenate([pc[0:1, :NSA_HD], pc[1:2, NSA_HD:]], axis=1)
    for h in range(NSA_KVH):
        o_ref[h] = r[h * rows:(h + 1) * rows]


def _cmp_pages_call(pool, page_table, w1, pe, *, B):
    n_pages = page_table.shape[1]
    rows = CMP_PAGES * CHUNKS_PER_PAGE
    view = _pool_rows(pool)
    w = w1.reshape(2, CMP_STRIDE, NSA_HD, NSA_HD).transpose(1, 2, 0, 3).reshape(CMP_STRIDE * NSA_HD, 2 * NSA_HD)
    pe_rows = jnp.pad(pe.reshape(2, CMP_STRIDE * NSA_HD), ((0, 6), (0, 0))).astype(BF16)

    def page_spec(i):
        return pl.BlockSpec((PAGE_ROWS, NSA_HD), lambda b, s, pt: (pt[b * n_pages + s * CMP_PAGES + i], 0))

    grid_spec = pltpu.PrefetchScalarGridSpec(
        num_scalar_prefetch=1,
        grid=(B, n_pages // CMP_PAGES),
        in_specs=[page_spec(i) for i in range(CMP_PAGES)]
        + [pl.BlockSpec((CMP_STRIDE * NSA_HD, 2 * NSA_HD), lambda b, s, pt: (0, 0)),
           pl.BlockSpec((8, CMP_STRIDE * NSA_HD), lambda b, s, pt: (0, 0))],
        out_specs=pl.BlockSpec((None, NSA_KVH, rows, 2 * NSA_HD), lambda b, s, pt: (b, 0, s, 0)),
    )
    return pl.pallas_call(
        _cmp_pages_body,
        grid_spec=grid_spec,
        out_shape=jax.ShapeDtypeStruct((B, NSA_KVH, n_pages * CHUNKS_PER_PAGE, 2 * NSA_HD), F32),
        compiler_params=_params(("arbitrary", "arbitrary")),
        name="nsa_cmp_pages",
    )(page_table.reshape(-1), *([view] * CMP_PAGES), w.astype(BF16), pe_rows)


SEL_WINDOWS = 4
SLC_LANES = 384


def _sample_q_rows(q_ref):
    q = q_ref[...] * (NSA_HD ** -0.5)
    return jnp.concatenate([q[:, g * NSA_HD:(g + 1) * NSA_HD] for g in range(NSA_G)], axis=0).astype(BF16)


def _nsa_sample_main_body(abk_ref, abv_ref, b1_ref, w2_ref, q_ref, wk_ref, wv_ref, kn_ref, vn_ref, bc_ref, bw_ref,
                          ocmp_ref, owin_ref, idx_ref, *, T, n_slc):
    tp = SAMPLE_PAD_T
    rows = NSA_G * tp
    ncmp = abk_ref.shape[0]

    def compressed(ab_ref, t):
        ab = ab_ref[...]
        h = ab[:, :NSA_HD] + pltpu.roll(ab[:, NSA_HD:], ncmp - 1, 0) + b1_ref[t]
        return _dot(_gelu_tanh(h).astype(BF16), w2_ref[t]).astype(BF16)

    kc, vc = compressed(abk_ref, 0), compressed(abv_ref, 1)
    q = _sample_q_rows(q_ref)
    step = jnp.bitwise_and(_iota2((rows, 1), 0), tp - 1)
    tpos = PAST_LEN + step
    vis = tpos >= _iota2((1, ncmp), 1) * CMP_STRIDE + (CMP_BLOCK - 1)
    p, l = _softmax_rows(jnp.where(vis, _dot_nt(q, kc) + bc_ref[...], NEG_INF))
    p = p / jnp.maximum(l, TINY)
    ocmp_ref[...] = _dot(p.astype(BF16), vc)
    psum = p[0:tp]
    for g in range(1, NSA_G):
        psum = psum + p[g * tp:(g + 1) * tp]
    cur = jnp.right_shift(PAST_LEN + _iota2((tp, 1), 0), SEL_SHIFT)
    _, picks = _top_blocks(_slc_scores(psum, SLC_LANES, n_slc), cur, N_SEL)
    idx_ref[...] = picks.astype(jnp.int32)

    wb = wk_ref.shape[0] // NSA_KVH
    wlen = bw_ref.shape[1]
    fill = jnp.zeros((wlen - wb - tp, NSA_HD), BF16)
    head = pl.program_id(1)
    k_all = jnp.concatenate([wk_ref[pl.ds(head, wb, stride=NSA_KVH), :].astype(BF16), kn_ref[...], fill], axis=0)
    v_all = jnp.concatenate([wv_ref[pl.ds(head, wb, stride=NSA_KVH), :].astype(BF16), vn_ref[...], fill], axis=0)
    col = _iota2((1, wlen), 1)
    dist = tpos - (PAST_LEN - wb + col)
    in_win = (dist >= 0) & (dist < WINDOW) & (col < wb + T)
    pw, lw = _softmax_rows(jnp.where(in_win, _dot_nt(q, k_all) + bw_ref[...], NEG_INF))
    owin_ref[...] = _dot(pw.astype(BF16), v_all) / jnp.maximum(lw, TINY)


def _nsa_sample_main_call(ya, kw16, vw16, abk, abv, b1, w2, wk, wv, bias_c, bias_w, *, B, T):
    tp = SAMPLE_PAD_T
    rows = NSA_G * tp
    gw = NSA_G * NSA_HD
    ncmp = abk.shape[2]
    wb = wk.shape[1]
    wlen = bias_w.shape[-1]
    n_slc = -(-(PAST_LEN + T) // SEL_BLOCK)
    assert n_slc <= SLC_LANES and T <= tp
    ab_spec = pl.BlockSpec((None, None, ncmp, 2 * NSA_HD), lambda b, h: (b, h, 0, 0))
    win_spec = pl.BlockSpec((wb * NSA_KVH, NSA_HD), lambda b, h: (b, 0))
    o_spec = pl.BlockSpec((None, None, rows, NSA_HD), lambda b, h: (b, h, 0, 0))
    return pl.pallas_call(
        functools.partial(_nsa_sample_main_body, T=T, n_slc=n_slc),
        grid=(B, NSA_KVH),
        in_specs=[ab_spec, ab_spec,
                  pl.BlockSpec((2, 1, NSA_HD), lambda b, h: (0, 0, 0)),
                  pl.BlockSpec((2, NSA_HD, NSA_HD), lambda b, h: (0, 0, 0)),
                  pl.BlockSpec((tp, gw), lambda b, h: (b, EVEN_A["qb"] // gw + h)),
                  win_spec, win_spec,
                  pl.BlockSpec((tp, NSA_HD), lambda b, h: (b, h)),
                  pl.BlockSpec((tp, NSA_HD), lambda b, h: (b, h)),
                  pl.BlockSpec((None, rows, ncmp), lambda b, h: (h, 0, 0)),
                  pl.BlockSpec((None, rows, wlen), lambda b, h: (h, 0, 0))],
        out_specs=[o_spec, o_spec, pl.BlockSpec((None, None, tp, LANES), lambda b, h: (b, h, 0, 0))],
        out_shape=[jax.ShapeDtypeStruct((B, NSA_KVH, rows, NSA_HD), F32),
                   jax.ShapeDtypeStruct((B, NSA_KVH, rows, NSA_HD), F32),
                   jax.ShapeDtypeStruct((B, NSA_KVH, tp, LANES), jnp.int32)],
        compiler_params=_params(("parallel", "parallel")),
        name="nsa_sample_main",
    )(abk, abv, b1.reshape(2, 1, NSA_HD), w2.astype(BF16), ya,
      wk.reshape(B * wb * NSA_KVH, NSA_HD), wv.reshape(B * wb * NSA_KVH, NSA_HD), kw16, vw16, bias_c, bias_w)


NEAR_BLOCKS = 3


def _nsa_sample_sel_body(idx_ref, pt_ref, q_ref, kn_ref, vn_ref, tbl_ref, ocmp_ref, owin_ref, gb_ref, bg_ref, zb_ref,
                         *refs, T):
    del pt_ref
    k_blocks = refs[:N_SEL]
    v_blocks = refs[N_SEL:2 * N_SEL]
    o_ref, osel = refs[2 * N_SEL:]
    tp = SAMPLE_PAD_T
    rows = NSA_G * tp
    b, h, t = pl.program_id(0), pl.program_id(1), pl.program_id(2)
    base = ((b * NSA_KVH + h) * T + t) * N_SEL
    first_new = PAST_LEN // SEL_BLOCK
    cur = jnp.right_shift(PAST_LEN + t, SEL_SHIFT)
    q = _sample_q_rows(q_ref)
    pad = jnp.zeros((SEL_BLOCK - tp, NSA_HD), BF16)
    k_new = jnp.concatenate([kn_ref[...], pad], axis=0)
    v_new = jnp.concatenate([vn_ref[...], pad], axis=0)
    lane = _iota2((1, LANES), 1)
    low = lane < SEL_BLOCK
    within = jnp.bitwise_and(lane, SEL_BLOCK - 1)
    ks, vs, bias, kpos = [], [], [], []
    for i in range(0, N_SEL, 2):
        pair_bias, pair_pos = [], []
        for j in (i, i + 1):
            blk = idx_ref[base + j]
            is_new = blk >= first_new
            ks.append(jnp.where(is_new, k_new, k_blocks[j][pl.ds(h, SEL_BLOCK, stride=NSA_KVH), :].astype(BF16)))
            vs.append(jnp.where(is_new, v_new, v_blocks[j][pl.ds(h, SEL_BLOCK, stride=NSA_KVH), :].astype(BF16)))
            pair_bias.append(tbl_ref[jnp.clip(blk - (first_new - NEAR_BLOCKS), 0, NEAR_BLOCKS)])
            pair_pos.append(jnp.where(blk <= cur, blk * SEL_BLOCK, PAST_LEN + SEL_BLOCK * LANES) + within)
        bias.append(jnp.where(low, pair_bias[0], pair_bias[1]))
        kpos.append(jnp.where(low, pair_pos[0], pair_pos[1]))
    k_all = jnp.concatenate(ks, axis=0)
    v_all = jnp.concatenate(vs, axis=0)
    step = jnp.bitwise_and(_iota2((rows, 1), 0), tp - 1)
    ok = jnp.concatenate(kpos, axis=1) <= PAST_LEN + step
    p, l = _softmax_rows(jnp.where(ok, _dot_nt(q, k_all) + jnp.concatenate(bias, axis=1), NEG_INF))
    o = _dot(p.astype(BF16), v_all) / jnp.maximum(l, TINY)

    @pl.when(t == 0)
    def _():
        osel[...] = jnp.zeros_like(osel)

    osel[...] = jnp.where(step == t, o, osel[...])

    @pl.when(t == T - 1)
    def _():
        gate = jax.nn.sigmoid(gb_ref[...] + bg_ref[...])
        zb = _silu(zb_ref[...])
        for g in range(NSA_G):
            r = slice(g * tp, (g + 1) * tp)
            head = h * NSA_G + g
            mix = (_lane_col(gate, head) * ocmp_ref[r, :] + _lane_col(gate, NSA_HEADS + head) * osel[r, :]
                   + _lane_col(gate, 2 * NSA_HEADS + head) * owin_ref[r, :])
            sl = slice(g * NSA_HD, (g + 1) * NSA_HD)
            o_ref[:, sl] = (mix * zb[:, sl]).astype(o_ref.dtype)


def _nsa_sample_sel_call(ya, yb, ks16, vs16, idx, page_table, pool_k, pool_v, tbl, o_cmp, o_win, bg_r, *, B, T):
    tp = SAMPLE_PAD_T
    rows = NSA_G * tp
    gw = NSA_G * NSA_HD
    n_pages = page_table.shape[1]
    halves = PAGE_SIZE // SEL_BLOCK
    idx_flat = idx[:, :, :T, :N_SEL].reshape(-1)
    view_k, view_v = _pool_rows(pool_k), _pool_rows(pool_v)

    def blk_spec(j):
        def index(b, h, t, idx_s, pt_s):
            blk = idx_s[((b * NSA_KVH + h) * T + t) * N_SEL + j]
            page = pt_s[b * n_pages + jnp.minimum(blk // halves, n_pages - 1)]
            return (page * halves + blk % halves, 0)
        return pl.BlockSpec((SEL_BLOCK * NSA_KVH, NSA_HD), index)

    o_spec = pl.BlockSpec((None, None, rows, NSA_HD), lambda b, h, t, *_: (b, h, 0, 0))
    grid_spec = pltpu.PrefetchScalarGridSpec(
        num_scalar_prefetch=2,
        grid=(B, NSA_KVH, T),
        in_specs=[pl.BlockSpec((tp, gw), lambda b, h, t, *_: (b, EVEN_A["qb"] // gw + h)),
                  pl.BlockSpec((tp, NSA_HD), lambda b, h, t, *_: (b, h)),
                  pl.BlockSpec((tp, NSA_HD), lambda b, h, t, *_: (b, h)),
                  pl.BlockSpec((None, NEAR_BLOCKS + 1, rows, LANES), lambda b, h, t, *_: (h, 0, 0, 0)),
                  o_spec, o_spec,
                  pl.BlockSpec((tp, LANES), lambda b, h, t, *_: (b, EVEN_B["gb"] // LANES)),
                  pl.BlockSpec((1, LANES), lambda b, h, t, *_: (0, 0)),
                  pl.BlockSpec((tp, gw), lambda b, h, t, *_: (b, EVEN_B["zb"] // gw + h))]
        + [blk_spec(j) for j in range(N_SEL)] * 2,
        out_specs=pl.BlockSpec((tp, gw), lambda b, h, t, *_: (b, h)),
        scratch_shapes=[pltpu.VMEM((rows, NSA_HD), F32)],
    )
    return pl.pallas_call(
        functools.partial(_nsa_sample_sel_body, T=T),
        grid_spec=grid_spec,
        out_shape=jax.ShapeDtypeStruct((B * tp, NSA_W), BF16),
        compiler_params=_params(("arbitrary", "arbitrary", "arbitrary")),
        name="nsa_sample_sel",
    )(idx_flat, page_table.reshape(-1), ya, ks16, vs16, tbl, o_cmp, o_win, yb, bg_r, yb,
      *([view_k] * N_SEL), *([view_v] * N_SEL))


def _sample_bias_tables(rel_bias, T, wb):
    tp = SAMPLE_PAD_T
    ncmp = PAST_LEN // CMP_STRIDE
    wlen = -(-(wb + tp) // LANES) * LANES
    first = PAST_LEN // SEL_BLOCK - NEAR_BLOCKS
    assert PAST_LEN - ((first + 1) * SEL_BLOCK - 1) >= REL_MAX_DIST
    lo, hi = -wlen, PAST_LEN + tp
    rev = _bias_line(rel_bias, lo, hi, descending=True)

    def rows(tbl):
        return tbl.reshape(NSA_KVH, NSA_G * tp, tbl.shape[-1])

    t_c = _toeplitz(rev, hi - 1 - (PAST_LEN - (CMP_BLOCK - 1)), tp, CMP_STRIDE * ncmp)[:, :, ::CMP_STRIDE]
    t_w = _toeplitz(rev, hi - 1 - wb, tp, wlen)
    far = jnp.broadcast_to(rev[:, hi - 1 - REL_MAX_DIST][:, None, None], (NSA_HEADS, tp, LANES))
    near = []
    for k in range(1, NEAR_BLOCKS + 1):
        half = _toeplitz(rev, hi - 1 - (PAST_LEN - (first + k) * SEL_BLOCK), tp, SEL_BLOCK)
        near.append(jnp.concatenate([half, half], axis=-1))
    t_s = jnp.stack([far] + near, axis=1).reshape(NSA_KVH, NSA_G, NEAR_BLOCKS + 1, tp, LANES)
    t_s = t_s.transpose(0, 2, 1, 3, 4).reshape(NSA_KVH, NEAR_BLOCKS + 1, NSA_G * tp, LANES)
    return rows(t_c), rows(t_w), t_s


def _tail_even(w):
    return _tail_relayout(w, EVEN_KV_OFF + 6 * NSA_KV_W, 3 * NSA_HEADS, NSA_W + MEM_W, EVEN_B_N)


def _tail_odd(w):
    return _tail_relayout(w, ODD_A_N, 2 * ML_HEADS, ML_V_W + MEM_W, ODD_B_N)


def _gate_bias_even(b_gate):
    return jnp.pad(b_gate, (0, LANES - 3 * NSA_HEADS)).reshape(1, LANES)


def _gate_bias_odd(b_if):
    return jnp.pad(b_if.reshape(2 * ML_HEADS), (0, LANES - 2 * ML_HEADS)).reshape(1, LANES)


def _rel_bucket(dist):
    n = np.maximum(dist, 0)
    exact = REL_BUCKETS // 2
    nf = np.maximum(n, 1).astype(np.float32)
    large = exact + (np.log(nf / exact) / math.log(REL_MAX_DIST / exact) * (REL_BUCKETS - exact)).astype(np.int32)
    return np.where(n < exact, n, np.minimum(large, REL_BUCKETS - 1))


def _bias_line(rel_bias, lo, hi, descending=False):
    dist = np.arange(hi - 1, lo - 1, -1) if descending else np.arange(lo, hi)
    buckets = _rel_bucket(dist)
    edges = np.flatnonzero(np.diff(buckets)) + 1
    starts = np.concatenate([[0], edges])
    ends = np.concatenate([edges, [hi - lo]])
    bias_t = rel_bias.T.astype(F32)
    runs = [jnp.broadcast_to(bias_t[:, int(buckets[s])][:, None], (NSA_HEADS, int(e - s))) for s, e in zip(starts, ends)]
    return jnp.concatenate(runs, axis=1)


def _skew_rows(v, rows, step, cols):
    n = v.shape[1]
    reps = -(-rows * (n + step) // n)
    return jnp.tile(v, (1, reps))[:, :rows * (n + step)].reshape(v.shape[0], rows, n + step)[:, :, :cols]


def _toeplitz(rev, start, rows, cols):
    seg = rev[:, start - (rows - 1):start + cols]
    return _skew_rows(jnp.roll(seg, -(rows - 1), axis=1), rows, -1, cols)


def _prompt_bias_tables(rel_bias, T):
    ncmp = T // CMP_STRIDE
    wlen = WINDOW + Q_BLOCK
    lo, hi = -(CMP_STRIDE * ncmp + CMP_BLOCK), T
    line = _bias_line(rel_bias, lo, hi)
    rev = _bias_line(rel_bias, lo, hi, descending=True)

    def split(tbl):
        return tbl.reshape((NSA_KVH, NSA_G) + tbl.shape[1:])

    back = CMP_STRIDE * (ncmp - 1)
    first = -(back + CMP_BLOCK - 1) - lo
    seg = line[:, first:first + T + back]
    t_c = _skew_rows(jnp.roll(seg, -back, axis=1), ncmp, -CMP_STRIDE, T).swapaxes(1, 2)
    t_s = _toeplitz(rev, hi - 1 - (T - Q_BLOCK), Q_BLOCK, T)
    t_w = _toeplitz(rev, hi - 1 - WINDOW, Q_BLOCK, wlen)
    return split(t_c), split(t_s), split(t_w)


def _nsa_sample(ya, yb, kv16, page_table, pk_cmp, pv_cmp, pk_sel, pv_sel, wk, wv, bg_r, w1, b1, w2, pe, rel_bias,
                *, B, T):
    assert (PAST_LEN + T) // CMP_STRIDE == PAST_LEN // CMP_STRIDE
    abk = _cmp_pages_call(pk_cmp, page_table, w1[0], pe[0], B=B)
    abv = _cmp_pages_call(pv_cmp, page_table, w1[1], pe[1], B=B)
    bias_c, bias_w, tbl = _sample_bias_tables(rel_bias, T, wk.shape[1])
    o_cmp, o_win, idx = _nsa_sample_main_call(ya, kv16[4], kv16[5], abk, abv, b1, w2, wk, wv, bias_c, bias_w, B=B, T=T)
    return _nsa_sample_sel_call(ya, yb, kv16[2], kv16[3], idx, page_table, pk_sel, pv_sel, tbl, o_cmp, o_win, bg_r,
                                B=B, T=T)


def _kv_project(x, wt):
    outs = [_matmul_heads(x, wt, first=EVEN_KV_OFF + j * NSA_KV_W, transposed=True) for j in range(6)]
    return [o[0] for o in outs], [o[1] for o in outs]


def _even_prompt(hp2d, npre, mk16, mv16, wt, wt_b, bg_r, w1, b1, w2, pe, lb, g_norm, w_out, rel_bias, *, B, T):
    ya, yb = _matmul_nt(npre, wt, tn=W_TILE_N, rows=(0, EVEN_A_N)), _matmul_nt(npre, wt_b)
    kv32, kv16 = _kv_project(npre, wt)
    oa, s_new = _hgrn_call(ya, jnp.zeros((B, HG_HEADS, HG_DK, HG_DV), F32), lb, g_norm, B=B, T=T, L=CHUNK, valid=CHUNK)
    kcmp = _compress_call(kv16[0], w1[0], b1[0], w2[0], pe[0], B=B, T=T)
    vcmp = _compress_call(kv16[1], w1[1], b1[1], w2[1], pe[1], B=B, T=T)
    ob = _nsa_prompt_call(ya, yb, kv16[2:], kcmp, vcmp, bg_r, *_prompt_bias_tables(rel_bias, T), B=B, T=T)
    om = _mem_call(yb, EVEN_B["qm"], mk16, mv16, B=B, T=T)
    h_new = _outproj([oa, ob, om], w_out, hp2d)
    wb = min(WINDOW, T)
    rows = [r.reshape(B, T, NSA_KVH, NSA_HD) for r in kv32]
    return h_new, (rows[0], rows[1], rows[2], rows[3], rows[4][:, -wb:], rows[5][:, -wb:], s_new)


def _even_sample(hs2d, nsam, mk_s, mv_s, page_table, pk_cmp, pv_cmp, pk_sel, pv_sel, wk, wv, s0,
                 wt, wt_b, bg_r, w1, b1, w2, pe, lb, g_norm, w_out, rel_bias, *, B, T):
    tp = SAMPLE_PAD_T
    ya, yb = _matmul_nt(nsam, wt, tn=W_TILE_N, rows=(0, EVEN_A_N)), _matmul_nt(nsam, wt_b)
    kv32, kv16 = _kv_project(nsam, wt)
    oa, s_new = _hgrn_call(ya, s0, lb, g_norm, B=B, T=tp, L=tp, valid=T)
    ob = _nsa_sample(ya, yb, kv16, page_table, pk_cmp, pv_cmp, pk_sel, pv_sel, wk, wv, bg_r, w1, b1, w2, pe, rel_bias,
                     B=B, T=T)
    om = _mem_call(yb, EVEN_B["qm"], mk_s.reshape(B * N_MEM, MEM_W), mv_s.reshape(B * N_MEM, MEM_W), B=B, T=tp)
    rows = [r.reshape(B, tp, NSA_KVH, NSA_HD)[:, :T] for r in kv32]
    wb = wk.shape[1]
    win_k = jnp.concatenate([wk, rows[4]], axis=1)[:, -wb:]
    win_v = jnp.concatenate([wv, rows[5]], axis=1)[:, -wb:]
    return _outproj([oa, ob, om], w_out, hs2d), (rows[0], rows[1], rows[2], rows[3], win_k, win_v, s_new)


def _odd_mix(h2d, hn, k2d, v2d, c0, n0, m0, wt, wt_b, bif_r, g_norm, w_out, *, B, T, L, valid):
    ya, yb = _matmul_nt(hn, wt, tn=W_TILE_N, rows=(0, ODD_A_N)), _matmul_nt(hn, wt_b)
    h, c_new, n_new, m_new = _mlstm_call(ya, yb, c0, n0, m0, bif_r, g_norm, B=B, T=T, L=L, valid=valid)
    om = _mem_call(yb, ODD_B["qm"], k2d, v2d, B=B, T=T)
    return _outproj([h, om], w_out, h2d), (c_new, n_new, m_new)


def _stack(lst, i):
    return jnp.stack([t[i] for t in lst])


def kernel(x_prompt, x_sample, cache_mem_k, cache_mem_v, cache_cmp_k, cache_cmp_v, cache_sel_k, cache_sel_v,
           cache_win_k, cache_win_v, state_hgrn, state_mlstm_c, state_mlstm_n, state_mlstm_m, page_table,
           mem_prompt, norm_w, mem_norm_w, final_norm_w, rel_bias, w_mem_kv, w_in_even, b_nsa_gate,
           w_cmp1, b_cmp1, w_cmp2, pe_cmp, hgrn_lb_logits, hgrn_norm_w, w_out_even, w_in_odd, b_mlstm_if,
           mlstm_norm_w, w_out_odd):
    bp, tp = x_prompt.shape[:2]
    bs, ts = x_sample.shape[:2]
    tsp = SAMPLE_PAD_T
    lbs = jnp.cumsum(jax.nn.softmax(hgrn_lb_logits.astype(F32), axis=0), axis=0)
    hp = x_prompt.reshape(bp * tp, D_MODEL)
    hs = jnp.pad(x_sample, ((0, 0), (0, tsp - ts), (0, 0))).reshape(bs * tsp, D_MODEL)
    mem2d = mem_prompt.reshape(bp * N_MEM, D_MODEL)
    mem_new, even_p, even_s, odd_p, odd_s = [], [], [], [], []
    for l in range(DEPTH):
        npre = _rmsnorm_rows(hp, norm_w[l], BF16)
        nsam = _rmsnorm_rows(hs, norm_w[l], BF16)
        nmem = _rmsnorm_rows(mem2d, mem_norm_w[l], BF16)
        mk32, mk16 = _matmul_heads(nmem, w_mem_kv[l], first=0)
        mv32, mv16 = _matmul_heads(nmem, w_mem_kv[l], first=MEM_W)
        mem_new.append((mk32.reshape(bp, N_MEM, MEM_HEADS, MEM_HD), mv32.reshape(bp, N_MEM, MEM_HEADS, MEM_HD)))
        mk_s, mv_s = cache_mem_k[l], cache_mem_v[l]
        if l % 2 == 0:
            e = l // 2
            w_in = w_in_even[e].T
            w_b = _tail_even(w_in)
            w_out = w_out_even[e].astype(BF16)
            bg_r = _gate_bias_even(b_nsa_gate[e])
            cmpw = (w_cmp1[e].reshape(2, CMP_BLOCK, NSA_HD, NSA_HD), b_cmp1[e], w_cmp2[e], pe_cmp[e])
            hp, st_p = _even_prompt(hp, npre, mk16, mv16, w_in, w_b, bg_r, *cmpw, lbs[l], hgrn_norm_w[e], w_out,
                                    rel_bias, B=bp, T=tp)
            hs, st_s = _even_sample(hs, nsam, mk_s, mv_s, page_table, cache_cmp_k[e], cache_cmp_v[e], cache_sel_k[e],
                                    cache_sel_v[e], cache_win_k[e], cache_win_v[e], state_hgrn[e], w_in, w_b, bg_r,
                                    *cmpw, lbs[l], hgrn_norm_w[e], w_out, rel_bias, B=bs, T=ts)
            even_p.append(st_p)
            even_s.append(st_s)
        else:
            o = l // 2
            w_in = w_in_odd[o].T
            w_b = _tail_odd(w_in)
            w_out = w_out_odd[o].astype(BF16)
            bif_r = _gate_bias_odd(b_mlstm_if[o])
            hp, st_p = _odd_mix(hp, npre, mk16, mv16, jnp.zeros((bp, ML_HEADS, ML_DV, ML_DK), F32),
                                jnp.zeros((bp, ML_HEADS, ML_DK), F32), jnp.zeros((bp, ML_HEADS), F32),
                                w_in, w_b, bif_r, mlstm_norm_w[o], w_out, B=bp, T=tp, L=ML_CHUNK, valid=ML_CHUNK)
            hs, st_s = _odd_mix(hs, nsam, mk_s.reshape(bs * N_MEM, MEM_W), mv_s.reshape(bs * N_MEM, MEM_W),
                                state_mlstm_c[o], state_mlstm_n[o], state_mlstm_m[o],
                                w_in, w_b, bif_r, mlstm_norm_w[o], w_out, B=bs, T=tsp, L=tsp, valid=ts)
            odd_p.append(st_p)
            odd_s.append(st_s)
    y_prompt = _rmsnorm_rows(hp, final_norm_w, F32).reshape(bp, tp, D_MODEL)
    y_sample = _rmsnorm_rows(hs, final_norm_w, F32).reshape(bs, tsp, D_MODEL)[:, :ts]
    return (y_prompt, y_sample,
            _stack(mem_new, 0), _stack(mem_new, 1),
            _stack(even_p, 0), _stack(even_p, 1), _stack(even_p, 2), _stack(even_p, 3),
            _stack(even_p, 4), _stack(even_p, 5), _stack(even_p, 6),
            _stack(odd_p, 0), _stack(odd_p, 1), _stack(odd_p, 2),
            _stack(even_s, 0), _stack(even_s, 1), _stack(even_s, 2), _stack(even_s, 3),
            _stack(even_s, 4), _stack(even_s, 5), _stack(even_s, 6),
            _stack(odd_s, 0), _stack(odd_s, 1), _stack(odd_s, 2))
```

```python
import functools
import math

import jax
import jax.numpy as jnp
import numpy as np
from jax import lax
from jax.experimental import pallas as pl
from jax.experimental.pallas import tpu as pltpu

D_MODEL = 4096
DEPTH = 2
PAST_LEN = 16384
PAGE_SIZE = 128
N_MEM = 256
EPS = 1e-6
CHUNK = 64

HG_DK = 128
HG_DV = 128
HG_HEADS = D_MODEL // 2 // HG_DV
HG_W = HG_HEADS * HG_DV

NSA_HD = 128
NSA_HEADS = D_MODEL // 2 // NSA_HD
NSA_KVH = 4
NSA_G = NSA_HEADS // NSA_KVH
NSA_W = NSA_HEADS * NSA_HD
NSA_KV_W = NSA_KVH * NSA_HD
CMP_BLOCK = 32
CMP_STRIDE = 16
SEL_BLOCK = 64
SEL_SHIFT = SEL_BLOCK.bit_length() - 1
N_SEL = 16
WINDOW = 512
Q_BLOCK = 128

ML_HEADS = D_MODEL // 512
ML_DK = D_MODEL // 2 // ML_HEADS
ML_DV = D_MODEL // ML_HEADS
ML_QK_W = ML_HEADS * ML_DK
ML_V_W = ML_HEADS * ML_DV

MEM_HEADS = 4
MEM_HD = 128
MEM_W = MEM_HEADS * MEM_HD

REL_BUCKETS = 32
REL_MAX_DIST = 128

F32 = jnp.float32
BF16 = jnp.bfloat16
LANES = 128
NEG_INF = float("-inf")
TINY = float(np.finfo(np.float32).tiny)
EXP_CLAMP = 80.0
VMEM_LIMIT = 56 * 1024 * 1024

HG_HB = 8
ML_HB = 2
ML_CHUNK = 256
W_TILE_M, W_TILE_N = 2048, 256
HG_SUB = 16
SAMPLE_PAD_T = 16

MM_TILE_N = 1024
EVEN_A = {"qa": 0, "fa": HG_W, "ia": 2 * HG_W, "za": 3 * HG_W, "qb": 4 * HG_W}
EVEN_A_N = 4 * HG_W + NSA_W
EVEN_B = {"zb": 0, "qm": NSA_W, "gb": NSA_W + MEM_W}
EVEN_B_N = -(-(NSA_W + MEM_W + LANES) // MM_TILE_N) * MM_TILE_N
EVEN_KV_OFF = EVEN_A_N
ODD_A = {"q": 0, "k": ML_QK_W, "v": 2 * ML_QK_W, "og": 2 * ML_QK_W + ML_V_W}
ODD_A_N = 2 * ML_QK_W + 2 * ML_V_W
ODD_B = {"z": 0, "qm": ML_V_W, "gates": ML_V_W + MEM_W}
ODD_B_N = -(-(ML_V_W + MEM_W + LANES) // MM_TILE_N) * MM_TILE_N


def _dot(a, b):
    return jnp.dot(a, b, preferred_element_type=F32)


def _dot_nt(a, b):
    return lax.dot_general(a, b, (((1,), (1,)), ((), ())), preferred_element_type=F32)


def _dot_tn(a, b):
    return lax.dot_general(a, b, (((0,), (0,)), ((), ())), preferred_element_type=F32)


def _iota2(shape, dim):
    return lax.broadcasted_iota(jnp.int32, shape, dim)


def _cumsum_rows(x, tri_b):
    hi = x.astype(BF16)
    r1 = x - hi.astype(F32)
    mid = r1.astype(BF16)
    lo = (r1 - mid.astype(F32)).astype(BF16)
    return _dot(tri_b, hi) + _dot(tri_b, mid) + _dot(tri_b, lo)


def _row_to_col(row, n):
    eye = _iota2((n, n), 0) == _iota2((n, n), 1)
    return jnp.sum(jnp.where(eye, row, 0.0), axis=1, keepdims=True)


def _col_to_row(col, n):
    eye = _iota2((n, n), 0) == _iota2((n, n), 1)
    return jnp.sum(jnp.where(eye, col, 0.0), axis=0, keepdims=True)


def _lane_col(x, idx):
    return jnp.sum(jnp.where(_iota2(x.shape, 1) == idx, x, 0.0), axis=1, keepdims=True)


def _silu(x):
    return x * jax.nn.sigmoid(x)


def _params(sem):
    return pltpu.CompilerParams(dimension_semantics=sem, vmem_limit_bytes=VMEM_LIMIT)


def _rmsnorm_body(x_ref, w_ref, o_ref):
    x = x_ref[...].astype(F32)
    y = x * lax.rsqrt(jnp.mean(x * x, axis=-1, keepdims=True) + EPS)
    o_ref[...] = (y * w_ref[...].astype(F32)).astype(o_ref.dtype)


def _rmsnorm_rows(x2d, w, out_dtype, tm=256):
    m, d = x2d.shape
    tm = min(tm, m)
    return pl.pallas_call(
        _rmsnorm_body,
        grid=(m // tm,),
        in_specs=[pl.BlockSpec((tm, d), lambda i: (i, 0)), pl.BlockSpec((1, d), lambda i: (0, 0))],
        out_specs=pl.BlockSpec((tm, d), lambda i: (i, 0)),
        out_shape=jax.ShapeDtypeStruct((m, d), out_dtype),
        compiler_params=_params(("parallel",)),
        name="rmsnorm",
    )(x2d, w.reshape(1, d))


def _matmul_nt_body(a_ref, bt_ref, o_ref):
    o_ref[...] = _dot_nt(a_ref[...], bt_ref[...].astype(BF16))


def _matmul_nt(a, bt, tm=1024, tn=MM_TILE_N, rows=None):
    m, k = a.shape
    first, n = rows or (0, bt.shape[0])
    tm, tn = min(tm, m), min(tn, n)
    assert m % tm == 0 and n % tn == 0 and first % tn == 0, (a.shape, bt.shape, rows)
    j0 = first // tn
    return pl.pallas_call(
        _matmul_nt_body,
        grid=(m // tm, n // tn),
        in_specs=[pl.BlockSpec((tm, k), lambda i, j: (i, 0)), pl.BlockSpec((tn, k), lambda i, j: (j0 + j, 0))],
        out_specs=pl.BlockSpec((tm, tn), lambda i, j: (i, j)),
        out_shape=jax.ShapeDtypeStruct((m, n), F32),
        compiler_params=_params(("parallel", "parallel")),
        name="matmul",
    )(a, bt)


def _tail_body(lo_ref, hi_ref, gate_ref, o_ref, *, shift, n_main):
    i = pl.program_id(0)
    main = jnp.concatenate([lo_ref[shift:, :], hi_ref[:shift, :]], axis=0)
    gates = jnp.where(_iota2((LANES, 1), 0) < shift, gate_ref[...], 0.0)
    o_ref[...] = jnp.where(i < n_main, main, jnp.where(i == n_main, gates, 0.0)).astype(o_ref.dtype)


def _tail_relayout(wt, first, shift, main, out_rows):
    n, k = wt.shape
    assert first % LANES == 0 and main % LANES == 0 and out_rows % LANES == 0 and shift % 8 == 0 and shift < LANES
    assert first + shift + main == n
    c0, n_main = first // LANES, main // LANES
    return pl.pallas_call(
        functools.partial(_tail_body, shift=shift, n_main=n_main),
        grid=(out_rows // LANES,),
        in_specs=[pl.BlockSpec((LANES, k), lambda i: (c0 + jnp.minimum(i, n_main - 1), 0)),
                  pl.BlockSpec((LANES, k), lambda i: (c0 + jnp.minimum(i, n_main - 1) + 1, 0)),
                  pl.BlockSpec((LANES, k), lambda i: (c0, 0))],
        out_specs=pl.BlockSpec((LANES, k), lambda i: (i, 0)),
        out_shape=jax.ShapeDtypeStruct((out_rows, k), BF16),
        compiler_params=_params(("parallel",)),
        name="tail_relayout",
    )(wt, wt, wt)


def _matmul_heads_body(a_ref, b_ref, o32_ref, o16_ref, *, transposed):
    b = b_ref[...].astype(BF16)
    acc = _dot_nt(a_ref[...], b) if transposed else _dot(a_ref[...], b)
    for h in range(MEM_HEADS):
        o32_ref[:, h, :] = acc[:, h * LANES:(h + 1) * LANES]
    o16_ref[...] = acc.astype(BF16)


def _matmul_heads(a, b, first=0, transposed=False, tm=1024):
    m, k = a.shape
    n = MEM_HEADS * LANES
    tm = min(tm, m)
    assert m % tm == 0 and first % n == 0, (a.shape, b.shape, first)
    j0 = first // n
    b_spec = pl.BlockSpec((n, k), lambda i: (j0, 0)) if transposed else pl.BlockSpec((k, n), lambda i: (0, j0))
    return pl.pallas_call(
        functools.partial(_matmul_heads_body, transposed=transposed),
        grid=(m // tm,),
        in_specs=[pl.BlockSpec((tm, k), lambda i: (i, 0)), b_spec],
        out_specs=[pl.BlockSpec((tm, MEM_HEADS, LANES), lambda i: (i, 0, 0)), pl.BlockSpec((tm, n), lambda i: (i, 0))],
        out_shape=[jax.ShapeDtypeStruct((m, MEM_HEADS, LANES), F32), jax.ShapeDtypeStruct((m, n), BF16)],
        compiler_params=_params(("parallel",)),
        name="matmul_heads",
    )(a, b)


def _outproj_body(*refs, widths):
    xs = refs[:len(widths)]
    w_ref, r_ref, o_ref = refs[len(widths):]
    acc = r_ref[...]
    off = 0
    for x_ref, w in zip(xs, widths):
        acc = acc + _dot(x_ref[...], w_ref[off:off + w, :])
        off += w
    o_ref[...] = acc


def _outproj(xs, w_bf16, resid, tm=1024, tn=512):
    m = resid.shape[0]
    n = w_bf16.shape[1]
    widths = tuple(x.shape[1] for x in xs)
    assert sum(widths) == w_bf16.shape[0]
    tm = min(tm, m)
    in_specs = [pl.BlockSpec((tm, w), lambda i, j: (i, 0)) for w in widths]
    in_specs += [pl.BlockSpec((w_bf16.shape[0], tn), lambda i, j: (0, j)), pl.BlockSpec((tm, tn), lambda i, j: (i, j))]
    return pl.pallas_call(
        functools.partial(_outproj_body, widths=widths),
        grid=(m // tm, n // tn),
        in_specs=in_specs,
        out_specs=pl.BlockSpec((tm, tn), lambda i, j: (i, j)),
        out_shape=jax.ShapeDtypeStruct((m, n), F32),
        compiler_params=_params(("parallel", "parallel")),
        name="outproj",
    )(*xs, w_bf16, resid)


def _hgrn_body(qa_ref, fa_ref, ia_ref, za_ref, lb_ref, gn_ref, s0_ref, o_ref, s_out, s_scr, *, L, valid):
    c = pl.program_id(2)

    @pl.when(c == 0)
    def _():
        s_scr[...] = s0_ref[...]

    lb = lb_ref[...]
    sig = jax.nn.sigmoid(fa_ref[...])
    logf = jnp.log(lb + (1.0 - lb) * sig)
    kk = (1.0 - lb) * (1.0 - sig)
    if valid < L:
        live = _iota2((L, 1), 0) < valid
        logf = jnp.where(live, logf, 0.0)
        kk = jnp.where(live, kk, 0.0)
    tri_b = (_iota2((L, L), 0) >= _iota2((L, L), 1)).astype(BF16)
    bc = _cumsum_rows(logf, tri_b)
    q = _silu(qa_ref[...])
    gate = _silu(za_ref[...])
    v = ia_ref[...]
    gn = gn_ref[...]
    nsub = L // HG_SUB
    rr = _iota2((L, nsub * L), 0)
    cc = _iota2((L, nsub * L), 1)
    keep = ((jnp.right_shift(cc, L.bit_length() - 1) == jnp.right_shift(rr, HG_SUB.bit_length() - 1))
            & (jnp.bitwise_and(cc, L - 1) <= rr))
    for j in range(HG_HB):
        sl = slice(j * HG_DK, (j + 1) * HG_DK)
        bj, qj, kj = bc[:, sl], q[:, sl], kk[:, sl]
        vb = v[:, sl].astype(BF16)
        s_prev = s_scr[j]
        inter = _dot((qj * jnp.exp(bj)).astype(BF16), s_prev.astype(BF16))
        mids = [bj[i * HG_SUB + HG_SUB // 2:i * HG_SUB + HG_SUB // 2 + 1, :] for i in range(nsub)]
        mid_rows = jnp.concatenate([jnp.broadcast_to(m, (HG_SUB, HG_DK)) for m in mids], axis=0)
        q_dec = qj * jnp.exp(jnp.minimum(bj - mid_rows, EXP_CLAMP))
        k_dec = jnp.concatenate([kj * jnp.exp(jnp.minimum(m - bj, EXP_CLAMP)) for m in mids], axis=0)
        att = jnp.where(keep, _dot_nt(q_dec.astype(BF16), k_dec.astype(BF16)), 0.0)
        o = inter + _dot(att.astype(BF16), jnp.concatenate([vb] * nsub, axis=0))
        o_n = o * lax.rsqrt(jnp.mean(o * o, axis=-1, keepdims=True) + EPS) * gn
        o_ref[:, sl] = (o_n * gate[:, sl]).astype(o_ref.dtype)
        bl = bj[L - 1:L, :]
        kd = kj * jnp.exp(bl - bj)
        s_scr[j] = _row_to_col(jnp.exp(bl), HG_DK) * s_prev + _dot_tn(kd.astype(BF16), vb)

    @pl.when(c == pl.num_programs(2) - 1)
    def _():
        s_out[...] = s_scr[...]


def _hgrn_call(y, s0, lb, gn, *, B, T, L, valid):
    nc = T // L
    w = HG_HB * HG_DK

    def col(name):
        blk = EVEN_A[name] // w
        return pl.BlockSpec((L, w), lambda b, hg, c: (b * nc + c, blk + hg))

    state_spec = pl.BlockSpec((None, HG_HB, HG_DK, HG_DV), lambda b, hg, c: (b, hg, 0, 0))
    return pl.pallas_call(
        functools.partial(_hgrn_body, L=L, valid=valid),
        grid=(B, HG_HEADS // HG_HB, nc),
        in_specs=[col("qa"), col("fa"), col("ia"), col("za"),
                  pl.BlockSpec((1, w), lambda b, hg, c: (0, hg)),
                  pl.BlockSpec((1, HG_DV), lambda b, hg, c: (0, 0)),
                  state_spec],
        out_specs=[pl.BlockSpec((L, w), lambda b, hg, c: (b * nc + c, hg)), state_spec],
        out_shape=[jax.ShapeDtypeStruct((B * T, HG_W), BF16),
                   jax.ShapeDtypeStruct((B, HG_HEADS, HG_DK, HG_DV), F32)],
        scratch_shapes=[pltpu.VMEM((HG_HB, HG_DK, HG_DV), F32)],
        compiler_params=_params(("arbitrary", "arbitrary", "arbitrary")),
        name="hgrn2",
    )(y, y, y, y, lb.reshape(1, HG_W), gn.reshape(1, HG_DV), s0)


def _mlstm_body(q_ref, k_ref, v_ref, og_ref, z_ref, g_ref, bif_ref, gn_ref, c0_ref, n0_ref, m0_ref,
                h_ref, c_out, n_out, m_out, c_scr, n_scr, m_scr, *, L, valid):
    c = pl.program_id(2)

    @pl.when(c == 0)
    def _():
        c_scr[...] = c0_ref[...]
        n_scr[...] = n0_ref[...]
        m_scr[...] = m0_ref[...]

    gates = g_ref[...] + bif_ref[...]
    log_i = gates
    log_f = jnp.minimum(gates, 0.0) - jnp.log(1.0 + jnp.exp(-jnp.abs(gates)))
    if valid < L:
        live = _iota2((L, 1), 0) < valid
        log_i = jnp.where(live, log_i, -1e30)
        log_f = jnp.where(live, log_f, 0.0)
    tri = _iota2((L, L), 0) >= _iota2((L, L), 1)
    bcs = _cumsum_rows(log_f, tri.astype(BF16))
    for j in range(ML_HB):
        head = pl.program_id(1) * ML_HB + j
        b_col = _lane_col(bcs, ML_HEADS + head)
        i_col = _lane_col(log_i, head)
        b_row = _col_to_row(b_col, L)
        i_row = _col_to_row(i_col, L)
        m_prev = m_scr[:, j:j + 1]
        dmat = jnp.where(tri, b_col - b_row + i_row, NEG_INF)
        inter = b_col + m_prev
        mt = jnp.maximum(inter, jnp.max(dmat, axis=1, keepdims=True))
        w_in = jnp.exp(dmat - mt)
        w_x = jnp.exp(inter - mt)
        qj = q_ref[:, j * ML_DK:(j + 1) * ML_DK]
        kj = k_ref[:, j * ML_DK:(j + 1) * ML_DK] * (ML_DK ** -0.5)
        vj = v_ref[:, j * ML_DV:(j + 1) * ML_DV]
        qb, kb = qj.astype(BF16), kj.astype(BF16)
        sw = _dot_nt(qb, kb) * w_in
        c_prev = c_scr[j]
        n_prev = n_scr[:, j * ML_DK:(j + 1) * ML_DK]
        num = w_x * _dot_nt(qb, c_prev.astype(BF16)) + _dot(sw.astype(BF16), vj.astype(BF16))
        den = w_x * jnp.sum(qj * n_prev, axis=1, keepdims=True) + jnp.sum(sw, axis=1, keepdims=True)
        h = num / jnp.maximum(jnp.abs(den), jnp.exp(-mt))
        m_last = mt[L - 1:L, :]
        b_last = b_col[L - 1:L, :]
        w_end = jnp.exp(b_last - b_col + i_col - m_last)
        d_c = jnp.exp(b_last + m_prev - m_last)
        c_scr[j] = d_c * c_prev + _dot_tn((w_end * vj).astype(BF16), kb)
        n_scr[:, j * ML_DK:(j + 1) * ML_DK] = d_c * n_prev + jnp.sum(w_end * kj, axis=0, keepdims=True)
        m_scr[:, j:j + 1] = m_last
        sv = slice(j * ML_DV, (j + 1) * ML_DV)
        h_n = h * lax.rsqrt(jnp.mean(h * h, axis=-1, keepdims=True) + EPS) * gn_ref[:, sv]
        h_ref[:, sv] = (h_n * jax.nn.sigmoid(og_ref[:, sv]) * _silu(z_ref[:, sv])).astype(h_ref.dtype)

    @pl.when(c == pl.num_programs(2) - 1)
    def _():
        c_out[...] = c_scr[...]
        n_out[...] = n_scr[...]
        m_out[...] = m_scr[...]


def _mlstm_call(ya, yb, c0, n0, m0, bif_r, gn, *, B, T, L, valid):
    nc = T // L
    ng = ML_HEADS // ML_HB
    wk, wv = ML_HB * ML_DK, ML_HB * ML_DV

    def col(name, w):
        blk = (ODD_A[name] if name in ODD_A else ODD_B[name]) // w
        return pl.BlockSpec((L, w), lambda b, hg, c: (b * nc + c, blk + hg))

    c_spec = pl.BlockSpec((None, ML_HB, ML_DV, ML_DK), lambda b, hg, c: (b, hg, 0, 0))
    n_spec = pl.BlockSpec((None, 1, wk), lambda b, hg, c: (b, 0, hg))
    m_spec = pl.BlockSpec((None, None, 1, LANES), lambda b, hg, c: (b, hg, 0, 0))
    m0_r = jnp.pad(m0.reshape(B, ng, 1, ML_HB), ((0, 0), (0, 0), (0, 0), (0, LANES - ML_HB)))
    h, c_new, n_new, m_new = pl.pallas_call(
        functools.partial(_mlstm_body, L=L, valid=valid),
        grid=(B, ng, nc),
        in_specs=[col("q", wk), col("k", wk), col("v", wv), col("og", wv), col("z", wv),
                  pl.BlockSpec((L, LANES), lambda b, hg, c: (b * nc + c, ODD_B["gates"] // LANES)),
                  pl.BlockSpec((1, LANES), lambda b, hg, c: (0, 0)),
                  pl.BlockSpec((1, wv), lambda b, hg, c: (0, hg)),
                  c_spec, n_spec, m_spec],
        out_specs=[pl.BlockSpec((L, wv), lambda b, hg, c: (b * nc + c, hg)), c_spec, n_spec, m_spec],
        out_shape=[jax.ShapeDtypeStruct((B * T, ML_V_W), BF16),
                   jax.ShapeDtypeStruct((B, ML_HEADS, ML_DV, ML_DK), F32),
                   jax.ShapeDtypeStruct((B, 1, ML_QK_W), F32),
                   jax.ShapeDtypeStruct((B, ng, 1, LANES), F32)],
        scratch_shapes=[pltpu.VMEM((ML_HB, ML_DV, ML_DK), F32), pltpu.VMEM((1, wk), F32), pltpu.VMEM((1, LANES), F32)],
        compiler_params=_params(("arbitrary", "arbitrary", "arbitrary")),
        name="mlstm",
    )(ya, ya, ya, ya, yb, yb, bif_r, gn.reshape(1, ML_V_W), c0, n0.reshape(B, 1, ML_QK_W), m0_r)
    return h, c_new, n_new.reshape(B, ML_HEADS, ML_DK), m_new[:, :, 0, :ML_HB].reshape(B, ML_HEADS)


def _mem_body(q_ref, k_ref, v_ref, o_ref):
    q = q_ref[...] * (MEM_HD ** -0.5)
    for h in range(MEM_HEADS):
        sl = slice(h * MEM_HD, (h + 1) * MEM_HD)
        s = _dot_nt(q[:, sl].astype(BF16), k_ref[:, sl].astype(BF16))
        p = jnp.exp(s - jnp.max(s, axis=-1, keepdims=True))
        o = _dot(p.astype(BF16), v_ref[:, sl].astype(BF16)) / jnp.sum(p, axis=-1, keepdims=True)
        o_ref[:, sl] = o.astype(o_ref.dtype)


def _mem_call(y, q_off, k2d, v2d, *, B, T, tq=256):
    tq = min(tq, T)
    nq = T // tq
    qb = q_off // MEM_W
    return pl.pallas_call(
        _mem_body,
        grid=(B, nq),
        in_specs=[pl.BlockSpec((tq, MEM_W), lambda b, i: (b * nq + i, qb)),
                  pl.BlockSpec((N_MEM, MEM_W), lambda b, i: (b, 0)),
                  pl.BlockSpec((N_MEM, MEM_W), lambda b, i: (b, 0))],
        out_specs=pl.BlockSpec((tq, MEM_W), lambda b, i: (b * nq + i, 0)),
        out_shape=jax.ShapeDtypeStruct((B * T, MEM_W), BF16),
        compiler_params=_params(("parallel", "parallel")),
        name="mem_attn",
    )(y, k2d, v2d)


def _gelu_tanh(x):
    return 0.5 * x * (1.0 + jnp.tanh(math.sqrt(2.0 / math.pi) * (x + 0.044715 * (x * x * x))))


def _compress_body(x_ref, w1_ref, b1_ref, w2_ref, pe_ref, o_ref, x32, *, nch):
    x32[...] = x_ref[...].astype(F32)
    a = jnp.zeros((nch, NSA_HD), F32)
    b = jnp.zeros((nch, NSA_HD), F32)
    for s in range(CMP_STRIDE):
        r = x32[pl.ds(s, nch, stride=CMP_STRIDE), :]
        a = a + _dot((r + pe_ref[s:s + 1, :]).astype(BF16), w1_ref[s])
        b = b + _dot((r + pe_ref[CMP_STRIDE + s:CMP_STRIDE + s + 1, :]).astype(BF16), w1_ref[CMP_STRIDE + s])
    h = a + pltpu.roll(b, nch - 1, 0) + b1_ref[...]
    o_ref[...] = _dot(_gelu_tanh(h).astype(BF16), w2_ref[...])


def _compress_call(x16, w1, b1, w2, pe, *, B, T):
    nch = T // CMP_STRIDE
    return pl.pallas_call(
        functools.partial(_compress_body, nch=nch),
        grid=(B, NSA_KVH),
        in_specs=[pl.BlockSpec((T, NSA_HD), lambda b, h: (b, h)),
                  pl.BlockSpec((CMP_BLOCK, NSA_HD, NSA_HD), lambda b, h: (0, 0, 0)),
                  pl.BlockSpec((1, NSA_HD), lambda b, h: (0, 0)),
                  pl.BlockSpec((NSA_HD, NSA_HD), lambda b, h: (0, 0)),
                  pl.BlockSpec((CMP_BLOCK, NSA_HD), lambda b, h: (0, 0))],
        out_specs=pl.BlockSpec((None, None, nch, NSA_HD), lambda b, h: (b, h, 0, 0)),
        out_shape=jax.ShapeDtypeStruct((B, NSA_KVH, nch, NSA_HD), F32),
        scratch_shapes=[pltpu.VMEM((T, NSA_HD), F32)],
        compiler_params=_params(("parallel", "parallel")),
        name="nsa_compress",
    )(x16, w1.astype(BF16), b1.reshape(1, NSA_HD), w2.astype(BF16), pe)


def _softmax_rows(s):
    m = jnp.max(s, axis=-1, keepdims=True)
    m = jnp.where(m == NEG_INF, 0.0, m)
    p = jnp.exp(s - m)
    return p, jnp.sum(p, axis=-1, keepdims=True)


def _slc_scores(psum, width, n_slc):
    ncmp = psum.shape[1]
    d = _iota2((ncmp, width), 0) - (SEL_BLOCK // CMP_STRIDE) * _iota2((ncmp, width), 1)
    wgt = jnp.where((d == -1) | (d == 3), 1.0, jnp.where((d >= 0) & (d <= 2), 2.0, 0.0))
    wgt = jnp.where(_iota2((ncmp, width), 1) < n_slc, wgt, 0.0).astype(BF16)
    p_hi = psum.astype(BF16)
    p_lo = (psum - p_hi.astype(F32)).astype(BF16)
    return _dot(p_hi, wgt) + _dot(p_lo, wgt)


def _top_blocks(slc, cur, n_pick):
    rows, width = slc.shape
    blk = _iota2((rows, width), 1)
    forced = (blk == 0) | (blk == cur) | (blk == cur - 1)
    score = jnp.where(forced, jnp.inf, slc)
    score = jnp.where(blk > cur, NEG_INF, score)
    blk_f = blk.astype(F32)
    lane = _iota2((rows, LANES), 1)
    sel = jnp.zeros((rows, width), F32)
    picks = jnp.zeros((rows, LANES), F32)
    for i in range(n_pick):
        mx = jnp.max(score, axis=-1, keepdims=True)
        first = jnp.min(jnp.where(score == mx, blk_f, float(width)), axis=-1, keepdims=True)
        pick = blk_f == first
        sel = jnp.where(pick, 1.0, sel)
        picks = jnp.where(lane == i, first, picks)
        score = jnp.where(pick, NEG_INF, score)
    return sel, picks


def _member_by_rank(psum, tpos_row, n_slc, n_pick):
    nq, ncmp = psum.shape
    nb = -(-n_slc // 8) * 8
    d = _iota2((nb, ncmp), 1) - (SEL_BLOCK // CMP_STRIDE) * _iota2((nb, ncmp), 0)
    wgt = jnp.where((d == -1) | (d == 3), 1.0, jnp.where((d >= 0) & (d <= 2), 2.0, 0.0))
    wgt = jnp.where(_iota2((nb, ncmp), 0) < n_slc, wgt, 0.0).astype(BF16)
    p_hi = psum.astype(BF16)
    p_lo = (psum - p_hi.astype(F32)).astype(BF16)
    slc = _dot_nt(wgt, p_hi) + _dot_nt(wgt, p_lo)
    blk = _iota2((nb, nq), 0)
    cur = jnp.right_shift(tpos_row, SEL_SHIFT)
    forced = (blk == 0) | (blk == cur) | (blk == cur - 1)
    score = jnp.where(forced, jnp.inf, slc)
    score = jnp.where(blk > cur, NEG_INF, score)
    ahead = jnp.zeros((nb, nq), F32)
    for i in range(n_slc):
        s_i = score[i:i + 1, :]
        ahead = ahead + jnp.where((s_i > score) | ((s_i == score) & (blk > i)), 1.0, 0.0)
    return jnp.where((ahead < n_pick) & (blk <= cur), 1.0, 0.0)


NEAR_COLS = 2 * Q_BLOCK


def _banded_attention(q, k_ref, v_ref, start, width, mask, near_bias):
    far = width - NEAR_COLS
    s_far = _dot_nt(q, k_ref[pl.ds(start, far), :]) + mask[:, :far]
    s_near = _dot_nt(q, k_ref[pl.ds(start + far, NEAR_COLS), :]) + near_bias + mask[:, far:]
    m = jnp.maximum(jnp.max(s_far, axis=-1, keepdims=True), jnp.max(s_near, axis=-1, keepdims=True))
    m = jnp.where(m == NEG_INF, 0.0, m)
    p_far, p_near = jnp.exp(s_far - m), jnp.exp(s_near - m)
    l = jnp.sum(p_far, axis=-1, keepdims=True) + jnp.sum(p_near, axis=-1, keepdims=True)
    o = (_dot(p_far.astype(BF16), v_ref[pl.ds(start, far), :])
         + _dot(p_near.astype(BF16), v_ref[pl.ds(start + far, NEAR_COLS), :]))
    return o / jnp.maximum(l, TINY)


def _nsa_prompt_body(q_ref, zb_ref, gb_ref, bg_ref, ks_ref, vs_ref, kw_ref, vw_ref, kc_ref, vc_ref,
                     bc_ref, bn_ref, o_ref, ksp, vsp, kwp, vwp, osel, *, T):
    qi = pl.program_id(2)
    tq = Q_BLOCK
    front = T - tq
    wlen = WINDOW + tq
    n_slc = T // SEL_BLOCK

    @pl.when(qi == 0)
    def _():
        ksp[0:front, :] = jnp.zeros((front, NSA_HD), BF16)
        vsp[0:front, :] = jnp.zeros((front, NSA_HD), BF16)
        ksp[front:front + T, :] = ks_ref[...].astype(BF16)
        vsp[front:front + T, :] = vs_ref[...].astype(BF16)
        kwp[0:WINDOW, :] = jnp.zeros((WINDOW, NSA_HD), BF16)
        vwp[0:WINDOW, :] = jnp.zeros((WINDOW, NSA_HD), BF16)
        kwp[WINDOW:WINDOW + T, :] = kw_ref[...].astype(BF16)
        vwp[WINDOW:WINDOW + T, :] = vw_ref[...].astype(BF16)

    t0 = pl.multiple_of(qi * tq, tq)
    tpos = _iota2((tq, 1), 0) + t0
    q_all = q_ref[...] * (NSA_HD ** -0.5)
    qs = [q_all[:, g * NSA_HD:(g + 1) * NSA_HD].astype(BF16) for g in range(NSA_G)]

    ncmp = T // CMP_STRIDE
    vis = tpos >= _iota2((1, ncmp), 1) * CMP_STRIDE + (CMP_BLOCK - 1)
    kcb = kc_ref[...].astype(BF16)
    vcb = vc_ref[...].astype(BF16)
    psum = jnp.zeros((tq, ncmp), F32)
    o_cmp = []
    for g in range(NSA_G):
        s = jnp.where(vis, _dot_nt(qs[g], kcb) + bc_ref[g], NEG_INF)
        p, l = _softmax_rows(s)
        p = p / jnp.maximum(l, TINY)
        psum = psum + p
        o_cmp.append(_dot(p.astype(BF16), vcb))

    member_t = _member_by_rank(psum, _iota2((1, tq), 1) + t0, n_slc, min(N_SEL, n_slc)).astype(BF16)

    nb = member_t.shape[0]
    n_win = SEL_WINDOWS if T % (SEL_WINDOWS * tq) == 0 else 1
    for i in range(n_win):
        w_prev, w = T * i // n_win, T * (i + 1) // n_win

        @pl.when((qi >= w_prev // tq) & (qi < w // tq))
        def _(w=w):
            off = T - w
            col_blk = (jnp.right_shift(_iota2((nb, w), 1) + off, SEL_SHIFT)
                       + (qi * (tq // SEL_BLOCK) + (tq - T) // SEL_BLOCK))
            expand = (col_blk == _iota2((nb, w), 0)).astype(BF16)
            kpos = _iota2((1, w), 1) + (t0 + tq - w)
            allowed = (_dot_tn(member_t, expand) > 0.5) & (kpos <= tpos)
            mask_s = jnp.where(allowed, 0.0, NEG_INF)
            for g in range(NSA_G):
                osel[g] = _banded_attention(qs[g], ksp, vsp, t0 + off, w, mask_s, bn_ref[g])

    dist = WINDOW + _iota2((tq, wlen), 0) - _iota2((tq, wlen), 1)
    in_win = (dist >= 0) & (dist < WINDOW) & (_iota2((1, wlen), 1) + (t0 - WINDOW) >= 0)
    mask_w = jnp.where(in_win, 0.0, NEG_INF)
    gate = jax.nn.sigmoid(gb_ref[...] + bg_ref[...])
    zb = _silu(zb_ref[...])
    for g in range(NSA_G):
        o_win = _banded_attention(qs[g], kwp, vwp, t0, wlen, mask_w, bn_ref[g])
        head = pl.program_id(1) * NSA_G + g
        mix = (_lane_col(gate, head) * o_cmp[g] + _lane_col(gate, NSA_HEADS + head) * osel[g]
               + _lane_col(gate, 2 * NSA_HEADS + head) * o_win)
        sl = slice(g * NSA_HD, (g + 1) * NSA_HD)
        o_ref[:, sl] = (mix * zb[:, sl]).astype(o_ref.dtype)


def _nsa_prompt_call(ya, yb, kv16, kcmp, vcmp, bg_r, bias_c, bias_near, *, B, T):
    nq = T // Q_BLOCK
    gw = NSA_G * NSA_HD
    kv_spec = pl.BlockSpec((T, NSA_HD), lambda b, h, i: (b, h))
    cmp_spec = pl.BlockSpec((None, None, T // CMP_STRIDE, NSA_HD), lambda b, h, i: (b, h, 0, 0))
    return pl.pallas_call(
        functools.partial(_nsa_prompt_body, T=T),
        grid=(B, NSA_KVH, nq),
        in_specs=[pl.BlockSpec((Q_BLOCK, gw), lambda b, h, i: (b * nq + i, EVEN_A["qb"] // gw + h)),
                  pl.BlockSpec((Q_BLOCK, gw), lambda b, h, i: (b * nq + i, EVEN_B["zb"] // gw + h)),
                  pl.BlockSpec((Q_BLOCK, LANES), lambda b, h, i: (b * nq + i, EVEN_B["gb"] // LANES)),
                  pl.BlockSpec((1, LANES), lambda b, h, i: (0, 0)),
                  kv_spec, kv_spec, kv_spec, kv_spec, cmp_spec, cmp_spec,
                  pl.BlockSpec((None, NSA_G, Q_BLOCK, T // CMP_STRIDE), lambda b, h, i: (h, 0, i, 0)),
                  pl.BlockSpec((None, NSA_G, Q_BLOCK, NEAR_COLS), lambda b, h, i: (h, 0, 0, 0))],
        out_specs=pl.BlockSpec((Q_BLOCK, gw), lambda b, h, i: (b * nq + i, h)),
        out_shape=jax.ShapeDtypeStruct((B * T, NSA_W), BF16),
        scratch_shapes=[pltpu.VMEM((2 * T - Q_BLOCK, NSA_HD), BF16), pltpu.VMEM((2 * T - Q_BLOCK, NSA_HD), BF16),
                        pltpu.VMEM((WINDOW + T, NSA_HD), BF16), pltpu.VMEM((WINDOW + T, NSA_HD), BF16),
                        pltpu.VMEM((NSA_G, Q_BLOCK, NSA_HD), F32)],
        compiler_params=_params(("arbitrary", "arbitrary", "arbitrary")),
        name="nsa_prompt",
    )(ya, yb, yb, bg_r, *kv16, kcmp, vcmp, bias_c, bias_near)


CMP_PAGES = 16
CHUNKS_PER_PAGE = PAGE_SIZE // CMP_STRIDE
PAGE_ROWS = PAGE_SIZE * NSA_KVH


def _pool_rows(pool):
    return pool.reshape(pool.shape[0] * PAGE_ROWS, NSA_HD)


def _cmp_pages_body(pt_ref, *refs):
    del pt_ref
    pages = refs[:CMP_PAGES]
    w_ref, pe_ref, o_ref = refs[CMP_PAGES:]
    rows = CMP_PAGES * CHUNKS_PER_PAGE
    per_head = [jnp.concatenate(
        [jnp.concatenate([pg[pl.ds(NSA_KVH * s + h, CHUNKS_PER_PAGE, stride=CMP_STRIDE * NSA_KVH), :]
                          for s in range(CMP_STRIDE)], axis=1) for pg in pages], axis=0) for h in range(NSA_KVH)]
    w = w_ref[...]
    r = _dot(jnp.concatenate(per_head, axis=0).astype(BF16), w)
    pc = _dot(pe_ref[...], w)
    r = r + jnp.concatenate([pc[0:1, :NSA_HD], pc[1:2, NSA_HD:]], axis=1)
    for h in range(NSA_KVH):
        o_ref[h] = r[h * rows:(h + 1) * rows]


def _cmp_pages_call(pool, page_table, w1, pe, *, B):
    n_pages = page_table.shape[1]
    rows = CMP_PAGES * CHUNKS_PER_PAGE
    view = _pool_rows(pool)
    w = w1.reshape(2, CMP_STRIDE, NSA_HD, NSA_HD).transpose(1, 2, 0, 3).reshape(CMP_STRIDE * NSA_HD, 2 * NSA_HD)
    pe_rows = jnp.pad(pe.reshape(2, CMP_STRIDE * NSA_HD), ((0, 6), (0, 0))).astype(BF16)

    def page_spec(i):
        return pl.BlockSpec((PAGE_ROWS, NSA_HD), lambda b, s, pt: (pt[b * n_pages + s * CMP_PAGES + i], 0))

    grid_spec = pltpu.PrefetchScalarGridSpec(
        num_scalar_prefetch=1,
        grid=(B, n_pages // CMP_PAGES),
        in_specs=[page_spec(i) for i in range(CMP_PAGES)]
        + [pl.BlockSpec((CMP_STRIDE * NSA_HD, 2 * NSA_HD), lambda b, s, pt: (0, 0)),
           pl.BlockSpec((8, CMP_STRIDE * NSA_HD), lambda b, s, pt: (0, 0))],
        out_specs=pl.BlockSpec((None, NSA_KVH, rows, 2 * NSA_HD), lambda b, s, pt: (b, 0, s, 0)),
    )
    return pl.pallas_call(
        _cmp_pages_body,
        grid_spec=grid_spec,
        out_shape=jax.ShapeDtypeStruct((B, NSA_KVH, n_pages * CHUNKS_PER_PAGE, 2 * NSA_HD), F32),
        compiler_params=_params(("arbitrary", "arbitrary")),
        name="nsa_cmp_pages",
    )(page_table.reshape(-1), *([view] * CMP_PAGES), w.astype(BF16), pe_rows)


SEL_WINDOWS = 4
SLC_LANES = 384


def _sample_q_rows(q_ref):
    q = q_ref[...] * (NSA_HD ** -0.5)
    return jnp.concatenate([q[:, g * NSA_HD:(g + 1) * NSA_HD] for g in range(NSA_G)], axis=0).astype(BF16)


def _nsa_sample_main_body(abk_ref, abv_ref, b1_ref, w2_ref, q_ref, wk_ref, wv_ref, kn_ref, vn_ref, bc_ref, bw_ref,
                          ocmp_ref, owin_ref, idx_ref, *, T, n_slc):
    tp = SAMPLE_PAD_T
    rows = NSA_G * tp
    ncmp = abk_ref.shape[0]

    def compressed(ab_ref, t):
        ab = ab_ref[...]
        h = ab[:, :NSA_HD] + pltpu.roll(ab[:, NSA_HD:], ncmp - 1, 0) + b1_ref[t]
        return _dot(_gelu_tanh(h).astype(BF16), w2_ref[t]).astype(BF16)

    kc, vc = compressed(abk_ref, 0), compressed(abv_ref, 1)
    q = _sample_q_rows(q_ref)
    step = jnp.bitwise_and(_iota2((rows, 1), 0), tp - 1)
    tpos = PAST_LEN + step
    vis = tpos >= _iota2((1, ncmp), 1) * CMP_STRIDE + (CMP_BLOCK - 1)
    p, l = _softmax_rows(jnp.where(vis, _dot_nt(q, kc) + bc_ref[...], NEG_INF))
    p = p / jnp.maximum(l, TINY)
    ocmp_ref[...] = _dot(p.astype(BF16), vc)
    psum = p[0:tp]
    for g in range(1, NSA_G):
        psum = psum + p[g * tp:(g + 1) * tp]
    cur = jnp.right_shift(PAST_LEN + _iota2((tp, 1), 0), SEL_SHIFT)
    _, picks = _top_blocks(_slc_scores(psum, SLC_LANES, n_slc), cur, N_SEL)
    idx_ref[...] = picks.astype(jnp.int32)

    wb = wk_ref.shape[0] // NSA_KVH
    wlen = bw_ref.shape[1]
    fill = jnp.zeros((wlen - wb - tp, NSA_HD), BF16)
    head = pl.program_id(1)
    k_all = jnp.concatenate([wk_ref[pl.ds(head, wb, stride=NSA_KVH), :].astype(BF16), kn_ref[...], fill], axis=0)
    v_all = jnp.concatenate([wv_ref[pl.ds(head, wb, stride=NSA_KVH), :].astype(BF16), vn_ref[...], fill], axis=0)
    col = _iota2((1, wlen), 1)
    dist = tpos - (PAST_LEN - wb + col)
    in_win = (dist >= 0) & (dist < WINDOW) & (col < wb + T)
    pw, lw = _softmax_rows(jnp.where(in_win, _dot_nt(q, k_all) + bw_ref[...], NEG_INF))
    owin_ref[...] = _dot(pw.astype(BF16), v_all) / jnp.maximum(lw, TINY)


def _nsa_sample_main_call(ya, kw16, vw16, abk, abv, b1, w2, wk, wv, bias_c, bias_w, *, B, T):
    tp = SAMPLE_PAD_T
    rows = NSA_G * tp
    gw = NSA_G * NSA_HD
    ncmp = abk.shape[2]
    wb = wk.shape[1]
    wlen = bias_w.shape[-1]
    n_slc = -(-(PAST_LEN + T) // SEL_BLOCK)
    assert n_slc <= SLC_LANES and T <= tp
    ab_spec = pl.BlockSpec((None, None, ncmp, 2 * NSA_HD), lambda b, h: (b, h, 0, 0))
    win_spec = pl.BlockSpec((wb * NSA_KVH, NSA_HD), lambda b, h: (b, 0))
    o_spec = pl.BlockSpec((None, None, rows, NSA_HD), lambda b, h: (b, h, 0, 0))
    return pl.pallas_call(
        functools.partial(_nsa_sample_main_body, T=T, n_slc=n_slc),
        grid=(B, NSA_KVH),
        in_specs=[ab_spec, ab_spec,
                  pl.BlockSpec((2, 1, NSA_HD), lambda b, h: (0, 0, 0)),
                  pl.BlockSpec((2, NSA_HD, NSA_HD), lambda b, h: (0, 0, 0)),
                  pl.BlockSpec((tp, gw), lambda b, h: (b, EVEN_A["qb"] // gw + h)),
                  win_spec, win_spec,
                  pl.BlockSpec((tp, NSA_HD), lambda b, h: (b, h)),
                  pl.BlockSpec((tp, NSA_HD), lambda b, h: (b, h)),
                  pl.BlockSpec((None, rows, ncmp), lambda b, h: (h, 0, 0)),
                  pl.BlockSpec((None, rows, wlen), lambda b, h: (h, 0, 0))],
        out_specs=[o_spec, o_spec, pl.BlockSpec((None, None, tp, LANES), lambda b, h: (b, h, 0, 0))],
        out_shape=[jax.ShapeDtypeStruct((B, NSA_KVH, rows, NSA_HD), F32),
                   jax.ShapeDtypeStruct((B, NSA_KVH, rows, NSA_HD), F32),
                   jax.ShapeDtypeStruct((B, NSA_KVH, tp, LANES), jnp.int32)],
        compiler_params=_params(("parallel", "parallel")),
        name="nsa_sample_main",
    )(abk, abv, b1.reshape(2, 1, NSA_HD), w2.astype(BF16), ya,
      wk.reshape(B * wb * NSA_KVH, NSA_HD), wv.reshape(B * wb * NSA_KVH, NSA_HD), kw16, vw16, bias_c, bias_w)


NEAR_BLOCKS = 3


def _nsa_sample_sel_body(idx_ref, pt_ref, q_ref, kn_ref, vn_ref, tbl_ref, ocmp_ref, owin_ref, gb_ref, bg_ref, zb_ref,
                         *refs, T):
    del pt_ref
    k_blocks = refs[:N_SEL]
    v_blocks = refs[N_SEL:2 * N_SEL]
    o_ref, osel = refs[2 * N_SEL:]
    tp = SAMPLE_PAD_T
    rows = NSA_G * tp
    b, h, t = pl.program_id(0), pl.program_id(1), pl.program_id(2)
    base = ((b * NSA_KVH + h) * T + t) * N_SEL
    first_new = PAST_LEN // SEL_BLOCK
    cur = jnp.right_shift(PAST_LEN + t, SEL_SHIFT)
    q = _sample_q_rows(q_ref)
    pad = jnp.zeros((SEL_BLOCK - tp, NSA_HD), BF16)
    k_new = jnp.concatenate([kn_ref[...], pad], axis=0)
    v_new = jnp.concatenate([vn_ref[...], pad], axis=0)
    lane = _iota2((1, LANES), 1)
    low = lane < SEL_BLOCK
    within = jnp.bitwise_and(lane, SEL_BLOCK - 1)
    ks, vs, bias, kpos = [], [], [], []
    for i in range(0, N_SEL, 2):
        pair_bias, pair_pos = [], []
        for j in (i, i + 1):
            blk = idx_ref[base + j]
            is_new = blk >= first_new
            ks.append(jnp.where(is_new, k_new, k_blocks[j][pl.ds(h, SEL_BLOCK, stride=NSA_KVH), :].astype(BF16)))
            vs.append(jnp.where(is_new, v_new, v_blocks[j][pl.ds(h, SEL_BLOCK, stride=NSA_KVH), :].astype(BF16)))
            pair_bias.append(tbl_ref[jnp.clip(blk - (first_new - NEAR_BLOCKS), 0, NEAR_BLOCKS)])
            pair_pos.append(jnp.where(blk <= cur, blk * SEL_BLOCK, PAST_LEN + SEL_BLOCK * LANES) + within)
        bias.append(jnp.where(low, pair_bias[0], pair_bias[1]))
        kpos.append(jnp.where(low, pair_pos[0], pair_pos[1]))
    k_all = jnp.concatenate(ks, axis=0)
    v_all = jnp.concatenate(vs, axis=0)
    step = jnp.bitwise_and(_iota2((rows, 1), 0), tp - 1)
    ok = jnp.concatenate(kpos, axis=1) <= PAST_LEN + step
    p, l = _softmax_rows(jnp.where(ok, _dot_nt(q, k_all) + jnp.concatenate(bias, axis=1), NEG_INF))
    o = _dot(p.astype(BF16), v_all) / jnp.maximum(l, TINY)

    @pl.when(t == 0)
    def _():
        osel[...] = jnp.zeros_like(osel)

    osel[...] = jnp.where(step == t, o, osel[...])

    @pl.when(t == T - 1)
    def _():
        gate = jax.nn.sigmoid(gb_ref[...] + bg_ref[...])
        zb = _silu(zb_ref[...])
        for g in range(NSA_G):
            r = slice(g * tp, (g + 1) * tp)
            head = h * NSA_G + g
            mix = (_lane_col(gate, head) * ocmp_ref[r, :] + _lane_col(gate, NSA_HEADS + head) * osel[r, :]
                   + _lane_col(gate, 2 * NSA_HEADS + head) * owin_ref[r, :])
            sl = slice(g * NSA_HD, (g + 1) * NSA_HD)
            o_ref[:, sl] = (mix * zb[:, sl]).astype(o_ref.dtype)


def _nsa_sample_sel_call(ya, yb, ks16, vs16, idx, page_table, pool_k, pool_v, tbl, o_cmp, o_win, bg_r, *, B, T):
    tp = SAMPLE_PAD_T
    rows = NSA_G * tp
    gw = NSA_G * NSA_HD
    n_pages = page_table.shape[1]
    halves = PAGE_SIZE // SEL_BLOCK
    idx_flat = idx[:, :, :T, :N_SEL].reshape(-1)
    view_k, view_v = _pool_rows(pool_k), _pool_rows(pool_v)

    def blk_spec(j):
        def index(b, h, t, idx_s, pt_s):
            blk = idx_s[((b * NSA_KVH + h) * T + t) * N_SEL + j]
            page = pt_s[b * n_pages + jnp.minimum(blk // halves, n_pages - 1)]
            return (page * halves + blk % halves, 0)
        return pl.BlockSpec((SEL_BLOCK * NSA_KVH, NSA_HD), index)

    o_spec = pl.BlockSpec((None, None, rows, NSA_HD), lambda b, h, t, *_: (b, h, 0, 0))
    grid_spec = pltpu.PrefetchScalarGridSpec(
        num_scalar_prefetch=2,
        grid=(B, NSA_KVH, T),
        in_specs=[pl.BlockSpec((tp, gw), lambda b, h, t, *_: (b, EVEN_A["qb"] // gw + h)),
                  pl.BlockSpec((tp, NSA_HD), lambda b, h, t, *_: (b, h)),
                  pl.BlockSpec((tp, NSA_HD), lambda b, h, t, *_: (b, h)),
                  pl.BlockSpec((None, NEAR_BLOCKS + 1, rows, LANES), lambda b, h, t, *_: (h, 0, 0, 0)),
                  o_spec, o_spec,
                  pl.BlockSpec((tp, LANES), lambda b, h, t, *_: (b, EVEN_B["gb"] // LANES)),
                  pl.BlockSpec((1, LANES), lambda b, h, t, *_: (0, 0)),
                  pl.BlockSpec((tp, gw), lambda b, h, t, *_: (b, EVEN_B["zb"] // gw + h))]
        + [blk_spec(j) for j in range(N_SEL)] * 2,
        out_specs=pl.BlockSpec((tp, gw), lambda b, h, t, *_: (b, h)),
        scratch_shapes=[pltpu.VMEM((rows, NSA_HD), F32)],
    )
    return pl.pallas_call(
        functools.partial(_nsa_sample_sel_body, T=T),
        grid_spec=grid_spec,
        out_shape=jax.ShapeDtypeStruct((B * tp, NSA_W), BF16),
        compiler_params=_params(("arbitrary", "arbitrary", "arbitrary")),
        name="nsa_sample_sel",
    )(idx_flat, page_table.reshape(-1), ya, ks16, vs16, tbl, o_cmp, o_win, yb, bg_r, yb,
      *([view_k] * N_SEL), *([view_v] * N_SEL))


def _sample_bias_tables(rel_bias, T, wb):
    tp = SAMPLE_PAD_T
    ncmp = PAST_LEN // CMP_STRIDE
    wlen = -(-(wb + tp) // LANES) * LANES
    first = PAST_LEN // SEL_BLOCK - NEAR_BLOCKS
    assert PAST_LEN - ((first + 1) * SEL_BLOCK - 1) >= REL_MAX_DIST
    lo, hi = -wlen, PAST_LEN + tp
    rev = _bias_line(rel_bias, lo, hi, descending=True)

    def rows(tbl):
        return tbl.reshape(NSA_KVH, NSA_G * tp, tbl.shape[-1])

    t_c = _toeplitz(rev, hi - 1 - (PAST_LEN - (CMP_BLOCK - 1)), tp, CMP_STRIDE * ncmp)[:, :, ::CMP_STRIDE]
    t_w = _toeplitz(rev, hi - 1 - wb, tp, wlen)
    far = jnp.broadcast_to(rev[:, hi - 1 - REL_MAX_DIST][:, None, None], (NSA_HEADS, tp, LANES))
    near = []
    for k in range(1, NEAR_BLOCKS + 1):
        half = _toeplitz(rev, hi - 1 - (PAST_LEN - (first + k) * SEL_BLOCK), tp, SEL_BLOCK)
        near.append(jnp.concatenate([half, half], axis=-1))
    t_s = jnp.stack([far] + near, axis=1).reshape(NSA_KVH, NSA_G, NEAR_BLOCKS + 1, tp, LANES)
    t_s = t_s.transpose(0, 2, 1, 3, 4).reshape(NSA_KVH, NEAR_BLOCKS + 1, NSA_G * tp, LANES)
    return rows(t_c), rows(t_w), t_s


def _tail_even(w):
    return _tail_relayout(w, EVEN_KV_OFF + 6 * NSA_KV_W, 3 * NSA_HEADS, NSA_W + MEM_W, EVEN_B_N)


def _tail_odd(w):
    return _tail_relayout(w, ODD_A_N, 2 * ML_HEADS, ML_V_W + MEM_W, ODD_B_N)


def _gate_bias_even(b_gate):
    return jnp.pad(b_gate, (0, LANES - 3 * NSA_HEADS)).reshape(1, LANES)


def _gate_bias_odd(b_if):
    return jnp.pad(b_if.reshape(2 * ML_HEADS), (0, LANES - 2 * ML_HEADS)).reshape(1, LANES)


def _rel_bucket(dist):
    n = np.maximum(dist, 0)
    exact = REL_BUCKETS // 2
    nf = np.maximum(n, 1).astype(np.float32)
    large = exact + (np.log(nf / exact) / math.log(REL_MAX_DIST / exact) * (REL_BUCKETS - exact)).astype(np.int32)
    return np.where(n < exact, n, np.minimum(large, REL_BUCKETS - 1))


def _bias_line(rel_bias, lo, hi, descending=False):
    dist = np.arange(hi - 1, lo - 1, -1) if descending else np.arange(lo, hi)
    buckets = _rel_bucket(dist)
    edges = np.flatnonzero(np.diff(buckets)) + 1
    starts = np.concatenate([[0], edges])
    ends = np.concatenate([edges, [hi - lo]])
    bias_t = rel_bias.T.astype(F32)
    runs = [jnp.broadcast_to(bias_t[:, int(buckets[s])][:, None], (NSA_HEADS, int(e - s))) for s, e in zip(starts, ends)]
    return jnp.concatenate(runs, axis=1)


def _skew_rows(v, rows, step, cols):
    n = v.shape[1]
    reps = -(-rows * (n + step) // n)
    return jnp.tile(v, (1, reps))[:, :rows * (n + step)].reshape(v.shape[0], rows, n + step)[:, :, :cols]


def _toeplitz(rev, start, rows, cols):
    seg = rev[:, start - (rows - 1):start + cols]
    return _skew_rows(jnp.roll(seg, -(rows - 1), axis=1), rows, -1, cols)


def _prompt_bias_tables(rel_bias, T):
    ncmp = T // CMP_STRIDE
    assert Q_BLOCK + 1 >= REL_MAX_DIST
    lo, hi = -(CMP_STRIDE * ncmp + CMP_BLOCK), T
    line = _bias_line(rel_bias, lo, hi)
    rev = _bias_line(rel_bias, lo, hi, descending=True)

    def split(tbl):
        return tbl.reshape((NSA_KVH, NSA_G) + tbl.shape[1:])

    back = CMP_STRIDE * (ncmp - 1)
    first = -(back + CMP_BLOCK - 1) - lo
    seg = line[:, first:first + T + back]
    t_c = _skew_rows(jnp.roll(seg, -back, axis=1), ncmp, -CMP_STRIDE, T).swapaxes(1, 2)
    far = rev[:, hi - 1 - REL_MAX_DIST]
    t_near = _toeplitz(rev, hi - 1 - Q_BLOCK, Q_BLOCK, NEAR_COLS) - far[:, None, None]
    return split(t_c), split(t_near)


def _nsa_sample(ya, yb, kv16, page_table, pk_cmp, pv_cmp, pk_sel, pv_sel, wk, wv, bg_r, w1, b1, w2, pe, rel_bias,
                *, B, T):
    assert (PAST_LEN + T) // CMP_STRIDE == PAST_LEN // CMP_STRIDE
    abk = _cmp_pages_call(pk_cmp, page_table, w1[0], pe[0], B=B)
    abv = _cmp_pages_call(pv_cmp, page_table, w1[1], pe[1], B=B)
    bias_c, bias_w, tbl = _sample_bias_tables(rel_bias, T, wk.shape[1])
    o_cmp, o_win, idx = _nsa_sample_main_call(ya, kv16[4], kv16[5], abk, abv, b1, w2, wk, wv, bias_c, bias_w, B=B, T=T)
    return _nsa_sample_sel_call(ya, yb, kv16[2], kv16[3], idx, page_table, pk_sel, pv_sel, tbl, o_cmp, o_win, bg_r,
                                B=B, T=T)


def _kv_project(x, wt):
    outs = [_matmul_heads(x, wt, first=EVEN_KV_OFF + j * NSA_KV_W, transposed=True) for j in range(6)]
    return [o[0] for o in outs], [o[1] for o in outs]


def _even_prompt(hp2d, npre, mk16, mv16, wt, wt_b, bg_r, w1, b1, w2, pe, lb, g_norm, w_out, rel_bias, *, B, T):
    ya, yb = _matmul_nt(npre, wt, tm=W_TILE_M, tn=W_TILE_N, rows=(0, EVEN_A_N)), _matmul_nt(npre, wt_b)
    kv32, kv16 = _kv_project(npre, wt)
    oa, s_new = _hgrn_call(ya, jnp.zeros((B, HG_HEADS, HG_DK, HG_DV), F32), lb, g_norm, B=B, T=T, L=CHUNK, valid=CHUNK)
    kcmp = _compress_call(kv16[0], w1[0], b1[0], w2[0], pe[0], B=B, T=T)
    vcmp = _compress_call(kv16[1], w1[1], b1[1], w2[1], pe[1], B=B, T=T)
    ob = _nsa_prompt_call(ya, yb, kv16[2:], kcmp, vcmp, bg_r, *_prompt_bias_tables(rel_bias, T), B=B, T=T)
    om = _mem_call(yb, EVEN_B["qm"], mk16, mv16, B=B, T=T)
    h_new = _outproj([oa, ob, om], w_out, hp2d)
    wb = min(WINDOW, T)
    rows = [r.reshape(B, T, NSA_KVH, NSA_HD) for r in kv32]
    return h_new, (rows[0], rows[1], rows[2], rows[3], rows[4][:, -wb:], rows[5][:, -wb:], s_new)


def _even_sample(hs2d, nsam, mk_s, mv_s, page_table, pk_cmp, pv_cmp, pk_sel, pv_sel, wk, wv, s0,
                 wt, wt_b, bg_r, w1, b1, w2, pe, lb, g_norm, w_out, rel_bias, *, B, T):
    tp = SAMPLE_PAD_T
    ya, yb = _matmul_nt(nsam, wt, tm=W_TILE_M, tn=W_TILE_N, rows=(0, EVEN_A_N)), _matmul_nt(nsam, wt_b)
    kv32, kv16 = _kv_project(nsam, wt)
    oa, s_new = _hgrn_call(ya, s0, lb, g_norm, B=B, T=tp, L=tp, valid=T)
    ob = _nsa_sample(ya, yb, kv16, page_table, pk_cmp, pv_cmp, pk_sel, pv_sel, wk, wv, bg_r, w1, b1, w2, pe, rel_bias,
                     B=B, T=T)
    om = _mem_call(yb, EVEN_B["qm"], mk_s.reshape(B * N_MEM, MEM_W), mv_s.reshape(B * N_MEM, MEM_W), B=B, T=tp)
    rows = [r.reshape(B, tp, NSA_KVH, NSA_HD)[:, :T] for r in kv32]
    wb = wk.shape[1]
    win_k = jnp.concatenate([wk, rows[4]], axis=1)[:, -wb:]
    win_v = jnp.concatenate([wv, rows[5]], axis=1)[:, -wb:]
    return _outproj([oa, ob, om], w_out, hs2d), (rows[0], rows[1], rows[2], rows[3], win_k, win_v, s_new)


def _odd_mix(h2d, hn, k2d, v2d, c0, n0, m0, wt, wt_b, bif_r, g_norm, w_out, *, B, T, L, valid):
    ya, yb = _matmul_nt(hn, wt, tm=W_TILE_M, tn=W_TILE_N, rows=(0, ODD_A_N)), _matmul_nt(hn, wt_b)
    h, c_new, n_new, m_new = _mlstm_call(ya, yb, c0, n0, m0, bif_r, g_norm, B=B, T=T, L=L, valid=valid)
    om = _mem_call(yb, ODD_B["qm"], k2d, v2d, B=B, T=T)
    return _outproj([h, om], w_out, h2d), (c_new, n_new, m_new)


def _stack(lst, i):
    return jnp.stack([t[i] for t in lst])


def kernel(x_prompt, x_sample, cache_mem_k, cache_mem_v, cache_cmp_k, cache_cmp_v, cache_sel_k, cache_sel_v,
           cache_win_k, cache_win_v, state_hgrn, state_mlstm_c, state_mlstm_n, state_mlstm_m, page_table,
           mem_prompt, norm_w, mem_norm_w, final_norm_w, rel_bias, w_mem_kv, w_in_even, b_nsa_gate,
           w_cmp1, b_cmp1, w_cmp2, pe_cmp, hgrn_lb_logits, hgrn_norm_w, w_out_even, w_in_odd, b_mlstm_if,
           mlstm_norm_w, w_out_odd):
    bp, tp = x_prompt.shape[:2]
    bs, ts = x_sample.shape[:2]
    tsp = SAMPLE_PAD_T
    lbs = jnp.cumsum(jax.nn.softmax(hgrn_lb_logits.astype(F32), axis=0), axis=0)
    hp = x_prompt.reshape(bp * tp, D_MODEL)
    hs = jnp.pad(x_sample, ((0, 0), (0, tsp - ts), (0, 0))).reshape(bs * tsp, D_MODEL)
    mem2d = mem_prompt.reshape(bp * N_MEM, D_MODEL)
    mem_new, even_p, even_s, odd_p, odd_s = [], [], [], [], []
    for l in range(DEPTH):
        npre = _rmsnorm_rows(hp, norm_w[l], BF16)
        nsam = _rmsnorm_rows(hs, norm_w[l], BF16)
        nmem = _rmsnorm_rows(mem2d, mem_norm_w[l], BF16)
        mk32, mk16 = _matmul_heads(nmem, w_mem_kv[l], first=0)
        mv32, mv16 = _matmul_heads(nmem, w_mem_kv[l], first=MEM_W)
        mem_new.append((mk32.reshape(bp, N_MEM, MEM_HEADS, MEM_HD), mv32.reshape(bp, N_MEM, MEM_HEADS, MEM_HD)))
        mk_s, mv_s = cache_mem_k[l], cache_mem_v[l]
        if l % 2 == 0:
            e = l // 2
            w_in = w_in_even[e].T
            w_b = _tail_even(w_in)
            w_out = w_out_even[e].astype(BF16)
            bg_r = _gate_bias_even(b_nsa_gate[e])
            cmpw = (w_cmp1[e].reshape(2, CMP_BLOCK, NSA_HD, NSA_HD), b_cmp1[e], w_cmp2[e], pe_cmp[e])
            hp, st_p = _even_prompt(hp, npre, mk16, mv16, w_in, w_b, bg_r, *cmpw, lbs[l], hgrn_norm_w[e], w_out,
                                    rel_bias, B=bp, T=tp)
            hs, st_s = _even_sample(hs, nsam, mk_s, mv_s, page_table, cache_cmp_k[e], cache_cmp_v[e], cache_sel_k[e],
                                    cache_sel_v[e], cache_win_k[e], cache_win_v[e], state_hgrn[e], w_in, w_b, bg_r,
                                    *cmpw, lbs[l], hgrn_norm_w[e], w_out, rel_bias, B=bs, T=ts)
            even_p.append(st_p)
            even_s.append(st_s)
        else:
            o = l // 2
            w_in = w_in_odd[o].T
            w_b = _tail_odd(w_in)
            w_out = w_out_odd[o].astype(BF16)
            bif_r = _gate_bias_odd(b_mlstm_if[o])
            hp, st_p = _odd_mix(hp, npre, mk16, mv16, jnp.zeros((bp, ML_HEADS, ML_DV, ML_DK), F32),
                                jnp.zeros((bp, ML_HEADS, ML_DK), F32), jnp.zeros((bp, ML_HEADS), F32),
                                w_in, w_b, bif_r, mlstm_norm_w[o], w_out, B=bp, T=tp, L=ML_CHUNK, valid=ML_CHUNK)
            hs, st_s = _odd_mix(hs, nsam, mk_s.reshape(bs * N_MEM, MEM_W), mv_s.reshape(bs * N_MEM, MEM_W),
                                state_mlstm_c[o], state_mlstm_n[o], state_mlstm_m[o],
                                w_in, w_b, bif_r, mlstm_norm_w[o], w_out, B=bs, T=tsp, L=tsp, valid=ts)
            odd_p.append(st_p)
            odd_s.append(st_s)
    y_prompt = _rmsnorm_rows(hp, final_norm_w, F32).reshape(bp, tp, D_MODEL)
    y_sample = _rmsnorm_rows(hs, final_norm_w, F32).reshape(bs, tsp, D_MODEL)[:, :ts]
    return (y_prompt, y_sample,
            _stack(mem_new, 0), _stack(mem_new, 1),
            _stack(even_p, 0), _stack(even_p, 1), _stack(even_p, 2), _stack(even_p, 3),
            _stack(even_p, 4), _stack(even_p, 5), _stack(even_p, 6),
            _stack(odd_p, 0), _stack(odd_p, 1), _stack(odd_p, 2),
            _stack(even_s, 0), _stack(even_s, 1), _stack(even_s, 2), _stack(even_s, 3),
            _stack(even_s, 4), _stack(even_s, 5), _stack(even_s, 6),
            _stack(odd_s, 0), _stack(odd_s, 1), _stack(odd_s, 2))
```

```python
import functools
import math

import jax
import jax.numpy as jnp
import numpy as np
from jax import lax
from jax.experimental import pallas as pl
from jax.experimental.pallas import tpu as pltpu

D_MODEL = 4096
DEPTH = 2
PAST_LEN = 16384
PAGE_SIZE = 128
N_MEM = 256
EPS = 1e-6
CHUNK = 64

HG_DK = 128
HG_DV = 128
HG_HEADS = D_MODEL // 2 // HG_DV
HG_W = HG_HEADS * HG_DV

NSA_HD = 128
NSA_HEADS = D_MODEL // 2 // NSA_HD
NSA_KVH = 4
NSA_G = NSA_HEADS // NSA_KVH
NSA_W = NSA_HEADS * NSA_HD
NSA_KV_W = NSA_KVH * NSA_HD
CMP_BLOCK = 32
CMP_STRIDE = 16
SEL_BLOCK = 64
SEL_SHIFT = SEL_BLOCK.bit_length() - 1
N_SEL = 16
WINDOW = 512
Q_BLOCK = 128

ML_HEADS = D_MODEL // 512
ML_DK = D_MODEL // 2 // ML_HEADS
ML_DV = D_MODEL // ML_HEADS
ML_QK_W = ML_HEADS * ML_DK
ML_V_W = ML_HEADS * ML_DV

MEM_HEADS = 4
MEM_HD = 128
MEM_W = MEM_HEADS * MEM_HD

REL_BUCKETS = 32
REL_MAX_DIST = 128

F32 = jnp.float32
BF16 = jnp.bfloat16
LANES = 128
NEG_INF = float("-inf")
TINY = float(np.finfo(np.float32).tiny)
EXP_CLAMP = 80.0
VMEM_LIMIT = 56 * 1024 * 1024

HG_HB = 8
ML_HB = 2
ML_CHUNK = 256
W_TILE_M, W_TILE_N = 1024, 512
HG_SUB = 16
SAMPLE_PAD_T = 16

MM_TILE_N = 1024
EVEN_A = {"qa": 0, "fa": HG_W, "ia": 2 * HG_W, "za": 3 * HG_W, "qb": 4 * HG_W}
EVEN_A_N = 4 * HG_W + NSA_W
EVEN_B = {"zb": 0, "qm": NSA_W, "gb": NSA_W + MEM_W}
EVEN_B_N = -(-(NSA_W + MEM_W + LANES) // MM_TILE_N) * MM_TILE_N
EVEN_KV_OFF = EVEN_A_N
ODD_A = {"q": 0, "k": ML_QK_W, "v": 2 * ML_QK_W, "og": 2 * ML_QK_W + ML_V_W}
ODD_A_N = 2 * ML_QK_W + 2 * ML_V_W
ODD_B = {"z": 0, "qm": ML_V_W, "gates": ML_V_W + MEM_W}
ODD_B_N = -(-(ML_V_W + MEM_W + LANES) // MM_TILE_N) * MM_TILE_N


def _dot(a, b):
    return jnp.dot(a, b, preferred_element_type=F32)


def _dot_nt(a, b):
    return lax.dot_general(a, b, (((1,), (1,)), ((), ())), preferred_element_type=F32)


def _dot_tn(a, b):
    return lax.dot_general(a, b, (((0,), (0,)), ((), ())), preferred_element_type=F32)


def _iota2(shape, dim):
    return lax.broadcasted_iota(jnp.int32, shape, dim)


def _cumsum_rows(x, tri_b):
    hi = x.astype(BF16)
    r1 = x - hi.astype(F32)
    mid = r1.astype(BF16)
    lo = (r1 - mid.astype(F32)).astype(BF16)
    return _dot(tri_b, hi) + _dot(tri_b, mid) + _dot(tri_b, lo)


def _row_to_col(row, n):
    eye = _iota2((n, n), 0) == _iota2((n, n), 1)
    return jnp.sum(jnp.where(eye, row, 0.0), axis=1, keepdims=True)


def _col_to_row(col, n):
    eye = _iota2((n, n), 0) == _iota2((n, n), 1)
    return jnp.sum(jnp.where(eye, col, 0.0), axis=0, keepdims=True)


def _lane_col(x, idx):
    return jnp.sum(jnp.where(_iota2(x.shape, 1) == idx, x, 0.0), axis=1, keepdims=True)


def _silu(x):
    return x * jax.nn.sigmoid(x)


def _params(sem):
    return pltpu.CompilerParams(dimension_semantics=sem, vmem_limit_bytes=VMEM_LIMIT)


def _rmsnorm_body(x_ref, w_ref, o_ref):
    x = x_ref[...].astype(F32)
    y = x * lax.rsqrt(jnp.mean(x * x, axis=-1, keepdims=True) + EPS)
    o_ref[...] = (y * w_ref[...].astype(F32)).astype(o_ref.dtype)


def _rmsnorm_rows(x2d, w, out_dtype, tm=256):
    m, d = x2d.shape
    tm = min(tm, m)
    return pl.pallas_call(
        _rmsnorm_body,
        grid=(m // tm,),
        in_specs=[pl.BlockSpec((tm, d), lambda i: (i, 0)), pl.BlockSpec((1, d), lambda i: (0, 0))],
        out_specs=pl.BlockSpec((tm, d), lambda i: (i, 0)),
        out_shape=jax.ShapeDtypeStruct((m, d), out_dtype),
        compiler_params=_params(("parallel",)),
        name="rmsnorm",
    )(x2d, w.reshape(1, d))


def _matmul_nt_body(a_ref, bt_ref, o_ref):
    o_ref[...] = _dot_nt(a_ref[...], bt_ref[...].astype(BF16))


def _matmul_nt(a, bt, tm=1024, tn=MM_TILE_N, rows=None):
    m, k = a.shape
    first, n = rows or (0, bt.shape[0])
    tm, tn = min(tm, m), min(tn, n)
    assert m % tm == 0 and n % tn == 0 and first % tn == 0, (a.shape, bt.shape, rows)
    j0 = first // tn
    return pl.pallas_call(
        _matmul_nt_body,
        grid=(m // tm, n // tn),
        in_specs=[pl.BlockSpec((tm, k), lambda i, j: (i, 0)), pl.BlockSpec((tn, k), lambda i, j: (j0 + j, 0))],
        out_specs=pl.BlockSpec((tm, tn), lambda i, j: (i, j)),
        out_shape=jax.ShapeDtypeStruct((m, n), F32),
        compiler_params=_params(("parallel", "parallel")),
        name="matmul",
    )(a, bt)


def _tail_body(lo_ref, hi_ref, gate_ref, o_ref, *, shift, n_main):
    i = pl.program_id(0)
    main = jnp.concatenate([lo_ref[shift:, :], hi_ref[:shift, :]], axis=0)
    gates = jnp.where(_iota2((LANES, 1), 0) < shift, gate_ref[...], 0.0)
    o_ref[...] = jnp.where(i < n_main, main, jnp.where(i == n_main, gates, 0.0)).astype(o_ref.dtype)


def _tail_relayout(wt, first, shift, main, out_rows):
    n, k = wt.shape
    assert first % LANES == 0 and main % LANES == 0 and out_rows % LANES == 0 and shift % 8 == 0 and shift < LANES
    assert first + shift + main == n
    c0, n_main = first // LANES, main // LANES
    return pl.pallas_call(
        functools.partial(_tail_body, shift=shift, n_main=n_main),
        grid=(out_rows // LANES,),
        in_specs=[pl.BlockSpec((LANES, k), lambda i: (c0 + jnp.minimum(i, n_main - 1), 0)),
                  pl.BlockSpec((LANES, k), lambda i: (c0 + jnp.minimum(i, n_main - 1) + 1, 0)),
                  pl.BlockSpec((LANES, k), lambda i: (c0, 0))],
        out_specs=pl.BlockSpec((LANES, k), lambda i: (i, 0)),
        out_shape=jax.ShapeDtypeStruct((out_rows, k), BF16),
        compiler_params=_params(("parallel",)),
        name="tail_relayout",
    )(wt, wt, wt)


def _matmul_heads_body(a_ref, b_ref, o32_ref, o16_ref, *, transposed):
    b = b_ref[...].astype(BF16)
    acc = _dot_nt(a_ref[...], b) if transposed else _dot(a_ref[...], b)
    for h in range(MEM_HEADS):
        o32_ref[:, h, :] = acc[:, h * LANES:(h + 1) * LANES]
    o16_ref[...] = acc.astype(BF16)


def _matmul_heads(a, b, first=0, transposed=False, tm=1024):
    m, k = a.shape
    n = MEM_HEADS * LANES
    tm = min(tm, m)
    assert m % tm == 0 and first % n == 0, (a.shape, b.shape, first)
    j0 = first // n
    b_spec = pl.BlockSpec((n, k), lambda i: (j0, 0)) if transposed else pl.BlockSpec((k, n), lambda i: (0, j0))
    return pl.pallas_call(
        functools.partial(_matmul_heads_body, transposed=transposed),
        grid=(m // tm,),
        in_specs=[pl.BlockSpec((tm, k), lambda i: (i, 0)), b_spec],
        out_specs=[pl.BlockSpec((tm, MEM_HEADS, LANES), lambda i: (i, 0, 0)), pl.BlockSpec((tm, n), lambda i: (i, 0))],
        out_shape=[jax.ShapeDtypeStruct((m, MEM_HEADS, LANES), F32), jax.ShapeDtypeStruct((m, n), BF16)],
        compiler_params=_params(("parallel",)),
        name="matmul_heads",
    )(a, b)


def _outproj_body(*refs, widths):
    xs = refs[:len(widths)]
    w_ref, r_ref, o_ref = refs[len(widths):]
    acc = r_ref[...]
    off = 0
    for x_ref, w in zip(xs, widths):
        acc = acc + _dot(x_ref[...], w_ref[off:off + w, :])
        off += w
    o_ref[...] = acc


def _outproj(xs, w_bf16, resid, tm=1024, tn=512):
    m = resid.shape[0]
    n = w_bf16.shape[1]
    widths = tuple(x.shape[1] for x in xs)
    assert sum(widths) == w_bf16.shape[0]
    tm = min(tm, m)
    in_specs = [pl.BlockSpec((tm, w), lambda i, j: (i, 0)) for w in widths]
    in_specs += [pl.BlockSpec((w_bf16.shape[0], tn), lambda i, j: (0, j)), pl.BlockSpec((tm, tn), lambda i, j: (i, j))]
    return pl.pallas_call(
        functools.partial(_outproj_body, widths=widths),
        grid=(m // tm, n // tn),
        in_specs=in_specs,
        out_specs=pl.BlockSpec((tm, tn), lambda i, j: (i, j)),
        out_shape=jax.ShapeDtypeStruct((m, n), F32),
        compiler_params=_params(("parallel", "parallel")),
        name="outproj",
    )(*xs, w_bf16, resid)


def _hgrn_body(qa_ref, fa_ref, ia_ref, za_ref, lb_ref, gn_ref, s0_ref, o_ref, s_out, s_scr, *, L, valid):
    c = pl.program_id(2)

    @pl.when(c == 0)
    def _():
        s_scr[...] = s0_ref[...]

    lb = lb_ref[...]
    sig = jax.nn.sigmoid(fa_ref[...])
    logf = jnp.log(lb + (1.0 - lb) * sig)
    kk = (1.0 - lb) * (1.0 - sig)
    if valid < L:
        live = _iota2((L, 1), 0) < valid
        logf = jnp.where(live, logf, 0.0)
        kk = jnp.where(live, kk, 0.0)
    tri_b = (_iota2((L, L), 0) >= _iota2((L, L), 1)).astype(BF16)
    bc = _cumsum_rows(logf, tri_b)
    q = _silu(qa_ref[...])
    gate = _silu(za_ref[...])
    v = ia_ref[...]
    gn = gn_ref[...]
    nsub = L // HG_SUB
    rr = _iota2((L, nsub * L), 0)
    cc = _iota2((L, nsub * L), 1)
    keep = ((jnp.right_shift(cc, L.bit_length() - 1) == jnp.right_shift(rr, HG_SUB.bit_length() - 1))
            & (jnp.bitwise_and(cc, L - 1) <= rr))
    for j in range(HG_HB):
        sl = slice(j * HG_DK, (j + 1) * HG_DK)
        bj, qj, kj = bc[:, sl], q[:, sl], kk[:, sl]
        vb = v[:, sl].astype(BF16)
        s_prev = s_scr[j]
        inter = _dot((qj * jnp.exp(bj)).astype(BF16), s_prev.astype(BF16))
        mids = [bj[i * HG_SUB + HG_SUB // 2:i * HG_SUB + HG_SUB // 2 + 1, :] for i in range(nsub)]
        mid_rows = jnp.concatenate([jnp.broadcast_to(m, (HG_SUB, HG_DK)) for m in mids], axis=0)
        q_dec = qj * jnp.exp(jnp.minimum(bj - mid_rows, EXP_CLAMP))
        k_dec = jnp.concatenate([kj * jnp.exp(jnp.minimum(m - bj, EXP_CLAMP)) for m in mids], axis=0)
        att = jnp.where(keep, _dot_nt(q_dec.astype(BF16), k_dec.astype(BF16)), 0.0)
        o = inter + _dot(att.astype(BF16), jnp.concatenate([vb] * nsub, axis=0))
        o_n = o * lax.rsqrt(jnp.mean(o * o, axis=-1, keepdims=True) + EPS) * gn
        o_ref[:, sl] = (o_n * gate[:, sl]).astype(o_ref.dtype)
        bl = bj[L - 1:L, :]
        kd = kj * jnp.exp(bl - bj)
        s_scr[j] = _row_to_col(jnp.exp(bl), HG_DK) * s_prev + _dot_tn(kd.astype(BF16), vb)

    @pl.when(c == pl.num_programs(2) - 1)
    def _():
        s_out[...] = s_scr[...]


def _hgrn_call(y, s0, lb, gn, *, B, T, L, valid):
    nc = T // L
    w = HG_HB * HG_DK

    def col(name):
        blk = EVEN_A[name] // w
        return pl.BlockSpec((L, w), lambda b, hg, c: (b * nc + c, blk + hg))

    state_spec = pl.BlockSpec((None, HG_HB, HG_DK, HG_DV), lambda b, hg, c: (b, hg, 0, 0))
    return pl.pallas_call(
        functools.partial(_hgrn_body, L=L, valid=valid),
        grid=(B, HG_HEADS // HG_HB, nc),
        in_specs=[col("qa"), col("fa"), col("ia"), col("za"),
                  pl.BlockSpec((1, w), lambda b, hg, c: (0, hg)),
                  pl.BlockSpec((1, HG_DV), lambda b, hg, c: (0, 0)),
                  state_spec],
        out_specs=[pl.BlockSpec((L, w), lambda b, hg, c: (b * nc + c, hg)), state_spec],
        out_shape=[jax.ShapeDtypeStruct((B * T, HG_W), BF16),
                   jax.ShapeDtypeStruct((B, HG_HEADS, HG_DK, HG_DV), F32)],
        scratch_shapes=[pltpu.VMEM((HG_HB, HG_DK, HG_DV), F32)],
        compiler_params=_params(("arbitrary", "arbitrary", "arbitrary")),
        name="hgrn2",
    )(y, y, y, y, lb.reshape(1, HG_W), gn.reshape(1, HG_DV), s0)


def _mlstm_body(q_ref, k_ref, v_ref, og_ref, z_ref, g_ref, bif_ref, gn_ref, c0_ref, n0_ref, m0_ref,
                h_ref, c_out, n_out, m_out, c_scr, n_scr, m_scr, *, L, valid):
    c = pl.program_id(2)

    @pl.when(c == 0)
    def _():
        c_scr[...] = c0_ref[...]
        n_scr[...] = n0_ref[...]
        m_scr[...] = m0_ref[...]

    gates = g_ref[...] + bif_ref[...]
    log_i = gates
    log_f = jnp.minimum(gates, 0.0) - jnp.log(1.0 + jnp.exp(-jnp.abs(gates)))
    if valid < L:
        live = _iota2((L, 1), 0) < valid
        log_i = jnp.where(live, log_i, -1e30)
        log_f = jnp.where(live, log_f, 0.0)
    tri = _iota2((L, L), 0) >= _iota2((L, L), 1)
    bcs = _cumsum_rows(log_f, tri.astype(BF16))
    for j in range(ML_HB):
        head = pl.program_id(1) * ML_HB + j
        b_col = _lane_col(bcs, ML_HEADS + head)
        i_col = _lane_col(log_i, head)
        b_row = _col_to_row(b_col, L)
        i_row = _col_to_row(i_col, L)
        m_prev = m_scr[:, j:j + 1]
        dmat = jnp.where(tri, b_col - b_row + i_row, NEG_INF)
        inter = b_col + m_prev
        mt = jnp.maximum(inter, jnp.max(dmat, axis=1, keepdims=True))
        w_in = jnp.exp(dmat - mt)
        w_x = jnp.exp(inter - mt)
        qj = q_ref[:, j * ML_DK:(j + 1) * ML_DK]
        kj = k_ref[:, j * ML_DK:(j + 1) * ML_DK] * (ML_DK ** -0.5)
        vj = v_ref[:, j * ML_DV:(j + 1) * ML_DV]
        qb, kb = qj.astype(BF16), kj.astype(BF16)
        sw = _dot_nt(qb, kb) * w_in
        c_prev = c_scr[j]
        n_prev = n_scr[:, j * ML_DK:(j + 1) * ML_DK]
        num = w_x * _dot_nt(qb, c_prev.astype(BF16)) + _dot(sw.astype(BF16), vj.astype(BF16))
        den = w_x * jnp.sum(qj * n_prev, axis=1, keepdims=True) + jnp.sum(sw, axis=1, keepdims=True)
        h = num / jnp.maximum(jnp.abs(den), jnp.exp(-mt))
        m_last = mt[L - 1:L, :]
        b_last = b_col[L - 1:L, :]
        w_end = jnp.exp(b_last - b_col + i_col - m_last)
        d_c = jnp.exp(b_last + m_prev - m_last)
        c_scr[j] = d_c * c_prev + _dot_tn((w_end * vj).astype(BF16), kb)
        n_scr[:, j * ML_DK:(j + 1) * ML_DK] = d_c * n_prev + jnp.sum(w_end * kj, axis=0, keepdims=True)
        m_scr[:, j:j + 1] = m_last
        sv = slice(j * ML_DV, (j + 1) * ML_DV)
        h_n = h * lax.rsqrt(jnp.mean(h * h, axis=-1, keepdims=True) + EPS) * gn_ref[:, sv]
        h_ref[:, sv] = (h_n * jax.nn.sigmoid(og_ref[:, sv]) * _silu(z_ref[:, sv])).astype(h_ref.dtype)

    @pl.when(c == pl.num_programs(2) - 1)
    def _():
        c_out[...] = c_scr[...]
        n_out[...] = n_scr[...]
        m_out[...] = m_scr[...]


def _mlstm_call(ya, yb, c0, n0, m0, bif_r, gn, *, B, T, L, valid):
    nc = T // L
    ng = ML_HEADS // ML_HB
    wk, wv = ML_HB * ML_DK, ML_HB * ML_DV

    def col(name, w):
        blk = (ODD_A[name] if name in ODD_A else ODD_B[name]) // w
        return pl.BlockSpec((L, w), lambda b, hg, c: (b * nc + c, blk + hg))

    c_spec = pl.BlockSpec((None, ML_HB, ML_DV, ML_DK), lambda b, hg, c: (b, hg, 0, 0))
    n_spec = pl.BlockSpec((None, 1, wk), lambda b, hg, c: (b, 0, hg))
    m_spec = pl.BlockSpec((None, None, 1, LANES), lambda b, hg, c: (b, hg, 0, 0))
    m0_r = jnp.pad(m0.reshape(B, ng, 1, ML_HB), ((0, 0), (0, 0), (0, 0), (0, LANES - ML_HB)))
    h, c_new, n_new, m_new = pl.pallas_call(
        functools.partial(_mlstm_body, L=L, valid=valid),
        grid=(B, ng, nc),
        in_specs=[col("q", wk), col("k", wk), col("v", wv), col("og", wv), col("z", wv),
                  pl.BlockSpec((L, LANES), lambda b, hg, c: (b * nc + c, ODD_B["gates"] // LANES)),
                  pl.BlockSpec((1, LANES), lambda b, hg, c: (0, 0)),
                  pl.BlockSpec((1, wv), lambda b, hg, c: (0, hg)),
                  c_spec, n_spec, m_spec],
        out_specs=[pl.BlockSpec((L, wv), lambda b, hg, c: (b * nc + c, hg)), c_spec, n_spec, m_spec],
        out_shape=[jax.ShapeDtypeStruct((B * T, ML_V_W), BF16),
                   jax.ShapeDtypeStruct((B, ML_HEADS, ML_DV, ML_DK), F32),
                   jax.ShapeDtypeStruct((B, 1, ML_QK_W), F32),
                   jax.ShapeDtypeStruct((B, ng, 1, LANES), F32)],
        scratch_shapes=[pltpu.VMEM((ML_HB, ML_DV, ML_DK), F32), pltpu.VMEM((1, wk), F32), pltpu.VMEM((1, LANES), F32)],
        compiler_params=_params(("arbitrary", "arbitrary", "arbitrary")),
        name="mlstm",
    )(ya, ya, ya, ya, yb, yb, bif_r, gn.reshape(1, ML_V_W), c0, n0.reshape(B, 1, ML_QK_W), m0_r)
    return h, c_new, n_new.reshape(B, ML_HEADS, ML_DK), m_new[:, :, 0, :ML_HB].reshape(B, ML_HEADS)


def _mem_body(q_ref, k_ref, v_ref, o_ref):
    q = q_ref[...] * (MEM_HD ** -0.5)
    for h in range(MEM_HEADS):
        sl = slice(h * MEM_HD, (h + 1) * MEM_HD)
        s = _dot_nt(q[:, sl].astype(BF16), k_ref[:, sl].astype(BF16))
        p = jnp.exp(s - jnp.max(s, axis=-1, keepdims=True))
        o = _dot(p.astype(BF16), v_ref[:, sl].astype(BF16)) / jnp.sum(p, axis=-1, keepdims=True)
        o_ref[:, sl] = o.astype(o_ref.dtype)


def _mem_call(y, q_off, k2d, v2d, *, B, T, tq=256):
    tq = min(tq, T)
    nq = T // tq
    qb = q_off // MEM_W
    return pl.pallas_call(
        _mem_body,
        grid=(B, nq),
        in_specs=[pl.BlockSpec((tq, MEM_W), lambda b, i: (b * nq + i, qb)),
                  pl.BlockSpec((N_MEM, MEM_W), lambda b, i: (b, 0)),
                  pl.BlockSpec((N_MEM, MEM_W), lambda b, i: (b, 0))],
        out_specs=pl.BlockSpec((tq, MEM_W), lambda b, i: (b * nq + i, 0)),
        out_shape=jax.ShapeDtypeStruct((B * T, MEM_W), BF16),
        compiler_params=_params(("parallel", "parallel")),
        name="mem_attn",
    )(y, k2d, v2d)


def _gelu_tanh(x):
    return 0.5 * x * (1.0 + jnp.tanh(math.sqrt(2.0 / math.pi) * (x + 0.044715 * (x * x * x))))


def _compress_body(x_ref, w1_ref, b1_ref, w2_ref, pe_ref, o_ref, x32, *, nch):
    x32[...] = x_ref[...].astype(F32)
    a = jnp.zeros((nch, NSA_HD), F32)
    b = jnp.zeros((nch, NSA_HD), F32)
    for s in range(CMP_STRIDE):
        r = x32[pl.ds(s, nch, stride=CMP_STRIDE), :]
        a = a + _dot((r + pe_ref[s:s + 1, :]).astype(BF16), w1_ref[s])
        b = b + _dot((r + pe_ref[CMP_STRIDE + s:CMP_STRIDE + s + 1, :]).astype(BF16), w1_ref[CMP_STRIDE + s])
    h = a + pltpu.roll(b, nch - 1, 0) + b1_ref[...]
    o_ref[...] = _dot(_gelu_tanh(h).astype(BF16), w2_ref[...])


def _compress_call(x16, w1, b1, w2, pe, *, B, T):
    nch = T // CMP_STRIDE
    return pl.pallas_call(
        functools.partial(_compress_body, nch=nch),
        grid=(B, NSA_KVH),
        in_specs=[pl.BlockSpec((T, NSA_HD), lambda b, h: (b, h)),
                  pl.BlockSpec((CMP_BLOCK, NSA_HD, NSA_HD), lambda b, h: (0, 0, 0)),
                  pl.BlockSpec((1, NSA_HD), lambda b, h: (0, 0)),
                  pl.BlockSpec((NSA_HD, NSA_HD), lambda b, h: (0, 0)),
                  pl.BlockSpec((CMP_BLOCK, NSA_HD), lambda b, h: (0, 0))],
        out_specs=pl.BlockSpec((None, None, nch, NSA_HD), lambda b, h: (b, h, 0, 0)),
        out_shape=jax.ShapeDtypeStruct((B, NSA_KVH, nch, NSA_HD), F32),
        scratch_shapes=[pltpu.VMEM((T, NSA_HD), F32)],
        compiler_params=_params(("parallel", "parallel")),
        name="nsa_compress",
    )(x16, w1.astype(BF16), b1.reshape(1, NSA_HD), w2.astype(BF16), pe)


def _softmax_rows(s):
    m = jnp.max(s, axis=-1, keepdims=True)
    m = jnp.where(m == NEG_INF, 0.0, m)
    p = jnp.exp(s - m)
    return p, jnp.sum(p, axis=-1, keepdims=True)


def _slc_scores(psum, width, n_slc):
    ncmp = psum.shape[1]
    d = _iota2((ncmp, width), 0) - (SEL_BLOCK // CMP_STRIDE) * _iota2((ncmp, width), 1)
    wgt = jnp.where((d == -1) | (d == 3), 1.0, jnp.where((d >= 0) & (d <= 2), 2.0, 0.0))
    wgt = jnp.where(_iota2((ncmp, width), 1) < n_slc, wgt, 0.0).astype(BF16)
    p_hi = psum.astype(BF16)
    p_lo = (psum - p_hi.astype(F32)).astype(BF16)
    return _dot(p_hi, wgt) + _dot(p_lo, wgt)


def _top_blocks(slc, cur, n_pick):
    rows, width = slc.shape
    blk = _iota2((rows, width), 1)
    forced = (blk == 0) | (blk == cur) | (blk == cur - 1)
    score = jnp.where(forced, jnp.inf, slc)
    score = jnp.where(blk > cur, NEG_INF, score)
    blk_f = blk.astype(F32)
    lane = _iota2((rows, LANES), 1)
    sel = jnp.zeros((rows, width), F32)
    picks = jnp.zeros((rows, LANES), F32)
    for i in range(n_pick):
        mx = jnp.max(score, axis=-1, keepdims=True)
        first = jnp.min(jnp.where(score == mx, blk_f, float(width)), axis=-1, keepdims=True)
        pick = blk_f == first
        sel = jnp.where(pick, 1.0, sel)
        picks = jnp.where(lane == i, first, picks)
        score = jnp.where(pick, NEG_INF, score)
    return sel, picks


def _member_by_rank(psum, tpos_row, n_slc, n_pick):
    nq, ncmp = psum.shape
    nb = -(-n_slc // 8) * 8
    d = _iota2((nb, ncmp), 1) - (SEL_BLOCK // CMP_STRIDE) * _iota2((nb, ncmp), 0)
    wgt = jnp.where((d == -1) | (d == 3), 1.0, jnp.where((d >= 0) & (d <= 2), 2.0, 0.0))
    wgt = jnp.where(_iota2((nb, ncmp), 0) < n_slc, wgt, 0.0).astype(BF16)
    p_hi = psum.astype(BF16)
    p_lo = (psum - p_hi.astype(F32)).astype(BF16)
    slc = _dot_nt(wgt, p_hi) + _dot_nt(wgt, p_lo)
    blk = _iota2((nb, nq), 0)
    cur = jnp.right_shift(tpos_row, SEL_SHIFT)
    forced = (blk == 0) | (blk == cur) | (blk == cur - 1)
    score = jnp.where(forced, jnp.inf, slc)
    score = jnp.where(blk > cur, NEG_INF, score)
    ahead = jnp.zeros((nb, nq), F32)
    for i in range(n_slc):
        s_i = score[i:i + 1, :]
        ahead = ahead + jnp.where((s_i > score) | ((s_i == score) & (blk > i)), 1.0, 0.0)
    return jnp.where((ahead < n_pick) & (blk <= cur), 1.0, 0.0)


NEAR_COLS = 2 * Q_BLOCK


def _banded_attention(q, k_ref, v_ref, start, width, mask, near_bias):
    far = width - NEAR_COLS
    s_far = _dot_nt(q, k_ref[pl.ds(start, far), :]) + mask[:, :far]
    s_near = _dot_nt(q, k_ref[pl.ds(start + far, NEAR_COLS), :]) + near_bias + mask[:, far:]
    m = jnp.maximum(jnp.max(s_far, axis=-1, keepdims=True), jnp.max(s_near, axis=-1, keepdims=True))
    m = jnp.where(m == NEG_INF, 0.0, m)
    p_far, p_near = jnp.exp(s_far - m), jnp.exp(s_near - m)
    l = jnp.sum(p_far, axis=-1, keepdims=True) + jnp.sum(p_near, axis=-1, keepdims=True)
    o = (_dot(p_far.astype(BF16), v_ref[pl.ds(start, far), :])
         + _dot(p_near.astype(BF16), v_ref[pl.ds(start + far, NEAR_COLS), :]))
    return o / jnp.maximum(l, TINY)


def _nsa_prompt_body(q_ref, zb_ref, gb_ref, bg_ref, ks_ref, vs_ref, kw_ref, vw_ref, kc_ref, vc_ref,
                     bc_ref, bn_ref, o_ref, ksp, vsp, kwp, vwp, osel, *, T):
    qi = pl.program_id(2)
    tq = Q_BLOCK
    front = T - tq
    wlen = WINDOW + tq
    n_slc = T // SEL_BLOCK

    @pl.when(qi == 0)
    def _():
        ksp[0:front, :] = jnp.zeros((front, NSA_HD), BF16)
        vsp[0:front, :] = jnp.zeros((front, NSA_HD), BF16)
        ksp[front:front + T, :] = ks_ref[...].astype(BF16)
        vsp[front:front + T, :] = vs_ref[...].astype(BF16)
        kwp[0:WINDOW, :] = jnp.zeros((WINDOW, NSA_HD), BF16)
        vwp[0:WINDOW, :] = jnp.zeros((WINDOW, NSA_HD), BF16)
        kwp[WINDOW:WINDOW + T, :] = kw_ref[...].astype(BF16)
        vwp[WINDOW:WINDOW + T, :] = vw_ref[...].astype(BF16)

    t0 = pl.multiple_of(qi * tq, tq)
    tpos = _iota2((tq, 1), 0) + t0
    q_all = q_ref[...] * (NSA_HD ** -0.5)
    q = jnp.concatenate([q_all[:, g * NSA_HD:(g + 1) * NSA_HD] for g in range(NSA_G)], axis=0).astype(BF16)
    bias_near = bn_ref[...].reshape(NSA_G * tq, NEAR_COLS)

    def per_head(a):
        return jnp.concatenate([a] * NSA_G, axis=0)

    ncmp = T // CMP_STRIDE
    vis = tpos >= _iota2((1, ncmp), 1) * CMP_STRIDE + (CMP_BLOCK - 1)
    s = _dot_nt(q, kc_ref[...].astype(BF16)) + bc_ref[...].reshape(NSA_G * tq, ncmp)
    p, l = _softmax_rows(s + per_head(jnp.where(vis, 0.0, NEG_INF)))
    p = p / jnp.maximum(l, TINY)
    o_cmp = _dot(p.astype(BF16), vc_ref[...].astype(BF16))
    psum = p[0:tq]
    for g in range(1, NSA_G):
        psum = psum + p[g * tq:(g + 1) * tq]

    member_t = _member_by_rank(psum, _iota2((1, tq), 1) + t0, n_slc, min(N_SEL, n_slc)).astype(BF16)

    nb = member_t.shape[0]
    n_win = SEL_WINDOWS if T % (SEL_WINDOWS * tq) == 0 else 1
    for i in range(n_win):
        w_prev, w = T * i // n_win, T * (i + 1) // n_win

        @pl.when((qi >= w_prev // tq) & (qi < w // tq))
        def _(w=w):
            off = T - w
            col_blk = (jnp.right_shift(_iota2((nb, w), 1) + off, SEL_SHIFT)
                       + (qi * (tq // SEL_BLOCK) + (tq - T) // SEL_BLOCK))
            expand = (col_blk == _iota2((nb, w), 0)).astype(BF16)
            kpos = _iota2((1, w), 1) + (t0 + tq - w)
            allowed = (_dot_tn(member_t, expand) > 0.5) & (kpos <= tpos)
            mask_s = per_head(jnp.where(allowed, 0.0, NEG_INF))
            osel[...] = _banded_attention(q, ksp, vsp, t0 + off, w, mask_s, bias_near)

    dist = WINDOW + _iota2((tq, wlen), 0) - _iota2((tq, wlen), 1)
    in_win = (dist >= 0) & (dist < WINDOW) & (_iota2((1, wlen), 1) + (t0 - WINDOW) >= 0)
    o_win = _banded_attention(q, kwp, vwp, t0, wlen, per_head(jnp.where(in_win, 0.0, NEG_INF)), bias_near)
    gate = jax.nn.sigmoid(gb_ref[...] + bg_ref[...])
    zb = _silu(zb_ref[...])
    for g in range(NSA_G):
        head = pl.program_id(1) * NSA_G + g
        r = slice(g * tq, (g + 1) * tq)
        mix = (_lane_col(gate, head) * o_cmp[r] + _lane_col(gate, NSA_HEADS + head) * osel[r, :]
               + _lane_col(gate, 2 * NSA_HEADS + head) * o_win[r])
        sl = slice(g * NSA_HD, (g + 1) * NSA_HD)
        o_ref[:, sl] = (mix * zb[:, sl]).astype(o_ref.dtype)


def _nsa_prompt_call(ya, yb, kv16, kcmp, vcmp, bg_r, bias_c, bias_near, *, B, T):
    nq = T // Q_BLOCK
    gw = NSA_G * NSA_HD
    kv_spec = pl.BlockSpec((T, NSA_HD), lambda b, h, i: (b, h))
    cmp_spec = pl.BlockSpec((None, None, T // CMP_STRIDE, NSA_HD), lambda b, h, i: (b, h, 0, 0))
    return pl.pallas_call(
        functools.partial(_nsa_prompt_body, T=T),
        grid=(B, NSA_KVH, nq),
        in_specs=[pl.BlockSpec((Q_BLOCK, gw), lambda b, h, i: (b * nq + i, EVEN_A["qb"] // gw + h)),
                  pl.BlockSpec((Q_BLOCK, gw), lambda b, h, i: (b * nq + i, EVEN_B["zb"] // gw + h)),
                  pl.BlockSpec((Q_BLOCK, LANES), lambda b, h, i: (b * nq + i, EVEN_B["gb"] // LANES)),
                  pl.BlockSpec((1, LANES), lambda b, h, i: (0, 0)),
                  kv_spec, kv_spec, kv_spec, kv_spec, cmp_spec, cmp_spec,
                  pl.BlockSpec((None, NSA_G, Q_BLOCK, T // CMP_STRIDE), lambda b, h, i: (h, 0, i, 0)),
                  pl.BlockSpec((None, NSA_G, Q_BLOCK, NEAR_COLS), lambda b, h, i: (h, 0, 0, 0))],
        out_specs=pl.BlockSpec((Q_BLOCK, gw), lambda b, h, i: (b * nq + i, h)),
        out_shape=jax.ShapeDtypeStruct((B * T, NSA_W), BF16),
        scratch_shapes=[pltpu.VMEM((2 * T - Q_BLOCK, NSA_HD), BF16), pltpu.VMEM((2 * T - Q_BLOCK, NSA_HD), BF16),
                        pltpu.VMEM((WINDOW + T, NSA_HD), BF16), pltpu.VMEM((WINDOW + T, NSA_HD), BF16),
                        pltpu.VMEM((NSA_G * Q_BLOCK, NSA_HD), F32)],
        compiler_params=_params(("arbitrary", "arbitrary", "arbitrary")),
        name="nsa_prompt",
    )(ya, yb, yb, bg_r, *kv16, kcmp, vcmp, bias_c, bias_near)


CMP_PAGES = 16
CHUNKS_PER_PAGE = PAGE_SIZE // CMP_STRIDE
PAGE_ROWS = PAGE_SIZE * NSA_KVH


def _pool_rows(pool):
    return pool.reshape(pool.shape[0] * PAGE_ROWS, NSA_HD)


def _cmp_pages_body(pt_ref, *refs):
    del pt_ref
    pages = refs[:CMP_PAGES]
    w_ref, pe_ref, o_ref = refs[CMP_PAGES:]
    rows = CMP_PAGES * CHUNKS_PER_PAGE
    per_head = [jnp.concatenate(
        [jnp.concatenate([pg[pl.ds(NSA_KVH * s + h, CHUNKS_PER_PAGE, stride=CMP_STRIDE * NSA_KVH), :]
                          for s in range(CMP_STRIDE)], axis=1) for pg in pages], axis=0) for h in range(NSA_KVH)]
    w = w_ref[...]
    r = _dot(jnp.concatenate(per_head, axis=0).astype(BF16), w)
    pc = _dot(pe_ref[...], w)
    r = r + jnp.concatenate([pc[0:1, :NSA_HD], pc[1:2, NSA_HD:]], axis=1)
    for h in range(NSA_KVH):
        o_ref[h] = r[h * rows:(h + 1) * rows]


def _cmp_pages_call(pool, page_table, w1, pe, *, B):
    n_pages = page_table.shape[1]
    rows = CMP_PAGES * CHUNKS_PER_PAGE
    view = _pool_rows(pool)
    w = w1.reshape(2, CMP_STRIDE, NSA_HD, NSA_HD).transpose(1, 2, 0, 3).reshape(CMP_STRIDE * NSA_HD, 2 * NSA_HD)
    pe_rows = jnp.pad(pe.reshape(2, CMP_STRIDE * NSA_HD), ((0, 6), (0, 0))).astype(BF16)

    def page_spec(i):
        return pl.BlockSpec((PAGE_ROWS, NSA_HD), lambda b, s, pt: (pt[b * n_pages + s * CMP_PAGES + i], 0))

    grid_spec = pltpu.PrefetchScalarGridSpec(
        num_scalar_prefetch=1,
        grid=(B, n_pages // CMP_PAGES),
        in_specs=[page_spec(i) for i in range(CMP_PAGES)]
        + [pl.BlockSpec((CMP_STRIDE * NSA_HD, 2 * NSA_HD), lambda b, s, pt: (0, 0)),
           pl.BlockSpec((8, CMP_STRIDE * NSA_HD), lambda b, s, pt: (0, 0))],
        out_specs=pl.BlockSpec((None, NSA_KVH, rows, 2 * NSA_HD), lambda b, s, pt: (b, 0, s, 0)),
    )
    return pl.pallas_call(
        _cmp_pages_body,
        grid_spec=grid_spec,
        out_shape=jax.ShapeDtypeStruct((B, NSA_KVH, n_pages * CHUNKS_PER_PAGE, 2 * NSA_HD), F32),
        compiler_params=_params(("arbitrary", "arbitrary")),
        name="nsa_cmp_pages",
    )(page_table.reshape(-1), *([view] * CMP_PAGES), w.astype(BF16), pe_rows)


SEL_WINDOWS = 4
SLC_LANES = 384


def _sample_q_rows(q_ref):
    q = q_ref[...] * (NSA_HD ** -0.5)
    return jnp.concatenate([q[:, g * NSA_HD:(g + 1) * NSA_HD] for g in range(NSA_G)], axis=0).astype(BF16)


def _nsa_sample_main_body(abk_ref, abv_ref, b1_ref, w2_ref, q_ref, wk_ref, wv_ref, kn_ref, vn_ref, bc_ref, bw_ref,
                          ocmp_ref, owin_ref, idx_ref, *, T, n_slc):
    tp = SAMPLE_PAD_T
    rows = NSA_G * tp
    ncmp = abk_ref.shape[0]

    def compressed(ab_ref, t):
        ab = ab_ref[...]
        h = ab[:, :NSA_HD] + pltpu.roll(ab[:, NSA_HD:], ncmp - 1, 0) + b1_ref[t]
        return _dot(_gelu_tanh(h).astype(BF16), w2_ref[t]).astype(BF16)

    kc, vc = compressed(abk_ref, 0), compressed(abv_ref, 1)
    q = _sample_q_rows(q_ref)
    step = jnp.bitwise_and(_iota2((rows, 1), 0), tp - 1)
    tpos = PAST_LEN + step
    vis = tpos >= _iota2((1, ncmp), 1) * CMP_STRIDE + (CMP_BLOCK - 1)
    p, l = _softmax_rows(jnp.where(vis, _dot_nt(q, kc) + bc_ref[...], NEG_INF))
    p = p / jnp.maximum(l, TINY)
    ocmp_ref[...] = _dot(p.astype(BF16), vc)
    psum = p[0:tp]
    for g in range(1, NSA_G):
        psum = psum + p[g * tp:(g + 1) * tp]
    cur = jnp.right_shift(PAST_LEN + _iota2((tp, 1), 0), SEL_SHIFT)
    _, picks = _top_blocks(_slc_scores(psum, SLC_LANES, n_slc), cur, N_SEL)
    idx_ref[...] = picks.astype(jnp.int32)

    wb = wk_ref.shape[0] // NSA_KVH
    wlen = bw_ref.shape[1]
    fill = jnp.zeros((wlen - wb - tp, NSA_HD), BF16)
    head = pl.program_id(1)
    k_all = jnp.concatenate([wk_ref[pl.ds(head, wb, stride=NSA_KVH), :].astype(BF16), kn_ref[...], fill], axis=0)
    v_all = jnp.concatenate([wv_ref[pl.ds(head, wb, stride=NSA_KVH), :].astype(BF16), vn_ref[...], fill], axis=0)
    col = _iota2((1, wlen), 1)
    dist = tpos - (PAST_LEN - wb + col)
    in_win = (dist >= 0) & (dist < WINDOW) & (col < wb + T)
    pw, lw = _softmax_rows(jnp.where(in_win, _dot_nt(q, k_all) + bw_ref[...], NEG_INF))
    owin_ref[...] = _dot(pw.astype(BF16), v_all) / jnp.maximum(lw, TINY)


def _nsa_sample_main_call(ya, kw16, vw16, abk, abv, b1, w2, wk, wv, bias_c, bias_w, *, B, T):
    tp = SAMPLE_PAD_T
    rows = NSA_G * tp
    gw = NSA_G * NSA_HD
    ncmp = abk.shape[2]
    wb = wk.shape[1]
    wlen = bias_w.shape[-1]
    n_slc = -(-(PAST_LEN + T) // SEL_BLOCK)
    assert n_slc <= SLC_LANES and T <= tp
    ab_spec = pl.BlockSpec((None, None, ncmp, 2 * NSA_HD), lambda b, h: (b, h, 0, 0))
    win_spec = pl.BlockSpec((wb * NSA_KVH, NSA_HD), lambda b, h: (b, 0))
    o_spec = pl.BlockSpec((None, None, rows, NSA_HD), lambda b, h: (b, h, 0, 0))
    return pl.pallas_call(
        functools.partial(_nsa_sample_main_body, T=T, n_slc=n_slc),
        grid=(B, NSA_KVH),
        in_specs=[ab_spec, ab_spec,
                  pl.BlockSpec((2, 1, NSA_HD), lambda b, h: (0, 0, 0)),
                  pl.BlockSpec((2, NSA_HD, NSA_HD), lambda b, h: (0, 0, 0)),
                  pl.BlockSpec((tp, gw), lambda b, h: (b, EVEN_A["qb"] // gw + h)),
                  win_spec, win_spec,
                  pl.BlockSpec((tp, NSA_HD), lambda b, h: (b, h)),
                  pl.BlockSpec((tp, NSA_HD), lambda b, h: (b, h)),
                  pl.BlockSpec((None, rows, ncmp), lambda b, h: (h, 0, 0)),
                  pl.BlockSpec((None, rows, wlen), lambda b, h: (h, 0, 0))],
        out_specs=[o_spec, o_spec, pl.BlockSpec((None, None, tp, LANES), lambda b, h: (b, h, 0, 0))],
        out_shape=[jax.ShapeDtypeStruct((B, NSA_KVH, rows, NSA_HD), F32),
                   jax.ShapeDtypeStruct((B, NSA_KVH, rows, NSA_HD), F32),
                   jax.ShapeDtypeStruct((B, NSA_KVH, tp, LANES), jnp.int32)],
        compiler_params=_params(("parallel", "parallel")),
        name="nsa_sample_main",
    )(abk, abv, b1.reshape(2, 1, NSA_HD), w2.astype(BF16), ya,
      wk.reshape(B * wb * NSA_KVH, NSA_HD), wv.reshape(B * wb * NSA_KVH, NSA_HD), kw16, vw16, bias_c, bias_w)


NEAR_BLOCKS = 3


def _nsa_sample_sel_body(idx_ref, pt_ref, q_ref, kn_ref, vn_ref, tbl_ref, ocmp_ref, owin_ref, gb_ref, bg_ref, zb_ref,
                         *refs, T):
    del pt_ref
    k_blocks = refs[:N_SEL]
    v_blocks = refs[N_SEL:2 * N_SEL]
    o_ref, osel = refs[2 * N_SEL:]
    tp = SAMPLE_PAD_T
    rows = NSA_G * tp
    b, h, t = pl.program_id(0), pl.program_id(1), pl.program_id(2)
    base = ((b * NSA_KVH + h) * T + t) * N_SEL
    first_new = PAST_LEN // SEL_BLOCK
    cur = jnp.right_shift(PAST_LEN + t, SEL_SHIFT)
    q = _sample_q_rows(q_ref)
    pad = jnp.zeros((SEL_BLOCK - tp, NSA_HD), BF16)
    k_new = jnp.concatenate([kn_ref[...], pad], axis=0)
    v_new = jnp.concatenate([vn_ref[...], pad], axis=0)
    lane = _iota2((1, LANES), 1)
    low = lane < SEL_BLOCK
    within = jnp.bitwise_and(lane, SEL_BLOCK - 1)
    ks, vs, bias, kpos = [], [], [], []
    for i in range(0, N_SEL, 2):
        pair_bias, pair_pos = [], []
        for j in (i, i + 1):
            blk = idx_ref[base + j]
            is_new = blk >= first_new
            ks.append(jnp.where(is_new, k_new, k_blocks[j][pl.ds(h, SEL_BLOCK, stride=NSA_KVH), :].astype(BF16)))
            vs.append(jnp.where(is_new, v_new, v_blocks[j][pl.ds(h, SEL_BLOCK, stride=NSA_KVH), :].astype(BF16)))
            pair_bias.append(tbl_ref[jnp.clip(blk - (first_new - NEAR_BLOCKS), 0, NEAR_BLOCKS)])
            pair_pos.append(jnp.where(blk <= cur, blk * SEL_BLOCK, PAST_LEN + SEL_BLOCK * LANES) + within)
        bias.append(jnp.where(low, pair_bias[0], pair_bias[1]))
        kpos.append(jnp.where(low, pair_pos[0], pair_pos[1]))
    k_all = jnp.concatenate(ks, axis=0)
    v_all = jnp.concatenate(vs, axis=0)
    step = jnp.bitwise_and(_iota2((rows, 1), 0), tp - 1)
    ok = jnp.concatenate(kpos, axis=1) <= PAST_LEN + step
    p, l = _softmax_rows(jnp.where(ok, _dot_nt(q, k_all) + jnp.concatenate(bias, axis=1), NEG_INF))
    o = _dot(p.astype(BF16), v_all) / jnp.maximum(l, TINY)

    @pl.when(t == 0)
    def _():
        osel[...] = jnp.zeros_like(osel)

    osel[...] = jnp.where(step == t, o, osel[...])

    @pl.when(t == T - 1)
    def _():
        gate = jax.nn.sigmoid(gb_ref[...] + bg_ref[...])
        zb = _silu(zb_ref[...])
        for g in range(NSA_G):
            r = slice(g * tp, (g + 1) * tp)
            head = h * NSA_G + g
            mix = (_lane_col(gate, head) * ocmp_ref[r, :] + _lane_col(gate, NSA_HEADS + head) * osel[r, :]
                   + _lane_col(gate, 2 * NSA_HEADS + head) * owin_ref[r, :])
            sl = slice(g * NSA_HD, (g + 1) * NSA_HD)
            o_ref[:, sl] = (mix * zb[:, sl]).astype(o_ref.dtype)


def _nsa_sample_sel_call(ya, yb, ks16, vs16, idx, page_table, pool_k, pool_v, tbl, o_cmp, o_win, bg_r, *, B, T):
    tp = SAMPLE_PAD_T
    rows = NSA_G * tp
    gw = NSA_G * NSA_HD
    n_pages = page_table.shape[1]
    halves = PAGE_SIZE // SEL_BLOCK
    idx_flat = idx[:, :, :T, :N_SEL].reshape(-1)
    view_k, view_v = _pool_rows(pool_k), _pool_rows(pool_v)

    def blk_spec(j):
        def index(b, h, t, idx_s, pt_s):
            blk = idx_s[((b * NSA_KVH + h) * T + t) * N_SEL + j]
            page = pt_s[b * n_pages + jnp.minimum(blk // halves, n_pages - 1)]
            return (page * halves + blk % halves, 0)
        return pl.BlockSpec((SEL_BLOCK * NSA_KVH, NSA_HD), index)

    o_spec = pl.BlockSpec((None, None, rows, NSA_HD), lambda b, h, t, *_: (b, h, 0, 0))
    grid_spec = pltpu.PrefetchScalarGridSpec(
        num_scalar_prefetch=2,
        grid=(B, NSA_KVH, T),
        in_specs=[pl.BlockSpec((tp, gw), lambda b, h, t, *_: (b, EVEN_A["qb"] // gw + h)),
                  pl.BlockSpec((tp, NSA_HD), lambda b, h, t, *_: (b, h)),
                  pl.BlockSpec((tp, NSA_HD), lambda b, h, t, *_: (b, h)),
                  pl.BlockSpec((None, NEAR_BLOCKS + 1, rows, LANES), lambda b, h, t, *_: (h, 0, 0, 0)),
                  o_spec, o_spec,
                  pl.BlockSpec((tp, LANES), lambda b, h, t, *_: (b, EVEN_B["gb"] // LANES)),
                  pl.BlockSpec((1, LANES), lambda b, h, t, *_: (0, 0)),
                  pl.BlockSpec((tp, gw), lambda b, h, t, *_: (b, EVEN_B["zb"] // gw + h))]
        + [blk_spec(j) for j in range(N_SEL)] * 2,
        out_specs=pl.BlockSpec((tp, gw), lambda b, h, t, *_: (b, h)),
        scratch_shapes=[pltpu.VMEM((rows, NSA_HD), F32)],
    )
    return pl.pallas_call(
        functools.partial(_nsa_sample_sel_body, T=T),
        grid_spec=grid_spec,
        out_shape=jax.ShapeDtypeStruct((B * tp, NSA_W), BF16),
        compiler_params=_params(("arbitrary", "arbitrary", "arbitrary")),
        name="nsa_sample_sel",
    )(idx_flat, page_table.reshape(-1), ya, ks16, vs16, tbl, o_cmp, o_win, yb, bg_r, yb,
      *([view_k] * N_SEL), *([view_v] * N_SEL))


def _sample_bias_tables(rel_bias, T, wb):
    tp = SAMPLE_PAD_T
    ncmp = PAST_LEN // CMP_STRIDE
    wlen = -(-(wb + tp) // LANES) * LANES
    first = PAST_LEN // SEL_BLOCK - NEAR_BLOCKS
    assert PAST_LEN - ((first + 1) * SEL_BLOCK - 1) >= REL_MAX_DIST
    lo, hi = -wlen, PAST_LEN + tp
    rev = _bias_line(rel_bias, lo, hi, descending=True)

    def rows(tbl):
        return tbl.reshape(NSA_KVH, NSA_G * tp, tbl.shape[-1])

    t_c = _toeplitz(rev, hi - 1 - (PAST_LEN - (CMP_BLOCK - 1)), tp, CMP_STRIDE * ncmp)[:, :, ::CMP_STRIDE]
    t_w = _toeplitz(rev, hi - 1 - wb, tp, wlen)
    far = jnp.broadcast_to(rev[:, hi - 1 - REL_MAX_DIST][:, None, None], (NSA_HEADS, tp, LANES))
    near = []
    for k in range(1, NEAR_BLOCKS + 1):
        half = _toeplitz(rev, hi - 1 - (PAST_LEN - (first + k) * SEL_BLOCK), tp, SEL_BLOCK)
        near.append(jnp.concatenate([half, half], axis=-1))
    t_s = jnp.stack([far] + near, axis=1).reshape(NSA_KVH, NSA_G, NEAR_BLOCKS + 1, tp, LANES)
    t_s = t_s.transpose(0, 2, 1, 3, 4).reshape(NSA_KVH, NEAR_BLOCKS + 1, NSA_G * tp, LANES)
    return rows(t_c), rows(t_w), t_s


def _tail_even(w):
    return _tail_relayout(w, EVEN_KV_OFF + 6 * NSA_KV_W, 3 * NSA_HEADS, NSA_W + MEM_W, EVEN_B_N)


def _tail_odd(w):
    return _tail_relayout(w, ODD_A_N, 2 * ML_HEADS, ML_V_W + MEM_W, ODD_B_N)


def _gate_bias_even(b_gate):
    return jnp.pad(b_gate, (0, LANES - 3 * NSA_HEADS)).reshape(1, LANES)


def _gate_bias_odd(b_if):
    return jnp.pad(b_if.reshape(2 * ML_HEADS), (0, LANES - 2 * ML_HEADS)).reshape(1, LANES)


def _rel_bucket(dist):
    n = np.maximum(dist, 0)
    exact = REL_BUCKETS // 2
    nf = np.maximum(n, 1).astype(np.float32)
    large = exact + (np.log(nf / exact) / math.log(REL_MAX_DIST / exact) * (REL_BUCKETS - exact)).astype(np.int32)
    return np.where(n < exact, n, np.minimum(large, REL_BUCKETS - 1))


def _bias_line(rel_bias, lo, hi, descending=False):
    dist = np.arange(hi - 1, lo - 1, -1) if descending else np.arange(lo, hi)
    buckets = _rel_bucket(dist)
    edges = np.flatnonzero(np.diff(buckets)) + 1
    starts = np.concatenate([[0], edges])
    ends = np.concatenate([edges, [hi - lo]])
    bias_t = rel_bias.T.astype(F32)
    runs = [jnp.broadcast_to(bias_t[:, int(buckets[s])][:, None], (NSA_HEADS, int(e - s))) for s, e in zip(starts, ends)]
    return jnp.concatenate(runs, axis=1)


def _skew_rows(v, rows, step, cols):
    n = v.shape[1]
    reps = -(-rows * (n + step) // n)
    return jnp.tile(v, (1, reps))[:, :rows * (n + step)].reshape(v.shape[0], rows, n + step)[:, :, :cols]


def _toeplitz(rev, start, rows, cols):
    seg = rev[:, start - (rows - 1):start + cols]
    return _skew_rows(jnp.roll(seg, -(rows - 1), axis=1), rows, -1, cols)


def _prompt_bias_tables(rel_bias, T):
    ncmp = T // CMP_STRIDE
    assert Q_BLOCK + 1 >= REL_MAX_DIST
    lo, hi = -(CMP_STRIDE * ncmp + CMP_BLOCK), T
    line = _bias_line(rel_bias, lo, hi)
    rev = _bias_line(rel_bias, lo, hi, descending=True)

    def split(tbl):
        return tbl.reshape((NSA_KVH, NSA_G) + tbl.shape[1:])

    back = CMP_STRIDE * (ncmp - 1)
    first = -(back + CMP_BLOCK - 1) - lo
    seg = line[:, first:first + T + back]
    t_c = _skew_rows(jnp.roll(seg, -back, axis=1), ncmp, -CMP_STRIDE, T).swapaxes(1, 2)
    far = rev[:, hi - 1 - REL_MAX_DIST]
    t_near = _toeplitz(rev, hi - 1 - Q_BLOCK, Q_BLOCK, NEAR_COLS) - far[:, None, None]
    return split(t_c), split(t_near)


def _nsa_sample(ya, yb, kv16, page_table, pk_cmp, pv_cmp, pk_sel, pv_sel, wk, wv, bg_r, w1, b1, w2, pe, rel_bias,
                *, B, T):
    assert (PAST_LEN + T) // CMP_STRIDE == PAST_LEN // CMP_STRIDE
    abk = _cmp_pages_call(pk_cmp, page_table, w1[0], pe[0], B=B)
    abv = _cmp_pages_call(pv_cmp, page_table, w1[1], pe[1], B=B)
    bias_c, bias_w, tbl = _sample_bias_tables(rel_bias, T, wk.shape[1])
    o_cmp, o_win, idx = _nsa_sample_main_call(ya, kv16[4], kv16[5], abk, abv, b1, w2, wk, wv, bias_c, bias_w, B=B, T=T)
    return _nsa_sample_sel_call(ya, yb, kv16[2], kv16[3], idx, page_table, pk_sel, pv_sel, tbl, o_cmp, o_win, bg_r,
                                B=B, T=T)


def _kv_project(x, wt):
    outs = [_matmul_heads(x, wt, first=EVEN_KV_OFF + j * NSA_KV_W, transposed=True) for j in range(6)]
    return [o[0] for o in outs], [o[1] for o in outs]


def _even_prompt(hp2d, npre, mk16, mv16, wt, wt_b, bg_r, w1, b1, w2, pe, lb, g_norm, w_out, rel_bias, *, B, T):
    ya, yb = _matmul_nt(npre, wt, tm=W_TILE_M, tn=W_TILE_N, rows=(0, EVEN_A_N)), _matmul_nt(npre, wt_b)
    kv32, kv16 = _kv_project(npre, wt)
    oa, s_new = _hgrn_call(ya, jnp.zeros((B, HG_HEADS, HG_DK, HG_DV), F32), lb, g_norm, B=B, T=T, L=CHUNK, valid=CHUNK)
    kcmp = _compress_call(kv16[0], w1[0], b1[0], w2[0], pe[0], B=B, T=T)
    vcmp = _compress_call(kv16[1], w1[1], b1[1], w2[1], pe[1], B=B, T=T)
    ob = _nsa_prompt_call(ya, yb, kv16[2:], kcmp, vcmp, bg_r, *_prompt_bias_tables(rel_bias, T), B=B, T=T)
    om = _mem_call(yb, EVEN_B["qm"], mk16, mv16, B=B, T=T)
    h_new = _outproj([oa, ob, om], w_out, hp2d)
    wb = min(WINDOW, T)
    rows = [r.reshape(B, T, NSA_KVH, NSA_HD) for r in kv32]
    return h_new, (rows[0], rows[1], rows[2], rows[3], rows[4][:, -wb:], rows[5][:, -wb:], s_new)


def _even_sample(hs2d, nsam, mk_s, mv_s, page_table, pk_cmp, pv_cmp, pk_sel, pv_sel, wk, wv, s0,
                 wt, wt_b, bg_r, w1, b1, w2, pe, lb, g_norm, w_out, rel_bias, *, B, T):
    tp = SAMPLE_PAD_T
    ya, yb = _matmul_nt(nsam, wt, tm=W_TILE_M, tn=W_TILE_N, rows=(0, EVEN_A_N)), _matmul_nt(nsam, wt_b)
    kv32, kv16 = _kv_project(nsam, wt)
    oa, s_new = _hgrn_call(ya, s0, lb, g_norm, B=B, T=tp, L=tp, valid=T)
    ob = _nsa_sample(ya, yb, kv16, page_table, pk_cmp, pv_cmp, pk_sel, pv_sel, wk, wv, bg_r, w1, b1, w2, pe, rel_bias,
                     B=B, T=T)
    om = _mem_call(yb, EVEN_B["qm"], mk_s.reshape(B * N_MEM, MEM_W), mv_s.reshape(B * N_MEM, MEM_W), B=B, T=tp)
    rows = [r.reshape(B, tp, NSA_KVH, NSA_HD)[:, :T] for r in kv32]
    wb = wk.shape[1]
    win_k = jnp.concatenate([wk, rows[4]], axis=1)[:, -wb:]
    win_v = jnp.concatenate([wv, rows[5]], axis=1)[:, -wb:]
    return _outproj([oa, ob, om], w_out, hs2d), (rows[0], rows[1], rows[2], rows[3], win_k, win_v, s_new)


def _odd_mix(h2d, hn, k2d, v2d, c0, n0, m0, wt, wt_b, bif_r, g_norm, w_out, *, B, T, L, valid):
    ya, yb = _matmul_nt(hn, wt, tm=W_TILE_M, tn=W_TILE_N, rows=(0, ODD_A_N)), _matmul_nt(hn, wt_b)
    h, c_new, n_new, m_new = _mlstm_call(ya, yb, c0, n0, m0, bif_r, g_norm, B=B, T=T, L=L, valid=valid)
    om = _mem_call(yb, ODD_B["qm"], k2d, v2d, B=B, T=T)
    return _outproj([h, om], w_out, h2d), (c_new, n_new, m_new)


def _stack(lst, i):
    return jnp.stack([t[i] for t in lst])


def kernel(x_prompt, x_sample, cache_mem_k, cache_mem_v, cache_cmp_k, cache_cmp_v, cache_sel_k, cache_sel_v,
           cache_win_k, cache_win_v, state_hgrn, state_mlstm_c, state_mlstm_n, state_mlstm_m, page_table,
           mem_prompt, norm_w, mem_norm_w, final_norm_w, rel_bias, w_mem_kv, w_in_even, b_nsa_gate,
           w_cmp1, b_cmp1, w_cmp2, pe_cmp, hgrn_lb_logits, hgrn_norm_w, w_out_even, w_in_odd, b_mlstm_if,
           mlstm_norm_w, w_out_odd):
    bp, tp = x_prompt.shape[:2]
    bs, ts = x_sample.shape[:2]
    tsp = SAMPLE_PAD_T
    lbs = jnp.cumsum(jax.nn.softmax(hgrn_lb_logits.astype(F32), axis=0), axis=0)
    hp = x_prompt.reshape(bp * tp, D_MODEL)
    hs = jnp.pad(x_sample, ((0, 0), (0, tsp - ts), (0, 0))).reshape(bs * tsp, D_MODEL)
    mem2d = mem_prompt.reshape(bp * N_MEM, D_MODEL)
    mem_new, even_p, even_s, odd_p, odd_s = [], [], [], [], []
    for l in range(DEPTH):
        npre = _rmsnorm_rows(hp, norm_w[l], BF16)
        nsam = _rmsnorm_rows(hs, norm_w[l], BF16)
        nmem = _rmsnorm_rows(mem2d, mem_norm_w[l], BF16)
        mk32, mk16 = _matmul_heads(nmem, w_mem_kv[l], first=0)
        mv32, mv16 = _matmul_heads(nmem, w_mem_kv[l], first=MEM_W)
        mem_new.append((mk32.reshape(bp, N_MEM, MEM_HEADS, MEM_HD), mv32.reshape(bp, N_MEM, MEM_HEADS, MEM_HD)))
        mk_s, mv_s = cache_mem_k[l], cache_mem_v[l]
        if l % 2 == 0:
            e = l // 2
            w_in = w_in_even[e].T
            w_b = _tail_even(w_in)
            w_out = w_out_even[e].astype(BF16)
            bg_r = _gate_bias_even(b_nsa_gate[e])
            cmpw = (w_cmp1[e].reshape(2, CMP_BLOCK, NSA_HD, NSA_HD), b_cmp1[e], w_cmp2[e], pe_cmp[e])
            hp, st_p = _even_prompt(hp, npre, mk16, mv16, w_in, w_b, bg_r, *cmpw, lbs[l], hgrn_norm_w[e], w_out,
                                    rel_bias, B=bp, T=tp)
            hs, st_s = _even_sample(hs, nsam, mk_s, mv_s, page_table, cache_cmp_k[e], cache_cmp_v[e], cache_sel_k[e],
                                    cache_sel_v[e], cache_win_k[e], cache_win_v[e], state_hgrn[e], w_in, w_b, bg_r,
                                    *cmpw, lbs[l], hgrn_norm_w[e], w_out, rel_bias, B=bs, T=ts)
            even_p.append(st_p)
            even_s.append(st_s)
        else:
            o = l // 2
            w_in = w_in_odd[o].T
            w_b = _tail_odd(w_in)
            w_out = w_out_odd[o].astype(BF16)
            bif_r = _gate_bias_odd(b_mlstm_if[o])
            hp, st_p = _odd_mix(hp, npre, mk16, mv16, jnp.zeros((bp, ML_HEADS, ML_DV, ML_DK), F32),
                                jnp.zeros((bp, ML_HEADS, ML_DK), F32), jnp.zeros((bp, ML_HEADS), F32),
                                w_in, w_b, bif_r, mlstm_norm_w[o], w_out, B=bp, T=tp, L=ML_CHUNK, valid=ML_CHUNK)
            hs, st_s = _odd_mix(hs, nsam, mk_s.reshape(bs * N_MEM, MEM_W), mv_s.reshape(bs * N_MEM, MEM_W),
                                state_mlstm_c[o], state_mlstm_n[o], state_mlstm_m[o],
                                w_in, w_b, bif_r, mlstm_norm_w[o], w_out, B=bs, T=tsp, L=tsp, valid=ts)
            odd_p.append(st_p)
            odd_s.append(st_s)
    y_prompt = _rmsnorm_rows(hp, final_norm_w, F32).reshape(bp, tp, D_MODEL)
    y_sample = _rmsnorm_rows(hs, final_norm_w, F32).reshape(bs, tsp, D_MODEL)[:, :ts]
    return (y_prompt, y_sample,
            _stack(mem_new, 0), _stack(mem_new, 1),
            _stack(even_p, 0), _stack(even_p, 1), _stack(even_p, 2), _stack(even_p, 3),
            _stack(even_p, 4), _stack(even_p, 5), _stack(even_p, 6),
            _stack(odd_p, 0), _stack(odd_p, 1), _stack(odd_p, 2),
            _stack(even_s, 0), _stack(even_s, 1), _stack(even_s, 2), _stack(even_s, 3),
            _stack(even_s, 4), _stack(even_s, 5), _stack(even_s, 6),
            _stack(odd_s, 0), _stack(odd_s, 1), _stack(odd_s, 2))
```

```python
import functools
import math

import jax
import jax.numpy as jnp
import numpy as np
from jax import lax
from jax.experimental import pallas as pl
from jax.experimental.pallas import tpu as pltpu

D_MODEL = 4096
DEPTH = 2
PAST_LEN = 16384
PAGE_SIZE = 128
N_MEM = 256
EPS = 1e-6
CHUNK = 64

HG_DK = 128
HG_DV = 128
HG_HEADS = D_MODEL // 2 // HG_DV
HG_W = HG_HEADS * HG_DV

NSA_HD = 128
NSA_HEADS = D_MODEL // 2 // NSA_HD
NSA_KVH = 4
NSA_G = NSA_HEADS // NSA_KVH
NSA_W = NSA_HEADS * NSA_HD
NSA_KV_W = NSA_KVH * NSA_HD
CMP_BLOCK = 32
CMP_STRIDE = 16
SEL_BLOCK = 64
SEL_SHIFT = SEL_BLOCK.bit_length() - 1
N_SEL = 16
WINDOW = 512
Q_BLOCK = 256

ML_HEADS = D_MODEL // 512
ML_DK = D_MODEL // 2 // ML_HEADS
ML_DV = D_MODEL // ML_HEADS
ML_QK_W = ML_HEADS * ML_DK
ML_V_W = ML_HEADS * ML_DV

MEM_HEADS = 4
MEM_HD = 128
MEM_W = MEM_HEADS * MEM_HD

REL_BUCKETS = 32
REL_MAX_DIST = 128

F32 = jnp.float32
BF16 = jnp.bfloat16
LANES = 128
NEG_INF = float("-inf")
TINY = float(np.finfo(np.float32).tiny)
EXP_CLAMP = 80.0
VMEM_LIMIT = 56 * 1024 * 1024

HG_HB = 16
ML_HB = 2
ML_CHUNK = 256
W_TILE_M, W_TILE_N = 1024, 512
HG_SUB = 16
SAMPLE_PAD_T = 16

MM_TILE_N = 1024
EVEN_A = {"qa": 0, "fa": HG_W, "ia": 2 * HG_W, "za": 3 * HG_W, "qb": 4 * HG_W}
EVEN_A_N = 4 * HG_W + NSA_W
EVEN_B = {"zb": 0, "qm": NSA_W, "gb": NSA_W + MEM_W}
EVEN_B_N = -(-(NSA_W + MEM_W + LANES) // MM_TILE_N) * MM_TILE_N
EVEN_KV_OFF = EVEN_A_N
ODD_A = {"q": 0, "k": ML_QK_W, "v": 2 * ML_QK_W, "og": 2 * ML_QK_W + ML_V_W}
ODD_A_N = 2 * ML_QK_W + 2 * ML_V_W
ODD_B = {"z": 0, "qm": ML_V_W, "gates": ML_V_W + MEM_W}
ODD_B_N = -(-(ML_V_W + MEM_W + LANES) // MM_TILE_N) * MM_TILE_N


def _dot(a, b):
    return jnp.dot(a, b, preferred_element_type=F32)


def _dot_nt(a, b):
    return lax.dot_general(a, b, (((1,), (1,)), ((), ())), preferred_element_type=F32)


def _dot_tn(a, b):
    return lax.dot_general(a, b, (((0,), (0,)), ((), ())), preferred_element_type=F32)


def _iota2(shape, dim):
    return lax.broadcasted_iota(jnp.int32, shape, dim)


def _cumsum_rows(x, tri_b):
    hi = x.astype(BF16)
    r1 = x - hi.astype(F32)
    mid = r1.astype(BF16)
    lo = (r1 - mid.astype(F32)).astype(BF16)
    return _dot(tri_b, hi) + _dot(tri_b, mid) + _dot(tri_b, lo)


def _row_to_col(row, n):
    eye = _iota2((n, n), 0) == _iota2((n, n), 1)
    return jnp.sum(jnp.where(eye, row, 0.0), axis=1, keepdims=True)


def _col_to_row(col, n):
    eye = _iota2((n, n), 0) == _iota2((n, n), 1)
    return jnp.sum(jnp.where(eye, col, 0.0), axis=0, keepdims=True)


def _lane_col(x, idx):
    return jnp.sum(jnp.where(_iota2(x.shape, 1) == idx, x, 0.0), axis=1, keepdims=True)


def _silu(x):
    return x * jax.nn.sigmoid(x)


def _params(sem):
    return pltpu.CompilerParams(dimension_semantics=sem, vmem_limit_bytes=VMEM_LIMIT)


def _rmsnorm_body(x_ref, w_ref, o_ref):
    x = x_ref[...].astype(F32)
    y = x * lax.rsqrt(jnp.mean(x * x, axis=-1, keepdims=True) + EPS)
    o_ref[...] = (y * w_ref[...].astype(F32)).astype(o_ref.dtype)


def _rmsnorm_rows(x2d, w, out_dtype, tm=256):
    m, d = x2d.shape
    tm = min(tm, m)
    return pl.pallas_call(
        _rmsnorm_body,
        grid=(m // tm,),
        in_specs=[pl.BlockSpec((tm, d), lambda i: (i, 0)), pl.BlockSpec((1, d), lambda i: (0, 0))],
        out_specs=pl.BlockSpec((tm, d), lambda i: (i, 0)),
        out_shape=jax.ShapeDtypeStruct((m, d), out_dtype),
        compiler_params=_params(("parallel",)),
        name="rmsnorm",
    )(x2d, w.reshape(1, d))


def _matmul_nt_body(a_ref, bt_ref, o_ref):
    o_ref[...] = _dot_nt(a_ref[...], bt_ref[...].astype(BF16))


def _matmul_nt(a, bt, tm=1024, tn=MM_TILE_N, rows=None):
    m, k = a.shape
    first, n = rows or (0, bt.shape[0])
    tm, tn = min(tm, m), min(tn, n)
    assert m % tm == 0 and n % tn == 0 and first % tn == 0, (a.shape, bt.shape, rows)
    j0 = first // tn
    return pl.pallas_call(
        _matmul_nt_body,
        grid=(m // tm, n // tn),
        in_specs=[pl.BlockSpec((tm, k), lambda i, j: (i, 0)), pl.BlockSpec((tn, k), lambda i, j: (j0 + j, 0))],
        out_specs=pl.BlockSpec((tm, tn), lambda i, j: (i, j)),
        out_shape=jax.ShapeDtypeStruct((m, n), F32),
        compiler_params=_params(("parallel", "parallel")),
        name="matmul",
    )(a, bt)


def _tail_body(lo_ref, hi_ref, gate_ref, o_ref, *, shift, n_main):
    i = pl.program_id(0)
    main = jnp.concatenate([lo_ref[shift:, :], hi_ref[:shift, :]], axis=0)
    gates = jnp.where(_iota2((LANES, 1), 0) < shift, gate_ref[...], 0.0)
    o_ref[...] = jnp.where(i < n_main, main, jnp.where(i == n_main, gates, 0.0)).astype(o_ref.dtype)


def _tail_relayout(wt, first, shift, main, out_rows):
    n, k = wt.shape
    assert first % LANES == 0 and main % LANES == 0 and out_rows % LANES == 0 and shift % 8 == 0 and shift < LANES
    assert first + shift + main == n
    c0, n_main = first // LANES, main // LANES
    return pl.pallas_call(
        functools.partial(_tail_body, shift=shift, n_main=n_main),
        grid=(out_rows // LANES,),
        in_specs=[pl.BlockSpec((LANES, k), lambda i: (c0 + jnp.minimum(i, n_main - 1), 0)),
                  pl.BlockSpec((LANES, k), lambda i: (c0 + jnp.minimum(i, n_main - 1) + 1, 0)),
                  pl.BlockSpec((LANES, k), lambda i: (c0, 0))],
        out_specs=pl.BlockSpec((LANES, k), lambda i: (i, 0)),
        out_shape=jax.ShapeDtypeStruct((out_rows, k), BF16),
        compiler_params=_params(("parallel",)),
        name="tail_relayout",
    )(wt, wt, wt)


def _matmul_heads_body(a_ref, b_ref, o32_ref, o16_ref, *, transposed):
    b = b_ref[...].astype(BF16)
    acc = _dot_nt(a_ref[...], b) if transposed else _dot(a_ref[...], b)
    for h in range(MEM_HEADS):
        o32_ref[:, h, :] = acc[:, h * LANES:(h + 1) * LANES]
    o16_ref[...] = acc.astype(BF16)


def _matmul_heads(a, b, first=0, transposed=False, tm=1024):
    m, k = a.shape
    n = MEM_HEADS * LANES
    tm = min(tm, m)
    assert m % tm == 0 and first % n == 0, (a.shape, b.shape, first)
    j0 = first // n
    b_spec = pl.BlockSpec((n, k), lambda i: (j0, 0)) if transposed else pl.BlockSpec((k, n), lambda i: (0, j0))
    return pl.pallas_call(
        functools.partial(_matmul_heads_body, transposed=transposed),
        grid=(m // tm,),
        in_specs=[pl.BlockSpec((tm, k), lambda i: (i, 0)), b_spec],
        out_specs=[pl.BlockSpec((tm, MEM_HEADS, LANES), lambda i: (i, 0, 0)), pl.BlockSpec((tm, n), lambda i: (i, 0))],
        out_shape=[jax.ShapeDtypeStruct((m, MEM_HEADS, LANES), F32), jax.ShapeDtypeStruct((m, n), BF16)],
        compiler_params=_params(("parallel",)),
        name="matmul_heads",
    )(a, b)


def _outproj_body(*refs, widths):
    xs = refs[:len(widths)]
    w_ref, r_ref, o_ref = refs[len(widths):]
    acc = r_ref[...]
    off = 0
    for x_ref, w in zip(xs, widths):
        acc = acc + _dot(x_ref[...], w_ref[off:off + w, :])
        off += w
    o_ref[...] = acc


def _outproj(xs, w_bf16, resid, tm=1024, tn=512):
    m = resid.shape[0]
    n = w_bf16.shape[1]
    widths = tuple(x.shape[1] for x in xs)
    assert sum(widths) == w_bf16.shape[0]
    tm = min(tm, m)
    in_specs = [pl.BlockSpec((tm, w), lambda i, j: (i, 0)) for w in widths]
    in_specs += [pl.BlockSpec((w_bf16.shape[0], tn), lambda i, j: (0, j)), pl.BlockSpec((tm, tn), lambda i, j: (i, j))]
    return pl.pallas_call(
        functools.partial(_outproj_body, widths=widths),
        grid=(m // tm, n // tn),
        in_specs=in_specs,
        out_specs=pl.BlockSpec((tm, tn), lambda i, j: (i, j)),
        out_shape=jax.ShapeDtypeStruct((m, n), F32),
        compiler_params=_params(("parallel", "parallel")),
        name="outproj",
    )(*xs, w_bf16, resid)


def _hgrn_body(qa_ref, fa_ref, ia_ref, za_ref, lb_ref, gn_ref, s0_ref, o_ref, s_out, s_scr, *, L, valid):
    c = pl.program_id(2)

    @pl.when(c == 0)
    def _():
        s_scr[...] = s0_ref[...]

    lb = lb_ref[...]
    sig = jax.nn.sigmoid(fa_ref[...])
    logf = jnp.log(lb + (1.0 - lb) * sig)
    kk = (1.0 - lb) * (1.0 - sig)
    if valid < L:
        live = _iota2((L, 1), 0) < valid
        logf = jnp.where(live, logf, 0.0)
        kk = jnp.where(live, kk, 0.0)
    tri_b = (_iota2((L, L), 0) >= _iota2((L, L), 1)).astype(BF16)
    bc = _cumsum_rows(logf, tri_b)
    q = _silu(qa_ref[...])
    gate = _silu(za_ref[...])
    v = ia_ref[...]
    gn = gn_ref[...]
    nsub = L // HG_SUB
    rr = _iota2((L, nsub * L), 0)
    cc = _iota2((L, nsub * L), 1)
    keep = ((jnp.right_shift(cc, L.bit_length() - 1) == jnp.right_shift(rr, HG_SUB.bit_length() - 1))
            & (jnp.bitwise_and(cc, L - 1) <= rr))
    for j in range(HG_HB):
        sl = slice(j * HG_DK, (j + 1) * HG_DK)
        bj, qj, kj = bc[:, sl], q[:, sl], kk[:, sl]
        vb = v[:, sl].astype(BF16)
        s_prev = s_scr[j]
        inter = _dot((qj * jnp.exp(bj)).astype(BF16), s_prev.astype(BF16))
        mids = [bj[i * HG_SUB + HG_SUB // 2:i * HG_SUB + HG_SUB // 2 + 1, :] for i in range(nsub)]
        mid_rows = jnp.concatenate([jnp.broadcast_to(m, (HG_SUB, HG_DK)) for m in mids], axis=0)
        q_dec = qj * jnp.exp(jnp.minimum(bj - mid_rows, EXP_CLAMP))
        k_dec = jnp.concatenate([kj * jnp.exp(jnp.minimum(m - bj, EXP_CLAMP)) for m in mids], axis=0)
        att = jnp.where(keep, _dot_nt(q_dec.astype(BF16), k_dec.astype(BF16)), 0.0)
        o = inter + _dot(att.astype(BF16), jnp.concatenate([vb] * nsub, axis=0))
        o_n = o * lax.rsqrt(jnp.mean(o * o, axis=-1, keepdims=True) + EPS) * gn
        o_ref[:, sl] = (o_n * gate[:, sl]).astype(o_ref.dtype)
        bl = bj[L - 1:L, :]
        kd = kj * jnp.exp(bl - bj)
        s_scr[j] = _row_to_col(jnp.exp(bl), HG_DK) * s_prev + _dot_tn(kd.astype(BF16), vb)

    @pl.when(c == pl.num_programs(2) - 1)
    def _():
        s_out[...] = s_scr[...]


def _hgrn_call(y, s0, lb, gn, *, B, T, L, valid):
    nc = T // L
    w = HG_HB * HG_DK

    def col(name):
        blk = EVEN_A[name] // w
        return pl.BlockSpec((L, w), lambda b, hg, c: (b * nc + c, blk + hg))

    state_spec = pl.BlockSpec((None, HG_HB, HG_DK, HG_DV), lambda b, hg, c: (b, hg, 0, 0))
    return pl.pallas_call(
        functools.partial(_hgrn_body, L=L, valid=valid),
        grid=(B, HG_HEADS // HG_HB, nc),
        in_specs=[col("qa"), col("fa"), col("ia"), col("za"),
                  pl.BlockSpec((1, w), lambda b, hg, c: (0, hg)),
                  pl.BlockSpec((1, HG_DV), lambda b, hg, c: (0, 0)),
                  state_spec],
        out_specs=[pl.BlockSpec((L, w), lambda b, hg, c: (b * nc + c, hg)), state_spec],
        out_shape=[jax.ShapeDtypeStruct((B * T, HG_W), BF16),
                   jax.ShapeDtypeStruct((B, HG_HEADS, HG_DK, HG_DV), F32)],
        scratch_shapes=[pltpu.VMEM((HG_HB, HG_DK, HG_DV), F32)],
        compiler_params=_params(("arbitrary", "arbitrary", "arbitrary")),
        name="hgrn2",
    )(y, y, y, y, lb.reshape(1, HG_W), gn.reshape(1, HG_DV), s0)


def _mlstm_body(q_ref, k_ref, v_ref, og_ref, z_ref, g_ref, bif_ref, gn_ref, c0_ref, n0_ref, m0_ref,
                h_ref, c_out, n_out, m_out, c_scr, n_scr, m_scr, *, L, valid):
    c = pl.program_id(2)

    @pl.when(c == 0)
    def _():
        c_scr[...] = c0_ref[...]
        n_scr[...] = n0_ref[...]
        m_scr[...] = m0_ref[...]

    gates = g_ref[...] + bif_ref[...]
    log_i = gates
    log_f = jnp.minimum(gates, 0.0) - jnp.log(1.0 + jnp.exp(-jnp.abs(gates)))
    if valid < L:
        live = _iota2((L, 1), 0) < valid
        log_i = jnp.where(live, log_i, -1e30)
        log_f = jnp.where(live, log_f, 0.0)
    tri = _iota2((L, L), 0) >= _iota2((L, L), 1)
    bcs = _cumsum_rows(log_f, tri.astype(BF16))
    for j in range(ML_HB):
        head = pl.program_id(1) * ML_HB + j
        b_col = _lane_col(bcs, ML_HEADS + head)
        i_col = _lane_col(log_i, head)
        b_row = _col_to_row(b_col, L)
        i_row = _col_to_row(i_col, L)
        m_prev = m_scr[:, j:j + 1]
        dmat = jnp.where(tri, b_col - b_row + i_row, NEG_INF)
        inter = b_col + m_prev
        mt = jnp.maximum(inter, jnp.max(dmat, axis=1, keepdims=True))
        w_in = jnp.exp(dmat - mt)
        w_x = jnp.exp(inter - mt)
        qj = q_ref[:, j * ML_DK:(j + 1) * ML_DK]
        kj = k_ref[:, j * ML_DK:(j + 1) * ML_DK] * (ML_DK ** -0.5)
        vj = v_ref[:, j * ML_DV:(j + 1) * ML_DV]
        qb, kb = qj.astype(BF16), kj.astype(BF16)
        sw = _dot_nt(qb, kb) * w_in
        c_prev = c_scr[j]
        n_prev = n_scr[:, j * ML_DK:(j + 1) * ML_DK]
        num = w_x * _dot_nt(qb, c_prev.astype(BF16)) + _dot(sw.astype(BF16), vj.astype(BF16))
        den = w_x * jnp.sum(qj * n_prev, axis=1, keepdims=True) + jnp.sum(sw, axis=1, keepdims=True)
        h = num / jnp.maximum(jnp.abs(den), jnp.exp(-mt))
        m_last = mt[L - 1:L, :]
        b_last = b_col[L - 1:L, :]
        w_end = jnp.exp(b_last - b_col + i_col - m_last)
        d_c = jnp.exp(b_last + m_prev - m_last)
        c_scr[j] = d_c * c_prev + _dot_tn((w_end * vj).astype(BF16), kb)
        n_scr[:, j * ML_DK:(j + 1) * ML_DK] = d_c * n_prev + jnp.sum(w_end * kj, axis=0, keepdims=True)
        m_scr[:, j:j + 1] = m_last
        sv = slice(j * ML_DV, (j + 1) * ML_DV)
        h_n = h * lax.rsqrt(jnp.mean(h * h, axis=-1, keepdims=True) + EPS) * gn_ref[:, sv]
        h_ref[:, sv] = (h_n * jax.nn.sigmoid(og_ref[:, sv]) * _silu(z_ref[:, sv])).astype(h_ref.dtype)

    @pl.when(c == pl.num_programs(2) - 1)
    def _():
        c_out[...] = c_scr[...]
        n_out[...] = n_scr[...]
        m_out[...] = m_scr[...]


def _mlstm_call(ya, yb, c0, n0, m0, bif_r, gn, *, B, T, L, valid):
    nc = T // L
    ng = ML_HEADS // ML_HB
    wk, wv = ML_HB * ML_DK, ML_HB * ML_DV

    def col(name, w):
        blk = (ODD_A[name] if name in ODD_A else ODD_B[name]) // w
        return pl.BlockSpec((L, w), lambda b, hg, c: (b * nc + c, blk + hg))

    c_spec = pl.BlockSpec((None, ML_HB, ML_DV, ML_DK), lambda b, hg, c: (b, hg, 0, 0))
    n_spec = pl.BlockSpec((None, 1, wk), lambda b, hg, c: (b, 0, hg))
    m_spec = pl.BlockSpec((None, None, 1, LANES), lambda b, hg, c: (b, hg, 0, 0))
    m0_r = jnp.pad(m0.reshape(B, ng, 1, ML_HB), ((0, 0), (0, 0), (0, 0), (0, LANES - ML_HB)))
    h, c_new, n_new, m_new = pl.pallas_call(
        functools.partial(_mlstm_body, L=L, valid=valid),
        grid=(B, ng, nc),
        in_specs=[col("q", wk), col("k", wk), col("v", wv), col("og", wv), col("z", wv),
                  pl.BlockSpec((L, LANES), lambda b, hg, c: (b * nc + c, ODD_B["gates"] // LANES)),
                  pl.BlockSpec((1, LANES), lambda b, hg, c: (0, 0)),
                  pl.BlockSpec((1, wv), lambda b, hg, c: (0, hg)),
                  c_spec, n_spec, m_spec],
        out_specs=[pl.BlockSpec((L, wv), lambda b, hg, c: (b * nc + c, hg)), c_spec, n_spec, m_spec],
        out_shape=[jax.ShapeDtypeStruct((B * T, ML_V_W), BF16),
                   jax.ShapeDtypeStruct((B, ML_HEADS, ML_DV, ML_DK), F32),
                   jax.ShapeDtypeStruct((B, 1, ML_QK_W), F32),
                   jax.ShapeDtypeStruct((B, ng, 1, LANES), F32)],
        scratch_shapes=[pltpu.VMEM((ML_HB, ML_DV, ML_DK), F32), pltpu.VMEM((1, wk), F32), pltpu.VMEM((1, LANES), F32)],
        compiler_params=_params(("arbitrary", "arbitrary", "arbitrary")),
        name="mlstm",
    )(ya, ya, ya, ya, yb, yb, bif_r, gn.reshape(1, ML_V_W), c0, n0.reshape(B, 1, ML_QK_W), m0_r)
    return h, c_new, n_new.reshape(B, ML_HEADS, ML_DK), m_new[:, :, 0, :ML_HB].reshape(B, ML_HEADS)


def _mem_body(q_ref, k_ref, v_ref, o_ref):
    q = q_ref[...] * (MEM_HD ** -0.5)
    for h in range(MEM_HEADS):
        sl = slice(h * MEM_HD, (h + 1) * MEM_HD)
        s = _dot_nt(q[:, sl].astype(BF16), k_ref[:, sl].astype(BF16))
        p = jnp.exp(s - jnp.max(s, axis=-1, keepdims=True))
        o = _dot(p.astype(BF16), v_ref[:, sl].astype(BF16)) / jnp.sum(p, axis=-1, keepdims=True)
        o_ref[:, sl] = o.astype(o_ref.dtype)


def _mem_call(y, q_off, k2d, v2d, *, B, T, tq=256):
    tq = min(tq, T)
    nq = T // tq
    qb = q_off // MEM_W
    return pl.pallas_call(
        _mem_body,
        grid=(B, nq),
        in_specs=[pl.BlockSpec((tq, MEM_W), lambda b, i: (b * nq + i, qb)),
                  pl.BlockSpec((N_MEM, MEM_W), lambda b, i: (b, 0)),
                  pl.BlockSpec((N_MEM, MEM_W), lambda b, i: (b, 0))],
        out_specs=pl.BlockSpec((tq, MEM_W), lambda b, i: (b * nq + i, 0)),
        out_shape=jax.ShapeDtypeStruct((B * T, MEM_W), BF16),
        compiler_params=_params(("parallel", "parallel")),
        name="mem_attn",
    )(y, k2d, v2d)


def _gelu_tanh(x):
    return 0.5 * x * (1.0 + jnp.tanh(math.sqrt(2.0 / math.pi) * (x + 0.044715 * (x * x * x))))


def _compress_body(x_ref, w1_ref, b1_ref, w2_ref, pe_ref, o_ref, x32, *, nch):
    x32[...] = x_ref[...].astype(F32)
    a = jnp.zeros((nch, NSA_HD), F32)
    b = jnp.zeros((nch, NSA_HD), F32)
    for s in range(CMP_STRIDE):
        r = x32[pl.ds(s, nch, stride=CMP_STRIDE), :]
        a = a + _dot((r + pe_ref[s:s + 1, :]).astype(BF16), w1_ref[s])
        b = b + _dot((r + pe_ref[CMP_STRIDE + s:CMP_STRIDE + s + 1, :]).astype(BF16), w1_ref[CMP_STRIDE + s])
    h = a + pltpu.roll(b, nch - 1, 0) + b1_ref[...]
    o_ref[...] = _dot(_gelu_tanh(h).astype(BF16), w2_ref[...])


def _compress_call(x16, w1, b1, w2, pe, *, B, T):
    nch = T // CMP_STRIDE
    return pl.pallas_call(
        functools.partial(_compress_body, nch=nch),
        grid=(B, NSA_KVH),
        in_specs=[pl.BlockSpec((T, NSA_HD), lambda b, h: (b, h)),
                  pl.BlockSpec((CMP_BLOCK, NSA_HD, NSA_HD), lambda b, h: (0, 0, 0)),
                  pl.BlockSpec((1, NSA_HD), lambda b, h: (0, 0)),
                  pl.BlockSpec((NSA_HD, NSA_HD), lambda b, h: (0, 0)),
                  pl.BlockSpec((CMP_BLOCK, NSA_HD), lambda b, h: (0, 0))],
        out_specs=pl.BlockSpec((None, None, nch, NSA_HD), lambda b, h: (b, h, 0, 0)),
        out_shape=jax.ShapeDtypeStruct((B, NSA_KVH, nch, NSA_HD), F32),
        scratch_shapes=[pltpu.VMEM((T, NSA_HD), F32)],
        compiler_params=_params(("parallel", "parallel")),
        name="nsa_compress",
    )(x16, w1.astype(BF16), b1.reshape(1, NSA_HD), w2.astype(BF16), pe)


def _softmax_rows(s):
    m = jnp.max(s, axis=-1, keepdims=True)
    m = jnp.where(m == NEG_INF, 0.0, m)
    p = jnp.exp(s - m)
    return p, jnp.sum(p, axis=-1, keepdims=True)


def _slc_scores(psum, width, n_slc):
    ncmp = psum.shape[1]
    d = _iota2((ncmp, width), 0) - (SEL_BLOCK // CMP_STRIDE) * _iota2((ncmp, width), 1)
    wgt = jnp.where((d == -1) | (d == 3), 1.0, jnp.where((d >= 0) & (d <= 2), 2.0, 0.0))
    wgt = jnp.where(_iota2((ncmp, width), 1) < n_slc, wgt, 0.0).astype(BF16)
    p_hi = psum.astype(BF16)
    p_lo = (psum - p_hi.astype(F32)).astype(BF16)
    return _dot(p_hi, wgt) + _dot(p_lo, wgt)


def _top_blocks(slc, cur, n_pick):
    rows, width = slc.shape
    blk = _iota2((rows, width), 1)
    forced = (blk == 0) | (blk == cur) | (blk == cur - 1)
    score = jnp.where(forced, jnp.inf, slc)
    score = jnp.where(blk > cur, NEG_INF, score)
    blk_f = blk.astype(F32)
    lane = _iota2((rows, LANES), 1)
    sel = jnp.zeros((rows, width), F32)
    picks = jnp.zeros((rows, LANES), F32)
    for i in range(n_pick):
        mx = jnp.max(score, axis=-1, keepdims=True)
        first = jnp.min(jnp.where(score == mx, blk_f, float(width)), axis=-1, keepdims=True)
        pick = blk_f == first
        sel = jnp.where(pick, 1.0, sel)
        picks = jnp.where(lane == i, first, picks)
        score = jnp.where(pick, NEG_INF, score)
    return sel, picks


def _member_by_rank(psum, tpos_row, n_slc, n_pick):
    nq, ncmp = psum.shape
    nb = -(-n_slc // 8) * 8
    d = _iota2((nb, ncmp), 1) - (SEL_BLOCK // CMP_STRIDE) * _iota2((nb, ncmp), 0)
    wgt = jnp.where((d == -1) | (d == 3), 1.0, jnp.where((d >= 0) & (d <= 2), 2.0, 0.0))
    wgt = jnp.where(_iota2((nb, ncmp), 0) < n_slc, wgt, 0.0).astype(BF16)
    p_hi = psum.astype(BF16)
    p_lo = (psum - p_hi.astype(F32)).astype(BF16)
    slc = _dot_nt(wgt, p_hi) + _dot_nt(wgt, p_lo)
    blk = _iota2((nb, nq), 0)
    cur = jnp.right_shift(tpos_row, SEL_SHIFT)
    forced = (blk == 0) | (blk == cur) | (blk == cur - 1)
    score = jnp.where(forced, jnp.inf, slc)
    score = jnp.where(blk > cur, NEG_INF, score)
    ahead = jnp.zeros((nb, nq), F32)
    for i in range(n_slc):
        s_i = score[i:i + 1, :]
        ahead = ahead + jnp.where((s_i > score) | ((s_i == score) & (blk > i)), 1.0, 0.0)
    return jnp.where((ahead < n_pick) & (blk <= cur), 1.0, 0.0)


NEAR_COLS = Q_BLOCK + REL_MAX_DIST


def _banded_attention(q, k_ref, v_ref, start, width, mask, near_bias):
    far = width - NEAR_COLS
    s_far = _dot_nt(q, k_ref[pl.ds(start, far), :]) + mask[:, :far]
    s_near = _dot_nt(q, k_ref[pl.ds(start + far, NEAR_COLS), :]) + near_bias + mask[:, far:]
    m = jnp.maximum(jnp.max(s_far, axis=-1, keepdims=True), jnp.max(s_near, axis=-1, keepdims=True))
    m = jnp.where(m == NEG_INF, 0.0, m)
    p_far, p_near = jnp.exp(s_far - m), jnp.exp(s_near - m)
    l = jnp.sum(p_far, axis=-1, keepdims=True) + jnp.sum(p_near, axis=-1, keepdims=True)
    o = (_dot(p_far.astype(BF16), v_ref[pl.ds(start, far), :])
         + _dot(p_near.astype(BF16), v_ref[pl.ds(start + far, NEAR_COLS), :]))
    return o / jnp.maximum(l, TINY)


def _nsa_prompt_body(q_ref, zb_ref, gb_ref, bg_ref, ks_ref, vs_ref, kw_ref, vw_ref, kc_ref, vc_ref,
                     bc_ref, bn_ref, o_ref, ksp, vsp, kwp, vwp, osel, *, T):
    qi = pl.program_id(2)
    tq = Q_BLOCK
    front = T - tq
    wlen = WINDOW + tq
    n_slc = T // SEL_BLOCK

    @pl.when(qi == 0)
    def _():
        ksp[0:front, :] = jnp.zeros((front, NSA_HD), BF16)
        vsp[0:front, :] = jnp.zeros((front, NSA_HD), BF16)
        ksp[front:front + T, :] = ks_ref[...].astype(BF16)
        vsp[front:front + T, :] = vs_ref[...].astype(BF16)
        kwp[0:WINDOW, :] = jnp.zeros((WINDOW, NSA_HD), BF16)
        vwp[0:WINDOW, :] = jnp.zeros((WINDOW, NSA_HD), BF16)
        kwp[WINDOW:WINDOW + T, :] = kw_ref[...].astype(BF16)
        vwp[WINDOW:WINDOW + T, :] = vw_ref[...].astype(BF16)

    t0 = pl.multiple_of(qi * tq, tq)
    tpos = _iota2((tq, 1), 0) + t0
    q_all = q_ref[...] * (NSA_HD ** -0.5)
    q = jnp.concatenate([q_all[:, g * NSA_HD:(g + 1) * NSA_HD] for g in range(NSA_G)], axis=0).astype(BF16)
    bias_near = bn_ref[...].reshape(NSA_G * tq, NEAR_COLS)

    def per_head(a):
        return jnp.concatenate([a] * NSA_G, axis=0)

    ncmp = T // CMP_STRIDE
    vis = tpos >= _iota2((1, ncmp), 1) * CMP_STRIDE + (CMP_BLOCK - 1)
    s = _dot_nt(q, kc_ref[...].astype(BF16)) + bc_ref[...].reshape(NSA_G * tq, ncmp)
    p, l = _softmax_rows(s + per_head(jnp.where(vis, 0.0, NEG_INF)))
    p = p / jnp.maximum(l, TINY)
    o_cmp = _dot(p.astype(BF16), vc_ref[...].astype(BF16))
    psum = p[0:tq]
    for g in range(1, NSA_G):
        psum = psum + p[g * tq:(g + 1) * tq]

    member_t = _member_by_rank(psum, _iota2((1, tq), 1) + t0, n_slc, min(N_SEL, n_slc)).astype(BF16)

    nb = member_t.shape[0]
    n_win = SEL_WINDOWS if T % (SEL_WINDOWS * tq) == 0 else 1
    for i in range(n_win):
        w_prev, w = T * i // n_win, T * (i + 1) // n_win

        @pl.when((qi >= w_prev // tq) & (qi < w // tq))
        def _(w=w):
            off = T - w
            col_blk = (jnp.right_shift(_iota2((nb, w), 1) + off, SEL_SHIFT)
                       + (qi * (tq // SEL_BLOCK) + (tq - T) // SEL_BLOCK))
            expand = (col_blk == _iota2((nb, w), 0)).astype(BF16)
            kpos = _iota2((1, w), 1) + (t0 + tq - w)
            allowed = (_dot_tn(member_t, expand) > 0.5) & (kpos <= tpos)
            mask_s = per_head(jnp.where(allowed, 0.0, NEG_INF))
            osel[...] = _banded_attention(q, ksp, vsp, t0 + off, w, mask_s, bias_near)

    dist = WINDOW + _iota2((tq, wlen), 0) - _iota2((tq, wlen), 1)
    in_win = (dist >= 0) & (dist < WINDOW) & (_iota2((1, wlen), 1) + (t0 - WINDOW) >= 0)
    o_win = _banded_attention(q, kwp, vwp, t0, wlen, per_head(jnp.where(in_win, 0.0, NEG_INF)), bias_near)
    gate = jax.nn.sigmoid(gb_ref[...] + bg_ref[...])
    zb = _silu(zb_ref[...])
    for g in range(NSA_G):
        head = pl.program_id(1) * NSA_G + g
        r = slice(g * tq, (g + 1) * tq)
        mix = (_lane_col(gate, head) * o_cmp[r] + _lane_col(gate, NSA_HEADS + head) * osel[r, :]
               + _lane_col(gate, 2 * NSA_HEADS + head) * o_win[r])
        sl = slice(g * NSA_HD, (g + 1) * NSA_HD)
        o_ref[:, sl] = (mix * zb[:, sl]).astype(o_ref.dtype)


def _nsa_prompt_call(ya, yb, kv16, kcmp, vcmp, bg_r, bias_c, bias_near, *, B, T):
    nq = T // Q_BLOCK
    gw = NSA_G * NSA_HD
    kv_spec = pl.BlockSpec((T, NSA_HD), lambda b, h, i: (b, h))
    cmp_spec = pl.BlockSpec((None, None, T // CMP_STRIDE, NSA_HD), lambda b, h, i: (b, h, 0, 0))
    return pl.pallas_call(
        functools.partial(_nsa_prompt_body, T=T),
        grid=(B, NSA_KVH, nq),
        in_specs=[pl.BlockSpec((Q_BLOCK, gw), lambda b, h, i: (b * nq + i, EVEN_A["qb"] // gw + h)),
                  pl.BlockSpec((Q_BLOCK, gw), lambda b, h, i: (b * nq + i, EVEN_B["zb"] // gw + h)),
                  pl.BlockSpec((Q_BLOCK, LANES), lambda b, h, i: (b * nq + i, EVEN_B["gb"] // LANES)),
                  pl.BlockSpec((1, LANES), lambda b, h, i: (0, 0)),
                  kv_spec, kv_spec, kv_spec, kv_spec, cmp_spec, cmp_spec,
                  pl.BlockSpec((None, NSA_G, Q_BLOCK, T // CMP_STRIDE), lambda b, h, i: (h, 0, i, 0)),
                  pl.BlockSpec((None, NSA_G, Q_BLOCK, NEAR_COLS), lambda b, h, i: (h, 0, 0, 0))],
        out_specs=pl.BlockSpec((Q_BLOCK, gw), lambda b, h, i: (b * nq + i, h)),
        out_shape=jax.ShapeDtypeStruct((B * T, NSA_W), BF16),
        scratch_shapes=[pltpu.VMEM((2 * T - Q_BLOCK, NSA_HD), BF16), pltpu.VMEM((2 * T - Q_BLOCK, NSA_HD), BF16),
                        pltpu.VMEM((WINDOW + T, NSA_HD), BF16), pltpu.VMEM((WINDOW + T, NSA_HD), BF16),
                        pltpu.VMEM((NSA_G * Q_BLOCK, NSA_HD), F32)],
        compiler_params=_params(("arbitrary", "arbitrary", "arbitrary")),
        name="nsa_prompt",
    )(ya, yb, yb, bg_r, *kv16, kcmp, vcmp, bias_c, bias_near)


CMP_PAGES = 16
CHUNKS_PER_PAGE = PAGE_SIZE // CMP_STRIDE
PAGE_ROWS = PAGE_SIZE * NSA_KVH


def _pool_rows(pool):
    return pool.reshape(pool.shape[0] * PAGE_ROWS, NSA_HD)


def _cmp_pages_body(pt_ref, *refs):
    del pt_ref
    pages = refs[:CMP_PAGES]
    w_ref, pe_ref, o_ref = refs[CMP_PAGES:]
    rows = CMP_PAGES * CHUNKS_PER_PAGE
    per_head = [jnp.concatenate(
        [jnp.concatenate([pg[pl.ds(NSA_KVH * s + h, CHUNKS_PER_PAGE, stride=CMP_STRIDE * NSA_KVH), :]
                          for s in range(CMP_STRIDE)], axis=1) for pg in pages], axis=0) for h in range(NSA_KVH)]
    w = w_ref[...]
    r = _dot(jnp.concatenate(per_head, axis=0).astype(BF16), w)
    pc = _dot(pe_ref[...], w)
    r = r + jnp.concatenate([pc[0:1, :NSA_HD], pc[1:2, NSA_HD:]], axis=1)
    for h in range(NSA_KVH):
        o_ref[h] = r[h * rows:(h + 1) * rows]


def _cmp_pages_call(pool, page_table, w1, pe, *, B):
    n_pages = page_table.shape[1]
    rows = CMP_PAGES * CHUNKS_PER_PAGE
    view = _pool_rows(pool)
    w = w1.reshape(2, CMP_STRIDE, NSA_HD, NSA_HD).transpose(1, 2, 0, 3).reshape(CMP_STRIDE * NSA_HD, 2 * NSA_HD)
    pe_rows = jnp.pad(pe.reshape(2, CMP_STRIDE * NSA_HD), ((0, 6), (0, 0))).astype(BF16)

    def page_spec(i):
        return pl.BlockSpec((PAGE_ROWS, NSA_HD), lambda b, s, pt: (pt[b * n_pages + s * CMP_PAGES + i], 0))

    grid_spec = pltpu.PrefetchScalarGridSpec(
        num_scalar_prefetch=1,
        grid=(B, n_pages // CMP_PAGES),
        in_specs=[page_spec(i) for i in range(CMP_PAGES)]
        + [pl.BlockSpec((CMP_STRIDE * NSA_HD, 2 * NSA_HD), lambda b, s, pt: (0, 0)),
           pl.BlockSpec((8, CMP_STRIDE * NSA_HD), lambda b, s, pt: (0, 0))],
        out_specs=pl.BlockSpec((None, NSA_KVH, rows, 2 * NSA_HD), lambda b, s, pt: (b, 0, s, 0)),
    )
    return pl.pallas_call(
        _cmp_pages_body,
        grid_spec=grid_spec,
        out_shape=jax.ShapeDtypeStruct((B, NSA_KVH, n_pages * CHUNKS_PER_PAGE, 2 * NSA_HD), F32),
        compiler_params=_params(("arbitrary", "arbitrary")),
        name="nsa_cmp_pages",
    )(page_table.reshape(-1), *([view] * CMP_PAGES), w.astype(BF16), pe_rows)


SEL_WINDOWS = 4
SLC_LANES = 384


def _sample_q_rows(q_ref):
    q = q_ref[...] * (NSA_HD ** -0.5)
    return jnp.concatenate([q[:, g * NSA_HD:(g + 1) * NSA_HD] for g in range(NSA_G)], axis=0).astype(BF16)


def _nsa_sample_main_body(abk_ref, abv_ref, b1_ref, w2_ref, q_ref, wk_ref, wv_ref, kn_ref, vn_ref, bc_ref, bw_ref,
                          ocmp_ref, owin_ref, idx_ref, *, T, n_slc):
    tp = SAMPLE_PAD_T
    rows = NSA_G * tp
    ncmp = abk_ref.shape[0]

    def compressed(ab_ref, t):
        ab = ab_ref[...]
        h = ab[:, :NSA_HD] + pltpu.roll(ab[:, NSA_HD:], ncmp - 1, 0) + b1_ref[t]
        return _dot(_gelu_tanh(h).astype(BF16), w2_ref[t]).astype(BF16)

    kc, vc = compressed(abk_ref, 0), compressed(abv_ref, 1)
    q = _sample_q_rows(q_ref)
    step = jnp.bitwise_and(_iota2((rows, 1), 0), tp - 1)
    tpos = PAST_LEN + step
    vis = tpos >= _iota2((1, ncmp), 1) * CMP_STRIDE + (CMP_BLOCK - 1)
    p, l = _softmax_rows(jnp.where(vis, _dot_nt(q, kc) + bc_ref[...], NEG_INF))
    p = p / jnp.maximum(l, TINY)
    ocmp_ref[...] = _dot(p.astype(BF16), vc)
    psum = p[0:tp]
    for g in range(1, NSA_G):
        psum = psum + p[g * tp:(g + 1) * tp]
    cur = jnp.right_shift(PAST_LEN + _iota2((tp, 1), 0), SEL_SHIFT)
    _, picks = _top_blocks(_slc_scores(psum, SLC_LANES, n_slc), cur, N_SEL)
    idx_ref[...] = picks.astype(jnp.int32)

    wb = wk_ref.shape[0] // NSA_KVH
    wlen = bw_ref.shape[1]
    fill = jnp.zeros((wlen - wb - tp, NSA_HD), BF16)
    head = pl.program_id(1)
    k_all = jnp.concatenate([wk_ref[pl.ds(head, wb, stride=NSA_KVH), :].astype(BF16), kn_ref[...], fill], axis=0)
    v_all = jnp.concatenate([wv_ref[pl.ds(head, wb, stride=NSA_KVH), :].astype(BF16), vn_ref[...], fill], axis=0)
    col = _iota2((1, wlen), 1)
    dist = tpos - (PAST_LEN - wb + col)
    in_win = (dist >= 0) & (dist < WINDOW) & (col < wb + T)
    pw, lw = _softmax_rows(jnp.where(in_win, _dot_nt(q, k_all) + bw_ref[...], NEG_INF))
    owin_ref[...] = _dot(pw.astype(BF16), v_all) / jnp.maximum(lw, TINY)


def _nsa_sample_main_call(ya, kw16, vw16, abk, abv, b1, w2, wk, wv, bias_c, bias_w, *, B, T):
    tp = SAMPLE_PAD_T
    rows = NSA_G * tp
    gw = NSA_G * NSA_HD
    ncmp = abk.shape[2]
    wb = wk.shape[1]
    wlen = bias_w.shape[-1]
    n_slc = -(-(PAST_LEN + T) // SEL_BLOCK)
    assert n_slc <= SLC_LANES and T <= tp
    ab_spec = pl.BlockSpec((None, None, ncmp, 2 * NSA_HD), lambda b, h: (b, h, 0, 0))
    win_spec = pl.BlockSpec((wb * NSA_KVH, NSA_HD), lambda b, h: (b, 0))
    o_spec = pl.BlockSpec((None, None, rows, NSA_HD), lambda b, h: (b, h, 0, 0))
    return pl.pallas_call(
        functools.partial(_nsa_sample_main_body, T=T, n_slc=n_slc),
        grid=(B, NSA_KVH),
        in_specs=[ab_spec, ab_spec,
                  pl.BlockSpec((2, 1, NSA_HD), lambda b, h: (0, 0, 0)),
                  pl.BlockSpec((2, NSA_HD, NSA_HD), lambda b, h: (0, 0, 0)),
                  pl.BlockSpec((tp, gw), lambda b, h: (b, EVEN_A["qb"] // gw + h)),
                  win_spec, win_spec,
                  pl.BlockSpec((tp, NSA_HD), lambda b, h: (b, h)),
                  pl.BlockSpec((tp, NSA_HD), lambda b, h: (b, h)),
                  pl.BlockSpec((None, rows, ncmp), lambda b, h: (h, 0, 0)),
                  pl.BlockSpec((None, rows, wlen), lambda b, h: (h, 0, 0))],
        out_specs=[o_spec, o_spec, pl.BlockSpec((None, None, tp, LANES), lambda b, h: (b, h, 0, 0))],
        out_shape=[jax.ShapeDtypeStruct((B, NSA_KVH, rows, NSA_HD), F32),
                   jax.ShapeDtypeStruct((B, NSA_KVH, rows, NSA_HD), F32),
                   jax.ShapeDtypeStruct((B, NSA_KVH, tp, LANES), jnp.int32)],
        compiler_params=_params(("parallel", "parallel")),
        name="nsa_sample_main",
    )(abk, abv, b1.reshape(2, 1, NSA_HD), w2.astype(BF16), ya,
      wk.reshape(B * wb * NSA_KVH, NSA_HD), wv.reshape(B * wb * NSA_KVH, NSA_HD), kw16, vw16, bias_c, bias_w)


NEAR_BLOCKS = 3


def _nsa_sample_sel_body(idx_ref, pt_ref, q_ref, kn_ref, vn_ref, tbl_ref, ocmp_ref, owin_ref, gb_ref, bg_ref, zb_ref,
                         *refs, T):
    del pt_ref
    k_blocks = refs[:N_SEL]
    v_blocks = refs[N_SEL:2 * N_SEL]
    o_ref, osel = refs[2 * N_SEL:]
    tp = SAMPLE_PAD_T
    rows = NSA_G * tp
    b, h, t = pl.program_id(0), pl.program_id(1), pl.program_id(2)
    base = ((b * NSA_KVH + h) * T + t) * N_SEL
    first_new = PAST_LEN // SEL_BLOCK
    cur = jnp.right_shift(PAST_LEN + t, SEL_SHIFT)
    q = _sample_q_rows(q_ref)
    pad = jnp.zeros((SEL_BLOCK - tp, NSA_HD), BF16)
    k_new = jnp.concatenate([kn_ref[...], pad], axis=0)
    v_new = jnp.concatenate([vn_ref[...], pad], axis=0)
    lane = _iota2((1, LANES), 1)
    low = lane < SEL_BLOCK
    within = jnp.bitwise_and(lane, SEL_BLOCK - 1)
    ks, vs, bias, kpos = [], [], [], []
    for i in range(0, N_SEL, 2):
        pair_bias, pair_pos = [], []
        for j in (i, i + 1):
            blk = idx_ref[base + j]
            is_new = blk >= first_new
            ks.append(jnp.where(is_new, k_new, k_blocks[j][pl.ds(h, SEL_BLOCK, stride=NSA_KVH), :].astype(BF16)))
            vs.append(jnp.where(is_new, v_new, v_blocks[j][pl.ds(h, SEL_BLOCK, stride=NSA_KVH), :].astype(BF16)))
            pair_bias.append(tbl_ref[jnp.clip(blk - (first_new - NEAR_BLOCKS), 0, NEAR_BLOCKS)])
            pair_pos.append(jnp.where(blk <= cur, blk * SEL_BLOCK, PAST_LEN + SEL_BLOCK * LANES) + within)
        bias.append(jnp.where(low, pair_bias[0], pair_bias[1]))
        kpos.append(jnp.where(low, pair_pos[0], pair_pos[1]))
    k_all = jnp.concatenate(ks, axis=0)
    v_all = jnp.concatenate(vs, axis=0)
    step = jnp.bitwise_and(_iota2((rows, 1), 0), tp - 1)
    ok = jnp.concatenate(kpos, axis=1) <= PAST_LEN + step
    p, l = _softmax_rows(jnp.where(ok, _dot_nt(q, k_all) + jnp.concatenate(bias, axis=1), NEG_INF))
    o = _dot(p.astype(BF16), v_all) / jnp.maximum(l, TINY)

    @pl.when(t == 0)
    def _():
        osel[...] = jnp.zeros_like(osel)

    osel[...] = jnp.where(step == t, o, osel[...])

    @pl.when(t == T - 1)
    def _():
        gate = jax.nn.sigmoid(gb_ref[...] + bg_ref[...])
        zb = _silu(zb_ref[...])
        for g in range(NSA_G):
            r = slice(g * tp, (g + 1) * tp)
            head = h * NSA_G + g
            mix = (_lane_col(gate, head) * ocmp_ref[r, :] + _lane_col(gate, NSA_HEADS + head) * osel[r, :]
                   + _lane_col(gate, 2 * NSA_HEADS + head) * owin_ref[r, :])
            sl = slice(g * NSA_HD, (g + 1) * NSA_HD)
            o_ref[:, sl] = (mix * zb[:, sl]).astype(o_ref.dtype)


def _nsa_sample_sel_call(ya, yb, ks16, vs16, idx, page_table, pool_k, pool_v, tbl, o_cmp, o_win, bg_r, *, B, T):
    tp = SAMPLE_PAD_T
    rows = NSA_G * tp
    gw = NSA_G * NSA_HD
    n_pages = page_table.shape[1]
    halves = PAGE_SIZE // SEL_BLOCK
    idx_flat = idx[:, :, :T, :N_SEL].reshape(-1)
    view_k, view_v = _pool_rows(pool_k), _pool_rows(pool_v)

    def blk_spec(j):
        def index(b, h, t, idx_s, pt_s):
            blk = idx_s[((b * NSA_KVH + h) * T + t) * N_SEL + j]
            page = pt_s[b * n_pages + jnp.minimum(blk // halves, n_pages - 1)]
            return (page * halves + blk % halves, 0)
        return pl.BlockSpec((SEL_BLOCK * NSA_KVH, NSA_HD), index)

    o_spec = pl.BlockSpec((None, None, rows, NSA_HD), lambda b, h, t, *_: (b, h, 0, 0))
    grid_spec = pltpu.PrefetchScalarGridSpec(
        num_scalar_prefetch=2,
        grid=(B, NSA_KVH, T),
        in_specs=[pl.BlockSpec((tp, gw), lambda b, h, t, *_: (b, EVEN_A["qb"] // gw + h)),
                  pl.BlockSpec((tp, NSA_HD), lambda b, h, t, *_: (b, h)),
                  pl.BlockSpec((tp, NSA_HD), lambda b, h, t, *_: (b, h)),
                  pl.BlockSpec((None, NEAR_BLOCKS + 1, rows, LANES), lambda b, h, t, *_: (h, 0, 0, 0)),
                  o_spec, o_spec,
                  pl.BlockSpec((tp, LANES), lambda b, h, t, *_: (b, EVEN_B["gb"] // LANES)),
                  pl.BlockSpec((1, LANES), lambda b, h, t, *_: (0, 0)),
                  pl.BlockSpec((tp, gw), lambda b, h, t, *_: (b, EVEN_B["zb"] // gw + h))]
        + [blk_spec(j) for j in range(N_SEL)] * 2,
        out_specs=pl.BlockSpec((tp, gw), lambda b, h, t, *_: (b, h)),
        scratch_shapes=[pltpu.VMEM((rows, NSA_HD), F32)],
    )
    return pl.pallas_call(
        functools.partial(_nsa_sample_sel_body, T=T),
        grid_spec=grid_spec,
        out_shape=jax.ShapeDtypeStruct((B * tp, NSA_W), BF16),
        compiler_params=_params(("arbitrary", "arbitrary", "arbitrary")),
        name="nsa_sample_sel",
    )(idx_flat, page_table.reshape(-1), ya, ks16, vs16, tbl, o_cmp, o_win, yb, bg_r, yb,
      *([view_k] * N_SEL), *([view_v] * N_SEL))


def _sample_bias_tables(rel_bias, T, wb):
    tp = SAMPLE_PAD_T
    ncmp = PAST_LEN // CMP_STRIDE
    wlen = -(-(wb + tp) // LANES) * LANES
    first = PAST_LEN // SEL_BLOCK - NEAR_BLOCKS
    assert PAST_LEN - ((first + 1) * SEL_BLOCK - 1) >= REL_MAX_DIST
    lo, hi = -wlen, PAST_LEN + tp
    rev = _bias_line(rel_bias, lo, hi, descending=True)

    def rows(tbl):
        return tbl.reshape(NSA_KVH, NSA_G * tp, tbl.shape[-1])

    t_c = _toeplitz(rev, hi - 1 - (PAST_LEN - (CMP_BLOCK - 1)), tp, CMP_STRIDE * ncmp)[:, :, ::CMP_STRIDE]
    t_w = _toeplitz(rev, hi - 1 - wb, tp, wlen)
    far = jnp.broadcast_to(rev[:, hi - 1 - REL_MAX_DIST][:, None, None], (NSA_HEADS, tp, LANES))
    near = []
    for k in range(1, NEAR_BLOCKS + 1):
        half = _toeplitz(rev, hi - 1 - (PAST_LEN - (first + k) * SEL_BLOCK), tp, SEL_BLOCK)
        near.append(jnp.concatenate([half, half], axis=-1))
    t_s = jnp.stack([far] + near, axis=1).reshape(NSA_KVH, NSA_G, NEAR_BLOCKS + 1, tp, LANES)
    t_s = t_s.transpose(0, 2, 1, 3, 4).reshape(NSA_KVH, NEAR_BLOCKS + 1, NSA_G * tp, LANES)
    return rows(t_c), rows(t_w), t_s


def _tail_even(w):
    return _tail_relayout(w, EVEN_KV_OFF + 6 * NSA_KV_W, 3 * NSA_HEADS, NSA_W + MEM_W, EVEN_B_N)


def _tail_odd(w):
    return _tail_relayout(w, ODD_A_N, 2 * ML_HEADS, ML_V_W + MEM_W, ODD_B_N)


def _gate_bias_even(b_gate):
    return jnp.pad(b_gate, (0, LANES - 3 * NSA_HEADS)).reshape(1, LANES)


def _gate_bias_odd(b_if):
    return jnp.pad(b_if.reshape(2 * ML_HEADS), (0, LANES - 2 * ML_HEADS)).reshape(1, LANES)


def _rel_bucket(dist):
    n = np.maximum(dist, 0)
    exact = REL_BUCKETS // 2
    nf = np.maximum(n, 1).astype(np.float32)
    large = exact + (np.log(nf / exact) / math.log(REL_MAX_DIST / exact) * (REL_BUCKETS - exact)).astype(np.int32)
    return np.where(n < exact, n, np.minimum(large, REL_BUCKETS - 1))


def _bias_line(rel_bias, lo, hi, descending=False):
    dist = np.arange(hi - 1, lo - 1, -1) if descending else np.arange(lo, hi)
    buckets = _rel_bucket(dist)
    edges = np.flatnonzero(np.diff(buckets)) + 1
    starts = np.concatenate([[0], edges])
    ends = np.concatenate([edges, [hi - lo]])
    bias_t = rel_bias.T.astype(F32)
    runs = [jnp.broadcast_to(bias_t[:, int(buckets[s])][:, None], (NSA_HEADS, int(e - s))) for s, e in zip(starts, ends)]
    return jnp.concatenate(runs, axis=1)


def _skew_rows(v, rows, step, cols):
    n = v.shape[1]
    reps = -(-rows * (n + step) // n)
    return jnp.tile(v, (1, reps))[:, :rows * (n + step)].reshape(v.shape[0], rows, n + step)[:, :, :cols]


def _toeplitz(rev, start, rows, cols):
    seg = rev[:, start - (rows - 1):start + cols]
    return _skew_rows(jnp.roll(seg, -(rows - 1), axis=1), rows, -1, cols)


def _prompt_bias_tables(rel_bias, T):
    ncmp = T // CMP_STRIDE
    assert Q_BLOCK + 1 >= REL_MAX_DIST
    lo, hi = -(CMP_STRIDE * ncmp + CMP_BLOCK), T
    line = _bias_line(rel_bias, lo, hi)
    rev = _bias_line(rel_bias, lo, hi, descending=True)

    def split(tbl):
        return tbl.reshape((NSA_KVH, NSA_G) + tbl.shape[1:])

    back = CMP_STRIDE * (ncmp - 1)
    first = -(back + CMP_BLOCK - 1) - lo
    seg = line[:, first:first + T + back]
    t_c = _skew_rows(jnp.roll(seg, -back, axis=1), ncmp, -CMP_STRIDE, T).swapaxes(1, 2)
    far = rev[:, hi - 1 - REL_MAX_DIST]
    t_near = _toeplitz(rev, hi - 1 - REL_MAX_DIST, Q_BLOCK, NEAR_COLS) - far[:, None, None]
    return split(t_c), split(t_near)


def _nsa_sample(ya, yb, kv16, page_table, pk_cmp, pv_cmp, pk_sel, pv_sel, wk, wv, bg_r, w1, b1, w2, pe, rel_bias,
                *, B, T):
    assert (PAST_LEN + T) // CMP_STRIDE == PAST_LEN // CMP_STRIDE
    abk = _cmp_pages_call(pk_cmp, page_table, w1[0], pe[0], B=B)
    abv = _cmp_pages_call(pv_cmp, page_table, w1[1], pe[1], B=B)
    bias_c, bias_w, tbl = _sample_bias_tables(rel_bias, T, wk.shape[1])
    o_cmp, o_win, idx = _nsa_sample_main_call(ya, kv16[4], kv16[5], abk, abv, b1, w2, wk, wv, bias_c, bias_w, B=B, T=T)
    return _nsa_sample_sel_call(ya, yb, kv16[2], kv16[3], idx, page_table, pk_sel, pv_sel, tbl, o_cmp, o_win, bg_r,
                                B=B, T=T)


def _kv_project(x, wt):
    outs = [_matmul_heads(x, wt, first=EVEN_KV_OFF + j * NSA_KV_W, transposed=True) for j in range(6)]
    return [o[0] for o in outs], [o[1] for o in outs]


def _even_prompt(hp2d, npre, mk16, mv16, wt, wt_b, bg_r, w1, b1, w2, pe, lb, g_norm, w_out, rel_bias, *, B, T):
    ya, yb = _matmul_nt(npre, wt, tm=W_TILE_M, tn=W_TILE_N, rows=(0, EVEN_A_N)), _matmul_nt(npre, wt_b)
    kv32, kv16 = _kv_project(npre, wt)
    oa, s_new = _hgrn_call(ya, jnp.zeros((B, HG_HEADS, HG_DK, HG_DV), F32), lb, g_norm, B=B, T=T, L=CHUNK, valid=CHUNK)
    kcmp = _compress_call(kv16[0], w1[0], b1[0], w2[0], pe[0], B=B, T=T)
    vcmp = _compress_call(kv16[1], w1[1], b1[1], w2[1], pe[1], B=B, T=T)
    ob = _nsa_prompt_call(ya, yb, kv16[2:], kcmp, vcmp, bg_r, *_prompt_bias_tables(rel_bias, T), B=B, T=T)
    om = _mem_call(yb, EVEN_B["qm"], mk16, mv16, B=B, T=T)
    h_new = _outproj([oa, ob, om], w_out, hp2d)
    wb = min(WINDOW, T)
    rows = [r.reshape(B, T, NSA_KVH, NSA_HD) for r in kv32]
    return h_new, (rows[0], rows[1], rows[2], rows[3], rows[4][:, -wb:], rows[5][:, -wb:], s_new)


def _even_sample(hs2d, nsam, mk_s, mv_s, page_table, pk_cmp, pv_cmp, pk_sel, pv_sel, wk, wv, s0,
                 wt, wt_b, bg_r, w1, b1, w2, pe, lb, g_norm, w_out, rel_bias, *, B, T):
    tp = SAMPLE_PAD_T
    ya, yb = _matmul_nt(nsam, wt, tm=W_TILE_M, tn=W_TILE_N, rows=(0, EVEN_A_N)), _matmul_nt(nsam, wt_b)
    kv32, kv16 = _kv_project(nsam, wt)
    oa, s_new = _hgrn_call(ya, s0, lb, g_norm, B=B, T=tp, L=tp, valid=T)
    ob = _nsa_sample(ya, yb, kv16, page_table, pk_cmp, pv_cmp, pk_sel, pv_sel, wk, wv, bg_r, w1, b1, w2, pe, rel_bias,
                     B=B, T=T)
    om = _mem_call(yb, EVEN_B["qm"], mk_s.reshape(B * N_MEM, MEM_W), mv_s.reshape(B * N_MEM, MEM_W), B=B, T=tp)
    rows = [r.reshape(B, tp, NSA_KVH, NSA_HD)[:, :T] for r in kv32]
    wb = wk.shape[1]
    win_k = jnp.concatenate([wk, rows[4]], axis=1)[:, -wb:]
    win_v = jnp.concatenate([wv, rows[5]], axis=1)[:, -wb:]
    return _outproj([oa, ob, om], w_out, hs2d), (rows[0], rows[1], rows[2], rows[3], win_k, win_v, s_new)


def _odd_mix(h2d, hn, k2d, v2d, c0, n0, m0, wt, wt_b, bif_r, g_norm, w_out, *, B, T, L, valid):
    ya, yb = _matmul_nt(hn, wt, tm=W_TILE_M, tn=W_TILE_N, rows=(0, ODD_A_N)), _matmul_nt(hn, wt_b)
    h, c_new, n_new, m_new = _mlstm_call(ya, yb, c0, n0, m0, bif_r, g_norm, B=B, T=T, L=L, valid=valid)
    om = _mem_call(yb, ODD_B["qm"], k2d, v2d, B=B, T=T)
    return _outproj([h, om], w_out, h2d), (c_new, n_new, m_new)


def _stack(lst, i):
    return jnp.stack([t[i] for t in lst])


def kernel(x_prompt, x_sample, cache_mem_k, cache_mem_v, cache_cmp_k, cache_cmp_v, cache_sel_k, cache_sel_v,
           cache_win_k, cache_win_v, state_hgrn, state_mlstm_c, state_mlstm_n, state_mlstm_m, page_table,
           mem_prompt, norm_w, mem_norm_w, final_norm_w, rel_bias, w_mem_kv, w_in_even, b_nsa_gate,
           w_cmp1, b_cmp1, w_cmp2, pe_cmp, hgrn_lb_logits, hgrn_norm_w, w_out_even, w_in_odd, b_mlstm_if,
           mlstm_norm_w, w_out_odd):
    bp, tp = x_prompt.shape[:2]
    bs, ts = x_sample.shape[:2]
    tsp = SAMPLE_PAD_T
    lbs = jnp.cumsum(jax.nn.softmax(hgrn_lb_logits.astype(F32), axis=0), axis=0)
    hp = x_prompt.reshape(bp * tp, D_MODEL)
    hs = jnp.pad(x_sample, ((0, 0), (0, tsp - ts), (0, 0))).reshape(bs * tsp, D_MODEL)
    mem2d = mem_prompt.reshape(bp * N_MEM, D_MODEL)
    mem_new, even_p, even_s, odd_p, odd_s = [], [], [], [], []
    for l in range(DEPTH):
        npre = _rmsnorm_rows(hp, norm_w[l], BF16)
        nsam = _rmsnorm_rows(hs, norm_w[l], BF16)
        nmem = _rmsnorm_rows(mem2d, mem_norm_w[l], BF16)
        mk32, mk16 = _matmul_heads(nmem, w_mem_kv[l], first=0)
        mv32, mv16 = _matmul_heads(nmem, w_mem_kv[l], first=MEM_W)
        mem_new.append((mk32.reshape(bp, N_MEM, MEM_HEADS, MEM_HD), mv32.reshape(bp, N_MEM, MEM_HEADS, MEM_HD)))
        mk_s, mv_s = cache_mem_k[l], cache_mem_v[l]
        if l % 2 == 0:
            e = l // 2
            w_in = w_in_even[e].T
            w_b = _tail_even(w_in)
            w_out = w_out_even[e].astype(BF16)
            bg_r = _gate_bias_even(b_nsa_gate[e])
            cmpw = (w_cmp1[e].reshape(2, CMP_BLOCK, NSA_HD, NSA_HD), b_cmp1[e], w_cmp2[e], pe_cmp[e])
            hp, st_p = _even_prompt(hp, npre, mk16, mv16, w_in, w_b, bg_r, *cmpw, lbs[l], hgrn_norm_w[e], w_out,
                                    rel_bias, B=bp, T=tp)
            hs, st_s = _even_sample(hs, nsam, mk_s, mv_s, page_table, cache_cmp_k[e], cache_cmp_v[e], cache_sel_k[e],
                                    cache_sel_v[e], cache_win_k[e], cache_win_v[e], state_hgrn[e], w_in, w_b, bg_r,
                                    *cmpw, lbs[l], hgrn_norm_w[e], w_out, rel_bias, B=bs, T=ts)
            even_p.append(st_p)
            even_s.append(st_s)
        else:
            o = l // 2
            w_in = w_in_odd[o].T
            w_b = _tail_odd(w_in)
            w_out = w_out_odd[o].astype(BF16)
            bif_r = _gate_bias_odd(b_mlstm_if[o])
            hp, st_p = _odd_mix(hp, npre, mk16, mv16, jnp.zeros((bp, ML_HEADS, ML_DV, ML_DK), F32),
                                jnp.zeros((bp, ML_HEADS, ML_DK), F32), jnp.zeros((bp, ML_HEADS), F32),
                                w_in, w_b, bif_r, mlstm_norm_w[o], w_out, B=bp, T=tp, L=ML_CHUNK, valid=ML_CHUNK)
            hs, st_s = _odd_mix(hs, nsam, mk_s.reshape(bs * N_MEM, MEM_W), mv_s.reshape(bs * N_MEM, MEM_W),
                                state_mlstm_c[o], state_mlstm_n[o], state_mlstm_m[o],
                                w_in, w_b, bif_r, mlstm_norm_w[o], w_out, B=bs, T=tsp, L=tsp, valid=ts)
            odd_p.append(st_p)
            odd_s.append(st_s)
    y_prompt = _rmsnorm_rows(hp, final_norm_w, F32).reshape(bp, tp, D_MODEL)
    y_sample = _rmsnorm_rows(hs, final_norm_w, F32).reshape(bs, tsp, D_MODEL)[:, :ts]
    return (y_prompt, y_sample,
            _stack(mem_new, 0), _stack(mem_new, 1),
            _stack(even_p, 0), _stack(even_p, 1), _stack(even_p, 2), _stack(even_p, 3),
            _stack(even_p, 4), _stack(even_p, 5), _stack(even_p, 6),
            _stack(odd_p, 0), _stack(odd_p, 1), _stack(odd_p, 2),
            _stack(even_s, 0), _stack(even_s, 1), _stack(even_s, 2), _stack(even_s, 3),
            _stack(even_s, 4), _stack(even_s, 5), _stack(even_s, 6),
            _stack(odd_s, 0), _stack(odd_s, 1), _stack(odd_s, 2))
```

```python
import functools
import math

import jax
import jax.numpy as jnp
import numpy as np
from jax import lax
from jax.experimental import pallas as pl
from jax.experimental.pallas import tpu as pltpu

D_MODEL = 4096
DEPTH = 2
PAST_LEN = 16384
PAGE_SIZE = 128
N_MEM = 256
EPS = 1e-6
CHUNK = 64

HG_DK = 128
HG_DV = 128
HG_HEADS = D_MODEL // 2 // HG_DV
HG_W = HG_HEADS * HG_DV

NSA_HD = 128
NSA_HEADS = D_MODEL // 2 // NSA_HD
NSA_KVH = 4
NSA_G = NSA_HEADS // NSA_KVH
NSA_W = NSA_HEADS * NSA_HD
NSA_KV_W = NSA_KVH * NSA_HD
CMP_BLOCK = 32
CMP_STRIDE = 16
SEL_BLOCK = 64
SEL_SHIFT = SEL_BLOCK.bit_length() - 1
N_SEL = 16
WINDOW = 512
Q_BLOCK = 256

ML_HEADS = D_MODEL // 512
ML_DK = D_MODEL // 2 // ML_HEADS
ML_DV = D_MODEL // ML_HEADS
ML_QK_W = ML_HEADS * ML_DK
ML_V_W = ML_HEADS * ML_DV

MEM_HEADS = 4
MEM_HD = 128
MEM_W = MEM_HEADS * MEM_HD

REL_BUCKETS = 32
REL_MAX_DIST = 128

F32 = jnp.float32
BF16 = jnp.bfloat16
LANES = 128
NEG_INF = float("-inf")
TINY = float(np.finfo(np.float32).tiny)
EXP_CLAMP = 80.0
VMEM_LIMIT = 56 * 1024 * 1024

HG_HB = 16
ML_HB = 2
ML_CHUNK = 256
W_TILE_M, W_TILE_N = 1024, 512
HG_SUB = 16
SAMPLE_PAD_T = 16

MM_TILE_N = 1024
EVEN_A = {"qa": 0, "fa": HG_W, "ia": 2 * HG_W, "za": 3 * HG_W, "qb": 4 * HG_W}
EVEN_A_N = 4 * HG_W + NSA_W
EVEN_B = {"zb": 0, "qm": NSA_W, "gb": NSA_W + MEM_W}
EVEN_B_N = -(-(NSA_W + MEM_W + LANES) // MM_TILE_N) * MM_TILE_N
EVEN_KV_OFF = EVEN_A_N
ODD_A = {"q": 0, "k": ML_QK_W, "v": 2 * ML_QK_W, "og": 2 * ML_QK_W + ML_V_W}
ODD_A_N = 2 * ML_QK_W + 2 * ML_V_W
ODD_B = {"z": 0, "qm": ML_V_W, "gates": ML_V_W + MEM_W}
ODD_B_N = -(-(ML_V_W + MEM_W + LANES) // MM_TILE_N) * MM_TILE_N


def _dot(a, b):
    return jnp.dot(a, b, preferred_element_type=F32)


def _dot_nt(a, b):
    return lax.dot_general(a, b, (((1,), (1,)), ((), ())), preferred_element_type=F32)


def _dot_tn(a, b):
    return lax.dot_general(a, b, (((0,), (0,)), ((), ())), preferred_element_type=F32)


def _iota2(shape, dim):
    return lax.broadcasted_iota(jnp.int32, shape, dim)


def _cumsum_rows(x, tri_b):
    hi = x.astype(BF16)
    r1 = x - hi.astype(F32)
    mid = r1.astype(BF16)
    lo = (r1 - mid.astype(F32)).astype(BF16)
    return _dot(tri_b, hi) + _dot(tri_b, mid) + _dot(tri_b, lo)


def _row_to_col(row, n):
    eye = _iota2((n, n), 0) == _iota2((n, n), 1)
    return jnp.sum(jnp.where(eye, row, 0.0), axis=1, keepdims=True)


def _col_to_row(col, n):
    eye = _iota2((n, n), 0) == _iota2((n, n), 1)
    return jnp.sum(jnp.where(eye, col, 0.0), axis=0, keepdims=True)


def _lane_col(x, idx):
    return jnp.sum(jnp.where(_iota2(x.shape, 1) == idx, x, 0.0), axis=1, keepdims=True)


def _silu(x):
    return x * jax.nn.sigmoid(x)


def _params(sem):
    return pltpu.CompilerParams(dimension_semantics=sem, vmem_limit_bytes=VMEM_LIMIT)


def _rmsnorm_body(x_ref, w_ref, o_ref):
    x = x_ref[...].astype(F32)
    y = x * lax.rsqrt(jnp.mean(x * x, axis=-1, keepdims=True) + EPS)
    o_ref[...] = (y * w_ref[...].astype(F32)).astype(o_ref.dtype)


def _rmsnorm_rows(x2d, w, out_dtype, tm=256):
    m, d = x2d.shape
    tm = min(tm, m)
    return pl.pallas_call(
        _rmsnorm_body,
        grid=(m // tm,),
        in_specs=[pl.BlockSpec((tm, d), lambda i: (i, 0)), pl.BlockSpec((1, d), lambda i: (0, 0))],
        out_specs=pl.BlockSpec((tm, d), lambda i: (i, 0)),
        out_shape=jax.ShapeDtypeStruct((m, d), out_dtype),
        compiler_params=_params(("parallel",)),
        name="rmsnorm",
    )(x2d, w.reshape(1, d))


def _matmul_nt_body(a_ref, bt_ref, o_ref):
    o_ref[...] = _dot_nt(a_ref[...], bt_ref[...].astype(BF16))


def _matmul_nt(a, bt, tm=1024, tn=MM_TILE_N, rows=None):
    m, k = a.shape
    first, n = rows or (0, bt.shape[0])
    tm, tn = min(tm, m), min(tn, n)
    assert m % tm == 0 and n % tn == 0 and first % tn == 0, (a.shape, bt.shape, rows)
    j0 = first // tn
    return pl.pallas_call(
        _matmul_nt_body,
        grid=(m // tm, n // tn),
        in_specs=[pl.BlockSpec((tm, k), lambda i, j: (i, 0)), pl.BlockSpec((tn, k), lambda i, j: (j0 + j, 0))],
        out_specs=pl.BlockSpec((tm, tn), lambda i, j: (i, j)),
        out_shape=jax.ShapeDtypeStruct((m, n), F32),
        compiler_params=_params(("parallel", "parallel")),
        name="matmul",
    )(a, bt)


def _tail_body(lo_ref, hi_ref, gate_ref, o_ref, *, shift, n_main):
    i = pl.program_id(0)
    main = jnp.concatenate([lo_ref[shift:, :], hi_ref[:shift, :]], axis=0)
    gates = jnp.where(_iota2((LANES, 1), 0) < shift, gate_ref[...], 0.0)
    o_ref[...] = jnp.where(i < n_main, main, jnp.where(i == n_main, gates, 0.0)).astype(o_ref.dtype)


def _tail_relayout(wt, first, shift, main, out_rows):
    n, k = wt.shape
    assert first % LANES == 0 and main % LANES == 0 and out_rows % LANES == 0 and shift % 8 == 0 and shift < LANES
    assert first + shift + main == n
    c0, n_main = first // LANES, main // LANES
    return pl.pallas_call(
        functools.partial(_tail_body, shift=shift, n_main=n_main),
        grid=(out_rows // LANES,),
        in_specs=[pl.BlockSpec((LANES, k), lambda i: (c0 + jnp.minimum(i, n_main - 1), 0)),
                  pl.BlockSpec((LANES, k), lambda i: (c0 + jnp.minimum(i, n_main - 1) + 1, 0)),
                  pl.BlockSpec((LANES, k), lambda i: (c0, 0))],
        out_specs=pl.BlockSpec((LANES, k), lambda i: (i, 0)),
        out_shape=jax.ShapeDtypeStruct((out_rows, k), BF16),
        compiler_params=_params(("parallel",)),
        name="tail_relayout",
    )(wt, wt, wt)


def _matmul_heads_body(a_ref, b_ref, *outs, transposed, count):
    b = b_ref[...].astype(BF16)
    acc = _dot_nt(a_ref[...], b) if transposed else _dot(a_ref[...], b)
    for c in range(count):
        @pl.when(pl.program_id(1) == c)
        def _(c=c):
            for h in range(MEM_HEADS):
                outs[c][:, h, :] = acc[:, h * LANES:(h + 1) * LANES]
            outs[count + c][...] = acc.astype(BF16)


def _matmul_heads(a, b, first, count, transposed=False, tm=512):
    m, k = a.shape
    n = MEM_HEADS * LANES
    tm = min(tm, m)
    assert m % tm == 0 and first % n == 0, (a.shape, b.shape, first)
    j0 = first // n
    b_spec = (pl.BlockSpec((n, k), lambda i, j: (j0 + j, 0)) if transposed
              else pl.BlockSpec((k, n), lambda i, j: (0, j0 + j)))
    outs = pl.pallas_call(
        functools.partial(_matmul_heads_body, transposed=transposed, count=count),
        grid=(m // tm, count),
        in_specs=[pl.BlockSpec((tm, k), lambda i, j: (i, 0)), b_spec],
        out_specs=[pl.BlockSpec((tm, MEM_HEADS, LANES), lambda i, j: (i, 0, 0))] * count
        + [pl.BlockSpec((tm, n), lambda i, j: (i, 0))] * count,
        out_shape=[jax.ShapeDtypeStruct((m, MEM_HEADS, LANES), F32)] * count
        + [jax.ShapeDtypeStruct((m, n), BF16)] * count,
        compiler_params=_params(("parallel", "arbitrary")),
        name="matmul_heads",
    )(a, b)
    return outs[:count], outs[count:]


def _outproj_body(*refs, widths):
    xs = refs[:len(widths)]
    w_ref, r_ref, o_ref = refs[len(widths):]
    acc = r_ref[...]
    off = 0
    for x_ref, w in zip(xs, widths):
        acc = acc + _dot(x_ref[...], w_ref[off:off + w, :])
        off += w
    o_ref[...] = acc


def _outproj(xs, w_bf16, resid, tm=1024, tn=512):
    m = resid.shape[0]
    n = w_bf16.shape[1]
    widths = tuple(x.shape[1] for x in xs)
    assert sum(widths) == w_bf16.shape[0]
    tm = min(tm, m)
    in_specs = [pl.BlockSpec((tm, w), lambda i, j: (i, 0)) for w in widths]
    in_specs += [pl.BlockSpec((w_bf16.shape[0], tn), lambda i, j: (0, j)), pl.BlockSpec((tm, tn), lambda i, j: (i, j))]
    return pl.pallas_call(
        functools.partial(_outproj_body, widths=widths),
        grid=(m // tm, n // tn),
        in_specs=in_specs,
        out_specs=pl.BlockSpec((tm, tn), lambda i, j: (i, j)),
        out_shape=jax.ShapeDtypeStruct((m, n), F32),
        compiler_params=_params(("parallel", "parallel")),
        name="outproj",
    )(*xs, w_bf16, resid)


def _hgrn_body(qa_ref, fa_ref, ia_ref, za_ref, lb_ref, gn_ref, s0_ref, o_ref, s_out, s_scr, *, L, valid):
    c = pl.program_id(2)

    @pl.when(c == 0)
    def _():
        s_scr[...] = s0_ref[...]

    lb = lb_ref[...]
    sig = jax.nn.sigmoid(fa_ref[...])
    logf = jnp.log(lb + (1.0 - lb) * sig)
    kk = (1.0 - lb) * (1.0 - sig)
    if valid < L:
        live = _iota2((L, 1), 0) < valid
        logf = jnp.where(live, logf, 0.0)
        kk = jnp.where(live, kk, 0.0)
    tri_b = (_iota2((L, L), 0) >= _iota2((L, L), 1)).astype(BF16)
    bc = _cumsum_rows(logf, tri_b)
    q = _silu(qa_ref[...])
    gate = _silu(za_ref[...])
    v = ia_ref[...]
    gn = gn_ref[...]
    nsub = L // HG_SUB
    rr = _iota2((L, nsub * L), 0)
    cc = _iota2((L, nsub * L), 1)
    keep = ((jnp.right_shift(cc, L.bit_length() - 1) == jnp.right_shift(rr, HG_SUB.bit_length() - 1))
            & (jnp.bitwise_and(cc, L - 1) <= rr))
    for j in range(HG_HB):
        sl = slice(j * HG_DK, (j + 1) * HG_DK)
        bj, qj, kj = bc[:, sl], q[:, sl], kk[:, sl]
        vb = v[:, sl].astype(BF16)
        s_prev = s_scr[j]
        inter = _dot((qj * jnp.exp(bj)).astype(BF16), s_prev.astype(BF16))
        mids = [bj[i * HG_SUB + HG_SUB // 2:i * HG_SUB + HG_SUB // 2 + 1, :] for i in range(nsub)]
        mid_rows = jnp.concatenate([jnp.broadcast_to(m, (HG_SUB, HG_DK)) for m in mids], axis=0)
        q_dec = qj * jnp.exp(jnp.minimum(bj - mid_rows, EXP_CLAMP))
        k_dec = jnp.concatenate([kj * jnp.exp(jnp.minimum(m - bj, EXP_CLAMP)) for m in mids], axis=0)
        att = jnp.where(keep, _dot_nt(q_dec.astype(BF16), k_dec.astype(BF16)), 0.0)
        o = inter + _dot(att.astype(BF16), jnp.concatenate([vb] * nsub, axis=0))
        o_n = o * lax.rsqrt(jnp.mean(o * o, axis=-1, keepdims=True) + EPS) * gn
        o_ref[:, sl] = (o_n * gate[:, sl]).astype(o_ref.dtype)
        bl = bj[L - 1:L, :]
        kd = kj * jnp.exp(bl - bj)
        s_scr[j] = _row_to_col(jnp.exp(bl), HG_DK) * s_prev + _dot_tn(kd.astype(BF16), vb)

    @pl.when(c == pl.num_programs(2) - 1)
    def _():
        s_out[...] = s_scr[...]


def _hgrn_call(y, s0, lb, gn, *, B, T, L, valid):
    nc = T // L
    w = HG_HB * HG_DK

    def col(name):
        blk = EVEN_A[name] // w
        return pl.BlockSpec((L, w), lambda b, hg, c: (b * nc + c, blk + hg))

    state_spec = pl.BlockSpec((None, HG_HB, HG_DK, HG_DV), lambda b, hg, c: (b, hg, 0, 0))
    return pl.pallas_call(
        functools.partial(_hgrn_body, L=L, valid=valid),
        grid=(B, HG_HEADS // HG_HB, nc),
        in_specs=[col("qa"), col("fa"), col("ia"), col("za"),
                  pl.BlockSpec((1, w), lambda b, hg, c: (0, hg)),
                  pl.BlockSpec((1, HG_DV), lambda b, hg, c: (0, 0)),
                  state_spec],
        out_specs=[pl.BlockSpec((L, w), lambda b, hg, c: (b * nc + c, hg)), state_spec],
        out_shape=[jax.ShapeDtypeStruct((B * T, HG_W), BF16),
                   jax.ShapeDtypeStruct((B, HG_HEADS, HG_DK, HG_DV), F32)],
        scratch_shapes=[pltpu.VMEM((HG_HB, HG_DK, HG_DV), F32)],
        compiler_params=_params(("arbitrary", "arbitrary", "arbitrary")),
        name="hgrn2",
    )(y, y, y, y, lb.reshape(1, HG_W), gn.reshape(1, HG_DV), s0)


def _mlstm_body(q_ref, k_ref, v_ref, og_ref, z_ref, g_ref, bif_ref, gn_ref, c0_ref, n0_ref, m0_ref,
                h_ref, c_out, n_out, m_out, c_scr, n_scr, m_scr, *, L, valid):
    c = pl.program_id(2)

    @pl.when(c == 0)
    def _():
        c_scr[...] = c0_ref[...]
        n_scr[...] = n0_ref[...]
        m_scr[...] = m0_ref[...]

    gates = g_ref[...] + bif_ref[...]
    log_i = gates
    log_f = jnp.minimum(gates, 0.0) - jnp.log(1.0 + jnp.exp(-jnp.abs(gates)))
    if valid < L:
        live = _iota2((L, 1), 0) < valid
        log_i = jnp.where(live, log_i, -1e30)
        log_f = jnp.where(live, log_f, 0.0)
    tri = _iota2((L, L), 0) >= _iota2((L, L), 1)
    bcs = _cumsum_rows(log_f, tri.astype(BF16))
    for j in range(ML_HB):
        head = pl.program_id(1) * ML_HB + j
        b_col = _lane_col(bcs, ML_HEADS + head)
        i_col = _lane_col(log_i, head)
        b_row = _col_to_row(b_col, L)
        i_row = _col_to_row(i_col, L)
        m_prev = m_scr[:, j:j + 1]
        dmat = jnp.where(tri, b_col - b_row + i_row, NEG_INF)
        inter = b_col + m_prev
        mt = jnp.maximum(inter, jnp.max(dmat, axis=1, keepdims=True))
        w_in = jnp.exp(dmat - mt)
        w_x = jnp.exp(inter - mt)
        qj = q_ref[:, j * ML_DK:(j + 1) * ML_DK]
        kj = k_ref[:, j * ML_DK:(j + 1) * ML_DK] * (ML_DK ** -0.5)
        vj = v_ref[:, j * ML_DV:(j + 1) * ML_DV]
        qb, kb = qj.astype(BF16), kj.astype(BF16)
        sw = _dot_nt(qb, kb) * w_in
        c_prev = c_scr[j]
        n_prev = n_scr[:, j * ML_DK:(j + 1) * ML_DK]
        num = w_x * _dot_nt(qb, c_prev.astype(BF16)) + _dot(sw.astype(BF16), vj.astype(BF16))
        den = w_x * jnp.sum(qj * n_prev, axis=1, keepdims=True) + jnp.sum(sw, axis=1, keepdims=True)
        h = num / jnp.maximum(jnp.abs(den), jnp.exp(-mt))
        m_last = mt[L - 1:L, :]
        b_last = b_col[L - 1:L, :]
        w_end = jnp.exp(b_last - b_col + i_col - m_last)
        d_c = jnp.exp(b_last + m_prev - m_last)
        c_scr[j] = d_c * c_prev + _dot_tn((w_end * vj).astype(BF16), kb)
        n_scr[:, j * ML_DK:(j + 1) * ML_DK] = d_c * n_prev + jnp.sum(w_end * kj, axis=0, keepdims=True)
        m_scr[:, j:j + 1] = m_last
        sv = slice(j * ML_DV, (j + 1) * ML_DV)
        h_n = h * lax.rsqrt(jnp.mean(h * h, axis=-1, keepdims=True) + EPS) * gn_ref[:, sv]
        h_ref[:, sv] = (h_n * jax.nn.sigmoid(og_ref[:, sv]) * _silu(z_ref[:, sv])).astype(h_ref.dtype)

    @pl.when(c == pl.num_programs(2) - 1)
    def _():
        c_out[...] = c_scr[...]
        n_out[...] = n_scr[...]
        m_out[...] = m_scr[...]


def _mlstm_call(ya, yb, c0, n0, m0, bif_r, gn, *, B, T, L, valid):
    nc = T // L
    ng = ML_HEADS // ML_HB
    wk, wv = ML_HB * ML_DK, ML_HB * ML_DV

    def col(name, w):
        blk = (ODD_A[name] if name in ODD_A else ODD_B[name]) // w
        return pl.BlockSpec((L, w), lambda b, hg, c: (b * nc + c, blk + hg))

    c_spec = pl.BlockSpec((None, ML_HB, ML_DV, ML_DK), lambda b, hg, c: (b, hg, 0, 0))
    n_spec = pl.BlockSpec((None, 1, wk), lambda b, hg, c: (b, 0, hg))
    m_spec = pl.BlockSpec((None, None, 1, LANES), lambda b, hg, c: (b, hg, 0, 0))
    m0_r = jnp.pad(m0.reshape(B, ng, 1, ML_HB), ((0, 0), (0, 0), (0, 0), (0, LANES - ML_HB)))
    h, c_new, n_new, m_new = pl.pallas_call(
        functools.partial(_mlstm_body, L=L, valid=valid),
        grid=(B, ng, nc),
        in_specs=[col("q", wk), col("k", wk), col("v", wv), col("og", wv), col("z", wv),
                  pl.BlockSpec((L, LANES), lambda b, hg, c: (b * nc + c, ODD_B["gates"] // LANES)),
                  pl.BlockSpec((1, LANES), lambda b, hg, c: (0, 0)),
                  pl.BlockSpec((1, wv), lambda b, hg, c: (0, hg)),
                  c_spec, n_spec, m_spec],
        out_specs=[pl.BlockSpec((L, wv), lambda b, hg, c: (b * nc + c, hg)), c_spec, n_spec, m_spec],
        out_shape=[jax.ShapeDtypeStruct((B * T, ML_V_W), BF16),
                   jax.ShapeDtypeStruct((B, ML_HEADS, ML_DV, ML_DK), F32),
                   jax.ShapeDtypeStruct((B, 1, ML_QK_W), F32),
                   jax.ShapeDtypeStruct((B, ng, 1, LANES), F32)],
        scratch_shapes=[pltpu.VMEM((ML_HB, ML_DV, ML_DK), F32), pltpu.VMEM((1, wk), F32), pltpu.VMEM((1, LANES), F32)],
        compiler_params=_params(("arbitrary", "arbitrary", "arbitrary")),
        name="mlstm",
    )(ya, ya, ya, ya, yb, yb, bif_r, gn.reshape(1, ML_V_W), c0, n0.reshape(B, 1, ML_QK_W), m0_r)
    return h, c_new, n_new.reshape(B, ML_HEADS, ML_DK), m_new[:, :, 0, :ML_HB].reshape(B, ML_HEADS)


def _mem_body(q_ref, k_ref, v_ref, o_ref):
    q = q_ref[...] * (MEM_HD ** -0.5)
    for h in range(MEM_HEADS):
        sl = slice(h * MEM_HD, (h + 1) * MEM_HD)
        s = _dot_nt(q[:, sl].astype(BF16), k_ref[:, sl].astype(BF16))
        p = jnp.exp(s - jnp.max(s, axis=-1, keepdims=True))
        o = _dot(p.astype(BF16), v_ref[:, sl].astype(BF16)) / jnp.sum(p, axis=-1, keepdims=True)
        o_ref[:, sl] = o.astype(o_ref.dtype)


def _mem_call(y, q_off, k2d, v2d, *, B, T, tq=256):
    tq = min(tq, T)
    nq = T // tq
    qb = q_off // MEM_W
    return pl.pallas_call(
        _mem_body,
        grid=(B, nq),
        in_specs=[pl.BlockSpec((tq, MEM_W), lambda b, i: (b * nq + i, qb)),
                  pl.BlockSpec((N_MEM, MEM_W), lambda b, i: (b, 0)),
                  pl.BlockSpec((N_MEM, MEM_W), lambda b, i: (b, 0))],
        out_specs=pl.BlockSpec((tq, MEM_W), lambda b, i: (b * nq + i, 0)),
        out_shape=jax.ShapeDtypeStruct((B * T, MEM_W), BF16),
        compiler_params=_params(("parallel", "parallel")),
        name="mem_attn",
    )(y, k2d, v2d)


def _gelu_tanh(x):
    return 0.5 * x * (1.0 + jnp.tanh(math.sqrt(2.0 / math.pi) * (x + 0.044715 * (x * x * x))))


def _compress_body(x_ref, w1_ref, b1_ref, w2_ref, pe_ref, o_ref, x32, *, nch):
    x32[...] = x_ref[...].astype(F32)
    a = jnp.zeros((nch, NSA_HD), F32)
    b = jnp.zeros((nch, NSA_HD), F32)
    for s in range(CMP_STRIDE):
        r = x32[pl.ds(s, nch, stride=CMP_STRIDE), :]
        a = a + _dot((r + pe_ref[s:s + 1, :]).astype(BF16), w1_ref[s])
        b = b + _dot((r + pe_ref[CMP_STRIDE + s:CMP_STRIDE + s + 1, :]).astype(BF16), w1_ref[CMP_STRIDE + s])
    h = a + pltpu.roll(b, nch - 1, 0) + b1_ref[...]
    o_ref[...] = _dot(_gelu_tanh(h).astype(BF16), w2_ref[...])


def _compress_call(x16, w1, b1, w2, pe, *, B, T):
    nch = T // CMP_STRIDE
    return pl.pallas_call(
        functools.partial(_compress_body, nch=nch),
        grid=(B, NSA_KVH),
        in_specs=[pl.BlockSpec((T, NSA_HD), lambda b, h: (b, h)),
                  pl.BlockSpec((CMP_BLOCK, NSA_HD, NSA_HD), lambda b, h: (0, 0, 0)),
                  pl.BlockSpec((1, NSA_HD), lambda b, h: (0, 0)),
                  pl.BlockSpec((NSA_HD, NSA_HD), lambda b, h: (0, 0)),
                  pl.BlockSpec((CMP_BLOCK, NSA_HD), lambda b, h: (0, 0))],
        out_specs=pl.BlockSpec((None, None, nch, NSA_HD), lambda b, h: (b, h, 0, 0)),
        out_shape=jax.ShapeDtypeStruct((B, NSA_KVH, nch, NSA_HD), F32),
        scratch_shapes=[pltpu.VMEM((T, NSA_HD), F32)],
        compiler_params=_params(("parallel", "parallel")),
        name="nsa_compress",
    )(x16, w1.astype(BF16), b1.reshape(1, NSA_HD), w2.astype(BF16), pe)


def _softmax_rows(s):
    m = jnp.max(s, axis=-1, keepdims=True)
    m = jnp.where(m == NEG_INF, 0.0, m)
    p = jnp.exp(s - m)
    return p, jnp.sum(p, axis=-1, keepdims=True)


def _slc_scores(psum, width, n_slc):
    ncmp = psum.shape[1]
    d = _iota2((ncmp, width), 0) - (SEL_BLOCK // CMP_STRIDE) * _iota2((ncmp, width), 1)
    wgt = jnp.where((d == -1) | (d == 3), 1.0, jnp.where((d >= 0) & (d <= 2), 2.0, 0.0))
    wgt = jnp.where(_iota2((ncmp, width), 1) < n_slc, wgt, 0.0).astype(BF16)
    p_hi = psum.astype(BF16)
    p_lo = (psum - p_hi.astype(F32)).astype(BF16)
    return _dot(p_hi, wgt) + _dot(p_lo, wgt)


def _top_blocks(slc, cur, n_pick):
    rows, width = slc.shape
    blk = _iota2((rows, width), 1)
    forced = (blk == 0) | (blk == cur) | (blk == cur - 1)
    score = jnp.where(forced, jnp.inf, slc)
    score = jnp.where(blk > cur, NEG_INF, score)
    blk_f = blk.astype(F32)
    lane = _iota2((rows, LANES), 1)
    sel = jnp.zeros((rows, width), F32)
    picks = jnp.zeros((rows, LANES), F32)
    for i in range(n_pick):
        mx = jnp.max(score, axis=-1, keepdims=True)
        first = jnp.min(jnp.where(score == mx, blk_f, float(width)), axis=-1, keepdims=True)
        pick = blk_f == first
        sel = jnp.where(pick, 1.0, sel)
        picks = jnp.where(lane == i, first, picks)
        score = jnp.where(pick, NEG_INF, score)
    return sel, picks


def _member_by_rank(psum, tpos_row, n_slc, n_pick):
    nq, ncmp = psum.shape
    nb = -(-n_slc // 8) * 8
    d = _iota2((nb, ncmp), 1) - (SEL_BLOCK // CMP_STRIDE) * _iota2((nb, ncmp), 0)
    wgt = jnp.where((d == -1) | (d == 3), 1.0, jnp.where((d >= 0) & (d <= 2), 2.0, 0.0))
    wgt = jnp.where(_iota2((nb, ncmp), 0) < n_slc, wgt, 0.0).astype(BF16)
    p_hi = psum.astype(BF16)
    p_lo = (psum - p_hi.astype(F32)).astype(BF16)
    slc = _dot_nt(wgt, p_hi) + _dot_nt(wgt, p_lo)
    blk = _iota2((nb, nq), 0)
    cur = jnp.right_shift(tpos_row, SEL_SHIFT)
    forced = (blk == 0) | (blk == cur) | (blk == cur - 1)
    score = jnp.where(forced, jnp.inf, slc)
    score = jnp.where(blk > cur, NEG_INF, score)
    ahead = jnp.zeros((nb, nq), F32)
    for i in range(n_slc):
        s_i = score[i:i + 1, :]
        ahead = ahead + jnp.where((s_i > score) | ((s_i == score) & (blk > i)), 1.0, 0.0)
    return jnp.where((ahead < n_pick) & (blk <= cur), 1.0, 0.0)


NEAR_COLS = Q_BLOCK + REL_MAX_DIST


def _banded_attention(q, k_ref, v_ref, start, width, mask, near_bias):
    far = width - NEAR_COLS
    s_far = _dot_nt(q, k_ref[pl.ds(start, far), :]) + mask[:, :far]
    s_near = _dot_nt(q, k_ref[pl.ds(start + far, NEAR_COLS), :]) + near_bias + mask[:, far:]
    m = jnp.maximum(jnp.max(s_far, axis=-1, keepdims=True), jnp.max(s_near, axis=-1, keepdims=True))
    m = jnp.where(m == NEG_INF, 0.0, m)
    p_far, p_near = jnp.exp(s_far - m), jnp.exp(s_near - m)
    l = jnp.sum(p_far, axis=-1, keepdims=True) + jnp.sum(p_near, axis=-1, keepdims=True)
    o = (_dot(p_far.astype(BF16), v_ref[pl.ds(start, far), :])
         + _dot(p_near.astype(BF16), v_ref[pl.ds(start + far, NEAR_COLS), :]))
    return o / jnp.maximum(l, TINY)


def _nsa_prompt_body(q_ref, zb_ref, gb_ref, bg_ref, ks_ref, vs_ref, kw_ref, vw_ref, kc_ref, vc_ref,
                     bc_ref, bn_ref, o_ref, ksp, vsp, kwp, vwp, osel, *, T):
    qi = pl.program_id(2)
    tq = Q_BLOCK
    front = T - tq
    wlen = WINDOW + tq
    n_slc = T // SEL_BLOCK

    @pl.when(qi == 0)
    def _():
        ksp[0:front, :] = jnp.zeros((front, NSA_HD), BF16)
        vsp[0:front, :] = jnp.zeros((front, NSA_HD), BF16)
        ksp[front:front + T, :] = ks_ref[...].astype(BF16)
        vsp[front:front + T, :] = vs_ref[...].astype(BF16)
        kwp[0:WINDOW, :] = jnp.zeros((WINDOW, NSA_HD), BF16)
        vwp[0:WINDOW, :] = jnp.zeros((WINDOW, NSA_HD), BF16)
        kwp[WINDOW:WINDOW + T, :] = kw_ref[...].astype(BF16)
        vwp[WINDOW:WINDOW + T, :] = vw_ref[...].astype(BF16)

    t0 = pl.multiple_of(qi * tq, tq)
    tpos = _iota2((tq, 1), 0) + t0
    q_all = q_ref[...] * (NSA_HD ** -0.5)
    q = jnp.concatenate([q_all[:, g * NSA_HD:(g + 1) * NSA_HD] for g in range(NSA_G)], axis=0).astype(BF16)
    bias_near = bn_ref[...].reshape(NSA_G * tq, NEAR_COLS)

    def per_head(a):
        return jnp.concatenate([a] * NSA_G, axis=0)

    ncmp = T // CMP_STRIDE
    vis = tpos >= _iota2((1, ncmp), 1) * CMP_STRIDE + (CMP_BLOCK - 1)
    s = _dot_nt(q, kc_ref[...].astype(BF16)) + bc_ref[...].reshape(NSA_G * tq, ncmp)
    p, l = _softmax_rows(s + per_head(jnp.where(vis, 0.0, NEG_INF)))
    p = p / jnp.maximum(l, TINY)
    o_cmp = _dot(p.astype(BF16), vc_ref[...].astype(BF16))
    psum = p[0:tq]
    for g in range(1, NSA_G):
        psum = psum + p[g * tq:(g + 1) * tq]

    member_t = _member_by_rank(psum, _iota2((1, tq), 1) + t0, n_slc, min(N_SEL, n_slc)).astype(BF16)

    nb = member_t.shape[0]
    n_win = SEL_WINDOWS if T % (SEL_WINDOWS * tq) == 0 else 1
    for i in range(n_win):
        w_prev, w = T * i // n_win, T * (i + 1) // n_win

        @pl.when((qi >= w_prev // tq) & (qi < w // tq))
        def _(w=w):
            off = T - w
            col_blk = (jnp.right_shift(_iota2((nb, w), 1) + off, SEL_SHIFT)
                       + (qi * (tq // SEL_BLOCK) + (tq - T) // SEL_BLOCK))
            expand = (col_blk == _iota2((nb, w), 0)).astype(BF16)
            kpos = _iota2((1, w), 1) + (t0 + tq - w)
            allowed = (_dot_tn(member_t, expand) > 0.5) & (kpos <= tpos)
            mask_s = per_head(jnp.where(allowed, 0.0, NEG_INF))
            osel[...] = _banded_attention(q, ksp, vsp, t0 + off, w, mask_s, bias_near)

    dist = WINDOW + _iota2((tq, wlen), 0) - _iota2((tq, wlen), 1)
    in_win = (dist >= 0) & (dist < WINDOW) & (_iota2((1, wlen), 1) + (t0 - WINDOW) >= 0)
    o_win = _banded_attention(q, kwp, vwp, t0, wlen, per_head(jnp.where(in_win, 0.0, NEG_INF)), bias_near)
    gate = jax.nn.sigmoid(gb_ref[...] + bg_ref[...])
    zb = _silu(zb_ref[...])
    for g in range(NSA_G):
        head = pl.program_id(1) * NSA_G + g
        r = slice(g * tq, (g + 1) * tq)
        mix = (_lane_col(gate, head) * o_cmp[r] + _lane_col(gate, NSA_HEADS + head) * osel[r, :]
               + _lane_col(gate, 2 * NSA_HEADS + head) * o_win[r])
        sl = slice(g * NSA_HD, (g + 1) * NSA_HD)
        o_ref[:, sl] = (mix * zb[:, sl]).astype(o_ref.dtype)


def _nsa_prompt_call(ya, yb, kv16, kcmp, vcmp, bg_r, bias_c, bias_near, *, B, T):
    nq = T // Q_BLOCK
    gw = NSA_G * NSA_HD
    kv_spec = pl.BlockSpec((T, NSA_HD), lambda b, h, i: (b, h))
    cmp_spec = pl.BlockSpec((None, None, T // CMP_STRIDE, NSA_HD), lambda b, h, i: (b, h, 0, 0))
    return pl.pallas_call(
        functools.partial(_nsa_prompt_body, T=T),
        grid=(B, NSA_KVH, nq),
        in_specs=[pl.BlockSpec((Q_BLOCK, gw), lambda b, h, i: (b * nq + i, EVEN_A["qb"] // gw + h)),
                  pl.BlockSpec((Q_BLOCK, gw), lambda b, h, i: (b * nq + i, EVEN_B["zb"] // gw + h)),
                  pl.BlockSpec((Q_BLOCK, LANES), lambda b, h, i: (b * nq + i, EVEN_B["gb"] // LANES)),
                  pl.BlockSpec((1, LANES), lambda b, h, i: (0, 0)),
                  kv_spec, kv_spec, kv_spec, kv_spec, cmp_spec, cmp_spec,
                  pl.BlockSpec((None, NSA_G, Q_BLOCK, T // CMP_STRIDE), lambda b, h, i: (h, 0, i, 0)),
                  pl.BlockSpec((None, NSA_G, Q_BLOCK, NEAR_COLS), lambda b, h, i: (h, 0, 0, 0))],
        out_specs=pl.BlockSpec((Q_BLOCK, gw), lambda b, h, i: (b * nq + i, h)),
        out_shape=jax.ShapeDtypeStruct((B * T, NSA_W), BF16),
        scratch_shapes=[pltpu.VMEM((2 * T - Q_BLOCK, NSA_HD), BF16), pltpu.VMEM((2 * T - Q_BLOCK, NSA_HD), BF16),
                        pltpu.VMEM((WINDOW + T, NSA_HD), BF16), pltpu.VMEM((WINDOW + T, NSA_HD), BF16),
                        pltpu.VMEM((NSA_G * Q_BLOCK, NSA_HD), F32)],
        compiler_params=_params(("arbitrary", "arbitrary", "arbitrary")),
        name="nsa_prompt",
    )(ya, yb, yb, bg_r, *kv16, kcmp, vcmp, bias_c, bias_near)


CMP_PAGES = 16
CHUNKS_PER_PAGE = PAGE_SIZE // CMP_STRIDE
PAGE_ROWS = PAGE_SIZE * NSA_KVH


def _pool_rows(pool):
    return pool.reshape(pool.shape[0] * PAGE_ROWS, NSA_HD)


def _cmp_pages_body(pt_ref, *refs):
    del pt_ref
    pages = refs[:CMP_PAGES]
    w_ref, pe_ref, o_ref = refs[CMP_PAGES:]
    rows = CMP_PAGES * CHUNKS_PER_PAGE
    per_head = [jnp.concatenate(
        [jnp.concatenate([pg[pl.ds(NSA_KVH * s + h, CHUNKS_PER_PAGE, stride=CMP_STRIDE * NSA_KVH), :]
                          for s in range(CMP_STRIDE)], axis=1) for pg in pages], axis=0) for h in range(NSA_KVH)]
    w = w_ref[...]
    r = _dot(jnp.concatenate(per_head, axis=0).astype(BF16), w)
    pc = _dot(pe_ref[...], w)
    r = r + jnp.concatenate([pc[0:1, :NSA_HD], pc[1:2, NSA_HD:]], axis=1)
    for h in range(NSA_KVH):
        o_ref[h] = r[h * rows:(h + 1) * rows]


def _cmp_pages_call(pool, page_table, w1, pe, *, B):
    n_pages = page_table.shape[1]
    rows = CMP_PAGES * CHUNKS_PER_PAGE
    view = _pool_rows(pool)
    w = w1.reshape(2, CMP_STRIDE, NSA_HD, NSA_HD).transpose(1, 2, 0, 3).reshape(CMP_STRIDE * NSA_HD, 2 * NSA_HD)
    pe_rows = jnp.pad(pe.reshape(2, CMP_STRIDE * NSA_HD), ((0, 6), (0, 0))).astype(BF16)

    def page_spec(i):
        return pl.BlockSpec((PAGE_ROWS, NSA_HD), lambda b, s, pt: (pt[b * n_pages + s * CMP_PAGES + i], 0))

    grid_spec = pltpu.PrefetchScalarGridSpec(
        num_scalar_prefetch=1,
        grid=(B, n_pages // CMP_PAGES),
        in_specs=[page_spec(i) for i in range(CMP_PAGES)]
        + [pl.BlockSpec((CMP_STRIDE * NSA_HD, 2 * NSA_HD), lambda b, s, pt: (0, 0)),
           pl.BlockSpec((8, CMP_STRIDE * NSA_HD), lambda b, s, pt: (0, 0))],
        out_specs=pl.BlockSpec((None, NSA_KVH, rows, 2 * NSA_HD), lambda b, s, pt: (b, 0, s, 0)),
    )
    return pl.pallas_call(
        _cmp_pages_body,
        grid_spec=grid_spec,
        out_shape=jax.ShapeDtypeStruct((B, NSA_KVH, n_pages * CHUNKS_PER_PAGE, 2 * NSA_HD), F32),
        compiler_params=_params(("arbitrary", "arbitrary")),
        name="nsa_cmp_pages",
    )(page_table.reshape(-1), *([view] * CMP_PAGES), w.astype(BF16), pe_rows)


SEL_WINDOWS = 4
SLC_LANES = 384


def _sample_q_rows(q_ref):
    q = q_ref[...] * (NSA_HD ** -0.5)
    return jnp.concatenate([q[:, g * NSA_HD:(g + 1) * NSA_HD] for g in range(NSA_G)], axis=0).astype(BF16)


def _nsa_sample_main_body(abk_ref, abv_ref, b1_ref, w2_ref, q_ref, wk_ref, wv_ref, kn_ref, vn_ref, bc_ref, bw_ref,
                          ocmp_ref, owin_ref, idx_ref, *, T, n_slc):
    tp = SAMPLE_PAD_T
    rows = NSA_G * tp
    ncmp = abk_ref.shape[0]

    def compressed(ab_ref, t):
        ab = ab_ref[...]
        h = ab[:, :NSA_HD] + pltpu.roll(ab[:, NSA_HD:], ncmp - 1, 0) + b1_ref[t]
        return _dot(_gelu_tanh(h).astype(BF16), w2_ref[t]).astype(BF16)

    kc, vc = compressed(abk_ref, 0), compressed(abv_ref, 1)
    q = _sample_q_rows(q_ref)
    step = jnp.bitwise_and(_iota2((rows, 1), 0), tp - 1)
    tpos = PAST_LEN + step
    vis = tpos >= _iota2((1, ncmp), 1) * CMP_STRIDE + (CMP_BLOCK - 1)
    p, l = _softmax_rows(jnp.where(vis, _dot_nt(q, kc) + bc_ref[...], NEG_INF))
    p = p / jnp.maximum(l, TINY)
    ocmp_ref[...] = _dot(p.astype(BF16), vc)
    psum = p[0:tp]
    for g in range(1, NSA_G):
        psum = psum + p[g * tp:(g + 1) * tp]
    cur = jnp.right_shift(PAST_LEN + _iota2((tp, 1), 0), SEL_SHIFT)
    _, picks = _top_blocks(_slc_scores(psum, SLC_LANES, n_slc), cur, N_SEL)
    idx_ref[...] = picks.astype(jnp.int32)

    wb = wk_ref.shape[0] // NSA_KVH
    wlen = bw_ref.shape[1]
    fill = jnp.zeros((wlen - wb - tp, NSA_HD), BF16)
    head = pl.program_id(1)
    k_all = jnp.concatenate([wk_ref[pl.ds(head, wb, stride=NSA_KVH), :].astype(BF16), kn_ref[...], fill], axis=0)
    v_all = jnp.concatenate([wv_ref[pl.ds(head, wb, stride=NSA_KVH), :].astype(BF16), vn_ref[...], fill], axis=0)
    col = _iota2((1, wlen), 1)
    dist = tpos - (PAST_LEN - wb + col)
    in_win = (dist >= 0) & (dist < WINDOW) & (col < wb + T)
    pw, lw = _softmax_rows(jnp.where(in_win, _dot_nt(q, k_all) + bw_ref[...], NEG_INF))
    owin_ref[...] = _dot(pw.astype(BF16), v_all) / jnp.maximum(lw, TINY)


def _nsa_sample_main_call(ya, kw16, vw16, abk, abv, b1, w2, wk, wv, bias_c, bias_w, *, B, T):
    tp = SAMPLE_PAD_T
    rows = NSA_G * tp
    gw = NSA_G * NSA_HD
    ncmp = abk.shape[2]
    wb = wk.shape[1]
    wlen = bias_w.shape[-1]
    n_slc = -(-(PAST_LEN + T) // SEL_BLOCK)
    assert n_slc <= SLC_LANES and T <= tp
    ab_spec = pl.BlockSpec((None, None, ncmp, 2 * NSA_HD), lambda b, h: (b, h, 0, 0))
    win_spec = pl.BlockSpec((wb * NSA_KVH, NSA_HD), lambda b, h: (b, 0))
    o_spec = pl.BlockSpec((None, None, rows, NSA_HD), lambda b, h: (b, h, 0, 0))
    return pl.pallas_call(
        functools.partial(_nsa_sample_main_body, T=T, n_slc=n_slc),
        grid=(B, NSA_KVH),
        in_specs=[ab_spec, ab_spec,
                  pl.BlockSpec((2, 1, NSA_HD), lambda b, h: (0, 0, 0)),
                  pl.BlockSpec((2, NSA_HD, NSA_HD), lambda b, h: (0, 0, 0)),
                  pl.BlockSpec((tp, gw), lambda b, h: (b, EVEN_A["qb"] // gw + h)),
                  win_spec, win_spec,
                  pl.BlockSpec((tp, NSA_HD), lambda b, h: (b, h)),
                  pl.BlockSpec((tp, NSA_HD), lambda b, h: (b, h)),
                  pl.BlockSpec((None, rows, ncmp), lambda b, h: (h, 0, 0)),
                  pl.BlockSpec((None, rows, wlen), lambda b, h: (h, 0, 0))],
        out_specs=[o_spec, o_spec, pl.BlockSpec((None, None, tp, LANES), lambda b, h: (b, h, 0, 0))],
        out_shape=[jax.ShapeDtypeStruct((B, NSA_KVH, rows, NSA_HD), F32),
                   jax.ShapeDtypeStruct((B, NSA_KVH, rows, NSA_HD), F32),
                   jax.ShapeDtypeStruct((B, NSA_KVH, tp, LANES), jnp.int32)],
        compiler_params=_params(("parallel", "parallel")),
        name="nsa_sample_main",
    )(abk, abv, b1.reshape(2, 1, NSA_HD), w2.astype(BF16), ya,
      wk.reshape(B * wb * NSA_KVH, NSA_HD), wv.reshape(B * wb * NSA_KVH, NSA_HD), kw16, vw16, bias_c, bias_w)


NEAR_BLOCKS = 3


def _nsa_sample_sel_body(idx_ref, pt_ref, q_ref, kn_ref, vn_ref, tbl_ref, ocmp_ref, owin_ref, gb_ref, bg_ref, zb_ref,
                         *refs, T):
    del pt_ref
    k_blocks = refs[:N_SEL]
    v_blocks = refs[N_SEL:2 * N_SEL]
    o_ref, osel = refs[2 * N_SEL:]
    tp = SAMPLE_PAD_T
    rows = NSA_G * tp
    b, h, t = pl.program_id(0), pl.program_id(1), pl.program_id(2)
    base = ((b * NSA_KVH + h) * T + t) * N_SEL
    first_new = PAST_LEN // SEL_BLOCK
    cur = jnp.right_shift(PAST_LEN + t, SEL_SHIFT)
    q = _sample_q_rows(q_ref)
    pad = jnp.zeros((SEL_BLOCK - tp, NSA_HD), BF16)
    k_new = jnp.concatenate([kn_ref[...], pad], axis=0)
    v_new = jnp.concatenate([vn_ref[...], pad], axis=0)
    lane = _iota2((1, LANES), 1)
    low = lane < SEL_BLOCK
    within = jnp.bitwise_and(lane, SEL_BLOCK - 1)
    ks, vs, bias, kpos = [], [], [], []
    for i in range(0, N_SEL, 2):
        pair_bias, pair_pos = [], []
        for j in (i, i + 1):
            blk = idx_ref[base + j]
            is_new = blk >= first_new
            ks.append(jnp.where(is_new, k_new, k_blocks[j][pl.ds(h, SEL_BLOCK, stride=NSA_KVH), :].astype(BF16)))
            vs.append(jnp.where(is_new, v_new, v_blocks[j][pl.ds(h, SEL_BLOCK, stride=NSA_KVH), :].astype(BF16)))
            pair_bias.append(tbl_ref[jnp.clip(blk - (first_new - NEAR_BLOCKS), 0, NEAR_BLOCKS)])
            pair_pos.append(jnp.where(blk <= cur, blk * SEL_BLOCK, PAST_LEN + SEL_BLOCK * LANES) + within)
        bias.append(jnp.where(low, pair_bias[0], pair_bias[1]))
        kpos.append(jnp.where(low, pair_pos[0], pair_pos[1]))
    k_all = jnp.concatenate(ks, axis=0)
    v_all = jnp.concatenate(vs, axis=0)
    step = jnp.bitwise_and(_iota2((rows, 1), 0), tp - 1)
    ok = jnp.concatenate(kpos, axis=1) <= PAST_LEN + step
    p, l = _softmax_rows(jnp.where(ok, _dot_nt(q, k_all) + jnp.concatenate(bias, axis=1), NEG_INF))
    o = _dot(p.astype(BF16), v_all) / jnp.maximum(l, TINY)

    @pl.when(t == 0)
    def _():
        osel[...] = jnp.zeros_like(osel)

    osel[...] = jnp.where(step == t, o, osel[...])

    @pl.when(t == T - 1)
    def _():
        gate = jax.nn.sigmoid(gb_ref[...] + bg_ref[...])
        zb = _silu(zb_ref[...])
        for g in range(NSA_G):
            r = slice(g * tp, (g + 1) * tp)
            head = h * NSA_G + g
            mix = (_lane_col(gate, head) * ocmp_ref[r, :] + _lane_col(gate, NSA_HEADS + head) * osel[r, :]
                   + _lane_col(gate, 2 * NSA_HEADS + head) * owin_ref[r, :])
            sl = slice(g * NSA_HD, (g + 1) * NSA_HD)
            o_ref[:, sl] = (mix * zb[:, sl]).astype(o_ref.dtype)


def _nsa_sample_sel_call(ya, yb, ks16, vs16, idx, page_table, pool_k, pool_v, tbl, o_cmp, o_win, bg_r, *, B, T):
    tp = SAMPLE_PAD_T
    rows = NSA_G * tp
    gw = NSA_G * NSA_HD
    n_pages = page_table.shape[1]
    halves = PAGE_SIZE // SEL_BLOCK
    idx_flat = idx[:, :, :T, :N_SEL].reshape(-1)
    view_k, view_v = _pool_rows(pool_k), _pool_rows(pool_v)

    def blk_spec(j):
        def index(b, h, t, idx_s, pt_s):
            blk = idx_s[((b * NSA_KVH + h) * T + t) * N_SEL + j]
            page = pt_s[b * n_pages + jnp.minimum(blk // halves, n_pages - 1)]
            return (page * halves + blk % halves, 0)
        return pl.BlockSpec((SEL_BLOCK * NSA_KVH, NSA_HD), index)

    o_spec = pl.BlockSpec((None, None, rows, NSA_HD), lambda b, h, t, *_: (b, h, 0, 0))
    grid_spec = pltpu.PrefetchScalarGridSpec(
        num_scalar_prefetch=2,
        grid=(B, NSA_KVH, T),
        in_specs=[pl.BlockSpec((tp, gw), lambda b, h, t, *_: (b, EVEN_A["qb"] // gw + h)),
                  pl.BlockSpec((tp, NSA_HD), lambda b, h, t, *_: (b, h)),
                  pl.BlockSpec((tp, NSA_HD), lambda b, h, t, *_: (b, h)),
                  pl.BlockSpec((None, NEAR_BLOCKS + 1, rows, LANES), lambda b, h, t, *_: (h, 0, 0, 0)),
                  o_spec, o_spec,
                  pl.BlockSpec((tp, LANES), lambda b, h, t, *_: (b, EVEN_B["gb"] // LANES)),
                  pl.BlockSpec((1, LANES), lambda b, h, t, *_: (0, 0)),
                  pl.BlockSpec((tp, gw), lambda b, h, t, *_: (b, EVEN_B["zb"] // gw + h))]
        + [blk_spec(j) for j in range(N_SEL)] * 2,
        out_specs=pl.BlockSpec((tp, gw), lambda b, h, t, *_: (b, h)),
        scratch_shapes=[pltpu.VMEM((rows, NSA_HD), F32)],
    )
    return pl.pallas_call(
        functools.partial(_nsa_sample_sel_body, T=T),
        grid_spec=grid_spec,
        out_shape=jax.ShapeDtypeStruct((B * tp, NSA_W), BF16),
        compiler_params=_params(("arbitrary", "arbitrary", "arbitrary")),
        name="nsa_sample_sel",
    )(idx_flat, page_table.reshape(-1), ya, ks16, vs16, tbl, o_cmp, o_win, yb, bg_r, yb,
      *([view_k] * N_SEL), *([view_v] * N_SEL))


def _sample_bias_tables(rel_bias, T, wb):
    tp = SAMPLE_PAD_T
    ncmp = PAST_LEN // CMP_STRIDE
    wlen = -(-(wb + tp) // LANES) * LANES
    first = PAST_LEN // SEL_BLOCK - NEAR_BLOCKS
    assert PAST_LEN - ((first + 1) * SEL_BLOCK - 1) >= REL_MAX_DIST
    lo, hi = -wlen, PAST_LEN + tp
    rev = _bias_line(rel_bias, lo, hi, descending=True)

    def rows(tbl):
        return tbl.reshape(NSA_KVH, NSA_G * tp, tbl.shape[-1])

    t_c = _toeplitz(rev, hi - 1 - (PAST_LEN - (CMP_BLOCK - 1)), tp, CMP_STRIDE * ncmp)[:, :, ::CMP_STRIDE]
    t_w = _toeplitz(rev, hi - 1 - wb, tp, wlen)
    far = jnp.broadcast_to(rev[:, hi - 1 - REL_MAX_DIST][:, None, None], (NSA_HEADS, tp, LANES))
    near = []
    for k in range(1, NEAR_BLOCKS + 1):
        half = _toeplitz(rev, hi - 1 - (PAST_LEN - (first + k) * SEL_BLOCK), tp, SEL_BLOCK)
        near.append(jnp.concatenate([half, half], axis=-1))
    t_s = jnp.stack([far] + near, axis=1).reshape(NSA_KVH, NSA_G, NEAR_BLOCKS + 1, tp, LANES)
    t_s = t_s.transpose(0, 2, 1, 3, 4).reshape(NSA_KVH, NEAR_BLOCKS + 1, NSA_G * tp, LANES)
    return rows(t_c), rows(t_w), t_s


def _tail_even(w):
    return _tail_relayout(w, EVEN_KV_OFF + 6 * NSA_KV_W, 3 * NSA_HEADS, NSA_W + MEM_W, EVEN_B_N)


def _tail_odd(w):
    return _tail_relayout(w, ODD_A_N, 2 * ML_HEADS, ML_V_W + MEM_W, ODD_B_N)


def _gate_bias_even(b_gate):
    return jnp.pad(b_gate, (0, LANES - 3 * NSA_HEADS)).reshape(1, LANES)


def _gate_bias_odd(b_if):
    return jnp.pad(b_if.reshape(2 * ML_HEADS), (0, LANES - 2 * ML_HEADS)).reshape(1, LANES)


def _rel_bucket(dist):
    n = np.maximum(dist, 0)
    exact = REL_BUCKETS // 2
    nf = np.maximum(n, 1).astype(np.float32)
    large = exact + (np.log(nf / exact) / math.log(REL_MAX_DIST / exact) * (REL_BUCKETS - exact)).astype(np.int32)
    return np.where(n < exact, n, np.minimum(large, REL_BUCKETS - 1))


def _bias_line(rel_bias, lo, hi, descending=False):
    dist = np.arange(hi - 1, lo - 1, -1) if descending else np.arange(lo, hi)
    buckets = _rel_bucket(dist)
    edges = np.flatnonzero(np.diff(buckets)) + 1
    starts = np.concatenate([[0], edges])
    ends = np.concatenate([edges, [hi - lo]])
    bias_t = rel_bias.T.astype(F32)
    runs = [jnp.broadcast_to(bias_t[:, int(buckets[s])][:, None], (NSA_HEADS, int(e - s))) for s, e in zip(starts, ends)]
    return jnp.concatenate(runs, axis=1)


def _skew_rows(v, rows, step, cols):
    n = v.shape[1]
    reps = -(-rows * (n + step) // n)
    return jnp.tile(v, (1, reps))[:, :rows * (n + step)].reshape(v.shape[0], rows, n + step)[:, :, :cols]


def _toeplitz(rev, start, rows, cols):
    seg = rev[:, start - (rows - 1):start + cols]
    return _skew_rows(jnp.roll(seg, -(rows - 1), axis=1), rows, -1, cols)


def _prompt_bias_tables(rel_bias, T):
    ncmp = T // CMP_STRIDE
    assert Q_BLOCK + 1 >= REL_MAX_DIST
    lo, hi = -(CMP_STRIDE * ncmp + CMP_BLOCK), T
    line = _bias_line(rel_bias, lo, hi)
    rev = _bias_line(rel_bias, lo, hi, descending=True)

    def split(tbl):
        return tbl.reshape((NSA_KVH, NSA_G) + tbl.shape[1:])

    back = CMP_STRIDE * (ncmp - 1)
    first = -(back + CMP_BLOCK - 1) - lo
    seg = line[:, first:first + T + back]
    t_c = _skew_rows(jnp.roll(seg, -back, axis=1), ncmp, -CMP_STRIDE, T).swapaxes(1, 2)
    far = rev[:, hi - 1 - REL_MAX_DIST]
    t_near = _toeplitz(rev, hi - 1 - REL_MAX_DIST, Q_BLOCK, NEAR_COLS) - far[:, None, None]
    return split(t_c), split(t_near)


def _nsa_sample(ya, yb, kv16, page_table, pk_cmp, pv_cmp, pk_sel, pv_sel, wk, wv, bg_r, w1, b1, w2, pe, rel_bias,
                *, B, T):
    assert (PAST_LEN + T) // CMP_STRIDE == PAST_LEN // CMP_STRIDE
    abk = _cmp_pages_call(pk_cmp, page_table, w1[0], pe[0], B=B)
    abv = _cmp_pages_call(pv_cmp, page_table, w1[1], pe[1], B=B)
    bias_c, bias_w, tbl = _sample_bias_tables(rel_bias, T, wk.shape[1])
    o_cmp, o_win, idx = _nsa_sample_main_call(ya, kv16[4], kv16[5], abk, abv, b1, w2, wk, wv, bias_c, bias_w, B=B, T=T)
    return _nsa_sample_sel_call(ya, yb, kv16[2], kv16[3], idx, page_table, pk_sel, pv_sel, tbl, o_cmp, o_win, bg_r,
                                B=B, T=T)


def _kv_project(x, wt):
    return _matmul_heads(x, wt, EVEN_KV_OFF, 6, transposed=True)


def _even_prompt(hp2d, npre, mk16, mv16, wt, wt_b, bg_r, w1, b1, w2, pe, lb, g_norm, w_out, rel_bias, *, B, T):
    ya, yb = _matmul_nt(npre, wt, tm=W_TILE_M, tn=W_TILE_N, rows=(0, EVEN_A_N)), _matmul_nt(npre, wt_b)
    kv32, kv16 = _kv_project(npre, wt)
    oa, s_new = _hgrn_call(ya, jnp.zeros((B, HG_HEADS, HG_DK, HG_DV), F32), lb, g_norm, B=B, T=T, L=CHUNK, valid=CHUNK)
    kcmp = _compress_call(kv16[0], w1[0], b1[0], w2[0], pe[0], B=B, T=T)
    vcmp = _compress_call(kv16[1], w1[1], b1[1], w2[1], pe[1], B=B, T=T)
    ob = _nsa_prompt_call(ya, yb, kv16[2:], kcmp, vcmp, bg_r, *_prompt_bias_tables(rel_bias, T), B=B, T=T)
    om = _mem_call(yb, EVEN_B["qm"], mk16, mv16, B=B, T=T)
    h_new = _outproj([oa, ob, om], w_out, hp2d)
    wb = min(WINDOW, T)
    rows = [r.reshape(B, T, NSA_KVH, NSA_HD) for r in kv32]
    return h_new, (rows[0], rows[1], rows[2], rows[3], rows[4][:, -wb:], rows[5][:, -wb:], s_new)


def _even_sample(hs2d, nsam, mk_s, mv_s, page_table, pk_cmp, pv_cmp, pk_sel, pv_sel, wk, wv, s0,
                 wt, wt_b, bg_r, w1, b1, w2, pe, lb, g_norm, w_out, rel_bias, *, B, T):
    tp = SAMPLE_PAD_T
    ya, yb = _matmul_nt(nsam, wt, tm=W_TILE_M, tn=W_TILE_N, rows=(0, EVEN_A_N)), _matmul_nt(nsam, wt_b)
    kv32, kv16 = _kv_project(nsam, wt)
    oa, s_new = _hgrn_call(ya, s0, lb, g_norm, B=B, T=tp, L=tp, valid=T)
    ob = _nsa_sample(ya, yb, kv16, page_table, pk_cmp, pv_cmp, pk_sel, pv_sel, wk, wv, bg_r, w1, b1, w2, pe, rel_bias,
                     B=B, T=T)
    om = _mem_call(yb, EVEN_B["qm"], mk_s.reshape(B * N_MEM, MEM_W), mv_s.reshape(B * N_MEM, MEM_W), B=B, T=tp)
    rows = [r.reshape(B, tp, NSA_KVH, NSA_HD)[:, :T] for r in kv32]
    wb = wk.shape[1]
    win_k = jnp.concatenate([wk, rows[4]], axis=1)[:, -wb:]
    win_v = jnp.concatenate([wv, rows[5]], axis=1)[:, -wb:]
    return _outproj([oa, ob, om], w_out, hs2d), (rows[0], rows[1], rows[2], rows[3], win_k, win_v, s_new)


def _odd_mix(h2d, hn, k2d, v2d, c0, n0, m0, wt, wt_b, bif_r, g_norm, w_out, *, B, T, L, valid):
    ya, yb = _matmul_nt(hn, wt, tm=W_TILE_M, tn=W_TILE_N, rows=(0, ODD_A_N)), _matmul_nt(hn, wt_b)
    h, c_new, n_new, m_new = _mlstm_call(ya, yb, c0, n0, m0, bif_r, g_norm, B=B, T=T, L=L, valid=valid)
    om = _mem_call(yb, ODD_B["qm"], k2d, v2d, B=B, T=T)
    return _outproj([h, om], w_out, h2d), (c_new, n_new, m_new)


def _stack(lst, i):
    return jnp.stack([t[i] for t in lst])


def kernel(x_prompt, x_sample, cache_mem_k, cache_mem_v, cache_cmp_k, cache_cmp_v, cache_sel_k, cache_sel_v,
           cache_win_k, cache_win_v, state_hgrn, state_mlstm_c, state_mlstm_n, state_mlstm_m, page_table,
           mem_prompt, norm_w, mem_norm_w, final_norm_w, rel_bias, w_mem_kv, w_in_even, b_nsa_gate,
           w_cmp1, b_cmp1, w_cmp2, pe_cmp, hgrn_lb_logits, hgrn_norm_w, w_out_even, w_in_odd, b_mlstm_if,
           mlstm_norm_w, w_out_odd):
    bp, tp = x_prompt.shape[:2]
    bs, ts = x_sample.shape[:2]
    tsp = SAMPLE_PAD_T
    lbs = jnp.cumsum(jax.nn.softmax(hgrn_lb_logits.astype(F32), axis=0), axis=0)
    hp = x_prompt.reshape(bp * tp, D_MODEL)
    hs = jnp.pad(x_sample, ((0, 0), (0, tsp - ts), (0, 0))).reshape(bs * tsp, D_MODEL)
    mem2d = mem_prompt.reshape(bp * N_MEM, D_MODEL)
    mem_new, even_p, even_s, odd_p, odd_s = [], [], [], [], []
    for l in range(DEPTH):
        npre = _rmsnorm_rows(hp, norm_w[l], BF16)
        nsam = _rmsnorm_rows(hs, norm_w[l], BF16)
        nmem = _rmsnorm_rows(mem2d, mem_norm_w[l], BF16)
        (mk32, mv32), (mk16, mv16) = _matmul_heads(nmem, w_mem_kv[l], 0, 2)
        mem_new.append((mk32.reshape(bp, N_MEM, MEM_HEADS, MEM_HD), mv32.reshape(bp, N_MEM, MEM_HEADS, MEM_HD)))
        mk_s, mv_s = cache_mem_k[l], cache_mem_v[l]
        if l % 2 == 0:
            e = l // 2
            w_in = w_in_even[e].T
            w_b = _tail_even(w_in)
            w_out = w_out_even[e].astype(BF16)
            bg_r = _gate_bias_even(b_nsa_gate[e])
            cmpw = (w_cmp1[e].reshape(2, CMP_BLOCK, NSA_HD, NSA_HD), b_cmp1[e], w_cmp2[e], pe_cmp[e])
            hp, st_p = _even_prompt(hp, npre, mk16, mv16, w_in, w_b, bg_r, *cmpw, lbs[l], hgrn_norm_w[e], w_out,
                                    rel_bias, B=bp, T=tp)
            hs, st_s = _even_sample(hs, nsam, mk_s, mv_s, page_table, cache_cmp_k[e], cache_cmp_v[e], cache_sel_k[e],
                                    cache_sel_v[e], cache_win_k[e], cache_win_v[e], state_hgrn[e], w_in, w_b, bg_r,
                                    *cmpw, lbs[l], hgrn_norm_w[e], w_out, rel_bias, B=bs, T=ts)
            even_p.append(st_p)
            even_s.append(st_s)
        else:
            o = l // 2
            w_in = w_in_odd[o].T
            w_b = _tail_odd(w_in)
            w_out = w_out_odd[o].astype(BF16)
            bif_r = _gate_bias_odd(b_mlstm_if[o])
            hp, st_p = _odd_mix(hp, npre, mk16, mv16, jnp.zeros((bp, ML_HEADS, ML_DV, ML_DK), F32),
                                jnp.zeros((bp, ML_HEADS, ML_DK), F32), jnp.zeros((bp, ML_HEADS), F32),
                                w_in, w_b, bif_r, mlstm_norm_w[o], w_out, B=bp, T=tp, L=ML_CHUNK, valid=ML_CHUNK)
            hs, st_s = _odd_mix(hs, nsam, mk_s.reshape(bs * N_MEM, MEM_W), mv_s.reshape(bs * N_MEM, MEM_W),
                                state_mlstm_c[o], state_mlstm_n[o], state_mlstm_m[o],
                                w_in, w_b, bif_r, mlstm_norm_w[o], w_out, B=bs, T=tsp, L=tsp, valid=ts)
            odd_p.append(st_p)
            odd_s.append(st_s)
    y_prompt = _rmsnorm_rows(hp, final_norm_w, F32).reshape(bp, tp, D_MODEL)
    y_sample = _rmsnorm_rows(hs, final_norm_w, F32).reshape(bs, tsp, D_MODEL)[:, :ts]
    return (y_prompt, y_sample,
            _stack(mem_new, 0), _stack(mem_new, 1),
            _stack(even_p, 0), _stack(even_p, 1), _stack(even_p, 2), _stack(even_p, 3),
            _stack(even_p, 4), _stack(even_p, 5), _stack(even_p, 6),
            _stack(odd_p, 0), _stack(odd_p, 1), _stack(odd_p, 2),
            _stack(even_s, 0), _stack(even_s, 1), _stack(even_s, 2), _stack(even_s, 3),
            _stack(even_s, 4), _stack(even_s, 5), _stack(even_s, 6),
            _stack(odd_s, 0), _stack(odd_s, 1), _stack(odd_s, 2))
```

```python
import functools
import math

import jax
import jax.numpy as jnp
import numpy as np
from jax import lax
from jax.experimental import pallas as pl
from jax.experimental.pallas import tpu as pltpu

D_MODEL = 4096
DEPTH = 2
PAST_LEN = 16384
PAGE_SIZE = 128
N_MEM = 256
EPS = 1e-6
CHUNK = 64

HG_DK = 128
HG_DV = 128
HG_HEADS = D_MODEL // 2 // HG_DV
HG_W = HG_HEADS * HG_DV

NSA_HD = 128
NSA_HEADS = D_MODEL // 2 // NSA_HD
NSA_KVH = 4
NSA_G = NSA_HEADS // NSA_KVH
NSA_W = NSA_HEADS * NSA_HD
NSA_KV_W = NSA_KVH * NSA_HD
CMP_BLOCK = 32
CMP_STRIDE = 16
SEL_BLOCK = 64
SEL_SHIFT = SEL_BLOCK.bit_length() - 1
N_SEL = 16
WINDOW = 512
Q_BLOCK = 256

ML_HEADS = D_MODEL // 512
ML_DK = D_MODEL // 2 // ML_HEADS
ML_DV = D_MODEL // ML_HEADS
ML_QK_W = ML_HEADS * ML_DK
ML_V_W = ML_HEADS * ML_DV

MEM_HEADS = 4
MEM_HD = 128
MEM_W = MEM_HEADS * MEM_HD

REL_BUCKETS = 32
REL_MAX_DIST = 128

F32 = jnp.float32
BF16 = jnp.bfloat16
LANES = 128
NEG_INF = float("-inf")
TINY = float(np.finfo(np.float32).tiny)
EXP_CLAMP = 80.0
VMEM_LIMIT = 56 * 1024 * 1024

HG_HB = 16
ML_HB = 2
ML_CHUNK = 256
W_TILE_M, W_TILE_N = 1024, 512
HG_SUB = 16
SAMPLE_PAD_T = 16

MM_TILE_N = 1024
EVEN_A = {"qa": 0, "fa": HG_W, "ia": 2 * HG_W, "za": 3 * HG_W, "qb": 4 * HG_W}
EVEN_A_N = 4 * HG_W + NSA_W
EVEN_B = {"zb": 0, "qm": NSA_W, "gb": NSA_W + MEM_W}
EVEN_B_N = -(-(NSA_W + MEM_W + LANES) // MM_TILE_N) * MM_TILE_N
EVEN_KV_OFF = EVEN_A_N
ODD_A = {"q": 0, "k": ML_QK_W, "v": 2 * ML_QK_W, "og": 2 * ML_QK_W + ML_V_W}
ODD_A_N = 2 * ML_QK_W + 2 * ML_V_W
ODD_B = {"z": 0, "qm": ML_V_W, "gates": ML_V_W + MEM_W}
ODD_B_N = -(-(ML_V_W + MEM_W + LANES) // MM_TILE_N) * MM_TILE_N


def _dot(a, b):
    return jnp.dot(a, b, preferred_element_type=F32)


def _dot_nt(a, b):
    return lax.dot_general(a, b, (((1,), (1,)), ((), ())), preferred_element_type=F32)


def _dot_tn(a, b):
    return lax.dot_general(a, b, (((0,), (0,)), ((), ())), preferred_element_type=F32)


def _iota2(shape, dim):
    return lax.broadcasted_iota(jnp.int32, shape, dim)


def _cumsum_rows(x, tri_b):
    hi = x.astype(BF16)
    r1 = x - hi.astype(F32)
    mid = r1.astype(BF16)
    lo = (r1 - mid.astype(F32)).astype(BF16)
    return _dot(tri_b, hi) + _dot(tri_b, mid) + _dot(tri_b, lo)


def _row_to_col(row, n):
    eye = _iota2((n, n), 0) == _iota2((n, n), 1)
    return jnp.sum(jnp.where(eye, row, 0.0), axis=1, keepdims=True)


def _col_to_row(col, n):
    eye = _iota2((n, n), 0) == _iota2((n, n), 1)
    return jnp.sum(jnp.where(eye, col, 0.0), axis=0, keepdims=True)


def _lane_col(x, idx):
    return jnp.sum(jnp.where(_iota2(x.shape, 1) == idx, x, 0.0), axis=1, keepdims=True)


def _silu(x):
    return x * jax.nn.sigmoid(x)


def _params(sem):
    return pltpu.CompilerParams(dimension_semantics=sem, vmem_limit_bytes=VMEM_LIMIT)


def _rmsnorm_body(x_ref, w_ref, o_ref):
    x = x_ref[...].astype(F32)
    y = x * lax.rsqrt(jnp.mean(x * x, axis=-1, keepdims=True) + EPS)
    o_ref[...] = (y * w_ref[...].astype(F32)).astype(o_ref.dtype)


def _rmsnorm_rows(x2d, w, out_dtype, tm=256):
    m, d = x2d.shape
    tm = min(tm, m)
    return pl.pallas_call(
        _rmsnorm_body,
        grid=(m // tm,),
        in_specs=[pl.BlockSpec((tm, d), lambda i: (i, 0)), pl.BlockSpec((1, d), lambda i: (0, 0))],
        out_specs=pl.BlockSpec((tm, d), lambda i: (i, 0)),
        out_shape=jax.ShapeDtypeStruct((m, d), out_dtype),
        compiler_params=_params(("parallel",)),
        name="rmsnorm",
    )(x2d, w.reshape(1, d))


def _project_body(ap_ref, as_ref, bt_ref, yp_ref, ys_ref, wb):
    @pl.when(pl.program_id(1) == 0)
    def _():
        wb[...] = bt_ref[...].astype(BF16)
        ys_ref[...] = _dot_nt(as_ref[...], wb[...])

    yp_ref[...] = _dot_nt(ap_ref[...], wb[...])


def _project(ap, as_, bt, tm=W_TILE_M, tn=W_TILE_N, rows=None):
    mp, k = ap.shape
    ms = as_.shape[0]
    first, n = rows or (0, bt.shape[0])
    assert mp % tm == 0 and n % tn == 0 and first % tn == 0, (ap.shape, bt.shape, rows)
    j0 = first // tn
    return pl.pallas_call(
        _project_body,
        grid=(n // tn, mp // tm),
        in_specs=[pl.BlockSpec((tm, k), lambda j, i: (i, 0)), pl.BlockSpec((ms, k), lambda j, i: (0, 0)),
                  pl.BlockSpec((tn, k), lambda j, i: (j0 + j, 0))],
        out_specs=[pl.BlockSpec((tm, tn), lambda j, i: (i, j)), pl.BlockSpec((ms, tn), lambda j, i: (0, j))],
        out_shape=[jax.ShapeDtypeStruct((mp, n), F32), jax.ShapeDtypeStruct((ms, n), F32)],
        scratch_shapes=[pltpu.VMEM((tn, k), BF16)],
        compiler_params=_params(("arbitrary", "arbitrary")),
        name="project",
    )(ap, as_, bt)


def _tail_body(lo_ref, hi_ref, gate_ref, o_ref, *, shift, n_main):
    i = pl.program_id(0)
    main = jnp.concatenate([lo_ref[shift:, :], hi_ref[:shift, :]], axis=0)
    gates = jnp.where(_iota2((LANES, 1), 0) < shift, gate_ref[...], 0.0)
    o_ref[...] = jnp.where(i < n_main, main, jnp.where(i == n_main, gates, 0.0)).astype(o_ref.dtype)


def _tail_relayout(wt, first, shift, main, out_rows):
    n, k = wt.shape
    assert first % LANES == 0 and main % LANES == 0 and out_rows % LANES == 0 and shift % 8 == 0 and shift < LANES
    assert first + shift + main == n
    c0, n_main = first // LANES, main // LANES
    return pl.pallas_call(
        functools.partial(_tail_body, shift=shift, n_main=n_main),
        grid=(out_rows // LANES,),
        in_specs=[pl.BlockSpec((LANES, k), lambda i: (c0 + jnp.minimum(i, n_main - 1), 0)),
                  pl.BlockSpec((LANES, k), lambda i: (c0 + jnp.minimum(i, n_main - 1) + 1, 0)),
                  pl.BlockSpec((LANES, k), lambda i: (c0, 0))],
        out_specs=pl.BlockSpec((LANES, k), lambda i: (i, 0)),
        out_shape=jax.ShapeDtypeStruct((out_rows, k), BF16),
        compiler_params=_params(("parallel",)),
        name="tail_relayout",
    )(wt, wt, wt)


def _matmul_heads_body(a_ref, b_ref, o32_ref, o16_ref, *, transposed):
    b = b_ref[...].astype(BF16)
    acc = _dot_nt(a_ref[...], b) if transposed else _dot(a_ref[...], b)
    for h in range(MEM_HEADS):
        o32_ref[:, h, :] = acc[:, h * LANES:(h + 1) * LANES]
    o16_ref[...] = acc.astype(BF16)


def _matmul_heads(a, b, first=0, transposed=False, tm=1024):
    m, k = a.shape
    n = MEM_HEADS * LANES
    tm = min(tm, m)
    assert m % tm == 0 and first % n == 0, (a.shape, b.shape, first)
    j0 = first // n
    b_spec = pl.BlockSpec((n, k), lambda i: (j0, 0)) if transposed else pl.BlockSpec((k, n), lambda i: (0, j0))
    return pl.pallas_call(
        functools.partial(_matmul_heads_body, transposed=transposed),
        grid=(m // tm,),
        in_specs=[pl.BlockSpec((tm, k), lambda i: (i, 0)), b_spec],
        out_specs=[pl.BlockSpec((tm, MEM_HEADS, LANES), lambda i: (i, 0, 0)), pl.BlockSpec((tm, n), lambda i: (i, 0))],
        out_shape=[jax.ShapeDtypeStruct((m, MEM_HEADS, LANES), F32), jax.ShapeDtypeStruct((m, n), BF16)],
        compiler_params=_params(("parallel",)),
        name="matmul_heads",
    )(a, b)


def _project_heads_body(ap_ref, as_ref, bt_ref, o32p, o16p, o32s, o16s, wb):
    def heads(a, o32, o16):
        acc = _dot_nt(a, wb[...])
        for h in range(MEM_HEADS):
            o32[:, h, :] = acc[:, h * LANES:(h + 1) * LANES]
        o16[...] = acc.astype(BF16)

    @pl.when(pl.program_id(0) == 0)
    def _():
        wb[...] = bt_ref[...].astype(BF16)
        heads(as_ref[...], o32s, o16s)

    heads(ap_ref[...], o32p, o16p)


def _project_heads(ap, as_, bt, first, tm=1024):
    mp, k = ap.shape
    ms = as_.shape[0]
    n = MEM_HEADS * LANES
    assert mp % tm == 0 and first % n == 0, (ap.shape, bt.shape, first)
    j0 = first // n

    return pl.pallas_call(
        _project_heads_body,
        grid=(mp // tm,),
        in_specs=[pl.BlockSpec((tm, k), lambda i: (i, 0)), pl.BlockSpec((ms, k), lambda i: (0, 0)),
                  pl.BlockSpec((n, k), lambda i: (j0, 0))],
        out_specs=[pl.BlockSpec((tm, MEM_HEADS, LANES), lambda i: (i, 0, 0)), pl.BlockSpec((tm, n), lambda i: (i, 0)),
                   pl.BlockSpec((ms, MEM_HEADS, LANES), lambda i: (0, 0, 0)), pl.BlockSpec((ms, n), lambda i: (0, 0))],
        out_shape=[jax.ShapeDtypeStruct((mp, MEM_HEADS, LANES), F32), jax.ShapeDtypeStruct((mp, n), BF16),
                   jax.ShapeDtypeStruct((ms, MEM_HEADS, LANES), F32), jax.ShapeDtypeStruct((ms, n), BF16)],
        scratch_shapes=[pltpu.VMEM((n, k), BF16)],
        compiler_params=_params(("arbitrary",)),
        name="project_heads",
    )(ap, as_, bt)


def _outproj_body(*refs, widths):
    xs = refs[:len(widths)]
    w_ref, r_ref, o_ref = refs[len(widths):]
    acc = r_ref[...]
    off = 0
    for x_ref, w in zip(xs, widths):
        acc = acc + _dot(x_ref[...], w_ref[off:off + w, :])
        off += w
    o_ref[...] = acc


def _outproj(xs, w_bf16, resid, tm=1024, tn=512):
    m = resid.shape[0]
    n = w_bf16.shape[1]
    widths = tuple(x.shape[1] for x in xs)
    assert sum(widths) == w_bf16.shape[0]
    tm = min(tm, m)
    in_specs = [pl.BlockSpec((tm, w), lambda i, j: (i, 0)) for w in widths]
    in_specs += [pl.BlockSpec((w_bf16.shape[0], tn), lambda i, j: (0, j)), pl.BlockSpec((tm, tn), lambda i, j: (i, j))]
    return pl.pallas_call(
        functools.partial(_outproj_body, widths=widths),
        grid=(m // tm, n // tn),
        in_specs=in_specs,
        out_specs=pl.BlockSpec((tm, tn), lambda i, j: (i, j)),
        out_shape=jax.ShapeDtypeStruct((m, n), F32),
        compiler_params=_params(("parallel", "parallel")),
        name="outproj",
    )(*xs, w_bf16, resid)


def _hgrn_body(qa_ref, fa_ref, ia_ref, za_ref, lb_ref, gn_ref, s0_ref, o_ref, s_out, s_scr, *, L, valid):
    c = pl.program_id(2)

    @pl.when(c == 0)
    def _():
        s_scr[...] = s0_ref[...]

    lb = lb_ref[...]
    sig = jax.nn.sigmoid(fa_ref[...])
    logf = jnp.log(lb + (1.0 - lb) * sig)
    kk = (1.0 - lb) * (1.0 - sig)
    if valid < L:
        live = _iota2((L, 1), 0) < valid
        logf = jnp.where(live, logf, 0.0)
        kk = jnp.where(live, kk, 0.0)
    tri_b = (_iota2((L, L), 0) >= _iota2((L, L), 1)).astype(BF16)
    bc = _cumsum_rows(logf, tri_b)
    q = _silu(qa_ref[...])
    gate = _silu(za_ref[...])
    v = ia_ref[...]
    gn = gn_ref[...]
    nsub = L // HG_SUB
    rr = _iota2((L, nsub * L), 0)
    cc = _iota2((L, nsub * L), 1)
    keep = ((jnp.right_shift(cc, L.bit_length() - 1) == jnp.right_shift(rr, HG_SUB.bit_length() - 1))
            & (jnp.bitwise_and(cc, L - 1) <= rr))
    for j in range(HG_HB):
        sl = slice(j * HG_DK, (j + 1) * HG_DK)
        bj, qj, kj = bc[:, sl], q[:, sl], kk[:, sl]
        vb = v[:, sl].astype(BF16)
        s_prev = s_scr[j]
        inter = _dot((qj * jnp.exp(bj)).astype(BF16), s_prev.astype(BF16))
        mids = [bj[i * HG_SUB + HG_SUB // 2:i * HG_SUB + HG_SUB // 2 + 1, :] for i in range(nsub)]
        mid_rows = jnp.concatenate([jnp.broadcast_to(m, (HG_SUB, HG_DK)) for m in mids], axis=0)
        q_dec = qj * jnp.exp(jnp.minimum(bj - mid_rows, EXP_CLAMP))
        k_dec = jnp.concatenate([kj * jnp.exp(jnp.minimum(m - bj, EXP_CLAMP)) for m in mids], axis=0)
        att = jnp.where(keep, _dot_nt(q_dec.astype(BF16), k_dec.astype(BF16)), 0.0)
        o = inter + _dot(att.astype(BF16), jnp.concatenate([vb] * nsub, axis=0))
        o_n = o * lax.rsqrt(jnp.mean(o * o, axis=-1, keepdims=True) + EPS) * gn
        o_ref[:, sl] = (o_n * gate[:, sl]).astype(o_ref.dtype)
        bl = bj[L - 1:L, :]
        kd = kj * jnp.exp(bl - bj)
        s_scr[j] = _row_to_col(jnp.exp(bl), HG_DK) * s_prev + _dot_tn(kd.astype(BF16), vb)

    @pl.when(c == pl.num_programs(2) - 1)
    def _():
        s_out[...] = s_scr[...]


def _hgrn_call(y, s0, lb, gn, *, B, T, L, valid):
    nc = T // L
    w = HG_HB * HG_DK

    def col(name):
        blk = EVEN_A[name] // w
        return pl.BlockSpec((L, w), lambda b, hg, c: (b * nc + c, blk + hg))

    state_spec = pl.BlockSpec((None, HG_HB, HG_DK, HG_DV), lambda b, hg, c: (b, hg, 0, 0))
    return pl.pallas_call(
        functools.partial(_hgrn_body, L=L, valid=valid),
        grid=(B, HG_HEADS // HG_HB, nc),
        in_specs=[col("qa"), col("fa"), col("ia"), col("za"),
                  pl.BlockSpec((1, w), lambda b, hg, c: (0, hg)),
                  pl.BlockSpec((1, HG_DV), lambda b, hg, c: (0, 0)),
                  state_spec],
        out_specs=[pl.BlockSpec((L, w), lambda b, hg, c: (b * nc + c, hg)), state_spec],
        out_shape=[jax.ShapeDtypeStruct((B * T, HG_W), BF16),
                   jax.ShapeDtypeStruct((B, HG_HEADS, HG_DK, HG_DV), F32)],
        scratch_shapes=[pltpu.VMEM((HG_HB, HG_DK, HG_DV), F32)],
        compiler_params=_params(("arbitrary", "arbitrary", "arbitrary")),
        name="hgrn2",
    )(y, y, y, y, lb.reshape(1, HG_W), gn.reshape(1, HG_DV), s0)


def _mlstm_body(q_ref, k_ref, v_ref, og_ref, z_ref, g_ref, bif_ref, gn_ref, c0_ref, n0_ref, m0_ref,
                h_ref, c_out, n_out, m_out, c_scr, n_scr, m_scr, *, L, valid):
    c = pl.program_id(2)

    @pl.when(c == 0)
    def _():
        c_scr[...] = c0_ref[...]
        n_scr[...] = n0_ref[...]
        m_scr[...] = m0_ref[...]

    gates = g_ref[...] + bif_ref[...]
    log_i = gates
    log_f = jnp.minimum(gates, 0.0) - jnp.log(1.0 + jnp.exp(-jnp.abs(gates)))
    if valid < L:
        live = _iota2((L, 1), 0) < valid
        log_i = jnp.where(live, log_i, -1e30)
        log_f = jnp.where(live, log_f, 0.0)
    tri = _iota2((L, L), 0) >= _iota2((L, L), 1)
    bcs = _cumsum_rows(log_f, tri.astype(BF16))
    for j in range(ML_HB):
        head = pl.program_id(1) * ML_HB + j
        b_col = _lane_col(bcs, ML_HEADS + head)
        i_col = _lane_col(log_i, head)
        b_row = _col_to_row(b_col, L)
        i_row = _col_to_row(i_col, L)
        m_prev = m_scr[:, j:j + 1]
        dmat = jnp.where(tri, b_col - b_row + i_row, NEG_INF)
        inter = b_col + m_prev
        mt = jnp.maximum(inter, jnp.max(dmat, axis=1, keepdims=True))
        w_in = jnp.exp(dmat - mt)
        w_x = jnp.exp(inter - mt)
        qj = q_ref[:, j * ML_DK:(j + 1) * ML_DK]
        kj = k_ref[:, j * ML_DK:(j + 1) * ML_DK] * (ML_DK ** -0.5)
        vj = v_ref[:, j * ML_DV:(j + 1) * ML_DV]
        qb, kb = qj.astype(BF16), kj.astype(BF16)
        sw = _dot_nt(qb, kb) * w_in
        c_prev = c_scr[j]
        n_prev = n_scr[:, j * ML_DK:(j + 1) * ML_DK]
        num = w_x * _dot_nt(qb, c_prev.astype(BF16)) + _dot(sw.astype(BF16), vj.astype(BF16))
        den = w_x * jnp.sum(qj * n_prev, axis=1, keepdims=True) + jnp.sum(sw, axis=1, keepdims=True)
        h = num / jnp.maximum(jnp.abs(den), jnp.exp(-mt))
        m_last = mt[L - 1:L, :]
        b_last = b_col[L - 1:L, :]
        w_end = jnp.exp(b_last - b_col + i_col - m_last)
        d_c = jnp.exp(b_last + m_prev - m_last)
        c_scr[j] = d_c * c_prev + _dot_tn((w_end * vj).astype(BF16), kb)
        n_scr[:, j * ML_DK:(j + 1) * ML_DK] = d_c * n_prev + jnp.sum(w_end * kj, axis=0, keepdims=True)
        m_scr[:, j:j + 1] = m_last
        sv = slice(j * ML_DV, (j + 1) * ML_DV)
        h_n = h * lax.rsqrt(jnp.mean(h * h, axis=-1, keepdims=True) + EPS) * gn_ref[:, sv]
        h_ref[:, sv] = (h_n * jax.nn.sigmoid(og_ref[:, sv]) * _silu(z_ref[:, sv])).astype(h_ref.dtype)

    @pl.when(c == pl.num_programs(2) - 1)
    def _():
        c_out[...] = c_scr[...]
        n_out[...] = n_scr[...]
        m_out[...] = m_scr[...]


def _mlstm_call(ya, yb, c0, n0, m0, bif_r, gn, *, B, T, L, valid):
    nc = T // L
    ng = ML_HEADS // ML_HB
    wk, wv = ML_HB * ML_DK, ML_HB * ML_DV

    def col(name, w):
        blk = (ODD_A[name] if name in ODD_A else ODD_B[name]) // w
        return pl.BlockSpec((L, w), lambda b, hg, c: (b * nc + c, blk + hg))

    c_spec = pl.BlockSpec((None, ML_HB, ML_DV, ML_DK), lambda b, hg, c: (b, hg, 0, 0))
    n_spec = pl.BlockSpec((None, 1, wk), lambda b, hg, c: (b, 0, hg))
    m_spec = pl.BlockSpec((None, None, 1, LANES), lambda b, hg, c: (b, hg, 0, 0))
    m0_r = jnp.pad(m0.reshape(B, ng, 1, ML_HB), ((0, 0), (0, 0), (0, 0), (0, LANES - ML_HB)))
    h, c_new, n_new, m_new = pl.pallas_call(
        functools.partial(_mlstm_body, L=L, valid=valid),
        grid=(B, ng, nc),
        in_specs=[col("q", wk), col("k", wk), col("v", wv), col("og", wv), col("z", wv),
                  pl.BlockSpec((L, LANES), lambda b, hg, c: (b * nc + c, ODD_B["gates"] // LANES)),
                  pl.BlockSpec((1, LANES), lambda b, hg, c: (0, 0)),
                  pl.BlockSpec((1, wv), lambda b, hg, c: (0, hg)),
                  c_spec, n_spec, m_spec],
        out_specs=[pl.BlockSpec((L, wv), lambda b, hg, c: (b * nc + c, hg)), c_spec, n_spec, m_spec],
        out_shape=[jax.ShapeDtypeStruct((B * T, ML_V_W), BF16),
                   jax.ShapeDtypeStruct((B, ML_HEADS, ML_DV, ML_DK), F32),
                   jax.ShapeDtypeStruct((B, 1, ML_QK_W), F32),
                   jax.ShapeDtypeStruct((B, ng, 1, LANES), F32)],
        scratch_shapes=[pltpu.VMEM((ML_HB, ML_DV, ML_DK), F32), pltpu.VMEM((1, wk), F32), pltpu.VMEM((1, LANES), F32)],
        compiler_params=_params(("arbitrary", "arbitrary", "arbitrary")),
        name="mlstm",
    )(ya, ya, ya, ya, yb, yb, bif_r, gn.reshape(1, ML_V_W), c0, n0.reshape(B, 1, ML_QK_W), m0_r)
    return h, c_new, n_new.reshape(B, ML_HEADS, ML_DK), m_new[:, :, 0, :ML_HB].reshape(B, ML_HEADS)


def _mem_body(q_ref, k_ref, v_ref, o_ref):
    q = q_ref[...] * (MEM_HD ** -0.5)
    for h in range(MEM_HEADS):
        sl = slice(h * MEM_HD, (h + 1) * MEM_HD)
        s = _dot_nt(q[:, sl].astype(BF16), k_ref[:, sl].astype(BF16))
        p = jnp.exp(s - jnp.max(s, axis=-1, keepdims=True))
        o = _dot(p.astype(BF16), v_ref[:, sl].astype(BF16)) / jnp.sum(p, axis=-1, keepdims=True)
        o_ref[:, sl] = o.astype(o_ref.dtype)


def _mem_call(y, q_off, k2d, v2d, *, B, T, tq=256):
    tq = min(tq, T)
    nq = T // tq
    qb = q_off // MEM_W
    return pl.pallas_call(
        _mem_body,
        grid=(B, nq),
        in_specs=[pl.BlockSpec((tq, MEM_W), lambda b, i: (b * nq + i, qb)),
                  pl.BlockSpec((N_MEM, MEM_W), lambda b, i: (b, 0)),
                  pl.BlockSpec((N_MEM, MEM_W), lambda b, i: (b, 0))],
        out_specs=pl.BlockSpec((tq, MEM_W), lambda b, i: (b * nq + i, 0)),
        out_shape=jax.ShapeDtypeStruct((B * T, MEM_W), BF16),
        compiler_params=_params(("parallel", "parallel")),
        name="mem_attn",
    )(y, k2d, v2d)


def _gelu_tanh(x):
    return 0.5 * x * (1.0 + jnp.tanh(math.sqrt(2.0 / math.pi) * (x + 0.044715 * (x * x * x))))


def _compress_body(x_ref, w1_ref, b1_ref, w2_ref, pe_ref, o_ref, x32, *, nch):
    x32[...] = x_ref[...].astype(F32)
    a = jnp.zeros((nch, NSA_HD), F32)
    b = jnp.zeros((nch, NSA_HD), F32)
    for s in range(CMP_STRIDE):
        r = x32[pl.ds(s, nch, stride=CMP_STRIDE), :]
        a = a + _dot((r + pe_ref[s:s + 1, :]).astype(BF16), w1_ref[s])
        b = b + _dot((r + pe_ref[CMP_STRIDE + s:CMP_STRIDE + s + 1, :]).astype(BF16), w1_ref[CMP_STRIDE + s])
    h = a + pltpu.roll(b, nch - 1, 0) + b1_ref[...]
    o_ref[...] = _dot(_gelu_tanh(h).astype(BF16), w2_ref[...])


def _compress_call(x16, w1, b1, w2, pe, *, B, T):
    nch = T // CMP_STRIDE
    return pl.pallas_call(
        functools.partial(_compress_body, nch=nch),
        grid=(B, NSA_KVH),
        in_specs=[pl.BlockSpec((T, NSA_HD), lambda b, h: (b, h)),
                  pl.BlockSpec((CMP_BLOCK, NSA_HD, NSA_HD), lambda b, h: (0, 0, 0)),
                  pl.BlockSpec((1, NSA_HD), lambda b, h: (0, 0)),
                  pl.BlockSpec((NSA_HD, NSA_HD), lambda b, h: (0, 0)),
                  pl.BlockSpec((CMP_BLOCK, NSA_HD), lambda b, h: (0, 0))],
        out_specs=pl.BlockSpec((None, None, nch, NSA_HD), lambda b, h: (b, h, 0, 0)),
        out_shape=jax.ShapeDtypeStruct((B, NSA_KVH, nch, NSA_HD), F32),
        scratch_shapes=[pltpu.VMEM((T, NSA_HD), F32)],
        compiler_params=_params(("parallel", "parallel")),
        name="nsa_compress",
    )(x16, w1.astype(BF16), b1.reshape(1, NSA_HD), w2.astype(BF16), pe)


def _softmax_rows(s):
    m = jnp.max(s, axis=-1, keepdims=True)
    m = jnp.where(m == NEG_INF, 0.0, m)
    p = jnp.exp(s - m)
    return p, jnp.sum(p, axis=-1, keepdims=True)


def _slc_scores(psum, width, n_slc):
    ncmp = psum.shape[1]
    d = _iota2((ncmp, width), 0) - (SEL_BLOCK // CMP_STRIDE) * _iota2((ncmp, width), 1)
    wgt = jnp.where((d == -1) | (d == 3), 1.0, jnp.where((d >= 0) & (d <= 2), 2.0, 0.0))
    wgt = jnp.where(_iota2((ncmp, width), 1) < n_slc, wgt, 0.0).astype(BF16)
    p_hi = psum.astype(BF16)
    p_lo = (psum - p_hi.astype(F32)).astype(BF16)
    return _dot(p_hi, wgt) + _dot(p_lo, wgt)


def _top_blocks(slc, cur, n_pick):
    rows, width = slc.shape
    blk = _iota2((rows, width), 1)
    forced = (blk == 0) | (blk == cur) | (blk == cur - 1)
    score = jnp.where(forced, jnp.inf, slc)
    score = jnp.where(blk > cur, NEG_INF, score)
    blk_f = blk.astype(F32)
    lane = _iota2((rows, LANES), 1)
    sel = jnp.zeros((rows, width), F32)
    picks = jnp.zeros((rows, LANES), F32)
    for i in range(n_pick):
        mx = jnp.max(score, axis=-1, keepdims=True)
        first = jnp.min(jnp.where(score == mx, blk_f, float(width)), axis=-1, keepdims=True)
        pick = blk_f == first
        sel = jnp.where(pick, 1.0, sel)
        picks = jnp.where(lane == i, first, picks)
        score = jnp.where(pick, NEG_INF, score)
    return sel, picks


def _member_by_rank(psum, tpos_row, n_slc, n_pick):
    nq, ncmp = psum.shape
    nb = -(-n_slc // 8) * 8
    d = _iota2((nb, ncmp), 1) - (SEL_BLOCK // CMP_STRIDE) * _iota2((nb, ncmp), 0)
    wgt = jnp.where((d == -1) | (d == 3), 1.0, jnp.where((d >= 0) & (d <= 2), 2.0, 0.0))
    wgt = jnp.where(_iota2((nb, ncmp), 0) < n_slc, wgt, 0.0).astype(BF16)
    p_hi = psum.astype(BF16)
    p_lo = (psum - p_hi.astype(F32)).astype(BF16)
    slc = _dot_nt(wgt, p_hi) + _dot_nt(wgt, p_lo)
    blk = _iota2((nb, nq), 0)
    cur = jnp.right_shift(tpos_row, SEL_SHIFT)
    forced = (blk == 0) | (blk == cur) | (blk == cur - 1)
    score = jnp.where(forced, jnp.inf, slc)
    score = jnp.where(blk > cur, NEG_INF, score)
    ahead = jnp.zeros((nb, nq), F32)
    for i in range(n_slc):
        s_i = score[i:i + 1, :]
        ahead = ahead + jnp.where((s_i > score) | ((s_i == score) & (blk > i)), 1.0, 0.0)
    return jnp.where((ahead < n_pick) & (blk <= cur), 1.0, 0.0)


NEAR_COLS = Q_BLOCK + REL_MAX_DIST


def _banded_attention(q, k_ref, v_ref, start, width, mask, near_bias):
    far = width - NEAR_COLS
    s_far = _dot_nt(q, k_ref[pl.ds(start, far), :]) + mask[:, :far]
    s_near = _dot_nt(q, k_ref[pl.ds(start + far, NEAR_COLS), :]) + near_bias + mask[:, far:]
    m = jnp.maximum(jnp.max(s_far, axis=-1, keepdims=True), jnp.max(s_near, axis=-1, keepdims=True))
    m = jnp.where(m == NEG_INF, 0.0, m)
    p_far, p_near = jnp.exp(s_far - m), jnp.exp(s_near - m)
    l = jnp.sum(p_far, axis=-1, keepdims=True) + jnp.sum(p_near, axis=-1, keepdims=True)
    o = (_dot(p_far.astype(BF16), v_ref[pl.ds(start, far), :])
         + _dot(p_near.astype(BF16), v_ref[pl.ds(start + far, NEAR_COLS), :]))
    return o / jnp.maximum(l, TINY)


def _nsa_prompt_body(q_ref, zb_ref, gb_ref, bg_ref, ks_ref, vs_ref, kw_ref, vw_ref, kc_ref, vc_ref,
                     bc_ref, bn_ref, o_ref, ksp, vsp, kwp, vwp, osel, *, T):
    qi = pl.program_id(2)
    tq = Q_BLOCK
    front = T - tq
    wlen = WINDOW + tq
    n_slc = T // SEL_BLOCK

    @pl.when(qi == 0)
    def _():
        ksp[0:front, :] = jnp.zeros((front, NSA_HD), BF16)
        vsp[0:front, :] = jnp.zeros((front, NSA_HD), BF16)
        ksp[front:front + T, :] = ks_ref[...].astype(BF16)
        vsp[front:front + T, :] = vs_ref[...].astype(BF16)
        kwp[0:WINDOW, :] = jnp.zeros((WINDOW, NSA_HD), BF16)
        vwp[0:WINDOW, :] = jnp.zeros((WINDOW, NSA_HD), BF16)
        kwp[WINDOW:WINDOW + T, :] = kw_ref[...].astype(BF16)
        vwp[WINDOW:WINDOW + T, :] = vw_ref[...].astype(BF16)

    t0 = pl.multiple_of(qi * tq, tq)
    tpos = _iota2((tq, 1), 0) + t0
    q_all = q_ref[...] * (NSA_HD ** -0.5)
    q = jnp.concatenate([q_all[:, g * NSA_HD:(g + 1) * NSA_HD] for g in range(NSA_G)], axis=0).astype(BF16)
    bias_near = bn_ref[...].reshape(NSA_G * tq, NEAR_COLS)

    def per_head(a):
        return jnp.concatenate([a] * NSA_G, axis=0)

    ncmp = T // CMP_STRIDE
    vis = tpos >= _iota2((1, ncmp), 1) * CMP_STRIDE + (CMP_BLOCK - 1)
    s = _dot_nt(q, kc_ref[...].astype(BF16)) + bc_ref[...].reshape(NSA_G * tq, ncmp)
    p, l = _softmax_rows(s + per_head(jnp.where(vis, 0.0, NEG_INF)))
    p = p / jnp.maximum(l, TINY)
    o_cmp = _dot(p.astype(BF16), vc_ref[...].astype(BF16))
    psum = p[0:tq]
    for g in range(1, NSA_G):
        psum = psum + p[g * tq:(g + 1) * tq]

    member_t = _member_by_rank(psum, _iota2((1, tq), 1) + t0, n_slc, min(N_SEL, n_slc)).astype(BF16)

    nb = member_t.shape[0]
    n_win = SEL_WINDOWS if T % (SEL_WINDOWS * tq) == 0 else 1
    for i in range(n_win):
        w_prev, w = T * i // n_win, T * (i + 1) // n_win

        @pl.when((qi >= w_prev // tq) & (qi < w // tq))
        def _(w=w):
            off = T - w
            col_blk = (jnp.right_shift(_iota2((nb, w), 1) + off, SEL_SHIFT)
                       + (qi * (tq // SEL_BLOCK) + (tq - T) // SEL_BLOCK))
            expand = (col_blk == _iota2((nb, w), 0)).astype(BF16)
            kpos = _iota2((1, w), 1) + (t0 + tq - w)
            allowed = (_dot_tn(member_t, expand) > 0.5) & (kpos <= tpos)
            mask_s = per_head(jnp.where(allowed, 0.0, NEG_INF))
            osel[...] = _banded_attention(q, ksp, vsp, t0 + off, w, mask_s, bias_near)

    dist = WINDOW + _iota2((tq, wlen), 0) - _iota2((tq, wlen), 1)
    in_win = (dist >= 0) & (dist < WINDOW) & (_iota2((1, wlen), 1) + (t0 - WINDOW) >= 0)
    o_win = _banded_attention(q, kwp, vwp, t0, wlen, per_head(jnp.where(in_win, 0.0, NEG_INF)), bias_near)
    gate = jax.nn.sigmoid(gb_ref[...] + bg_ref[...])
    zb = _silu(zb_ref[...])
    for g in range(NSA_G):
        head = pl.program_id(1) * NSA_G + g
        r = slice(g * tq, (g + 1) * tq)
        mix = (_lane_col(gate, head) * o_cmp[r] + _lane_col(gate, NSA_HEADS + head) * osel[r, :]
               + _lane_col(gate, 2 * NSA_HEADS + head) * o_win[r])
        sl = slice(g * NSA_HD, (g + 1) * NSA_HD)
        o_ref[:, sl] = (mix * zb[:, sl]).astype(o_ref.dtype)


def _nsa_prompt_call(ya, yb, kv16, kcmp, vcmp, bg_r, bias_c, bias_near, *, B, T):
    nq = T // Q_BLOCK
    gw = NSA_G * NSA_HD
    kv_spec = pl.BlockSpec((T, NSA_HD), lambda b, h, i: (b, h))
    cmp_spec = pl.BlockSpec((None, None, T // CMP_STRIDE, NSA_HD), lambda b, h, i: (b, h, 0, 0))
    return pl.pallas_call(
        functools.partial(_nsa_prompt_body, T=T),
        grid=(B, NSA_KVH, nq),
        in_specs=[pl.BlockSpec((Q_BLOCK, gw), lambda b, h, i: (b * nq + i, EVEN_A["qb"] // gw + h)),
                  pl.BlockSpec((Q_BLOCK, gw), lambda b, h, i: (b * nq + i, EVEN_B["zb"] // gw + h)),
                  pl.BlockSpec((Q_BLOCK, LANES), lambda b, h, i: (b * nq + i, EVEN_B["gb"] // LANES)),
                  pl.BlockSpec((1, LANES), lambda b, h, i: (0, 0)),
                  kv_spec, kv_spec, kv_spec, kv_spec, cmp_spec, cmp_spec,
                  pl.BlockSpec((None, NSA_G, Q_BLOCK, T // CMP_STRIDE), lambda b, h, i: (h, 0, i, 0)),
                  pl.BlockSpec((None, NSA_G, Q_BLOCK, NEAR_COLS), lambda b, h, i: (h, 0, 0, 0))],
        out_specs=pl.BlockSpec((Q_BLOCK, gw), lambda b, h, i: (b * nq + i, h)),
        out_shape=jax.ShapeDtypeStruct((B * T, NSA_W), BF16),
        scratch_shapes=[pltpu.VMEM((2 * T - Q_BLOCK, NSA_HD), BF16), pltpu.VMEM((2 * T - Q_BLOCK, NSA_HD), BF16),
                        pltpu.VMEM((WINDOW + T, NSA_HD), BF16), pltpu.VMEM((WINDOW + T, NSA_HD), BF16),
                        pltpu.VMEM((NSA_G * Q_BLOCK, NSA_HD), F32)],
        compiler_params=_params(("arbitrary", "arbitrary", "arbitrary")),
        name="nsa_prompt",
    )(ya, yb, yb, bg_r, *kv16, kcmp, vcmp, bias_c, bias_near)


CMP_PAGES = 16
CHUNKS_PER_PAGE = PAGE_SIZE // CMP_STRIDE
PAGE_ROWS = PAGE_SIZE * NSA_KVH


def _pool_rows(pool):
    return pool.reshape(pool.shape[0] * PAGE_ROWS, NSA_HD)


def _cmp_pages_body(pt_ref, *refs):
    del pt_ref
    pages = refs[:CMP_PAGES]
    w_ref, pe_ref, o_ref = refs[CMP_PAGES:]
    rows = CMP_PAGES * CHUNKS_PER_PAGE
    per_head = [jnp.concatenate(
        [jnp.concatenate([pg[pl.ds(NSA_KVH * s + h, CHUNKS_PER_PAGE, stride=CMP_STRIDE * NSA_KVH), :]
                          for s in range(CMP_STRIDE)], axis=1) for pg in pages], axis=0) for h in range(NSA_KVH)]
    w = w_ref[...]
    r = _dot(jnp.concatenate(per_head, axis=0).astype(BF16), w)
    pc = _dot(pe_ref[...], w)
    r = r + jnp.concatenate([pc[0:1, :NSA_HD], pc[1:2, NSA_HD:]], axis=1)
    for h in range(NSA_KVH):
        o_ref[h] = r[h * rows:(h + 1) * rows]


def _cmp_pages_call(pool, page_table, w1, pe, *, B):
    n_pages = page_table.shape[1]
    rows = CMP_PAGES * CHUNKS_PER_PAGE
    view = _pool_rows(pool)
    w = w1.reshape(2, CMP_STRIDE, NSA_HD, NSA_HD).transpose(1, 2, 0, 3).reshape(CMP_STRIDE * NSA_HD, 2 * NSA_HD)
    pe_rows = jnp.pad(pe.reshape(2, CMP_STRIDE * NSA_HD), ((0, 6), (0, 0))).astype(BF16)

    def page_spec(i):
        return pl.BlockSpec((PAGE_ROWS, NSA_HD), lambda b, s, pt: (pt[b * n_pages + s * CMP_PAGES + i], 0))

    grid_spec = pltpu.PrefetchScalarGridSpec(
        num_scalar_prefetch=1,
        grid=(B, n_pages // CMP_PAGES),
        in_specs=[page_spec(i) for i in range(CMP_PAGES)]
        + [pl.BlockSpec((CMP_STRIDE * NSA_HD, 2 * NSA_HD), lambda b, s, pt: (0, 0)),
           pl.BlockSpec((8, CMP_STRIDE * NSA_HD), lambda b, s, pt: (0, 0))],
        out_specs=pl.BlockSpec((None, NSA_KVH, rows, 2 * NSA_HD), lambda b, s, pt: (b, 0, s, 0)),
    )
    return pl.pallas_call(
        _cmp_pages_body,
        grid_spec=grid_spec,
        out_shape=jax.ShapeDtypeStruct((B, NSA_KVH, n_pages * CHUNKS_PER_PAGE, 2 * NSA_HD), F32),
        compiler_params=_params(("arbitrary", "arbitrary")),
        name="nsa_cmp_pages",
    )(page_table.reshape(-1), *([view] * CMP_PAGES), w.astype(BF16), pe_rows)


SEL_WINDOWS = 4
SLC_LANES = 384


def _sample_q_rows(q_ref):
    q = q_ref[...] * (NSA_HD ** -0.5)
    return jnp.concatenate([q[:, g * NSA_HD:(g + 1) * NSA_HD] for g in range(NSA_G)], axis=0).astype(BF16)


def _nsa_sample_main_body(abk_ref, abv_ref, b1_ref, w2_ref, q_ref, wk_ref, wv_ref, kn_ref, vn_ref, bc_ref, bw_ref,
                          ocmp_ref, owin_ref, idx_ref, *, T, n_slc):
    tp = SAMPLE_PAD_T
    rows = NSA_G * tp
    ncmp = abk_ref.shape[0]

    def compressed(ab_ref, t):
        ab = ab_ref[...]
        h = ab[:, :NSA_HD] + pltpu.roll(ab[:, NSA_HD:], ncmp - 1, 0) + b1_ref[t]
        return _dot(_gelu_tanh(h).astype(BF16), w2_ref[t]).astype(BF16)

    kc, vc = compressed(abk_ref, 0), compressed(abv_ref, 1)
    q = _sample_q_rows(q_ref)
    step = jnp.bitwise_and(_iota2((rows, 1), 0), tp - 1)
    tpos = PAST_LEN + step
    vis = tpos >= _iota2((1, ncmp), 1) * CMP_STRIDE + (CMP_BLOCK - 1)
    p, l = _softmax_rows(jnp.where(vis, _dot_nt(q, kc) + bc_ref[...], NEG_INF))
    p = p / jnp.maximum(l, TINY)
    ocmp_ref[...] = _dot(p.astype(BF16), vc)
    psum = p[0:tp]
    for g in range(1, NSA_G):
        psum = psum + p[g * tp:(g + 1) * tp]
    cur = jnp.right_shift(PAST_LEN + _iota2((tp, 1), 0), SEL_SHIFT)
    _, picks = _top_blocks(_slc_scores(psum, SLC_LANES, n_slc), cur, N_SEL)
    idx_ref[...] = picks.astype(jnp.int32)

    wb = wk_ref.shape[0] // NSA_KVH
    wlen = bw_ref.shape[1]
    fill = jnp.zeros((wlen - wb - tp, NSA_HD), BF16)
    head = pl.program_id(1)
    k_all = jnp.concatenate([wk_ref[pl.ds(head, wb, stride=NSA_KVH), :].astype(BF16), kn_ref[...], fill], axis=0)
    v_all = jnp.concatenate([wv_ref[pl.ds(head, wb, stride=NSA_KVH), :].astype(BF16), vn_ref[...], fill], axis=0)
    col = _iota2((1, wlen), 1)
    dist = tpos - (PAST_LEN - wb + col)
    in_win = (dist >= 0) & (dist < WINDOW) & (col < wb + T)
    pw, lw = _softmax_rows(jnp.where(in_win, _dot_nt(q, k_all) + bw_ref[...], NEG_INF))
    owin_ref[...] = _dot(pw.astype(BF16), v_all) / jnp.maximum(lw, TINY)


def _nsa_sample_main_call(ya, kw16, vw16, abk, abv, b1, w2, wk, wv, bias_c, bias_w, *, B, T):
    tp = SAMPLE_PAD_T
    rows = NSA_G * tp
    gw = NSA_G * NSA_HD
    ncmp = abk.shape[2]
    wb = wk.shape[1]
    wlen = bias_w.shape[-1]
    n_slc = -(-(PAST_LEN + T) // SEL_BLOCK)
    assert n_slc <= SLC_LANES and T <= tp
    ab_spec = pl.BlockSpec((None, None, ncmp, 2 * NSA_HD), lambda b, h: (b, h, 0, 0))
    win_spec = pl.BlockSpec((wb * NSA_KVH, NSA_HD), lambda b, h: (b, 0))
    o_spec = pl.BlockSpec((None, None, rows, NSA_HD), lambda b, h: (b, h, 0, 0))
    return pl.pallas_call(
        functools.partial(_nsa_sample_main_body, T=T, n_slc=n_slc),
        grid=(B, NSA_KVH),
        in_specs=[ab_spec, ab_spec,
                  pl.BlockSpec((2, 1, NSA_HD), lambda b, h: (0, 0, 0)),
                  pl.BlockSpec((2, NSA_HD, NSA_HD), lambda b, h: (0, 0, 0)),
                  pl.BlockSpec((tp, gw), lambda b, h: (b, EVEN_A["qb"] // gw + h)),
                  win_spec, win_spec,
                  pl.BlockSpec((tp, NSA_HD), lambda b, h: (b, h)),
                  pl.BlockSpec((tp, NSA_HD), lambda b, h: (b, h)),
                  pl.BlockSpec((None, rows, ncmp), lambda b, h: (h, 0, 0)),
                  pl.BlockSpec((None, rows, wlen), lambda b, h: (h, 0, 0))],
        out_specs=[o_spec, o_spec, pl.BlockSpec((None, None, tp, LANES), lambda b, h: (b, h, 0, 0))],
        out_shape=[jax.ShapeDtypeStruct((B, NSA_KVH, rows, NSA_HD), F32),
                   jax.ShapeDtypeStruct((B, NSA_KVH, rows, NSA_HD), F32),
                   jax.ShapeDtypeStruct((B, NSA_KVH, tp, LANES), jnp.int32)],
        compiler_params=_params(("parallel", "parallel")),
        name="nsa_sample_main",
    )(abk, abv, b1.reshape(2, 1, NSA_HD), w2.astype(BF16), ya,
      wk.reshape(B * wb * NSA_KVH, NSA_HD), wv.reshape(B * wb * NSA_KVH, NSA_HD), kw16, vw16, bias_c, bias_w)


NEAR_BLOCKS = 3


def _nsa_sample_sel_body(idx_ref, pt_ref, q_ref, kn_ref, vn_ref, tbl_ref, ocmp_ref, owin_ref, gb_ref, bg_ref, zb_ref,
                         *refs, T):
    del pt_ref
    k_blocks = refs[:N_SEL]
    v_blocks = refs[N_SEL:2 * N_SEL]
    o_ref, osel = refs[2 * N_SEL:]
    tp = SAMPLE_PAD_T
    rows = NSA_G * tp
    b, h, t = pl.program_id(0), pl.program_id(1), pl.program_id(2)
    base = ((b * NSA_KVH + h) * T + t) * N_SEL
    first_new = PAST_LEN // SEL_BLOCK
    cur = jnp.right_shift(PAST_LEN + t, SEL_SHIFT)
    q = _sample_q_rows(q_ref)
    pad = jnp.zeros((SEL_BLOCK - tp, NSA_HD), BF16)
    k_new = jnp.concatenate([kn_ref[...], pad], axis=0)
    v_new = jnp.concatenate([vn_ref[...], pad], axis=0)
    lane = _iota2((1, LANES), 1)
    low = lane < SEL_BLOCK
    within = jnp.bitwise_and(lane, SEL_BLOCK - 1)
    ks, vs, bias, kpos = [], [], [], []
    for i in range(0, N_SEL, 2):
        pair_bias, pair_pos = [], []
        for j in (i, i + 1):
            blk = idx_ref[base + j]
            is_new = blk >= first_new
            ks.append(jnp.where(is_new, k_new, k_blocks[j][pl.ds(h, SEL_BLOCK, stride=NSA_KVH), :].astype(BF16)))
            vs.append(jnp.where(is_new, v_new, v_blocks[j][pl.ds(h, SEL_BLOCK, stride=NSA_KVH), :].astype(BF16)))
            pair_bias.append(tbl_ref[jnp.clip(blk - (first_new - NEAR_BLOCKS), 0, NEAR_BLOCKS)])
            pair_pos.append(jnp.where(blk <= cur, blk * SEL_BLOCK, PAST_LEN + SEL_BLOCK * LANES) + within)
        bias.append(jnp.where(low, pair_bias[0], pair_bias[1]))
        kpos.append(jnp.where(low, pair_pos[0], pair_pos[1]))
    k_all = jnp.concatenate(ks, axis=0)
    v_all = jnp.concatenate(vs, axis=0)
    step = jnp.bitwise_and(_iota2((rows, 1), 0), tp - 1)
    ok = jnp.concatenate(kpos, axis=1) <= PAST_LEN + step
    p, l = _softmax_rows(jnp.where(ok, _dot_nt(q, k_all) + jnp.concatenate(bias, axis=1), NEG_INF))
    o = _dot(p.astype(BF16), v_all) / jnp.maximum(l, TINY)

    @pl.when(t == 0)
    def _():
        osel[...] = jnp.zeros_like(osel)

    osel[...] = jnp.where(step == t, o, osel[...])

    @pl.when(t == T - 1)
    def _():
        gate = jax.nn.sigmoid(gb_ref[...] + bg_ref[...])
        zb = _silu(zb_ref[...])
        for g in range(NSA_G):
            r = slice(g * tp, (g + 1) * tp)
            head = h * NSA_G + g
            mix = (_lane_col(gate, head) * ocmp_ref[r, :] + _lane_col(gate, NSA_HEADS + head) * osel[r, :]
                   + _lane_col(gate, 2 * NSA_HEADS + head) * owin_ref[r, :])
            sl = slice(g * NSA_HD, (g + 1) * NSA_HD)
            o_ref[:, sl] = (mix * zb[:, sl]).astype(o_ref.dtype)


def _nsa_sample_sel_call(ya, yb, ks16, vs16, idx, page_table, pool_k, pool_v, tbl, o_cmp, o_win, bg_r, *, B, T):
    tp = SAMPLE_PAD_T
    rows = NSA_G * tp
    gw = NSA_G * NSA_HD
    n_pages = page_table.shape[1]
    halves = PAGE_SIZE // SEL_BLOCK
    idx_flat = idx[:, :, :T, :N_SEL].reshape(-1)
    view_k, view_v = _pool_rows(pool_k), _pool_rows(pool_v)

    def blk_spec(j):
        def index(b, h, t, idx_s, pt_s):
            blk = idx_s[((b * NSA_KVH + h) * T + t) * N_SEL + j]
            page = pt_s[b * n_pages + jnp.minimum(blk // halves, n_pages - 1)]
            return (page * halves + blk % halves, 0)
        return pl.BlockSpec((SEL_BLOCK * NSA_KVH, NSA_HD), index)

    o_spec = pl.BlockSpec((None, None, rows, NSA_HD), lambda b, h, t, *_: (b, h, 0, 0))
    grid_spec = pltpu.PrefetchScalarGridSpec(
        num_scalar_prefetch=2,
        grid=(B, NSA_KVH, T),
        in_specs=[pl.BlockSpec((tp, gw), lambda b, h, t, *_: (b, EVEN_A["qb"] // gw + h)),
                  pl.BlockSpec((tp, NSA_HD), lambda b, h, t, *_: (b, h)),
                  pl.BlockSpec((tp, NSA_HD), lambda b, h, t, *_: (b, h)),
                  pl.BlockSpec((None, NEAR_BLOCKS + 1, rows, LANES), lambda b, h, t, *_: (h, 0, 0, 0)),
                  o_spec, o_spec,
                  pl.BlockSpec((tp, LANES), lambda b, h, t, *_: (b, EVEN_B["gb"] // LANES)),
                  pl.BlockSpec((1, LANES), lambda b, h, t, *_: (0, 0)),
                  pl.BlockSpec((tp, gw), lambda b, h, t, *_: (b, EVEN_B["zb"] // gw + h))]
        + [blk_spec(j) for j in range(N_SEL)] * 2,
        out_specs=pl.BlockSpec((tp, gw), lambda b, h, t, *_: (b, h)),
        scratch_shapes=[pltpu.VMEM((rows, NSA_HD), F32)],
    )
    return pl.pallas_call(
        functools.partial(_nsa_sample_sel_body, T=T),
        grid_spec=grid_spec,
        out_shape=jax.ShapeDtypeStruct((B * tp, NSA_W), BF16),
        compiler_params=_params(("arbitrary", "arbitrary", "arbitrary")),
        name="nsa_sample_sel",
    )(idx_flat, page_table.reshape(-1), ya, ks16, vs16, tbl, o_cmp, o_win, yb, bg_r, yb,
      *([view_k] * N_SEL), *([view_v] * N_SEL))


def _sample_bias_tables(rel_bias, T, wb):
    tp = SAMPLE_PAD_T
    ncmp = PAST_LEN // CMP_STRIDE
    wlen = -(-(wb + tp) // LANES) * LANES
    first = PAST_LEN // SEL_BLOCK - NEAR_BLOCKS
    assert PAST_LEN - ((first + 1) * SEL_BLOCK - 1) >= REL_MAX_DIST
    lo, hi = -wlen, PAST_LEN + tp
    rev = _bias_line(rel_bias, lo, hi, descending=True)

    def rows(tbl):
        return tbl.reshape(NSA_KVH, NSA_G * tp, tbl.shape[-1])

    t_c = _toeplitz(rev, hi - 1 - (PAST_LEN - (CMP_BLOCK - 1)), tp, CMP_STRIDE * ncmp)[:, :, ::CMP_STRIDE]
    t_w = _toeplitz(rev, hi - 1 - wb, tp, wlen)
    far = jnp.broadcast_to(rev[:, hi - 1 - REL_MAX_DIST][:, None, None], (NSA_HEADS, tp, LANES))
    near = []
    for k in range(1, NEAR_BLOCKS + 1):
        half = _toeplitz(rev, hi - 1 - (PAST_LEN - (first + k) * SEL_BLOCK), tp, SEL_BLOCK)
        near.append(jnp.concatenate([half, half], axis=-1))
    t_s = jnp.stack([far] + near, axis=1).reshape(NSA_KVH, NSA_G, NEAR_BLOCKS + 1, tp, LANES)
    t_s = t_s.transpose(0, 2, 1, 3, 4).reshape(NSA_KVH, NEAR_BLOCKS + 1, NSA_G * tp, LANES)
    return rows(t_c), rows(t_w), t_s


def _tail_even(w):
    return _tail_relayout(w, EVEN_KV_OFF + 6 * NSA_KV_W, 3 * NSA_HEADS, NSA_W + MEM_W, EVEN_B_N)


def _tail_odd(w):
    return _tail_relayout(w, ODD_A_N, 2 * ML_HEADS, ML_V_W + MEM_W, ODD_B_N)


def _gate_bias_even(b_gate):
    return jnp.pad(b_gate, (0, LANES - 3 * NSA_HEADS)).reshape(1, LANES)


def _gate_bias_odd(b_if):
    return jnp.pad(b_if.reshape(2 * ML_HEADS), (0, LANES - 2 * ML_HEADS)).reshape(1, LANES)


def _rel_bucket(dist):
    n = np.maximum(dist, 0)
    exact = REL_BUCKETS // 2
    nf = np.maximum(n, 1).astype(np.float32)
    large = exact + (np.log(nf / exact) / math.log(REL_MAX_DIST / exact) * (REL_BUCKETS - exact)).astype(np.int32)
    return np.where(n < exact, n, np.minimum(large, REL_BUCKETS - 1))


def _bias_line(rel_bias, lo, hi, descending=False):
    dist = np.arange(hi - 1, lo - 1, -1) if descending else np.arange(lo, hi)
    buckets = _rel_bucket(dist)
    edges = np.flatnonzero(np.diff(buckets)) + 1
    starts = np.concatenate([[0], edges])
    ends = np.concatenate([edges, [hi - lo]])
    bias_t = rel_bias.T.astype(F32)
    runs = [jnp.broadcast_to(bias_t[:, int(buckets[s])][:, None], (NSA_HEADS, int(e - s))) for s, e in zip(starts, ends)]
    return jnp.concatenate(runs, axis=1)


def _skew_rows(v, rows, step, cols):
    n = v.shape[1]
    reps = -(-rows * (n + step) // n)
    return jnp.tile(v, (1, reps))[:, :rows * (n + step)].reshape(v.shape[0], rows, n + step)[:, :, :cols]


def _toeplitz(rev, start, rows, cols):
    seg = rev[:, start - (rows - 1):start + cols]
    return _skew_rows(jnp.roll(seg, -(rows - 1), axis=1), rows, -1, cols)


def _prompt_bias_tables(rel_bias, T):
    ncmp = T // CMP_STRIDE
    assert Q_BLOCK + 1 >= REL_MAX_DIST
    lo, hi = -(CMP_STRIDE * ncmp + CMP_BLOCK), T
    line = _bias_line(rel_bias, lo, hi)
    rev = _bias_line(rel_bias, lo, hi, descending=True)

    def split(tbl):
        return tbl.reshape((NSA_KVH, NSA_G) + tbl.shape[1:])

    back = CMP_STRIDE * (ncmp - 1)
    first = -(back + CMP_BLOCK - 1) - lo
    seg = line[:, first:first + T + back]
    t_c = _skew_rows(jnp.roll(seg, -back, axis=1), ncmp, -CMP_STRIDE, T).swapaxes(1, 2)
    far = rev[:, hi - 1 - REL_MAX_DIST]
    t_near = _toeplitz(rev, hi - 1 - REL_MAX_DIST, Q_BLOCK, NEAR_COLS) - far[:, None, None]
    return split(t_c), split(t_near)


def _nsa_sample(ya, yb, kv16, page_table, pk_cmp, pv_cmp, pk_sel, pv_sel, wk, wv, bg_r, w1, b1, w2, pe, rel_bias,
                *, B, T):
    assert (PAST_LEN + T) // CMP_STRIDE == PAST_LEN // CMP_STRIDE
    abk = _cmp_pages_call(pk_cmp, page_table, w1[0], pe[0], B=B)
    abv = _cmp_pages_call(pv_cmp, page_table, w1[1], pe[1], B=B)
    bias_c, bias_w, tbl = _sample_bias_tables(rel_bias, T, wk.shape[1])
    o_cmp, o_win, idx = _nsa_sample_main_call(ya, kv16[4], kv16[5], abk, abv, b1, w2, wk, wv, bias_c, bias_w, B=B, T=T)
    return _nsa_sample_sel_call(ya, yb, kv16[2], kv16[3], idx, page_table, pk_sel, pv_sel, tbl, o_cmp, o_win, bg_r,
                                B=B, T=T)


def _kv_project(xp, xs, wt):
    outs = [_project_heads(xp, xs, wt, EVEN_KV_OFF + j * NSA_KV_W) for j in range(6)]
    return ([o[0] for o in outs], [o[1] for o in outs]), ([o[2] for o in outs], [o[3] for o in outs])


def _even_prompt(hp2d, ya, yb, kv32, kv16, mk16, mv16, bg_r, w1, b1, w2, pe, lb, g_norm, w_out, rel_bias, *, B, T):
    oa, s_new = _hgrn_call(ya, jnp.zeros((B, HG_HEADS, HG_DK, HG_DV), F32), lb, g_norm, B=B, T=T, L=CHUNK, valid=CHUNK)
    kcmp = _compress_call(kv16[0], w1[0], b1[0], w2[0], pe[0], B=B, T=T)
    vcmp = _compress_call(kv16[1], w1[1], b1[1], w2[1], pe[1], B=B, T=T)
    ob = _nsa_prompt_call(ya, yb, kv16[2:], kcmp, vcmp, bg_r, *_prompt_bias_tables(rel_bias, T), B=B, T=T)
    om = _mem_call(yb, EVEN_B["qm"], mk16, mv16, B=B, T=T)
    h_new = _outproj([oa, ob, om], w_out, hp2d)
    wb = min(WINDOW, T)
    rows = [r.reshape(B, T, NSA_KVH, NSA_HD) for r in kv32]
    return h_new, (rows[0], rows[1], rows[2], rows[3], rows[4][:, -wb:], rows[5][:, -wb:], s_new)


def _even_sample(hs2d, ya, yb, kv32, kv16, mk_s, mv_s, page_table, pk_cmp, pv_cmp, pk_sel, pv_sel, wk, wv, s0,
                 bg_r, w1, b1, w2, pe, lb, g_norm, w_out, rel_bias, *, B, T):
    tp = SAMPLE_PAD_T
    oa, s_new = _hgrn_call(ya, s0, lb, g_norm, B=B, T=tp, L=tp, valid=T)
    ob = _nsa_sample(ya, yb, kv16, page_table, pk_cmp, pv_cmp, pk_sel, pv_sel, wk, wv, bg_r, w1, b1, w2, pe, rel_bias,
                     B=B, T=T)
    om = _mem_call(yb, EVEN_B["qm"], mk_s.reshape(B * N_MEM, MEM_W), mv_s.reshape(B * N_MEM, MEM_W), B=B, T=tp)
    rows = [r.reshape(B, tp, NSA_KVH, NSA_HD)[:, :T] for r in kv32]
    wb = wk.shape[1]
    win_k = jnp.concatenate([wk, rows[4]], axis=1)[:, -wb:]
    win_v = jnp.concatenate([wv, rows[5]], axis=1)[:, -wb:]
    return _outproj([oa, ob, om], w_out, hs2d), (rows[0], rows[1], rows[2], rows[3], win_k, win_v, s_new)


def _odd_mix(h2d, ya, yb, k2d, v2d, c0, n0, m0, bif_r, g_norm, w_out, *, B, T, L, valid):
    h, c_new, n_new, m_new = _mlstm_call(ya, yb, c0, n0, m0, bif_r, g_norm, B=B, T=T, L=L, valid=valid)
    om = _mem_call(yb, ODD_B["qm"], k2d, v2d, B=B, T=T)
    return _outproj([h, om], w_out, h2d), (c_new, n_new, m_new)


def _stack(lst, i):
    return jnp.stack([t[i] for t in lst])


def kernel(x_prompt, x_sample, cache_mem_k, cache_mem_v, cache_cmp_k, cache_cmp_v, cache_sel_k, cache_sel_v,
           cache_win_k, cache_win_v, state_hgrn, state_mlstm_c, state_mlstm_n, state_mlstm_m, page_table,
           mem_prompt, norm_w, mem_norm_w, final_norm_w, rel_bias, w_mem_kv, w_in_even, b_nsa_gate,
           w_cmp1, b_cmp1, w_cmp2, pe_cmp, hgrn_lb_logits, hgrn_norm_w, w_out_even, w_in_odd, b_mlstm_if,
           mlstm_norm_w, w_out_odd):
    bp, tp = x_prompt.shape[:2]
    bs, ts = x_sample.shape[:2]
    tsp = SAMPLE_PAD_T
    lbs = jnp.cumsum(jax.nn.softmax(hgrn_lb_logits.astype(F32), axis=0), axis=0)
    hp = x_prompt.reshape(bp * tp, D_MODEL)
    hs = jnp.pad(x_sample, ((0, 0), (0, tsp - ts), (0, 0))).reshape(bs * tsp, D_MODEL)
    mem2d = mem_prompt.reshape(bp * N_MEM, D_MODEL)
    mem_new, even_p, even_s, odd_p, odd_s = [], [], [], [], []
    for l in range(DEPTH):
        npre = _rmsnorm_rows(hp, norm_w[l], BF16)
        nsam = _rmsnorm_rows(hs, norm_w[l], BF16)
        nmem = _rmsnorm_rows(mem2d, mem_norm_w[l], BF16)
        mk32, mk16 = _matmul_heads(nmem, w_mem_kv[l], first=0)
        mv32, mv16 = _matmul_heads(nmem, w_mem_kv[l], first=MEM_W)
        mem_new.append((mk32.reshape(bp, N_MEM, MEM_HEADS, MEM_HD), mv32.reshape(bp, N_MEM, MEM_HEADS, MEM_HD)))
        mk_s, mv_s = cache_mem_k[l], cache_mem_v[l]
        if l % 2 == 0:
            e = l // 2
            w_in = w_in_even[e].T
            ya_p, ya_s = _project(npre, nsam, w_in, rows=(0, EVEN_A_N))
            yb_p, yb_s = _project(npre, nsam, _tail_even(w_in))
            (kv32_p, kv16_p), (kv32_s, kv16_s) = _kv_project(npre, nsam, w_in)
            w_out = w_out_even[e].astype(BF16)
            bg_r = _gate_bias_even(b_nsa_gate[e])
            cmpw = (w_cmp1[e].reshape(2, CMP_BLOCK, NSA_HD, NSA_HD), b_cmp1[e], w_cmp2[e], pe_cmp[e])
            hp, st_p = _even_prompt(hp, ya_p, yb_p, kv32_p, kv16_p, mk16, mv16, bg_r, *cmpw, lbs[l], hgrn_norm_w[e],
                                    w_out, rel_bias, B=bp, T=tp)
            hs, st_s = _even_sample(hs, ya_s, yb_s, kv32_s, kv16_s, mk_s, mv_s, page_table, cache_cmp_k[e],
                                    cache_cmp_v[e], cache_sel_k[e], cache_sel_v[e], cache_win_k[e], cache_win_v[e],
                                    state_hgrn[e], bg_r, *cmpw, lbs[l], hgrn_norm_w[e], w_out, rel_bias, B=bs, T=ts)
            even_p.append(st_p)
            even_s.append(st_s)
        else:
            o = l // 2
            w_in = w_in_odd[o].T
            ya_p, ya_s = _project(npre, nsam, w_in, rows=(0, ODD_A_N))
            yb_p, yb_s = _project(npre, nsam, _tail_odd(w_in))
            w_out = w_out_odd[o].astype(BF16)
            bif_r = _gate_bias_odd(b_mlstm_if[o])
            hp, st_p = _odd_mix(hp, ya_p, yb_p, mk16, mv16, jnp.zeros((bp, ML_HEADS, ML_DV, ML_DK), F32),
                                jnp.zeros((bp, ML_HEADS, ML_DK), F32), jnp.zeros((bp, ML_HEADS), F32),
                                bif_r, mlstm_norm_w[o], w_out, B=bp, T=tp, L=ML_CHUNK, valid=ML_CHUNK)
            hs, st_s = _odd_mix(hs, ya_s, yb_s, mk_s.reshape(bs * N_MEM, MEM_W), mv_s.reshape(bs * N_MEM, MEM_W),
                                state_mlstm_c[o], state_mlstm_n[o], state_mlstm_m[o],
                                bif_r, mlstm_norm_w[o], w_out, B=bs, T=tsp, L=tsp, valid=ts)
            odd_p.append(st_p)
            odd_s.append(st_s)
    y_prompt = _rmsnorm_rows(hp, final_norm_w, F32).reshape(bp, tp, D_MODEL)
    y_sample = _rmsnorm_rows(hs, final_norm_w, F32).reshape(bs, tsp, D_MODEL)[:, :ts]
    return (y_prompt, y_sample,
            _stack(mem_new, 0), _stack(mem_new, 1),
            _stack(even_p, 0), _stack(even_p, 1), _stack(even_p, 2), _stack(even_p, 3),
            _stack(even_p, 4), _stack(even_p, 5), _stack(even_p, 6),
            _stack(odd_p, 0), _stack(odd_p, 1), _stack(odd_p, 2),
            _stack(even_s, 0), _stack(even_s, 1), _stack(even_s, 2), _stack(even_s, 3),
            _stack(even_s, 4), _stack(even_s, 5), _stack(even_s, 6),
            _stack(odd_s, 0), _stack(odd_s, 1), _stack(odd_s, 2))
```

```python
import functools
import math

import jax
import jax.numpy as jnp
import numpy as np
from jax import lax
from jax.experimental import pallas as pl
from jax.experimental.pallas import tpu as pltpu

D_MODEL = 4096
DEPTH = 2
PAST_LEN = 16384
PAGE_SIZE = 128
N_MEM = 256
EPS = 1e-6
CHUNK = 64

HG_DK = 128
HG_DV = 128
HG_HEADS = D_MODEL // 2 // HG_DV
HG_W = HG_HEADS * HG_DV

NSA_HD = 128
NSA_HEADS = D_MODEL // 2 // NSA_HD
NSA_KVH = 4
NSA_G = NSA_HEADS // NSA_KVH
NSA_W = NSA_HEADS * NSA_HD
NSA_KV_W = NSA_KVH * NSA_HD
CMP_BLOCK = 32
CMP_STRIDE = 16
SEL_BLOCK = 64
SEL_SHIFT = SEL_BLOCK.bit_length() - 1
N_SEL = 16
WINDOW = 512
Q_BLOCK = 256

ML_HEADS = D_MODEL // 512
ML_DK = D_MODEL // 2 // ML_HEADS
ML_DV = D_MODEL // ML_HEADS
ML_QK_W = ML_HEADS * ML_DK
ML_V_W = ML_HEADS * ML_DV

MEM_HEADS = 4
MEM_HD = 128
MEM_W = MEM_HEADS * MEM_HD

REL_BUCKETS = 32
REL_MAX_DIST = 128

F32 = jnp.float32
BF16 = jnp.bfloat16
LANES = 128
NEG_INF = float("-inf")
TINY = float(np.finfo(np.float32).tiny)
EXP_CLAMP = 80.0
VMEM_LIMIT = 56 * 1024 * 1024

HG_HB = 16
ML_HB = 4
ML_CHUNK = 256
W_TILE_M, W_TILE_N = 1024, 512
HG_SUB = 16
SAMPLE_PAD_T = 16

MM_TILE_N = 1024
EVEN_A = {"qa": 0, "fa": HG_W, "ia": 2 * HG_W, "za": 3 * HG_W, "qb": 4 * HG_W}
EVEN_A_N = 4 * HG_W + NSA_W
EVEN_B = {"zb": 0, "qm": NSA_W, "gb": NSA_W + MEM_W}
EVEN_B_N = -(-(NSA_W + MEM_W + LANES) // MM_TILE_N) * MM_TILE_N
EVEN_KV_OFF = EVEN_A_N
ODD_A = {"q": 0, "k": ML_QK_W, "v": 2 * ML_QK_W, "og": 2 * ML_QK_W + ML_V_W}
ODD_A_N = 2 * ML_QK_W + 2 * ML_V_W
ODD_B = {"z": 0, "qm": ML_V_W, "gates": ML_V_W + MEM_W}
ODD_B_N = -(-(ML_V_W + MEM_W + LANES) // MM_TILE_N) * MM_TILE_N


def _dot(a, b):
    return jnp.dot(a, b, preferred_element_type=F32)


def _dot_nt(a, b):
    return lax.dot_general(a, b, (((1,), (1,)), ((), ())), preferred_element_type=F32)


def _dot_tn(a, b):
    return lax.dot_general(a, b, (((0,), (0,)), ((), ())), preferred_element_type=F32)


def _iota2(shape, dim):
    return lax.broadcasted_iota(jnp.int32, shape, dim)


def _cumsum_rows(x, tri_b):
    hi = x.astype(BF16)
    r1 = x - hi.astype(F32)
    mid = r1.astype(BF16)
    lo = (r1 - mid.astype(F32)).astype(BF16)
    return _dot(tri_b, hi) + _dot(tri_b, mid) + _dot(tri_b, lo)


def _row_to_col(row, n):
    eye = _iota2((n, n), 0) == _iota2((n, n), 1)
    return jnp.sum(jnp.where(eye, row, 0.0), axis=1, keepdims=True)


def _col_to_row(col, n):
    eye = _iota2((n, n), 0) == _iota2((n, n), 1)
    return jnp.sum(jnp.where(eye, col, 0.0), axis=0, keepdims=True)


def _lane_col(x, idx):
    return jnp.sum(jnp.where(_iota2(x.shape, 1) == idx, x, 0.0), axis=1, keepdims=True)


def _silu(x):
    return x * jax.nn.sigmoid(x)


def _params(sem):
    return pltpu.CompilerParams(dimension_semantics=sem, vmem_limit_bytes=VMEM_LIMIT)


def _rmsnorm_body(x_ref, w_ref, o_ref):
    x = x_ref[...].astype(F32)
    y = x * lax.rsqrt(jnp.mean(x * x, axis=-1, keepdims=True) + EPS)
    o_ref[...] = (y * w_ref[...].astype(F32)).astype(o_ref.dtype)


def _rmsnorm_rows(x2d, w, out_dtype, tm=256):
    m, d = x2d.shape
    tm = min(tm, m)
    return pl.pallas_call(
        _rmsnorm_body,
        grid=(m // tm,),
        in_specs=[pl.BlockSpec((tm, d), lambda i: (i, 0)), pl.BlockSpec((1, d), lambda i: (0, 0))],
        out_specs=pl.BlockSpec((tm, d), lambda i: (i, 0)),
        out_shape=jax.ShapeDtypeStruct((m, d), out_dtype),
        compiler_params=_params(("parallel",)),
        name="rmsnorm",
    )(x2d, w.reshape(1, d))


def _matmul_nt_body(a_ref, bt_ref, o_ref):
    o_ref[...] = _dot_nt(a_ref[...], bt_ref[...].astype(BF16))


def _matmul_nt(a, bt, tm=1024, tn=MM_TILE_N, rows=None):
    m, k = a.shape
    first, n = rows or (0, bt.shape[0])
    tm, tn = min(tm, m), min(tn, n)
    assert m % tm == 0 and n % tn == 0 and first % tn == 0, (a.shape, bt.shape, rows)
    j0 = first // tn
    return pl.pallas_call(
        _matmul_nt_body,
        grid=(m // tm, n // tn),
        in_specs=[pl.BlockSpec((tm, k), lambda i, j: (i, 0)), pl.BlockSpec((tn, k), lambda i, j: (j0 + j, 0))],
        out_specs=pl.BlockSpec((tm, tn), lambda i, j: (i, j)),
        out_shape=jax.ShapeDtypeStruct((m, n), F32),
        compiler_params=_params(("parallel", "parallel")),
        name="matmul",
    )(a, bt)


def _tail_body(lo_ref, hi_ref, gate_ref, o_ref, *, shift, n_main):
    i = pl.program_id(0)
    main = jnp.concatenate([lo_ref[shift:, :], hi_ref[:shift, :]], axis=0)
    gates = jnp.where(_iota2((LANES, 1), 0) < shift, gate_ref[...], 0.0)
    o_ref[...] = jnp.where(i < n_main, main, jnp.where(i == n_main, gates, 0.0)).astype(o_ref.dtype)


def _tail_relayout(wt, first, shift, main, out_rows):
    n, k = wt.shape
    assert first % LANES == 0 and main % LANES == 0 and out_rows % LANES == 0 and shift % 8 == 0 and shift < LANES
    assert first + shift + main == n
    c0, n_main = first // LANES, main // LANES
    return pl.pallas_call(
        functools.partial(_tail_body, shift=shift, n_main=n_main),
        grid=(out_rows // LANES,),
        in_specs=[pl.BlockSpec((LANES, k), lambda i: (c0 + jnp.minimum(i, n_main - 1), 0)),
                  pl.BlockSpec((LANES, k), lambda i: (c0 + jnp.minimum(i, n_main - 1) + 1, 0)),
                  pl.BlockSpec((LANES, k), lambda i: (c0, 0))],
        out_specs=pl.BlockSpec((LANES, k), lambda i: (i, 0)),
        out_shape=jax.ShapeDtypeStruct((out_rows, k), BF16),
        compiler_params=_params(("parallel",)),
        name="tail_relayout",
    )(wt, wt, wt)


def _matmul_heads_body(a_ref, b_ref, o32_ref, o16_ref, *, transposed):
    b = b_ref[...].astype(BF16)
    acc = _dot_nt(a_ref[...], b) if transposed else _dot(a_ref[...], b)
    for h in range(MEM_HEADS):
        o32_ref[:, h, :] = acc[:, h * LANES:(h + 1) * LANES]
    o16_ref[...] = acc.astype(BF16)


def _matmul_heads(a, b, first=0, transposed=False, tm=1024):
    m, k = a.shape
    n = MEM_HEADS * LANES
    tm = min(tm, m)
    assert m % tm == 0 and first % n == 0, (a.shape, b.shape, first)
    j0 = first // n
    b_spec = pl.BlockSpec((n, k), lambda i: (j0, 0)) if transposed else pl.BlockSpec((k, n), lambda i: (0, j0))
    return pl.pallas_call(
        functools.partial(_matmul_heads_body, transposed=transposed),
        grid=(m // tm,),
        in_specs=[pl.BlockSpec((tm, k), lambda i: (i, 0)), b_spec],
        out_specs=[pl.BlockSpec((tm, MEM_HEADS, LANES), lambda i: (i, 0, 0)), pl.BlockSpec((tm, n), lambda i: (i, 0))],
        out_shape=[jax.ShapeDtypeStruct((m, MEM_HEADS, LANES), F32), jax.ShapeDtypeStruct((m, n), BF16)],
        compiler_params=_params(("parallel",)),
        name="matmul_heads",
    )(a, b)


def _outproj_body(*refs, widths):
    xs = refs[:len(widths)]
    w_ref, r_ref, o_ref = refs[len(widths):]
    acc = r_ref[...]
    off = 0
    for x_ref, w in zip(xs, widths):
        acc = acc + _dot(x_ref[...], w_ref[off:off + w, :])
        off += w
    o_ref[...] = acc


def _outproj(xs, w_bf16, resid, tm=1024, tn=512):
    m = resid.shape[0]
    n = w_bf16.shape[1]
    widths = tuple(x.shape[1] for x in xs)
    assert sum(widths) == w_bf16.shape[0]
    tm = min(tm, m)
    in_specs = [pl.BlockSpec((tm, w), lambda i, j: (i, 0)) for w in widths]
    in_specs += [pl.BlockSpec((w_bf16.shape[0], tn), lambda i, j: (0, j)), pl.BlockSpec((tm, tn), lambda i, j: (i, j))]
    return pl.pallas_call(
        functools.partial(_outproj_body, widths=widths),
        grid=(m // tm, n // tn),
        in_specs=in_specs,
        out_specs=pl.BlockSpec((tm, tn), lambda i, j: (i, j)),
        out_shape=jax.ShapeDtypeStruct((m, n), F32),
        compiler_params=_params(("parallel", "parallel")),
        name="outproj",
    )(*xs, w_bf16, resid)


def _hgrn_body(qa_ref, fa_ref, ia_ref, za_ref, lb_ref, gn_ref, s0_ref, o_ref, s_out, s_scr, *, L, valid):
    c = pl.program_id(2)

    @pl.when(c == 0)
    def _():
        s_scr[...] = s0_ref[...]

    lb = lb_ref[...]
    sig = jax.nn.sigmoid(fa_ref[...])
    logf = jnp.log(lb + (1.0 - lb) * sig)
    kk = (1.0 - lb) * (1.0 - sig)
    if valid < L:
        live = _iota2((L, 1), 0) < valid
        logf = jnp.where(live, logf, 0.0)
        kk = jnp.where(live, kk, 0.0)
    tri_b = (_iota2((L, L), 0) >= _iota2((L, L), 1)).astype(BF16)
    bc = _cumsum_rows(logf, tri_b)
    q = _silu(qa_ref[...])
    gate = _silu(za_ref[...])
    v = ia_ref[...]
    gn = gn_ref[...]
    nsub = L // HG_SUB
    rr = _iota2((L, nsub * L), 0)
    cc = _iota2((L, nsub * L), 1)
    keep = ((jnp.right_shift(cc, L.bit_length() - 1) == jnp.right_shift(rr, HG_SUB.bit_length() - 1))
            & (jnp.bitwise_and(cc, L - 1) <= rr))
    for j in range(HG_HB):
        sl = slice(j * HG_DK, (j + 1) * HG_DK)
        bj, qj, kj = bc[:, sl], q[:, sl], kk[:, sl]
        vb = v[:, sl].astype(BF16)
        s_prev = s_scr[j]
        inter = _dot((qj * jnp.exp(bj)).astype(BF16), s_prev.astype(BF16))
        mids = [bj[i * HG_SUB + HG_SUB // 2:i * HG_SUB + HG_SUB // 2 + 1, :] for i in range(nsub)]
        mid_rows = jnp.concatenate([jnp.broadcast_to(m, (HG_SUB, HG_DK)) for m in mids], axis=0)
        q_dec = qj * jnp.exp(jnp.minimum(bj - mid_rows, EXP_CLAMP))
        k_dec = jnp.concatenate([kj * jnp.exp(jnp.minimum(m - bj, EXP_CLAMP)) for m in mids], axis=0)
        att = jnp.where(keep, _dot_nt(q_dec.astype(BF16), k_dec.astype(BF16)), 0.0)
        o = inter + _dot(att.astype(BF16), jnp.concatenate([vb] * nsub, axis=0))
        o_n = o * lax.rsqrt(jnp.mean(o * o, axis=-1, keepdims=True) + EPS) * gn
        o_ref[:, sl] = (o_n * gate[:, sl]).astype(o_ref.dtype)
        bl = bj[L - 1:L, :]
        kd = kj * jnp.exp(bl - bj)
        s_scr[j] = _row_to_col(jnp.exp(bl), HG_DK) * s_prev + _dot_tn(kd.astype(BF16), vb)

    @pl.when(c == pl.num_programs(2) - 1)
    def _():
        s_out[...] = s_scr[...]


def _hgrn_call(y, s0, lb, gn, *, B, T, L, valid):
    nc = T // L
    w = HG_HB * HG_DK

    def col(name):
        blk = EVEN_A[name] // w
        return pl.BlockSpec((L, w), lambda b, hg, c: (b * nc + c, blk + hg))

    state_spec = pl.BlockSpec((None, HG_HB, HG_DK, HG_DV), lambda b, hg, c: (b, hg, 0, 0))
    return pl.pallas_call(
        functools.partial(_hgrn_body, L=L, valid=valid),
        grid=(B, HG_HEADS // HG_HB, nc),
        in_specs=[col("qa"), col("fa"), col("ia"), col("za"),
                  pl.BlockSpec((1, w), lambda b, hg, c: (0, hg)),
                  pl.BlockSpec((1, HG_DV), lambda b, hg, c: (0, 0)),
                  state_spec],
        out_specs=[pl.BlockSpec((L, w), lambda b, hg, c: (b * nc + c, hg)), state_spec],
        out_shape=[jax.ShapeDtypeStruct((B * T, HG_W), BF16),
                   jax.ShapeDtypeStruct((B, HG_HEADS, HG_DK, HG_DV), F32)],
        scratch_shapes=[pltpu.VMEM((HG_HB, HG_DK, HG_DV), F32)],
        compiler_params=_params(("arbitrary", "arbitrary", "arbitrary")),
        name="hgrn2",
    )(y, y, y, y, lb.reshape(1, HG_W), gn.reshape(1, HG_DV), s0)


def _mlstm_body(q_ref, k_ref, v_ref, og_ref, z_ref, g_ref, bif_ref, gn_ref, c0_ref, n0_ref, m0_ref,
                h_ref, c_out, n_out, m_out, c_scr, n_scr, m_scr, *, L, valid):
    c = pl.program_id(2)

    @pl.when(c == 0)
    def _():
        c_scr[...] = c0_ref[...]
        n_scr[...] = n0_ref[...]
        m_scr[...] = m0_ref[...]

    gates = g_ref[...] + bif_ref[...]
    log_i = gates
    log_f = jnp.minimum(gates, 0.0) - jnp.log(1.0 + jnp.exp(-jnp.abs(gates)))
    if valid < L:
        live = _iota2((L, 1), 0) < valid
        log_i = jnp.where(live, log_i, -1e30)
        log_f = jnp.where(live, log_f, 0.0)
    tri = _iota2((L, L), 0) >= _iota2((L, L), 1)
    bcs = _cumsum_rows(log_f, tri.astype(BF16))
    for j in range(ML_HB):
        head = pl.program_id(1) * ML_HB + j
        b_col = _lane_col(bcs, ML_HEADS + head)
        i_col = _lane_col(log_i, head)
        b_row = _col_to_row(b_col, L)
        i_row = _col_to_row(i_col, L)
        m_prev = m_scr[:, j:j + 1]
        dmat = jnp.where(tri, b_col - b_row + i_row, NEG_INF)
        inter = b_col + m_prev
        mt = jnp.maximum(inter, jnp.max(dmat, axis=1, keepdims=True))
        w_in = jnp.exp(dmat - mt)
        w_x = jnp.exp(inter - mt)
        qj = q_ref[:, j * ML_DK:(j + 1) * ML_DK]
        kj = k_ref[:, j * ML_DK:(j + 1) * ML_DK] * (ML_DK ** -0.5)
        vj = v_ref[:, j * ML_DV:(j + 1) * ML_DV]
        qb, kb = qj.astype(BF16), kj.astype(BF16)
        sw = _dot_nt(qb, kb) * w_in
        c_prev = c_scr[j]
        n_prev = n_scr[:, j * ML_DK:(j + 1) * ML_DK]
        num = w_x * _dot_nt(qb, c_prev.astype(BF16)) + _dot(sw.astype(BF16), vj.astype(BF16))
        den = w_x * jnp.sum(qj * n_prev, axis=1, keepdims=True) + jnp.sum(sw, axis=1, keepdims=True)
        h = num / jnp.maximum(jnp.abs(den), jnp.exp(-mt))
        m_last = mt[L - 1:L, :]
        b_last = b_col[L - 1:L, :]
        w_end = jnp.exp(b_last - b_col + i_col - m_last)
        d_c = jnp.exp(b_last + m_prev - m_last)
        c_scr[j] = d_c * c_prev + _dot_tn((w_end * vj).astype(BF16), kb)
        n_scr[:, j * ML_DK:(j + 1) * ML_DK] = d_c * n_prev + jnp.sum(w_end * kj, axis=0, keepdims=True)
        m_scr[:, j:j + 1] = m_last
        sv = slice(j * ML_DV, (j + 1) * ML_DV)
        h_n = h * lax.rsqrt(jnp.mean(h * h, axis=-1, keepdims=True) + EPS) * gn_ref[:, sv]
        h_ref[:, sv] = (h_n * jax.nn.sigmoid(og_ref[:, sv]) * _silu(z_ref[:, sv])).astype(h_ref.dtype)

    @pl.when(c == pl.num_programs(2) - 1)
    def _():
        c_out[...] = c_scr[...]
        n_out[...] = n_scr[...]
        m_out[...] = m_scr[...]


def _mlstm_call(ya, yb, c0, n0, m0, bif_r, gn, *, B, T, L, valid):
    nc = T // L
    ng = ML_HEADS // ML_HB
    wk, wv = ML_HB * ML_DK, ML_HB * ML_DV

    def col(name, w):
        blk = (ODD_A[name] if name in ODD_A else ODD_B[name]) // w
        return pl.BlockSpec((L, w), lambda b, hg, c: (b * nc + c, blk + hg))

    c_spec = pl.BlockSpec((None, ML_HB, ML_DV, ML_DK), lambda b, hg, c: (b, hg, 0, 0))
    n_spec = pl.BlockSpec((None, 1, wk), lambda b, hg, c: (b, 0, hg))
    m_spec = pl.BlockSpec((None, None, 1, LANES), lambda b, hg, c: (b, hg, 0, 0))
    m0_r = jnp.pad(m0.reshape(B, ng, 1, ML_HB), ((0, 0), (0, 0), (0, 0), (0, LANES - ML_HB)))
    h, c_new, n_new, m_new = pl.pallas_call(
        functools.partial(_mlstm_body, L=L, valid=valid),
        grid=(B, ng, nc),
        in_specs=[col("q", wk), col("k", wk), col("v", wv), col("og", wv), col("z", wv),
                  pl.BlockSpec((L, LANES), lambda b, hg, c: (b * nc + c, ODD_B["gates"] // LANES)),
                  pl.BlockSpec((1, LANES), lambda b, hg, c: (0, 0)),
                  pl.BlockSpec((1, wv), lambda b, hg, c: (0, hg)),
                  c_spec, n_spec, m_spec],
        out_specs=[pl.BlockSpec((L, wv), lambda b, hg, c: (b * nc + c, hg)), c_spec, n_spec, m_spec],
        out_shape=[jax.ShapeDtypeStruct((B * T, ML_V_W), BF16),
                   jax.ShapeDtypeStruct((B, ML_HEADS, ML_DV, ML_DK), F32),
                   jax.ShapeDtypeStruct((B, 1, ML_QK_W), F32),
                   jax.ShapeDtypeStruct((B, ng, 1, LANES), F32)],
        scratch_shapes=[pltpu.VMEM((ML_HB, ML_DV, ML_DK), F32), pltpu.VMEM((1, wk), F32), pltpu.VMEM((1, LANES), F32)],
        compiler_params=_params(("arbitrary", "arbitrary", "arbitrary")),
        name="mlstm",
    )(ya, ya, ya, ya, yb, yb, bif_r, gn.reshape(1, ML_V_W), c0, n0.reshape(B, 1, ML_QK_W), m0_r)
    return h, c_new, n_new.reshape(B, ML_HEADS, ML_DK), m_new[:, :, 0, :ML_HB].reshape(B, ML_HEADS)


def _mem_body(q_ref, k_ref, v_ref, o_ref):
    q = q_ref[...] * (MEM_HD ** -0.5)
    for h in range(MEM_HEADS):
        sl = slice(h * MEM_HD, (h + 1) * MEM_HD)
        s = _dot_nt(q[:, sl].astype(BF16), k_ref[:, sl].astype(BF16))
        p = jnp.exp(s - jnp.max(s, axis=-1, keepdims=True))
        o = _dot(p.astype(BF16), v_ref[:, sl].astype(BF16)) / jnp.sum(p, axis=-1, keepdims=True)
        o_ref[:, sl] = o.astype(o_ref.dtype)


def _mem_call(y, q_off, k2d, v2d, *, B, T, tq=256):
    tq = min(tq, T)
    nq = T // tq
    qb = q_off // MEM_W
    return pl.pallas_call(
        _mem_body,
        grid=(B, nq),
        in_specs=[pl.BlockSpec((tq, MEM_W), lambda b, i: (b * nq + i, qb)),
                  pl.BlockSpec((N_MEM, MEM_W), lambda b, i: (b, 0)),
                  pl.BlockSpec((N_MEM, MEM_W), lambda b, i: (b, 0))],
        out_specs=pl.BlockSpec((tq, MEM_W), lambda b, i: (b * nq + i, 0)),
        out_shape=jax.ShapeDtypeStruct((B * T, MEM_W), BF16),
        compiler_params=_params(("parallel", "parallel")),
        name="mem_attn",
    )(y, k2d, v2d)


def _gelu_tanh(x):
    return 0.5 * x * (1.0 + jnp.tanh(math.sqrt(2.0 / math.pi) * (x + 0.044715 * (x * x * x))))


def _compress_body(x_ref, w1_ref, b1_ref, w2_ref, pe_ref, o_ref, x32, *, nch):
    x32[...] = x_ref[...].astype(F32)
    a = jnp.zeros((nch, NSA_HD), F32)
    b = jnp.zeros((nch, NSA_HD), F32)
    for s in range(CMP_STRIDE):
        r = x32[pl.ds(s, nch, stride=CMP_STRIDE), :]
        a = a + _dot((r + pe_ref[s:s + 1, :]).astype(BF16), w1_ref[s])
        b = b + _dot((r + pe_ref[CMP_STRIDE + s:CMP_STRIDE + s + 1, :]).astype(BF16), w1_ref[CMP_STRIDE + s])
    h = a + pltpu.roll(b, nch - 1, 0) + b1_ref[...]
    o_ref[...] = _dot(_gelu_tanh(h).astype(BF16), w2_ref[...])


def _compress_call(x16, w1, b1, w2, pe, *, B, T):
    nch = T // CMP_STRIDE
    return pl.pallas_call(
        functools.partial(_compress_body, nch=nch),
        grid=(B, NSA_KVH),
        in_specs=[pl.BlockSpec((T, NSA_HD), lambda b, h: (b, h)),
                  pl.BlockSpec((CMP_BLOCK, NSA_HD, NSA_HD), lambda b, h: (0, 0, 0)),
                  pl.BlockSpec((1, NSA_HD), lambda b, h: (0, 0)),
                  pl.BlockSpec((NSA_HD, NSA_HD), lambda b, h: (0, 0)),
                  pl.BlockSpec((CMP_BLOCK, NSA_HD), lambda b, h: (0, 0))],
        out_specs=pl.BlockSpec((None, None, nch, NSA_HD), lambda b, h: (b, h, 0, 0)),
        out_shape=jax.ShapeDtypeStruct((B, NSA_KVH, nch, NSA_HD), F32),
        scratch_shapes=[pltpu.VMEM((T, NSA_HD), F32)],
        compiler_params=_params(("parallel", "parallel")),
        name="nsa_compress",
    )(x16, w1.astype(BF16), b1.reshape(1, NSA_HD), w2.astype(BF16), pe)


def _softmax_rows(s):
    m = jnp.max(s, axis=-1, keepdims=True)
    m = jnp.where(m == NEG_INF, 0.0, m)
    p = jnp.exp(s - m)
    return p, jnp.sum(p, axis=-1, keepdims=True)


def _slc_scores(psum, width, n_slc):
    ncmp = psum.shape[1]
    d = _iota2((ncmp, width), 0) - (SEL_BLOCK // CMP_STRIDE) * _iota2((ncmp, width), 1)
    wgt = jnp.where((d == -1) | (d == 3), 1.0, jnp.where((d >= 0) & (d <= 2), 2.0, 0.0))
    wgt = jnp.where(_iota2((ncmp, width), 1) < n_slc, wgt, 0.0).astype(BF16)
    p_hi = psum.astype(BF16)
    p_lo = (psum - p_hi.astype(F32)).astype(BF16)
    return _dot(p_hi, wgt) + _dot(p_lo, wgt)


def _top_blocks(slc, cur, n_pick):
    rows, width = slc.shape
    blk = _iota2((rows, width), 1)
    forced = (blk == 0) | (blk == cur) | (blk == cur - 1)
    score = jnp.where(forced, jnp.inf, slc)
    score = jnp.where(blk > cur, NEG_INF, score)
    blk_f = blk.astype(F32)
    lane = _iota2((rows, LANES), 1)
    sel = jnp.zeros((rows, width), F32)
    picks = jnp.zeros((rows, LANES), F32)
    for i in range(n_pick):
        mx = jnp.max(score, axis=-1, keepdims=True)
        first = jnp.min(jnp.where(score == mx, blk_f, float(width)), axis=-1, keepdims=True)
        pick = blk_f == first
        sel = jnp.where(pick, 1.0, sel)
        picks = jnp.where(lane == i, first, picks)
        score = jnp.where(pick, NEG_INF, score)
    return sel, picks


def _member_by_rank(psum, tpos_row, n_slc, n_pick):
    nq, ncmp = psum.shape
    nb = -(-n_slc // 8) * 8
    d = _iota2((nb, ncmp), 1) - (SEL_BLOCK // CMP_STRIDE) * _iota2((nb, ncmp), 0)
    wgt = jnp.where((d == -1) | (d == 3), 1.0, jnp.where((d >= 0) & (d <= 2), 2.0, 0.0))
    wgt = jnp.where(_iota2((nb, ncmp), 0) < n_slc, wgt, 0.0).astype(BF16)
    p_hi = psum.astype(BF16)
    p_lo = (psum - p_hi.astype(F32)).astype(BF16)
    slc = _dot_nt(wgt, p_hi) + _dot_nt(wgt, p_lo)
    blk = _iota2((nb, nq), 0)
    cur = jnp.right_shift(tpos_row, SEL_SHIFT)
    forced = (blk == 0) | (blk == cur) | (blk == cur - 1)
    score = jnp.where(forced, jnp.inf, slc)
    score = jnp.where(blk > cur, NEG_INF, score)
    ahead = jnp.zeros((nb, nq), F32)
    for i in range(n_slc):
        s_i = score[i:i + 1, :]
        ahead = ahead + jnp.where((s_i > score) | ((s_i == score) & (blk > i)), 1.0, 0.0)
    return jnp.where((ahead < n_pick) & (blk <= cur), 1.0, 0.0)


NEAR_COLS = Q_BLOCK + REL_MAX_DIST


def _banded_attention(q, k_ref, v_ref, start, width, mask, near_bias):
    far = width - NEAR_COLS
    s_far = _dot_nt(q, k_ref[pl.ds(start, far), :]) + mask[:, :far]
    s_near = _dot_nt(q, k_ref[pl.ds(start + far, NEAR_COLS), :]) + near_bias + mask[:, far:]
    m = jnp.maximum(jnp.max(s_far, axis=-1, keepdims=True), jnp.max(s_near, axis=-1, keepdims=True))
    m = jnp.where(m == NEG_INF, 0.0, m)
    p_far, p_near = jnp.exp(s_far - m), jnp.exp(s_near - m)
    l = jnp.sum(p_far, axis=-1, keepdims=True) + jnp.sum(p_near, axis=-1, keepdims=True)
    o = (_dot(p_far.astype(BF16), v_ref[pl.ds(start, far), :])
         + _dot(p_near.astype(BF16), v_ref[pl.ds(start + far, NEAR_COLS), :]))
    return o / jnp.maximum(l, TINY)


def _nsa_prompt_body(q_ref, zb_ref, gb_ref, bg_ref, ks_ref, vs_ref, kw_ref, vw_ref, kc_ref, vc_ref,
                     bc_ref, bn_ref, o_ref, ksp, vsp, kwp, vwp, osel, *, T):
    qi = pl.program_id(2)
    tq = Q_BLOCK
    front = T - tq
    wlen = WINDOW + tq
    n_slc = T // SEL_BLOCK

    @pl.when(qi == 0)
    def _():
        ksp[0:front, :] = jnp.zeros((front, NSA_HD), BF16)
        vsp[0:front, :] = jnp.zeros((front, NSA_HD), BF16)
        ksp[front:front + T, :] = ks_ref[...].astype(BF16)
        vsp[front:front + T, :] = vs_ref[...].astype(BF16)
        kwp[0:WINDOW, :] = jnp.zeros((WINDOW, NSA_HD), BF16)
        vwp[0:WINDOW, :] = jnp.zeros((WINDOW, NSA_HD), BF16)
        kwp[WINDOW:WINDOW + T, :] = kw_ref[...].astype(BF16)
        vwp[WINDOW:WINDOW + T, :] = vw_ref[...].astype(BF16)

    t0 = pl.multiple_of(qi * tq, tq)
    tpos = _iota2((tq, 1), 0) + t0
    q_all = q_ref[...] * (NSA_HD ** -0.5)
    q = jnp.concatenate([q_all[:, g * NSA_HD:(g + 1) * NSA_HD] for g in range(NSA_G)], axis=0).astype(BF16)
    bias_near = bn_ref[...].reshape(NSA_G * tq, NEAR_COLS)

    def per_head(a):
        return jnp.concatenate([a] * NSA_G, axis=0)

    ncmp = T // CMP_STRIDE
    vis = tpos >= _iota2((1, ncmp), 1) * CMP_STRIDE + (CMP_BLOCK - 1)
    s = _dot_nt(q, kc_ref[...].astype(BF16)) + bc_ref[...].reshape(NSA_G * tq, ncmp)
    p, l = _softmax_rows(s + per_head(jnp.where(vis, 0.0, NEG_INF)))
    p = p / jnp.maximum(l, TINY)
    o_cmp = _dot(p.astype(BF16), vc_ref[...].astype(BF16))
    psum = p[0:tq]
    for g in range(1, NSA_G):
        psum = psum + p[g * tq:(g + 1) * tq]

    member_t = _member_by_rank(psum, _iota2((1, tq), 1) + t0, n_slc, min(N_SEL, n_slc)).astype(BF16)

    nb = member_t.shape[0]
    n_win = SEL_WINDOWS if T % (SEL_WINDOWS * tq) == 0 else 1
    for i in range(n_win):
        w_prev, w = T * i // n_win, T * (i + 1) // n_win

        @pl.when((qi >= w_prev // tq) & (qi < w // tq))
        def _(w=w):
            off = T - w
            col_blk = (jnp.right_shift(_iota2((nb, w), 1) + off, SEL_SHIFT)
                       + (qi * (tq // SEL_BLOCK) + (tq - T) // SEL_BLOCK))
            expand = (col_blk == _iota2((nb, w), 0)).astype(BF16)
            kpos = _iota2((1, w), 1) + (t0 + tq - w)
            allowed = (_dot_tn(member_t, expand) > 0.5) & (kpos <= tpos)
            mask_s = per_head(jnp.where(allowed, 0.0, NEG_INF))
            osel[...] = _banded_attention(q, ksp, vsp, t0 + off, w, mask_s, bias_near)

    dist = WINDOW + _iota2((tq, wlen), 0) - _iota2((tq, wlen), 1)
    in_win = (dist >= 0) & (dist < WINDOW) & (_iota2((1, wlen), 1) + (t0 - WINDOW) >= 0)
    o_win = _banded_attention(q, kwp, vwp, t0, wlen, per_head(jnp.where(in_win, 0.0, NEG_INF)), bias_near)
    gate = jax.nn.sigmoid(gb_ref[...] + bg_ref[...])
    zb = _silu(zb_ref[...])
    for g in range(NSA_G):
        head = pl.program_id(1) * NSA_G + g
        r = slice(g * tq, (g + 1) * tq)
        mix = (_lane_col(gate, head) * o_cmp[r] + _lane_col(gate, NSA_HEADS + head) * osel[r, :]
               + _lane_col(gate, 2 * NSA_HEADS + head) * o_win[r])
        sl = slice(g * NSA_HD, (g + 1) * NSA_HD)
        o_ref[:, sl] = (mix * zb[:, sl]).astype(o_ref.dtype)


def _nsa_prompt_call(ya, yb, kv16, kcmp, vcmp, bg_r, bias_c, bias_near, *, B, T):
    nq = T // Q_BLOCK
    gw = NSA_G * NSA_HD
    kv_spec = pl.BlockSpec((T, NSA_HD), lambda b, h, i: (b, h))
    cmp_spec = pl.BlockSpec((None, None, T // CMP_STRIDE, NSA_HD), lambda b, h, i: (b, h, 0, 0))
    return pl.pallas_call(
        functools.partial(_nsa_prompt_body, T=T),
        grid=(B, NSA_KVH, nq),
        in_specs=[pl.BlockSpec((Q_BLOCK, gw), lambda b, h, i: (b * nq + i, EVEN_A["qb"] // gw + h)),
                  pl.BlockSpec((Q_BLOCK, gw), lambda b, h, i: (b * nq + i, EVEN_B["zb"] // gw + h)),
                  pl.BlockSpec((Q_BLOCK, LANES), lambda b, h, i: (b * nq + i, EVEN_B["gb"] // LANES)),
                  pl.BlockSpec((1, LANES), lambda b, h, i: (0, 0)),
                  kv_spec, kv_spec, kv_spec, kv_spec, cmp_spec, cmp_spec,
                  pl.BlockSpec((None, NSA_G, Q_BLOCK, T // CMP_STRIDE), lambda b, h, i: (h, 0, i, 0)),
                  pl.BlockSpec((None, NSA_G, Q_BLOCK, NEAR_COLS), lambda b, h, i: (h, 0, 0, 0))],
        out_specs=pl.BlockSpec((Q_BLOCK, gw), lambda b, h, i: (b * nq + i, h)),
        out_shape=jax.ShapeDtypeStruct((B * T, NSA_W), BF16),
        scratch_shapes=[pltpu.VMEM((2 * T - Q_BLOCK, NSA_HD), BF16), pltpu.VMEM((2 * T - Q_BLOCK, NSA_HD), BF16),
                        pltpu.VMEM((WINDOW + T, NSA_HD), BF16), pltpu.VMEM((WINDOW + T, NSA_HD), BF16),
                        pltpu.VMEM((NSA_G * Q_BLOCK, NSA_HD), F32)],
        compiler_params=_params(("arbitrary", "arbitrary", "arbitrary")),
        name="nsa_prompt",
    )(ya, yb, yb, bg_r, *kv16, kcmp, vcmp, bias_c, bias_near)


CMP_PAGES = 16
CHUNKS_PER_PAGE = PAGE_SIZE // CMP_STRIDE
PAGE_ROWS = PAGE_SIZE * NSA_KVH


def _pool_rows(pool):
    return pool.reshape(pool.shape[0] * PAGE_ROWS, NSA_HD)


def _cmp_pages_body(pt_ref, *refs):
    del pt_ref
    pages = refs[:CMP_PAGES]
    w_ref, pe_ref, o_ref = refs[CMP_PAGES:]
    rows = CMP_PAGES * CHUNKS_PER_PAGE
    per_head = [jnp.concatenate(
        [jnp.concatenate([pg[pl.ds(NSA_KVH * s + h, CHUNKS_PER_PAGE, stride=CMP_STRIDE * NSA_KVH), :]
                          for s in range(CMP_STRIDE)], axis=1) for pg in pages], axis=0) for h in range(NSA_KVH)]
    w = w_ref[...]
    r = _dot(jnp.concatenate(per_head, axis=0).astype(BF16), w)
    pc = _dot(pe_ref[...], w)
    r = r + jnp.concatenate([pc[0:1, :NSA_HD], pc[1:2, NSA_HD:]], axis=1)
    for h in range(NSA_KVH):
        o_ref[h] = r[h * rows:(h + 1) * rows]


def _cmp_pages_call(pool, page_table, w1, pe, *, B):
    n_pages = page_table.shape[1]
    rows = CMP_PAGES * CHUNKS_PER_PAGE
    view = _pool_rows(pool)
    w = w1.reshape(2, CMP_STRIDE, NSA_HD, NSA_HD).transpose(1, 2, 0, 3).reshape(CMP_STRIDE * NSA_HD, 2 * NSA_HD)
    pe_rows = jnp.pad(pe.reshape(2, CMP_STRIDE * NSA_HD), ((0, 6), (0, 0))).astype(BF16)

    def page_spec(i):
        return pl.BlockSpec((PAGE_ROWS, NSA_HD), lambda b, s, pt: (pt[b * n_pages + s * CMP_PAGES + i], 0))

    grid_spec = pltpu.PrefetchScalarGridSpec(
        num_scalar_prefetch=1,
        grid=(B, n_pages // CMP_PAGES),
        in_specs=[page_spec(i) for i in range(CMP_PAGES)]
        + [pl.BlockSpec((CMP_STRIDE * NSA_HD, 2 * NSA_HD), lambda b, s, pt: (0, 0)),
           pl.BlockSpec((8, CMP_STRIDE * NSA_HD), lambda b, s, pt: (0, 0))],
        out_specs=pl.BlockSpec((None, NSA_KVH, rows, 2 * NSA_HD), lambda b, s, pt: (b, 0, s, 0)),
    )
    return pl.pallas_call(
        _cmp_pages_body,
        grid_spec=grid_spec,
        out_shape=jax.ShapeDtypeStruct((B, NSA_KVH, n_pages * CHUNKS_PER_PAGE, 2 * NSA_HD), F32),
        compiler_params=_params(("arbitrary", "arbitrary")),
        name="nsa_cmp_pages",
    )(page_table.reshape(-1), *([view] * CMP_PAGES), w.astype(BF16), pe_rows)


SEL_WINDOWS = 4
SLC_LANES = 384


def _sample_q_rows(q_ref):
    q = q_ref[...] * (NSA_HD ** -0.5)
    return jnp.concatenate([q[:, g * NSA_HD:(g + 1) * NSA_HD] for g in range(NSA_G)], axis=0).astype(BF16)


def _nsa_sample_main_body(abk_ref, abv_ref, b1_ref, w2_ref, q_ref, wk_ref, wv_ref, kn_ref, vn_ref, bc_ref, bw_ref,
                          ocmp_ref, owin_ref, idx_ref, *, T, n_slc):
    tp = SAMPLE_PAD_T
    rows = NSA_G * tp
    ncmp = abk_ref.shape[0]

    def compressed(ab_ref, t):
        ab = ab_ref[...]
        h = ab[:, :NSA_HD] + pltpu.roll(ab[:, NSA_HD:], ncmp - 1, 0) + b1_ref[t]
        return _dot(_gelu_tanh(h).astype(BF16), w2_ref[t]).astype(BF16)

    kc, vc = compressed(abk_ref, 0), compressed(abv_ref, 1)
    q = _sample_q_rows(q_ref)
    step = jnp.bitwise_and(_iota2((rows, 1), 0), tp - 1)
    tpos = PAST_LEN + step
    vis = tpos >= _iota2((1, ncmp), 1) * CMP_STRIDE + (CMP_BLOCK - 1)
    p, l = _softmax_rows(jnp.where(vis, _dot_nt(q, kc) + bc_ref[...], NEG_INF))
    p = p / jnp.maximum(l, TINY)
    ocmp_ref[...] = _dot(p.astype(BF16), vc)
    psum = p[0:tp]
    for g in range(1, NSA_G):
        psum = psum + p[g * tp:(g + 1) * tp]
    cur = jnp.right_shift(PAST_LEN + _iota2((tp, 1), 0), SEL_SHIFT)
    _, picks = _top_blocks(_slc_scores(psum, SLC_LANES, n_slc), cur, N_SEL)
    idx_ref[...] = picks.astype(jnp.int32)

    wb = wk_ref.shape[0] // NSA_KVH
    wlen = bw_ref.shape[1]
    fill = jnp.zeros((wlen - wb - tp, NSA_HD), BF16)
    head = pl.program_id(1)
    k_all = jnp.concatenate([wk_ref[pl.ds(head, wb, stride=NSA_KVH), :].astype(BF16), kn_ref[...], fill], axis=0)
    v_all = jnp.concatenate([wv_ref[pl.ds(head, wb, stride=NSA_KVH), :].astype(BF16), vn_ref[...], fill], axis=0)
    col = _iota2((1, wlen), 1)
    dist = tpos - (PAST_LEN - wb + col)
    in_win = (dist >= 0) & (dist < WINDOW) & (col < wb + T)
    pw, lw = _softmax_rows(jnp.where(in_win, _dot_nt(q, k_all) + bw_ref[...], NEG_INF))
    owin_ref[...] = _dot(pw.astype(BF16), v_all) / jnp.maximum(lw, TINY)


def _nsa_sample_main_call(ya, kw16, vw16, abk, abv, b1, w2, wk, wv, bias_c, bias_w, *, B, T):
    tp = SAMPLE_PAD_T
    rows = NSA_G * tp
    gw = NSA_G * NSA_HD
    ncmp = abk.shape[2]
    wb = wk.shape[1]
    wlen = bias_w.shape[-1]
    n_slc = -(-(PAST_LEN + T) // SEL_BLOCK)
    assert n_slc <= SLC_LANES and T <= tp
    ab_spec = pl.BlockSpec((None, None, ncmp, 2 * NSA_HD), lambda b, h: (b, h, 0, 0))
    win_spec = pl.BlockSpec((wb * NSA_KVH, NSA_HD), lambda b, h: (b, 0))
    o_spec = pl.BlockSpec((None, None, rows, NSA_HD), lambda b, h: (b, h, 0, 0))
    return pl.pallas_call(
        functools.partial(_nsa_sample_main_body, T=T, n_slc=n_slc),
        grid=(B, NSA_KVH),
        in_specs=[ab_spec, ab_spec,
                  pl.BlockSpec((2, 1, NSA_HD), lambda b, h: (0, 0, 0)),
                  pl.BlockSpec((2, NSA_HD, NSA_HD), lambda b, h: (0, 0, 0)),
                  pl.BlockSpec((tp, gw), lambda b, h: (b, EVEN_A["qb"] // gw + h)),
                  win_spec, win_spec,
                  pl.BlockSpec((tp, NSA_HD), lambda b, h: (b, h)),
                  pl.BlockSpec((tp, NSA_HD), lambda b, h: (b, h)),
                  pl.BlockSpec((None, rows, ncmp), lambda b, h: (h, 0, 0)),
                  pl.BlockSpec((None, rows, wlen), lambda b, h: (h, 0, 0))],
        out_specs=[o_spec, o_spec, pl.BlockSpec((None, None, tp, LANES), lambda b, h: (b, h, 0, 0))],
        out_shape=[jax.ShapeDtypeStruct((B, NSA_KVH, rows, NSA_HD), F32),
                   jax.ShapeDtypeStruct((B, NSA_KVH, rows, NSA_HD), F32),
                   jax.ShapeDtypeStruct((B, NSA_KVH, tp, LANES), jnp.int32)],
        compiler_params=_params(("parallel", "parallel")),
        name="nsa_sample_main",
    )(abk, abv, b1.reshape(2, 1, NSA_HD), w2.astype(BF16), ya,
      wk.reshape(B * wb * NSA_KVH, NSA_HD), wv.reshape(B * wb * NSA_KVH, NSA_HD), kw16, vw16, bias_c, bias_w)


NEAR_BLOCKS = 3


def _nsa_sample_sel_body(idx_ref, pt_ref, q_ref, kn_ref, vn_ref, tbl_ref, ocmp_ref, owin_ref, gb_ref, bg_ref, zb_ref,
                         *refs, T):
    del pt_ref
    k_blocks = refs[:N_SEL]
    v_blocks = refs[N_SEL:2 * N_SEL]
    o_ref, osel = refs[2 * N_SEL:]
    tp = SAMPLE_PAD_T
    rows = NSA_G * tp
    b, h, t = pl.program_id(0), pl.program_id(1), pl.program_id(2)
    base = ((b * NSA_KVH + h) * T + t) * N_SEL
    first_new = PAST_LEN // SEL_BLOCK
    cur = jnp.right_shift(PAST_LEN + t, SEL_SHIFT)
    q = _sample_q_rows(q_ref)
    pad = jnp.zeros((SEL_BLOCK - tp, NSA_HD), BF16)
    k_new = jnp.concatenate([kn_ref[...], pad], axis=0)
    v_new = jnp.concatenate([vn_ref[...], pad], axis=0)
    lane = _iota2((1, LANES), 1)
    low = lane < SEL_BLOCK
    within = jnp.bitwise_and(lane, SEL_BLOCK - 1)
    ks, vs, bias, kpos = [], [], [], []
    for i in range(0, N_SEL, 2):
        pair_bias, pair_pos = [], []
        for j in (i, i + 1):
            blk = idx_ref[base + j]
            is_new = blk >= first_new
            ks.append(jnp.where(is_new, k_new, k_blocks[j][pl.ds(h, SEL_BLOCK, stride=NSA_KVH), :].astype(BF16)))
            vs.append(jnp.where(is_new, v_new, v_blocks[j][pl.ds(h, SEL_BLOCK, stride=NSA_KVH), :].astype(BF16)))
            pair_bias.append(tbl_ref[jnp.clip(blk - (first_new - NEAR_BLOCKS), 0, NEAR_BLOCKS)])
            pair_pos.append(jnp.where(blk <= cur, blk * SEL_BLOCK, PAST_LEN + SEL_BLOCK * LANES) + within)
        bias.append(jnp.where(low, pair_bias[0], pair_bias[1]))
        kpos.append(jnp.where(low, pair_pos[0], pair_pos[1]))
    k_all = jnp.concatenate(ks, axis=0)
    v_all = jnp.concatenate(vs, axis=0)
    step = jnp.bitwise_and(_iota2((rows, 1), 0), tp - 1)
    ok = jnp.concatenate(kpos, axis=1) <= PAST_LEN + step
    p, l = _softmax_rows(jnp.where(ok, _dot_nt(q, k_all) + jnp.concatenate(bias, axis=1), NEG_INF))
    o = _dot(p.astype(BF16), v_all) / jnp.maximum(l, TINY)

    @pl.when(t == 0)
    def _():
        osel[...] = jnp.zeros_like(osel)

    osel[...] = jnp.where(step == t, o, osel[...])

    @pl.when(t == T - 1)
    def _():
        gate = jax.nn.sigmoid(gb_ref[...] + bg_ref[...])
        zb = _silu(zb_ref[...])
        for g in range(NSA_G):
            r = slice(g * tp, (g + 1) * tp)
            head = h * NSA_G + g
            mix = (_lane_col(gate, head) * ocmp_ref[r, :] + _lane_col(gate, NSA_HEADS + head) * osel[r, :]
                   + _lane_col(gate, 2 * NSA_HEADS + head) * owin_ref[r, :])
            sl = slice(g * NSA_HD, (g + 1) * NSA_HD)
            o_ref[:, sl] = (mix * zb[:, sl]).astype(o_ref.dtype)


def _nsa_sample_sel_call(ya, yb, ks16, vs16, idx, page_table, pool_k, pool_v, tbl, o_cmp, o_win, bg_r, *, B, T):
    tp = SAMPLE_PAD_T
    rows = NSA_G * tp
    gw = NSA_G * NSA_HD
    n_pages = page_table.shape[1]
    halves = PAGE_SIZE // SEL_BLOCK
    idx_flat = idx[:, :, :T, :N_SEL].reshape(-1)
    view_k, view_v = _pool_rows(pool_k), _pool_rows(pool_v)

    def blk_spec(j):
        def index(b, h, t, idx_s, pt_s):
            blk = idx_s[((b * NSA_KVH + h) * T + t) * N_SEL + j]
            page = pt_s[b * n_pages + jnp.minimum(blk // halves, n_pages - 1)]
            return (page * halves + blk % halves, 0)
        return pl.BlockSpec((SEL_BLOCK * NSA_KVH, NSA_HD), index)

    o_spec = pl.BlockSpec((None, None, rows, NSA_HD), lambda b, h, t, *_: (b, h, 0, 0))
    grid_spec = pltpu.PrefetchScalarGridSpec(
        num_scalar_prefetch=2,
        grid=(B, NSA_KVH, T),
        in_specs=[pl.BlockSpec((tp, gw), lambda b, h, t, *_: (b, EVEN_A["qb"] // gw + h)),
                  pl.BlockSpec((tp, NSA_HD), lambda b, h, t, *_: (b, h)),
                  pl.BlockSpec((tp, NSA_HD), lambda b, h, t, *_: (b, h)),
                  pl.BlockSpec((None, NEAR_BLOCKS + 1, rows, LANES), lambda b, h, t, *_: (h, 0, 0, 0)),
                  o_spec, o_spec,
                  pl.BlockSpec((tp, LANES), lambda b, h, t, *_: (b, EVEN_B["gb"] // LANES)),
                  pl.BlockSpec((1, LANES), lambda b, h, t, *_: (0, 0)),
                  pl.BlockSpec((tp, gw), lambda b, h, t, *_: (b, EVEN_B["zb"] // gw + h))]
        + [blk_spec(j) for j in range(N_SEL)] * 2,
        out_specs=pl.BlockSpec((tp, gw), lambda b, h, t, *_: (b, h)),
        scratch_shapes=[pltpu.VMEM((rows, NSA_HD), F32)],
    )
    return pl.pallas_call(
        functools.partial(_nsa_sample_sel_body, T=T),
        grid_spec=grid_spec,
        out_shape=jax.ShapeDtypeStruct((B * tp, NSA_W), BF16),
        compiler_params=_params(("arbitrary", "arbitrary", "arbitrary")),
        name="nsa_sample_sel",
    )(idx_flat, page_table.reshape(-1), ya, ks16, vs16, tbl, o_cmp, o_win, yb, bg_r, yb,
      *([view_k] * N_SEL), *([view_v] * N_SEL))


def _sample_bias_tables(rel_bias, T, wb):
    tp = SAMPLE_PAD_T
    ncmp = PAST_LEN // CMP_STRIDE
    wlen = -(-(wb + tp) // LANES) * LANES
    first = PAST_LEN // SEL_BLOCK - NEAR_BLOCKS
    assert PAST_LEN - ((first + 1) * SEL_BLOCK - 1) >= REL_MAX_DIST
    lo, hi = -wlen, PAST_LEN + tp
    rev = _bias_line(rel_bias, lo, hi, descending=True)

    def rows(tbl):
        return tbl.reshape(NSA_KVH, NSA_G * tp, tbl.shape[-1])

    t_c = _toeplitz(rev, hi - 1 - (PAST_LEN - (CMP_BLOCK - 1)), tp, CMP_STRIDE * ncmp)[:, :, ::CMP_STRIDE]
    t_w = _toeplitz(rev, hi - 1 - wb, tp, wlen)
    far = jnp.broadcast_to(rev[:, hi - 1 - REL_MAX_DIST][:, None, None], (NSA_HEADS, tp, LANES))
    near = []
    for k in range(1, NEAR_BLOCKS + 1):
        half = _toeplitz(rev, hi - 1 - (PAST_LEN - (first + k) * SEL_BLOCK), tp, SEL_BLOCK)
        near.append(jnp.concatenate([half, half], axis=-1))
    t_s = jnp.stack([far] + near, axis=1).reshape(NSA_KVH, NSA_G, NEAR_BLOCKS + 1, tp, LANES)
    t_s = t_s.transpose(0, 2, 1, 3, 4).reshape(NSA_KVH, NEAR_BLOCKS + 1, NSA_G * tp, LANES)
    return rows(t_c), rows(t_w), t_s


def _tail_even(w):
    return _tail_relayout(w, EVEN_KV_OFF + 6 * NSA_KV_W, 3 * NSA_HEADS, NSA_W + MEM_W, EVEN_B_N)


def _tail_odd(w):
    return _tail_relayout(w, ODD_A_N, 2 * ML_HEADS, ML_V_W + MEM_W, ODD_B_N)


def _gate_bias_even(b_gate):
    return jnp.pad(b_gate, (0, LANES - 3 * NSA_HEADS)).reshape(1, LANES)


def _gate_bias_odd(b_if):
    return jnp.pad(b_if.reshape(2 * ML_HEADS), (0, LANES - 2 * ML_HEADS)).reshape(1, LANES)


def _rel_bucket(dist):
    n = np.maximum(dist, 0)
    exact = REL_BUCKETS // 2
    nf = np.maximum(n, 1).astype(np.float32)
    large = exact + (np.log(nf / exact) / math.log(REL_MAX_DIST / exact) * (REL_BUCKETS - exact)).astype(np.int32)
    return np.where(n < exact, n, np.minimum(large, REL_BUCKETS - 1))


def _bias_line(rel_bias, lo, hi, descending=False):
    dist = np.arange(hi - 1, lo - 1, -1) if descending else np.arange(lo, hi)
    buckets = _rel_bucket(dist)
    edges = np.flatnonzero(np.diff(buckets)) + 1
    starts = np.concatenate([[0], edges])
    ends = np.concatenate([edges, [hi - lo]])
    bias_t = rel_bias.T.astype(F32)
    runs = [jnp.broadcast_to(bias_t[:, int(buckets[s])][:, None], (NSA_HEADS, int(e - s))) for s, e in zip(starts, ends)]
    return jnp.concatenate(runs, axis=1)


def _skew_rows(v, rows, step, cols):
    n = v.shape[1]
    reps = -(-rows * (n + step) // n)
    return jnp.tile(v, (1, reps))[:, :rows * (n + step)].reshape(v.shape[0], rows, n + step)[:, :, :cols]


def _toeplitz(rev, start, rows, cols):
    seg = rev[:, start - (rows - 1):start + cols]
    return _skew_rows(jnp.roll(seg, -(rows - 1), axis=1), rows, -1, cols)


def _prompt_bias_tables(rel_bias, T):
    ncmp = T // CMP_STRIDE
    assert Q_BLOCK + 1 >= REL_MAX_DIST
    lo, hi = -(CMP_STRIDE * ncmp + CMP_BLOCK), T
    line = _bias_line(rel_bias, lo, hi)
    rev = _bias_line(rel_bias, lo, hi, descending=True)

    def split(tbl):
        return tbl.reshape((NSA_KVH, NSA_G) + tbl.shape[1:])

    back = CMP_STRIDE * (ncmp - 1)
    first = -(back + CMP_BLOCK - 1) - lo
    seg = line[:, first:first + T + back]
    t_c = _skew_rows(jnp.roll(seg, -back, axis=1), ncmp, -CMP_STRIDE, T).swapaxes(1, 2)
    far = rev[:, hi - 1 - REL_MAX_DIST]
    t_near = _toeplitz(rev, hi - 1 - REL_MAX_DIST, Q_BLOCK, NEAR_COLS) - far[:, None, None]
    return split(t_c), split(t_near)


def _nsa_sample(ya, yb, kv16, page_table, pk_cmp, pv_cmp, pk_sel, pv_sel, wk, wv, bg_r, w1, b1, w2, pe, rel_bias,
                *, B, T):
    assert (PAST_LEN + T) // CMP_STRIDE == PAST_LEN // CMP_STRIDE
    abk = _cmp_pages_call(pk_cmp, page_table, w1[0], pe[0], B=B)
    abv = _cmp_pages_call(pv_cmp, page_table, w1[1], pe[1], B=B)
    bias_c, bias_w, tbl = _sample_bias_tables(rel_bias, T, wk.shape[1])
    o_cmp, o_win, idx = _nsa_sample_main_call(ya, kv16[4], kv16[5], abk, abv, b1, w2, wk, wv, bias_c, bias_w, B=B, T=T)
    return _nsa_sample_sel_call(ya, yb, kv16[2], kv16[3], idx, page_table, pk_sel, pv_sel, tbl, o_cmp, o_win, bg_r,
                                B=B, T=T)


def _kv_project(x, wt):
    outs = [_matmul_heads(x, wt, first=EVEN_KV_OFF + j * NSA_KV_W, transposed=True) for j in range(6)]
    return [o[0] for o in outs], [o[1] for o in outs]


def _even_prompt(hp2d, npre, mk16, mv16, wt, wt_b, bg_r, w1, b1, w2, pe, lb, g_norm, w_out, rel_bias, *, B, T):
    ya, yb = _matmul_nt(npre, wt, tm=W_TILE_M, tn=W_TILE_N, rows=(0, EVEN_A_N)), _matmul_nt(npre, wt_b)
    kv32, kv16 = _kv_project(npre, wt)
    oa, s_new = _hgrn_call(ya, jnp.zeros((B, HG_HEADS, HG_DK, HG_DV), F32), lb, g_norm, B=B, T=T, L=CHUNK, valid=CHUNK)
    kcmp = _compress_call(kv16[0], w1[0], b1[0], w2[0], pe[0], B=B, T=T)
    vcmp = _compress_call(kv16[1], w1[1], b1[1], w2[1], pe[1], B=B, T=T)
    ob = _nsa_prompt_call(ya, yb, kv16[2:], kcmp, vcmp, bg_r, *_prompt_bias_tables(rel_bias, T), B=B, T=T)
    om = _mem_call(yb, EVEN_B["qm"], mk16, mv16, B=B, T=T)
    h_new = _outproj([oa, ob, om], w_out, hp2d)
    wb = min(WINDOW, T)
    rows = [r.reshape(B, T, NSA_KVH, NSA_HD) for r in kv32]
    return h_new, (rows[0], rows[1], rows[2], rows[3], rows[4][:, -wb:], rows[5][:, -wb:], s_new)


def _even_sample(hs2d, nsam, mk_s, mv_s, page_table, pk_cmp, pv_cmp, pk_sel, pv_sel, wk, wv, s0,
                 wt, wt_b, bg_r, w1, b1, w2, pe, lb, g_norm, w_out, rel_bias, *, B, T):
    tp = SAMPLE_PAD_T
    ya, yb = _matmul_nt(nsam, wt, tm=W_TILE_M, tn=W_TILE_N, rows=(0, EVEN_A_N)), _matmul_nt(nsam, wt_b)
    kv32, kv16 = _kv_project(nsam, wt)
    oa, s_new = _hgrn_call(ya, s0, lb, g_norm, B=B, T=tp, L=tp, valid=T)
    ob = _nsa_sample(ya, yb, kv16, page_table, pk_cmp, pv_cmp, pk_sel, pv_sel, wk, wv, bg_r, w1, b1, w2, pe, rel_bias,
                     B=B, T=T)
    om = _mem_call(yb, EVEN_B["qm"], mk_s.reshape(B * N_MEM, MEM_W), mv_s.reshape(B * N_MEM, MEM_W), B=B, T=tp)
    rows = [r.reshape(B, tp, NSA_KVH, NSA_HD)[:, :T] for r in kv32]
    wb = wk.shape[1]
    win_k = jnp.concatenate([wk, rows[4]], axis=1)[:, -wb:]
    win_v = jnp.concatenate([wv, rows[5]], axis=1)[:, -wb:]
    return _outproj([oa, ob, om], w_out, hs2d), (rows[0], rows[1], rows[2], rows[3], win_k, win_v, s_new)


def _odd_mix(h2d, hn, k2d, v2d, c0, n0, m0, wt, wt_b, bif_r, g_norm, w_out, *, B, T, L, valid):
    ya, yb = _matmul_nt(hn, wt, tm=W_TILE_M, tn=W_TILE_N, rows=(0, ODD_A_N)), _matmul_nt(hn, wt_b)
    h, c_new, n_new, m_new = _mlstm_call(ya, yb, c0, n0, m0, bif_r, g_norm, B=B, T=T, L=L, valid=valid)
    om = _mem_call(yb, ODD_B["qm"], k2d, v2d, B=B, T=T)
    return _outproj([h, om], w_out, h2d), (c_new, n_new, m_new)


def _stack(lst, i):
    return jnp.stack([t[i] for t in lst])


def kernel(x_prompt, x_sample, cache_mem_k, cache_mem_v, cache_cmp_k, cache_cmp_v, cache_sel_k, cache_sel_v,
           cache_win_k, cache_win_v, state_hgrn, state_mlstm_c, state_mlstm_n, state_mlstm_m, page_table,
           mem_prompt, norm_w, mem_norm_w, final_norm_w, rel_bias, w_mem_kv, w_in_even, b_nsa_gate,
           w_cmp1, b_cmp1, w_cmp2, pe_cmp, hgrn_lb_logits, hgrn_norm_w, w_out_even, w_in_odd, b_mlstm_if,
           mlstm_norm_w, w_out_odd):
    bp, tp = x_prompt.shape[:2]
    bs, ts = x_sample.shape[:2]
    tsp = SAMPLE_PAD_T
    lbs = jnp.cumsum(jax.nn.softmax(hgrn_lb_logits.astype(F32), axis=0), axis=0)
    hp = x_prompt.reshape(bp * tp, D_MODEL)
    hs = jnp.pad(x_sample, ((0, 0), (0, tsp - ts), (0, 0))).reshape(bs * tsp, D_MODEL)
    mem2d = mem_prompt.reshape(bp * N_MEM, D_MODEL)
    mem_new, even_p, even_s, odd_p, odd_s = [], [], [], [], []
    for l in range(DEPTH):
        npre = _rmsnorm_rows(hp, norm_w[l], BF16)
        nsam = _rmsnorm_rows(hs, norm_w[l], BF16)
        nmem = _rmsnorm_rows(mem2d, mem_norm_w[l], BF16)
        mk32, mk16 = _matmul_heads(nmem, w_mem_kv[l], first=0)
        mv32, mv16 = _matmul_heads(nmem, w_mem_kv[l], first=MEM_W)
        mem_new.append((mk32.reshape(bp, N_MEM, MEM_HEADS, MEM_HD), mv32.reshape(bp, N_MEM, MEM_HEADS, MEM_HD)))
        mk_s, mv_s = cache_mem_k[l], cache_mem_v[l]
        if l % 2 == 0:
            e = l // 2
            w_in = w_in_even[e].T
            w_b = _tail_even(w_in)
            w_out = w_out_even[e].astype(BF16)
            bg_r = _gate_bias_even(b_nsa_gate[e])
            cmpw = (w_cmp1[e].reshape(2, CMP_BLOCK, NSA_HD, NSA_HD), b_cmp1[e], w_cmp2[e], pe_cmp[e])
            hp, st_p = _even_prompt(hp, npre, mk16, mv16, w_in, w_b, bg_r, *cmpw, lbs[l], hgrn_norm_w[e], w_out,
                                    rel_bias, B=bp, T=tp)
            hs, st_s = _even_sample(hs, nsam, mk_s, mv_s, page_table, cache_cmp_k[e], cache_cmp_v[e], cache_sel_k[e],
                                    cache_sel_v[e], cache_win_k[e], cache_win_v[e], state_hgrn[e], w_in, w_b, bg_r,
                                    *cmpw, lbs[l], hgrn_norm_w[e], w_out, rel_bias, B=bs, T=ts)
            even_p.append(st_p)
            even_s.append(st_s)
        else:
            o = l // 2
            w_in = w_in_odd[o].T
            w_b = _tail_odd(w_in)
            w_out = w_out_odd[o].astype(BF16)
            bif_r = _gate_bias_odd(b_mlstm_if[o])
            hp, st_p = _odd_mix(hp, npre, mk16, mv16, jnp.zeros((bp, ML_HEADS, ML_DV, ML_DK), F32),
                                jnp.zeros((bp, ML_HEADS, ML_DK), F32), jnp.zeros((bp, ML_HEADS), F32),
                                w_in, w_b, bif_r, mlstm_norm_w[o], w_out, B=bp, T=tp, L=ML_CHUNK, valid=ML_CHUNK)
            hs, st_s = _odd_mix(hs, nsam, mk_s.reshape(bs * N_MEM, MEM_W), mv_s.reshape(bs * N_MEM, MEM_W),
                                state_mlstm_c[o], state_mlstm_n[o], state_mlstm_m[o],
                                w_in, w_b, bif_r, mlstm_norm_w[o], w_out, B=bs, T=tsp, L=tsp, valid=ts)
            odd_p.append(st_p)
            odd_s.append(st_s)
    y_prompt = _rmsnorm_rows(hp, final_norm_w, F32).reshape(bp, tp, D_MODEL)
    y_sample = _rmsnorm_rows(hs, final_norm_w, F32).reshape(bs, tsp, D_MODEL)[:, :ts]
    return (y_prompt, y_sample,
            _stack(mem_new, 0), _stack(mem_new, 1),
            _stack(even_p, 0), _stack(even_p, 1), _stack(even_p, 2), _stack(even_p, 3),
            _stack(even_p, 4), _stack(even_p, 5), _stack(even_p, 6),
            _stack(odd_p, 0), _stack(odd_p, 1), _stack(odd_p, 2),
            _stack(even_s, 0), _stack(even_s, 1), _stack(even_s, 2), _stack(even_s, 3),
            _stack(even_s, 4), _stack(even_s, 5), _stack(even_s, 6),
            _stack(odd_s, 0), _stack(odd_s, 1), _stack(odd_s, 2))
```

```python
import functools
import math

import jax
import jax.numpy as jnp
import numpy as np
from jax import lax
from jax.experimental import pallas as pl
from jax.experimental.pallas import tpu as pltpu

D_MODEL = 4096
DEPTH = 2
PAST_LEN = 16384
PAGE_SIZE = 128
N_MEM = 256
EPS = 1e-6
CHUNK = 64

HG_DK = 128
HG_DV = 128
HG_HEADS = D_MODEL // 2 // HG_DV
HG_W = HG_HEADS * HG_DV

NSA_HD = 128
NSA_HEADS = D_MODEL // 2 // NSA_HD
NSA_KVH = 4
NSA_G = NSA_HEADS // NSA_KVH
NSA_W = NSA_HEADS * NSA_HD
NSA_KV_W = NSA_KVH * NSA_HD
CMP_BLOCK = 32
CMP_STRIDE = 16
SEL_BLOCK = 64
SEL_SHIFT = SEL_BLOCK.bit_length() - 1
N_SEL = 16
WINDOW = 512
Q_BLOCK = 256

ML_HEADS = D_MODEL // 512
ML_DK = D_MODEL // 2 // ML_HEADS
ML_DV = D_MODEL // ML_HEADS
ML_QK_W = ML_HEADS * ML_DK
ML_V_W = ML_HEADS * ML_DV

MEM_HEADS = 4
MEM_HD = 128
MEM_W = MEM_HEADS * MEM_HD

REL_BUCKETS = 32
REL_MAX_DIST = 128

F32 = jnp.float32
BF16 = jnp.bfloat16
LANES = 128
NEG_INF = float("-inf")
TINY = float(np.finfo(np.float32).tiny)
EXP_CLAMP = 80.0
VMEM_LIMIT = 56 * 1024 * 1024

HG_HB = 16
ML_HB = 4
ML_CHUNK = 256
W_TILE_M, W_TILE_N = 1024, 512
HG_SUB = 16
SAMPLE_PAD_T = 16

MM_TILE_N = 1024
EVEN_A = {"qa": 0, "fa": HG_W, "ia": 2 * HG_W, "za": 3 * HG_W, "qb": 4 * HG_W}
EVEN_A_N = 4 * HG_W + NSA_W
EVEN_B = {"zb": 0, "qm": NSA_W, "gb": NSA_W + MEM_W}
EVEN_B_N = -(-(NSA_W + MEM_W + LANES) // MM_TILE_N) * MM_TILE_N
EVEN_KV_OFF = EVEN_A_N
ODD_A = {"q": 0, "k": ML_QK_W, "v": 2 * ML_QK_W, "og": 2 * ML_QK_W + ML_V_W}
ODD_A_N = 2 * ML_QK_W + 2 * ML_V_W
ODD_B = {"z": 0, "qm": ML_V_W, "gates": ML_V_W + MEM_W}
ODD_B_N = -(-(ML_V_W + MEM_W + LANES) // MM_TILE_N) * MM_TILE_N


def _dot(a, b):
    return jnp.dot(a, b, preferred_element_type=F32)


def _dot_nt(a, b):
    return lax.dot_general(a, b, (((1,), (1,)), ((), ())), preferred_element_type=F32)


def _dot_tn(a, b):
    return lax.dot_general(a, b, (((0,), (0,)), ((), ())), preferred_element_type=F32)


def _iota2(shape, dim):
    return lax.broadcasted_iota(jnp.int32, shape, dim)


def _cumsum_rows(x, tri_b):
    hi = x.astype(BF16)
    r1 = x - hi.astype(F32)
    mid = r1.astype(BF16)
    lo = (r1 - mid.astype(F32)).astype(BF16)
    return _dot(tri_b, hi) + _dot(tri_b, mid) + _dot(tri_b, lo)


def _row_to_col(row, n):
    eye = _iota2((n, n), 0) == _iota2((n, n), 1)
    return jnp.sum(jnp.where(eye, row, 0.0), axis=1, keepdims=True)


def _col_to_row(col, n):
    eye = _iota2((n, n), 0) == _iota2((n, n), 1)
    return jnp.sum(jnp.where(eye, col, 0.0), axis=0, keepdims=True)


def _lane_col(x, idx):
    return jnp.sum(jnp.where(_iota2(x.shape, 1) == idx, x, 0.0), axis=1, keepdims=True)


def _silu(x):
    return x * jax.nn.sigmoid(x)


def _params(sem):
    return pltpu.CompilerParams(dimension_semantics=sem, vmem_limit_bytes=VMEM_LIMIT)


def _rmsnorm_body(x_ref, w_ref, o_ref):
    x = x_ref[...].astype(F32)
    y = x * lax.rsqrt(jnp.mean(x * x, axis=-1, keepdims=True) + EPS)
    o_ref[...] = (y * w_ref[...].astype(F32)).astype(o_ref.dtype)


def _rmsnorm_rows(x2d, w, out_dtype, tm=256):
    m, d = x2d.shape
    tm = min(tm, m)
    return pl.pallas_call(
        _rmsnorm_body,
        grid=(m // tm,),
        in_specs=[pl.BlockSpec((tm, d), lambda i: (i, 0)), pl.BlockSpec((1, d), lambda i: (0, 0))],
        out_specs=pl.BlockSpec((tm, d), lambda i: (i, 0)),
        out_shape=jax.ShapeDtypeStruct((m, d), out_dtype),
        compiler_params=_params(("parallel",)),
        name="rmsnorm",
    )(x2d, w.reshape(1, d))


def _matmul_nt_body(a_ref, bt_ref, o_ref):
    o_ref[...] = _dot_nt(a_ref[...], bt_ref[...].astype(BF16))


def _matmul_nt(a, bt, tm=1024, tn=MM_TILE_N, rows=None):
    m, k = a.shape
    first, n = rows or (0, bt.shape[0])
    tm, tn = min(tm, m), min(tn, n)
    assert m % tm == 0 and n % tn == 0 and first % tn == 0, (a.shape, bt.shape, rows)
    j0 = first // tn
    return pl.pallas_call(
        _matmul_nt_body,
        grid=(m // tm, n // tn),
        in_specs=[pl.BlockSpec((tm, k), lambda i, j: (i, 0)), pl.BlockSpec((tn, k), lambda i, j: (j0 + j, 0))],
        out_specs=pl.BlockSpec((tm, tn), lambda i, j: (i, j)),
        out_shape=jax.ShapeDtypeStruct((m, n), F32),
        compiler_params=_params(("parallel", "parallel")),
        name="matmul",
    )(a, bt)


def _tail_body(lo_ref, hi_ref, gate_ref, o_ref, *, shift, n_main):
    i = pl.program_id(0)
    main = jnp.concatenate([lo_ref[shift:, :], hi_ref[:shift, :]], axis=0)
    gates = jnp.where(_iota2((LANES, 1), 0) < shift, gate_ref[...], 0.0)
    o_ref[...] = jnp.where(i < n_main, main, jnp.where(i == n_main, gates, 0.0)).astype(o_ref.dtype)


def _tail_relayout(wt, first, shift, main, out_rows):
    n, k = wt.shape
    assert first % LANES == 0 and main % LANES == 0 and out_rows % LANES == 0 and shift % 8 == 0 and shift < LANES
    assert first + shift + main == n
    c0, n_main = first // LANES, main // LANES
    return pl.pallas_call(
        functools.partial(_tail_body, shift=shift, n_main=n_main),
        grid=(out_rows // LANES,),
        in_specs=[pl.BlockSpec((LANES, k), lambda i: (c0 + jnp.minimum(i, n_main - 1), 0)),
                  pl.BlockSpec((LANES, k), lambda i: (c0 + jnp.minimum(i, n_main - 1) + 1, 0)),
                  pl.BlockSpec((LANES, k), lambda i: (c0, 0))],
        out_specs=pl.BlockSpec((LANES, k), lambda i: (i, 0)),
        out_shape=jax.ShapeDtypeStruct((out_rows, k), BF16),
        compiler_params=_params(("parallel",)),
        name="tail_relayout",
    )(wt, wt, wt)


def _matmul_heads_body(a_ref, b_ref, o32_ref, o16_ref, *, transposed):
    b = b_ref[...].astype(BF16)
    acc = _dot_nt(a_ref[...], b) if transposed else _dot(a_ref[...], b)
    for h in range(MEM_HEADS):
        o32_ref[:, h, :] = acc[:, h * LANES:(h + 1) * LANES]
    o16_ref[...] = acc.astype(BF16)


def _matmul_heads(a, b, first=0, transposed=False, tm=1024):
    m, k = a.shape
    n = MEM_HEADS * LANES
    tm = min(tm, m)
    assert m % tm == 0 and first % n == 0, (a.shape, b.shape, first)
    j0 = first // n
    b_spec = pl.BlockSpec((n, k), lambda i: (j0, 0)) if transposed else pl.BlockSpec((k, n), lambda i: (0, j0))
    return pl.pallas_call(
        functools.partial(_matmul_heads_body, transposed=transposed),
        grid=(m // tm,),
        in_specs=[pl.BlockSpec((tm, k), lambda i: (i, 0)), b_spec],
        out_specs=[pl.BlockSpec((tm, MEM_HEADS, LANES), lambda i: (i, 0, 0)), pl.BlockSpec((tm, n), lambda i: (i, 0))],
        out_shape=[jax.ShapeDtypeStruct((m, MEM_HEADS, LANES), F32), jax.ShapeDtypeStruct((m, n), BF16)],
        compiler_params=_params(("parallel",)),
        name="matmul_heads",
    )(a, b)


def _outproj_body(*refs, widths):
    xs = refs[:len(widths)]
    w_ref, r_ref, o_ref = refs[len(widths):]
    acc = r_ref[...]
    off = 0
    for x_ref, w in zip(xs, widths):
        acc = acc + _dot(x_ref[...], w_ref[off:off + w, :])
        off += w
    o_ref[...] = acc


def _outproj(xs, w_bf16, resid, tm=1024, tn=512):
    m = resid.shape[0]
    n = w_bf16.shape[1]
    widths = tuple(x.shape[1] for x in xs)
    assert sum(widths) == w_bf16.shape[0]
    tm = min(tm, m)
    in_specs = [pl.BlockSpec((tm, w), lambda i, j: (i, 0)) for w in widths]
    in_specs += [pl.BlockSpec((w_bf16.shape[0], tn), lambda i, j: (0, j)), pl.BlockSpec((tm, tn), lambda i, j: (i, j))]
    return pl.pallas_call(
        functools.partial(_outproj_body, widths=widths),
        grid=(m // tm, n // tn),
        in_specs=in_specs,
        out_specs=pl.BlockSpec((tm, tn), lambda i, j: (i, j)),
        out_shape=jax.ShapeDtypeStruct((m, n), F32),
        compiler_params=_params(("parallel", "parallel")),
        name="outproj",
    )(*xs, w_bf16, resid)


def _hgrn_body(qa_ref, fa_ref, ia_ref, za_ref, lb_ref, gn_ref, s0_ref, o_ref, s_out, s_scr, *, L, valid):
    c = pl.program_id(2)

    @pl.when(c == 0)
    def _():
        s_scr[...] = s0_ref[...]

    lb = lb_ref[...]
    sig = jax.nn.sigmoid(fa_ref[...])
    logf = jnp.log(lb + (1.0 - lb) * sig)
    kk = (1.0 - lb) * (1.0 - sig)
    if valid < L:
        live = _iota2((L, 1), 0) < valid
        logf = jnp.where(live, logf, 0.0)
        kk = jnp.where(live, kk, 0.0)
    tri_b = (_iota2((L, L), 0) >= _iota2((L, L), 1)).astype(BF16)
    bc = _cumsum_rows(logf, tri_b)
    q = _silu(qa_ref[...])
    gate = _silu(za_ref[...])
    v = ia_ref[...]
    gn = gn_ref[...]
    nsub = L // HG_SUB
    rr = _iota2((L, nsub * L), 0)
    cc = _iota2((L, nsub * L), 1)
    keep = ((jnp.right_shift(cc, L.bit_length() - 1) == jnp.right_shift(rr, HG_SUB.bit_length() - 1))
            & (jnp.bitwise_and(cc, L - 1) <= rr))
    for j in range(HG_HB):
        sl = slice(j * HG_DK, (j + 1) * HG_DK)
        bj, qj, kj = bc[:, sl], q[:, sl], kk[:, sl]
        vb = v[:, sl].astype(BF16)
        s_prev = s_scr[j]
        inter = _dot((qj * jnp.exp(bj)).astype(BF16), s_prev.astype(BF16))
        mids = [bj[i * HG_SUB + HG_SUB // 2:i * HG_SUB + HG_SUB // 2 + 1, :] for i in range(nsub)]
        mid_rows = jnp.concatenate([jnp.broadcast_to(m, (HG_SUB, HG_DK)) for m in mids], axis=0)
        q_dec = qj * jnp.exp(jnp.minimum(bj - mid_rows, EXP_CLAMP))
        k_dec = jnp.concatenate([kj * jnp.exp(jnp.minimum(m - bj, EXP_CLAMP)) for m in mids], axis=0)
        att = jnp.where(keep, _dot_nt(q_dec.astype(BF16), k_dec.astype(BF16)), 0.0)
        o = inter + _dot(att.astype(BF16), jnp.concatenate([vb] * nsub, axis=0))
        o_n = o * lax.rsqrt(jnp.mean(o * o, axis=-1, keepdims=True) + EPS) * gn
        o_ref[:, sl] = (o_n * gate[:, sl]).astype(o_ref.dtype)
        bl = bj[L - 1:L, :]
        kd = kj * jnp.exp(bl - bj)
        s_scr[j] = _row_to_col(jnp.exp(bl), HG_DK) * s_prev + _dot_tn(kd.astype(BF16), vb)

    @pl.when(c == pl.num_programs(2) - 1)
    def _():
        s_out[...] = s_scr[...]


def _hgrn_call(y, s0, lb, gn, *, B, T, L, valid):
    nc = T // L
    w = HG_HB * HG_DK

    def col(name):
        blk = EVEN_A[name] // w
        return pl.BlockSpec((L, w), lambda b, hg, c: (b * nc + c, blk + hg))

    state_spec = pl.BlockSpec((None, HG_HB, HG_DK, HG_DV), lambda b, hg, c: (b, hg, 0, 0))
    return pl.pallas_call(
        functools.partial(_hgrn_body, L=L, valid=valid),
        grid=(B, HG_HEADS // HG_HB, nc),
        in_specs=[col("qa"), col("fa"), col("ia"), col("za"),
                  pl.BlockSpec((1, w), lambda b, hg, c: (0, hg)),
                  pl.BlockSpec((1, HG_DV), lambda b, hg, c: (0, 0)),
                  state_spec],
        out_specs=[pl.BlockSpec((L, w), lambda b, hg, c: (b * nc + c, hg)), state_spec],
        out_shape=[jax.ShapeDtypeStruct((B * T, HG_W), BF16),
                   jax.ShapeDtypeStruct((B, HG_HEADS, HG_DK, HG_DV), F32)],
        scratch_shapes=[pltpu.VMEM((HG_HB, HG_DK, HG_DV), F32)],
        compiler_params=_params(("arbitrary", "arbitrary", "arbitrary")),
        name="hgrn2",
    )(y, y, y, y, lb.reshape(1, HG_W), gn.reshape(1, HG_DV), s0)


def _mlstm_body(q_ref, k_ref, v_ref, og_ref, z_ref, g_ref, bif_ref, gn_ref, c0_ref, n0_ref, m0_ref,
                h_ref, c_out, n_out, m_out, c_scr, n_scr, m_scr, *, L, valid):
    c = pl.program_id(2)

    @pl.when(c == 0)
    def _():
        c_scr[...] = c0_ref[...]
        n_scr[...] = n0_ref[...]
        m_scr[...] = m0_ref[...]

    gates = g_ref[...] + bif_ref[...]
    log_i = gates
    log_f = jnp.minimum(gates, 0.0) - jnp.log(1.0 + jnp.exp(-jnp.abs(gates)))
    if valid < L:
        live = _iota2((L, 1), 0) < valid
        log_i = jnp.where(live, log_i, -1e30)
        log_f = jnp.where(live, log_f, 0.0)
    tri = _iota2((L, L), 0) >= _iota2((L, L), 1)
    bcs = _cumsum_rows(log_f, tri.astype(BF16))
    for j in range(ML_HB):
        head = pl.program_id(1) * ML_HB + j
        b_col = _lane_col(bcs, ML_HEADS + head)
        i_col = _lane_col(log_i, head)
        b_row = _col_to_row(b_col, L)
        i_row = _col_to_row(i_col, L)
        m_prev = m_scr[:, j:j + 1]
        dmat = jnp.where(tri, b_col - b_row + i_row, NEG_INF)
        inter = b_col + m_prev
        mt = jnp.maximum(inter, jnp.max(dmat, axis=1, keepdims=True))
        w_in = jnp.exp(dmat - mt)
        w_x = jnp.exp(inter - mt)
        qj = q_ref[:, j * ML_DK:(j + 1) * ML_DK]
        kj = k_ref[:, j * ML_DK:(j + 1) * ML_DK] * (ML_DK ** -0.5)
        vj = v_ref[:, j * ML_DV:(j + 1) * ML_DV]
        qb, kb = qj.astype(BF16), kj.astype(BF16)
        sw = _dot_nt(qb, kb) * w_in
        c_prev = c_scr[j]
        n_prev = n_scr[:, j * ML_DK:(j + 1) * ML_DK]
        num = w_x * _dot_nt(qb, c_prev.astype(BF16)) + _dot(sw.astype(BF16), vj.astype(BF16))
        den = w_x * jnp.sum(qj * n_prev, axis=1, keepdims=True) + jnp.sum(sw, axis=1, keepdims=True)
        h = num / jnp.maximum(jnp.abs(den), jnp.exp(-mt))
        m_last = mt[L - 1:L, :]
        b_last = b_col[L - 1:L, :]
        w_end = jnp.exp(b_last - b_col + i_col - m_last)
        d_c = jnp.exp(b_last + m_prev - m_last)
        c_scr[j] = d_c * c_prev + _dot_tn((w_end * vj).astype(BF16), kb)
        n_scr[:, j * ML_DK:(j + 1) * ML_DK] = d_c * n_prev + jnp.sum(w_end * kj, axis=0, keepdims=True)
        m_scr[:, j:j + 1] = m_last
        sv = slice(j * ML_DV, (j + 1) * ML_DV)
        h_n = h * lax.rsqrt(jnp.mean(h * h, axis=-1, keepdims=True) + EPS) * gn_ref[:, sv]
        h_ref[:, sv] = (h_n * jax.nn.sigmoid(og_ref[:, sv]) * _silu(z_ref[:, sv])).astype(h_ref.dtype)

    @pl.when(c == pl.num_programs(2) - 1)
    def _():
        c_out[...] = c_scr[...]
        n_out[...] = n_scr[...]
        m_out[...] = m_scr[...]


def _mlstm_call(ya, yb, c0, n0, m0, bif_r, gn, *, B, T, L, valid):
    nc = T // L
    ng = ML_HEADS // ML_HB
    wk, wv = ML_HB * ML_DK, ML_HB * ML_DV

    def col(name, w):
        blk = (ODD_A[name] if name in ODD_A else ODD_B[name]) // w
        return pl.BlockSpec((L, w), lambda b, hg, c: (b * nc + c, blk + hg))

    c_spec = pl.BlockSpec((None, ML_HB, ML_DV, ML_DK), lambda b, hg, c: (b, hg, 0, 0))
    n_spec = pl.BlockSpec((None, 1, wk), lambda b, hg, c: (b, 0, hg))
    m_spec = pl.BlockSpec((None, None, 1, LANES), lambda b, hg, c: (b, hg, 0, 0))
    m0_r = jnp.pad(m0.reshape(B, ng, 1, ML_HB), ((0, 0), (0, 0), (0, 0), (0, LANES - ML_HB)))
    h, c_new, n_new, m_new = pl.pallas_call(
        functools.partial(_mlstm_body, L=L, valid=valid),
        grid=(B, ng, nc),
        in_specs=[col("q", wk), col("k", wk), col("v", wv), col("og", wv), col("z", wv),
                  pl.BlockSpec((L, LANES), lambda b, hg, c: (b * nc + c, ODD_B["gates"] // LANES)),
                  pl.BlockSpec((1, LANES), lambda b, hg, c: (0, 0)),
                  pl.BlockSpec((1, wv), lambda b, hg, c: (0, hg)),
                  c_spec, n_spec, m_spec],
        out_specs=[pl.BlockSpec((L, wv), lambda b, hg, c: (b * nc + c, hg)), c_spec, n_spec, m_spec],
        out_shape=[jax.ShapeDtypeStruct((B * T, ML_V_W), BF16),
                   jax.ShapeDtypeStruct((B, ML_HEADS, ML_DV, ML_DK), F32),
                   jax.ShapeDtypeStruct((B, 1, ML_QK_W), F32),
                   jax.ShapeDtypeStruct((B, ng, 1, LANES), F32)],
        scratch_shapes=[pltpu.VMEM((ML_HB, ML_DV, ML_DK), F32), pltpu.VMEM((1, wk), F32), pltpu.VMEM((1, LANES), F32)],
        compiler_params=_params(("arbitrary", "arbitrary", "arbitrary")),
        name="mlstm",
    )(ya, ya, ya, ya, yb, yb, bif_r, gn.reshape(1, ML_V_W), c0, n0.reshape(B, 1, ML_QK_W), m0_r)
    return h, c_new, n_new.reshape(B, ML_HEADS, ML_DK), m_new[:, :, 0, :ML_HB].reshape(B, ML_HEADS)


def _mem_body(q_ref, k_ref, v_ref, o_ref):
    q = q_ref[...] * (MEM_HD ** -0.5)
    for h in range(MEM_HEADS):
        sl = slice(h * MEM_HD, (h + 1) * MEM_HD)
        s = _dot_nt(q[:, sl].astype(BF16), k_ref[:, sl].astype(BF16))
        p = jnp.exp(s - jnp.max(s, axis=-1, keepdims=True))
        o = _dot(p.astype(BF16), v_ref[:, sl].astype(BF16)) / jnp.sum(p, axis=-1, keepdims=True)
        o_ref[:, sl] = o.astype(o_ref.dtype)


def _mem_call(y, q_off, k2d, v2d, *, B, T, tq=256):
    tq = min(tq, T)
    nq = T // tq
    qb = q_off // MEM_W
    return pl.pallas_call(
        _mem_body,
        grid=(B, nq),
        in_specs=[pl.BlockSpec((tq, MEM_W), lambda b, i: (b * nq + i, qb)),
                  pl.BlockSpec((N_MEM, MEM_W), lambda b, i: (b, 0)),
                  pl.BlockSpec((N_MEM, MEM_W), lambda b, i: (b, 0))],
        out_specs=pl.BlockSpec((tq, MEM_W), lambda b, i: (b * nq + i, 0)),
        out_shape=jax.ShapeDtypeStruct((B * T, MEM_W), BF16),
        compiler_params=_params(("parallel", "parallel")),
        name="mem_attn",
    )(y, k2d, v2d)


def _gelu_tanh(x):
    return 0.5 * x * (1.0 + jnp.tanh(math.sqrt(2.0 / math.pi) * (x + 0.044715 * (x * x * x))))


def _compress_body(x_ref, w1_ref, b1_ref, w2_ref, pe_ref, o_ref, x32, *, nch):
    x32[...] = x_ref[...].astype(F32)
    a = jnp.zeros((nch, NSA_HD), F32)
    b = jnp.zeros((nch, NSA_HD), F32)
    for s in range(CMP_STRIDE):
        r = x32[pl.ds(s, nch, stride=CMP_STRIDE), :]
        a = a + _dot((r + pe_ref[s:s + 1, :]).astype(BF16), w1_ref[s])
        b = b + _dot((r + pe_ref[CMP_STRIDE + s:CMP_STRIDE + s + 1, :]).astype(BF16), w1_ref[CMP_STRIDE + s])
    h = a + pltpu.roll(b, nch - 1, 0) + b1_ref[...]
    o_ref[...] = _dot(_gelu_tanh(h).astype(BF16), w2_ref[...])


def _compress_call(x16, w1, b1, w2, pe, *, B, T):
    nch = T // CMP_STRIDE
    return pl.pallas_call(
        functools.partial(_compress_body, nch=nch),
        grid=(B, NSA_KVH),
        in_specs=[pl.BlockSpec((T, NSA_HD), lambda b, h: (b, h)),
                  pl.BlockSpec((CMP_BLOCK, NSA_HD, NSA_HD), lambda b, h: (0, 0, 0)),
                  pl.BlockSpec((1, NSA_HD), lambda b, h: (0, 0)),
                  pl.BlockSpec((NSA_HD, NSA_HD), lambda b, h: (0, 0)),
                  pl.BlockSpec((CMP_BLOCK, NSA_HD), lambda b, h: (0, 0))],
        out_specs=pl.BlockSpec((None, None, nch, NSA_HD), lambda b, h: (b, h, 0, 0)),
        out_shape=jax.ShapeDtypeStruct((B, NSA_KVH, nch, NSA_HD), F32),
        scratch_shapes=[pltpu.VMEM((T, NSA_HD), F32)],
        compiler_params=_params(("parallel", "parallel")),
        name="nsa_compress",
    )(x16, w1.astype(BF16), b1.reshape(1, NSA_HD), w2.astype(BF16), pe)


def _softmax_rows(s):
    m = jnp.max(s, axis=-1, keepdims=True)
    m = jnp.where(m == NEG_INF, 0.0, m)
    p = jnp.exp(s - m)
    return p, jnp.sum(p, axis=-1, keepdims=True)


def _slc_scores(psum, width, n_slc):
    ncmp = psum.shape[1]
    d = _iota2((ncmp, width), 0) - (SEL_BLOCK // CMP_STRIDE) * _iota2((ncmp, width), 1)
    wgt = jnp.where((d == -1) | (d == 3), 1.0, jnp.where((d >= 0) & (d <= 2), 2.0, 0.0))
    wgt = jnp.where(_iota2((ncmp, width), 1) < n_slc, wgt, 0.0).astype(BF16)
    p_hi = psum.astype(BF16)
    p_lo = (psum - p_hi.astype(F32)).astype(BF16)
    return _dot(p_hi, wgt) + _dot(p_lo, wgt)


def _top_blocks(slc, cur, n_pick):
    rows, width = slc.shape
    blk = _iota2((rows, width), 1)
    forced = (blk == 0) | (blk == cur) | (blk == cur - 1)
    score = jnp.where(forced, jnp.inf, slc)
    score = jnp.where(blk > cur, NEG_INF, score)
    blk_f = blk.astype(F32)
    lane = _iota2((rows, LANES), 1)
    sel = jnp.zeros((rows, width), F32)
    picks = jnp.zeros((rows, LANES), F32)
    for i in range(n_pick):
        mx = jnp.max(score, axis=-1, keepdims=True)
        first = jnp.min(jnp.where(score == mx, blk_f, float(width)), axis=-1, keepdims=True)
        pick = blk_f == first
        sel = jnp.where(pick, 1.0, sel)
        picks = jnp.where(lane == i, first, picks)
        score = jnp.where(pick, NEG_INF, score)
    return sel, picks


def _member_by_rank(psum, tpos_row, n_slc, n_pick):
    nq, ncmp = psum.shape
    nb = -(-n_slc // 8) * 8
    d = _iota2((nb, ncmp), 1) - (SEL_BLOCK // CMP_STRIDE) * _iota2((nb, ncmp), 0)
    wgt = jnp.where((d == -1) | (d == 3), 1.0, jnp.where((d >= 0) & (d <= 2), 2.0, 0.0))
    wgt = jnp.where(_iota2((nb, ncmp), 0) < n_slc, wgt, 0.0).astype(BF16)
    p_hi = psum.astype(BF16)
    p_lo = (psum - p_hi.astype(F32)).astype(BF16)
    slc = _dot_nt(wgt, p_hi) + _dot_nt(wgt, p_lo)
    blk = _iota2((nb, nq), 0)
    cur = jnp.right_shift(tpos_row, SEL_SHIFT)
    forced = (blk == 0) | (blk == cur) | (blk == cur - 1)
    score = jnp.where(forced, jnp.inf, slc)
    score = jnp.where(blk > cur, NEG_INF, score)
    ahead = jnp.zeros((nb, nq), F32)
    for i in range(n_slc):
        s_i = score[i:i + 1, :]
        ahead = ahead + jnp.where((s_i > score) | ((s_i == score) & (blk > i)), 1.0, 0.0)
    return jnp.where((ahead < n_pick) & (blk <= cur), 1.0, 0.0)


NEAR_COLS = Q_BLOCK + REL_MAX_DIST


def _add_per_head(s, mask):
    return (s.reshape(NSA_G, mask.shape[0], mask.shape[1]) + mask[None]).reshape(s.shape)


def _banded_attention(q, k_ref, v_ref, start, width, mask, near_bias):
    far = width - NEAR_COLS
    s_far = _add_per_head(_dot_nt(q, k_ref[pl.ds(start, far), :]), mask[:, :far])
    s_near = _add_per_head(_dot_nt(q, k_ref[pl.ds(start + far, NEAR_COLS), :]) + near_bias, mask[:, far:])
    m = jnp.maximum(jnp.max(s_far, axis=-1, keepdims=True), jnp.max(s_near, axis=-1, keepdims=True))
    m = jnp.where(m == NEG_INF, 0.0, m)
    p_far, p_near = jnp.exp(s_far - m), jnp.exp(s_near - m)
    l = jnp.sum(p_far, axis=-1, keepdims=True) + jnp.sum(p_near, axis=-1, keepdims=True)
    o = (_dot(p_far.astype(BF16), v_ref[pl.ds(start, far), :])
         + _dot(p_near.astype(BF16), v_ref[pl.ds(start + far, NEAR_COLS), :]))
    return o / jnp.maximum(l, TINY)


def _nsa_prompt_body(q_ref, zb_ref, gb_ref, bg_ref, ks_ref, vs_ref, kw_ref, vw_ref, kc_ref, vc_ref,
                     bc_ref, bn_ref, o_ref, ksp, vsp, kwp, vwp, osel, *, T):
    qi = pl.program_id(2)
    tq = Q_BLOCK
    front = T - tq
    wlen = WINDOW + tq
    n_slc = T // SEL_BLOCK

    @pl.when(qi == 0)
    def _():
        ksp[0:front, :] = jnp.zeros((front, NSA_HD), BF16)
        vsp[0:front, :] = jnp.zeros((front, NSA_HD), BF16)
        ksp[front:front + T, :] = ks_ref[...].astype(BF16)
        vsp[front:front + T, :] = vs_ref[...].astype(BF16)
        kwp[0:WINDOW, :] = jnp.zeros((WINDOW, NSA_HD), BF16)
        vwp[0:WINDOW, :] = jnp.zeros((WINDOW, NSA_HD), BF16)
        kwp[WINDOW:WINDOW + T, :] = kw_ref[...].astype(BF16)
        vwp[WINDOW:WINDOW + T, :] = vw_ref[...].astype(BF16)

    t0 = pl.multiple_of(qi * tq, tq)
    tpos = _iota2((tq, 1), 0) + t0
    q_all = q_ref[...] * (NSA_HD ** -0.5)
    q = jnp.concatenate([q_all[:, g * NSA_HD:(g + 1) * NSA_HD] for g in range(NSA_G)], axis=0).astype(BF16)
    bias_near = bn_ref[...].reshape(NSA_G * tq, NEAR_COLS)

    ncmp = T // CMP_STRIDE
    vis = tpos >= _iota2((1, ncmp), 1) * CMP_STRIDE + (CMP_BLOCK - 1)
    s = _dot_nt(q, kc_ref[...].astype(BF16)) + bc_ref[...].reshape(NSA_G * tq, ncmp)
    p, l = _softmax_rows(_add_per_head(s, jnp.where(vis, 0.0, NEG_INF)))
    p = p / jnp.maximum(l, TINY)
    o_cmp = _dot(p.astype(BF16), vc_ref[...].astype(BF16))
    psum = p[0:tq]
    for g in range(1, NSA_G):
        psum = psum + p[g * tq:(g + 1) * tq]

    member_t = _member_by_rank(psum, _iota2((1, tq), 1) + t0, n_slc, min(N_SEL, n_slc)).astype(BF16)

    nb = member_t.shape[0]
    n_win = SEL_WINDOWS if T % (SEL_WINDOWS * tq) == 0 else 1
    for i in range(n_win):
        w_prev, w = T * i // n_win, T * (i + 1) // n_win

        @pl.when((qi >= w_prev // tq) & (qi < w // tq))
        def _(w=w):
            off = T - w
            col_blk = (jnp.right_shift(_iota2((nb, w), 1) + off, SEL_SHIFT)
                       + (qi * (tq // SEL_BLOCK) + (tq - T) // SEL_BLOCK))
            expand = (col_blk == _iota2((nb, w), 0)).astype(BF16)
            kpos = _iota2((1, w), 1) + (t0 + tq - w)
            allowed = (_dot_tn(member_t, expand) > 0.5) & (kpos <= tpos)
            mask_s = jnp.where(allowed, 0.0, NEG_INF)
            osel[...] = _banded_attention(q, ksp, vsp, t0 + off, w, mask_s, bias_near)

    dist = WINDOW + _iota2((tq, wlen), 0) - _iota2((tq, wlen), 1)
    in_win = (dist >= 0) & (dist < WINDOW) & (_iota2((1, wlen), 1) + (t0 - WINDOW) >= 0)
    o_win = _banded_attention(q, kwp, vwp, t0, wlen, jnp.where(in_win, 0.0, NEG_INF), bias_near)
    gate = jax.nn.sigmoid(gb_ref[...] + bg_ref[...])
    zb = _silu(zb_ref[...])
    for g in range(NSA_G):
        head = pl.program_id(1) * NSA_G + g
        r = slice(g * tq, (g + 1) * tq)
        mix = (_lane_col(gate, head) * o_cmp[r] + _lane_col(gate, NSA_HEADS + head) * osel[r, :]
               + _lane_col(gate, 2 * NSA_HEADS + head) * o_win[r])
        sl = slice(g * NSA_HD, (g + 1) * NSA_HD)
        o_ref[:, sl] = (mix * zb[:, sl]).astype(o_ref.dtype)


def _nsa_prompt_call(ya, yb, kv16, kcmp, vcmp, bg_r, bias_c, bias_near, *, B, T):
    nq = T // Q_BLOCK
    gw = NSA_G * NSA_HD
    kv_spec = pl.BlockSpec((T, NSA_HD), lambda b, h, i: (b, h))
    cmp_spec = pl.BlockSpec((None, None, T // CMP_STRIDE, NSA_HD), lambda b, h, i: (b, h, 0, 0))
    return pl.pallas_call(
        functools.partial(_nsa_prompt_body, T=T),
        grid=(B, NSA_KVH, nq),
        in_specs=[pl.BlockSpec((Q_BLOCK, gw), lambda b, h, i: (b * nq + i, EVEN_A["qb"] // gw + h)),
                  pl.BlockSpec((Q_BLOCK, gw), lambda b, h, i: (b * nq + i, EVEN_B["zb"] // gw + h)),
                  pl.BlockSpec((Q_BLOCK, LANES), lambda b, h, i: (b * nq + i, EVEN_B["gb"] // LANES)),
                  pl.BlockSpec((1, LANES), lambda b, h, i: (0, 0)),
                  kv_spec, kv_spec, kv_spec, kv_spec, cmp_spec, cmp_spec,
                  pl.BlockSpec((None, NSA_G, Q_BLOCK, T // CMP_STRIDE), lambda b, h, i: (h, 0, i, 0)),
                  pl.BlockSpec((None, NSA_G, Q_BLOCK, NEAR_COLS), lambda b, h, i: (h, 0, 0, 0))],
        out_specs=pl.BlockSpec((Q_BLOCK, gw), lambda b, h, i: (b * nq + i, h)),
        out_shape=jax.ShapeDtypeStruct((B * T, NSA_W), BF16),
        scratch_shapes=[pltpu.VMEM((2 * T - Q_BLOCK, NSA_HD), BF16), pltpu.VMEM((2 * T - Q_BLOCK, NSA_HD), BF16),
                        pltpu.VMEM((WINDOW + T, NSA_HD), BF16), pltpu.VMEM((WINDOW + T, NSA_HD), BF16),
                        pltpu.VMEM((NSA_G * Q_BLOCK, NSA_HD), F32)],
        compiler_params=_params(("arbitrary", "arbitrary", "arbitrary")),
        name="nsa_prompt",
    )(ya, yb, yb, bg_r, *kv16, kcmp, vcmp, bias_c, bias_near)


CMP_PAGES = 32
CHUNKS_PER_PAGE = PAGE_SIZE // CMP_STRIDE
PAGE_ROWS = PAGE_SIZE * NSA_KVH


def _pool_rows(pool):
    return pool.reshape(pool.shape[0] * PAGE_ROWS, NSA_HD)


def _cmp_pages_body(pt_ref, *refs):
    del pt_ref
    pages = refs[:CMP_PAGES]
    w_ref, pe_ref, o_ref = refs[CMP_PAGES:]
    rows = CMP_PAGES * CHUNKS_PER_PAGE
    per_head = [jnp.concatenate(
        [jnp.concatenate([pg[pl.ds(NSA_KVH * s + h, CHUNKS_PER_PAGE, stride=CMP_STRIDE * NSA_KVH), :]
                          for s in range(CMP_STRIDE)], axis=1) for pg in pages], axis=0) for h in range(NSA_KVH)]
    w = w_ref[...]
    r = _dot(jnp.concatenate(per_head, axis=0).astype(BF16), w)
    pc = _dot(pe_ref[...], w)
    r = r + jnp.concatenate([pc[0:1, :NSA_HD], pc[1:2, NSA_HD:]], axis=1)
    for h in range(NSA_KVH):
        o_ref[h] = r[h * rows:(h + 1) * rows]


def _cmp_pages_call(pool, page_table, w1, pe, *, B):
    n_pages = page_table.shape[1]
    rows = CMP_PAGES * CHUNKS_PER_PAGE
    view = _pool_rows(pool)
    w = w1.reshape(2, CMP_STRIDE, NSA_HD, NSA_HD).transpose(1, 2, 0, 3).reshape(CMP_STRIDE * NSA_HD, 2 * NSA_HD)
    pe_rows = jnp.pad(pe.reshape(2, CMP_STRIDE * NSA_HD), ((0, 6), (0, 0))).astype(BF16)

    def page_spec(i):
        return pl.BlockSpec((PAGE_ROWS, NSA_HD), lambda b, s, pt: (pt[b * n_pages + s * CMP_PAGES + i], 0))

    grid_spec = pltpu.PrefetchScalarGridSpec(
        num_scalar_prefetch=1,
        grid=(B, n_pages // CMP_PAGES),
        in_specs=[page_spec(i) for i in range(CMP_PAGES)]
        + [pl.BlockSpec((CMP_STRIDE * NSA_HD, 2 * NSA_HD), lambda b, s, pt: (0, 0)),
           pl.BlockSpec((8, CMP_STRIDE * NSA_HD), lambda b, s, pt: (0, 0))],
        out_specs=pl.BlockSpec((None, NSA_KVH, rows, 2 * NSA_HD), lambda b, s, pt: (b, 0, s, 0)),
    )
    return pl.pallas_call(
        _cmp_pages_body,
        grid_spec=grid_spec,
        out_shape=jax.ShapeDtypeStruct((B, NSA_KVH, n_pages * CHUNKS_PER_PAGE, 2 * NSA_HD), F32),
        compiler_params=_params(("arbitrary", "arbitrary")),
        name="nsa_cmp_pages",
    )(page_table.reshape(-1), *([view] * CMP_PAGES), w.astype(BF16), pe_rows)


SEL_WINDOWS = 4
SLC_LANES = 384


def _sample_q_rows(q_ref):
    q = q_ref[...] * (NSA_HD ** -0.5)
    return jnp.concatenate([q[:, g * NSA_HD:(g + 1) * NSA_HD] for g in range(NSA_G)], axis=0).astype(BF16)


def _nsa_sample_main_body(abk_ref, abv_ref, b1_ref, w2_ref, q_ref, wk_ref, wv_ref, kn_ref, vn_ref, bc_ref, bw_ref,
                          ocmp_ref, owin_ref, idx_ref, *, T, n_slc):
    tp = SAMPLE_PAD_T
    rows = NSA_G * tp
    ncmp = abk_ref.shape[0]

    def compressed(ab_ref, t):
        ab = ab_ref[...]
        h = ab[:, :NSA_HD] + pltpu.roll(ab[:, NSA_HD:], ncmp - 1, 0) + b1_ref[t]
        return _dot(_gelu_tanh(h).astype(BF16), w2_ref[t]).astype(BF16)

    kc, vc = compressed(abk_ref, 0), compressed(abv_ref, 1)
    q = _sample_q_rows(q_ref)
    step = jnp.bitwise_and(_iota2((rows, 1), 0), tp - 1)
    tpos = PAST_LEN + step
    vis = tpos >= _iota2((1, ncmp), 1) * CMP_STRIDE + (CMP_BLOCK - 1)
    p, l = _softmax_rows(jnp.where(vis, _dot_nt(q, kc) + bc_ref[...], NEG_INF))
    p = p / jnp.maximum(l, TINY)
    ocmp_ref[...] = _dot(p.astype(BF16), vc)
    psum = p[0:tp]
    for g in range(1, NSA_G):
        psum = psum + p[g * tp:(g + 1) * tp]
    cur = jnp.right_shift(PAST_LEN + _iota2((tp, 1), 0), SEL_SHIFT)
    _, picks = _top_blocks(_slc_scores(psum, SLC_LANES, n_slc), cur, N_SEL)
    idx_ref[...] = picks.astype(jnp.int32)

    wb = wk_ref.shape[0] // NSA_KVH
    wlen = bw_ref.shape[1]
    fill = jnp.zeros((wlen - wb - tp, NSA_HD), BF16)
    head = pl.program_id(1)
    k_all = jnp.concatenate([wk_ref[pl.ds(head, wb, stride=NSA_KVH), :].astype(BF16), kn_ref[...], fill], axis=0)
    v_all = jnp.concatenate([wv_ref[pl.ds(head, wb, stride=NSA_KVH), :].astype(BF16), vn_ref[...], fill], axis=0)
    col = _iota2((1, wlen), 1)
    dist = tpos - (PAST_LEN - wb + col)
    in_win = (dist >= 0) & (dist < WINDOW) & (col < wb + T)
    pw, lw = _softmax_rows(jnp.where(in_win, _dot_nt(q, k_all) + bw_ref[...], NEG_INF))
    owin_ref[...] = _dot(pw.astype(BF16), v_all) / jnp.maximum(lw, TINY)


def _nsa_sample_main_call(ya, kw16, vw16, abk, abv, b1, w2, wk, wv, bias_c, bias_w, *, B, T):
    tp = SAMPLE_PAD_T
    rows = NSA_G * tp
    gw = NSA_G * NSA_HD
    ncmp = abk.shape[2]
    wb = wk.shape[1]
    wlen = bias_w.shape[-1]
    n_slc = -(-(PAST_LEN + T) // SEL_BLOCK)
    assert n_slc <= SLC_LANES and T <= tp
    ab_spec = pl.BlockSpec((None, None, ncmp, 2 * NSA_HD), lambda b, h: (b, h, 0, 0))
    win_spec = pl.BlockSpec((wb * NSA_KVH, NSA_HD), lambda b, h: (b, 0))
    o_spec = pl.BlockSpec((None, None, rows, NSA_HD), lambda b, h: (b, h, 0, 0))
    return pl.pallas_call(
        functools.partial(_nsa_sample_main_body, T=T, n_slc=n_slc),
        grid=(B, NSA_KVH),
        in_specs=[ab_spec, ab_spec,
                  pl.BlockSpec((2, 1, NSA_HD), lambda b, h: (0, 0, 0)),
                  pl.BlockSpec((2, NSA_HD, NSA_HD), lambda b, h: (0, 0, 0)),
                  pl.BlockSpec((tp, gw), lambda b, h: (b, EVEN_A["qb"] // gw + h)),
                  win_spec, win_spec,
                  pl.BlockSpec((tp, NSA_HD), lambda b, h: (b, h)),
                  pl.BlockSpec((tp, NSA_HD), lambda b, h: (b, h)),
                  pl.BlockSpec((None, rows, ncmp), lambda b, h: (h, 0, 0)),
                  pl.BlockSpec((None, rows, wlen), lambda b, h: (h, 0, 0))],
        out_specs=[o_spec, o_spec, pl.BlockSpec((None, None, tp, LANES), lambda b, h: (b, h, 0, 0))],
        out_shape=[jax.ShapeDtypeStruct((B, NSA_KVH, rows, NSA_HD), F32),
                   jax.ShapeDtypeStruct((B, NSA_KVH, rows, NSA_HD), F32),
                   jax.ShapeDtypeStruct((B, NSA_KVH, tp, LANES), jnp.int32)],
        compiler_params=_params(("parallel", "parallel")),
        name="nsa_sample_main",
    )(abk, abv, b1.reshape(2, 1, NSA_HD), w2.astype(BF16), ya,
      wk.reshape(B * wb * NSA_KVH, NSA_HD), wv.reshape(B * wb * NSA_KVH, NSA_HD), kw16, vw16, bias_c, bias_w)


NEAR_BLOCKS = 3


def _nsa_sample_sel_body(idx_ref, pt_ref, q_ref, kn_ref, vn_ref, tbl_ref, ocmp_ref, owin_ref, gb_ref, bg_ref, zb_ref,
                         *refs, T):
    del pt_ref
    k_blocks = refs[:N_SEL]
    v_blocks = refs[N_SEL:2 * N_SEL]
    o_ref, osel = refs[2 * N_SEL:]
    tp = SAMPLE_PAD_T
    rows = NSA_G * tp
    b, h, t = pl.program_id(0), pl.program_id(1), pl.program_id(2)
    base = ((b * NSA_KVH + h) * T + t) * N_SEL
    first_new = PAST_LEN // SEL_BLOCK
    cur = jnp.right_shift(PAST_LEN + t, SEL_SHIFT)
    q = _sample_q_rows(q_ref)
    pad = jnp.zeros((SEL_BLOCK - tp, NSA_HD), BF16)
    k_new = jnp.concatenate([kn_ref[...], pad], axis=0)
    v_new = jnp.concatenate([vn_ref[...], pad], axis=0)
    lane = _iota2((1, LANES), 1)
    low = lane < SEL_BLOCK
    within = jnp.bitwise_and(lane, SEL_BLOCK - 1)
    ks, vs, bias, kpos = [], [], [], []
    for i in range(0, N_SEL, 2):
        pair_bias, pair_pos = [], []
        for j in (i, i + 1):
            blk = idx_ref[base + j]
            is_new = blk >= first_new
            ks.append(jnp.where(is_new, k_new, k_blocks[j][pl.ds(h, SEL_BLOCK, stride=NSA_KVH), :].astype(BF16)))
            vs.append(jnp.where(is_new, v_new, v_blocks[j][pl.ds(h, SEL_BLOCK, stride=NSA_KVH), :].astype(BF16)))
            pair_bias.append(tbl_ref[jnp.clip(blk - (first_new - NEAR_BLOCKS), 0, NEAR_BLOCKS)])
            pair_pos.append(jnp.where(blk <= cur, blk * SEL_BLOCK, PAST_LEN + SEL_BLOCK * LANES) + within)
        bias.append(jnp.where(low, pair_bias[0], pair_bias[1]))
        kpos.append(jnp.where(low, pair_pos[0], pair_pos[1]))
    k_all = jnp.concatenate(ks, axis=0)
    v_all = jnp.concatenate(vs, axis=0)
    step = jnp.bitwise_and(_iota2((rows, 1), 0), tp - 1)
    ok = jnp.concatenate(kpos, axis=1) <= PAST_LEN + step
    p, l = _softmax_rows(jnp.where(ok, _dot_nt(q, k_all) + jnp.concatenate(bias, axis=1), NEG_INF))
    o = _dot(p.astype(BF16), v_all) / jnp.maximum(l, TINY)

    @pl.when(t == 0)
    def _():
        osel[...] = jnp.zeros_like(osel)

    osel[...] = jnp.where(step == t, o, osel[...])

    @pl.when(t == T - 1)
    def _():
        gate = jax.nn.sigmoid(gb_ref[...] + bg_ref[...])
        zb = _silu(zb_ref[...])
        for g in range(NSA_G):
            r = slice(g * tp, (g + 1) * tp)
            head = h * NSA_G + g
            mix = (_lane_col(gate, head) * ocmp_ref[r, :] + _lane_col(gate, NSA_HEADS + head) * osel[r, :]
                   + _lane_col(gate, 2 * NSA_HEADS + head) * owin_ref[r, :])
            sl = slice(g * NSA_HD, (g + 1) * NSA_HD)
            o_ref[:, sl] = (mix * zb[:, sl]).astype(o_ref.dtype)


def _nsa_sample_sel_call(ya, yb, ks16, vs16, idx, page_table, pool_k, pool_v, tbl, o_cmp, o_win, bg_r, *, B, T):
    tp = SAMPLE_PAD_T
    rows = NSA_G * tp
    gw = NSA_G * NSA_HD
    n_pages = page_table.shape[1]
    halves = PAGE_SIZE // SEL_BLOCK
    idx_flat = idx[:, :, :T, :N_SEL].reshape(-1)
    view_k, view_v = _pool_rows(pool_k), _pool_rows(pool_v)

    def blk_spec(j):
        def index(b, h, t, idx_s, pt_s):
            blk = idx_s[((b * NSA_KVH + h) * T + t) * N_SEL + j]
            page = pt_s[b * n_pages + jnp.minimum(blk // halves, n_pages - 1)]
            return (page * halves + blk % halves, 0)
        return pl.BlockSpec((SEL_BLOCK * NSA_KVH, NSA_HD), index)

    o_spec = pl.BlockSpec((None, None, rows, NSA_HD), lambda b, h, t, *_: (b, h, 0, 0))
    grid_spec = pltpu.PrefetchScalarGridSpec(
        num_scalar_prefetch=2,
        grid=(B, NSA_KVH, T),
        in_specs=[pl.BlockSpec((tp, gw), lambda b, h, t, *_: (b, EVEN_A["qb"] // gw + h)),
                  pl.BlockSpec((tp, NSA_HD), lambda b, h, t, *_: (b, h)),
                  pl.BlockSpec((tp, NSA_HD), lambda b, h, t, *_: (b, h)),
                  pl.BlockSpec((None, NEAR_BLOCKS + 1, rows, LANES), lambda b, h, t, *_: (h, 0, 0, 0)),
                  o_spec, o_spec,
                  pl.BlockSpec((tp, LANES), lambda b, h, t, *_: (b, EVEN_B["gb"] // LANES)),
                  pl.BlockSpec((1, LANES), lambda b, h, t, *_: (0, 0)),
                  pl.BlockSpec((tp, gw), lambda b, h, t, *_: (b, EVEN_B["zb"] // gw + h))]
        + [blk_spec(j) for j in range(N_SEL)] * 2,
        out_specs=pl.BlockSpec((tp, gw), lambda b, h, t, *_: (b, h)),
        scratch_shapes=[pltpu.VMEM((rows, NSA_HD), F32)],
    )
    return pl.pallas_call(
        functools.partial(_nsa_sample_sel_body, T=T),
        grid_spec=grid_spec,
        out_shape=jax.ShapeDtypeStruct((B * tp, NSA_W), BF16),
        compiler_params=_params(("arbitrary", "arbitrary", "arbitrary")),
        name="nsa_sample_sel",
    )(idx_flat, page_table.reshape(-1), ya, ks16, vs16, tbl, o_cmp, o_win, yb, bg_r, yb,
      *([view_k] * N_SEL), *([view_v] * N_SEL))


def _sample_bias_tables(rel_bias, T, wb):
    tp = SAMPLE_PAD_T
    ncmp = PAST_LEN // CMP_STRIDE
    wlen = -(-(wb + tp) // LANES) * LANES
    first = PAST_LEN // SEL_BLOCK - NEAR_BLOCKS
    assert PAST_LEN - ((first + 1) * SEL_BLOCK - 1) >= REL_MAX_DIST
    lo, hi = -wlen, PAST_LEN + tp
    rev = _bias_line(rel_bias, lo, hi, descending=True)

    def rows(tbl):
        return tbl.reshape(NSA_KVH, NSA_G * tp, tbl.shape[-1])

    t_c = _toeplitz(rev, hi - 1 - (PAST_LEN - (CMP_BLOCK - 1)), tp, CMP_STRIDE * ncmp)[:, :, ::CMP_STRIDE]
    t_w = _toeplitz(rev, hi - 1 - wb, tp, wlen)
    far = jnp.broadcast_to(rev[:, hi - 1 - REL_MAX_DIST][:, None, None], (NSA_HEADS, tp, LANES))
    near = []
    for k in range(1, NEAR_BLOCKS + 1):
        half = _toeplitz(rev, hi - 1 - (PAST_LEN - (first + k) * SEL_BLOCK), tp, SEL_BLOCK)
        near.append(jnp.concatenate([half, half], axis=-1))
    t_s = jnp.stack([far] + near, axis=1).reshape(NSA_KVH, NSA_G, NEAR_BLOCKS + 1, tp, LANES)
    t_s = t_s.transpose(0, 2, 1, 3, 4).reshape(NSA_KVH, NEAR_BLOCKS + 1, NSA_G * tp, LANES)
    return rows(t_c), rows(t_w), t_s


def _tail_even(w):
    return _tail_relayout(w, EVEN_KV_OFF + 6 * NSA_KV_W, 3 * NSA_HEADS, NSA_W + MEM_W, EVEN_B_N)


def _tail_odd(w):
    return _tail_relayout(w, ODD_A_N, 2 * ML_HEADS, ML_V_W + MEM_W, ODD_B_N)


def _gate_bias_even(b_gate):
    return jnp.pad(b_gate, (0, LANES - 3 * NSA_HEADS)).reshape(1, LANES)


def _gate_bias_odd(b_if):
    return jnp.pad(b_if.reshape(2 * ML_HEADS), (0, LANES - 2 * ML_HEADS)).reshape(1, LANES)


def _rel_bucket(dist):
    n = np.maximum(dist, 0)
    exact = REL_BUCKETS // 2
    nf = np.maximum(n, 1).astype(np.float32)
    large = exact + (np.log(nf / exact) / math.log(REL_MAX_DIST / exact) * (REL_BUCKETS - exact)).astype(np.int32)
    return np.where(n < exact, n, np.minimum(large, REL_BUCKETS - 1))


def _bias_line(rel_bias, lo, hi, descending=False):
    dist = np.arange(hi - 1, lo - 1, -1) if descending else np.arange(lo, hi)
    buckets = _rel_bucket(dist)
    edges = np.flatnonzero(np.diff(buckets)) + 1
    starts = np.concatenate([[0], edges])
    ends = np.concatenate([edges, [hi - lo]])
    bias_t = rel_bias.T.astype(F32)
    runs = [jnp.broadcast_to(bias_t[:, int(buckets[s])][:, None], (NSA_HEADS, int(e - s))) for s, e in zip(starts, ends)]
    return jnp.concatenate(runs, axis=1)


def _skew_rows(v, rows, step, cols):
    n = v.shape[1]
    reps = -(-rows * (n + step) // n)
    return jnp.tile(v, (1, reps))[:, :rows * (n + step)].reshape(v.shape[0], rows, n + step)[:, :, :cols]


def _toeplitz(rev, start, rows, cols):
    seg = rev[:, start - (rows - 1):start + cols]
    return _skew_rows(jnp.roll(seg, -(rows - 1), axis=1), rows, -1, cols)


def _prompt_bias_tables(rel_bias, T):
    ncmp = T // CMP_STRIDE
    assert Q_BLOCK + 1 >= REL_MAX_DIST
    lo, hi = -(CMP_STRIDE * ncmp + CMP_BLOCK), T
    line = _bias_line(rel_bias, lo, hi)
    rev = _bias_line(rel_bias, lo, hi, descending=True)

    def split(tbl):
        return tbl.reshape((NSA_KVH, NSA_G) + tbl.shape[1:])

    back = CMP_STRIDE * (ncmp - 1)
    first = -(back + CMP_BLOCK - 1) - lo
    seg = line[:, first:first + T + back]
    t_c = _skew_rows(jnp.roll(seg, -back, axis=1), ncmp, -CMP_STRIDE, T).swapaxes(1, 2)
    far = rev[:, hi - 1 - REL_MAX_DIST]
    t_near = _toeplitz(rev, hi - 1 - REL_MAX_DIST, Q_BLOCK, NEAR_COLS) - far[:, None, None]
    return split(t_c), split(t_near)


def _nsa_sample(ya, yb, kv16, page_table, pk_cmp, pv_cmp, pk_sel, pv_sel, wk, wv, bg_r, w1, b1, w2, pe, rel_bias,
                *, B, T):
    assert (PAST_LEN + T) // CMP_STRIDE == PAST_LEN // CMP_STRIDE
    abk = _cmp_pages_call(pk_cmp, page_table, w1[0], pe[0], B=B)
    abv = _cmp_pages_call(pv_cmp, page_table, w1[1], pe[1], B=B)
    bias_c, bias_w, tbl = _sample_bias_tables(rel_bias, T, wk.shape[1])
    o_cmp, o_win, idx = _nsa_sample_main_call(ya, kv16[4], kv16[5], abk, abv, b1, w2, wk, wv, bias_c, bias_w, B=B, T=T)
    return _nsa_sample_sel_call(ya, yb, kv16[2], kv16[3], idx, page_table, pk_sel, pv_sel, tbl, o_cmp, o_win, bg_r,
                                B=B, T=T)


def _kv_project(x, wt):
    outs = [_matmul_heads(x, wt, first=EVEN_KV_OFF + j * NSA_KV_W, transposed=True) for j in range(6)]
    return [o[0] for o in outs], [o[1] for o in outs]


def _even_prompt(hp2d, npre, mk16, mv16, wt, wt_b, bg_r, w1, b1, w2, pe, lb, g_norm, w_out, rel_bias, *, B, T):
    ya, yb = _matmul_nt(npre, wt, tm=W_TILE_M, tn=W_TILE_N, rows=(0, EVEN_A_N)), _matmul_nt(npre, wt_b)
    kv32, kv16 = _kv_project(npre, wt)
    oa, s_new = _hgrn_call(ya, jnp.zeros((B, HG_HEADS, HG_DK, HG_DV), F32), lb, g_norm, B=B, T=T, L=CHUNK, valid=CHUNK)
    kcmp = _compress_call(kv16[0], w1[0], b1[0], w2[0], pe[0], B=B, T=T)
    vcmp = _compress_call(kv16[1], w1[1], b1[1], w2[1], pe[1], B=B, T=T)
    ob = _nsa_prompt_call(ya, yb, kv16[2:], kcmp, vcmp, bg_r, *_prompt_bias_tables(rel_bias, T), B=B, T=T)
    om = _mem_call(yb, EVEN_B["qm"], mk16, mv16, B=B, T=T)
    h_new = _outproj([oa, ob, om], w_out, hp2d)
    wb = min(WINDOW, T)
    rows = [r.reshape(B, T, NSA_KVH, NSA_HD) for r in kv32]
    return h_new, (rows[0], rows[1], rows[2], rows[3], rows[4][:, -wb:], rows[5][:, -wb:], s_new)


def _even_sample(hs2d, nsam, mk_s, mv_s, page_table, pk_cmp, pv_cmp, pk_sel, pv_sel, wk, wv, s0,
                 wt, wt_b, bg_r, w1, b1, w2, pe, lb, g_norm, w_out, rel_bias, *, B, T):
    tp = SAMPLE_PAD_T
    ya, yb = _matmul_nt(nsam, wt, tm=W_TILE_M, tn=W_TILE_N, rows=(0, EVEN_A_N)), _matmul_nt(nsam, wt_b)
    kv32, kv16 = _kv_project(nsam, wt)
    oa, s_new = _hgrn_call(ya, s0, lb, g_norm, B=B, T=tp, L=tp, valid=T)
    ob = _nsa_sample(ya, yb, kv16, page_table, pk_cmp, pv_cmp, pk_sel, pv_sel, wk, wv, bg_r, w1, b1, w2, pe, rel_bias,
                     B=B, T=T)
    om = _mem_call(yb, EVEN_B["qm"], mk_s.reshape(B * N_MEM, MEM_W), mv_s.reshape(B * N_MEM, MEM_W), B=B, T=tp)
    rows = [r.reshape(B, tp, NSA_KVH, NSA_HD)[:, :T] for r in kv32]
    wb = wk.shape[1]
    win_k = jnp.concatenate([wk, rows[4]], axis=1)[:, -wb:]
    win_v = jnp.concatenate([wv, rows[5]], axis=1)[:, -wb:]
    return _outproj([oa, ob, om], w_out, hs2d), (rows[0], rows[1], rows[2], rows[3], win_k, win_v, s_new)


def _odd_mix(h2d, hn, k2d, v2d, c0, n0, m0, wt, wt_b, bif_r, g_norm, w_out, *, B, T, L, valid):
    ya, yb = _matmul_nt(hn, wt, tm=W_TILE_M, tn=W_TILE_N, rows=(0, ODD_A_N)), _matmul_nt(hn, wt_b)
    h, c_new, n_new, m_new = _mlstm_call(ya, yb, c0, n0, m0, bif_r, g_norm, B=B, T=T, L=L, valid=valid)
    om = _mem_call(yb, ODD_B["qm"], k2d, v2d, B=B, T=T)
    return _outproj([h, om], w_out, h2d), (c_new, n_new, m_new)


def _stack(lst, i):
    return jnp.stack([t[i] for t in lst])


def kernel(x_prompt, x_sample, cache_mem_k, cache_mem_v, cache_cmp_k, cache_cmp_v, cache_sel_k, cache_sel_v,
           cache_win_k, cache_win_v, state_hgrn, state_mlstm_c, state_mlstm_n, state_mlstm_m, page_table,
           mem_prompt, norm_w, mem_norm_w, final_norm_w, rel_bias, w_mem_kv, w_in_even, b_nsa_gate,
           w_cmp1, b_cmp1, w_cmp2, pe_cmp, hgrn_lb_logits, hgrn_norm_w, w_out_even, w_in_odd, b_mlstm_if,
           mlstm_norm_w, w_out_odd):
    bp, tp = x_prompt.shape[:2]
    bs, ts = x_sample.shape[:2]
    tsp = SAMPLE_PAD_T
    lbs = jnp.cumsum(jax.nn.softmax(hgrn_lb_logits.astype(F32), axis=0), axis=0)
    hp = x_prompt.reshape(bp * tp, D_MODEL)
    hs = jnp.pad(x_sample, ((0, 0), (0, tsp - ts), (0, 0))).reshape(bs * tsp, D_MODEL)
    mem2d = mem_prompt.reshape(bp * N_MEM, D_MODEL)
    mem_new, even_p, even_s, odd_p, odd_s = [], [], [], [], []
    for l in range(DEPTH):
        npre = _rmsnorm_rows(hp, norm_w[l], BF16)
        nsam = _rmsnorm_rows(hs, norm_w[l], BF16)
        nmem = _rmsnorm_rows(mem2d, mem_norm_w[l], BF16)
        mk32, mk16 = _matmul_heads(nmem, w_mem_kv[l], first=0)
        mv32, mv16 = _matmul_heads(nmem, w_mem_kv[l], first=MEM_W)
        mem_new.append((mk32.reshape(bp, N_MEM, MEM_HEADS, MEM_HD), mv32.reshape(bp, N_MEM, MEM_HEADS, MEM_HD)))
        mk_s, mv_s = cache_mem_k[l], cache_mem_v[l]
        if l % 2 == 0:
            e = l // 2
            w_in = w_in_even[e].T
            w_b = _tail_even(w_in)
            w_out = w_out_even[e].astype(BF16)
            bg_r = _gate_bias_even(b_nsa_gate[e])
            cmpw = (w_cmp1[e].reshape(2, CMP_BLOCK, NSA_HD, NSA_HD), b_cmp1[e], w_cmp2[e], pe_cmp[e])
            hp, st_p = _even_prompt(hp, npre, mk16, mv16, w_in, w_b, bg_r, *cmpw, lbs[l], hgrn_norm_w[e], w_out,
                                    rel_bias, B=bp, T=tp)
            hs, st_s = _even_sample(hs, nsam, mk_s, mv_s, page_table, cache_cmp_k[e], cache_cmp_v[e], cache_sel_k[e],
                                    cache_sel_v[e], cache_win_k[e], cache_win_v[e], state_hgrn[e], w_in, w_b, bg_r,
                                    *cmpw, lbs[l], hgrn_norm_w[e], w_out, rel_bias, B=bs, T=ts)
            even_p.append(st_p)
            even_s.append(st_s)
        else:
            o = l // 2
            w_in = w_in_odd[o].T
            w_b = _tail_odd(w_in)
            w_out = w_out_odd[o].astype(BF16)
            bif_r = _gate_bias_odd(b_mlstm_if[o])
            hp, st_p = _odd_mix(hp, npre, mk16, mv16, jnp.zeros((bp, ML_HEADS, ML_DV, ML_DK), F32),
                                jnp.zeros((bp, ML_HEADS, ML_DK), F32), jnp.zeros((bp, ML_HEADS), F32),
                                w_in, w_b, bif_r, mlstm_norm_w[o], w_out, B=bp, T=tp, L=ML_CHUNK, valid=ML_CHUNK)
            hs, st_s = _odd_mix(hs, nsam, mk_s.reshape(bs * N_MEM, MEM_W), mv_s.reshape(bs * N_MEM, MEM_W),
                                state_mlstm_c[o], state_mlstm_n[o], state_mlstm_m[o],
                                w_in, w_b, bif_r, mlstm_norm_w[o], w_out, B=bs, T=tsp, L=tsp, valid=ts)
            odd_p.append(st_p)
            odd_s.append(st_s)
    y_prompt = _rmsnorm_rows(hp, final_norm_w, F32).reshape(bp, tp, D_MODEL)
    y_sample = _rmsnorm_rows(hs, final_norm_w, F32).reshape(bs, tsp, D_MODEL)[:, :ts]
    return (y_prompt, y_sample,
            _stack(mem_new, 0), _stack(mem_new, 1),
            _stack(even_p, 0), _stack(even_p, 1), _stack(even_p, 2), _stack(even_p, 3),
            _stack(even_p, 4), _stack(even_p, 5), _stack(even_p, 6),
            _stack(odd_p, 0), _stack(odd_p, 1), _stack(odd_p, 2),
            _stack(even_s, 0), _stack(even_s, 1), _stack(even_s, 2), _stack(even_s, 3),
            _stack(even_s, 4), _stack(even_s, 5), _stack(even_s, 6),
            _stack(odd_s, 0), _stack(odd_s, 1), _stack(odd_s, 2))
```

```python
import functools
import math

import jax
import jax.numpy as jnp
import numpy as np
from jax import lax
from jax.experimental import pallas as pl
from jax.experimental.pallas import tpu as pltpu

D_MODEL = 4096
DEPTH = 2
PAST_LEN = 16384
PAGE_SIZE = 128
N_MEM = 256
EPS = 1e-6
CHUNK = 64

HG_DK = 128
HG_DV = 128
HG_HEADS = D_MODEL // 2 // HG_DV
HG_W = HG_HEADS * HG_DV

NSA_HD = 128
NSA_HEADS = D_MODEL // 2 // NSA_HD
NSA_KVH = 4
NSA_G = NSA_HEADS // NSA_KVH
NSA_W = NSA_HEADS * NSA_HD
NSA_KV_W = NSA_KVH * NSA_HD
CMP_BLOCK = 32
CMP_STRIDE = 16
SEL_BLOCK = 64
SEL_SHIFT = SEL_BLOCK.bit_length() - 1
N_SEL = 16
WINDOW = 512
Q_BLOCK = 256

ML_HEADS = D_MODEL // 512
ML_DK = D_MODEL // 2 // ML_HEADS
ML_DV = D_MODEL // ML_HEADS
ML_QK_W = ML_HEADS * ML_DK
ML_V_W = ML_HEADS * ML_DV

MEM_HEADS = 4
MEM_HD = 128
MEM_W = MEM_HEADS * MEM_HD

REL_BUCKETS = 32
REL_MAX_DIST = 128

F32 = jnp.float32
BF16 = jnp.bfloat16
LANES = 128
NEG_INF = float("-inf")
TINY = float(np.finfo(np.float32).tiny)
EXP_CLAMP = 80.0
VMEM_LIMIT = 56 * 1024 * 1024

HG_HB = 16
ML_HB = 4
ML_CHUNK = 256
W_TILE_M, W_TILE_N = 1024, 512
HG_SUB = 16
SAMPLE_PAD_T = 16

MM_TILE_N = 1024
EVEN_A = {"qa": 0, "fa": HG_W, "ia": 2 * HG_W, "za": 3 * HG_W, "qb": 4 * HG_W}
EVEN_A_N = 4 * HG_W + NSA_W
EVEN_B = {"zb": 0, "qm": NSA_W, "gb": NSA_W + MEM_W}
EVEN_B_N = -(-(NSA_W + MEM_W + LANES) // MM_TILE_N) * MM_TILE_N
EVEN_KV_OFF = EVEN_A_N
ODD_A = {"q": 0, "k": ML_QK_W, "v": 2 * ML_QK_W, "og": 2 * ML_QK_W + ML_V_W}
ODD_A_N = 2 * ML_QK_W + 2 * ML_V_W
ODD_B = {"z": 0, "qm": ML_V_W, "gates": ML_V_W + MEM_W}
ODD_B_N = -(-(ML_V_W + MEM_W + LANES) // MM_TILE_N) * MM_TILE_N


def _dot(a, b):
    return jnp.dot(a, b, preferred_element_type=F32)


def _dot_nt(a, b):
    return lax.dot_general(a, b, (((1,), (1,)), ((), ())), preferred_element_type=F32)


def _dot_tn(a, b):
    return lax.dot_general(a, b, (((0,), (0,)), ((), ())), preferred_element_type=F32)


def _iota2(shape, dim):
    return lax.broadcasted_iota(jnp.int32, shape, dim)


def _cumsum_rows(x, tri_b):
    hi = x.astype(BF16)
    r1 = x - hi.astype(F32)
    mid = r1.astype(BF16)
    lo = (r1 - mid.astype(F32)).astype(BF16)
    return _dot(tri_b, hi) + _dot(tri_b, mid) + _dot(tri_b, lo)


def _row_to_col(row, n):
    eye = _iota2((n, n), 0) == _iota2((n, n), 1)
    return jnp.sum(jnp.where(eye, row, 0.0), axis=1, keepdims=True)


def _col_to_row(col, n):
    eye = _iota2((n, n), 0) == _iota2((n, n), 1)
    return jnp.sum(jnp.where(eye, col, 0.0), axis=0, keepdims=True)


def _lane_col(x, idx):
    return jnp.sum(jnp.where(_iota2(x.shape, 1) == idx, x, 0.0), axis=1, keepdims=True)


def _silu(x):
    return x * jax.nn.sigmoid(x)


def _params(sem):
    return pltpu.CompilerParams(dimension_semantics=sem, vmem_limit_bytes=VMEM_LIMIT)


def _rmsnorm_body(x_ref, w_ref, o_ref):
    x = x_ref[...].astype(F32)
    y = x * lax.rsqrt(jnp.mean(x * x, axis=-1, keepdims=True) + EPS)
    o_ref[...] = (y * w_ref[...].astype(F32)).astype(o_ref.dtype)


def _rmsnorm_rows(x2d, w, out_dtype, tm=256):
    m, d = x2d.shape
    tm = min(tm, m)
    return pl.pallas_call(
        _rmsnorm_body,
        grid=(m // tm,),
        in_specs=[pl.BlockSpec((tm, d), lambda i: (i, 0)), pl.BlockSpec((1, d), lambda i: (0, 0))],
        out_specs=pl.BlockSpec((tm, d), lambda i: (i, 0)),
        out_shape=jax.ShapeDtypeStruct((m, d), out_dtype),
        compiler_params=_params(("parallel",)),
        name="rmsnorm",
    )(x2d, w.reshape(1, d))


def _matmul_nt_body(a_ref, bt_ref, o_ref):
    o_ref[...] = _dot_nt(a_ref[...], bt_ref[...].astype(BF16))


def _matmul_nt(a, bt, tm=1024, tn=MM_TILE_N, rows=None):
    m, k = a.shape
    first, n = rows or (0, bt.shape[0])
    tm, tn = min(tm, m), min(tn, n)
    assert m % tm == 0 and n % tn == 0 and first % tn == 0, (a.shape, bt.shape, rows)
    j0 = first // tn
    return pl.pallas_call(
        _matmul_nt_body,
        grid=(m // tm, n // tn),
        in_specs=[pl.BlockSpec((tm, k), lambda i, j: (i, 0)), pl.BlockSpec((tn, k), lambda i, j: (j0 + j, 0))],
        out_specs=pl.BlockSpec((tm, tn), lambda i, j: (i, j)),
        out_shape=jax.ShapeDtypeStruct((m, n), F32),
        compiler_params=_params(("parallel", "parallel")),
        name="matmul",
    )(a, bt)


def _tail_body(lo_ref, hi_ref, gate_ref, o_ref, *, shift, n_main):
    i = pl.program_id(0)
    main = jnp.concatenate([lo_ref[shift:, :], hi_ref[:shift, :]], axis=0)
    gates = jnp.where(_iota2((LANES, 1), 0) < shift, gate_ref[...], 0.0)
    o_ref[...] = jnp.where(i < n_main, main, jnp.where(i == n_main, gates, 0.0)).astype(o_ref.dtype)


def _tail_relayout(wt, first, shift, main, out_rows):
    n, k = wt.shape
    assert first % LANES == 0 and main % LANES == 0 and out_rows % LANES == 0 and shift % 8 == 0 and shift < LANES
    assert first + shift + main == n
    c0, n_main = first // LANES, main // LANES
    return pl.pallas_call(
        functools.partial(_tail_body, shift=shift, n_main=n_main),
        grid=(out_rows // LANES,),
        in_specs=[pl.BlockSpec((LANES, k), lambda i: (c0 + jnp.minimum(i, n_main - 1), 0)),
                  pl.BlockSpec((LANES, k), lambda i: (c0 + jnp.minimum(i, n_main - 1) + 1, 0)),
                  pl.BlockSpec((LANES, k), lambda i: (c0, 0))],
        out_specs=pl.BlockSpec((LANES, k), lambda i: (i, 0)),
        out_shape=jax.ShapeDtypeStruct((out_rows, k), BF16),
        compiler_params=_params(("parallel",)),
        name="tail_relayout",
    )(wt, wt, wt)


def _matmul_heads_body(a_ref, b_ref, o32_ref, o16_ref, *, transposed):
    b = b_ref[...].astype(BF16)
    acc = _dot_nt(a_ref[...], b) if transposed else _dot(a_ref[...], b)
    for h in range(MEM_HEADS):
        o32_ref[:, h, :] = acc[:, h * LANES:(h + 1) * LANES]
    o16_ref[...] = acc.astype(BF16)


def _matmul_heads(a, b, first=0, transposed=False, tm=1024):
    m, k = a.shape
    n = MEM_HEADS * LANES
    tm = min(tm, m)
    assert m % tm == 0 and first % n == 0, (a.shape, b.shape, first)
    j0 = first // n
    b_spec = pl.BlockSpec((n, k), lambda i: (j0, 0)) if transposed else pl.BlockSpec((k, n), lambda i: (0, j0))
    return pl.pallas_call(
        functools.partial(_matmul_heads_body, transposed=transposed),
        grid=(m // tm,),
        in_specs=[pl.BlockSpec((tm, k), lambda i: (i, 0)), b_spec],
        out_specs=[pl.BlockSpec((tm, MEM_HEADS, LANES), lambda i: (i, 0, 0)), pl.BlockSpec((tm, n), lambda i: (i, 0))],
        out_shape=[jax.ShapeDtypeStruct((m, MEM_HEADS, LANES), F32), jax.ShapeDtypeStruct((m, n), BF16)],
        compiler_params=_params(("parallel",)),
        name="matmul_heads",
    )(a, b)


def _outproj_body(*refs, widths):
    xs = refs[:len(widths)]
    w_ref, r_ref, o_ref = refs[len(widths):]
    acc = r_ref[...]
    off = 0
    for x_ref, w in zip(xs, widths):
        acc = acc + _dot(x_ref[...], w_ref[off:off + w, :])
        off += w
    o_ref[...] = acc


def _outproj(xs, w_bf16, resid, tm=1024, tn=512):
    m = resid.shape[0]
    n = w_bf16.shape[1]
    widths = tuple(x.shape[1] for x in xs)
    assert sum(widths) == w_bf16.shape[0]
    tm = min(tm, m)
    in_specs = [pl.BlockSpec((tm, w), lambda i, j: (i, 0)) for w in widths]
    in_specs += [pl.BlockSpec((w_bf16.shape[0], tn), lambda i, j: (0, j)), pl.BlockSpec((tm, tn), lambda i, j: (i, j))]
    return pl.pallas_call(
        functools.partial(_outproj_body, widths=widths),
        grid=(m // tm, n // tn),
        in_specs=in_specs,
        out_specs=pl.BlockSpec((tm, tn), lambda i, j: (i, j)),
        out_shape=jax.ShapeDtypeStruct((m, n), F32),
        compiler_params=_params(("parallel", "parallel")),
        name="outproj",
    )(*xs, w_bf16, resid)


def _hgrn_body(qa_ref, fa_ref, ia_ref, za_ref, lb_ref, gn_ref, s0_ref, o_ref, s_out, s_scr, *, L, valid):
    c = pl.program_id(2)

    @pl.when(c == 0)
    def _():
        s_scr[...] = s0_ref[...]

    lb = lb_ref[...]
    sig = jax.nn.sigmoid(fa_ref[...])
    logf = jnp.log(lb + (1.0 - lb) * sig)
    kk = (1.0 - lb) * (1.0 - sig)
    if valid < L:
        live = _iota2((L, 1), 0) < valid
        logf = jnp.where(live, logf, 0.0)
        kk = jnp.where(live, kk, 0.0)
    tri_b = (_iota2((L, L), 0) >= _iota2((L, L), 1)).astype(BF16)
    bc = _cumsum_rows(logf, tri_b)
    q = _silu(qa_ref[...])
    gate = _silu(za_ref[...])
    v = ia_ref[...]
    gn = gn_ref[...]
    nsub = L // HG_SUB
    rr = _iota2((L, nsub * L), 0)
    cc = _iota2((L, nsub * L), 1)
    keep = ((jnp.right_shift(cc, L.bit_length() - 1) == jnp.right_shift(rr, HG_SUB.bit_length() - 1))
            & (jnp.bitwise_and(cc, L - 1) <= rr))
    for j in range(HG_HB):
        sl = slice(j * HG_DK, (j + 1) * HG_DK)
        bj, qj, kj = bc[:, sl], q[:, sl], kk[:, sl]
        vb = v[:, sl].astype(BF16)
        s_prev = s_scr[j]
        mids = [bj[i * HG_SUB + HG_SUB // 2:i * HG_SUB + HG_SUB // 2 + 1, :] for i in range(nsub)]
        mid_rows = jnp.concatenate([jnp.broadcast_to(m, (HG_SUB, HG_DK)) for m in mids], axis=0)
        q_dec = qj * jnp.exp(jnp.minimum(bj - mid_rows, EXP_CLAMP))
        k_dec = jnp.concatenate([kj * jnp.exp(jnp.minimum(m - bj, EXP_CLAMP)) for m in mids], axis=0)
        att = jnp.where(keep, _dot_nt(q_dec.astype(BF16), k_dec.astype(BF16)), 0.0)
        q_state = (qj * jnp.exp(bj)).astype(BF16)
        v_rep = jnp.concatenate([vb] * nsub, axis=0)
        if (nsub * L) % LANES == 0:
            o = _dot(jnp.concatenate([q_state, att.astype(BF16)], axis=1),
                     jnp.concatenate([s_prev.astype(BF16), v_rep], axis=0))
        else:
            o = _dot(q_state, s_prev.astype(BF16)) + _dot(att.astype(BF16), v_rep)
        o_n = o * lax.rsqrt(jnp.mean(o * o, axis=-1, keepdims=True) + EPS) * gn
        o_ref[:, sl] = (o_n * gate[:, sl]).astype(o_ref.dtype)
        bl = bj[L - 1:L, :]
        kd = kj * jnp.exp(bl - bj)
        s_scr[j] = _row_to_col(jnp.exp(bl), HG_DK) * s_prev + _dot_tn(kd.astype(BF16), vb)

    @pl.when(c == pl.num_programs(2) - 1)
    def _():
        s_out[...] = s_scr[...]


def _hgrn_call(y, s0, lb, gn, *, B, T, L, valid):
    nc = T // L
    w = HG_HB * HG_DK

    def col(name):
        blk = EVEN_A[name] // w
        return pl.BlockSpec((L, w), lambda b, hg, c: (b * nc + c, blk + hg))

    state_spec = pl.BlockSpec((None, HG_HB, HG_DK, HG_DV), lambda b, hg, c: (b, hg, 0, 0))
    return pl.pallas_call(
        functools.partial(_hgrn_body, L=L, valid=valid),
        grid=(B, HG_HEADS // HG_HB, nc),
        in_specs=[col("qa"), col("fa"), col("ia"), col("za"),
                  pl.BlockSpec((1, w), lambda b, hg, c: (0, hg)),
                  pl.BlockSpec((1, HG_DV), lambda b, hg, c: (0, 0)),
                  state_spec],
        out_specs=[pl.BlockSpec((L, w), lambda b, hg, c: (b * nc + c, hg)), state_spec],
        out_shape=[jax.ShapeDtypeStruct((B * T, HG_W), BF16),
                   jax.ShapeDtypeStruct((B, HG_HEADS, HG_DK, HG_DV), F32)],
        scratch_shapes=[pltpu.VMEM((HG_HB, HG_DK, HG_DV), F32)],
        compiler_params=_params(("arbitrary", "arbitrary", "arbitrary")),
        name="hgrn2",
    )(y, y, y, y, lb.reshape(1, HG_W), gn.reshape(1, HG_DV), s0)


def _mlstm_body(q_ref, k_ref, v_ref, og_ref, z_ref, g_ref, bif_ref, gn_ref, c0_ref, n0_ref, m0_ref,
                h_ref, c_out, n_out, m_out, c_scr, n_scr, m_scr, *, L, valid):
    c = pl.program_id(2)

    @pl.when(c == 0)
    def _():
        c_scr[...] = c0_ref[...]
        n_scr[...] = n0_ref[...]
        m_scr[...] = m0_ref[...]

    gates = g_ref[...] + bif_ref[...]
    log_i = gates
    log_f = jnp.minimum(gates, 0.0) - jnp.log(1.0 + jnp.exp(-jnp.abs(gates)))
    if valid < L:
        live = _iota2((L, 1), 0) < valid
        log_i = jnp.where(live, log_i, -1e30)
        log_f = jnp.where(live, log_f, 0.0)
    tri = _iota2((L, L), 0) >= _iota2((L, L), 1)
    bcs = _cumsum_rows(log_f, tri.astype(BF16))
    for j in range(ML_HB):
        head = pl.program_id(1) * ML_HB + j
        b_col = _lane_col(bcs, ML_HEADS + head)
        i_col = _lane_col(log_i, head)
        b_row = _col_to_row(b_col, L)
        i_row = _col_to_row(i_col, L)
        m_prev = m_scr[:, j:j + 1]
        dmat = jnp.where(tri, b_col - b_row + i_row, NEG_INF)
        inter = b_col + m_prev
        mt = jnp.maximum(inter, jnp.max(dmat, axis=1, keepdims=True))
        w_in = jnp.exp(dmat - mt)
        w_x = jnp.exp(inter - mt)
        qj = q_ref[:, j * ML_DK:(j + 1) * ML_DK]
        kj = k_ref[:, j * ML_DK:(j + 1) * ML_DK] * (ML_DK ** -0.5)
        vj = v_ref[:, j * ML_DV:(j + 1) * ML_DV]
        qb, kb = qj.astype(BF16), kj.astype(BF16)
        sw = _dot_nt(qb, kb) * w_in
        c_prev = c_scr[j]
        n_prev = n_scr[:, j * ML_DK:(j + 1) * ML_DK]
        num = w_x * _dot_nt(qb, c_prev.astype(BF16)) + _dot(sw.astype(BF16), vj.astype(BF16))
        den = w_x * jnp.sum(qj * n_prev, axis=1, keepdims=True) + jnp.sum(sw, axis=1, keepdims=True)
        h = num / jnp.maximum(jnp.abs(den), jnp.exp(-mt))
        m_last = mt[L - 1:L, :]
        b_last = b_col[L - 1:L, :]
        w_end = jnp.exp(b_last - b_col + i_col - m_last)
        d_c = jnp.exp(b_last + m_prev - m_last)
        c_scr[j] = d_c * c_prev + _dot_tn((w_end * vj).astype(BF16), kb)
        n_scr[:, j * ML_DK:(j + 1) * ML_DK] = d_c * n_prev + jnp.sum(w_end * kj, axis=0, keepdims=True)
        m_scr[:, j:j + 1] = m_last
        sv = slice(j * ML_DV, (j + 1) * ML_DV)
        h_n = h * lax.rsqrt(jnp.mean(h * h, axis=-1, keepdims=True) + EPS) * gn_ref[:, sv]
        h_ref[:, sv] = (h_n * jax.nn.sigmoid(og_ref[:, sv]) * _silu(z_ref[:, sv])).astype(h_ref.dtype)

    @pl.when(c == pl.num_programs(2) - 1)
    def _():
        c_out[...] = c_scr[...]
        n_out[...] = n_scr[...]
        m_out[...] = m_scr[...]


def _mlstm_call(ya, yb, c0, n0, m0, bif_r, gn, *, B, T, L, valid):
    nc = T // L
    ng = ML_HEADS // ML_HB
    wk, wv = ML_HB * ML_DK, ML_HB * ML_DV

    def col(name, w):
        blk = (ODD_A[name] if name in ODD_A else ODD_B[name]) // w
        return pl.BlockSpec((L, w), lambda b, hg, c: (b * nc + c, blk + hg))

    c_spec = pl.BlockSpec((None, ML_HB, ML_DV, ML_DK), lambda b, hg, c: (b, hg, 0, 0))
    n_spec = pl.BlockSpec((None, 1, wk), lambda b, hg, c: (b, 0, hg))
    m_spec = pl.BlockSpec((None, None, 1, LANES), lambda b, hg, c: (b, hg, 0, 0))
    m0_r = jnp.pad(m0.reshape(B, ng, 1, ML_HB), ((0, 0), (0, 0), (0, 0), (0, LANES - ML_HB)))
    h, c_new, n_new, m_new = pl.pallas_call(
        functools.partial(_mlstm_body, L=L, valid=valid),
        grid=(B, ng, nc),
        in_specs=[col("q", wk), col("k", wk), col("v", wv), col("og", wv), col("z", wv),
                  pl.BlockSpec((L, LANES), lambda b, hg, c: (b * nc + c, ODD_B["gates"] // LANES)),
                  pl.BlockSpec((1, LANES), lambda b, hg, c: (0, 0)),
                  pl.BlockSpec((1, wv), lambda b, hg, c: (0, hg)),
                  c_spec, n_spec, m_spec],
        out_specs=[pl.BlockSpec((L, wv), lambda b, hg, c: (b * nc + c, hg)), c_spec, n_spec, m_spec],
        out_shape=[jax.ShapeDtypeStruct((B * T, ML_V_W), BF16),
                   jax.ShapeDtypeStruct((B, ML_HEADS, ML_DV, ML_DK), F32),
                   jax.ShapeDtypeStruct((B, 1, ML_QK_W), F32),
                   jax.ShapeDtypeStruct((B, ng, 1, LANES), F32)],
        scratch_shapes=[pltpu.VMEM((ML_HB, ML_DV, ML_DK), F32), pltpu.VMEM((1, wk), F32), pltpu.VMEM((1, LANES), F32)],
        compiler_params=_params(("arbitrary", "arbitrary", "arbitrary")),
        name="mlstm",
    )(ya, ya, ya, ya, yb, yb, bif_r, gn.reshape(1, ML_V_W), c0, n0.reshape(B, 1, ML_QK_W), m0_r)
    return h, c_new, n_new.reshape(B, ML_HEADS, ML_DK), m_new[:, :, 0, :ML_HB].reshape(B, ML_HEADS)


def _mem_body(q_ref, k_ref, v_ref, o_ref):
    q = q_ref[...] * (MEM_HD ** -0.5)
    for h in range(MEM_HEADS):
        sl = slice(h * MEM_HD, (h + 1) * MEM_HD)
        s = _dot_nt(q[:, sl].astype(BF16), k_ref[:, sl].astype(BF16))
        p = jnp.exp(s - jnp.max(s, axis=-1, keepdims=True))
        o = _dot(p.astype(BF16), v_ref[:, sl].astype(BF16)) / jnp.sum(p, axis=-1, keepdims=True)
        o_ref[:, sl] = o.astype(o_ref.dtype)


def _mem_call(y, q_off, k2d, v2d, *, B, T, tq=256):
    tq = min(tq, T)
    nq = T // tq
    qb = q_off // MEM_W
    return pl.pallas_call(
        _mem_body,
        grid=(B, nq),
        in_specs=[pl.BlockSpec((tq, MEM_W), lambda b, i: (b * nq + i, qb)),
                  pl.BlockSpec((N_MEM, MEM_W), lambda b, i: (b, 0)),
                  pl.BlockSpec((N_MEM, MEM_W), lambda b, i: (b, 0))],
        out_specs=pl.BlockSpec((tq, MEM_W), lambda b, i: (b * nq + i, 0)),
        out_shape=jax.ShapeDtypeStruct((B * T, MEM_W), BF16),
        compiler_params=_params(("parallel", "parallel")),
        name="mem_attn",
    )(y, k2d, v2d)


def _gelu_tanh(x):
    return 0.5 * x * (1.0 + jnp.tanh(math.sqrt(2.0 / math.pi) * (x + 0.044715 * (x * x * x))))


def _compress_body(x_ref, w1_ref, b1_ref, w2_ref, pe_ref, o_ref, x32, *, nch):
    x32[...] = x_ref[...].astype(F32)
    a = jnp.zeros((nch, NSA_HD), F32)
    b = jnp.zeros((nch, NSA_HD), F32)
    for s in range(CMP_STRIDE):
        r = x32[pl.ds(s, nch, stride=CMP_STRIDE), :]
        a = a + _dot((r + pe_ref[s:s + 1, :]).astype(BF16), w1_ref[s])
        b = b + _dot((r + pe_ref[CMP_STRIDE + s:CMP_STRIDE + s + 1, :]).astype(BF16), w1_ref[CMP_STRIDE + s])
    h = a + pltpu.roll(b, nch - 1, 0) + b1_ref[...]
    o_ref[...] = _dot(_gelu_tanh(h).astype(BF16), w2_ref[...])


def _compress_call(x16, w1, b1, w2, pe, *, B, T):
    nch = T // CMP_STRIDE
    return pl.pallas_call(
        functools.partial(_compress_body, nch=nch),
        grid=(B, NSA_KVH),
        in_specs=[pl.BlockSpec((T, NSA_HD), lambda b, h: (b, h)),
                  pl.BlockSpec((CMP_BLOCK, NSA_HD, NSA_HD), lambda b, h: (0, 0, 0)),
                  pl.BlockSpec((1, NSA_HD), lambda b, h: (0, 0)),
                  pl.BlockSpec((NSA_HD, NSA_HD), lambda b, h: (0, 0)),
                  pl.BlockSpec((CMP_BLOCK, NSA_HD), lambda b, h: (0, 0))],
        out_specs=pl.BlockSpec((None, None, nch, NSA_HD), lambda b, h: (b, h, 0, 0)),
        out_shape=jax.ShapeDtypeStruct((B, NSA_KVH, nch, NSA_HD), F32),
        scratch_shapes=[pltpu.VMEM((T, NSA_HD), F32)],
        compiler_params=_params(("parallel", "parallel")),
        name="nsa_compress",
    )(x16, w1.astype(BF16), b1.reshape(1, NSA_HD), w2.astype(BF16), pe)


def _softmax_rows(s):
    m = jnp.max(s, axis=-1, keepdims=True)
    m = jnp.where(m == NEG_INF, 0.0, m)
    p = jnp.exp(s - m)
    return p, jnp.sum(p, axis=-1, keepdims=True)


def _slc_scores(psum, width, n_slc):
    ncmp = psum.shape[1]
    d = _iota2((ncmp, width), 0) - (SEL_BLOCK // CMP_STRIDE) * _iota2((ncmp, width), 1)
    wgt = jnp.where((d == -1) | (d == 3), 1.0, jnp.where((d >= 0) & (d <= 2), 2.0, 0.0))
    wgt = jnp.where(_iota2((ncmp, width), 1) < n_slc, wgt, 0.0).astype(BF16)
    p_hi = psum.astype(BF16)
    p_lo = (psum - p_hi.astype(F32)).astype(BF16)
    return _dot(p_hi, wgt) + _dot(p_lo, wgt)


def _top_blocks(slc, cur, n_pick):
    rows, width = slc.shape
    blk = _iota2((rows, width), 1)
    forced = (blk == 0) | (blk == cur) | (blk == cur - 1)
    score = jnp.where(forced, jnp.inf, slc)
    score = jnp.where(blk > cur, NEG_INF, score)
    blk_f = blk.astype(F32)
    lane = _iota2((rows, LANES), 1)
    sel = jnp.zeros((rows, width), F32)
    picks = jnp.zeros((rows, LANES), F32)
    for i in range(n_pick):
        mx = jnp.max(score, axis=-1, keepdims=True)
        first = jnp.min(jnp.where(score == mx, blk_f, float(width)), axis=-1, keepdims=True)
        pick = blk_f == first
        sel = jnp.where(pick, 1.0, sel)
        picks = jnp.where(lane == i, first, picks)
        score = jnp.where(pick, NEG_INF, score)
    return sel, picks


def _member_by_rank(psum, tpos_row, n_slc, n_pick):
    nq, ncmp = psum.shape
    nb = -(-n_slc // 8) * 8
    d = _iota2((nb, ncmp), 1) - (SEL_BLOCK // CMP_STRIDE) * _iota2((nb, ncmp), 0)
    wgt = jnp.where((d == -1) | (d == 3), 1.0, jnp.where((d >= 0) & (d <= 2), 2.0, 0.0))
    wgt = jnp.where(_iota2((nb, ncmp), 0) < n_slc, wgt, 0.0).astype(BF16)
    p_hi = psum.astype(BF16)
    p_lo = (psum - p_hi.astype(F32)).astype(BF16)
    slc = _dot_nt(wgt, p_hi) + _dot_nt(wgt, p_lo)
    blk = _iota2((nb, nq), 0)
    cur = jnp.right_shift(tpos_row, SEL_SHIFT)
    forced = (blk == 0) | (blk == cur) | (blk == cur - 1)
    score = jnp.where(forced, jnp.inf, slc)
    score = jnp.where(blk > cur, NEG_INF, score)
    ahead = jnp.zeros((nb, nq), F32)
    for i in range(n_slc):
        s_i = score[i:i + 1, :]
        ahead = ahead + jnp.where((s_i > score) | ((s_i == score) & (blk > i)), 1.0, 0.0)
    return jnp.where((ahead < n_pick) & (blk <= cur), 1.0, 0.0)


NEAR_COLS = Q_BLOCK + REL_MAX_DIST


def _add_per_head(s, mask):
    return (s.reshape(NSA_G, mask.shape[0], mask.shape[1]) + mask[None]).reshape(s.shape)


def _banded_attention(q, k_ref, v_ref, start, width, mask, near_bias):
    far = width - NEAR_COLS
    s_far = _add_per_head(_dot_nt(q, k_ref[pl.ds(start, far), :]), mask[:, :far])
    s_near = _add_per_head(_dot_nt(q, k_ref[pl.ds(start + far, NEAR_COLS), :]) + near_bias, mask[:, far:])
    m = jnp.maximum(jnp.max(s_far, axis=-1, keepdims=True), jnp.max(s_near, axis=-1, keepdims=True))
    m = jnp.where(m == NEG_INF, 0.0, m)
    p_far, p_near = jnp.exp(s_far - m), jnp.exp(s_near - m)
    l = jnp.sum(p_far, axis=-1, keepdims=True) + jnp.sum(p_near, axis=-1, keepdims=True)
    o = (_dot(p_far.astype(BF16), v_ref[pl.ds(start, far), :])
         + _dot(p_near.astype(BF16), v_ref[pl.ds(start + far, NEAR_COLS), :]))
    return o / jnp.maximum(l, TINY)


def _nsa_prompt_body(q_ref, zb_ref, gb_ref, bg_ref, ks_ref, vs_ref, kw_ref, vw_ref, kc_ref, vc_ref,
                     bc_ref, bn_ref, o_ref, ksp, vsp, kwp, vwp, osel, *, T):
    qi = pl.program_id(2)
    tq = Q_BLOCK
    front = T - tq
    wlen = WINDOW + tq
    n_slc = T // SEL_BLOCK

    @pl.when(qi == 0)
    def _():
        ksp[0:front, :] = jnp.zeros((front, NSA_HD), BF16)
        vsp[0:front, :] = jnp.zeros((front, NSA_HD), BF16)
        ksp[front:front + T, :] = ks_ref[...].astype(BF16)
        vsp[front:front + T, :] = vs_ref[...].astype(BF16)
        kwp[0:WINDOW, :] = jnp.zeros((WINDOW, NSA_HD), BF16)
        vwp[0:WINDOW, :] = jnp.zeros((WINDOW, NSA_HD), BF16)
        kwp[WINDOW:WINDOW + T, :] = kw_ref[...].astype(BF16)
        vwp[WINDOW:WINDOW + T, :] = vw_ref[...].astype(BF16)

    t0 = pl.multiple_of(qi * tq, tq)
    tpos = _iota2((tq, 1), 0) + t0
    q_all = q_ref[...] * (NSA_HD ** -0.5)
    q = jnp.concatenate([q_all[:, g * NSA_HD:(g + 1) * NSA_HD] for g in range(NSA_G)], axis=0).astype(BF16)
    bias_near = bn_ref[...].reshape(NSA_G * tq, NEAR_COLS)

    ncmp = T // CMP_STRIDE
    vis = tpos >= _iota2((1, ncmp), 1) * CMP_STRIDE + (CMP_BLOCK - 1)
    s = _dot_nt(q, kc_ref[...].astype(BF16)) + bc_ref[...].reshape(NSA_G * tq, ncmp)
    p, l = _softmax_rows(_add_per_head(s, jnp.where(vis, 0.0, NEG_INF)))
    p = p / jnp.maximum(l, TINY)
    o_cmp = _dot(p.astype(BF16), vc_ref[...].astype(BF16))
    psum = p[0:tq]
    for g in range(1, NSA_G):
        psum = psum + p[g * tq:(g + 1) * tq]

    member_t = _member_by_rank(psum, _iota2((1, tq), 1) + t0, n_slc, min(N_SEL, n_slc)).astype(BF16)

    nb = member_t.shape[0]
    n_win = SEL_WINDOWS if T % (SEL_WINDOWS * tq) == 0 else 1
    for i in range(n_win):
        w_prev, w = T * i // n_win, T * (i + 1) // n_win

        @pl.when((qi >= w_prev // tq) & (qi < w // tq))
        def _(w=w):
            off = T - w
            col_blk = (jnp.right_shift(_iota2((nb, w), 1) + off, SEL_SHIFT)
                       + (qi * (tq // SEL_BLOCK) + (tq - T) // SEL_BLOCK))
            expand = (col_blk == _iota2((nb, w), 0)).astype(BF16)
            kpos = _iota2((1, w), 1) + (t0 + tq - w)
            allowed = (_dot_tn(member_t, expand) > 0.5) & (kpos <= tpos)
            mask_s = jnp.where(allowed, 0.0, NEG_INF)
            osel[...] = _banded_attention(q, ksp, vsp, t0 + off, w, mask_s, bias_near)

    dist = WINDOW + _iota2((tq, wlen), 0) - _iota2((tq, wlen), 1)
    in_win = (dist >= 0) & (dist < WINDOW) & (_iota2((1, wlen), 1) + (t0 - WINDOW) >= 0)
    o_win = _banded_attention(q, kwp, vwp, t0, wlen, jnp.where(in_win, 0.0, NEG_INF), bias_near)
    gate = jax.nn.sigmoid(gb_ref[...] + bg_ref[...])
    zb = _silu(zb_ref[...])
    for g in range(NSA_G):
        head = pl.program_id(1) * NSA_G + g
        r = slice(g * tq, (g + 1) * tq)
        mix = (_lane_col(gate, head) * o_cmp[r] + _lane_col(gate, NSA_HEADS + head) * osel[r, :]
               + _lane_col(gate, 2 * NSA_HEADS + head) * o_win[r])
        sl = slice(g * NSA_HD, (g + 1) * NSA_HD)
        o_ref[:, sl] = (mix * zb[:, sl]).astype(o_ref.dtype)


def _nsa_prompt_call(ya, yb, kv16, kcmp, vcmp, bg_r, bias_c, bias_near, *, B, T):
    nq = T // Q_BLOCK
    gw = NSA_G * NSA_HD
    kv_spec = pl.BlockSpec((T, NSA_HD), lambda b, h, i: (b, h))
    cmp_spec = pl.BlockSpec((None, None, T // CMP_STRIDE, NSA_HD), lambda b, h, i: (b, h, 0, 0))
    return pl.pallas_call(
        functools.partial(_nsa_prompt_body, T=T),
        grid=(B, NSA_KVH, nq),
        in_specs=[pl.BlockSpec((Q_BLOCK, gw), lambda b, h, i: (b * nq + i, EVEN_A["qb"] // gw + h)),
                  pl.BlockSpec((Q_BLOCK, gw), lambda b, h, i: (b * nq + i, EVEN_B["zb"] // gw + h)),
                  pl.BlockSpec((Q_BLOCK, LANES), lambda b, h, i: (b * nq + i, EVEN_B["gb"] // LANES)),
                  pl.BlockSpec((1, LANES), lambda b, h, i: (0, 0)),
                  kv_spec, kv_spec, kv_spec, kv_spec, cmp_spec, cmp_spec,
                  pl.BlockSpec((None, NSA_G, Q_BLOCK, T // CMP_STRIDE), lambda b, h, i: (h, 0, i, 0)),
                  pl.BlockSpec((None, NSA_G, Q_BLOCK, NEAR_COLS), lambda b, h, i: (h, 0, 0, 0))],
        out_specs=pl.BlockSpec((Q_BLOCK, gw), lambda b, h, i: (b * nq + i, h)),
        out_shape=jax.ShapeDtypeStruct((B * T, NSA_W), BF16),
        scratch_shapes=[pltpu.VMEM((2 * T - Q_BLOCK, NSA_HD), BF16), pltpu.VMEM((2 * T - Q_BLOCK, NSA_HD), BF16),
                        pltpu.VMEM((WINDOW + T, NSA_HD), BF16), pltpu.VMEM((WINDOW + T, NSA_HD), BF16),
                        pltpu.VMEM((NSA_G * Q_BLOCK, NSA_HD), F32)],
        compiler_params=_params(("arbitrary", "arbitrary", "arbitrary")),
        name="nsa_prompt",
    )(ya, yb, yb, bg_r, *kv16, kcmp, vcmp, bias_c, bias_near)


CMP_PAGES = 32
CHUNKS_PER_PAGE = PAGE_SIZE // CMP_STRIDE
PAGE_ROWS = PAGE_SIZE * NSA_KVH


def _pool_rows(pool):
    return pool.reshape(pool.shape[0] * PAGE_ROWS, NSA_HD)


def _cmp_pages_body(pt_ref, *refs):
    del pt_ref
    pages = refs[:CMP_PAGES]
    w_ref, pe_ref, o_ref = refs[CMP_PAGES:]
    rows = CMP_PAGES * CHUNKS_PER_PAGE
    per_head = [jnp.concatenate(
        [jnp.concatenate([pg[pl.ds(NSA_KVH * s + h, CHUNKS_PER_PAGE, stride=CMP_STRIDE * NSA_KVH), :]
                          for s in range(CMP_STRIDE)], axis=1) for pg in pages], axis=0) for h in range(NSA_KVH)]
    w = w_ref[...]
    r = _dot(jnp.concatenate(per_head, axis=0).astype(BF16), w)
    pc = _dot(pe_ref[...], w)
    r = r + jnp.concatenate([pc[0:1, :NSA_HD], pc[1:2, NSA_HD:]], axis=1)
    for h in range(NSA_KVH):
        o_ref[h] = r[h * rows:(h + 1) * rows]


def _cmp_pages_call(pool, page_table, w1, pe, *, B):
    n_pages = page_table.shape[1]
    rows = CMP_PAGES * CHUNKS_PER_PAGE
    view = _pool_rows(pool)
    w = w1.reshape(2, CMP_STRIDE, NSA_HD, NSA_HD).transpose(1, 2, 0, 3).reshape(CMP_STRIDE * NSA_HD, 2 * NSA_HD)
    pe_rows = jnp.pad(pe.reshape(2, CMP_STRIDE * NSA_HD), ((0, 6), (0, 0))).astype(BF16)

    def page_spec(i):
        return pl.BlockSpec((PAGE_ROWS, NSA_HD), lambda b, s, pt: (pt[b * n_pages + s * CMP_PAGES + i], 0))

    grid_spec = pltpu.PrefetchScalarGridSpec(
        num_scalar_prefetch=1,
        grid=(B, n_pages // CMP_PAGES),
        in_specs=[page_spec(i) for i in range(CMP_PAGES)]
        + [pl.BlockSpec((CMP_STRIDE * NSA_HD, 2 * NSA_HD), lambda b, s, pt: (0, 0)),
           pl.BlockSpec((8, CMP_STRIDE * NSA_HD), lambda b, s, pt: (0, 0))],
        out_specs=pl.BlockSpec((None, NSA_KVH, rows, 2 * NSA_HD), lambda b, s, pt: (b, 0, s, 0)),
    )
    return pl.pallas_call(
        _cmp_pages_body,
        grid_spec=grid_spec,
        out_shape=jax.ShapeDtypeStruct((B, NSA_KVH, n_pages * CHUNKS_PER_PAGE, 2 * NSA_HD), F32),
        compiler_params=_params(("arbitrary", "arbitrary")),
        name="nsa_cmp_pages",
    )(page_table.reshape(-1), *([view] * CMP_PAGES), w.astype(BF16), pe_rows)


SEL_WINDOWS = 4
SLC_LANES = 384


def _sample_q_rows(q_ref):
    q = q_ref[...] * (NSA_HD ** -0.5)
    return jnp.concatenate([q[:, g * NSA_HD:(g + 1) * NSA_HD] for g in range(NSA_G)], axis=0).astype(BF16)


def _nsa_sample_main_body(abk_ref, abv_ref, b1_ref, w2_ref, q_ref, wk_ref, wv_ref, kn_ref, vn_ref, bc_ref, bw_ref,
                          ocmp_ref, owin_ref, idx_ref, *, T, n_slc):
    tp = SAMPLE_PAD_T
    rows = NSA_G * tp
    ncmp = abk_ref.shape[0]

    def compressed(ab_ref, t):
        ab = ab_ref[...]
        h = ab[:, :NSA_HD] + pltpu.roll(ab[:, NSA_HD:], ncmp - 1, 0) + b1_ref[t]
        return _dot(_gelu_tanh(h).astype(BF16), w2_ref[t]).astype(BF16)

    kc, vc = compressed(abk_ref, 0), compressed(abv_ref, 1)
    q = _sample_q_rows(q_ref)
    step = jnp.bitwise_and(_iota2((rows, 1), 0), tp - 1)
    tpos = PAST_LEN + step
    vis = tpos >= _iota2((1, ncmp), 1) * CMP_STRIDE + (CMP_BLOCK - 1)
    p, l = _softmax_rows(jnp.where(vis, _dot_nt(q, kc) + bc_ref[...], NEG_INF))
    p = p / jnp.maximum(l, TINY)
    ocmp_ref[...] = _dot(p.astype(BF16), vc)
    psum = p[0:tp]
    for g in range(1, NSA_G):
        psum = psum + p[g * tp:(g + 1) * tp]
    cur = jnp.right_shift(PAST_LEN + _iota2((tp, 1), 0), SEL_SHIFT)
    _, picks = _top_blocks(_slc_scores(psum, SLC_LANES, n_slc), cur, N_SEL)
    idx_ref[...] = picks.astype(jnp.int32)

    wb = wk_ref.shape[0] // NSA_KVH
    wlen = bw_ref.shape[1]
    fill = jnp.zeros((wlen - wb - tp, NSA_HD), BF16)
    head = pl.program_id(1)
    k_all = jnp.concatenate([wk_ref[pl.ds(head, wb, stride=NSA_KVH), :].astype(BF16), kn_ref[...], fill], axis=0)
    v_all = jnp.concatenate([wv_ref[pl.ds(head, wb, stride=NSA_KVH), :].astype(BF16), vn_ref[...], fill], axis=0)
    col = _iota2((1, wlen), 1)
    dist = tpos - (PAST_LEN - wb + col)
    in_win = (dist >= 0) & (dist < WINDOW) & (col < wb + T)
    pw, lw = _softmax_rows(jnp.where(in_win, _dot_nt(q, k_all) + bw_ref[...], NEG_INF))
    owin_ref[...] = _dot(pw.astype(BF16), v_all) / jnp.maximum(lw, TINY)


def _nsa_sample_main_call(ya, kw16, vw16, abk, abv, b1, w2, wk, wv, bias_c, bias_w, *, B, T):
    tp = SAMPLE_PAD_T
    rows = NSA_G * tp
    gw = NSA_G * NSA_HD
    ncmp = abk.shape[2]
    wb = wk.shape[1]
    wlen = bias_w.shape[-1]
    n_slc = -(-(PAST_LEN + T) // SEL_BLOCK)
    assert n_slc <= SLC_LANES and T <= tp
    ab_spec = pl.BlockSpec((None, None, ncmp, 2 * NSA_HD), lambda b, h: (b, h, 0, 0))
    win_spec = pl.BlockSpec((wb * NSA_KVH, NSA_HD), lambda b, h: (b, 0))
    o_spec = pl.BlockSpec((None, None, rows, NSA_HD), lambda b, h: (b, h, 0, 0))
    return pl.pallas_call(
        functools.partial(_nsa_sample_main_body, T=T, n_slc=n_slc),
        grid=(B, NSA_KVH),
        in_specs=[ab_spec, ab_spec,
                  pl.BlockSpec((2, 1, NSA_HD), lambda b, h: (0, 0, 0)),
                  pl.BlockSpec((2, NSA_HD, NSA_HD), lambda b, h: (0, 0, 0)),
                  pl.BlockSpec((tp, gw), lambda b, h: (b, EVEN_A["qb"] // gw + h)),
                  win_spec, win_spec,
                  pl.BlockSpec((tp, NSA_HD), lambda b, h: (b, h)),
                  pl.BlockSpec((tp, NSA_HD), lambda b, h: (b, h)),
                  pl.BlockSpec((None, rows, ncmp), lambda b, h: (h, 0, 0)),
                  pl.BlockSpec((None, rows, wlen), lambda b, h: (h, 0, 0))],
        out_specs=[o_spec, o_spec, pl.BlockSpec((None, None, tp, LANES), lambda b, h: (b, h, 0, 0))],
        out_shape=[jax.ShapeDtypeStruct((B, NSA_KVH, rows, NSA_HD), F32),
                   jax.ShapeDtypeStruct((B, NSA_KVH, rows, NSA_HD), F32),
                   jax.ShapeDtypeStruct((B, NSA_KVH, tp, LANES), jnp.int32)],
        compiler_params=_params(("parallel", "parallel")),
        name="nsa_sample_main",
    )(abk, abv, b1.reshape(2, 1, NSA_HD), w2.astype(BF16), ya,
      wk.reshape(B * wb * NSA_KVH, NSA_HD), wv.reshape(B * wb * NSA_KVH, NSA_HD), kw16, vw16, bias_c, bias_w)


NEAR_BLOCKS = 3


def _nsa_sample_sel_body(idx_ref, pt_ref, q_ref, kn_ref, vn_ref, tbl_ref, ocmp_ref, owin_ref, gb_ref, bg_ref, zb_ref,
                         *refs, T):
    del pt_ref
    k_blocks = refs[:N_SEL]
    v_blocks = refs[N_SEL:2 * N_SEL]
    o_ref, osel = refs[2 * N_SEL:]
    tp = SAMPLE_PAD_T
    rows = NSA_G * tp
    b, h, t = pl.program_id(0), pl.program_id(1), pl.program_id(2)
    base = ((b * NSA_KVH + h) * T + t) * N_SEL
    first_new = PAST_LEN // SEL_BLOCK
    cur = jnp.right_shift(PAST_LEN + t, SEL_SHIFT)
    q = _sample_q_rows(q_ref)
    pad = jnp.zeros((SEL_BLOCK - tp, NSA_HD), BF16)
    k_new = jnp.concatenate([kn_ref[...], pad], axis=0)
    v_new = jnp.concatenate([vn_ref[...], pad], axis=0)
    lane = _iota2((1, LANES), 1)
    low = lane < SEL_BLOCK
    within = jnp.bitwise_and(lane, SEL_BLOCK - 1)
    ks, vs, bias, kpos = [], [], [], []
    for i in range(0, N_SEL, 2):
        pair_bias, pair_pos = [], []
        for j in (i, i + 1):
            blk = idx_ref[base + j]
            is_new = blk >= first_new
            ks.append(jnp.where(is_new, k_new, k_blocks[j][pl.ds(h, SEL_BLOCK, stride=NSA_KVH), :].astype(BF16)))
            vs.append(jnp.where(is_new, v_new, v_blocks[j][pl.ds(h, SEL_BLOCK, stride=NSA_KVH), :].astype(BF16)))
            pair_bias.append(tbl_ref[jnp.clip(blk - (first_new - NEAR_BLOCKS), 0, NEAR_BLOCKS)])
            pair_pos.append(jnp.where(blk <= cur, blk * SEL_BLOCK, PAST_LEN + SEL_BLOCK * LANES) + within)
        bias.append(jnp.where(low, pair_bias[0], pair_bias[1]))
        kpos.append(jnp.where(low, pair_pos[0], pair_pos[1]))
    k_all = jnp.concatenate(ks, axis=0)
    v_all = jnp.concatenate(vs, axis=0)
    step = jnp.bitwise_and(_iota2((rows, 1), 0), tp - 1)
    ok = jnp.concatenate(kpos, axis=1) <= PAST_LEN + step
    p, l = _softmax_rows(jnp.where(ok, _dot_nt(q, k_all) + jnp.concatenate(bias, axis=1), NEG_INF))
    o = _dot(p.astype(BF16), v_all) / jnp.maximum(l, TINY)

    @pl.when(t == 0)
    def _():
        osel[...] = jnp.zeros_like(osel)

    osel[...] = jnp.where(step == t, o, osel[...])

    @pl.when(t == T - 1)
    def _():
        gate = jax.nn.sigmoid(gb_ref[...] + bg_ref[...])
        zb = _silu(zb_ref[...])
        for g in range(NSA_G):
            r = slice(g * tp, (g + 1) * tp)
            head = h * NSA_G + g
            mix = (_lane_col(gate, head) * ocmp_ref[r, :] + _lane_col(gate, NSA_HEADS + head) * osel[r, :]
                   + _lane_col(gate, 2 * NSA_HEADS + head) * owin_ref[r, :])
            sl = slice(g * NSA_HD, (g + 1) * NSA_HD)
            o_ref[:, sl] = (mix * zb[:, sl]).astype(o_ref.dtype)


def _nsa_sample_sel_call(ya, yb, ks16, vs16, idx, page_table, pool_k, pool_v, tbl, o_cmp, o_win, bg_r, *, B, T):
    tp = SAMPLE_PAD_T
    rows = NSA_G * tp
    gw = NSA_G * NSA_HD
    n_pages = page_table.shape[1]
    halves = PAGE_SIZE // SEL_BLOCK
    idx_flat = idx[:, :, :T, :N_SEL].reshape(-1)
    view_k, view_v = _pool_rows(pool_k), _pool_rows(pool_v)

    def blk_spec(j):
        def index(b, h, t, idx_s, pt_s):
            blk = idx_s[((b * NSA_KVH + h) * T + t) * N_SEL + j]
            page = pt_s[b * n_pages + jnp.minimum(blk // halves, n_pages - 1)]
            return (page * halves + blk % halves, 0)
        return pl.BlockSpec((SEL_BLOCK * NSA_KVH, NSA_HD), index)

    o_spec = pl.BlockSpec((None, None, rows, NSA_HD), lambda b, h, t, *_: (b, h, 0, 0))
    grid_spec = pltpu.PrefetchScalarGridSpec(
        num_scalar_prefetch=2,
        grid=(B, NSA_KVH, T),
        in_specs=[pl.BlockSpec((tp, gw), lambda b, h, t, *_: (b, EVEN_A["qb"] // gw + h)),
                  pl.BlockSpec((tp, NSA_HD), lambda b, h, t, *_: (b, h)),
                  pl.BlockSpec((tp, NSA_HD), lambda b, h, t, *_: (b, h)),
                  pl.BlockSpec((None, NEAR_BLOCKS + 1, rows, LANES), lambda b, h, t, *_: (h, 0, 0, 0)),
                  o_spec, o_spec,
                  pl.BlockSpec((tp, LANES), lambda b, h, t, *_: (b, EVEN_B["gb"] // LANES)),
                  pl.BlockSpec((1, LANES), lambda b, h, t, *_: (0, 0)),
                  pl.BlockSpec((tp, gw), lambda b, h, t, *_: (b, EVEN_B["zb"] // gw + h))]
        + [blk_spec(j) for j in range(N_SEL)] * 2,
        out_specs=pl.BlockSpec((tp, gw), lambda b, h, t, *_: (b, h)),
        scratch_shapes=[pltpu.VMEM((rows, NSA_HD), F32)],
    )
    return pl.pallas_call(
        functools.partial(_nsa_sample_sel_body, T=T),
        grid_spec=grid_spec,
        out_shape=jax.ShapeDtypeStruct((B * tp, NSA_W), BF16),
        compiler_params=_params(("arbitrary", "arbitrary", "arbitrary")),
        name="nsa_sample_sel",
    )(idx_flat, page_table.reshape(-1), ya, ks16, vs16, tbl, o_cmp, o_win, yb, bg_r, yb,
      *([view_k] * N_SEL), *([view_v] * N_SEL))


def _sample_bias_tables(rel_bias, T, wb):
    tp = SAMPLE_PAD_T
    ncmp = PAST_LEN // CMP_STRIDE
    wlen = -(-(wb + tp) // LANES) * LANES
    first = PAST_LEN // SEL_BLOCK - NEAR_BLOCKS
    assert PAST_LEN - ((first + 1) * SEL_BLOCK - 1) >= REL_MAX_DIST
    lo, hi = -wlen, PAST_LEN + tp
    rev = _bias_line(rel_bias, lo, hi, descending=True)

    def rows(tbl):
        return tbl.reshape(NSA_KVH, NSA_G * tp, tbl.shape[-1])

    t_c = _toeplitz(rev, hi - 1 - (PAST_LEN - (CMP_BLOCK - 1)), tp, CMP_STRIDE * ncmp)[:, :, ::CMP_STRIDE]
    t_w = _toeplitz(rev, hi - 1 - wb, tp, wlen)
    far = jnp.broadcast_to(rev[:, hi - 1 - REL_MAX_DIST][:, None, None], (NSA_HEADS, tp, LANES))
    near = []
    for k in range(1, NEAR_BLOCKS + 1):
        half = _toeplitz(rev, hi - 1 - (PAST_LEN - (first + k) * SEL_BLOCK), tp, SEL_BLOCK)
        near.append(jnp.concatenate([half, half], axis=-1))
    t_s = jnp.stack([far] + near, axis=1).reshape(NSA_KVH, NSA_G, NEAR_BLOCKS + 1, tp, LANES)
    t_s = t_s.transpose(0, 2, 1, 3, 4).reshape(NSA_KVH, NEAR_BLOCKS + 1, NSA_G * tp, LANES)
    return rows(t_c), rows(t_w), t_s


def _tail_even(w):
    return _tail_relayout(w, EVEN_KV_OFF + 6 * NSA_KV_W, 3 * NSA_HEADS, NSA_W + MEM_W, EVEN_B_N)


def _tail_odd(w):
    return _tail_relayout(w, ODD_A_N, 2 * ML_HEADS, ML_V_W + MEM_W, ODD_B_N)


def _gate_bias_even(b_gate):
    return jnp.pad(b_gate, (0, LANES - 3 * NSA_HEADS)).reshape(1, LANES)


def _gate_bias_odd(b_if):
    return jnp.pad(b_if.reshape(2 * ML_HEADS), (0, LANES - 2 * ML_HEADS)).reshape(1, LANES)


def _rel_bucket(dist):
    n = np.maximum(dist, 0)
    exact = REL_BUCKETS // 2
    nf = np.maximum(n, 1).astype(np.float32)
    large = exact + (np.log(nf / exact) / math.log(REL_MAX_DIST / exact) * (REL_BUCKETS - exact)).astype(np.int32)
    return np.where(n < exact, n, np.minimum(large, REL_BUCKETS - 1))


def _bias_line(rel_bias, lo, hi, descending=False):
    dist = np.arange(hi - 1, lo - 1, -1) if descending else np.arange(lo, hi)
    buckets = _rel_bucket(dist)
    edges = np.flatnonzero(np.diff(buckets)) + 1
    starts = np.concatenate([[0], edges])
    ends = np.concatenate([edges, [hi - lo]])
    bias_t = rel_bias.T.astype(F32)
    runs = [jnp.broadcast_to(bias_t[:, int(buckets[s])][:, None], (NSA_HEADS, int(e - s))) for s, e in zip(starts, ends)]
    return jnp.concatenate(runs, axis=1)


def _skew_rows(v, rows, step, cols):
    n = v.shape[1]
    reps = -(-rows * (n + step) // n)
    return jnp.tile(v, (1, reps))[:, :rows * (n + step)].reshape(v.shape[0], rows, n + step)[:, :, :cols]


def _toeplitz(rev, start, rows, cols):
    seg = rev[:, start - (rows - 1):start + cols]
    return _skew_rows(jnp.roll(seg, -(rows - 1), axis=1), rows, -1, cols)


def _prompt_bias_tables(rel_bias, T):
    ncmp = T // CMP_STRIDE
    assert Q_BLOCK + 1 >= REL_MAX_DIST
    lo, hi = -(CMP_STRIDE * ncmp + CMP_BLOCK), T
    line = _bias_line(rel_bias, lo, hi)
    rev = _bias_line(rel_bias, lo, hi, descending=True)

    def split(tbl):
        return tbl.reshape((NSA_KVH, NSA_G) + tbl.shape[1:])

    back = CMP_STRIDE * (ncmp - 1)
    first = -(back + CMP_BLOCK - 1) - lo
    seg = line[:, first:first + T + back]
    t_c = _skew_rows(jnp.roll(seg, -back, axis=1), ncmp, -CMP_STRIDE, T).swapaxes(1, 2)
    far = rev[:, hi - 1 - REL_MAX_DIST]
    t_near = _toeplitz(rev, hi - 1 - REL_MAX_DIST, Q_BLOCK, NEAR_COLS) - far[:, None, None]
    return split(t_c), split(t_near)


def _nsa_sample(ya, yb, kv16, page_table, pk_cmp, pv_cmp, pk_sel, pv_sel, wk, wv, bg_r, w1, b1, w2, pe, rel_bias,
                *, B, T):
    assert (PAST_LEN + T) // CMP_STRIDE == PAST_LEN // CMP_STRIDE
    abk = _cmp_pages_call(pk_cmp, page_table, w1[0], pe[0], B=B)
    abv = _cmp_pages_call(pv_cmp, page_table, w1[1], pe[1], B=B)
    bias_c, bias_w, tbl = _sample_bias_tables(rel_bias, T, wk.shape[1])
    o_cmp, o_win, idx = _nsa_sample_main_call(ya, kv16[4], kv16[5], abk, abv, b1, w2, wk, wv, bias_c, bias_w, B=B, T=T)
    return _nsa_sample_sel_call(ya, yb, kv16[2], kv16[3], idx, page_table, pk_sel, pv_sel, tbl, o_cmp, o_win, bg_r,
                                B=B, T=T)


def _kv_project(x, wt):
    outs = [_matmul_heads(x, wt, first=EVEN_KV_OFF + j * NSA_KV_W, transposed=True) for j in range(6)]
    return [o[0] for o in outs], [o[1] for o in outs]


def _even_prompt(hp2d, npre, mk16, mv16, wt, wt_b, bg_r, w1, b1, w2, pe, lb, g_norm, w_out, rel_bias, *, B, T):
    ya, yb = _matmul_nt(npre, wt, tm=W_TILE_M, tn=W_TILE_N, rows=(0, EVEN_A_N)), _matmul_nt(npre, wt_b)
    kv32, kv16 = _kv_project(npre, wt)
    oa, s_new = _hgrn_call(ya, jnp.zeros((B, HG_HEADS, HG_DK, HG_DV), F32), lb, g_norm, B=B, T=T, L=CHUNK, valid=CHUNK)
    kcmp = _compress_call(kv16[0], w1[0], b1[0], w2[0], pe[0], B=B, T=T)
    vcmp = _compress_call(kv16[1], w1[1], b1[1], w2[1], pe[1], B=B, T=T)
    ob = _nsa_prompt_call(ya, yb, kv16[2:], kcmp, vcmp, bg_r, *_prompt_bias_tables(rel_bias, T), B=B, T=T)
    om = _mem_call(yb, EVEN_B["qm"], mk16, mv16, B=B, T=T)
    h_new = _outproj([oa, ob, om], w_out, hp2d)
    wb = min(WINDOW, T)
    rows = [r.reshape(B, T, NSA_KVH, NSA_HD) for r in kv32]
    return h_new, (rows[0], rows[1], rows[2], rows[3], rows[4][:, -wb:], rows[5][:, -wb:], s_new)


def _even_sample(hs2d, nsam, mk_s, mv_s, page_table, pk_cmp, pv_cmp, pk_sel, pv_sel, wk, wv, s0,
                 wt, wt_b, bg_r, w1, b1, w2, pe, lb, g_norm, w_out, rel_bias, *, B, T):
    tp = SAMPLE_PAD_T
    ya, yb = _matmul_nt(nsam, wt, tm=W_TILE_M, tn=W_TILE_N, rows=(0, EVEN_A_N)), _matmul_nt(nsam, wt_b)
    kv32, kv16 = _kv_project(nsam, wt)
    oa, s_new = _hgrn_call(ya, s0, lb, g_norm, B=B, T=tp, L=tp, valid=T)
    ob = _nsa_sample(ya, yb, kv16, page_table, pk_cmp, pv_cmp, pk_sel, pv_sel, wk, wv, bg_r, w1, b1, w2, pe, rel_bias,
                     B=B, T=T)
    om = _mem_call(yb, EVEN_B["qm"], mk_s.reshape(B * N_MEM, MEM_W), mv_s.reshape(B * N_MEM, MEM_W), B=B, T=tp)
    rows = [r.reshape(B, tp, NSA_KVH, NSA_HD)[:, :T] for r in kv32]
    wb = wk.shape[1]
    win_k = jnp.concatenate([wk, rows[4]], axis=1)[:, -wb:]
    win_v = jnp.concatenate([wv, rows[5]], axis=1)[:, -wb:]
    return _outproj([oa, ob, om], w_out, hs2d), (rows[0], rows[1], rows[2], rows[3], win_k, win_v, s_new)


def _odd_mix(h2d, hn, k2d, v2d, c0, n0, m0, wt, wt_b, bif_r, g_norm, w_out, *, B, T, L, valid):
    ya, yb = _matmul_nt(hn, wt, tm=W_TILE_M, tn=W_TILE_N, rows=(0, ODD_A_N)), _matmul_nt(hn, wt_b)
    h, c_new, n_new, m_new = _mlstm_call(ya, yb, c0, n0, m0, bif_r, g_norm, B=B, T=T, L=L, valid=valid)
    om = _mem_call(yb, ODD_B["qm"], k2d, v2d, B=B, T=T)
    return _outproj([h, om], w_out, h2d), (c_new, n_new, m_new)


def _stack(lst, i):
    return jnp.stack([t[i] for t in lst])


def kernel(x_prompt, x_sample, cache_mem_k, cache_mem_v, cache_cmp_k, cache_cmp_v, cache_sel_k, cache_sel_v,
           cache_win_k, cache_win_v, state_hgrn, state_mlstm_c, state_mlstm_n, state_mlstm_m, page_table,
           mem_prompt, norm_w, mem_norm_w, final_norm_w, rel_bias, w_mem_kv, w_in_even, b_nsa_gate,
           w_cmp1, b_cmp1, w_cmp2, pe_cmp, hgrn_lb_logits, hgrn_norm_w, w_out_even, w_in_odd, b_mlstm_if,
           mlstm_norm_w, w_out_odd):
    bp, tp = x_prompt.shape[:2]
    bs, ts = x_sample.shape[:2]
    tsp = SAMPLE_PAD_T
    lbs = jnp.cumsum(jax.nn.softmax(hgrn_lb_logits.astype(F32), axis=0), axis=0)
    hp = x_prompt.reshape(bp * tp, D_MODEL)
    hs = jnp.pad(x_sample, ((0, 0), (0, tsp - ts), (0, 0))).reshape(bs * tsp, D_MODEL)
    mem2d = mem_prompt.reshape(bp * N_MEM, D_MODEL)
    mem_new, even_p, even_s, odd_p, odd_s = [], [], [], [], []
    for l in range(DEPTH):
        npre = _rmsnorm_rows(hp, norm_w[l], BF16)
        nsam = _rmsnorm_rows(hs, norm_w[l], BF16)
        nmem = _rmsnorm_rows(mem2d, mem_norm_w[l], BF16)
        mk32, mk16 = _matmul_heads(nmem, w_mem_kv[l], first=0)
        mv32, mv16 = _matmul_heads(nmem, w_mem_kv[l], first=MEM_W)
        mem_new.append((mk32.reshape(bp, N_MEM, MEM_HEADS, MEM_HD), mv32.reshape(bp, N_MEM, MEM_HEADS, MEM_HD)))
        mk_s, mv_s = cache_mem_k[l], cache_mem_v[l]
        if l % 2 == 0:
            e = l // 2
            w_in = w_in_even[e].T
            w_b = _tail_even(w_in)
            w_out = w_out_even[e].astype(BF16)
            bg_r = _gate_bias_even(b_nsa_gate[e])
            cmpw = (w_cmp1[e].reshape(2, CMP_BLOCK, NSA_HD, NSA_HD), b_cmp1[e], w_cmp2[e], pe_cmp[e])
            hp, st_p = _even_prompt(hp, npre, mk16, mv16, w_in, w_b, bg_r, *cmpw, lbs[l], hgrn_norm_w[e], w_out,
                                    rel_bias, B=bp, T=tp)
            hs, st_s = _even_sample(hs, nsam, mk_s, mv_s, page_table, cache_cmp_k[e], cache_cmp_v[e], cache_sel_k[e],
                                    cache_sel_v[e], cache_win_k[e], cache_win_v[e], state_hgrn[e], w_in, w_b, bg_r,
                                    *cmpw, lbs[l], hgrn_norm_w[e], w_out, rel_bias, B=bs, T=ts)
            even_p.append(st_p)
            even_s.append(st_s)
        else:
            o = l // 2
            w_in = w_in_odd[o].T
            w_b = _tail_odd(w_in)
            w_out = w_out_odd[o].astype(BF16)
            bif_r = _gate_bias_odd(b_mlstm_if[o])
            hp, st_p = _odd_mix(hp, npre, mk16, mv16, jnp.zeros((bp, ML_HEADS, ML_DV, ML_DK), F32),
                                jnp.zeros((bp, ML_HEADS, ML_DK), F32), jnp.zeros((bp, ML_HEADS), F32),
                                w_in, w_b, bif_r, mlstm_norm_w[o], w_out, B=bp, T=tp, L=ML_CHUNK, valid=ML_CHUNK)
            hs, st_s = _odd_mix(hs, nsam, mk_s.reshape(bs * N_MEM, MEM_W), mv_s.reshape(bs * N_MEM, MEM_W),
                                state_mlstm_c[o], state_mlstm_n[o], state_mlstm_m[o],
                                w_in, w_b, bif_r, mlstm_norm_w[o], w_out, B=bs, T=tsp, L=tsp, valid=ts)
            odd_p.append(st_p)
            odd_s.append(st_s)
    y_prompt = _rmsnorm_rows(hp, final_norm_w, F32).reshape(bp, tp, D_MODEL)
    y_sample = _rmsnorm_rows(hs, final_norm_w, F32).reshape(bs, tsp, D_MODEL)[:, :ts]
    return (y_prompt, y_sample,
            _stack(mem_new, 0), _stack(mem_new, 1),
            _stack(even_p, 0), _stack(even_p, 1), _stack(even_p, 2), _stack(even_p, 3),
            _stack(even_p, 4), _stack(even_p, 5), _stack(even_p, 6),
            _stack(odd_p, 0), _stack(odd_p, 1), _stack(odd_p, 2),
            _stack(even_s, 0), _stack(even_s, 1), _stack(even_s, 2), _stack(even_s, 3),
            _stack(even_s, 4), _stack(even_s, 5), _stack(even_s, 6),
            _stack(odd_s, 0), _stack(odd_s, 1), _stack(odd_s, 2))
```

```python
import functools
import math

import jax
import jax.numpy as jnp
import numpy as np
from jax import lax
from jax.experimental import pallas as pl
from jax.experimental.pallas import tpu as pltpu

D_MODEL = 4096
DEPTH = 2
PAST_LEN = 16384
PAGE_SIZE = 128
N_MEM = 256
EPS = 1e-6
CHUNK = 64

HG_DK = 128
HG_DV = 128
HG_HEADS = D_MODEL // 2 // HG_DV
HG_W = HG_HEADS * HG_DV

NSA_HD = 128
NSA_HEADS = D_MODEL // 2 // NSA_HD
NSA_KVH = 4
NSA_G = NSA_HEADS // NSA_KVH
NSA_W = NSA_HEADS * NSA_HD
NSA_KV_W = NSA_KVH * NSA_HD
CMP_BLOCK = 32
CMP_STRIDE = 16
SEL_BLOCK = 64
SEL_SHIFT = SEL_BLOCK.bit_length() - 1
N_SEL = 16
WINDOW = 512
Q_BLOCK = 256

ML_HEADS = D_MODEL // 512
ML_DK = D_MODEL // 2 // ML_HEADS
ML_DV = D_MODEL // ML_HEADS
ML_QK_W = ML_HEADS * ML_DK
ML_V_W = ML_HEADS * ML_DV

MEM_HEADS = 4
MEM_HD = 128
MEM_W = MEM_HEADS * MEM_HD

REL_BUCKETS = 32
REL_MAX_DIST = 128

F32 = jnp.float32
BF16 = jnp.bfloat16
LANES = 128
NEG_INF = float("-inf")
TINY = float(np.finfo(np.float32).tiny)
EXP_CLAMP = 80.0
VMEM_LIMIT = 56 * 1024 * 1024

HG_HB = 16
ML_HB = 4
ML_CHUNK = 256
W_TILE_M, W_TILE_N = 1024, 512
HG_SUB = 16
SAMPLE_PAD_T = 16

MM_TILE_N = 1024
EVEN_A = {"qa": 0, "fa": HG_W, "ia": 2 * HG_W, "za": 3 * HG_W, "qb": 4 * HG_W}
EVEN_A_N = 4 * HG_W + NSA_W
EVEN_B = {"zb": 0, "qm": NSA_W, "gb": NSA_W + MEM_W}
EVEN_B_N = -(-(NSA_W + MEM_W + LANES) // MM_TILE_N) * MM_TILE_N
EVEN_KV_OFF = EVEN_A_N
ODD_A = {"q": 0, "k": ML_QK_W, "v": 2 * ML_QK_W, "og": 2 * ML_QK_W + ML_V_W}
ODD_A_N = 2 * ML_QK_W + 2 * ML_V_W
ODD_B = {"z": 0, "qm": ML_V_W, "gates": ML_V_W + MEM_W}
ODD_B_N = -(-(ML_V_W + MEM_W + LANES) // MM_TILE_N) * MM_TILE_N


def _dot(a, b):
    return jnp.dot(a, b, preferred_element_type=F32)


def _dot_nt(a, b):
    return lax.dot_general(a, b, (((1,), (1,)), ((), ())), preferred_element_type=F32)


def _dot_tn(a, b):
    return lax.dot_general(a, b, (((0,), (0,)), ((), ())), preferred_element_type=F32)


def _iota2(shape, dim):
    return lax.broadcasted_iota(jnp.int32, shape, dim)


def _cumsum_rows(x, tri_b):
    hi = x.astype(BF16)
    r1 = x - hi.astype(F32)
    mid = r1.astype(BF16)
    lo = (r1 - mid.astype(F32)).astype(BF16)
    return _dot(tri_b, hi) + _dot(tri_b, mid) + _dot(tri_b, lo)


def _row_to_col(row, n):
    eye = _iota2((n, n), 0) == _iota2((n, n), 1)
    return jnp.sum(jnp.where(eye, row, 0.0), axis=1, keepdims=True)


def _col_to_row(col, n):
    eye = _iota2((n, n), 0) == _iota2((n, n), 1)
    return jnp.sum(jnp.where(eye, col, 0.0), axis=0, keepdims=True)


def _lane_col(x, idx):
    return jnp.sum(jnp.where(_iota2(x.shape, 1) == idx, x, 0.0), axis=1, keepdims=True)


def _silu(x):
    return x * jax.nn.sigmoid(x)


def _params(sem):
    return pltpu.CompilerParams(dimension_semantics=sem, vmem_limit_bytes=VMEM_LIMIT)


def _rmsnorm_body(x_ref, w_ref, o_ref):
    x = x_ref[...].astype(F32)
    y = x * lax.rsqrt(jnp.mean(x * x, axis=-1, keepdims=True) + EPS)
    o_ref[...] = (y * w_ref[...].astype(F32)).astype(o_ref.dtype)


def _rmsnorm_rows(x2d, w, out_dtype, tm=512):
    m, d = x2d.shape
    tm = min(tm, m)
    return pl.pallas_call(
        _rmsnorm_body,
        grid=(m // tm,),
        in_specs=[pl.BlockSpec((tm, d), lambda i: (i, 0)), pl.BlockSpec((1, d), lambda i: (0, 0))],
        out_specs=pl.BlockSpec((tm, d), lambda i: (i, 0)),
        out_shape=jax.ShapeDtypeStruct((m, d), out_dtype),
        compiler_params=_params(("parallel",)),
        name="rmsnorm",
    )(x2d, w.reshape(1, d))


def _matmul_nt_body(a_ref, bt_ref, o_ref):
    o_ref[...] = _dot_nt(a_ref[...], bt_ref[...].astype(BF16))


def _matmul_nt(a, bt, tm=1024, tn=MM_TILE_N, rows=None):
    m, k = a.shape
    first, n = rows or (0, bt.shape[0])
    tm, tn = min(tm, m), min(tn, n)
    assert m % tm == 0 and n % tn == 0 and first % tn == 0, (a.shape, bt.shape, rows)
    j0 = first // tn
    return pl.pallas_call(
        _matmul_nt_body,
        grid=(m // tm, n // tn),
        in_specs=[pl.BlockSpec((tm, k), lambda i, j: (i, 0)), pl.BlockSpec((tn, k), lambda i, j: (j0 + j, 0))],
        out_specs=pl.BlockSpec((tm, tn), lambda i, j: (i, j)),
        out_shape=jax.ShapeDtypeStruct((m, n), F32),
        compiler_params=_params(("parallel", "parallel")),
        name="matmul",
    )(a, bt)


def _tail_body(lo_ref, hi_ref, gate_ref, o_ref, *, shift, n_main):
    i = pl.program_id(0)
    main = jnp.concatenate([lo_ref[shift:, :], hi_ref[:shift, :]], axis=0)
    gates = jnp.where(_iota2((LANES, 1), 0) < shift, gate_ref[...], 0.0)
    o_ref[...] = jnp.where(i < n_main, main, jnp.where(i == n_main, gates, 0.0)).astype(o_ref.dtype)


def _tail_relayout(wt, first, shift, main, out_rows):
    n, k = wt.shape
    assert first % LANES == 0 and main % LANES == 0 and out_rows % LANES == 0 and shift % 8 == 0 and shift < LANES
    assert first + shift + main == n
    c0, n_main = first // LANES, main // LANES
    return pl.pallas_call(
        functools.partial(_tail_body, shift=shift, n_main=n_main),
        grid=(out_rows // LANES,),
        in_specs=[pl.BlockSpec((LANES, k), lambda i: (c0 + jnp.minimum(i, n_main - 1), 0)),
                  pl.BlockSpec((LANES, k), lambda i: (c0 + jnp.minimum(i, n_main - 1) + 1, 0)),
                  pl.BlockSpec((LANES, k), lambda i: (c0, 0))],
        out_specs=pl.BlockSpec((LANES, k), lambda i: (i, 0)),
        out_shape=jax.ShapeDtypeStruct((out_rows, k), BF16),
        compiler_params=_params(("parallel",)),
        name="tail_relayout",
    )(wt, wt, wt)


def _matmul_heads_body(a_ref, b_ref, o32_ref, o16_ref, *, transposed):
    b = b_ref[...].astype(BF16)
    acc = _dot_nt(a_ref[...], b) if transposed else _dot(a_ref[...], b)
    for h in range(MEM_HEADS):
        o32_ref[:, h, :] = acc[:, h * LANES:(h + 1) * LANES]
    o16_ref[...] = acc.astype(BF16)


def _matmul_heads(a, b, first=0, transposed=False, tm=1024):
    m, k = a.shape
    n = MEM_HEADS * LANES
    tm = min(tm, m)
    assert m % tm == 0 and first % n == 0, (a.shape, b.shape, first)
    j0 = first // n
    b_spec = pl.BlockSpec((n, k), lambda i: (j0, 0)) if transposed else pl.BlockSpec((k, n), lambda i: (0, j0))
    return pl.pallas_call(
        functools.partial(_matmul_heads_body, transposed=transposed),
        grid=(m // tm,),
        in_specs=[pl.BlockSpec((tm, k), lambda i: (i, 0)), b_spec],
        out_specs=[pl.BlockSpec((tm, MEM_HEADS, LANES), lambda i: (i, 0, 0)), pl.BlockSpec((tm, n), lambda i: (i, 0))],
        out_shape=[jax.ShapeDtypeStruct((m, MEM_HEADS, LANES), F32), jax.ShapeDtypeStruct((m, n), BF16)],
        compiler_params=_params(("parallel",)),
        name="matmul_heads",
    )(a, b)


def _outproj_body(*refs, widths):
    xs = refs[:len(widths)]
    w_ref, r_ref, o_ref = refs[len(widths):]
    acc = r_ref[...]
    off = 0
    for x_ref, w in zip(xs, widths):
        acc = acc + _dot(x_ref[...], w_ref[off:off + w, :])
        off += w
    o_ref[...] = acc


def _outproj(xs, w_bf16, resid, tm=1024, tn=512):
    m = resid.shape[0]
    n = w_bf16.shape[1]
    widths = tuple(x.shape[1] for x in xs)
    assert sum(widths) == w_bf16.shape[0]
    tm = min(tm, m)
    in_specs = [pl.BlockSpec((tm, w), lambda i, j: (i, 0)) for w in widths]
    in_specs += [pl.BlockSpec((w_bf16.shape[0], tn), lambda i, j: (0, j)), pl.BlockSpec((tm, tn), lambda i, j: (i, j))]
    return pl.pallas_call(
        functools.partial(_outproj_body, widths=widths),
        grid=(m // tm, n // tn),
        in_specs=in_specs,
        out_specs=pl.BlockSpec((tm, tn), lambda i, j: (i, j)),
        out_shape=jax.ShapeDtypeStruct((m, n), F32),
        compiler_params=_params(("parallel", "parallel")),
        name="outproj",
    )(*xs, w_bf16, resid)


def _hgrn_body(qa_ref, fa_ref, ia_ref, za_ref, lb_ref, gn_ref, s0_ref, o_ref, s_out, s_scr, *, L, valid):
    c = pl.program_id(2)

    @pl.when(c == 0)
    def _():
        s_scr[...] = s0_ref[...]

    lb = lb_ref[...]
    sig = jax.nn.sigmoid(fa_ref[...])
    logf = jnp.log(lb + (1.0 - lb) * sig)
    kk = (1.0 - lb) * (1.0 - sig)
    if valid < L:
        live = _iota2((L, 1), 0) < valid
        logf = jnp.where(live, logf, 0.0)
        kk = jnp.where(live, kk, 0.0)
    tri_b = (_iota2((L, L), 0) >= _iota2((L, L), 1)).astype(BF16)
    bc = _cumsum_rows(logf, tri_b)
    q = _silu(qa_ref[...])
    gate = _silu(za_ref[...])
    v = ia_ref[...]
    gn = gn_ref[...]
    nsub = L // HG_SUB
    rr = _iota2((L, nsub * L), 0)
    cc = _iota2((L, nsub * L), 1)
    keep = ((jnp.right_shift(cc, L.bit_length() - 1) == jnp.right_shift(rr, HG_SUB.bit_length() - 1))
            & (jnp.bitwise_and(cc, L - 1) <= rr))
    for j in range(HG_HB):
        sl = slice(j * HG_DK, (j + 1) * HG_DK)
        bj, qj, kj = bc[:, sl], q[:, sl], kk[:, sl]
        vb = v[:, sl].astype(BF16)
        s_prev = s_scr[j]
        mids = [bj[i * HG_SUB + HG_SUB // 2:i * HG_SUB + HG_SUB // 2 + 1, :] for i in range(nsub)]
        mid_rows = jnp.concatenate([jnp.broadcast_to(m, (HG_SUB, HG_DK)) for m in mids], axis=0)
        q_dec = qj * jnp.exp(jnp.minimum(bj - mid_rows, EXP_CLAMP))
        k_dec = jnp.concatenate([kj * jnp.exp(jnp.minimum(m - bj, EXP_CLAMP)) for m in mids], axis=0)
        att = jnp.where(keep, _dot_nt(q_dec.astype(BF16), k_dec.astype(BF16)), 0.0)
        q_state = (qj * jnp.exp(bj)).astype(BF16)
        v_rep = jnp.concatenate([vb] * nsub, axis=0)
        if (nsub * L) % LANES == 0:
            o = _dot(jnp.concatenate([q_state, att.astype(BF16)], axis=1),
                     jnp.concatenate([s_prev.astype(BF16), v_rep], axis=0))
        else:
            o = _dot(q_state, s_prev.astype(BF16)) + _dot(att.astype(BF16), v_rep)
        o_n = o * lax.rsqrt(jnp.mean(o * o, axis=-1, keepdims=True) + EPS) * gn
        o_ref[:, sl] = (o_n * gate[:, sl]).astype(o_ref.dtype)
        bl = bj[L - 1:L, :]
        kd = kj * jnp.exp(bl - bj)
        s_scr[j] = _row_to_col(jnp.exp(bl), HG_DK) * s_prev + _dot_tn(kd.astype(BF16), vb)

    @pl.when(c == pl.num_programs(2) - 1)
    def _():
        s_out[...] = s_scr[...]


def _hgrn_call(y, s0, lb, gn, *, B, T, L, valid):
    nc = T // L
    w = HG_HB * HG_DK

    def col(name):
        blk = EVEN_A[name] // w
        return pl.BlockSpec((L, w), lambda b, hg, c: (b * nc + c, blk + hg))

    state_spec = pl.BlockSpec((None, HG_HB, HG_DK, HG_DV), lambda b, hg, c: (b, hg, 0, 0))
    return pl.pallas_call(
        functools.partial(_hgrn_body, L=L, valid=valid),
        grid=(B, HG_HEADS // HG_HB, nc),
        in_specs=[col("qa"), col("fa"), col("ia"), col("za"),
                  pl.BlockSpec((1, w), lambda b, hg, c: (0, hg)),
                  pl.BlockSpec((1, HG_DV), lambda b, hg, c: (0, 0)),
                  state_spec],
        out_specs=[pl.BlockSpec((L, w), lambda b, hg, c: (b * nc + c, hg)), state_spec],
        out_shape=[jax.ShapeDtypeStruct((B * T, HG_W), BF16),
                   jax.ShapeDtypeStruct((B, HG_HEADS, HG_DK, HG_DV), F32)],
        scratch_shapes=[pltpu.VMEM((HG_HB, HG_DK, HG_DV), F32)],
        compiler_params=_params(("arbitrary", "arbitrary", "arbitrary")),
        name="hgrn2",
    )(y, y, y, y, lb.reshape(1, HG_W), gn.reshape(1, HG_DV), s0)


def _mlstm_body(q_ref, k_ref, v_ref, og_ref, z_ref, g_ref, bif_ref, gn_ref, c0_ref, n0_ref, m0_ref,
                h_ref, c_out, n_out, m_out, c_scr, n_scr, m_scr, *, L, valid):
    c = pl.program_id(2)

    @pl.when(c == 0)
    def _():
        c_scr[...] = c0_ref[...]
        n_scr[...] = n0_ref[...]
        m_scr[...] = m0_ref[...]

    gates = g_ref[...] + bif_ref[...]
    log_i = gates
    log_f = jnp.minimum(gates, 0.0) - jnp.log(1.0 + jnp.exp(-jnp.abs(gates)))
    if valid < L:
        live = _iota2((L, 1), 0) < valid
        log_i = jnp.where(live, log_i, -1e30)
        log_f = jnp.where(live, log_f, 0.0)
    tri = _iota2((L, L), 0) >= _iota2((L, L), 1)
    bcs = _cumsum_rows(log_f, tri.astype(BF16))
    for j in range(ML_HB):
        head = pl.program_id(1) * ML_HB + j
        b_col = _lane_col(bcs, ML_HEADS + head)
        i_col = _lane_col(log_i, head)
        b_row = _col_to_row(b_col, L)
        i_row = _col_to_row(i_col, L)
        m_prev = m_scr[:, j:j + 1]
        dmat = jnp.where(tri, b_col - b_row + i_row, NEG_INF)
        inter = b_col + m_prev
        mt = jnp.maximum(inter, jnp.max(dmat, axis=1, keepdims=True))
        w_in = jnp.exp(dmat - mt)
        w_x = jnp.exp(inter - mt)
        qj = q_ref[:, j * ML_DK:(j + 1) * ML_DK]
        kj = k_ref[:, j * ML_DK:(j + 1) * ML_DK] * (ML_DK ** -0.5)
        vj = v_ref[:, j * ML_DV:(j + 1) * ML_DV]
        qb, kb = qj.astype(BF16), kj.astype(BF16)
        sw = _dot_nt(qb, kb) * w_in
        c_prev = c_scr[j]
        n_prev = n_scr[:, j * ML_DK:(j + 1) * ML_DK]
        num = w_x * _dot_nt(qb, c_prev.astype(BF16)) + _dot(sw.astype(BF16), vj.astype(BF16))
        den = w_x * jnp.sum(qj * n_prev, axis=1, keepdims=True) + jnp.sum(sw, axis=1, keepdims=True)
        h = num / jnp.maximum(jnp.abs(den), jnp.exp(-mt))
        m_last = mt[L - 1:L, :]
        b_last = b_col[L - 1:L, :]
        w_end = jnp.exp(b_last - b_col + i_col - m_last)
        d_c = jnp.exp(b_last + m_prev - m_last)
        c_scr[j] = d_c * c_prev + _dot_tn((w_end * vj).astype(BF16), kb)
        n_scr[:, j * ML_DK:(j + 1) * ML_DK] = d_c * n_prev + jnp.sum(w_end * kj, axis=0, keepdims=True)
        m_scr[:, j:j + 1] = m_last
        sv = slice(j * ML_DV, (j + 1) * ML_DV)
        h_n = h * lax.rsqrt(jnp.mean(h * h, axis=-1, keepdims=True) + EPS) * gn_ref[:, sv]
        h_ref[:, sv] = (h_n * jax.nn.sigmoid(og_ref[:, sv]) * _silu(z_ref[:, sv])).astype(h_ref.dtype)

    @pl.when(c == pl.num_programs(2) - 1)
    def _():
        c_out[...] = c_scr[...]
        n_out[...] = n_scr[...]
        m_out[...] = m_scr[...]


def _mlstm_call(ya, yb, c0, n0, m0, bif_r, gn, *, B, T, L, valid):
    nc = T // L
    ng = ML_HEADS // ML_HB
    wk, wv = ML_HB * ML_DK, ML_HB * ML_DV

    def col(name, w):
        blk = (ODD_A[name] if name in ODD_A else ODD_B[name]) // w
        return pl.BlockSpec((L, w), lambda b, hg, c: (b * nc + c, blk + hg))

    c_spec = pl.BlockSpec((None, ML_HB, ML_DV, ML_DK), lambda b, hg, c: (b, hg, 0, 0))
    n_spec = pl.BlockSpec((None, 1, wk), lambda b, hg, c: (b, 0, hg))
    m_spec = pl.BlockSpec((None, None, 1, LANES), lambda b, hg, c: (b, hg, 0, 0))
    m0_r = jnp.pad(m0.reshape(B, ng, 1, ML_HB), ((0, 0), (0, 0), (0, 0), (0, LANES - ML_HB)))
    h, c_new, n_new, m_new = pl.pallas_call(
        functools.partial(_mlstm_body, L=L, valid=valid),
        grid=(B, ng, nc),
        in_specs=[col("q", wk), col("k", wk), col("v", wv), col("og", wv), col("z", wv),
                  pl.BlockSpec((L, LANES), lambda b, hg, c: (b * nc + c, ODD_B["gates"] // LANES)),
                  pl.BlockSpec((1, LANES), lambda b, hg, c: (0, 0)),
                  pl.BlockSpec((1, wv), lambda b, hg, c: (0, hg)),
                  c_spec, n_spec, m_spec],
        out_specs=[pl.BlockSpec((L, wv), lambda b, hg, c: (b * nc + c, hg)), c_spec, n_spec, m_spec],
        out_shape=[jax.ShapeDtypeStruct((B * T, ML_V_W), BF16),
                   jax.ShapeDtypeStruct((B, ML_HEADS, ML_DV, ML_DK), F32),
                   jax.ShapeDtypeStruct((B, 1, ML_QK_W), F32),
                   jax.ShapeDtypeStruct((B, ng, 1, LANES), F32)],
        scratch_shapes=[pltpu.VMEM((ML_HB, ML_DV, ML_DK), F32), pltpu.VMEM((1, wk), F32), pltpu.VMEM((1, LANES), F32)],
        compiler_params=_params(("arbitrary", "arbitrary", "arbitrary")),
        name="mlstm",
    )(ya, ya, ya, ya, yb, yb, bif_r, gn.reshape(1, ML_V_W), c0, n0.reshape(B, 1, ML_QK_W), m0_r)
    return h, c_new, n_new.reshape(B, ML_HEADS, ML_DK), m_new[:, :, 0, :ML_HB].reshape(B, ML_HEADS)


def _mem_body(q_ref, k_ref, v_ref, o_ref):
    q = q_ref[...] * (MEM_HD ** -0.5)
    for h in range(MEM_HEADS):
        sl = slice(h * MEM_HD, (h + 1) * MEM_HD)
        s = _dot_nt(q[:, sl].astype(BF16), k_ref[:, sl].astype(BF16))
        p = jnp.exp(s - jnp.max(s, axis=-1, keepdims=True))
        o = _dot(p.astype(BF16), v_ref[:, sl].astype(BF16)) / jnp.sum(p, axis=-1, keepdims=True)
        o_ref[:, sl] = o.astype(o_ref.dtype)


def _mem_call(y, q_off, k2d, v2d, *, B, T, tq=512):
    tq = min(tq, T)
    nq = T // tq
    qb = q_off // MEM_W
    return pl.pallas_call(
        _mem_body,
        grid=(B, nq),
        in_specs=[pl.BlockSpec((tq, MEM_W), lambda b, i: (b * nq + i, qb)),
                  pl.BlockSpec((N_MEM, MEM_W), lambda b, i: (b, 0)),
                  pl.BlockSpec((N_MEM, MEM_W), lambda b, i: (b, 0))],
        out_specs=pl.BlockSpec((tq, MEM_W), lambda b, i: (b * nq + i, 0)),
        out_shape=jax.ShapeDtypeStruct((B * T, MEM_W), BF16),
        compiler_params=_params(("parallel", "parallel")),
        name="mem_attn",
    )(y, k2d, v2d)


def _gelu_tanh(x):
    return 0.5 * x * (1.0 + jnp.tanh(math.sqrt(2.0 / math.pi) * (x + 0.044715 * (x * x * x))))


def _compress_body(x_ref, w1_ref, b1_ref, w2_ref, pe_ref, o_ref, x32, *, nch):
    x32[...] = x_ref[...].astype(F32)
    a = jnp.zeros((nch, NSA_HD), F32)
    b = jnp.zeros((nch, NSA_HD), F32)
    for s in range(CMP_STRIDE):
        r = x32[pl.ds(s, nch, stride=CMP_STRIDE), :]
        a = a + _dot((r + pe_ref[s:s + 1, :]).astype(BF16), w1_ref[s])
        b = b + _dot((r + pe_ref[CMP_STRIDE + s:CMP_STRIDE + s + 1, :]).astype(BF16), w1_ref[CMP_STRIDE + s])
    h = a + pltpu.roll(b, nch - 1, 0) + b1_ref[...]
    o_ref[...] = _dot(_gelu_tanh(h).astype(BF16), w2_ref[...])


def _compress_call(x16, w1, b1, w2, pe, *, B, T):
    nch = T // CMP_STRIDE
    return pl.pallas_call(
        functools.partial(_compress_body, nch=nch),
        grid=(B, NSA_KVH),
        in_specs=[pl.BlockSpec((T, NSA_HD), lambda b, h: (b, h)),
                  pl.BlockSpec((CMP_BLOCK, NSA_HD, NSA_HD), lambda b, h: (0, 0, 0)),
                  pl.BlockSpec((1, NSA_HD), lambda b, h: (0, 0)),
                  pl.BlockSpec((NSA_HD, NSA_HD), lambda b, h: (0, 0)),
                  pl.BlockSpec((CMP_BLOCK, NSA_HD), lambda b, h: (0, 0))],
        out_specs=pl.BlockSpec((None, None, nch, NSA_HD), lambda b, h: (b, h, 0, 0)),
        out_shape=jax.ShapeDtypeStruct((B, NSA_KVH, nch, NSA_HD), F32),
        scratch_shapes=[pltpu.VMEM((T, NSA_HD), F32)],
        compiler_params=_params(("parallel", "parallel")),
        name="nsa_compress",
    )(x16, w1.astype(BF16), b1.reshape(1, NSA_HD), w2.astype(BF16), pe)


def _softmax_rows(s):
    m = jnp.max(s, axis=-1, keepdims=True)
    m = jnp.where(m == NEG_INF, 0.0, m)
    p = jnp.exp(s - m)
    return p, jnp.sum(p, axis=-1, keepdims=True)


def _slc_scores(psum, width, n_slc):
    ncmp = psum.shape[1]
    d = _iota2((ncmp, width), 0) - (SEL_BLOCK // CMP_STRIDE) * _iota2((ncmp, width), 1)
    wgt = jnp.where((d == -1) | (d == 3), 1.0, jnp.where((d >= 0) & (d <= 2), 2.0, 0.0))
    wgt = jnp.where(_iota2((ncmp, width), 1) < n_slc, wgt, 0.0).astype(BF16)
    p_hi = psum.astype(BF16)
    p_lo = (psum - p_hi.astype(F32)).astype(BF16)
    return _dot(p_hi, wgt) + _dot(p_lo, wgt)


def _top_blocks(slc, cur, n_pick):
    rows, width = slc.shape
    blk = _iota2((rows, width), 1)
    forced = (blk == 0) | (blk == cur) | (blk == cur - 1)
    score = jnp.where(forced, jnp.inf, slc)
    score = jnp.where(blk > cur, NEG_INF, score)
    blk_f = blk.astype(F32)
    lane = _iota2((rows, LANES), 1)
    sel = jnp.zeros((rows, width), F32)
    picks = jnp.zeros((rows, LANES), F32)
    for i in range(n_pick):
        mx = jnp.max(score, axis=-1, keepdims=True)
        first = jnp.min(jnp.where(score == mx, blk_f, float(width)), axis=-1, keepdims=True)
        pick = blk_f == first
        sel = jnp.where(pick, 1.0, sel)
        picks = jnp.where(lane == i, first, picks)
        score = jnp.where(pick, NEG_INF, score)
    return sel, picks


def _member_by_rank(psum, tpos_row, n_slc, n_pick):
    nq, ncmp = psum.shape
    nb = -(-n_slc // 8) * 8
    d = _iota2((nb, ncmp), 1) - (SEL_BLOCK // CMP_STRIDE) * _iota2((nb, ncmp), 0)
    wgt = jnp.where((d == -1) | (d == 3), 1.0, jnp.where((d >= 0) & (d <= 2), 2.0, 0.0))
    wgt = jnp.where(_iota2((nb, ncmp), 0) < n_slc, wgt, 0.0).astype(BF16)
    p_hi = psum.astype(BF16)
    p_lo = (psum - p_hi.astype(F32)).astype(BF16)
    slc = _dot_nt(wgt, p_hi) + _dot_nt(wgt, p_lo)
    blk = _iota2((nb, nq), 0)
    cur = jnp.right_shift(tpos_row, SEL_SHIFT)
    forced = (blk == 0) | (blk == cur) | (blk == cur - 1)
    score = jnp.where(forced, jnp.inf, slc)
    score = jnp.where(blk > cur, NEG_INF, score)
    ahead = jnp.zeros((nb, nq), F32)
    for i in range(n_slc):
        s_i = score[i:i + 1, :]
        ahead = ahead + jnp.where((s_i > score) | ((s_i == score) & (blk > i)), 1.0, 0.0)
    return jnp.where((ahead < n_pick) & (blk <= cur), 1.0, 0.0)


NEAR_COLS = Q_BLOCK + REL_MAX_DIST


def _add_per_head(s, mask):
    return (s.reshape(NSA_G, mask.shape[0], mask.shape[1]) + mask[None]).reshape(s.shape)


def _banded_attention(q, k_ref, v_ref, start, width, mask, near_bias):
    far = width - NEAR_COLS
    s_far = _add_per_head(_dot_nt(q, k_ref[pl.ds(start, far), :]), mask[:, :far])
    s_near = _add_per_head(_dot_nt(q, k_ref[pl.ds(start + far, NEAR_COLS), :]) + near_bias, mask[:, far:])
    m = jnp.maximum(jnp.max(s_far, axis=-1, keepdims=True), jnp.max(s_near, axis=-1, keepdims=True))
    m = jnp.where(m == NEG_INF, 0.0, m)
    p_far, p_near = jnp.exp(s_far - m), jnp.exp(s_near - m)
    l = jnp.sum(p_far, axis=-1, keepdims=True) + jnp.sum(p_near, axis=-1, keepdims=True)
    o = (_dot(p_far.astype(BF16), v_ref[pl.ds(start, far), :])
         + _dot(p_near.astype(BF16), v_ref[pl.ds(start + far, NEAR_COLS), :]))
    return o / jnp.maximum(l, TINY)


def _nsa_prompt_body(q_ref, zb_ref, gb_ref, bg_ref, ks_ref, vs_ref, kw_ref, vw_ref, kc_ref, vc_ref,
                     bc_ref, bn_ref, o_ref, ksp, vsp, kwp, vwp, osel, *, T):
    qi = pl.program_id(2)
    tq = Q_BLOCK
    front = T - tq
    wlen = WINDOW + tq
    n_slc = T // SEL_BLOCK

    @pl.when(qi == 0)
    def _():
        ksp[0:front, :] = jnp.zeros((front, NSA_HD), BF16)
        vsp[0:front, :] = jnp.zeros((front, NSA_HD), BF16)
        ksp[front:front + T, :] = ks_ref[...].astype(BF16)
        vsp[front:front + T, :] = vs_ref[...].astype(BF16)
        kwp[0:WINDOW, :] = jnp.zeros((WINDOW, NSA_HD), BF16)
        vwp[0:WINDOW, :] = jnp.zeros((WINDOW, NSA_HD), BF16)
        kwp[WINDOW:WINDOW + T, :] = kw_ref[...].astype(BF16)
        vwp[WINDOW:WINDOW + T, :] = vw_ref[...].astype(BF16)

    t0 = pl.multiple_of(qi * tq, tq)
    tpos = _iota2((tq, 1), 0) + t0
    q_all = q_ref[...] * (NSA_HD ** -0.5)
    q = jnp.concatenate([q_all[:, g * NSA_HD:(g + 1) * NSA_HD] for g in range(NSA_G)], axis=0).astype(BF16)
    bias_near = bn_ref[...].reshape(NSA_G * tq, NEAR_COLS)

    ncmp = T // CMP_STRIDE
    vis = tpos >= _iota2((1, ncmp), 1) * CMP_STRIDE + (CMP_BLOCK - 1)
    s = _dot_nt(q, kc_ref[...].astype(BF16)) + bc_ref[...].reshape(NSA_G * tq, ncmp)
    p, l = _softmax_rows(_add_per_head(s, jnp.where(vis, 0.0, NEG_INF)))
    p = p / jnp.maximum(l, TINY)
    o_cmp = _dot(p.astype(BF16), vc_ref[...].astype(BF16))
    psum = p[0:tq]
    for g in range(1, NSA_G):
        psum = psum + p[g * tq:(g + 1) * tq]

    member_t = _member_by_rank(psum, _iota2((1, tq), 1) + t0, n_slc, min(N_SEL, n_slc)).astype(BF16)

    nb = member_t.shape[0]
    n_win = SEL_WINDOWS if T % (SEL_WINDOWS * tq) == 0 else 1
    for i in range(n_win):
        w_prev, w = T * i // n_win, T * (i + 1) // n_win

        @pl.when((qi >= w_prev // tq) & (qi < w // tq))
        def _(w=w):
            off = T - w
            col_blk = (jnp.right_shift(_iota2((nb, w), 1) + off, SEL_SHIFT)
                       + (qi * (tq // SEL_BLOCK) + (tq - T) // SEL_BLOCK))
            expand = (col_blk == _iota2((nb, w), 0)).astype(BF16)
            kpos = _iota2((1, w), 1) + (t0 + tq - w)
            allowed = (_dot_tn(member_t, expand) > 0.5) & (kpos <= tpos)
            mask_s = jnp.where(allowed, 0.0, NEG_INF)
            osel[...] = _banded_attention(q, ksp, vsp, t0 + off, w, mask_s, bias_near)

    dist = WINDOW + _iota2((tq, wlen), 0) - _iota2((tq, wlen), 1)
    in_win = (dist >= 0) & (dist < WINDOW) & (_iota2((1, wlen), 1) + (t0 - WINDOW) >= 0)
    o_win = _banded_attention(q, kwp, vwp, t0, wlen, jnp.where(in_win, 0.0, NEG_INF), bias_near)
    gate = jax.nn.sigmoid(gb_ref[...] + bg_ref[...])
    zb = _silu(zb_ref[...])
    for g in range(NSA_G):
        head = pl.program_id(1) * NSA_G + g
        r = slice(g * tq, (g + 1) * tq)
        mix = (_lane_col(gate, head) * o_cmp[r] + _lane_col(gate, NSA_HEADS + head) * osel[r, :]
               + _lane_col(gate, 2 * NSA_HEADS + head) * o_win[r])
        sl = slice(g * NSA_HD, (g + 1) * NSA_HD)
        o_ref[:, sl] = (mix * zb[:, sl]).astype(o_ref.dtype)


def _nsa_prompt_call(ya, yb, kv16, kcmp, vcmp, bg_r, bias_c, bias_near, *, B, T):
    nq = T // Q_BLOCK
    gw = NSA_G * NSA_HD
    kv_spec = pl.BlockSpec((T, NSA_HD), lambda b, h, i: (b, h))
    cmp_spec = pl.BlockSpec((None, None, T // CMP_STRIDE, NSA_HD), lambda b, h, i: (b, h, 0, 0))
    return pl.pallas_call(
        functools.partial(_nsa_prompt_body, T=T),
        grid=(B, NSA_KVH, nq),
        in_specs=[pl.BlockSpec((Q_BLOCK, gw), lambda b, h, i: (b * nq + i, EVEN_A["qb"] // gw + h)),
                  pl.BlockSpec((Q_BLOCK, gw), lambda b, h, i: (b * nq + i, EVEN_B["zb"] // gw + h)),
                  pl.BlockSpec((Q_BLOCK, LANES), lambda b, h, i: (b * nq + i, EVEN_B["gb"] // LANES)),
                  pl.BlockSpec((1, LANES), lambda b, h, i: (0, 0)),
                  kv_spec, kv_spec, kv_spec, kv_spec, cmp_spec, cmp_spec,
                  pl.BlockSpec((None, NSA_G, Q_BLOCK, T // CMP_STRIDE), lambda b, h, i: (h, 0, i, 0)),
                  pl.BlockSpec((None, NSA_G, Q_BLOCK, NEAR_COLS), lambda b, h, i: (h, 0, 0, 0))],
        out_specs=pl.BlockSpec((Q_BLOCK, gw), lambda b, h, i: (b * nq + i, h)),
        out_shape=jax.ShapeDtypeStruct((B * T, NSA_W), BF16),
        scratch_shapes=[pltpu.VMEM((2 * T - Q_BLOCK, NSA_HD), BF16), pltpu.VMEM((2 * T - Q_BLOCK, NSA_HD), BF16),
                        pltpu.VMEM((WINDOW + T, NSA_HD), BF16), pltpu.VMEM((WINDOW + T, NSA_HD), BF16),
                        pltpu.VMEM((NSA_G * Q_BLOCK, NSA_HD), F32)],
        compiler_params=_params(("arbitrary", "arbitrary", "arbitrary")),
        name="nsa_prompt",
    )(ya, yb, yb, bg_r, *kv16, kcmp, vcmp, bias_c, bias_near)


CMP_PAGES = 32
CHUNKS_PER_PAGE = PAGE_SIZE // CMP_STRIDE
PAGE_ROWS = PAGE_SIZE * NSA_KVH


def _pool_rows(pool):
    return pool.reshape(pool.shape[0] * PAGE_ROWS, NSA_HD)


def _cmp_pages_body(pt_ref, *refs):
    del pt_ref
    pages = refs[:CMP_PAGES]
    w_ref, pe_ref, o_ref = refs[CMP_PAGES:]
    rows = CMP_PAGES * CHUNKS_PER_PAGE
    per_head = [jnp.concatenate(
        [jnp.concatenate([pg[pl.ds(NSA_KVH * s + h, CHUNKS_PER_PAGE, stride=CMP_STRIDE * NSA_KVH), :]
                          for s in range(CMP_STRIDE)], axis=1) for pg in pages], axis=0) for h in range(NSA_KVH)]
    w = w_ref[...]
    r = _dot(jnp.concatenate(per_head, axis=0).astype(BF16), w)
    pc = _dot(pe_ref[...], w)
    r = r + jnp.concatenate([pc[0:1, :NSA_HD], pc[1:2, NSA_HD:]], axis=1)
    for h in range(NSA_KVH):
        o_ref[h] = r[h * rows:(h + 1) * rows]


def _cmp_pages_call(pool, page_table, w1, pe, *, B):
    n_pages = page_table.shape[1]
    rows = CMP_PAGES * CHUNKS_PER_PAGE
    view = _pool_rows(pool)
    w = w1.reshape(2, CMP_STRIDE, NSA_HD, NSA_HD).transpose(1, 2, 0, 3).reshape(CMP_STRIDE * NSA_HD, 2 * NSA_HD)
    pe_rows = jnp.pad(pe.reshape(2, CMP_STRIDE * NSA_HD), ((0, 6), (0, 0))).astype(BF16)

    def page_spec(i):
        return pl.BlockSpec((PAGE_ROWS, NSA_HD), lambda b, s, pt: (pt[b * n_pages + s * CMP_PAGES + i], 0))

    grid_spec = pltpu.PrefetchScalarGridSpec(
        num_scalar_prefetch=1,
        grid=(B, n_pages // CMP_PAGES),
        in_specs=[page_spec(i) for i in range(CMP_PAGES)]
        + [pl.BlockSpec((CMP_STRIDE * NSA_HD, 2 * NSA_HD), lambda b, s, pt: (0, 0)),
           pl.BlockSpec((8, CMP_STRIDE * NSA_HD), lambda b, s, pt: (0, 0))],
        out_specs=pl.BlockSpec((None, NSA_KVH, rows, 2 * NSA_HD), lambda b, s, pt: (b, 0, s, 0)),
    )
    return pl.pallas_call(
        _cmp_pages_body,
        grid_spec=grid_spec,
        out_shape=jax.ShapeDtypeStruct((B, NSA_KVH, n_pages * CHUNKS_PER_PAGE, 2 * NSA_HD), F32),
        compiler_params=_params(("arbitrary", "arbitrary")),
        name="nsa_cmp_pages",
    )(page_table.reshape(-1), *([view] * CMP_PAGES), w.astype(BF16), pe_rows)


SEL_WINDOWS = 4
SLC_LANES = 384


def _sample_q_rows(q_ref):
    q = q_ref[...] * (NSA_HD ** -0.5)
    return jnp.concatenate([q[:, g * NSA_HD:(g + 1) * NSA_HD] for g in range(NSA_G)], axis=0).astype(BF16)


def _nsa_sample_main_body(abk_ref, abv_ref, b1_ref, w2_ref, q_ref, wk_ref, wv_ref, kn_ref, vn_ref, bc_ref, bw_ref,
                          ocmp_ref, owin_ref, idx_ref, *, T, n_slc):
    tp = SAMPLE_PAD_T
    rows = NSA_G * tp
    ncmp = abk_ref.shape[0]

    def compressed(ab_ref, t):
        ab = ab_ref[...]
        h = ab[:, :NSA_HD] + pltpu.roll(ab[:, NSA_HD:], ncmp - 1, 0) + b1_ref[t]
        return _dot(_gelu_tanh(h).astype(BF16), w2_ref[t]).astype(BF16)

    kc, vc = compressed(abk_ref, 0), compressed(abv_ref, 1)
    q = _sample_q_rows(q_ref)
    step = jnp.bitwise_and(_iota2((rows, 1), 0), tp - 1)
    tpos = PAST_LEN + step
    vis = tpos >= _iota2((1, ncmp), 1) * CMP_STRIDE + (CMP_BLOCK - 1)
    p, l = _softmax_rows(jnp.where(vis, _dot_nt(q, kc) + bc_ref[...], NEG_INF))
    p = p / jnp.maximum(l, TINY)
    ocmp_ref[...] = _dot(p.astype(BF16), vc)
    psum = p[0:tp]
    for g in range(1, NSA_G):
        psum = psum + p[g * tp:(g + 1) * tp]
    cur = jnp.right_shift(PAST_LEN + _iota2((tp, 1), 0), SEL_SHIFT)
    _, picks = _top_blocks(_slc_scores(psum, SLC_LANES, n_slc), cur, N_SEL)
    idx_ref[...] = picks.astype(jnp.int32)

    wb = wk_ref.shape[0] // NSA_KVH
    wlen = bw_ref.shape[1]
    fill = jnp.zeros((wlen - wb - tp, NSA_HD), BF16)
    head = pl.program_id(1)
    k_all = jnp.concatenate([wk_ref[pl.ds(head, wb, stride=NSA_KVH), :].astype(BF16), kn_ref[...], fill], axis=0)
    v_all = jnp.concatenate([wv_ref[pl.ds(head, wb, stride=NSA_KVH), :].astype(BF16), vn_ref[...], fill], axis=0)
    col = _iota2((1, wlen), 1)
    dist = tpos - (PAST_LEN - wb + col)
    in_win = (dist >= 0) & (dist < WINDOW) & (col < wb + T)
    pw, lw = _softmax_rows(jnp.where(in_win, _dot_nt(q, k_all) + bw_ref[...], NEG_INF))
    owin_ref[...] = _dot(pw.astype(BF16), v_all) / jnp.maximum(lw, TINY)


def _nsa_sample_main_call(ya, kw16, vw16, abk, abv, b1, w2, wk, wv, bias_c, bias_w, *, B, T):
    tp = SAMPLE_PAD_T
    rows = NSA_G * tp
    gw = NSA_G * NSA_HD
    ncmp = abk.shape[2]
    wb = wk.shape[1]
    wlen = bias_w.shape[-1]
    n_slc = -(-(PAST_LEN + T) // SEL_BLOCK)
    assert n_slc <= SLC_LANES and T <= tp
    ab_spec = pl.BlockSpec((None, None, ncmp, 2 * NSA_HD), lambda b, h: (b, h, 0, 0))
    win_spec = pl.BlockSpec((wb * NSA_KVH, NSA_HD), lambda b, h: (b, 0))
    o_spec = pl.BlockSpec((None, None, rows, NSA_HD), lambda b, h: (b, h, 0, 0))
    return pl.pallas_call(
        functools.partial(_nsa_sample_main_body, T=T, n_slc=n_slc),
        grid=(B, NSA_KVH),
        in_specs=[ab_spec, ab_spec,
                  pl.BlockSpec((2, 1, NSA_HD), lambda b, h: (0, 0, 0)),
                  pl.BlockSpec((2, NSA_HD, NSA_HD), lambda b, h: (0, 0, 0)),
                  pl.BlockSpec((tp, gw), lambda b, h: (b, EVEN_A["qb"] // gw + h)),
                  win_spec, win_spec,
                  pl.BlockSpec((tp, NSA_HD), lambda b, h: (b, h)),
                  pl.BlockSpec((tp, NSA_HD), lambda b, h: (b, h)),
                  pl.BlockSpec((None, rows, ncmp), lambda b, h: (h, 0, 0)),
                  pl.BlockSpec((None, rows, wlen), lambda b, h: (h, 0, 0))],
        out_specs=[o_spec, o_spec, pl.BlockSpec((None, None, tp, LANES), lambda b, h: (b, h, 0, 0))],
        out_shape=[jax.ShapeDtypeStruct((B, NSA_KVH, rows, NSA_HD), F32),
                   jax.ShapeDtypeStruct((B, NSA_KVH, rows, NSA_HD), F32),
                   jax.ShapeDtypeStruct((B, NSA_KVH, tp, LANES), jnp.int32)],
        compiler_params=_params(("parallel", "parallel")),
        name="nsa_sample_main",
    )(abk, abv, b1.reshape(2, 1, NSA_HD), w2.astype(BF16), ya,
      wk.reshape(B * wb * NSA_KVH, NSA_HD), wv.reshape(B * wb * NSA_KVH, NSA_HD), kw16, vw16, bias_c, bias_w)


NEAR_BLOCKS = 3


def _nsa_sample_sel_body(idx_ref, pt_ref, q_ref, kn_ref, vn_ref, tbl_ref, ocmp_ref, owin_ref, gb_ref, bg_ref, zb_ref,
                         *refs, T):
    del pt_ref
    k_blocks = refs[:N_SEL]
    v_blocks = refs[N_SEL:2 * N_SEL]
    o_ref, osel = refs[2 * N_SEL:]
    tp = SAMPLE_PAD_T
    rows = NSA_G * tp
    b, h, t = pl.program_id(0), pl.program_id(1), pl.program_id(2)
    base = ((b * NSA_KVH + h) * T + t) * N_SEL
    first_new = PAST_LEN // SEL_BLOCK
    cur = jnp.right_shift(PAST_LEN + t, SEL_SHIFT)
    q = _sample_q_rows(q_ref)
    pad = jnp.zeros((SEL_BLOCK - tp, NSA_HD), BF16)
    k_new = jnp.concatenate([kn_ref[...], pad], axis=0)
    v_new = jnp.concatenate([vn_ref[...], pad], axis=0)
    lane = _iota2((1, LANES), 1)
    low = lane < SEL_BLOCK
    within = jnp.bitwise_and(lane, SEL_BLOCK - 1)
    ks, vs, bias, kpos = [], [], [], []
    for i in range(0, N_SEL, 2):
        pair_bias, pair_pos = [], []
        for j in (i, i + 1):
            blk = idx_ref[base + j]
            is_new = blk >= first_new
            ks.append(jnp.where(is_new, k_new, k_blocks[j][pl.ds(h, SEL_BLOCK, stride=NSA_KVH), :].astype(BF16)))
            vs.append(jnp.where(is_new, v_new, v_blocks[j][pl.ds(h, SEL_BLOCK, stride=NSA_KVH), :].astype(BF16)))
            pair_bias.append(tbl_ref[jnp.clip(blk - (first_new - NEAR_BLOCKS), 0, NEAR_BLOCKS)])
            pair_pos.append(jnp.where(blk <= cur, blk * SEL_BLOCK, PAST_LEN + SEL_BLOCK * LANES) + within)
        bias.append(jnp.where(low, pair_bias[0], pair_bias[1]))
        kpos.append(jnp.where(low, pair_pos[0], pair_pos[1]))
    k_all = jnp.concatenate(ks, axis=0)
    v_all = jnp.concatenate(vs, axis=0)
    step = jnp.bitwise_and(_iota2((rows, 1), 0), tp - 1)
    ok = jnp.concatenate(kpos, axis=1) <= PAST_LEN + step
    p, l = _softmax_rows(jnp.where(ok, _dot_nt(q, k_all) + jnp.concatenate(bias, axis=1), NEG_INF))
    o = _dot(p.astype(BF16), v_all) / jnp.maximum(l, TINY)

    @pl.when(t == 0)
    def _():
        osel[...] = jnp.zeros_like(osel)

    osel[...] = jnp.where(step == t, o, osel[...])

    @pl.when(t == T - 1)
    def _():
        gate = jax.nn.sigmoid(gb_ref[...] + bg_ref[...])
        zb = _silu(zb_ref[...])
        for g in range(NSA_G):
            r = slice(g * tp, (g + 1) * tp)
            head = h * NSA_G + g
            mix = (_lane_col(gate, head) * ocmp_ref[r, :] + _lane_col(gate, NSA_HEADS + head) * osel[r, :]
                   + _lane_col(gate, 2 * NSA_HEADS + head) * owin_ref[r, :])
            sl = slice(g * NSA_HD, (g + 1) * NSA_HD)
            o_ref[:, sl] = (mix * zb[:, sl]).astype(o_ref.dtype)


def _nsa_sample_sel_call(ya, yb, ks16, vs16, idx, page_table, pool_k, pool_v, tbl, o_cmp, o_win, bg_r, *, B, T):
    tp = SAMPLE_PAD_T
    rows = NSA_G * tp
    gw = NSA_G * NSA_HD
    n_pages = page_table.shape[1]
    halves = PAGE_SIZE // SEL_BLOCK
    idx_flat = idx[:, :, :T, :N_SEL].reshape(-1)
    view_k, view_v = _pool_rows(pool_k), _pool_rows(pool_v)

    def blk_spec(j):
        def index(b, h, t, idx_s, pt_s):
            blk = idx_s[((b * NSA_KVH + h) * T + t) * N_SEL + j]
            page = pt_s[b * n_pages + jnp.minimum(blk // halves, n_pages - 1)]
            return (page * halves + blk % halves, 0)
        return pl.BlockSpec((SEL_BLOCK * NSA_KVH, NSA_HD), index)

    o_spec = pl.BlockSpec((None, None, rows, NSA_HD), lambda b, h, t, *_: (b, h, 0, 0))
    grid_spec = pltpu.PrefetchScalarGridSpec(
        num_scalar_prefetch=2,
        grid=(B, NSA_KVH, T),
        in_specs=[pl.BlockSpec((tp, gw), lambda b, h, t, *_: (b, EVEN_A["qb"] // gw + h)),
                  pl.BlockSpec((tp, NSA_HD), lambda b, h, t, *_: (b, h)),
                  pl.BlockSpec((tp, NSA_HD), lambda b, h, t, *_: (b, h)),
                  pl.BlockSpec((None, NEAR_BLOCKS + 1, rows, LANES), lambda b, h, t, *_: (h, 0, 0, 0)),
                  o_spec, o_spec,
                  pl.BlockSpec((tp, LANES), lambda b, h, t, *_: (b, EVEN_B["gb"] // LANES)),
                  pl.BlockSpec((1, LANES), lambda b, h, t, *_: (0, 0)),
                  pl.BlockSpec((tp, gw), lambda b, h, t, *_: (b, EVEN_B["zb"] // gw + h))]
        + [blk_spec(j) for j in range(N_SEL)] * 2,
        out_specs=pl.BlockSpec((tp, gw), lambda b, h, t, *_: (b, h)),
        scratch_shapes=[pltpu.VMEM((rows, NSA_HD), F32)],
    )
    return pl.pallas_call(
        functools.partial(_nsa_sample_sel_body, T=T),
        grid_spec=grid_spec,
        out_shape=jax.ShapeDtypeStruct((B * tp, NSA_W), BF16),
        compiler_params=_params(("arbitrary", "arbitrary", "arbitrary")),
        name="nsa_sample_sel",
    )(idx_flat, page_table.reshape(-1), ya, ks16, vs16, tbl, o_cmp, o_win, yb, bg_r, yb,
      *([view_k] * N_SEL), *([view_v] * N_SEL))


def _sample_bias_tables(rel_bias, T, wb):
    tp = SAMPLE_PAD_T
    ncmp = PAST_LEN // CMP_STRIDE
    wlen = -(-(wb + tp) // LANES) * LANES
    first = PAST_LEN // SEL_BLOCK - NEAR_BLOCKS
    assert PAST_LEN - ((first + 1) * SEL_BLOCK - 1) >= REL_MAX_DIST
    lo, hi = -wlen, PAST_LEN + tp
    rev = _bias_line(rel_bias, lo, hi, descending=True)

    def rows(tbl):
        return tbl.reshape(NSA_KVH, NSA_G * tp, tbl.shape[-1])

    t_c = _toeplitz(rev, hi - 1 - (PAST_LEN - (CMP_BLOCK - 1)), tp, CMP_STRIDE * ncmp)[:, :, ::CMP_STRIDE]
    t_w = _toeplitz(rev, hi - 1 - wb, tp, wlen)
    far = jnp.broadcast_to(rev[:, hi - 1 - REL_MAX_DIST][:, None, None], (NSA_HEADS, tp, LANES))
    near = []
    for k in range(1, NEAR_BLOCKS + 1):
        half = _toeplitz(rev, hi - 1 - (PAST_LEN - (first + k) * SEL_BLOCK), tp, SEL_BLOCK)
        near.append(jnp.concatenate([half, half], axis=-1))
    t_s = jnp.stack([far] + near, axis=1).reshape(NSA_KVH, NSA_G, NEAR_BLOCKS + 1, tp, LANES)
    t_s = t_s.transpose(0, 2, 1, 3, 4).reshape(NSA_KVH, NEAR_BLOCKS + 1, NSA_G * tp, LANES)
    return rows(t_c), rows(t_w), t_s


def _tail_even(w):
    return _tail_relayout(w, EVEN_KV_OFF + 6 * NSA_KV_W, 3 * NSA_HEADS, NSA_W + MEM_W, EVEN_B_N)


def _tail_odd(w):
    return _tail_relayout(w, ODD_A_N, 2 * ML_HEADS, ML_V_W + MEM_W, ODD_B_N)


def _gate_bias_even(b_gate):
    return jnp.pad(b_gate, (0, LANES - 3 * NSA_HEADS)).reshape(1, LANES)


def _gate_bias_odd(b_if):
    return jnp.pad(b_if.reshape(2 * ML_HEADS), (0, LANES - 2 * ML_HEADS)).reshape(1, LANES)


def _rel_bucket(dist):
    n = np.maximum(dist, 0)
    exact = REL_BUCKETS // 2
    nf = np.maximum(n, 1).astype(np.float32)
    large = exact + (np.log(nf / exact) / math.log(REL_MAX_DIST / exact) * (REL_BUCKETS - exact)).astype(np.int32)
    return np.where(n < exact, n, np.minimum(large, REL_BUCKETS - 1))


def _bias_line(rel_bias, lo, hi, descending=False):
    dist = np.arange(hi - 1, lo - 1, -1) if descending else np.arange(lo, hi)
    buckets = _rel_bucket(dist)
    edges = np.flatnonzero(np.diff(buckets)) + 1
    starts = np.concatenate([[0], edges])
    ends = np.concatenate([edges, [hi - lo]])
    bias_t = rel_bias.T.astype(F32)
    runs = [jnp.broadcast_to(bias_t[:, int(buckets[s])][:, None], (NSA_HEADS, int(e - s))) for s, e in zip(starts, ends)]
    return jnp.concatenate(runs, axis=1)


def _skew_rows(v, rows, step, cols):
    n = v.shape[1]
    reps = -(-rows * (n + step) // n)
    return jnp.tile(v, (1, reps))[:, :rows * (n + step)].reshape(v.shape[0], rows, n + step)[:, :, :cols]


def _toeplitz(rev, start, rows, cols):
    seg = rev[:, start - (rows - 1):start + cols]
    return _skew_rows(jnp.roll(seg, -(rows - 1), axis=1), rows, -1, cols)


def _prompt_bias_tables(rel_bias, T):
    ncmp = T // CMP_STRIDE
    assert Q_BLOCK + 1 >= REL_MAX_DIST
    lo, hi = -(CMP_STRIDE * ncmp + CMP_BLOCK), T
    line = _bias_line(rel_bias, lo, hi)
    rev = _bias_line(rel_bias, lo, hi, descending=True)

    def split(tbl):
        return tbl.reshape((NSA_KVH, NSA_G) + tbl.shape[1:])

    back = CMP_STRIDE * (ncmp - 1)
    first = -(back + CMP_BLOCK - 1) - lo
    seg = line[:, first:first + T + back]
    t_c = _skew_rows(jnp.roll(seg, -back, axis=1), ncmp, -CMP_STRIDE, T).swapaxes(1, 2)
    far = rev[:, hi - 1 - REL_MAX_DIST]
    t_near = _toeplitz(rev, hi - 1 - REL_MAX_DIST, Q_BLOCK, NEAR_COLS) - far[:, None, None]
    return split(t_c), split(t_near)


def _nsa_sample(ya, yb, kv16, page_table, pk_cmp, pv_cmp, pk_sel, pv_sel, wk, wv, bg_r, w1, b1, w2, pe, rel_bias,
                *, B, T):
    assert (PAST_LEN + T) // CMP_STRIDE == PAST_LEN // CMP_STRIDE
    abk = _cmp_pages_call(pk_cmp, page_table, w1[0], pe[0], B=B)
    abv = _cmp_pages_call(pv_cmp, page_table, w1[1], pe[1], B=B)
    bias_c, bias_w, tbl = _sample_bias_tables(rel_bias, T, wk.shape[1])
    o_cmp, o_win, idx = _nsa_sample_main_call(ya, kv16[4], kv16[5], abk, abv, b1, w2, wk, wv, bias_c, bias_w, B=B, T=T)
    return _nsa_sample_sel_call(ya, yb, kv16[2], kv16[3], idx, page_table, pk_sel, pv_sel, tbl, o_cmp, o_win, bg_r,
                                B=B, T=T)


def _kv_project(x, wt):
    outs = [_matmul_heads(x, wt, first=EVEN_KV_OFF + j * NSA_KV_W, transposed=True) for j in range(6)]
    return [o[0] for o in outs], [o[1] for o in outs]


def _even_prompt(hp2d, npre, mk16, mv16, wt, wt_b, bg_r, w1, b1, w2, pe, lb, g_norm, w_out, rel_bias, *, B, T):
    ya, yb = _matmul_nt(npre, wt, tm=W_TILE_M, tn=W_TILE_N, rows=(0, EVEN_A_N)), _matmul_nt(npre, wt_b)
    kv32, kv16 = _kv_project(npre, wt)
    oa, s_new = _hgrn_call(ya, jnp.zeros((B, HG_HEADS, HG_DK, HG_DV), F32), lb, g_norm, B=B, T=T, L=CHUNK, valid=CHUNK)
    kcmp = _compress_call(kv16[0], w1[0], b1[0], w2[0], pe[0], B=B, T=T)
    vcmp = _compress_call(kv16[1], w1[1], b1[1], w2[1], pe[1], B=B, T=T)
    ob = _nsa_prompt_call(ya, yb, kv16[2:], kcmp, vcmp, bg_r, *_prompt_bias_tables(rel_bias, T), B=B, T=T)
    om = _mem_call(yb, EVEN_B["qm"], mk16, mv16, B=B, T=T)
    h_new = _outproj([oa, ob, om], w_out, hp2d)
    wb = min(WINDOW, T)
    rows = [r.reshape(B, T, NSA_KVH, NSA_HD) for r in kv32]
    return h_new, (rows[0], rows[1], rows[2], rows[3], rows[4][:, -wb:], rows[5][:, -wb:], s_new)


def _even_sample(hs2d, nsam, mk_s, mv_s, page_table, pk_cmp, pv_cmp, pk_sel, pv_sel, wk, wv, s0,
                 wt, wt_b, bg_r, w1, b1, w2, pe, lb, g_norm, w_out, rel_bias, *, B, T):
    tp = SAMPLE_PAD_T
    ya, yb = _matmul_nt(nsam, wt, tm=W_TILE_M, tn=W_TILE_N, rows=(0, EVEN_A_N)), _matmul_nt(nsam, wt_b)
    kv32, kv16 = _kv_project(nsam, wt)
    oa, s_new = _hgrn_call(ya, s0, lb, g_norm, B=B, T=tp, L=tp, valid=T)
    ob = _nsa_sample(ya, yb, kv16, page_table, pk_cmp, pv_cmp, pk_sel, pv_sel, wk, wv, bg_r, w1, b1, w2, pe, rel_bias,
                     B=B, T=T)
    om = _mem_call(yb, EVEN_B["qm"], mk_s.reshape(B * N_MEM, MEM_W), mv_s.reshape(B * N_MEM, MEM_W), B=B, T=tp)
    rows = [r.reshape(B, tp, NSA_KVH, NSA_HD)[:, :T] for r in kv32]
    wb = wk.shape[1]
    win_k = jnp.concatenate([wk, rows[4]], axis=1)[:, -wb:]
    win_v = jnp.concatenate([wv, rows[5]], axis=1)[:, -wb:]
    return _outproj([oa, ob, om], w_out, hs2d), (rows[0], rows[1], rows[2], rows[3], win_k, win_v, s_new)


def _odd_mix(h2d, hn, k2d, v2d, c0, n0, m0, wt, wt_b, bif_r, g_norm, w_out, *, B, T, L, valid):
    ya, yb = _matmul_nt(hn, wt, tm=W_TILE_M, tn=W_TILE_N, rows=(0, ODD_A_N)), _matmul_nt(hn, wt_b)
    h, c_new, n_new, m_new = _mlstm_call(ya, yb, c0, n0, m0, bif_r, g_norm, B=B, T=T, L=L, valid=valid)
    om = _mem_call(yb, ODD_B["qm"], k2d, v2d, B=B, T=T)
    return _outproj([h, om], w_out, h2d), (c_new, n_new, m_new)


def _stack(lst, i):
    return jnp.stack([t[i] for t in lst])


def kernel(x_prompt, x_sample, cache_mem_k, cache_mem_v, cache_cmp_k, cache_cmp_v, cache_sel_k, cache_sel_v,
           cache_win_k, cache_win_v, state_hgrn, state_mlstm_c, state_mlstm_n, state_mlstm_m, page_table,
           mem_prompt, norm_w, mem_norm_w, final_norm_w, rel_bias, w_mem_kv, w_in_even, b_nsa_gate,
           w_cmp1, b_cmp1, w_cmp2, pe_cmp, hgrn_lb_logits, hgrn_norm_w, w_out_even, w_in_odd, b_mlstm_if,
           mlstm_norm_w, w_out_odd):
    bp, tp = x_prompt.shape[:2]
    bs, ts = x_sample.shape[:2]
    tsp = SAMPLE_PAD_T
    lbs = jnp.cumsum(jax.nn.softmax(hgrn_lb_logits.astype(F32), axis=0), axis=0)
    hp = x_prompt.reshape(bp * tp, D_MODEL)
    hs = jnp.pad(x_sample, ((0, 0), (0, tsp - ts), (0, 0))).reshape(bs * tsp, D_MODEL)
    mem2d = mem_prompt.reshape(bp * N_MEM, D_MODEL)
    mem_new, even_p, even_s, odd_p, odd_s = [], [], [], [], []
    for l in range(DEPTH):
        npre = _rmsnorm_rows(hp, norm_w[l], BF16)
        nsam = _rmsnorm_rows(hs, norm_w[l], BF16)
        nmem = _rmsnorm_rows(mem2d, mem_norm_w[l], BF16)
        mk32, mk16 = _matmul_heads(nmem, w_mem_kv[l], first=0)
        mv32, mv16 = _matmul_heads(nmem, w_mem_kv[l], first=MEM_W)
        mem_new.append((mk32.reshape(bp, N_MEM, MEM_HEADS, MEM_HD), mv32.reshape(bp, N_MEM, MEM_HEADS, MEM_HD)))
        mk_s, mv_s = cache_mem_k[l], cache_mem_v[l]
        if l % 2 == 0:
            e = l // 2
            w_in = w_in_even[e].T
            w_b = _tail_even(w_in)
            w_out = w_out_even[e].astype(BF16)
            bg_r = _gate_bias_even(b_nsa_gate[e])
            cmpw = (w_cmp1[e].reshape(2, CMP_BLOCK, NSA_HD, NSA_HD), b_cmp1[e], w_cmp2[e], pe_cmp[e])
            hp, st_p = _even_prompt(hp, npre, mk16, mv16, w_in, w_b, bg_r, *cmpw, lbs[l], hgrn_norm_w[e], w_out,
                                    rel_bias, B=bp, T=tp)
            hs, st_s = _even_sample(hs, nsam, mk_s, mv_s, page_table, cache_cmp_k[e], cache_cmp_v[e], cache_sel_k[e],
                                    cache_sel_v[e], cache_win_k[e], cache_win_v[e], state_hgrn[e], w_in, w_b, bg_r,
                                    *cmpw, lbs[l], hgrn_norm_w[e], w_out, rel_bias, B=bs, T=ts)
            even_p.append(st_p)
            even_s.append(st_s)
        else:
            o = l // 2
            w_in = w_in_odd[o].T
            w_b = _tail_odd(w_in)
            w_out = w_out_odd[o].astype(BF16)
            bif_r = _gate_bias_odd(b_mlstm_if[o])
            hp, st_p = _odd_mix(hp, npre, mk16, mv16, jnp.zeros((bp, ML_HEADS, ML_DV, ML_DK), F32),
                                jnp.zeros((bp, ML_HEADS, ML_DK), F32), jnp.zeros((bp, ML_HEADS), F32),
                                w_in, w_b, bif_r, mlstm_norm_w[o], w_out, B=bp, T=tp, L=ML_CHUNK, valid=ML_CHUNK)
            hs, st_s = _odd_mix(hs, nsam, mk_s.reshape(bs * N_MEM, MEM_W), mv_s.reshape(bs * N_MEM, MEM_W),
                                state_mlstm_c[o], state_mlstm_n[o], state_mlstm_m[o],
                                w_in, w_b, bif_r, mlstm_norm_w[o], w_out, B=bs, T=tsp, L=tsp, valid=ts)
            odd_p.append(st_p)
            odd_s.append(st_s)
    y_prompt = _rmsnorm_rows(hp, final_norm_w, F32).reshape(bp, tp, D_MODEL)
    y_sample = _rmsnorm_rows(hs, final_norm_w, F32).reshape(bs, tsp, D_MODEL)[:, :ts]
    return (y_prompt, y_sample,
            _stack(mem_new, 0), _stack(mem_new, 1),
            _stack(even_p, 0), _stack(even_p, 1), _stack(even_p, 2), _stack(even_p, 3),
            _stack(even_p, 4), _stack(even_p, 5), _stack(even_p, 6),
            _stack(odd_p, 0), _stack(odd_p, 1), _stack(odd_p, 2),
            _stack(even_s, 0), _stack(even_s, 1), _stack(even_s, 2), _stack(even_s, 3),
            _stack(even_s, 4), _stack(even_s, 5), _stack(even_s, 6),
            _stack(odd_s, 0), _stack(odd_s, 1), _stack(odd_s, 2))
```

```python
import functools
import math

import jax
import jax.numpy as jnp
import numpy as np
from jax import lax
from jax.experimental import pallas as pl
from jax.experimental.pallas import tpu as pltpu

D_MODEL = 4096
DEPTH = 2
PAST_LEN = 16384
PAGE_SIZE = 128
N_MEM = 256
EPS = 1e-6
CHUNK = 64

HG_DK = 128
HG_DV = 128
HG_HEADS = D_MODEL // 2 // HG_DV
HG_W = HG_HEADS * HG_DV

NSA_HD = 128
NSA_HEADS = D_MODEL // 2 // NSA_HD
NSA_KVH = 4
NSA_G = NSA_HEADS // NSA_KVH
NSA_W = NSA_HEADS * NSA_HD
NSA_KV_W = NSA_KVH * NSA_HD
CMP_BLOCK = 32
CMP_STRIDE = 16
SEL_BLOCK = 64
SEL_SHIFT = SEL_BLOCK.bit_length() - 1
N_SEL = 16
WINDOW = 512
Q_BLOCK = 256

ML_HEADS = D_MODEL // 512
ML_DK = D_MODEL // 2 // ML_HEADS
ML_DV = D_MODEL // ML_HEADS
ML_QK_W = ML_HEADS * ML_DK
ML_V_W = ML_HEADS * ML_DV

MEM_HEADS = 4
MEM_HD = 128
MEM_W = MEM_HEADS * MEM_HD

REL_BUCKETS = 32
REL_MAX_DIST = 128

F32 = jnp.float32
BF16 = jnp.bfloat16
LANES = 128
NEG_INF = float("-inf")
TINY = float(np.finfo(np.float32).tiny)
EXP_CLAMP = 80.0
VMEM_LIMIT = 56 * 1024 * 1024

HG_HB = 16
ML_HB = 4
ML_CHUNK = 256
W_TILE_M, W_TILE_N = 1024, 512
HG_SUB = 16
SAMPLE_PAD_T = 16

MM_TILE_N = 1024
EVEN_A = {"qa": 0, "fa": HG_W, "ia": 2 * HG_W, "za": 3 * HG_W, "qb": 4 * HG_W}
EVEN_A_N = 4 * HG_W + NSA_W
EVEN_B = {"zb": 0, "qm": NSA_W, "gb": NSA_W + MEM_W}
EVEN_B_N = -(-(NSA_W + MEM_W + LANES) // MM_TILE_N) * MM_TILE_N
EVEN_KV_OFF = EVEN_A_N
ODD_A = {"q": 0, "k": ML_QK_W, "v": 2 * ML_QK_W, "og": 2 * ML_QK_W + ML_V_W}
ODD_A_N = 2 * ML_QK_W + 2 * ML_V_W
ODD_B = {"z": 0, "qm": ML_V_W, "gates": ML_V_W + MEM_W}
ODD_B_N = -(-(ML_V_W + MEM_W + LANES) // MM_TILE_N) * MM_TILE_N


def _dot(a, b):
    return jnp.dot(a, b, preferred_element_type=F32)


def _dot_nt(a, b):
    return lax.dot_general(a, b, (((1,), (1,)), ((), ())), preferred_element_type=F32)


def _dot_tn(a, b):
    return lax.dot_general(a, b, (((0,), (0,)), ((), ())), preferred_element_type=F32)


def _iota2(shape, dim):
    return lax.broadcasted_iota(jnp.int32, shape, dim)


def _cumsum_rows(x, tri_b):
    hi = x.astype(BF16)
    r1 = x - hi.astype(F32)
    mid = r1.astype(BF16)
    lo = (r1 - mid.astype(F32)).astype(BF16)
    return _dot(tri_b, hi) + _dot(tri_b, mid) + _dot(tri_b, lo)


def _row_to_col(row, n):
    eye = _iota2((n, n), 0) == _iota2((n, n), 1)
    return jnp.sum(jnp.where(eye, row, 0.0), axis=1, keepdims=True)


def _col_to_row(col, n):
    eye = _iota2((n, n), 0) == _iota2((n, n), 1)
    return jnp.sum(jnp.where(eye, col, 0.0), axis=0, keepdims=True)


def _lane_col(x, idx):
    return jnp.sum(jnp.where(_iota2(x.shape, 1) == idx, x, 0.0), axis=1, keepdims=True)


def _silu(x):
    return x * jax.nn.sigmoid(x)


def _params(sem):
    return pltpu.CompilerParams(dimension_semantics=sem, vmem_limit_bytes=VMEM_LIMIT)


def _rmsnorm_body(x_ref, w_ref, o_ref):
    x = x_ref[...].astype(F32)
    y = x * lax.rsqrt(jnp.mean(x * x, axis=-1, keepdims=True) + EPS)
    o_ref[...] = (y * w_ref[...].astype(F32)).astype(o_ref.dtype)


def _rmsnorm_rows(x2d, w, out_dtype, tm=512):
    m, d = x2d.shape
    tm = min(tm, m)
    return pl.pallas_call(
        _rmsnorm_body,
        grid=(m // tm,),
        in_specs=[pl.BlockSpec((tm, d), lambda i: (i, 0)), pl.BlockSpec((1, d), lambda i: (0, 0))],
        out_specs=pl.BlockSpec((tm, d), lambda i: (i, 0)),
        out_shape=jax.ShapeDtypeStruct((m, d), out_dtype),
        compiler_params=_params(("parallel",)),
        name="rmsnorm",
    )(x2d, w.reshape(1, d))


def _matmul_nt_body(a_ref, bt_ref, o_ref):
    o_ref[...] = _dot_nt(a_ref[...], bt_ref[...].astype(BF16))


def _matmul_nt(a, bt, tm=1024, tn=MM_TILE_N, rows=None):
    m, k = a.shape
    first, n = rows or (0, bt.shape[0])
    tm, tn = min(tm, m), min(tn, n)
    assert m % tm == 0 and n % tn == 0 and first % tn == 0, (a.shape, bt.shape, rows)
    j0 = first // tn
    return pl.pallas_call(
        _matmul_nt_body,
        grid=(m // tm, n // tn),
        in_specs=[pl.BlockSpec((tm, k), lambda i, j: (i, 0)), pl.BlockSpec((tn, k), lambda i, j: (j0 + j, 0))],
        out_specs=pl.BlockSpec((tm, tn), lambda i, j: (i, j)),
        out_shape=jax.ShapeDtypeStruct((m, n), F32),
        compiler_params=_params(("parallel", "parallel")),
        name="matmul",
    )(a, bt)


def _tail_body(lo_ref, hi_ref, gate_ref, o_ref, *, shift, n_main):
    i = pl.program_id(0)
    main = jnp.concatenate([lo_ref[shift:, :], hi_ref[:shift, :]], axis=0)
    gates = jnp.where(_iota2((LANES, 1), 0) < shift, gate_ref[...], 0.0)
    o_ref[...] = jnp.where(i < n_main, main, jnp.where(i == n_main, gates, 0.0)).astype(o_ref.dtype)


def _tail_relayout(wt, first, shift, main, out_rows):
    n, k = wt.shape
    assert first % LANES == 0 and main % LANES == 0 and out_rows % LANES == 0 and shift % 8 == 0 and shift < LANES
    assert first + shift + main == n
    c0, n_main = first // LANES, main // LANES
    return pl.pallas_call(
        functools.partial(_tail_body, shift=shift, n_main=n_main),
        grid=(out_rows // LANES,),
        in_specs=[pl.BlockSpec((LANES, k), lambda i: (c0 + jnp.minimum(i, n_main - 1), 0)),
                  pl.BlockSpec((LANES, k), lambda i: (c0 + jnp.minimum(i, n_main - 1) + 1, 0)),
                  pl.BlockSpec((LANES, k), lambda i: (c0, 0))],
        out_specs=pl.BlockSpec((LANES, k), lambda i: (i, 0)),
        out_shape=jax.ShapeDtypeStruct((out_rows, k), BF16),
        compiler_params=_params(("parallel",)),
        name="tail_relayout",
    )(wt, wt, wt)


def _matmul_heads_body(a_ref, b_ref, o32_ref, o16_ref, *, transposed):
    b = b_ref[...].astype(BF16)
    acc = _dot_nt(a_ref[...], b) if transposed else _dot(a_ref[...], b)
    for h in range(MEM_HEADS):
        o32_ref[:, h, :] = acc[:, h * LANES:(h + 1) * LANES]
    o16_ref[...] = acc.astype(BF16)


def _matmul_heads(a, b, first=0, transposed=False, tm=1024):
    m, k = a.shape
    n = MEM_HEADS * LANES
    tm = min(tm, m)
    assert m % tm == 0 and first % n == 0, (a.shape, b.shape, first)
    j0 = first // n
    b_spec = pl.BlockSpec((n, k), lambda i: (j0, 0)) if transposed else pl.BlockSpec((k, n), lambda i: (0, j0))
    return pl.pallas_call(
        functools.partial(_matmul_heads_body, transposed=transposed),
        grid=(m // tm,),
        in_specs=[pl.BlockSpec((tm, k), lambda i: (i, 0)), b_spec],
        out_specs=[pl.BlockSpec((tm, MEM_HEADS, LANES), lambda i: (i, 0, 0)), pl.BlockSpec((tm, n), lambda i: (i, 0))],
        out_shape=[jax.ShapeDtypeStruct((m, MEM_HEADS, LANES), F32), jax.ShapeDtypeStruct((m, n), BF16)],
        compiler_params=_params(("parallel",)),
        name="matmul_heads",
    )(a, b)


def _outproj_body(*refs, widths):
    xs = refs[:len(widths)]
    w_ref, r_ref, o_ref = refs[len(widths):]
    acc = r_ref[...]
    off = 0
    for x_ref, w in zip(xs, widths):
        acc = acc + _dot(x_ref[...], w_ref[off:off + w, :])
        off += w
    o_ref[...] = acc


def _outproj(xs, w_bf16, resid, tm=1024, tn=512):
    m = resid.shape[0]
    n = w_bf16.shape[1]
    widths = tuple(x.shape[1] for x in xs)
    assert sum(widths) == w_bf16.shape[0]
    tm = min(tm, m)
    in_specs = [pl.BlockSpec((tm, w), lambda i, j: (i, 0)) for w in widths]
    in_specs += [pl.BlockSpec((w_bf16.shape[0], tn), lambda i, j: (0, j)), pl.BlockSpec((tm, tn), lambda i, j: (i, j))]
    return pl.pallas_call(
        functools.partial(_outproj_body, widths=widths),
        grid=(m // tm, n // tn),
        in_specs=in_specs,
        out_specs=pl.BlockSpec((tm, tn), lambda i, j: (i, j)),
        out_shape=jax.ShapeDtypeStruct((m, n), F32),
        compiler_params=_params(("parallel", "parallel")),
        name="outproj",
    )(*xs, w_bf16, resid)


def _hgrn_body(qa_ref, fa_ref, ia_ref, za_ref, lb_ref, gn_ref, s0_ref, o_ref, s_out, s_scr, *, L, valid):
    c = pl.program_id(2)

    @pl.when(c == 0)
    def _():
        s_scr[...] = s0_ref[...]

    lb = lb_ref[...]
    sig = jax.nn.sigmoid(fa_ref[...])
    logf = jnp.log(lb + (1.0 - lb) * sig)
    kk = (1.0 - lb) * (1.0 - sig)
    if valid < L:
        live = _iota2((L, 1), 0) < valid
        logf = jnp.where(live, logf, 0.0)
        kk = jnp.where(live, kk, 0.0)
    tri_b = (_iota2((L, L), 0) >= _iota2((L, L), 1)).astype(BF16)
    bc = _cumsum_rows(logf, tri_b)
    q = _silu(qa_ref[...])
    gate = _silu(za_ref[...])
    v = ia_ref[...]
    gn = gn_ref[...]
    nsub = L // HG_SUB
    rr = _iota2((L, nsub * L), 0)
    cc = _iota2((L, nsub * L), 1)
    keep = ((jnp.right_shift(cc, L.bit_length() - 1) == jnp.right_shift(rr, HG_SUB.bit_length() - 1))
            & (jnp.bitwise_and(cc, L - 1) <= rr))
    for j in range(HG_HB):
        sl = slice(j * HG_DK, (j + 1) * HG_DK)
        bj, qj, kj = bc[:, sl], q[:, sl], kk[:, sl]
        vb = v[:, sl].astype(BF16)
        s_prev = s_scr[j]
        mids = [bj[i * HG_SUB + HG_SUB // 2:i * HG_SUB + HG_SUB // 2 + 1, :] for i in range(nsub)]
        mid_rows = jnp.concatenate([jnp.broadcast_to(m, (HG_SUB, HG_DK)) for m in mids], axis=0)
        q_dec = qj * jnp.exp(jnp.minimum(bj - mid_rows, EXP_CLAMP))
        k_dec = jnp.concatenate([kj * jnp.exp(jnp.minimum(m - bj, EXP_CLAMP)) for m in mids], axis=0)
        att = jnp.where(keep, _dot_nt(q_dec.astype(BF16), k_dec.astype(BF16)), 0.0)
        q_state = (qj * jnp.exp(bj)).astype(BF16)
        v_rep = jnp.concatenate([vb] * nsub, axis=0)
        if (nsub * L) % LANES == 0:
            o = _dot(jnp.concatenate([q_state, att.astype(BF16)], axis=1),
                     jnp.concatenate([s_prev.astype(BF16), v_rep], axis=0))
        else:
            o = _dot(q_state, s_prev.astype(BF16)) + _dot(att.astype(BF16), v_rep)
        o_n = o * lax.rsqrt(jnp.mean(o * o, axis=-1, keepdims=True) + EPS) * gn
        o_ref[:, sl] = (o_n * gate[:, sl]).astype(o_ref.dtype)
        bl = bj[L - 1:L, :]
        kd = kj * jnp.exp(bl - bj)
        s_scr[j] = _row_to_col(jnp.exp(bl), HG_DK) * s_prev + _dot_tn(kd.astype(BF16), vb)

    @pl.when(c == pl.num_programs(2) - 1)
    def _():
        s_out[...] = s_scr[...]


def _hgrn_call(y, s0, lb, gn, *, B, T, L, valid):
    nc = T // L
    w = HG_HB * HG_DK

    def col(name):
        blk = EVEN_A[name] // w
        return pl.BlockSpec((L, w), lambda b, hg, c: (b * nc + c, blk + hg))

    state_spec = pl.BlockSpec((None, HG_HB, HG_DK, HG_DV), lambda b, hg, c: (b, hg, 0, 0))
    return pl.pallas_call(
        functools.partial(_hgrn_body, L=L, valid=valid),
        grid=(B, HG_HEADS // HG_HB, nc),
        in_specs=[col("qa"), col("fa"), col("ia"), col("za"),
                  pl.BlockSpec((1, w), lambda b, hg, c: (0, hg)),
                  pl.BlockSpec((1, HG_DV), lambda b, hg, c: (0, 0)),
                  state_spec],
        out_specs=[pl.BlockSpec((L, w), lambda b, hg, c: (b * nc + c, hg)), state_spec],
        out_shape=[jax.ShapeDtypeStruct((B * T, HG_W), BF16),
                   jax.ShapeDtypeStruct((B, HG_HEADS, HG_DK, HG_DV), F32)],
        scratch_shapes=[pltpu.VMEM((HG_HB, HG_DK, HG_DV), F32)],
        compiler_params=_params(("arbitrary", "arbitrary", "arbitrary")),
        name="hgrn2",
    )(y, y, y, y, lb.reshape(1, HG_W), gn.reshape(1, HG_DV), s0)


def _mlstm_body(q_ref, k_ref, v_ref, og_ref, z_ref, g_ref, bif_ref, gn_ref, c0_ref, n0_ref, m0_ref,
                h_ref, c_out, n_out, m_out, c_scr, n_scr, m_scr, *, L, valid):
    c = pl.program_id(2)

    @pl.when(c == 0)
    def _():
        c_scr[...] = c0_ref[...]
        n_scr[...] = n0_ref[...]
        m_scr[...] = m0_ref[...]

    gates = g_ref[...] + bif_ref[...]
    log_i = gates
    log_f = jnp.minimum(gates, 0.0) - jnp.log(1.0 + jnp.exp(-jnp.abs(gates)))
    if valid < L:
        live = _iota2((L, 1), 0) < valid
        log_i = jnp.where(live, log_i, -1e30)
        log_f = jnp.where(live, log_f, 0.0)
    tri = _iota2((L, L), 0) >= _iota2((L, L), 1)
    bcs = _cumsum_rows(log_f, tri.astype(BF16))
    for j in range(ML_HB):
        head = pl.program_id(1) * ML_HB + j
        b_col = _lane_col(bcs, ML_HEADS + head)
        i_col = _lane_col(log_i, head)
        b_row = _col_to_row(b_col, L)
        i_row = _col_to_row(i_col, L)
        m_prev = m_scr[:, j:j + 1]
        dmat = jnp.where(tri, b_col - b_row + i_row, NEG_INF)
        inter = b_col + m_prev
        mt = jnp.maximum(inter, jnp.max(dmat, axis=1, keepdims=True))
        w_in = jnp.exp(dmat - mt)
        w_x = jnp.exp(inter - mt)
        qj = q_ref[:, j * ML_DK:(j + 1) * ML_DK]
        kj = k_ref[:, j * ML_DK:(j + 1) * ML_DK] * (ML_DK ** -0.5)
        vj = v_ref[:, j * ML_DV:(j + 1) * ML_DV]
        qb, kb = qj.astype(BF16), kj.astype(BF16)
        sw = _dot_nt(qb, kb) * w_in
        c_prev = c_scr[j]
        n_prev = n_scr[:, j * ML_DK:(j + 1) * ML_DK]
        num = w_x * _dot_nt(qb, c_prev.astype(BF16)) + _dot(sw.astype(BF16), vj.astype(BF16))
        den = w_x * jnp.sum(qj * n_prev, axis=1, keepdims=True) + jnp.sum(sw, axis=1, keepdims=True)
        h = num / jnp.maximum(jnp.abs(den), jnp.exp(-mt))
        m_last = mt[L - 1:L, :]
        b_last = b_col[L - 1:L, :]
        w_end = jnp.exp(b_last - b_col + i_col - m_last)
        d_c = jnp.exp(b_last + m_prev - m_last)
        c_scr[j] = d_c * c_prev + _dot_tn((w_end * vj).astype(BF16), kb)
        n_scr[:, j * ML_DK:(j + 1) * ML_DK] = d_c * n_prev + jnp.sum(w_end * kj, axis=0, keepdims=True)
        m_scr[:, j:j + 1] = m_last
        sv = slice(j * ML_DV, (j + 1) * ML_DV)
        h_n = h * lax.rsqrt(jnp.mean(h * h, axis=-1, keepdims=True) + EPS) * gn_ref[:, sv]
        h_ref[:, sv] = (h_n * jax.nn.sigmoid(og_ref[:, sv]) * _silu(z_ref[:, sv])).astype(h_ref.dtype)

    @pl.when(c == pl.num_programs(2) - 1)
    def _():
        c_out[...] = c_scr[...]
        n_out[...] = n_scr[...]
        m_out[...] = m_scr[...]


def _mlstm_call(ya, yb, c0, n0, m0, bif_r, gn, *, B, T, L, valid):
    nc = T // L
    ng = ML_HEADS // ML_HB
    wk, wv = ML_HB * ML_DK, ML_HB * ML_DV

    def col(name, w):
        blk = (ODD_A[name] if name in ODD_A else ODD_B[name]) // w
        return pl.BlockSpec((L, w), lambda b, hg, c: (b * nc + c, blk + hg))

    c_spec = pl.BlockSpec((None, ML_HB, ML_DV, ML_DK), lambda b, hg, c: (b, hg, 0, 0))
    n_spec = pl.BlockSpec((None, 1, wk), lambda b, hg, c: (b, 0, hg))
    m_spec = pl.BlockSpec((None, None, 1, LANES), lambda b, hg, c: (b, hg, 0, 0))
    m0_r = jnp.pad(m0.reshape(B, ng, 1, ML_HB), ((0, 0), (0, 0), (0, 0), (0, LANES - ML_HB)))
    h, c_new, n_new, m_new = pl.pallas_call(
        functools.partial(_mlstm_body, L=L, valid=valid),
        grid=(B, ng, nc),
        in_specs=[col("q", wk), col("k", wk), col("v", wv), col("og", wv), col("z", wv),
                  pl.BlockSpec((L, LANES), lambda b, hg, c: (b * nc + c, ODD_B["gates"] // LANES)),
                  pl.BlockSpec((1, LANES), lambda b, hg, c: (0, 0)),
                  pl.BlockSpec((1, wv), lambda b, hg, c: (0, hg)),
                  c_spec, n_spec, m_spec],
        out_specs=[pl.BlockSpec((L, wv), lambda b, hg, c: (b * nc + c, hg)), c_spec, n_spec, m_spec],
        out_shape=[jax.ShapeDtypeStruct((B * T, ML_V_W), BF16),
                   jax.ShapeDtypeStruct((B, ML_HEADS, ML_DV, ML_DK), F32),
                   jax.ShapeDtypeStruct((B, 1, ML_QK_W), F32),
                   jax.ShapeDtypeStruct((B, ng, 1, LANES), F32)],
        scratch_shapes=[pltpu.VMEM((ML_HB, ML_DV, ML_DK), F32), pltpu.VMEM((1, wk), F32), pltpu.VMEM((1, LANES), F32)],
        compiler_params=_params(("arbitrary", "arbitrary", "arbitrary")),
        name="mlstm",
    )(ya, ya, ya, ya, yb, yb, bif_r, gn.reshape(1, ML_V_W), c0, n0.reshape(B, 1, ML_QK_W), m0_r)
    return h, c_new, n_new.reshape(B, ML_HEADS, ML_DK), m_new[:, :, 0, :ML_HB].reshape(B, ML_HEADS)


def _mem_body(q_ref, k_ref, v_ref, o_ref):
    q = q_ref[...] * (MEM_HD ** -0.5)
    for h in range(MEM_HEADS):
        sl = slice(h * MEM_HD, (h + 1) * MEM_HD)
        s = _dot_nt(q[:, sl].astype(BF16), k_ref[:, sl].astype(BF16))
        p = jnp.exp(s - jnp.max(s, axis=-1, keepdims=True))
        o = _dot(p.astype(BF16), v_ref[:, sl].astype(BF16)) / jnp.sum(p, axis=-1, keepdims=True)
        o_ref[:, sl] = o.astype(o_ref.dtype)


def _mem_call(y, q_off, k2d, v2d, *, B, T, tq=512):
    tq = min(tq, T)
    nq = T // tq
    qb = q_off // MEM_W
    return pl.pallas_call(
        _mem_body,
        grid=(B, nq),
        in_specs=[pl.BlockSpec((tq, MEM_W), lambda b, i: (b * nq + i, qb)),
                  pl.BlockSpec((N_MEM, MEM_W), lambda b, i: (b, 0)),
                  pl.BlockSpec((N_MEM, MEM_W), lambda b, i: (b, 0))],
        out_specs=pl.BlockSpec((tq, MEM_W), lambda b, i: (b * nq + i, 0)),
        out_shape=jax.ShapeDtypeStruct((B * T, MEM_W), BF16),
        compiler_params=_params(("parallel", "parallel")),
        name="mem_attn",
    )(y, k2d, v2d)


def _gelu_tanh(x):
    return 0.5 * x * (1.0 + jnp.tanh(math.sqrt(2.0 / math.pi) * (x + 0.044715 * (x * x * x))))


def _compress_body(x_ref, w1_ref, b1_ref, w2_ref, pe_ref, o_ref, x32, *, nch):
    x32[...] = x_ref[...].astype(F32)
    a = jnp.zeros((nch, NSA_HD), F32)
    b = jnp.zeros((nch, NSA_HD), F32)
    for s in range(CMP_STRIDE):
        r = x32[pl.ds(s, nch, stride=CMP_STRIDE), :]
        a = a + _dot((r + pe_ref[s:s + 1, :]).astype(BF16), w1_ref[s])
        b = b + _dot((r + pe_ref[CMP_STRIDE + s:CMP_STRIDE + s + 1, :]).astype(BF16), w1_ref[CMP_STRIDE + s])
    h = a + pltpu.roll(b, nch - 1, 0) + b1_ref[...]
    o_ref[...] = _dot(_gelu_tanh(h).astype(BF16), w2_ref[...])


def _compress_call(x16, w1, b1, w2, pe, *, B, T):
    nch = T // CMP_STRIDE
    return pl.pallas_call(
        functools.partial(_compress_body, nch=nch),
        grid=(B, NSA_KVH),
        in_specs=[pl.BlockSpec((T, NSA_HD), lambda b, h: (b, h)),
                  pl.BlockSpec((CMP_BLOCK, NSA_HD, NSA_HD), lambda b, h: (0, 0, 0)),
                  pl.BlockSpec((1, NSA_HD), lambda b, h: (0, 0)),
                  pl.BlockSpec((NSA_HD, NSA_HD), lambda b, h: (0, 0)),
                  pl.BlockSpec((CMP_BLOCK, NSA_HD), lambda b, h: (0, 0))],
        out_specs=pl.BlockSpec((None, None, nch, NSA_HD), lambda b, h: (b, h, 0, 0)),
        out_shape=jax.ShapeDtypeStruct((B, NSA_KVH, nch, NSA_HD), F32),
        scratch_shapes=[pltpu.VMEM((T, NSA_HD), F32)],
        compiler_params=_params(("parallel", "parallel")),
        name="nsa_compress",
    )(x16, w1.astype(BF16), b1.reshape(1, NSA_HD), w2.astype(BF16), pe)


def _softmax_rows(s):
    m = jnp.max(s, axis=-1, keepdims=True)
    m = jnp.where(m == NEG_INF, 0.0, m)
    p = jnp.exp(s - m)
    return p, jnp.sum(p, axis=-1, keepdims=True)


def _slc_scores(psum, width, n_slc):
    ncmp = psum.shape[1]
    d = _iota2((ncmp, width), 0) - (SEL_BLOCK // CMP_STRIDE) * _iota2((ncmp, width), 1)
    wgt = jnp.where((d == -1) | (d == 3), 1.0, jnp.where((d >= 0) & (d <= 2), 2.0, 0.0))
    wgt = jnp.where(_iota2((ncmp, width), 1) < n_slc, wgt, 0.0).astype(BF16)
    p_hi = psum.astype(BF16)
    p_lo = (psum - p_hi.astype(F32)).astype(BF16)
    return _dot(p_hi, wgt) + _dot(p_lo, wgt)


def _top_blocks(slc, cur, n_pick):
    rows, width = slc.shape
    blk = _iota2((rows, width), 1)
    forced = (blk == 0) | (blk == cur) | (blk == cur - 1)
    score = jnp.where(forced, jnp.inf, slc)
    score = jnp.where(blk > cur, NEG_INF, score)
    blk_f = blk.astype(F32)
    lane = _iota2((rows, LANES), 1)
    sel = jnp.zeros((rows, width), F32)
    picks = jnp.zeros((rows, LANES), F32)
    for i in range(n_pick):
        mx = jnp.max(score, axis=-1, keepdims=True)
        first = jnp.min(jnp.where(score == mx, blk_f, float(width)), axis=-1, keepdims=True)
        pick = blk_f == first
        sel = jnp.where(pick, 1.0, sel)
        picks = jnp.where(lane == i, first, picks)
        score = jnp.where(pick, NEG_INF, score)
    return sel, picks


def _member_by_rank(psum, tpos_row, n_slc, n_pick):
    nq, ncmp = psum.shape
    nb = -(-n_slc // 8) * 8
    d = _iota2((nb, ncmp), 1) - (SEL_BLOCK // CMP_STRIDE) * _iota2((nb, ncmp), 0)
    wgt = jnp.where((d == -1) | (d == 3), 1.0, jnp.where((d >= 0) & (d <= 2), 2.0, 0.0))
    wgt = jnp.where(_iota2((nb, ncmp), 0) < n_slc, wgt, 0.0).astype(BF16)
    p_hi = psum.astype(BF16)
    p_lo = (psum - p_hi.astype(F32)).astype(BF16)
    slc = _dot_nt(wgt, p_hi) + _dot_nt(wgt, p_lo)
    blk = _iota2((nb, nq), 0)
    cur = jnp.right_shift(tpos_row, SEL_SHIFT)
    forced = (blk == 0) | (blk == cur) | (blk == cur - 1)
    score = jnp.where(forced, jnp.inf, slc)
    score = jnp.where(blk > cur, NEG_INF, score)
    ahead = jnp.zeros((nb, nq), F32)
    for i in range(n_slc):
        s_i = score[i:i + 1, :]
        ahead = ahead + jnp.where((s_i > score) | ((s_i == score) & (blk > i)), 1.0, 0.0)
    return jnp.where((ahead < n_pick) & (blk <= cur), 1.0, 0.0)


NEAR_COLS = Q_BLOCK + REL_MAX_DIST


def _add_per_head(s, mask):
    return (s.reshape(NSA_G, mask.shape[0], mask.shape[1]) + mask[None]).reshape(s.shape)


def _banded_attention(q, k_ref, v_ref, start, width, mask, near_bias):
    far = width - NEAR_COLS
    s_far = _add_per_head(_dot_nt(q, k_ref[pl.ds(start, far), :]), mask[:, :far])
    s_near = _add_per_head(_dot_nt(q, k_ref[pl.ds(start + far, NEAR_COLS), :]) + near_bias, mask[:, far:])
    m = jnp.maximum(jnp.max(s_far, axis=-1, keepdims=True), jnp.max(s_near, axis=-1, keepdims=True))
    m = jnp.where(m == NEG_INF, 0.0, m)
    p_far, p_near = jnp.exp(s_far - m), jnp.exp(s_near - m)
    l = jnp.sum(p_far, axis=-1, keepdims=True) + jnp.sum(p_near, axis=-1, keepdims=True)
    o = (_dot(p_far.astype(BF16), v_ref[pl.ds(start, far), :])
         + _dot(p_near.astype(BF16), v_ref[pl.ds(start + far, NEAR_COLS), :]))
    return o / jnp.maximum(l, TINY)


def _nsa_prompt_body(q_ref, zb_ref, gb_ref, bg_ref, ks_ref, vs_ref, kw_ref, vw_ref, kc_ref, vc_ref,
                     bc_ref, bn_ref, o_ref, ksp, vsp, kwp, vwp, osel, *, T):
    qi = pl.program_id(2)
    tq = Q_BLOCK
    front = T - tq
    wlen = WINDOW + tq
    n_slc = T // SEL_BLOCK

    @pl.when(qi == 0)
    def _():
        ksp[0:front, :] = jnp.zeros((front, NSA_HD), BF16)
        vsp[0:front, :] = jnp.zeros((front, NSA_HD), BF16)
        ksp[front:front + T, :] = ks_ref[...].astype(BF16)
        vsp[front:front + T, :] = vs_ref[...].astype(BF16)
        kwp[0:WINDOW, :] = jnp.zeros((WINDOW, NSA_HD), BF16)
        vwp[0:WINDOW, :] = jnp.zeros((WINDOW, NSA_HD), BF16)
        kwp[WINDOW:WINDOW + T, :] = kw_ref[...].astype(BF16)
        vwp[WINDOW:WINDOW + T, :] = vw_ref[...].astype(BF16)

    t0 = pl.multiple_of(qi * tq, tq)
    tpos = _iota2((tq, 1), 0) + t0
    q_all = q_ref[...] * (NSA_HD ** -0.5)
    q = jnp.concatenate([q_all[:, g * NSA_HD:(g + 1) * NSA_HD] for g in range(NSA_G)], axis=0).astype(BF16)
    bias_near = bn_ref[...].reshape(NSA_G * tq, NEAR_COLS)

    ncmp = T // CMP_STRIDE
    vis = tpos >= _iota2((1, ncmp), 1) * CMP_STRIDE + (CMP_BLOCK - 1)
    s = _dot_nt(q, kc_ref[...].astype(BF16)) + bc_ref[...].reshape(NSA_G * tq, ncmp)
    p, l = _softmax_rows(_add_per_head(s, jnp.where(vis, 0.0, NEG_INF)))
    p = p / jnp.maximum(l, TINY)
    o_cmp = _dot(p.astype(BF16), vc_ref[...].astype(BF16))
    psum = p[0:tq]
    for g in range(1, NSA_G):
        psum = psum + p[g * tq:(g + 1) * tq]

    member_t = _member_by_rank(psum, _iota2((1, tq), 1) + t0, n_slc, min(N_SEL, n_slc)).astype(BF16)

    nb = member_t.shape[0]
    n_win = SEL_WINDOWS if T % (SEL_WINDOWS * tq) == 0 else 1
    for i in range(n_win):
        w_prev, w = T * i // n_win, T * (i + 1) // n_win

        @pl.when((qi >= w_prev // tq) & (qi < w // tq))
        def _(w=w):
            off = T - w
            col_blk = (jnp.right_shift(_iota2((nb, w), 1) + off, SEL_SHIFT)
                       + (qi * (tq // SEL_BLOCK) + (tq - T) // SEL_BLOCK))
            expand = (col_blk == _iota2((nb, w), 0)).astype(BF16)
            kpos = _iota2((1, w), 1) + (t0 + tq - w)
            allowed = (_dot_tn(member_t, expand) > 0.5) & (kpos <= tpos)
            mask_s = jnp.where(allowed, 0.0, NEG_INF)
            osel[...] = _banded_attention(q, ksp, vsp, t0 + off, w, mask_s, bias_near)

    dist = WINDOW + _iota2((tq, wlen), 0) - _iota2((tq, wlen), 1)
    in_win = (dist >= 0) & (dist < WINDOW) & (_iota2((1, wlen), 1) + (t0 - WINDOW) >= 0)
    o_win = _banded_attention(q, kwp, vwp, t0, wlen, jnp.where(in_win, 0.0, NEG_INF), bias_near)
    gate = jax.nn.sigmoid(gb_ref[...] + bg_ref[...])
    zb = _silu(zb_ref[...])
    for g in range(NSA_G):
        head = pl.program_id(1) * NSA_G + g
        r = slice(g * tq, (g + 1) * tq)
        mix = (_lane_col(gate, head) * o_cmp[r] + _lane_col(gate, NSA_HEADS + head) * osel[r, :]
               + _lane_col(gate, 2 * NSA_HEADS + head) * o_win[r])
        sl = slice(g * NSA_HD, (g + 1) * NSA_HD)
        o_ref[:, sl] = (mix * zb[:, sl]).astype(o_ref.dtype)


def _nsa_prompt_call(ya, yb, kv16, kcmp, vcmp, bg_r, bias_c, bias_near, *, B, T):
    nq = T // Q_BLOCK
    gw = NSA_G * NSA_HD
    kv_spec = pl.BlockSpec((T, NSA_HD), lambda b, h, i: (b, h))
    cmp_spec = pl.BlockSpec((None, None, T // CMP_STRIDE, NSA_HD), lambda b, h, i: (b, h, 0, 0))
    return pl.pallas_call(
        functools.partial(_nsa_prompt_body, T=T),
        grid=(B, NSA_KVH, nq),
        in_specs=[pl.BlockSpec((Q_BLOCK, gw), lambda b, h, i: (b * nq + i, EVEN_A["qb"] // gw + h)),
                  pl.BlockSpec((Q_BLOCK, gw), lambda b, h, i: (b * nq + i, EVEN_B["zb"] // gw + h)),
                  pl.BlockSpec((Q_BLOCK, LANES), lambda b, h, i: (b * nq + i, EVEN_B["gb"] // LANES)),
                  pl.BlockSpec((1, LANES), lambda b, h, i: (0, 0)),
                  kv_spec, kv_spec, kv_spec, kv_spec, cmp_spec, cmp_spec,
                  pl.BlockSpec((None, NSA_G, Q_BLOCK, T // CMP_STRIDE), lambda b, h, i: (h, 0, i, 0)),
                  pl.BlockSpec((None, NSA_G, Q_BLOCK, NEAR_COLS), lambda b, h, i: (h, 0, 0, 0))],
        out_specs=pl.BlockSpec((Q_BLOCK, gw), lambda b, h, i: (b * nq + i, h)),
        out_shape=jax.ShapeDtypeStruct((B * T, NSA_W), BF16),
        scratch_shapes=[pltpu.VMEM((2 * T - Q_BLOCK, NSA_HD), BF16), pltpu.VMEM((2 * T - Q_BLOCK, NSA_HD), BF16),
                        pltpu.VMEM((WINDOW + T, NSA_HD), BF16), pltpu.VMEM((WINDOW + T, NSA_HD), BF16),
                        pltpu.VMEM((NSA_G * Q_BLOCK, NSA_HD), F32)],
        compiler_params=_params(("arbitrary", "arbitrary", "arbitrary")),
        name="nsa_prompt",
    )(ya, yb, yb, bg_r, *kv16, kcmp, vcmp, bias_c, bias_near)


CMP_PAGES = 32
CHUNKS_PER_PAGE = PAGE_SIZE // CMP_STRIDE
PAGE_ROWS = PAGE_SIZE * NSA_KVH


def _pool_rows(pool):
    return pool.reshape(pool.shape[0] * PAGE_ROWS, NSA_HD)


def _cmp_pages_body(pt_ref, *refs):
    del pt_ref
    pages = refs[:CMP_PAGES]
    w_ref, pe_ref, o_ref = refs[CMP_PAGES:]
    rows = CMP_PAGES * CHUNKS_PER_PAGE * 8
    even_pos = jnp.bitwise_and(_iota2((rows, 1), 0), NSA_KVH) == 0
    acc = jnp.zeros((rows, 2 * NSA_HD), F32)
    for j in range(CMP_STRIDE // 2):
        x = jnp.concatenate([pg[pl.ds(CMP_STRIDE * NSA_KVH * c + 8 * j, 8), :]
                             for pg in pages for c in range(CHUNKS_PER_PAGE)], axis=0).astype(BF16)
        r0 = _dot(x, w_ref[2 * j * NSA_HD:(2 * j + 1) * NSA_HD, :])
        r1 = _dot(x, w_ref[(2 * j + 1) * NSA_HD:(2 * j + 2) * NSA_HD, :])
        acc = acc + jnp.where(even_pos, r0, r1)
    pc = _dot(pe_ref[...], w_ref[...])
    res = acc + pltpu.roll(acc, rows - NSA_KVH, 0)
    o_ref[0] = res[:, :NSA_HD] + pc[0:1, :NSA_HD]
    o_ref[1] = res[:, NSA_HD:] + pc[1:2, NSA_HD:]


def _cmp_pages_call(pool, page_table, w1, pe, *, B):
    n_pages = page_table.shape[1]
    rows = CMP_PAGES * CHUNKS_PER_PAGE * 8
    view = _pool_rows(pool)
    w = w1.reshape(2, CMP_STRIDE, NSA_HD, NSA_HD).transpose(1, 2, 0, 3).reshape(CMP_STRIDE * NSA_HD, 2 * NSA_HD)
    pe_rows = jnp.pad(pe.reshape(2, CMP_STRIDE * NSA_HD), ((0, 6), (0, 0))).astype(BF16)

    def page_spec(i):
        return pl.BlockSpec((PAGE_ROWS, NSA_HD), lambda b, s, pt: (pt[b * n_pages + s * CMP_PAGES + i], 0))

    grid_spec = pltpu.PrefetchScalarGridSpec(
        num_scalar_prefetch=1,
        grid=(B, n_pages // CMP_PAGES),
        in_specs=[page_spec(i) for i in range(CMP_PAGES)]
        + [pl.BlockSpec((CMP_STRIDE * NSA_HD, 2 * NSA_HD), lambda b, s, pt: (0, 0)),
           pl.BlockSpec((8, CMP_STRIDE * NSA_HD), lambda b, s, pt: (0, 0))],
        out_specs=pl.BlockSpec((None, 2, rows, NSA_HD), lambda b, s, pt: (b, 0, s, 0)),
    )
    return pl.pallas_call(
        _cmp_pages_body,
        grid_spec=grid_spec,
        out_shape=jax.ShapeDtypeStruct((B, 2, n_pages * CHUNKS_PER_PAGE * 8, NSA_HD), F32),
        compiler_params=_params(("arbitrary", "arbitrary")),
        name="nsa_cmp_pages",
    )(page_table.reshape(-1), *([view] * CMP_PAGES), w.astype(BF16), pe_rows)


SEL_WINDOWS = 4
SLC_LANES = 384


def _sample_q_rows(q_ref):
    q = q_ref[...] * (NSA_HD ** -0.5)
    return jnp.concatenate([q[:, g * NSA_HD:(g + 1) * NSA_HD] for g in range(NSA_G)], axis=0).astype(BF16)


def _nsa_sample_main_body(abk_ref, abv_ref, b1_ref, w2_ref, q_ref, wk_ref, wv_ref, kn_ref, vn_ref, bc_ref, bw_ref,
                          ocmp_ref, owin_ref, idx_ref, *, T, n_slc):
    tp = SAMPLE_PAD_T
    rows = NSA_G * tp
    ncmp = abk_ref.shape[1] // 8

    def compressed(ab_ref, t):
        rows_h = pl.ds(pl.program_id(1), ncmp, stride=8)
        h = ab_ref[0, rows_h, :] + pltpu.roll(ab_ref[1, rows_h, :], ncmp - 1, 0) + b1_ref[t]
        return _dot(_gelu_tanh(h).astype(BF16), w2_ref[t]).astype(BF16)

    kc, vc = compressed(abk_ref, 0), compressed(abv_ref, 1)
    q = _sample_q_rows(q_ref)
    step = jnp.bitwise_and(_iota2((rows, 1), 0), tp - 1)
    tpos = PAST_LEN + step
    vis = tpos >= _iota2((1, ncmp), 1) * CMP_STRIDE + (CMP_BLOCK - 1)
    p, l = _softmax_rows(jnp.where(vis, _dot_nt(q, kc) + bc_ref[...], NEG_INF))
    p = p / jnp.maximum(l, TINY)
    ocmp_ref[...] = _dot(p.astype(BF16), vc)
    psum = p[0:tp]
    for g in range(1, NSA_G):
        psum = psum + p[g * tp:(g + 1) * tp]
    cur = jnp.right_shift(PAST_LEN + _iota2((tp, 1), 0), SEL_SHIFT)
    _, picks = _top_blocks(_slc_scores(psum, SLC_LANES, n_slc), cur, N_SEL)
    idx_ref[...] = picks.astype(jnp.int32)

    wb = wk_ref.shape[0] // NSA_KVH
    wlen = bw_ref.shape[1]
    fill = jnp.zeros((wlen - wb - tp, NSA_HD), BF16)
    head = pl.program_id(1)
    k_all = jnp.concatenate([wk_ref[pl.ds(head, wb, stride=NSA_KVH), :].astype(BF16), kn_ref[...], fill], axis=0)
    v_all = jnp.concatenate([wv_ref[pl.ds(head, wb, stride=NSA_KVH), :].astype(BF16), vn_ref[...], fill], axis=0)
    col = _iota2((1, wlen), 1)
    dist = tpos - (PAST_LEN - wb + col)
    in_win = (dist >= 0) & (dist < WINDOW) & (col < wb + T)
    pw, lw = _softmax_rows(jnp.where(in_win, _dot_nt(q, k_all) + bw_ref[...], NEG_INF))
    owin_ref[...] = _dot(pw.astype(BF16), v_all) / jnp.maximum(lw, TINY)


def _nsa_sample_main_call(ya, kw16, vw16, abk, abv, b1, w2, wk, wv, bias_c, bias_w, *, B, T):
    tp = SAMPLE_PAD_T
    rows = NSA_G * tp
    gw = NSA_G * NSA_HD
    ncmp = abk.shape[2] // 8
    wb = wk.shape[1]
    wlen = bias_w.shape[-1]
    n_slc = -(-(PAST_LEN + T) // SEL_BLOCK)
    assert n_slc <= SLC_LANES and T <= tp
    ab_spec = pl.BlockSpec((None, 2, 8 * ncmp, NSA_HD), lambda b, h: (b, 0, 0, 0))
    win_spec = pl.BlockSpec((wb * NSA_KVH, NSA_HD), lambda b, h: (b, 0))
    o_spec = pl.BlockSpec((None, None, rows, NSA_HD), lambda b, h: (b, h, 0, 0))
    return pl.pallas_call(
        functools.partial(_nsa_sample_main_body, T=T, n_slc=n_slc),
        grid=(B, NSA_KVH),
        in_specs=[ab_spec, ab_spec,
                  pl.BlockSpec((2, 1, NSA_HD), lambda b, h: (0, 0, 0)),
                  pl.BlockSpec((2, NSA_HD, NSA_HD), lambda b, h: (0, 0, 0)),
                  pl.BlockSpec((tp, gw), lambda b, h: (b, EVEN_A["qb"] // gw + h)),
                  win_spec, win_spec,
                  pl.BlockSpec((tp, NSA_HD), lambda b, h: (b, h)),
                  pl.BlockSpec((tp, NSA_HD), lambda b, h: (b, h)),
                  pl.BlockSpec((None, rows, ncmp), lambda b, h: (h, 0, 0)),
                  pl.BlockSpec((None, rows, wlen), lambda b, h: (h, 0, 0))],
        out_specs=[o_spec, o_spec, pl.BlockSpec((None, None, tp, LANES), lambda b, h: (b, h, 0, 0))],
        out_shape=[jax.ShapeDtypeStruct((B, NSA_KVH, rows, NSA_HD), F32),
                   jax.ShapeDtypeStruct((B, NSA_KVH, rows, NSA_HD), F32),
                   jax.ShapeDtypeStruct((B, NSA_KVH, tp, LANES), jnp.int32)],
        compiler_params=_params(("parallel", "parallel")),
        name="nsa_sample_main",
    )(abk, abv, b1.reshape(2, 1, NSA_HD), w2.astype(BF16), ya,
      wk.reshape(B * wb * NSA_KVH, NSA_HD), wv.reshape(B * wb * NSA_KVH, NSA_HD), kw16, vw16, bias_c, bias_w)


NEAR_BLOCKS = 3


def _nsa_sample_sel_body(idx_ref, pt_ref, q_ref, kn_ref, vn_ref, tbl_ref, ocmp_ref, owin_ref, gb_ref, bg_ref, zb_ref,
                         *refs, T):
    del pt_ref
    k_blocks = refs[:N_SEL]
    v_blocks = refs[N_SEL:2 * N_SEL]
    o_ref, osel = refs[2 * N_SEL:]
    tp = SAMPLE_PAD_T
    rows = NSA_G * tp
    b, h, t = pl.program_id(0), pl.program_id(1), pl.program_id(2)
    base = ((b * NSA_KVH + h) * T + t) * N_SEL
    first_new = PAST_LEN // SEL_BLOCK
    cur = jnp.right_shift(PAST_LEN + t, SEL_SHIFT)
    q = _sample_q_rows(q_ref)
    pad = jnp.zeros((SEL_BLOCK - tp, NSA_HD), BF16)
    k_new = jnp.concatenate([kn_ref[...], pad], axis=0)
    v_new = jnp.concatenate([vn_ref[...], pad], axis=0)
    lane = _iota2((1, LANES), 1)
    low = lane < SEL_BLOCK
    within = jnp.bitwise_and(lane, SEL_BLOCK - 1)
    ks, vs, bias, kpos = [], [], [], []
    for i in range(0, N_SEL, 2):
        pair_bias, pair_pos = [], []
        for j in (i, i + 1):
            blk = idx_ref[base + j]
            is_new = blk >= first_new
            ks.append(jnp.where(is_new, k_new, k_blocks[j][pl.ds(h, SEL_BLOCK, stride=NSA_KVH), :].astype(BF16)))
            vs.append(jnp.where(is_new, v_new, v_blocks[j][pl.ds(h, SEL_BLOCK, stride=NSA_KVH), :].astype(BF16)))
            pair_bias.append(tbl_ref[jnp.clip(blk - (first_new - NEAR_BLOCKS), 0, NEAR_BLOCKS)])
            pair_pos.append(jnp.where(blk <= cur, blk * SEL_BLOCK, PAST_LEN + SEL_BLOCK * LANES) + within)
        bias.append(jnp.where(low, pair_bias[0], pair_bias[1]))
        kpos.append(jnp.where(low, pair_pos[0], pair_pos[1]))
    k_all = jnp.concatenate(ks, axis=0)
    v_all = jnp.concatenate(vs, axis=0)
    step = jnp.bitwise_and(_iota2((rows, 1), 0), tp - 1)
    ok = jnp.concatenate(kpos, axis=1) <= PAST_LEN + step
    p, l = _softmax_rows(jnp.where(ok, _dot_nt(q, k_all) + jnp.concatenate(bias, axis=1), NEG_INF))
    o = _dot(p.astype(BF16), v_all) / jnp.maximum(l, TINY)

    @pl.when(t == 0)
    def _():
        osel[...] = jnp.zeros_like(osel)

    osel[...] = jnp.where(step == t, o, osel[...])

    @pl.when(t == T - 1)
    def _():
        gate = jax.nn.sigmoid(gb_ref[...] + bg_ref[...])
        zb = _silu(zb_ref[...])
        for g in range(NSA_G):
            r = slice(g * tp, (g + 1) * tp)
            head = h * NSA_G + g
            mix = (_lane_col(gate, head) * ocmp_ref[r, :] + _lane_col(gate, NSA_HEADS + head) * osel[r, :]
                   + _lane_col(gate, 2 * NSA_HEADS + head) * owin_ref[r, :])
            sl = slice(g * NSA_HD, (g + 1) * NSA_HD)
            o_ref[:, sl] = (mix * zb[:, sl]).astype(o_ref.dtype)


def _nsa_sample_sel_call(ya, yb, ks16, vs16, idx, page_table, pool_k, pool_v, tbl, o_cmp, o_win, bg_r, *, B, T):
    tp = SAMPLE_PAD_T
    rows = NSA_G * tp
    gw = NSA_G * NSA_HD
    n_pages = page_table.shape[1]
    halves = PAGE_SIZE // SEL_BLOCK
    idx_flat = idx[:, :, :T, :N_SEL].reshape(-1)
    view_k, view_v = _pool_rows(pool_k), _pool_rows(pool_v)

    def blk_spec(j):
        def index(b, h, t, idx_s, pt_s):
            blk = idx_s[((b * NSA_KVH + h) * T + t) * N_SEL + j]
            page = pt_s[b * n_pages + jnp.minimum(blk // halves, n_pages - 1)]
            return (page * halves + blk % halves, 0)
        return pl.BlockSpec((SEL_BLOCK * NSA_KVH, NSA_HD), index)

    o_spec = pl.BlockSpec((None, None, rows, NSA_HD), lambda b, h, t, *_: (b, h, 0, 0))
    grid_spec = pltpu.PrefetchScalarGridSpec(
        num_scalar_prefetch=2,
        grid=(B, NSA_KVH, T),
        in_specs=[pl.BlockSpec((tp, gw), lambda b, h, t, *_: (b, EVEN_A["qb"] // gw + h)),
                  pl.BlockSpec((tp, NSA_HD), lambda b, h, t, *_: (b, h)),
                  pl.BlockSpec((tp, NSA_HD), lambda b, h, t, *_: (b, h)),
                  pl.BlockSpec((None, NEAR_BLOCKS + 1, rows, LANES), lambda b, h, t, *_: (h, 0, 0, 0)),
                  o_spec, o_spec,
                  pl.BlockSpec((tp, LANES), lambda b, h, t, *_: (b, EVEN_B["gb"] // LANES)),
                  pl.BlockSpec((1, LANES), lambda b, h, t, *_: (0, 0)),
                  pl.BlockSpec((tp, gw), lambda b, h, t, *_: (b, EVEN_B["zb"] // gw + h))]
        + [blk_spec(j) for j in range(N_SEL)] * 2,
        out_specs=pl.BlockSpec((tp, gw), lambda b, h, t, *_: (b, h)),
        scratch_shapes=[pltpu.VMEM((rows, NSA_HD), F32)],
    )
    return pl.pallas_call(
        functools.partial(_nsa_sample_sel_body, T=T),
        grid_spec=grid_spec,
        out_shape=jax.ShapeDtypeStruct((B * tp, NSA_W), BF16),
        compiler_params=_params(("arbitrary", "arbitrary", "arbitrary")),
        name="nsa_sample_sel",
    )(idx_flat, page_table.reshape(-1), ya, ks16, vs16, tbl, o_cmp, o_win, yb, bg_r, yb,
      *([view_k] * N_SEL), *([view_v] * N_SEL))


def _sample_bias_tables(rel_bias, T, wb):
    tp = SAMPLE_PAD_T
    ncmp = PAST_LEN // CMP_STRIDE
    wlen = -(-(wb + tp) // LANES) * LANES
    first = PAST_LEN // SEL_BLOCK - NEAR_BLOCKS
    assert PAST_LEN - ((first + 1) * SEL_BLOCK - 1) >= REL_MAX_DIST
    lo, hi = -wlen, PAST_LEN + tp
    rev = _bias_line(rel_bias, lo, hi, descending=True)

    def rows(tbl):
        return tbl.reshape(NSA_KVH, NSA_G * tp, tbl.shape[-1])

    t_c = _toeplitz(rev, hi - 1 - (PAST_LEN - (CMP_BLOCK - 1)), tp, CMP_STRIDE * ncmp)[:, :, ::CMP_STRIDE]
    t_w = _toeplitz(rev, hi - 1 - wb, tp, wlen)
    far = jnp.broadcast_to(rev[:, hi - 1 - REL_MAX_DIST][:, None, None], (NSA_HEADS, tp, LANES))
    near = []
    for k in range(1, NEAR_BLOCKS + 1):
        half = _toeplitz(rev, hi - 1 - (PAST_LEN - (first + k) * SEL_BLOCK), tp, SEL_BLOCK)
        near.append(jnp.concatenate([half, half], axis=-1))
    t_s = jnp.stack([far] + near, axis=1).reshape(NSA_KVH, NSA_G, NEAR_BLOCKS + 1, tp, LANES)
    t_s = t_s.transpose(0, 2, 1, 3, 4).reshape(NSA_KVH, NEAR_BLOCKS + 1, NSA_G * tp, LANES)
    return rows(t_c), rows(t_w), t_s


def _tail_even(w):
    return _tail_relayout(w, EVEN_KV_OFF + 6 * NSA_KV_W, 3 * NSA_HEADS, NSA_W + MEM_W, EVEN_B_N)


def _tail_odd(w):
    return _tail_relayout(w, ODD_A_N, 2 * ML_HEADS, ML_V_W + MEM_W, ODD_B_N)


def _gate_bias_even(b_gate):
    return jnp.pad(b_gate, (0, LANES - 3 * NSA_HEADS)).reshape(1, LANES)


def _gate_bias_odd(b_if):
    return jnp.pad(b_if.reshape(2 * ML_HEADS), (0, LANES - 2 * ML_HEADS)).reshape(1, LANES)


def _rel_bucket(dist):
    n = np.maximum(dist, 0)
    exact = REL_BUCKETS // 2
    nf = np.maximum(n, 1).astype(np.float32)
    large = exact + (np.log(nf / exact) / math.log(REL_MAX_DIST / exact) * (REL_BUCKETS - exact)).astype(np.int32)
    return np.where(n < exact, n, np.minimum(large, REL_BUCKETS - 1))


def _bias_line(rel_bias, lo, hi, descending=False):
    dist = np.arange(hi - 1, lo - 1, -1) if descending else np.arange(lo, hi)
    buckets = _rel_bucket(dist)
    edges = np.flatnonzero(np.diff(buckets)) + 1
    starts = np.concatenate([[0], edges])
    ends = np.concatenate([edges, [hi - lo]])
    bias_t = rel_bias.T.astype(F32)
    runs = [jnp.broadcast_to(bias_t[:, int(buckets[s])][:, None], (NSA_HEADS, int(e - s))) for s, e in zip(starts, ends)]
    return jnp.concatenate(runs, axis=1)


def _skew_rows(v, rows, step, cols):
    n = v.shape[1]
    reps = -(-rows * (n + step) // n)
    return jnp.tile(v, (1, reps))[:, :rows * (n + step)].reshape(v.shape[0], rows, n + step)[:, :, :cols]


def _toeplitz(rev, start, rows, cols):
    seg = rev[:, start - (rows - 1):start + cols]
    return _skew_rows(jnp.roll(seg, -(rows - 1), axis=1), rows, -1, cols)


def _prompt_bias_tables(rel_bias, T):
    ncmp = T // CMP_STRIDE
    assert Q_BLOCK + 1 >= REL_MAX_DIST
    lo, hi = -(CMP_STRIDE * ncmp + CMP_BLOCK), T
    line = _bias_line(rel_bias, lo, hi)
    rev = _bias_line(rel_bias, lo, hi, descending=True)

    def split(tbl):
        return tbl.reshape((NSA_KVH, NSA_G) + tbl.shape[1:])

    back = CMP_STRIDE * (ncmp - 1)
    first = -(back + CMP_BLOCK - 1) - lo
    seg = line[:, first:first + T + back]
    t_c = _skew_rows(jnp.roll(seg, -back, axis=1), ncmp, -CMP_STRIDE, T).swapaxes(1, 2)
    far = rev[:, hi - 1 - REL_MAX_DIST]
    t_near = _toeplitz(rev, hi - 1 - REL_MAX_DIST, Q_BLOCK, NEAR_COLS) - far[:, None, None]
    return split(t_c), split(t_near)


def _nsa_sample(ya, yb, kv16, page_table, pk_cmp, pv_cmp, pk_sel, pv_sel, wk, wv, bg_r, w1, b1, w2, pe, rel_bias,
                *, B, T):
    assert (PAST_LEN + T) // CMP_STRIDE == PAST_LEN // CMP_STRIDE
    abk = _cmp_pages_call(pk_cmp, page_table, w1[0], pe[0], B=B)
    abv = _cmp_pages_call(pv_cmp, page_table, w1[1], pe[1], B=B)
    bias_c, bias_w, tbl = _sample_bias_tables(rel_bias, T, wk.shape[1])
    o_cmp, o_win, idx = _nsa_sample_main_call(ya, kv16[4], kv16[5], abk, abv, b1, w2, wk, wv, bias_c, bias_w, B=B, T=T)
    return _nsa_sample_sel_call(ya, yb, kv16[2], kv16[3], idx, page_table, pk_sel, pv_sel, tbl, o_cmp, o_win, bg_r,
                                B=B, T=T)


def _kv_project(x, wt):
    outs = [_matmul_heads(x, wt, first=EVEN_KV_OFF + j * NSA_KV_W, transposed=True) for j in range(6)]
    return [o[0] for o in outs], [o[1] for o in outs]


def _even_prompt(hp2d, npre, mk16, mv16, wt, wt_b, bg_r, w1, b1, w2, pe, lb, g_norm, w_out, rel_bias, *, B, T):
    ya, yb = _matmul_nt(npre, wt, tm=W_TILE_M, tn=W_TILE_N, rows=(0, EVEN_A_N)), _matmul_nt(npre, wt_b)
    kv32, kv16 = _kv_project(npre, wt)
    oa, s_new = _hgrn_call(ya, jnp.zeros((B, HG_HEADS, HG_DK, HG_DV), F32), lb, g_norm, B=B, T=T, L=CHUNK, valid=CHUNK)
    kcmp = _compress_call(kv16[0], w1[0], b1[0], w2[0], pe[0], B=B, T=T)
    vcmp = _compress_call(kv16[1], w1[1], b1[1], w2[1], pe[1], B=B, T=T)
    ob = _nsa_prompt_call(ya, yb, kv16[2:], kcmp, vcmp, bg_r, *_prompt_bias_tables(rel_bias, T), B=B, T=T)
    om = _mem_call(yb, EVEN_B["qm"], mk16, mv16, B=B, T=T)
    h_new = _outproj([oa, ob, om], w_out, hp2d)
    wb = min(WINDOW, T)
    rows = [r.reshape(B, T, NSA_KVH, NSA_HD) for r in kv32]
    return h_new, (rows[0], rows[1], rows[2], rows[3], rows[4][:, -wb:], rows[5][:, -wb:], s_new)


def _even_sample(hs2d, nsam, mk_s, mv_s, page_table, pk_cmp, pv_cmp, pk_sel, pv_sel, wk, wv, s0,
                 wt, wt_b, bg_r, w1, b1, w2, pe, lb, g_norm, w_out, rel_bias, *, B, T):
    tp = SAMPLE_PAD_T
    ya, yb = _matmul_nt(nsam, wt, tm=W_TILE_M, tn=W_TILE_N, rows=(0, EVEN_A_N)), _matmul_nt(nsam, wt_b)
    kv32, kv16 = _kv_project(nsam, wt)
    oa, s_new = _hgrn_call(ya, s0, lb, g_norm, B=B, T=tp, L=tp, valid=T)
    ob = _nsa_sample(ya, yb, kv16, page_table, pk_cmp, pv_cmp, pk_sel, pv_sel, wk, wv, bg_r, w1, b1, w2, pe, rel_bias,
                     B=B, T=T)
    om = _mem_call(yb, EVEN_B["qm"], mk_s.reshape(B * N_MEM, MEM_W), mv_s.reshape(B * N_MEM, MEM_W), B=B, T=tp)
    rows = [r.reshape(B, tp, NSA_KVH, NSA_HD)[:, :T] for r in kv32]
    wb = wk.shape[1]
    win_k = jnp.concatenate([wk, rows[4]], axis=1)[:, -wb:]
    win_v = jnp.concatenate([wv, rows[5]], axis=1)[:, -wb:]
    return _outproj([oa, ob, om], w_out, hs2d), (rows[0], rows[1], rows[2], rows[3], win_k, win_v, s_new)


def _odd_mix(h2d, hn, k2d, v2d, c0, n0, m0, wt, wt_b, bif_r, g_norm, w_out, *, B, T, L, valid):
    ya, yb = _matmul_nt(hn, wt, tm=W_TILE_M, tn=W_TILE_N, rows=(0, ODD_A_N)), _matmul_nt(hn, wt_b)
    h, c_new, n_new, m_new = _mlstm_call(ya, yb, c0, n0, m0, bif_r, g_norm, B=B, T=T, L=L, valid=valid)
    om = _mem_call(yb, ODD_B["qm"], k2d, v2d, B=B, T=T)
    return _outproj([h, om], w_out, h2d), (c_new, n_new, m_new)


def _stack(lst, i):
    return jnp.stack([t[i] for t in lst])


def kernel(x_prompt, x_sample, cache_mem_k, cache_mem_v, cache_cmp_k, cache_cmp_v, cache_sel_k, cache_sel_v,
           cache_win_k, cache_win_v, state_hgrn, state_mlstm_c, state_mlstm_n, state_mlstm_m, page_table,
           mem_prompt, norm_w, mem_norm_w, final_norm_w, rel_bias, w_mem_kv, w_in_even, b_nsa_gate,
           w_cmp1, b_cmp1, w_cmp2, pe_cmp, hgrn_lb_logits, hgrn_norm_w, w_out_even, w_in_odd, b_mlstm_if,
           mlstm_norm_w, w_out_odd):
    bp, tp = x_prompt.shape[:2]
    bs, ts = x_sample.shape[:2]
    tsp = SAMPLE_PAD_T
    lbs = jnp.cumsum(jax.nn.softmax(hgrn_lb_logits.astype(F32), axis=0), axis=0)
    hp = x_prompt.reshape(bp * tp, D_MODEL)
    hs = jnp.pad(x_sample, ((0, 0), (0, tsp - ts), (0, 0))).reshape(bs * tsp, D_MODEL)
    mem2d = mem_prompt.reshape(bp * N_MEM, D_MODEL)
    mem_new, even_p, even_s, odd_p, odd_s = [], [], [], [], []
    for l in range(DEPTH):
        npre = _rmsnorm_rows(hp, norm_w[l], BF16)
        nsam = _rmsnorm_rows(hs, norm_w[l], BF16)
        nmem = _rmsnorm_rows(mem2d, mem_norm_w[l], BF16)
        mk32, mk16 = _matmul_heads(nmem, w_mem_kv[l], first=0)
        mv32, mv16 = _matmul_heads(nmem, w_mem_kv[l], first=MEM_W)
        mem_new.append((mk32.reshape(bp, N_MEM, MEM_HEADS, MEM_HD), mv32.reshape(bp, N_MEM, MEM_HEADS, MEM_HD)))
        mk_s, mv_s = cache_mem_k[l], cache_mem_v[l]
        if l % 2 == 0:
            e = l // 2
            w_in = w_in_even[e].T
            w_b = _tail_even(w_in)
            w_out = w_out_even[e].astype(BF16)
            bg_r = _gate_bias_even(b_nsa_gate[e])
            cmpw = (w_cmp1[e].reshape(2, CMP_BLOCK, NSA_HD, NSA_HD), b_cmp1[e], w_cmp2[e], pe_cmp[e])
            hp, st_p = _even_prompt(hp, npre, mk16, mv16, w_in, w_b, bg_r, *cmpw, lbs[l], hgrn_norm_w[e], w_out,
                                    rel_bias, B=bp, T=tp)
            hs, st_s = _even_sample(hs, nsam, mk_s, mv_s, page_table, cache_cmp_k[e], cache_cmp_v[e], cache_sel_k[e],
                                    cache_sel_v[e], cache_win_k[e], cache_win_v[e], state_hgrn[e], w_in, w_b, bg_r,
                                    *cmpw, lbs[l], hgrn_norm_w[e], w_out, rel_bias, B=bs, T=ts)
            even_p.append(st_p)
            even_s.append(st_s)
        else:
            o = l // 2
            w_in = w_in_odd[o].T
            w_b = _tail_odd(w_in)
            w_out = w_out_odd[o].astype(BF16)
            bif_r = _gate_bias_odd(b_mlstm_if[o])
            hp, st_p = _odd_mix(hp, npre, mk16, mv16, jnp.zeros((bp, ML_HEADS, ML_DV, ML_DK), F32),
                                jnp.zeros((bp, ML_HEADS, ML_DK), F32), jnp.zeros((bp, ML_HEADS), F32),
                                w_in, w_b, bif_r, mlstm_norm_w[o], w_out, B=bp, T=tp, L=ML_CHUNK, valid=ML_CHUNK)
            hs, st_s = _odd_mix(hs, nsam, mk_s.reshape(bs * N_MEM, MEM_W), mv_s.reshape(bs * N_MEM, MEM_W),
                                state_mlstm_c[o], state_mlstm_n[o], state_mlstm_m[o],
                                w_in, w_b, bif_r, mlstm_norm_w[o], w_out, B=bs, T=tsp, L=tsp, valid=ts)
            odd_p.append(st_p)
            odd_s.append(st_s)
    y_prompt = _rmsnorm_rows(hp, final_norm_w, F32).reshape(bp, tp, D_MODEL)
    y_sample = _rmsnorm_rows(hs, final_norm_w, F32).reshape(bs, tsp, D_MODEL)[:, :ts]
    return (y_prompt, y_sample,
            _stack(mem_new, 0), _stack(mem_new, 1),
            _stack(even_p, 0), _stack(even_p, 1), _stack(even_p, 2), _stack(even_p, 3),
            _stack(even_p, 4), _stack(even_p, 5), _stack(even_p, 6),
            _stack(odd_p, 0), _stack(odd_p, 1), _stack(odd_p, 2),
            _stack(even_s, 0), _stack(even_s, 1), _stack(even_s, 2), _stack(even_s, 3),
            _stack(even_s, 4), _stack(even_s, 5), _stack(even_s, 6),
            _stack(odd_s, 0), _stack(odd_s, 1), _stack(odd_s, 2))
```
